```python
import numpy as np
import jax
import jax.numpy as jnp
from jax import lax

D_MODEL = 1024
BATCH = 4
SEQ = 4096
DEPTH = 4
DEC_BATCH = 32
DEC_SEQ = 4
PAST_LEN = 8192
PAGE_SIZE = 128

N_MIXERS = 4
N_HEADS = 16
HEAD_DIM = D_MODEL // N_HEADS
N_KV_HEADS = 4
GROUP = N_HEADS // N_KV_HEADS
Q_DIM = N_HEADS * HEAD_DIM
KV_COLS = N_KV_HEADS * HEAD_DIM
D_FF = -(-8 * D_MODEL // (3 * 256)) * 256
ROPE_THETA = 10000.0
LN_EPS = 1e-5
ALPHA = (2 * DEPTH) ** 0.25
BETA = (8 * DEPTH) ** -0.25
Q_BLOCK = 128
ATTN_SCALE = HEAD_DIM ** -0.5

IDX_HEADS = 8
IDX_DIM = 64
IDX_TOPK = 256
IDX_SCALE = (IDX_HEADS * IDX_DIM) ** -0.5

FORGET_BIAS_INIT = 3.0

CMP_LEN = 32
CMP_STRIDE = 16
CMP_HID = 2 * HEAD_DIM
SEL_BLOCK = 64
SEL_TOPN = 16
WINDOW = 512

MOBA_BLOCK = 256
MOBA_TOPK = 3
MOBA_Q_BLOCK = 16

kernel_name = 'hybrid_dsa_fox_nsa_moba_decode_step'


def split_cols(p, sizes):
    return jnp.split(p, [int(c) for c in np.cumsum(sizes)[:-1]], axis=-1)


def layer_norm(x, g, b):
    xf = x.astype(jnp.float32)
    mu = jnp.mean(xf, axis=-1, keepdims=True)
    var = jnp.mean(jnp.square(xf - mu), axis=-1, keepdims=True)
    return ((xf - mu) * lax.rsqrt(var + LN_EPS) * g + b).astype(x.dtype)


def rope(x, pos):
    half = x.shape[-1] // 2
    inv = ROPE_THETA ** (-jnp.arange(half, dtype=jnp.float32) / half)
    ang = pos.astype(jnp.float32)[:, None] * inv[None, :]
    cos = jnp.cos(ang)[:, None, :].astype(x.dtype)
    sin = jnp.sin(ang)[:, None, :].astype(x.dtype)
    x1, x2 = x[..., :half], x[..., half:]
    return jnp.concatenate([x1 * cos - x2 * sin, x1 * sin + x2 * cos], axis=-1)


def safe_softmax(s, mask):
    s = jnp.where(mask, s.astype(jnp.float32), -jnp.inf)
    m = jnp.max(s, axis=-1, keepdims=True)
    e = jnp.exp(s - jnp.where(jnp.isfinite(m), m, 0.0))
    return e / jnp.maximum(jnp.sum(e, axis=-1, keepdims=True), 1e-30)


def sweep(fn, blk, pos, *arrs):
    nb = pos.shape[0] // blk
    def split(a):
        return jnp.moveaxis(a.reshape((a.shape[0], nb, blk) + a.shape[2:]), 1, 0)
    out = lax.map(lambda xs: fn(*xs), (pos.reshape(nb, blk),) + tuple(split(a) for a in arrs))
    out = jnp.moveaxis(out, 0, 1)
    return out.reshape((out.shape[0], nb * blk) + out.shape[3:])


def take_rows(arr, pos, head=None):
    def one(a, p):
        return a[p] if head is None else a[p, :, head]
    return jax.vmap(one)(arr, pos)


def paged_rows(pool, new, page_table, pos, head=None):
    def one(new_b, pt_b, p):
        pp = jnp.clip(p, 0, PAST_LEN - 1)
        phys = pt_b[pp // PAGE_SIZE]
        off = pp % PAGE_SIZE
        q = jnp.clip(p - PAST_LEN, 0, new_b.shape[0] - 1)
        if head is None:
            old, cur = pool[phys, off], new_b[q]
        else:
            old, cur = pool[phys, off, :, head], new_b[q, :, head]
        past = (p < PAST_LEN).reshape(p.shape + (1,) * (old.ndim - p.ndim))
        return jnp.where(past, old, cur)
    return jax.vmap(one)(new, page_table, pos)


def gather_past(pool, page_table):
    return pool[page_table].reshape((page_table.shape[0], PAST_LEN) + pool.shape[2:])


def kv_heads(k, v, B, T):
    return jnp.stack([k.reshape(B, T, N_KV_HEADS, HEAD_DIM), v.reshape(B, T, N_KV_HEADS, HEAD_DIM)], axis=2)


def dsa_project(x, pos, w_in):
    B, T, _ = x.shape
    q, k, v, qi, ki, wi = split_cols(x @ w_in, [Q_DIM, KV_COLS, KV_COLS, IDX_HEADS * IDX_DIM, IDX_DIM, IDX_HEADS])
    q = rope(q.reshape(B, T, N_HEADS, HEAD_DIM), pos)
    kv = jnp.stack([rope(k.reshape(B, T, N_KV_HEADS, HEAD_DIM), pos), v.reshape(B, T, N_KV_HEADS, HEAD_DIM)], axis=2)
    qi = rope(qi.reshape(B, T, IDX_HEADS, IDX_DIM), pos)
    ki = rope(ki.reshape(B, T, 1, IDX_DIM), pos)[:, :, 0]
    return q, qi, wi * IDX_SCALE, kv, ki


def dsa_attend(q_pos, q, qi, wi, ki_full, fetch_kv):
    B, Tq = q.shape[:2]
    L = ki_full.shape[1]
    topk = min(IDX_TOPK, L // 4)
    rel = jax.nn.relu(jnp.einsum('bthd,bsd->bths', qi, ki_full).astype(jnp.float32))
    score = jnp.einsum('bths,bth->bts', rel, wi.astype(jnp.float32))
    causal = jnp.arange(L)[None, :] <= q_pos[:, None]
    score = jnp.where(causal[None], score, -jnp.inf)
    top_val, top_idx = lax.top_k(score, topk)
    kv = fetch_kv(top_idx, None)
    qg = q.reshape(B, Tq, N_KV_HEADS, GROUP, HEAD_DIM)
    s = jnp.einsum('btgjd,btkgd->btgjk', qg, kv[:, :, :, 0]) * ATTN_SCALE
    p = safe_softmax(s, jnp.isfinite(top_val)[:, :, None, None, :])
    o = jnp.einsum('btgjk,btkgd->btgjd', p.astype(q.dtype), kv[:, :, :, 1])
    return o.reshape(B, Tq, Q_DIM)


def dsa_prompt(x, w_in, w_out):
    pos = jnp.arange(x.shape[1], dtype=jnp.int32)
    q, qi, wi, kv, ki = dsa_project(x, pos, w_in)
    fetch = lambda p, h=None: take_rows(kv, p, h)
    o = sweep(lambda qp, qb, qib, wib: dsa_attend(qp, qb, qib, wib, ki, fetch), Q_BLOCK, pos, q, qi, wi)
    return o @ w_out, kv, ki


def dsa_sample(x, cache_kv, cache_kidx, page_table, w_in, w_out):
    pos = PAST_LEN + jnp.arange(x.shape[1], dtype=jnp.int32)
    q, qi, wi, kv, ki = dsa_project(x, pos, w_in)
    ki_full = jnp.concatenate([gather_past(cache_kidx, page_table), ki], axis=1)
    fetch = lambda p, h=None: paged_rows(cache_kv, kv, page_table, p, h)
    o = dsa_attend(pos, q, qi, wi, ki_full, fetch)
    return o @ w_out, kv, ki


def fox_project(x, w_in, b_f):
    B, T, _ = x.shape
    q, k, v, f = split_cols(x @ w_in, [Q_DIM, KV_COLS, KV_COLS, N_HEADS])
    logf = jax.nn.log_sigmoid((f + b_f).astype(jnp.float32)).astype(x.dtype)
    return q.reshape(B, T, N_HEADS, HEAD_DIM), kv_heads(k, v, B, T), logf


def fox_attend(q_pos, q, c_q, kv, c):
    B, Tq = q.shape[:2]
    L = kv.shape[1]
    qg = q.reshape(B, Tq, N_KV_HEADS, GROUP, HEAD_DIM)
    s = jnp.einsum('btgjd,bsgd->btgjs', qg, kv[:, :, 0]).astype(jnp.float32) * ATTN_SCALE
    decay = c_q.reshape(B, Tq, N_KV_HEADS, GROUP, 1) - jnp.moveaxis(c.reshape(B, L, N_KV_HEADS, GROUP), 1, -1)[:, None]
    causal = jnp.arange(L)[None, :] <= q_pos[:, None]
    p = safe_softmax(s + decay, causal[None, :, None, None, :])
    o = jnp.einsum('btgjs,bsgd->btgjd', p.astype(q.dtype), kv[:, :, 1])
    return o.reshape(B, Tq, Q_DIM)


def fox_prompt(x, w_in, b_f, w_out):
    pos = jnp.arange(x.shape[1], dtype=jnp.int32)
    q, kv, logf = fox_project(x, w_in, b_f)
    c = jnp.cumsum(logf.astype(jnp.float32), axis=1)
    o = sweep(lambda qp, qb, cb: fox_attend(qp, qb, cb, kv, c), Q_BLOCK, pos, q, c)
    return o @ w_out, kv, logf


def fox_sample(x, cache_kv, cache_logf, page_table, w_in, b_f, w_out):
    pos = PAST_LEN + jnp.arange(x.shape[1], dtype=jnp.int32)
    q, kv, logf = fox_project(x, w_in, b_f)
    logf_all = jnp.concatenate([gather_past(cache_logf, page_table), logf], axis=1)
    c = jnp.cumsum(logf_all.astype(jnp.float32), axis=1)
    kv_all = jnp.concatenate([gather_past(cache_kv, page_table), kv], axis=1)
    o = fox_attend(pos, q, c[:, PAST_LEN:], kv_all, c)
    return o @ w_out, kv, logf


def nsa_project(x, pos, w_in, b_gate):
    B, T, _ = x.shape
    q, kc, vc, ks, vs, kw, vw, g = split_cols(x @ w_in, [Q_DIM] + [KV_COLS] * 6 + [3 * N_HEADS])
    q = q.reshape(B, T, N_HEADS, HEAD_DIM)
    kv_cmp = kv_heads(kc, vc, B, T)
    kv_slc = jnp.stack([rope(ks.reshape(B, T, N_KV_HEADS, HEAD_DIM), pos), vs.reshape(B, T, N_KV_HEADS, HEAD_DIM)], axis=2)
    kv_win = jnp.stack([rope(kw.reshape(B, T, N_KV_HEADS, HEAD_DIM), pos), vw.reshape(B, T, N_KV_HEADS, HEAD_DIM)], axis=2)
    gate = jax.nn.sigmoid((g + b_gate).astype(jnp.float32)).reshape(B, T, 3, N_HEADS)
    return q, rope(q, pos), gate, kv_cmp, kv_slc, kv_win


def nsa_compress(kv, pe, w1, w2):
    B, L = kv.shape[:2]
    r = CMP_LEN // CMP_STRIDE
    nc = (L - CMP_LEN) // CMP_STRIDE + 1
    n_chunk = nc + r - 1
    chunks = kv[:, :n_chunk * CMP_STRIDE].reshape(B, n_chunk, CMP_STRIDE, 2, N_KV_HEADS, HEAD_DIM)
    part = jnp.einsum('bnlckd,crldh->bnrckh', chunks, w1.reshape(2, r, CMP_STRIDE, HEAD_DIM, CMP_HID))
    h = jnp.einsum('cld,cldh->ch', pe, w1)[:, None, :]
    for j in range(r):
        h = h + part[:, j:j + nc, j]
    return jnp.einsum('bnckh,chd->bnckd', jax.nn.gelu(h), w2)


def cmp_sel_cover(nc, ns):
    c0 = np.arange(nc)[:, None] * CMP_STRIDE
    s0 = np.arange(ns)[None, :] * SEL_BLOCK
    cover = (c0 <= s0 + SEL_BLOCK - 1) & (c0 + CMP_LEN - 1 >= s0)
    return jnp.asarray(cover, dtype=jnp.float32)


def nsa_attend(q_pos, q, q_rot, gate, cmp, n_keys, fetch_slc, win, win_pos):
    B, Tq = q.shape[:2]
    nc = cmp.shape[1]
    ns = -(-n_keys // SEL_BLOCK)
    n_sel = min(SEL_TOPN, ns)
    qg = q.reshape(B, Tq, N_KV_HEADS, GROUP, HEAD_DIM)
    qr = q_rot.reshape(B, Tq, N_KV_HEADS, GROUP, HEAD_DIM)
    s_c = jnp.einsum('btgjd,bngd->btgjn', qg, cmp[:, :, 0]) * ATTN_SCALE
    c_ok = (jnp.arange(nc) * CMP_STRIDE + CMP_LEN - 1)[None, :] <= q_pos[:, None]
    p_c = safe_softmax(s_c, c_ok[None, :, None, None, :])
    o_c = jnp.einsum('btgjn,bngd->btgjd', p_c.astype(q.dtype), cmp[:, :, 1])
    imp = jnp.einsum('btgjn,ns->btgs', p_c, cmp_sel_cover(nc, ns))
    blk = jnp.arange(ns)[None, :]
    cur = (q_pos // SEL_BLOCK)[:, None]
    forced = (blk == 0) | (blk == cur) | (blk == cur - 1)
    imp = jnp.where(forced[None, :, None, :], jnp.inf, imp)
    imp = jnp.where((blk <= cur)[None, :, None, :], imp, -jnp.inf)
    top_val, top_blk = lax.top_k(imp, n_sel)
    k_pos = top_blk[..., None] * SEL_BLOCK + jnp.arange(SEL_BLOCK)
    kv_s = fetch_slc(k_pos, jnp.arange(N_KV_HEADS)[:, None, None])
    kv_s = kv_s.reshape(B, Tq, N_KV_HEADS, n_sel * SEL_BLOCK, 2, HEAD_DIM)
    s_ok = ((top_val > -jnp.inf)[..., None] & (k_pos <= q_pos[None, :, None, None, None]))
    s_ok = s_ok.reshape(B, Tq, N_KV_HEADS, 1, n_sel * SEL_BLOCK)
    s_s = jnp.einsum('btgjd,btgmd->btgjm', qr, kv_s[..., 0, :]) * ATTN_SCALE
    p_s = safe_softmax(s_s, s_ok)
    o_s = jnp.einsum('btgjm,btgmd->btgjd', p_s.astype(q.dtype), kv_s[..., 1, :])
    w_ok = (win_pos[None, :] <= q_pos[:, None]) & (win_pos[None, :] >= q_pos[:, None] - WINDOW) & (win_pos[None, :] >= 0)
    s_w = jnp.einsum('btgjd,bsgd->btgjs', qr, win[:, :, 0]) * ATTN_SCALE
    p_w = safe_softmax(s_w, w_ok[None, :, None, None, :])
    o_w = jnp.einsum('btgjs,bsgd->btgjd', p_w.astype(q.dtype), win[:, :, 1])
    g = gate.reshape(B, Tq, 3, N_KV_HEADS, GROUP, 1)
    o = g[:, :, 0] * o_c + g[:, :, 1] * o_s + g[:, :, 2] * o_w
    return o.astype(q.dtype).reshape(B, Tq, Q_DIM)


def nsa_prompt(x, w_in, b_gate, cmp_pe, cmp_w1, cmp_w2, w_out):
    T = x.shape[1]
    pos = jnp.arange(T, dtype=jnp.int32)
    q, q_rot, gate, kv_cmp, kv_slc, kv_win = nsa_project(x, pos, w_in, b_gate)
    cmp = nsa_compress(kv_cmp, cmp_pe, cmp_w1, cmp_w2)
    win_pad = jnp.pad(kv_win, ((0, 0), (WINDOW, 0), (0, 0), (0, 0), (0, 0)))
    fetch = lambda p, h=None: take_rows(kv_slc, p, h)
    def block(qp, qb, qrb, gb):
        start = qp[0]
        width = WINDOW + qp.shape[0]
        win = lax.dynamic_slice_in_dim(win_pad, start, width, axis=1)
        win_pos = start - WINDOW + jnp.arange(width, dtype=jnp.int32)
        return nsa_attend(qp, qb, qrb, gb, cmp, T, fetch, win, win_pos)
    o = sweep(block, Q_BLOCK, pos, q, q_rot, gate)
    return o @ w_out, kv_cmp, kv_slc, kv_win[:, -min(WINDOW, T):]


def nsa_sample(x, cache_cmp, cache_slc, state_win, page_table, w_in, b_gate, cmp_pe, cmp_w1, cmp_w2, w_out):
    T = x.shape[1]
    pos = PAST_LEN + jnp.arange(T, dtype=jnp.int32)
    q, q_rot, gate, kv_cmp, kv_slc, kv_win = nsa_project(x, pos, w_in, b_gate)
    cmp = nsa_compress(jnp.concatenate([gather_past(cache_cmp, page_table), kv_cmp], axis=1), cmp_pe, cmp_w1, cmp_w2)
    fetch = lambda p, h=None: paged_rows(cache_slc, kv_slc, page_table, p, h)
    win_buf = state_win.shape[1]
    win = jnp.concatenate([state_win, kv_win], axis=1)
    win_pos = PAST_LEN - win_buf + jnp.arange(win_buf + T, dtype=jnp.int32)
    o = nsa_attend(pos, q, q_rot, gate, cmp, PAST_LEN + T, fetch, win, win_pos)
    return o @ w_out, kv_cmp, kv_slc, win[:, -win_buf:]


def moba_project(x, pos, w_in):
    B, T, _ = x.shape
    q, k, v = split_cols(x @ w_in, [Q_DIM, KV_COLS, KV_COLS])
    q = rope(q.reshape(B, T, N_HEADS, HEAD_DIM), pos)
    kv = jnp.stack([rope(k.reshape(B, T, N_KV_HEADS, HEAD_DIM), pos), v.reshape(B, T, N_KV_HEADS, HEAD_DIM)], axis=2)
    return q, kv


def block_means(k):
    B, L = k.shape[:2]
    nb = -(-L // MOBA_BLOCK)
    k = jnp.pad(k.astype(jnp.float32), ((0, 0), (0, nb * MOBA_BLOCK - L), (0, 0), (0, 0)))
    return jnp.mean(k.reshape(B, nb, MOBA_BLOCK, N_KV_HEADS, HEAD_DIM), axis=2)


def moba_attend(q_pos, q, k_mean, fetch_kv):
    B, Tq = q.shape[:2]
    nb = k_mean.shape[1]
    k_top = min(MOBA_TOPK, nb)
    qg = q.reshape(B, Tq, N_KV_HEADS, GROUP, HEAD_DIM)
    n_past = q_pos // MOBA_BLOCK
    score = jnp.einsum('btgjd,bngd->btgjn', qg.astype(jnp.float32), k_mean)
    score = jnp.where((jnp.arange(nb)[None, :] < n_past[:, None])[None, :, None, None, :], score, -jnp.inf)
    top_val, top_blk = lax.top_k(score, k_top)
    p_pos = top_blk[..., None] * MOBA_BLOCK + jnp.arange(MOBA_BLOCK)
    kv_p = fetch_kv(p_pos, jnp.arange(N_KV_HEADS)[:, None, None, None])
    kv_p = kv_p.reshape(B, Tq, N_KV_HEADS, GROUP, k_top * MOBA_BLOCK, 2, HEAD_DIM)
    o_pos = (n_past * MOBA_BLOCK)[:, None] + jnp.arange(MOBA_BLOCK)
    kv_o = fetch_kv(jnp.broadcast_to(o_pos[None], (B, Tq, MOBA_BLOCK)), None)
    s_p = jnp.einsum('btgjd,btgjmd->btgjm', qg, kv_p[..., 0, :])
    s_o = jnp.einsum('btgjd,btsgd->btgjs', qg, kv_o[:, :, :, 0])
    ok_p = jnp.broadcast_to((top_val > -jnp.inf)[..., None], top_val.shape + (MOBA_BLOCK,)).reshape(s_p.shape)
    ok_o = jnp.broadcast_to((o_pos <= q_pos[:, None])[None, :, None, None, :], s_o.shape)
    p = safe_softmax(jnp.concatenate([s_p, s_o], axis=-1) * ATTN_SCALE, jnp.concatenate([ok_p, ok_o], axis=-1))
    p = p.astype(q.dtype)
    m = k_top * MOBA_BLOCK
    o = (jnp.einsum('btgjm,btgjmd->btgjd', p[..., :m], kv_p[..., 1, :])
         + jnp.einsum('btgjs,btsgd->btgjd', p[..., m:], kv_o[:, :, :, 1]))
    return o.reshape(B, Tq, Q_DIM)


def moba_prompt(x, w_in, w_out):
    pos = jnp.arange(x.shape[1], dtype=jnp.int32)
    q, kv = moba_project(x, pos, w_in)
    k_mean = block_means(kv[:, :, 0])
    fetch = lambda p, h=None: take_rows(kv, p, h)
    o = sweep(lambda qp, qb: moba_attend(qp, qb, k_mean, fetch), MOBA_Q_BLOCK, pos, q)
    return o @ w_out, kv


def moba_sample(x, cache_kv, page_table, w_in, w_out):
    B, T, _ = x.shape
    pos = PAST_LEN + jnp.arange(T, dtype=jnp.int32)
    q, kv = moba_project(x, pos, w_in)
    k_past = cache_kv[page_table, :, 0].reshape(B, PAST_LEN, N_KV_HEADS, HEAD_DIM)
    k_mean = block_means(jnp.concatenate([k_past, kv[:, :, 0]], axis=1))
    fetch = lambda p, h=None: paged_rows(cache_kv, kv, page_table, p, h)
    o = moba_attend(pos, q, k_mean, fetch)
    return o @ w_out, kv


def swiglu(x, w_gu, w_down):
    g, u = jnp.split(x @ w_gu, 2, axis=-1)
    return (jax.nn.silu(g) * u) @ w_down


def setup_inputs(seed: int = 0) -> dict:
    key = jax.random.key(seed)
    def nrm(i, shape, scale=1.0):
        return jax.random.normal(jax.random.fold_in(key, i), shape, jnp.float32) * scale
    n_pages = PAST_LEN // PAGE_SIZE
    n_pool = (5 * DEC_BATCH * n_pages + 3) // 4
    win_buf = min(WINDOW, PAST_LEN)
    pool_kv = (n_pool, PAGE_SIZE, 2, N_KV_HEADS, HEAD_DIM)
    perm = jax.random.permutation(jax.random.fold_in(key, 999), n_pool)
    page_table = perm[:DEC_BATCH * n_pages].reshape(DEC_BATCH, n_pages).astype(jnp.int32)
    in_s = D_MODEL ** -0.5
    out_s = BETA * Q_DIM ** -0.5
    return {
        'x_prompt': nrm(0, (BATCH, SEQ, D_MODEL)),
        'x_sample': nrm(1, (DEC_BATCH, DEC_SEQ, D_MODEL)),
        'cache_a_kv': nrm(2, pool_kv),
        'cache_a_kidx': nrm(3, (n_pool, PAGE_SIZE, IDX_DIM)),
        'cache_b_kv': nrm(4, pool_kv),
        'cache_b_logf': jax.nn.log_sigmoid(FORGET_BIAS_INIT + nrm(5, (n_pool, PAGE_SIZE, N_HEADS))),
        'cache_c_cmp_kv': nrm(6, pool_kv),
        'cache_c_slc_kv': nrm(7, pool_kv),
        'state_c_win_kv': nrm(8, (DEC_BATCH, win_buf, 2, N_KV_HEADS, HEAD_DIM)),
        'cache_d_kv': nrm(9, pool_kv),
        'page_table': page_table,
        'a_w_in': nrm(10, (D_MODEL, Q_DIM + 2 * KV_COLS + IDX_HEADS * IDX_DIM + IDX_DIM + IDX_HEADS), in_s),
        'a_w_out': nrm(11, (Q_DIM, D_MODEL), out_s),
        'b_w_in': nrm(12, (D_MODEL, Q_DIM + 2 * KV_COLS + N_HEADS), in_s),
        'b_b_f': FORGET_BIAS_INIT + nrm(13, (N_HEADS,), 0.1),
        'b_w_out': nrm(14, (Q_DIM, D_MODEL), out_s),
        'c_w_in': nrm(15, (D_MODEL, Q_DIM + 6 * KV_COLS + 3 * N_HEADS), in_s),
        'c_b_gate': nrm(16, (3 * N_HEADS,), 0.01),
        'c_cmp_pe': nrm(17, (2, CMP_LEN, HEAD_DIM), 0.1),
        'c_cmp_w1': nrm(18, (2, CMP_LEN, HEAD_DIM, CMP_HID), (CMP_LEN * HEAD_DIM) ** -0.5),
        'c_cmp_w2': nrm(19, (2, CMP_HID, HEAD_DIM), CMP_HID ** -0.5),
        'c_w_out': nrm(20, (Q_DIM, D_MODEL), out_s),
        'd_w_in': nrm(21, (D_MODEL, Q_DIM + 2 * KV_COLS), in_s),
        'd_w_out': nrm(22, (Q_DIM, D_MODEL), out_s),
        'ln_g': 1.0 + nrm(23, (DEPTH, 2, D_MODEL), 0.02),
        'ln_b': nrm(24, (DEPTH, 2, D_MODEL), 0.02),
        'ffn_w_gu': nrm(25, (DEPTH, D_MODEL, 2 * D_FF), in_s),
        'ffn_w_down': nrm(26, (DEPTH, D_FF, D_MODEL), BETA * D_FF ** -0.5),
    }


def reference(x_prompt, x_sample, cache_a_kv, cache_a_kidx, cache_b_kv, cache_b_logf, cache_c_cmp_kv,
              cache_c_slc_kv, state_c_win_kv, cache_d_kv, page_table, a_w_in, a_w_out, b_w_in, b_b_f, b_w_out,
              c_w_in, c_b_gate, c_cmp_pe, c_cmp_w1, c_cmp_w2, c_w_out, d_w_in, d_w_out, ln_g, ln_b,
              ffn_w_gu, ffn_w_down):
    xp, xs = x_prompt, x_sample
    for i in range(DEPTH):
        kind = i % N_MIXERS
        if kind == 0:
            mp, a_kv_p, a_kidx_p = dsa_prompt(xp, a_w_in, a_w_out)
            ms, a_kv_s, a_kidx_s = dsa_sample(xs, cache_a_kv, cache_a_kidx, page_table, a_w_in, a_w_out)
        elif kind == 1:
            mp, b_kv_p, b_logf_p = fox_prompt(xp, b_w_in, b_b_f, b_w_out)
            ms, b_kv_s, b_logf_s = fox_sample(xs, cache_b_kv, cache_b_logf, page_table, b_w_in, b_b_f, b_w_out)
        elif kind == 2:
            mp, c_cmp_kv_p, c_slc_kv_p, c_win_kv_p = nsa_prompt(xp, c_w_in, c_b_gate, c_cmp_pe, c_cmp_w1, c_cmp_w2, c_w_out)
            ms, c_cmp_kv_s, c_slc_kv_s, c_win_kv_s = nsa_sample(xs, cache_c_cmp_kv, cache_c_slc_kv, state_c_win_kv,
                                                                page_table, c_w_in, c_b_gate, c_cmp_pe, c_cmp_w1,
                                                                c_cmp_w2, c_w_out)
        else:
            mp, d_kv_p = moba_prompt(xp, d_w_in, d_w_out)
            ms, d_kv_s = moba_sample(xs, cache_d_kv, page_table, d_w_in, d_w_out)
        xp = layer_norm(ALPHA * xp + mp, ln_g[i, 0], ln_b[i, 0])
        xs = layer_norm(ALPHA * xs + ms, ln_g[i, 0], ln_b[i, 0])
        xp = layer_norm(ALPHA * xp + swiglu(xp, ffn_w_gu[i], ffn_w_down[i]), ln_g[i, 1], ln_b[i, 1])
        xs = layer_norm(ALPHA * xs + swiglu(xs, ffn_w_gu[i], ffn_w_down[i]), ln_g[i, 1], ln_b[i, 1])
    return (xp, xs, a_kv_p, a_kv_s, a_kidx_p, a_kidx_s, b_kv_p, b_kv_s, b_logf_p, b_logf_s,
            c_cmp_kv_p, c_cmp_kv_s, c_slc_kv_p, c_slc_kv_s, c_win_kv_p, c_win_kv_s, d_kv_p, d_kv_s)
```

```python
import functools

import numpy as np
import jax
import jax.numpy as jnp
from jax import lax
from jax.experimental import pallas as pl
from jax.experimental.pallas import tpu as pltpu

F32 = jnp.float32
BF16 = jnp.bfloat16
I32 = jnp.int32

N_HEADS = 16
HEAD_DIM = 64
N_KV_HEADS = 4
GROUP = N_HEADS // N_KV_HEADS
Q_DIM = N_HEADS * HEAD_DIM
KV_COLS = N_KV_HEADS * HEAD_DIM
DEPTH = 4
PAGE_SIZE = 128
ROPE_THETA = 10000.0
LN_EPS = 1e-5
ALPHA = (2 * DEPTH) ** 0.25
ATTN_SCALE = HEAD_DIM ** -0.5
IDX_HEADS = 8
IDX_DIM = 64
IDX_TOPK = 256
IDX_SCALE = (IDX_HEADS * IDX_DIM) ** -0.5
CMP_LEN = 32
CMP_STRIDE = 16
CMP_HID = 2 * HEAD_DIM
SEL_BLOCK = 64
SEL_TOPN = 16
WINDOW = 512
MOBA_BLOCK = 256
MOBA_TOPK = 3

LANES = 128
VMEM_LIMIT = 56 * 2 ** 20
NEG = -1e30
KEY_NEG_INF = -2139095041
KEY_POS_INF = 2139095040
INT_MIN = -2 ** 31


def _params(*sem):
    return pltpu.CompilerParams(dimension_semantics=sem, vmem_limit_bytes=VMEM_LIMIT)


def _dot_t(a, b):
    return lax.dot_general(a, b, (((1,), (1,)), ((), ())), preferred_element_type=F32)


def _dot(a, b):
    return jnp.dot(a, b, preferred_element_type=F32)


def _dot_hp(a, b):
    hi = a.astype(BF16)
    r1 = a - hi.astype(F32)
    mid = r1.astype(BF16)
    lo = (r1 - mid.astype(F32)).astype(BF16)
    return _dot(hi, b) + _dot(mid, b) + _dot(lo, b)


def _mm_kernel(x_ref, w_ref, o_ref):
    o_ref[...] = _dot(x_ref[...].astype(BF16), w_ref[...])


def _matmul(x, w):
    m, k = x.shape
    n = w.shape[1]
    tm = min(m, 512)
    return pl.pallas_call(
        _mm_kernel,
        grid=(m // tm,),
        in_specs=[pl.BlockSpec((tm, k), lambda i: (i, 0)), pl.BlockSpec((k, n), lambda i: (0, 0))],
        out_specs=pl.BlockSpec((tm, n), lambda i: (i, 0)),
        out_shape=jax.ShapeDtypeStruct((m, n), F32),
        compiler_params=_params("parallel"),
    )(x, w)


def _layer_norm(y, g, b):
    mu = jnp.mean(y, axis=-1, keepdims=True)
    d = y - mu
    var = jnp.mean(d * d, axis=-1, keepdims=True)
    return d * lax.rsqrt(var + LN_EPS) * g + b


def _out_ln_kernel(o_ref, w_ref, x_ref, g_ref, b_ref, y_ref):
    y = ALPHA * x_ref[...] + _dot(o_ref[...].astype(BF16), w_ref[...])
    y_ref[...] = _layer_norm(y, g_ref[...], b_ref[...])


def _out_ln(o, w, x, g, b):
    m, d = x.shape
    k = o.shape[1]
    tm = min(m, 512)
    row = lambda i: (i, 0)
    fix = lambda i: (0, 0)
    return pl.pallas_call(
        _out_ln_kernel,
        grid=(m // tm,),
        in_specs=[pl.BlockSpec((tm, k), row), pl.BlockSpec((k, d), fix), pl.BlockSpec((tm, d), row),
                  pl.BlockSpec((1, d), fix), pl.BlockSpec((1, d), fix)],
        out_specs=pl.BlockSpec((tm, d), row),
        out_shape=jax.ShapeDtypeStruct((m, d), F32),
        compiler_params=_params("parallel"),
    )(o, w, x, g.reshape(1, d), b.reshape(1, d))


def _ffn_ln_kernel(x_ref, wgu_ref, wd_ref, g_ref, b_ref, y_ref, *, d_ff, chunk):
    x = x_ref[...]
    xb = x.astype(BF16)
    acc = jnp.zeros(x.shape, F32)
    for c in range(d_ff // chunk):
        gate = _dot(xb, wgu_ref[:, c * chunk:(c + 1) * chunk])
        up = _dot(xb, wgu_ref[:, d_ff + c * chunk:d_ff + (c + 1) * chunk])
        h = gate * (1.0 / (1.0 + jnp.exp(-gate))) * up
        acc = acc + _dot(h.astype(BF16), wd_ref[c * chunk:(c + 1) * chunk, :])
    y_ref[...] = _layer_norm(ALPHA * x + acc, g_ref[...], b_ref[...])


def _ffn_ln(x, wgu, wd, g, b):
    m, d = x.shape
    d_ff = wd.shape[0]
    tm = min(m, 256)
    row = lambda i: (i, 0)
    fix = lambda i: (0, 0)
    return pl.pallas_call(
        functools.partial(_ffn_ln_kernel, d_ff=d_ff, chunk=256),
        grid=(m // tm,),
        in_specs=[pl.BlockSpec((tm, d), row), pl.BlockSpec((d, 2 * d_ff), fix), pl.BlockSpec((d_ff, d), fix),
                  pl.BlockSpec((1, d), fix), pl.BlockSpec((1, d), fix)],
        out_specs=pl.BlockSpec((tm, d), row),
        out_shape=jax.ShapeDtypeStruct((m, d), F32),
        compiler_params=_params("parallel"),
    )(x, wgu, wd, g.reshape(1, d), b.reshape(1, d))


def _sortable(x):
    x = jnp.where(x == 0.0, 0.0, x)
    b = lax.bitcast_convert_type(x, I32)
    return b ^ ((b >> 31) & I32(0x7FFFFFFF))


def _kth_largest_key(u_ref, k):
    rows = u_ref.shape[0]

    def count_ge(cand):
        return jnp.sum((u_ref[...] >= cand).astype(I32), axis=1, keepdims=True)

    base = jnp.where(count_ge(jnp.zeros((rows, 1), I32)) >= k, I32(0), I32(INT_MIN))

    def body(i, base):
        cand = base | jnp.left_shift(I32(1), 30 - i)
        return jnp.where(count_ge(cand) >= k, cand, base)

    return lax.fori_loop(0, 31, body, base)


def _emit_selection(u_ref, k, write):
    rows, n = u_ref.shape
    thr = _kth_largest_key(u_ref, k)
    n_gt = jnp.sum((u_ref[...] > thr).astype(I32), axis=1, keepdims=True)
    need = (k - n_gt).astype(F32)
    r_i = lax.broadcasted_iota(I32, (LANES, LANES), 0)
    c_i = lax.broadcasted_iota(I32, (LANES, LANES), 1)
    tri = jnp.where(r_i <= c_i, 1.0, 0.0).astype(BF16)
    carry = jnp.zeros((rows, 1), F32)
    for ci in range(n // LANES):
        u = u_ref[:, ci * LANES:(ci + 1) * LANES]
        tie = u == thr
        tie_f = jnp.where(tie, 1.0, 0.0)
        inc = _dot(tie_f.astype(BF16), tri)
        rank = carry + inc - tie_f
        sel = (u > thr) | (tie & (rank < need))
        write(ci, sel, u)
        carry = carry + inc[:, LANES - 1:LANES]


def _flash_prompt_kernel(*refs, kind, tq, tk, n_extra):
    q_ref, k_ref, v_ref = refs[:3]
    extra = refs[3:3 + n_extra]
    o_ref, m_ref, l_ref, acc_ref = refs[3 + n_extra:]
    g = pl.program_id(1)
    q0 = pl.program_id(2) * tq
    rows = GROUP * tq
    q = q_ref[0, 0].reshape(rows, HEAD_DIM)
    m_ref[...] = jnp.full(m_ref.shape, NEG, F32)
    l_ref[...] = jnp.zeros(l_ref.shape, F32)
    acc_ref[...] = jnp.zeros(acc_ref.shape, F32)
    t_idx = q0 + lax.broadcasted_iota(I32, (GROUP, tq, tk), 1)
    k_off = lax.broadcasted_iota(I32, (GROUP, tq, tk), 2)
    c_hi = (q0 + tq + tk - 1) // tk
    c_lo = jnp.maximum(q0 - WINDOW, 0) // tk if kind == "nsa_win" else 0
    if kind == "nsa_sel":
        selb = extra[0][0, 0].astype(BF16)
    if kind == "moba":
        selb = extra[0][0, 0].reshape(rows, extra[0].shape[-1]).astype(BF16)

    def body(c, carry):
        start = pl.multiple_of(c * tk, tk)
        k = k_ref[0, g, pl.ds(start, tk), :]
        v = v_ref[0, g, pl.ds(start, tk), :]
        s3 = _dot_t(q, k).reshape(GROUP, tq, tk)
        s_idx = start + k_off
        causal = s_idx <= t_idx
        if kind == "fox":
            ck = extra[0][0, 0, :, pl.ds(start, tk)]
            s3 = s3 - ck[:, None, :]
            ok = causal
        elif kind == "dsa":
            msk = extra[0][0, :, pl.ds(start, tk)]
            ok = jnp.broadcast_to((msk > 0)[None], (GROUP, tq, tk))
        elif kind == "nsa_sel":
            hit = _dot(selb, extra[1][c])
            ok = causal & (hit > 0.5)[None]
        elif kind == "nsa_win":
            ok = causal & (s_idx >= t_idx - WINDOW)
        else:
            hit = _dot(selb, extra[1][c]).reshape(GROUP, tq, tk)
            own = (s_idx // MOBA_BLOCK) == (t_idx // MOBA_BLOCK)
            ok = (hit > 0.5) | (causal & own)
        s = jnp.where(ok, s3, NEG).reshape(rows, tk)
        okf = ok.reshape(rows, tk)
        m_prev = m_ref[...]
        m_new = jnp.maximum(m_prev, jnp.max(s, axis=1, keepdims=True))
        alpha = jnp.exp(m_prev - m_new)
        p = jnp.where(okf, jnp.exp(s - pltpu.repeat(m_new, tk // LANES, axis=1)), 0.0)
        l_ref[...] = alpha * l_ref[...] + jnp.sum(p, axis=1, keepdims=True)
        acc_ref[...] = acc_ref[...] * alpha[:, :HEAD_DIM] + _dot(p.astype(BF16), v)
        m_ref[...] = m_new
        return carry

    lax.fori_loop(c_lo, c_hi, body, 0)
    o = acc_ref[...] / jnp.maximum(l_ref[...][:, :HEAD_DIM], 1e-30)
    o_ref[0, 0] = o.reshape(GROUP, tq, HEAD_DIM)


def _flash_prompt(kind, q, k, v, extra, tq=128, tk=512):
    bsz, _, _, t, _ = q.shape
    tk = min(tk, t)
    tq = min(tq, t)
    qspec = pl.BlockSpec((1, 1, GROUP, tq, HEAD_DIM), lambda b, g, i: (b, g, 0, i, 0))
    kvspec = pl.BlockSpec((1, N_KV_HEADS, t, HEAD_DIM), lambda b, g, i: (b, 0, 0, 0))
    if kind == "fox":
        especs = [pl.BlockSpec((1, 1, GROUP, t), lambda b, g, i: (b, g, 0, 0))]
    elif kind == "dsa":
        especs = [pl.BlockSpec((1, tq, t), lambda b, g, i: (b, i, 0))]
    elif kind == "nsa_sel":
        especs = [pl.BlockSpec((1, 1, tq, extra[0].shape[-1]), lambda b, g, i: (b, g, i, 0)),
                  pl.BlockSpec(extra[1].shape, lambda b, g, i: (0, 0, 0))]
    elif kind == "moba":
        especs = [pl.BlockSpec((1, 1, GROUP, tq, extra[0].shape[-1]), lambda b, g, i: (b, g, 0, i, 0)),
                  pl.BlockSpec(extra[1].shape, lambda b, g, i: (0, 0, 0))]
    else:
        especs = []
    rows = GROUP * tq
    return pl.pallas_call(
        functools.partial(_flash_prompt_kernel, kind=kind, tq=tq, tk=tk, n_extra=len(extra)),
        grid=(bsz, N_KV_HEADS, t // tq),
        in_specs=[qspec, kvspec, kvspec] + especs,
        out_specs=qspec,
        out_shape=jax.ShapeDtypeStruct(q.shape, F32),
        scratch_shapes=[pltpu.VMEM((rows, LANES), F32), pltpu.VMEM((rows, LANES), F32),
                        pltpu.VMEM((rows, HEAD_DIM), F32)],
        compiler_params=_params("parallel", "parallel", "parallel"),
    )(q, k, v, *extra)


def _expand_matrix(n_blocks_padded, block, t, tk):
    s = np.arange(t)
    e = (s[None, :] // block == np.arange(n_blocks_padded)[:, None]).astype(np.float32)
    e = e.reshape(n_blocks_padded, t // tk, tk).transpose(1, 0, 2)
    return jnp.asarray(e, dtype=BF16)


def _dsa_select_prompt_kernel(qi_ref, ki_ref, wi_ref, mask_ref, u_ref, *, tq, t, tk, topk):
    q0 = pl.program_id(1) * tq
    w = wi_ref[0]
    t_idx = q0 + lax.broadcasted_iota(I32, (tq, tk), 0)
    k_off = lax.broadcasted_iota(I32, (tq, tk), 1)
    for c in range(t // tk):
        kc = ki_ref[0, c * tk:(c + 1) * tk, :]
        s = jnp.zeros((tq, tk), F32)
        for h in range(IDX_HEADS):
            s = s + jnp.maximum(_dot_t(qi_ref[0, h], kc), 0.0) * w[:, h:h + 1]
        s = jnp.where(c * tk + k_off <= t_idx, s, -jnp.inf)
        u_ref[:, c * tk:(c + 1) * tk] = _sortable(s)

    def write(ci, sel, u):
        keep = sel & (u > KEY_NEG_INF) & (u < KEY_POS_INF)
        mask_ref[0, :, ci * LANES:(ci + 1) * LANES] = jnp.where(keep, 1.0, 0.0).astype(BF16)

    _emit_selection(u_ref, topk, write)


def _dsa_select_prompt(qi, ki, wi, tq=128, tk=512):
    bsz, _, t, _ = qi.shape
    tq, tk = min(tq, t), min(tk, t)
    topk = min(IDX_TOPK, t // 4)
    return pl.pallas_call(
        functools.partial(_dsa_select_prompt_kernel, tq=tq, t=t, tk=tk, topk=topk),
        grid=(bsz, t // tq),
        in_specs=[pl.BlockSpec((1, IDX_HEADS, tq, IDX_DIM), lambda b, i: (b, 0, i, 0)),
                  pl.BlockSpec((1, t, IDX_DIM), lambda b, i: (b, 0, 0)),
                  pl.BlockSpec((1, tq, IDX_HEADS), lambda b, i: (b, i, 0))],
        out_specs=pl.BlockSpec((1, tq, t), lambda b, i: (b, i, 0)),
        out_shape=jax.ShapeDtypeStruct((bsz, t, t), BF16),
        scratch_shapes=[pltpu.VMEM((tq, t), I32)],
        compiler_params=_params("parallel", "parallel"),
    )(qi, ki, wi)


def _dsa_select_sample_kernel(pt_ref, qi_ref, wi_ref, *refs, n_pg, n_steps, tpad, n_new, topk):
    pages = refs[:n_pg]
    new_ref, mask_ref, u_ref = refs[n_pg:]
    j = pl.program_id(1)
    q = qi_ref[0]
    w = wi_ref[0]

    def scores(kc):
        rel = jnp.maximum(_dot_t(q, kc.astype(BF16)), 0.0) * w
        return rel.reshape(tpad, IDX_HEADS, LANES).sum(axis=1)

    @pl.when(j < n_steps - 1)
    def _():
        for i in range(n_pg):
            start = pl.multiple_of((j * n_pg + i) * LANES, LANES)
            u_ref[:, pl.ds(start, LANES)] = _sortable(scores(pages[i][0]))

    @pl.when(j == n_steps - 1)
    def _():
        s = scores(new_ref[0])
        t_idx = lax.broadcasted_iota(I32, (tpad, LANES), 0)
        c_idx = lax.broadcasted_iota(I32, (tpad, LANES), 1)
        s = jnp.where((c_idx <= t_idx) & (c_idx < n_new), s, -jnp.inf)
        base = (n_steps - 1) * n_pg * LANES
        u_ref[:, base:base + LANES] = _sortable(s)
        for i in range(1, n_pg):
            u_ref[:, base + i * LANES:base + (i + 1) * LANES] = jnp.full((tpad, LANES), KEY_NEG_INF, I32)

        def write(ci, sel, u):
            keep = sel & (u > KEY_NEG_INF) & (u < KEY_POS_INF)
            mask_ref[0, :, ci * LANES:(ci + 1) * LANES] = jnp.where(keep, 1.0, 0.0)

        _emit_selection(u_ref, topk, write)


def _dsa_select_sample(page_table, qi, wi, pool_kidx, new_ki, n_new, n_pg=8):
    bsz, n_pages = page_table.shape
    tpad = qi.shape[1] // IDX_HEADS
    n_steps = n_pages // n_pg + 1
    width = n_steps * n_pg * LANES
    topk = min(IDX_TOPK, (n_pages * PAGE_SIZE + n_new) // 4)
    fix = lambda b, j, pt: (b, 0, 0)
    grid_spec = pltpu.PrefetchScalarGridSpec(
        num_scalar_prefetch=1,
        grid=(bsz, n_steps),
        in_specs=[pl.BlockSpec((1, tpad * IDX_HEADS, IDX_DIM), fix), pl.BlockSpec((1, tpad * IDX_HEADS, 1), fix)]
        + [pl.BlockSpec((1, PAGE_SIZE, IDX_DIM), _page_map(i, n_pg, n_pages)) for i in range(n_pg)]
        + [pl.BlockSpec((1, PAGE_SIZE, IDX_DIM), fix)],
        out_specs=pl.BlockSpec((1, tpad, width), fix),
        scratch_shapes=[pltpu.VMEM((tpad, width), I32)],
    )
    return pl.pallas_call(
        functools.partial(_dsa_select_sample_kernel, n_pg=n_pg, n_steps=n_steps, tpad=tpad, n_new=n_new, topk=topk),
        grid_spec=grid_spec,
        out_shape=jax.ShapeDtypeStruct((bsz, tpad, width), F32),
        compiler_params=_params("parallel", "arbitrary"),
    )(page_table, qi, wi, *([pool_kidx] * n_pg), new_ki)


def _page_map(i, n_pg, n_pages):
    return lambda b, j, pt: (pt[b, jnp.minimum(j * n_pg + i, n_pages - 1)], 0, 0)


def _paged_flash_kernel(pt_ref, q_ref, *refs, n_pg, n_steps):
    pages = refs[:n_pg]
    new_ref, bias_ref, o_ref, m_ref, l_ref, acc_ref = refs[n_pg:]
    j = pl.program_id(1)
    q = q_ref[0]

    @pl.when(j == 0)
    def _():
        m_ref[...] = jnp.full(m_ref.shape, NEG, F32)
        l_ref[...] = jnp.zeros(l_ref.shape, F32)
        acc_ref[...] = jnp.zeros(acc_ref.shape, F32)

    def update(page, bias):
        k = page[:, :KV_COLS].astype(BF16)
        v = page[:, KV_COLS:].astype(BF16)
        s = _dot_t(q, k) + bias
        m_prev = m_ref[...]
        m_new = jnp.maximum(m_prev, jnp.max(s, axis=1, keepdims=True))
        alpha = jnp.exp(m_prev - m_new)
        p = jnp.where(s > 0.5 * NEG, jnp.exp(s - m_new), 0.0)
        l_ref[...] = alpha * l_ref[...] + jnp.sum(p, axis=1, keepdims=True)
        acc_ref[...] = acc_ref[...] * pltpu.repeat(alpha, KV_COLS // LANES, axis=1) + _dot(p.astype(BF16), v)
        m_ref[...] = m_new

    @pl.when(j < n_steps - 1)
    def _():
        for i in range(n_pg):
            update(pages[i][0], bias_ref[0, :, i * LANES:(i + 1) * LANES])

    @pl.when(j == n_steps - 1)
    def _():
        update(new_ref[0], bias_ref[0, :, :LANES])
        o_ref[0] = acc_ref[...] / jnp.maximum(pltpu.repeat(l_ref[...], KV_COLS // LANES, axis=1), 1e-30)


def _paged_flash(page_table, q_bd, pool, new_page, bias, n_pg):
    bsz, n_pages = page_table.shape
    n_steps = n_pages // n_pg + 1
    rows = q_bd.shape[1]
    fix = lambda b, j, pt: (b, 0, 0)
    grid_spec = pltpu.PrefetchScalarGridSpec(
        num_scalar_prefetch=1,
        grid=(bsz, n_steps),
        in_specs=[pl.BlockSpec((1, rows, KV_COLS), fix)]
        + [pl.BlockSpec((1, PAGE_SIZE, 2 * KV_COLS), _page_map(i, n_pg, n_pages)) for i in range(n_pg)]
        + [pl.BlockSpec((1, PAGE_SIZE, 2 * KV_COLS), fix),
           pl.BlockSpec((1, rows, n_pg * LANES), lambda b, j, pt: (b, 0, j))],
        out_specs=pl.BlockSpec((1, rows, KV_COLS), fix),
        scratch_shapes=[pltpu.VMEM((rows, LANES), F32), pltpu.VMEM((rows, LANES), F32),
                        pltpu.VMEM((rows, KV_COLS), F32)],
    )
    return pl.pallas_call(
        functools.partial(_paged_flash_kernel, n_pg=n_pg, n_steps=n_steps),
        grid_spec=grid_spec,
        out_shape=jax.ShapeDtypeStruct((bsz, rows, KV_COLS), F32),
        compiler_params=_params("parallel", "arbitrary"),
    )(page_table, q_bd, *([pool] * n_pg), new_page, bias)


def _cumsum_kernel(pt_ref, *refs, n_pg, n_steps):
    pages = refs[:n_pg]
    new_ref, o_ref, carry_ref = refs[n_pg:]
    j = pl.program_id(1)
    r_i = lax.broadcasted_iota(I32, (LANES, LANES), 0)
    c_i = lax.broadcasted_iota(I32, (LANES, LANES), 1)
    tri = jnp.where(r_i <= c_i, 1.0, 0.0).astype(BF16)

    @pl.when(j == 0)
    def _():
        carry_ref[...] = jnp.zeros(carry_ref.shape, F32)

    def step(x, i):
        c = _dot_hp(x, tri) + carry_ref[...]
        o_ref[0, :, i * LANES:(i + 1) * LANES] = c
        carry_ref[...] = jnp.broadcast_to(c[:, LANES - 1:LANES], c.shape)

    @pl.when(j < n_steps - 1)
    def _():
        for i in range(n_pg):
            step(pages[i][0], i)

    @pl.when(j == n_steps - 1)
    def _():
        step(new_ref[0], 0)
        for i in range(1, n_pg):
            o_ref[0, :, i * LANES:(i + 1) * LANES] = jnp.zeros((N_HEADS, LANES), F32)


def _paged_cumsum(page_table, pool_t, new_t, n_pg):
    bsz, n_pages = page_table.shape
    n_steps = n_pages // n_pg + 1
    fix = lambda b, j, pt: (b, 0, 0)
    grid_spec = pltpu.PrefetchScalarGridSpec(
        num_scalar_prefetch=1,
        grid=(bsz, n_steps),
        in_specs=[pl.BlockSpec((1, N_HEADS, LANES), _page_map(i, n_pg, n_pages)) for i in range(n_pg)]
        + [pl.BlockSpec((1, N_HEADS, LANES), fix)],
        out_specs=pl.BlockSpec((1, N_HEADS, n_pg * LANES), lambda b, j, pt: (b, 0, j)),
        scratch_shapes=[pltpu.VMEM((N_HEADS, LANES), F32)],
    )
    return pl.pallas_call(
        functools.partial(_cumsum_kernel, n_pg=n_pg, n_steps=n_steps),
        grid_spec=grid_spec,
        out_shape=jax.ShapeDtypeStruct((bsz, N_HEADS, n_steps * n_pg * LANES), F32),
        compiler_params=_params("parallel", "arbitrary"),
    )(page_table, *([pool_t] * n_pg), new_t)


def _gelu_tanh(x):
    return 0.5 * x * (1.0 + jnp.tanh(0.7978845608028654 * (x + 0.044715 * x * x * x)))


def _nsa_compress_kernel(pt_ref, *refs, n_pg, n_steps, nc):
    pages = refs[:n_pg]
    pe_ref, w1_ref, w2_ref, o_ref, x_ref = refs[n_pg:]
    c = pl.program_id(1) // N_KV_HEADS
    j = pl.program_id(2)
    per = PAGE_SIZE // CMP_STRIDE
    for i in range(n_pg):
        start = pl.multiple_of((j * n_pg + i) * per, per)
        x_ref[pl.ds(start, per), :] = pages[i][0, 0].astype(F32)

    @pl.when(j == n_steps - 1)
    def _():
        n_chunk = x_ref.shape[0]
        half = CMP_STRIDE * HEAD_DIM
        w1 = w1_ref[c]
        part = _dot(x_ref[...].astype(BF16), w1)
        pe = pe_ref[c]
        pe_term = _dot(pe[:, :half], w1)[:, :CMP_HID] + _dot(pe[:, half:], w1)[:, CMP_HID:]
        h = pe_term[0:1, :] + part[:, :CMP_HID] + pltpu.roll(part[:, CMP_HID:], n_chunk - 1, 0)
        out = _dot(_gelu_tanh(h).astype(BF16), w2_ref[c])
        row = lax.broadcasted_iota(I32, out.shape, 0)
        o_ref[0, 0] = jnp.where(row < nc, out, 0.0)


def _nsa_compress(page_table, pool_t, pe8, w1cat, w2, n_pg=8):
    bsz, n_pages = page_table.shape
    per = PAGE_SIZE // CMP_STRIDE
    n_chunk = n_pages * per
    nc = n_chunk - CMP_LEN // CMP_STRIDE + 1
    n_steps = n_pages // n_pg
    width = CMP_STRIDE * HEAD_DIM

    def page_map(i):
        return lambda b, ck, j, pt: (pt[b, j * n_pg + i], ck, 0, 0)

    fix3 = lambda b, ck, j, pt: (0, 0, 0)
    grid_spec = pltpu.PrefetchScalarGridSpec(
        num_scalar_prefetch=1,
        grid=(bsz, 2 * N_KV_HEADS, n_steps),
        in_specs=[pl.BlockSpec((1, 1, per, width), page_map(i)) for i in range(n_pg)]
        + [pl.BlockSpec(pe8.shape, fix3), pl.BlockSpec(w1cat.shape, fix3), pl.BlockSpec(w2.shape, fix3)],
        out_specs=pl.BlockSpec((1, 1, n_chunk, HEAD_DIM), lambda b, ck, j, pt: (b, ck, 0, 0)),
        scratch_shapes=[pltpu.VMEM((n_chunk, width), F32)],
    )
    return pl.pallas_call(
        functools.partial(_nsa_compress_kernel, n_pg=n_pg, n_steps=n_steps, nc=nc),
        grid_spec=grid_spec,
        out_shape=jax.ShapeDtypeStruct((bsz, 2 * N_KV_HEADS, n_chunk, HEAD_DIM), F32),
        compiler_params=_params("parallel", "parallel", "arbitrary"),
    )(page_table, *([pool_t] * n_pg), pe8, w1cat, w2), nc


def _nsa_cmp_kernel(q_ref, ck_ref, cv_ref, cover_ref, o_ref, sel_ref, u_ref, *, tq, pos0, nc, n_sel):
    t0 = pos0 + pl.program_id(1) * tq
    ncp = ck_ref.shape[2]
    nsp = cover_ref.shape[1]
    rows = GROUP * tq
    n_idx = lax.broadcasted_iota(I32, (tq, ncp), 1)
    t_idx = t0 + lax.broadcasted_iota(I32, (tq, ncp), 0)
    c_ok = ((n_idx * CMP_STRIDE + CMP_LEN - 1 <= t_idx) & (n_idx < nc))[None]
    blk = lax.broadcasted_iota(I32, (tq, nsp), 1)
    cur = (t0 + lax.broadcasted_iota(I32, (tq, nsp), 0)) // SEL_BLOCK
    forced = (blk == 0) | (blk == cur) | (blk == cur - 1)
    for g in range(N_KV_HEADS):
        q = q_ref[0, g].reshape(rows, HEAD_DIM)
        s3 = jnp.where(c_ok, _dot_t(q, ck_ref[0, g]).reshape(GROUP, tq, ncp), NEG)
        m = jnp.max(s3, axis=-1, keepdims=True)
        e = jnp.where(c_ok, jnp.exp(s3 - m), 0.0)
        p = e / jnp.maximum(jnp.sum(e, axis=-1, keepdims=True), 1e-30)
        o = _dot(p.reshape(rows, ncp).astype(BF16), cv_ref[0, g])
        o_ref[0, g] = o.reshape(GROUP, tq, HEAD_DIM)
        imp = _dot_hp(p[0] + p[1] + p[2] + p[3], cover_ref[...])
        imp = jnp.where(forced, jnp.inf, imp)
        imp = jnp.where(blk <= cur, imp, -jnp.inf)
        u_ref[g * tq:(g + 1) * tq, :] = _sortable(imp)

    def write(ci, sel, u):
        keep = jnp.where(sel & (u > KEY_NEG_INF), 1.0, 0.0)
        sel_ref[0, :, :, ci * LANES:(ci + 1) * LANES] = keep.reshape(N_KV_HEADS, tq, LANES)

    _emit_selection(u_ref, n_sel, write)


def _nsa_cmp_select(q, cmp_k, cmp_v, pos0, nc, n_keys, tq):
    bsz, _, _, t, _ = q.shape
    ncp = cmp_k.shape[2]
    ns = -(-n_keys // SEL_BLOCK)
    nsp = -(-ns // LANES) * LANES
    n_sel = min(SEL_TOPN, ns)
    c0 = np.arange(ncp)[:, None] * CMP_STRIDE
    s0 = np.arange(nsp)[None, :] * SEL_BLOCK
    cover = (c0 <= s0 + SEL_BLOCK - 1) & (c0 + CMP_LEN - 1 >= s0) & (np.arange(ncp)[:, None] < nc) & (np.arange(nsp)[None, :] < ns)
    cover = jnp.asarray(cover.astype(np.float32), dtype=BF16)
    qspec = pl.BlockSpec((1, N_KV_HEADS, GROUP, tq, HEAD_DIM), lambda b, i: (b, 0, 0, i, 0))
    cspec = pl.BlockSpec((1, N_KV_HEADS, ncp, HEAD_DIM), lambda b, i: (b, 0, 0, 0))
    return pl.pallas_call(
        functools.partial(_nsa_cmp_kernel, tq=tq, pos0=pos0, nc=nc, n_sel=n_sel),
        grid=(bsz, t // tq),
        in_specs=[qspec, cspec, cspec, pl.BlockSpec((ncp, nsp), lambda b, i: (0, 0))],
        out_specs=[qspec, pl.BlockSpec((1, N_KV_HEADS, tq, nsp), lambda b, i: (b, 0, i, 0))],
        out_shape=[jax.ShapeDtypeStruct(q.shape, F32), jax.ShapeDtypeStruct((bsz, N_KV_HEADS, t, nsp), F32)],
        scratch_shapes=[pltpu.VMEM((N_KV_HEADS * tq, nsp), I32)],
        compiler_params=_params("parallel", "parallel"),
    )(q, cmp_k, cmp_v, cover)


def _kmean_kernel(*refs):
    o_ref = refs[-1]
    tot = jnp.sum(refs[0][0], axis=0, keepdims=True)
    for r in refs[1:-1]:
        tot = tot + jnp.sum(r[0], axis=0, keepdims=True)
    o_ref[0, 0] = tot * (1.0 / MOBA_BLOCK)


def _kmean_prompt(kv):
    bsz, t, _ = kv.shape
    nb = t // MOBA_BLOCK
    return pl.pallas_call(
        _kmean_kernel,
        grid=(bsz, nb),
        in_specs=[pl.BlockSpec((1, MOBA_BLOCK, KV_COLS), lambda b, i: (b, i, 0))],
        out_specs=pl.BlockSpec((1, 1, 1, KV_COLS), lambda b, i: (b, i, 0, 0)),
        out_shape=jax.ShapeDtypeStruct((bsz, nb, 1, KV_COLS), F32),
        compiler_params=_params("parallel", "parallel"),
    )(kv)


def _kmean_sample(page_table, pool):
    bsz, n_pages = page_table.shape
    per = MOBA_BLOCK // PAGE_SIZE
    nb = n_pages // per
    grid_spec = pltpu.PrefetchScalarGridSpec(
        num_scalar_prefetch=1,
        grid=(bsz, nb),
        in_specs=[pl.BlockSpec((1, PAGE_SIZE, KV_COLS), (lambda i: (lambda b, n, pt: (pt[b, n * per + i], 0, 0)))(i))
                  for i in range(per)],
        out_specs=pl.BlockSpec((1, 1, 1, KV_COLS), lambda b, n, pt: (b, n, 0, 0)),
    )

    def body(pt_ref, *refs):
        _kmean_kernel(*refs)

    return pl.pallas_call(
        body,
        grid_spec=grid_spec,
        out_shape=jax.ShapeDtypeStruct((bsz, nb, 1, KV_COLS), F32),
        compiler_params=_params("parallel", "parallel"),
    )(page_table, *([pool] * per))


def _moba_select_kernel(q_ref, km_ref, sel_ref, u_ref, *, tq, pos0, k_top):
    t0 = pos0 + pl.program_id(1) * tq
    nbp = km_ref.shape[2]
    rows = GROUP * tq
    blk = lax.broadcasted_iota(I32, (GROUP, tq, nbp), 2)
    n_past = (t0 + lax.broadcasted_iota(I32, (GROUP, tq, nbp), 1)) // MOBA_BLOCK
    for g in range(N_KV_HEADS):
        q = q_ref[0, g].reshape(rows, HEAD_DIM)
        s3 = _dot_t(q, km_ref[0, g]).reshape(GROUP, tq, nbp)
        s3 = jnp.where(blk < n_past, s3, -jnp.inf)
        u_ref[g * rows:(g + 1) * rows, :] = _sortable(s3.reshape(rows, nbp))

    def write(ci, sel, u):
        keep = jnp.where(sel & (u > KEY_NEG_INF), 1.0, 0.0)
        sel_ref[0, :, :, :, ci * LANES:(ci + 1) * LANES] = keep.reshape(N_KV_HEADS, GROUP, tq, LANES)

    _emit_selection(u_ref, k_top, write)


def _moba_select(q, kmean, pos0, nb, tq):
    bsz, _, _, t, _ = q.shape
    nbp = kmean.shape[2]
    qspec = pl.BlockSpec((1, N_KV_HEADS, GROUP, tq, HEAD_DIM), lambda b, i: (b, 0, 0, i, 0))
    return pl.pallas_call(
        functools.partial(_moba_select_kernel, tq=tq, pos0=pos0, k_top=min(MOBA_TOPK, nb)),
        grid=(bsz, t // tq),
        in_specs=[qspec, pl.BlockSpec((1, N_KV_HEADS, nbp, HEAD_DIM), lambda b, i: (b, 0, 0, 0))],
        out_specs=pl.BlockSpec((1, N_KV_HEADS, GROUP, tq, nbp), lambda b, i: (b, 0, 0, i, 0)),
        out_shape=jax.ShapeDtypeStruct((bsz, N_KV_HEADS, GROUP, t, nbp), F32),
        scratch_shapes=[pltpu.VMEM((N_HEADS * tq, nbp), I32)],
        compiler_params=_params("parallel", "parallel"),
    )(q, kmean)


def _rope_tables(pos):
    half = HEAD_DIM // 2
    inv = ROPE_THETA ** (-jnp.arange(half, dtype=F32) / half)
    ang = pos.astype(F32)[:, None] * inv[None, :]
    return jnp.cos(ang)[:, None, :], jnp.sin(ang)[:, None, :]


def _rope(x, cs):
    cos, sin = cs
    half = x.shape[-1] // 2
    x1, x2 = x[..., :half], x[..., half:]
    return jnp.concatenate([x1 * cos - x2 * sin, x1 * sin + x2 * cos], axis=-1)


def _pad_to(x, axis, size):
    pad = [(0, 0)] * x.ndim
    pad[axis] = (0, size - x.shape[axis])
    return jnp.pad(x, pad)


def _q_groups(q):
    b, t = q.shape[:2]
    return (q * ATTN_SCALE).transpose(0, 2, 1, 3).reshape(b, N_KV_HEADS, GROUP, t, HEAD_DIM).astype(BF16)


def _kv_major(kv):
    return kv[:, :, 0].transpose(0, 2, 1, 3).astype(BF16), kv[:, :, 1].transpose(0, 2, 1, 3).astype(BF16)


def _merge_heads(o):
    b, _, _, t, _ = o.shape
    return o.reshape(b, N_HEADS, t, HEAD_DIM).transpose(0, 2, 1, 3).reshape(b * t, Q_DIM)


_HEAD_TO_GROUP = np.equal(np.arange(N_HEADS)[:, None] // GROUP, np.arange(N_KV_HEADS)[None, :]).astype(np.float32)


def _q_block_diag(q):
    s, tn = q.shape[:2]
    qb = (q * ATTN_SCALE)[:, :, :, None, :] * _HEAD_TO_GROUP[None, None, :, :, None]
    return qb.reshape(s, tn * N_HEADS, KV_COLS).astype(BF16)


def _extract_block_diag(o, tn):
    s = o.shape[0]
    o5 = o.reshape(s, tn, N_HEADS, N_KV_HEADS, HEAD_DIM) * _HEAD_TO_GROUP[None, None, :, :, None]
    return o5.sum(axis=3).reshape(s * tn, Q_DIM)


def _new_page(kv_new):
    s, tn = kv_new.shape[:2]
    return _pad_to(kv_new.reshape(s, tn, 2 * KV_COLS), 1, PAGE_SIZE)


def _pick_pages(n_pages):
    for n in (8, 4, 2, 1):
        if n_pages % n == 0:
            return n


def _identity_pages(bsz, n_pages):
    return jnp.arange(bsz * n_pages, dtype=I32).reshape(bsz, n_pages)


def _split(proj, sizes):
    out, o = [], 0
    for s in sizes:
        out.append(proj[:, o:o + s])
        o += s
    return out


def _rows_th(ok, s, tn):
    return jnp.where(ok, 0.0, NEG).astype(F32).reshape(s, tn * N_HEADS, ok.shape[-1])


def _dsa_project(x, bsz, t, w_in, cs):
    q, k, v, qi, ki, wi = _split(_matmul(x, w_in), [Q_DIM, KV_COLS, KV_COLS, IDX_HEADS * IDX_DIM, IDX_DIM, IDX_HEADS])
    q = _rope(q.reshape(bsz, t, N_HEADS, HEAD_DIM), cs)
    k = _rope(k.reshape(bsz, t, N_KV_HEADS, HEAD_DIM), cs)
    kv = jnp.stack([k, v.reshape(bsz, t, N_KV_HEADS, HEAD_DIM)], axis=2)
    qi = _rope(qi.reshape(bsz, t, IDX_HEADS, IDX_DIM), cs)
    ki = _rope(ki.reshape(bsz, t, 1, IDX_DIM), cs)[:, :, 0]
    return q, qi, wi.reshape(bsz, t, IDX_HEADS) * IDX_SCALE, kv, ki


def _dsa_prompt(x, bsz, t, w_in, cs):
    q, qi, wi, kv, ki = _dsa_project(x, bsz, t, w_in, cs)
    mask = _dsa_select_prompt(qi.transpose(0, 2, 1, 3).astype(BF16), ki.astype(BF16), wi)
    kh, vh = _kv_major(kv)
    o = _flash_prompt("dsa", _q_groups(q), kh, vh, [mask])
    return _merge_heads(o), kv, ki


def _dsa_sample(x, s, tn, w_in, cs, page_table, cache_kv, cache_kidx, n_pg):
    q, qi, wi, kv, ki = _dsa_project(x, s, tn, w_in, cs)
    tpad = 8
    qi_p = _pad_to(qi, 1, tpad).reshape(s, tpad * IDX_HEADS, IDX_DIM).astype(BF16)
    wi_p = _pad_to(wi, 1, tpad).reshape(s, tpad * IDX_HEADS, 1)
    mask = _dsa_select_sample(page_table, qi_p, wi_p, cache_kidx, _pad_to(ki, 1, PAGE_SIZE), tn, n_pg)
    ok = jnp.broadcast_to(mask[:, :tn, None, :] > 0.5, (s, tn, N_HEADS, mask.shape[-1]))
    o = _paged_flash(page_table, _q_block_diag(q), cache_kv.reshape(-1, PAGE_SIZE, 2 * KV_COLS), _new_page(kv),
                     _rows_th(ok, s, tn), n_pg)
    return _extract_block_diag(o, tn), kv, ki


def _fox_project(x, bsz, t, w_in, b_f):
    q, k, v, f = _split(_matmul(x, w_in), [Q_DIM, KV_COLS, KV_COLS, N_HEADS])
    logf = jax.nn.log_sigmoid(f.reshape(bsz, t, N_HEADS) + b_f)
    kv = jnp.stack([k.reshape(bsz, t, N_KV_HEADS, HEAD_DIM), v.reshape(bsz, t, N_KV_HEADS, HEAD_DIM)], axis=2)
    return q.reshape(bsz, t, N_HEADS, HEAD_DIM), kv, logf


def _fox_prompt(x, bsz, t, w_in, b_f):
    q, kv, logf = _fox_project(x, bsz, t, w_in, b_f)
    n_pages = t // LANES
    pool_t = logf.reshape(bsz, n_pages, LANES, N_HEADS).transpose(0, 1, 3, 2).reshape(bsz * n_pages, N_HEADS, LANES)
    c = _paged_cumsum(_identity_pages(bsz, n_pages), pool_t, jnp.zeros((bsz, N_HEADS, LANES), F32), _pick_pages(n_pages))
    c = c[:, :, :t].reshape(bsz, N_KV_HEADS, GROUP, t)
    kh, vh = _kv_major(kv)
    o = _flash_prompt("fox", _q_groups(q), kh, vh, [c])
    return _merge_heads(o), kv, logf


def _fox_sample(x, s, tn, w_in, b_f, page_table, cache_kv, cache_logf, n_pg):
    q, kv, logf = _fox_project(x, s, tn, w_in, b_f)
    past = page_table.shape[1] * PAGE_SIZE
    c = _paged_cumsum(page_table, cache_logf.transpose(0, 2, 1), _pad_to(logf.transpose(0, 2, 1), 2, LANES), n_pg)
    col = jnp.arange(c.shape[-1])
    valid = (col[None, :] < past) | ((col[None, :] - past <= jnp.arange(tn)[:, None]) & (col[None, :] < past + tn))
    bias = jnp.where(valid[None, :, None, :], -c[:, None, :, :], NEG).reshape(s, tn * N_HEADS, c.shape[-1])
    o = _paged_flash(page_table, _q_block_diag(q), cache_kv.reshape(-1, PAGE_SIZE, 2 * KV_COLS), _new_page(kv), bias, n_pg)
    return _extract_block_diag(o, tn), kv, logf


def _nsa_project(x, bsz, t, w_in, b_gate, cs):
    q, kc, vc, ks, vs, kw, vw, g = _split(_matmul(x, w_in), [Q_DIM] + [KV_COLS] * 6 + [3 * N_HEADS])
    hd = lambda a: a.reshape(bsz, t, N_KV_HEADS, HEAD_DIM)
    q = q.reshape(bsz, t, N_HEADS, HEAD_DIM)
    kv_cmp = jnp.stack([hd(kc), hd(vc)], axis=2)
    kv_slc = jnp.stack([_rope(hd(ks), cs), hd(vs)], axis=2)
    kv_win = jnp.stack([_rope(hd(kw), cs), hd(vw)], axis=2)
    gate = jax.nn.sigmoid(g.reshape(bsz, t, 3 * N_HEADS) + b_gate).reshape(bsz, t, 3, N_HEADS)
    return q, _rope(q, cs), gate, kv_cmp, kv_slc, kv_win


def _nsa_weights(pe, w1, w2):
    r = CMP_LEN // CMP_STRIDE
    w1cat = w1.reshape(2, r, CMP_STRIDE * HEAD_DIM, CMP_HID).transpose(0, 2, 1, 3).reshape(2, CMP_STRIDE * HEAD_DIM, r * CMP_HID)
    pe8 = jnp.broadcast_to(pe.reshape(2, 1, CMP_LEN * HEAD_DIM), (2, 8, CMP_LEN * HEAD_DIM))
    return pe8.astype(BF16), w1cat.astype(BF16), w2.astype(BF16)


def _chunk_pages(kv):
    n = kv.shape[0]
    per = PAGE_SIZE // CMP_STRIDE
    x = kv.reshape(n, per, CMP_STRIDE, 2 * N_KV_HEADS, HEAD_DIM).transpose(0, 3, 1, 2, 4)
    return x.reshape(n, 2 * N_KV_HEADS, per, CMP_STRIDE * HEAD_DIM).astype(BF16)


def _gate_mix(gate, o_c, o_s, o_w):
    b, t = gate.shape[:2]
    g = gate.transpose(0, 2, 3, 1).reshape(b, 3, N_KV_HEADS, GROUP, t, 1)
    return g[:, 0] * o_c + g[:, 1] * o_s + g[:, 2] * o_w


def _nsa_prompt(x, bsz, t, w_in, b_gate, cmp_w, cs, tq=128, tk=512):
    q, q_rot, gate, kv_cmp, kv_slc, kv_win = _nsa_project(x, bsz, t, w_in, b_gate, cs)
    n_pages = t // PAGE_SIZE
    cmp, nc = _nsa_compress(_identity_pages(bsz, n_pages), _chunk_pages(kv_cmp.reshape(bsz * n_pages, PAGE_SIZE, 2, N_KV_HEADS, HEAD_DIM)),
                            *cmp_w, n_pg=_pick_pages(n_pages))
    cmp = cmp.astype(BF16)
    tq, tk = min(tq, t), min(tk, t)
    o_c, selblk = _nsa_cmp_select(_q_groups(q), cmp[:, :N_KV_HEADS], cmp[:, N_KV_HEADS:], 0, nc, t, tq)
    qg = _q_groups(q_rot)
    e3 = _expand_matrix(selblk.shape[-1], SEL_BLOCK, t, tk)
    o_s = _flash_prompt("nsa_sel", qg, *_kv_major(kv_slc), [selblk, e3], tq, tk)
    o_w = _flash_prompt("nsa_win", qg, *_kv_major(kv_win), [], tq, tk)
    o = _gate_mix(gate, o_c, o_s, o_w)
    return _merge_heads(o), kv_cmp, kv_slc, kv_win[:, -min(WINDOW, t):]


def _nsa_sample(x, s, tn, w_in, b_gate, cmp_w, cs, page_table, cache_cmp, cache_slc, state_win, n_pg):
    q, q_rot, gate, kv_cmp, kv_slc, kv_win = _nsa_project(x, s, tn, w_in, b_gate, cs)
    past = page_table.shape[1] * PAGE_SIZE
    tpad = 8
    cmp, nc = _nsa_compress(page_table, _chunk_pages(cache_cmp), *cmp_w, n_pg=n_pg)
    cmp = cmp.astype(BF16)
    o_c, selblk = _nsa_cmp_select(_q_groups(_pad_to(q, 1, tpad)), cmp[:, :N_KV_HEADS], cmp[:, N_KV_HEADS:],
                                  past, nc, past + tn, tpad)
    o_c = o_c[:, :, :, :tn]
    qbd = _q_block_diag(q_rot)
    width = (page_table.shape[1] // n_pg + 1) * n_pg * LANES
    col = jnp.arange(width)
    pos = past + jnp.arange(tn)
    sel_key = jnp.repeat(selblk[:, :, :tn, :-(-width // SEL_BLOCK)], SEL_BLOCK, axis=-1)[..., :width]
    ok = (sel_key > 0.5) & (col[None, :] <= pos[:, None])[None, None]
    ok = jnp.broadcast_to(ok.transpose(0, 2, 1, 3)[:, :, :, None, :], (s, tn, N_KV_HEADS, GROUP, width))
    o_s = _paged_flash(page_table, qbd, cache_slc.reshape(-1, PAGE_SIZE, 2 * KV_COLS), _new_page(kv_slc),
                       _rows_th(ok.reshape(s, tn, N_HEADS, width), s, tn), n_pg)
    win_buf = state_win.shape[1]
    n_wp = win_buf // PAGE_SIZE
    wcol = jnp.arange(2 * n_wp * LANES)
    win_pos = jnp.where(wcol < win_buf, past - win_buf + wcol, jnp.where(wcol < win_buf + tn, past + wcol - win_buf, -1))
    w_ok = (win_pos[None, :] <= pos[:, None]) & (win_pos[None, :] >= pos[:, None] - WINDOW) & (win_pos[None, :] >= 0)
    w_ok = jnp.broadcast_to(w_ok[None, :, None, :], (s, tn, N_HEADS, wcol.shape[0]))
    o_w = _paged_flash(_identity_pages(s, n_wp), qbd, state_win.reshape(s * n_wp, PAGE_SIZE, 2 * KV_COLS), _new_page(kv_win),
                       _rows_th(w_ok, s, tn), n_wp)
    unbd = lambda o: _extract_block_diag(o, tn).reshape(s, tn, N_KV_HEADS, GROUP, HEAD_DIM).transpose(0, 2, 3, 1, 4)
    o = _gate_mix(gate, o_c, unbd(o_s), unbd(o_w))
    win = jnp.concatenate([state_win, kv_win], axis=1)[:, -win_buf:]
    return _merge_heads(o), kv_cmp, kv_slc, win


def _moba_project(x, bsz, t, w_in, cs):
    q, k, v = _split(_matmul(x, w_in), [Q_DIM, KV_COLS, KV_COLS])
    q = _rope(q.reshape(bsz, t, N_HEADS, HEAD_DIM), cs)
    kv = jnp.stack([_rope(k.reshape(bsz, t, N_KV_HEADS, HEAD_DIM), cs), v.reshape(bsz, t, N_KV_HEADS, HEAD_DIM)], axis=2)
    return q, kv


def _kmean_heads(km):
    b, nb = km.shape[:2]
    return _pad_to(km.reshape(b, nb, N_KV_HEADS, HEAD_DIM).transpose(0, 2, 1, 3), 2, LANES).astype(BF16)


def _moba_prompt(x, bsz, t, w_in, cs, tq=128, tk=512):
    q, kv = _moba_project(x, bsz, t, w_in, cs)
    tq, tk = min(tq, t), min(tk, t)
    nb = -(-t // MOBA_BLOCK)
    km = _kmean_heads(_kmean_prompt(kv.reshape(bsz, t, 2 * KV_COLS)))
    qg = _q_groups(q)
    sel = _moba_select(qg, km, 0, nb, tq)
    e3 = _expand_matrix(LANES, MOBA_BLOCK, t, tk)
    o = _flash_prompt("moba", qg, *_kv_major(kv), [sel, e3], tq, tk)
    return _merge_heads(o), kv


def _moba_sample(x, s, tn, w_in, cs, page_table, cache_kv, n_pg):
    q, kv = _moba_project(x, s, tn, w_in, cs)
    past = page_table.shape[1] * PAGE_SIZE
    tpad = 8
    pool = cache_kv.reshape(-1, PAGE_SIZE, 2 * KV_COLS)
    nb = -(-(past + tn) // MOBA_BLOCK)
    km = _kmean_heads(_kmean_sample(page_table, pool))
    sel = _moba_select(_q_groups(_pad_to(q, 1, tpad)), km, past, nb, tpad)
    width = (page_table.shape[1] // n_pg + 1) * n_pg * LANES
    col = jnp.arange(width)
    pos = past + jnp.arange(tn)
    sel_key = jnp.repeat(sel[:, :, :, :tn, :-(-width // MOBA_BLOCK)], MOBA_BLOCK, axis=-1)[..., :width]
    own = (col[None, :] // MOBA_BLOCK == pos[:, None] // MOBA_BLOCK) & (col[None, :] <= pos[:, None])
    ok = (sel_key > 0.5) | own[None, None, None]
    ok = ok.transpose(0, 3, 1, 2, 4).reshape(s, tn, N_HEADS, width)
    o = _paged_flash(page_table, _q_block_diag(q), pool, _new_page(kv), _rows_th(ok, s, tn), n_pg)
    return _extract_block_diag(o, tn), kv


def _cast_w(w):
    return _pad_to(w, 1, -(-w.shape[1] // LANES) * LANES).astype(BF16)


def kernel(x_prompt, x_sample, cache_a_kv, cache_a_kidx, cache_b_kv, cache_b_logf, cache_c_cmp_kv, cache_c_slc_kv, state_c_win_kv, cache_d_kv, page_table, a_w_in, a_w_out, b_w_in, b_b_f, b_w_out, c_w_in, c_b_gate, c_cmp_pe, c_cmp_w1, c_cmp_w2, c_w_out, d_w_in, d_w_out, ln_g, ln_b, ffn_w_gu, ffn_w_down):
    bsz, t, d = x_prompt.shape
    s, tn, _ = x_sample.shape
    n_pages = page_table.shape[1]
    past = n_pages * PAGE_SIZE
    n_pg = _pick_pages(n_pages)
    cs_p = _rope_tables(jnp.arange(t, dtype=I32))
    cs_s = _rope_tables(past + jnp.arange(tn, dtype=I32))
    xp = x_prompt.reshape(bsz * t, d)
    xs = x_sample.reshape(s * tn, d)
    cmp_w = _nsa_weights(c_cmp_pe, c_cmp_w1, c_cmp_w2)
    w_out = [_cast_w(w) for w in (a_w_out, b_w_out, c_w_out, d_w_out)]

    op, a_kv_p, a_kidx_p = _dsa_prompt(xp, bsz, t, _cast_w(a_w_in), cs_p)
    os_, a_kv_s, a_kidx_s = _dsa_sample(xs, s, tn, _cast_w(a_w_in), cs_s, page_table, cache_a_kv, cache_a_kidx, n_pg)

    def finish(i, xp, xs, op, os_):
        xp = _out_ln(op, w_out[i], xp, ln_g[i, 0], ln_b[i, 0])
        xs = _out_ln(os_, w_out[i], xs, ln_g[i, 0], ln_b[i, 0])
        wgu, wd = ffn_w_gu[i].astype(BF16), ffn_w_down[i].astype(BF16)
        xp = _ffn_ln(xp, wgu, wd, ln_g[i, 1], ln_b[i, 1])
        xs = _ffn_ln(xs, wgu, wd, ln_g[i, 1], ln_b[i, 1])
        return xp, xs

    xp, xs = finish(0, xp, xs, op, os_)

    op, b_kv_p, b_logf_p = _fox_prompt(xp, bsz, t, _cast_w(b_w_in), b_b_f)
    os_, b_kv_s, b_logf_s = _fox_sample(xs, s, tn, _cast_w(b_w_in), b_b_f, page_table, cache_b_kv, cache_b_logf, n_pg)
    xp, xs = finish(1, xp, xs, op, os_)

    op, c_cmp_kv_p, c_slc_kv_p, c_win_kv_p = _nsa_prompt(xp, bsz, t, _cast_w(c_w_in), c_b_gate, cmp_w, cs_p)
    os_, c_cmp_kv_s, c_slc_kv_s, c_win_kv_s = _nsa_sample(xs, s, tn, _cast_w(c_w_in), c_b_gate, cmp_w, cs_s, page_table,
                                                          cache_c_cmp_kv, cache_c_slc_kv, state_c_win_kv, n_pg)
    xp, xs = finish(2, xp, xs, op, os_)

    op, d_kv_p = _moba_prompt(xp, bsz, t, _cast_w(d_w_in), cs_p)
    os_, d_kv_s = _moba_sample(xs, s, tn, _cast_w(d_w_in), cs_s, page_table, cache_d_kv, n_pg)
    xp, xs = finish(3, xp, xs, op, os_)

    return (xp.reshape(bsz, t, d), xs.reshape(s, tn, d), a_kv_p, a_kv_s, a_kidx_p, a_kidx_s, b_kv_p, b_kv_s,
            b_logf_p, b_logf_s, c_cmp_kv_p, c_cmp_kv_s, c_slc_kv_p, c_slc_kv_s, c_win_kv_p, c_win_kv_s, d_kv_p, d_kv_s)
```

```python
import functools

import numpy as np
import jax
import jax.numpy as jnp
from jax import lax
from jax.experimental import pallas as pl
from jax.experimental.pallas import tpu as pltpu

F32 = jnp.float32
BF16 = jnp.bfloat16
I32 = jnp.int32

N_HEADS = 16
HEAD_DIM = 64
N_KV_HEADS = 4
GROUP = N_HEADS // N_KV_HEADS
Q_DIM = N_HEADS * HEAD_DIM
KV_COLS = N_KV_HEADS * HEAD_DIM
DEPTH = 4
PAGE_SIZE = 128
ROPE_THETA = 10000.0
LN_EPS = 1e-5
ALPHA = (2 * DEPTH) ** 0.25
ATTN_SCALE = HEAD_DIM ** -0.5
LOG2E = 1.4426950408889634
IDX_HEADS = 8
IDX_DIM = 64
IDX_TOPK = 256
IDX_SCALE = (IDX_HEADS * IDX_DIM) ** -0.5
CMP_LEN = 32
CMP_STRIDE = 16
CMP_HID = 2 * HEAD_DIM
SEL_BLOCK = 64
SEL_TOPN = 16
WINDOW = 512
MOBA_BLOCK = 256
MOBA_TOPK = 3

LANES = 128
VMEM_LIMIT = 56 * 2 ** 20
NEG = -1e30
KEY_NEG_INF = -2139095041
KEY_POS_INF = 2139095040
INT_MIN = -2 ** 31


def _params(*sem):
    return pltpu.CompilerParams(dimension_semantics=sem, vmem_limit_bytes=VMEM_LIMIT)


def _dot_t(a, b):
    return lax.dot_general(a, b, (((1,), (1,)), ((), ())), preferred_element_type=F32)


def _dot(a, b):
    return jnp.dot(a, b, preferred_element_type=F32)


def _dot_hp(a, b):
    hi = a.astype(BF16)
    r1 = a - hi.astype(F32)
    mid = r1.astype(BF16)
    lo = (r1 - mid.astype(F32)).astype(BF16)
    return _dot(hi, b) + _dot(mid, b) + _dot(lo, b)


def _mm_kernel(x_ref, w_ref, o_ref):
    o_ref[...] = _dot(x_ref[...].astype(BF16), w_ref[...])


def _matmul(x, w):
    m, k = x.shape
    n = w.shape[1]
    tm = min(m, 512)
    return pl.pallas_call(
        _mm_kernel,
        grid=(m // tm,),
        in_specs=[pl.BlockSpec((tm, k), lambda i: (i, 0)), pl.BlockSpec((k, n), lambda i: (0, 0))],
        out_specs=pl.BlockSpec((tm, n), lambda i: (i, 0)),
        out_shape=jax.ShapeDtypeStruct((m, n), F32),
        compiler_params=_params("parallel"),
        name="in_proj",
    )(x, w)


def _layer_norm(y, g, b):
    mu = jnp.mean(y, axis=-1, keepdims=True)
    d = y - mu
    var = jnp.mean(d * d, axis=-1, keepdims=True)
    return d * lax.rsqrt(var + LN_EPS) * g + b


def _out_ln_kernel(o_ref, w_ref, x_ref, g_ref, b_ref, y_ref):
    y = ALPHA * x_ref[...] + _dot(o_ref[...].astype(BF16), w_ref[...])
    y_ref[...] = _layer_norm(y, g_ref[...], b_ref[...])


def _out_ln(o, w, x, g, b):
    m, d = x.shape
    k = o.shape[1]
    tm = min(m, 512)
    row = lambda i: (i, 0)
    fix = lambda i: (0, 0)
    return pl.pallas_call(
        _out_ln_kernel,
        grid=(m // tm,),
        in_specs=[pl.BlockSpec((tm, k), row), pl.BlockSpec((k, d), fix), pl.BlockSpec((tm, d), row),
                  pl.BlockSpec((1, d), fix), pl.BlockSpec((1, d), fix)],
        out_specs=pl.BlockSpec((tm, d), row),
        out_shape=jax.ShapeDtypeStruct((m, d), F32),
        compiler_params=_params("parallel"),
        name="out_proj_ln",
    )(o, w, x, g.reshape(1, d), b.reshape(1, d))


def _ffn_ln_kernel(x_ref, wgu_ref, wd_ref, g_ref, b_ref, y_ref, *, d_ff, chunk):
    x = x_ref[...]
    xb = x.astype(BF16)
    acc = jnp.zeros(x.shape, F32)
    for c in range(d_ff // chunk):
        gate = _dot(xb, wgu_ref[:, c * chunk:(c + 1) * chunk])
        up = _dot(xb, wgu_ref[:, d_ff + c * chunk:d_ff + (c + 1) * chunk])
        h = gate * (1.0 / (1.0 + jnp.exp(-gate))) * up
        acc = acc + _dot(h.astype(BF16), wd_ref[c * chunk:(c + 1) * chunk, :])
    y_ref[...] = _layer_norm(ALPHA * x + acc, g_ref[...], b_ref[...])


def _ffn_ln(x, wgu, wd, g, b):
    m, d = x.shape
    d_ff = wd.shape[0]
    tm = min(m, 256)
    row = lambda i: (i, 0)
    fix = lambda i: (0, 0)
    return pl.pallas_call(
        functools.partial(_ffn_ln_kernel, d_ff=d_ff, chunk=256),
        grid=(m // tm,),
        in_specs=[pl.BlockSpec((tm, d), row), pl.BlockSpec((d, 2 * d_ff), fix), pl.BlockSpec((d_ff, d), fix),
                  pl.BlockSpec((1, d), fix), pl.BlockSpec((1, d), fix)],
        out_specs=pl.BlockSpec((tm, d), row),
        out_shape=jax.ShapeDtypeStruct((m, d), F32),
        compiler_params=_params("parallel"),
        name="ffn_ln",
    )(x, wgu, wd, g.reshape(1, d), b.reshape(1, d))


def _sortable(x):
    x = jnp.where(x == 0.0, 0.0, x)
    b = lax.bitcast_convert_type(x, I32)
    return b ^ ((b >> 31) & I32(0x7FFFFFFF))


def _kth_largest_key(u_ref, k):
    rows = u_ref.shape[0]

    def count_ge(cand):
        return jnp.sum((u_ref[...] >= cand).astype(I32), axis=1, keepdims=True)

    base = jnp.where(count_ge(jnp.zeros((rows, 1), I32)) >= k, I32(0), I32(INT_MIN))

    def body(i, base):
        cand = base | jnp.left_shift(I32(1), 30 - i)
        return jnp.where(count_ge(cand) >= k, cand, base)

    return lax.fori_loop(0, 31, body, base)


def _emit_selection(u_ref, k, write):
    rows, n = u_ref.shape
    thr = _kth_largest_key(u_ref, k)
    n_gt = jnp.sum((u_ref[...] > thr).astype(I32), axis=1, keepdims=True)
    need = (k - n_gt).astype(F32)
    r_i = lax.broadcasted_iota(I32, (LANES, LANES), 0)
    c_i = lax.broadcasted_iota(I32, (LANES, LANES), 1)
    tri = jnp.where(r_i <= c_i, 1.0, 0.0).astype(BF16)
    carry = jnp.zeros((rows, 1), F32)
    for ci in range(n // LANES):
        u = u_ref[:, ci * LANES:(ci + 1) * LANES]
        tie = u == thr
        tie_f = jnp.where(tie, 1.0, 0.0)
        inc = _dot(tie_f.astype(BF16), tri)
        rank = carry + inc - tie_f
        sel = (u > thr) | (tie & (rank < need))
        write(ci, sel, u)
        carry = carry + inc[:, LANES - 1:LANES]


def _flash_prompt_kernel(*refs, kind, tq, tk, n_extra):
    q_ref, k_ref, v_ref = refs[:3]
    extra = refs[3:3 + n_extra]
    o_ref, m_ref, acc_ref, s_ref = refs[3 + n_extra:]
    g = pl.program_id(1)
    q0 = pl.program_id(2) * tq
    rows = GROUP * tq
    q = q_ref[0, 0].reshape(rows, HEAD_DIM)
    m_ref[...] = jnp.full(m_ref.shape, NEG, F32)
    acc_ref[...] = jnp.zeros(acc_ref.shape, F32)
    c_diag = q0 // tk
    c_lo = jnp.maximum(q0 - WINDOW, 0) // tk if kind == "nsa_win" else 0
    if kind == "nsa_sel":
        selb = extra[0][0, 0].astype(BF16)
    if kind == "moba":
        selb = extra[0][0, 0].reshape(rows, extra[0].shape[-1]).astype(BF16)

    def scores(c):
        return _dot_t(q, k_ref[0, g, pl.ds(pl.multiple_of(c * tk, tk), tk), :])

    def chunk(c, diag):
        if kind in ("nsa_sel", "moba"):
            hit = _dot(selb, extra[1][c]) > 0.5
        s3 = s_ref[c % 2].reshape(GROUP, tq, tk)
        if not diag:
            s_ref[(c + 1) % 2] = scores(c + 1)
        start = pl.multiple_of(c * tk, tk)
        v = v_ref[0, g, pl.ds(start, tk), :]
        ok = None
        if diag or kind == "nsa_win":
            t_idx = q0 + lax.broadcasted_iota(I32, (GROUP, tq, tk), 1)
            s_idx = start + lax.broadcasted_iota(I32, (GROUP, tq, tk), 2)
            causal = s_idx <= t_idx
        if kind == "fox":
            ck = extra[0][0, 0, :, pl.ds(start, tk)]
            s3 = s3 - ck[:, None, :]
            ok = causal if diag else None
        elif kind == "dsa":
            msk = extra[0][0, :, pl.ds(start, tk)]
            ok = jnp.broadcast_to((msk > 0)[None], (GROUP, tq, tk))
        elif kind == "nsa_sel":
            ok = jnp.broadcast_to(hit[None], (GROUP, tq, tk))
            ok = (ok & causal) if diag else ok
        elif kind == "nsa_win":
            ok = causal & (s_idx >= t_idx - WINDOW)
        else:
            ok = hit.reshape(GROUP, tq, tk)
            if diag:
                ok = ok | (causal & ((s_idx // MOBA_BLOCK) == (t_idx // MOBA_BLOCK)))
        if ok is not None:
            s3 = jnp.where(ok, s3, NEG)
        s = s3.reshape(rows, tk)
        m_prev = m_ref[...]
        m_new = jnp.maximum(m_prev, jnp.max(s, axis=1, keepdims=True))
        alpha = jnp.exp2(m_prev - m_new)
        p = jnp.exp2(s - pltpu.repeat(m_new, tk // LANES, axis=1))
        acc_ref[...] = acc_ref[...] * alpha + _dot(p.astype(BF16), v)
        m_ref[...] = m_new

    def body(c, carry):
        chunk(c, False)
        return carry

    s_ref[c_lo % 2] = scores(c_lo)
    lax.fori_loop(c_lo, c_diag, body, 0)
    chunk(c_diag, True)
    acc = acc_ref[...]
    o = acc[:, :HEAD_DIM] / jnp.maximum(acc[:, HEAD_DIM:HEAD_DIM + 1], 1e-30)
    o_ref[0, 0] = o.reshape(GROUP, tq, HEAD_DIM)


def _flash_prompt(kind, q, k, v, extra, tq=128, tk=512):
    bsz, _, _, t, _ = q.shape
    tk = min(tk, t)
    tq = min(tq, t)
    qspec = pl.BlockSpec((1, 1, GROUP, tq, HEAD_DIM), lambda b, g, i: (b, g, 0, i, 0))
    kvspec = pl.BlockSpec((1, N_KV_HEADS, t, HEAD_DIM), lambda b, g, i: (b, 0, 0, 0))
    vspec = pl.BlockSpec((1, N_KV_HEADS, t, LANES), lambda b, g, i: (b, 0, 0, 0))
    if kind == "fox":
        especs = [pl.BlockSpec((1, 1, GROUP, t), lambda b, g, i: (b, g, 0, 0))]
    elif kind == "dsa":
        especs = [pl.BlockSpec((1, tq, t), lambda b, g, i: (b, i, 0))]
    elif kind == "nsa_sel":
        especs = [pl.BlockSpec((1, 1, tq, extra[0].shape[-1]), lambda b, g, i: (b, g, i, 0)),
                  pl.BlockSpec(extra[1].shape, lambda b, g, i: (0, 0, 0))]
    elif kind == "moba":
        especs = [pl.BlockSpec((1, 1, GROUP, tq, extra[0].shape[-1]), lambda b, g, i: (b, g, 0, i, 0)),
                  pl.BlockSpec(extra[1].shape, lambda b, g, i: (0, 0, 0))]
    else:
        especs = []
    rows = GROUP * tq
    return pl.pallas_call(
        functools.partial(_flash_prompt_kernel, kind=kind, tq=tq, tk=tk, n_extra=len(extra)),
        grid=(bsz, N_KV_HEADS, t // tq),
        in_specs=[qspec, kvspec, vspec] + especs,
        out_specs=qspec,
        out_shape=jax.ShapeDtypeStruct(q.shape, F32),
        scratch_shapes=[pltpu.VMEM((rows, LANES), F32), pltpu.VMEM((rows, LANES), F32),
                        pltpu.VMEM((2, rows, tk), F32)],
        compiler_params=_params("parallel", "parallel", "parallel"),
        name="flash_prompt_" + kind,
    )(q, k, v, *extra)


def _expand_matrix(n_blocks_padded, block, t, tk):
    s = np.arange(t)
    e = (s[None, :] // block == np.arange(n_blocks_padded)[:, None]).astype(np.float32)
    e = e.reshape(n_blocks_padded, t // tk, tk).transpose(1, 0, 2)
    return jnp.asarray(e, dtype=BF16)


def _dsa_select_prompt_kernel(qi_ref, ki_ref, wi_ref, mask_ref, u_ref, *, tq, t, tk, topk):
    q0 = pl.program_id(1) * tq
    n_act = q0 // tk + 1
    sub = tk // LANES
    w = wi_ref[0]
    t_idx = q0 + lax.broadcasted_iota(I32, (tq, tk), 0)
    k_off = lax.broadcasted_iota(I32, (tq, tk), 1)

    def chunk_at(c):
        return pl.ds(pl.multiple_of(c * tk, tk), tk)

    def fill(c, carry):
        kc = ki_ref[0, chunk_at(c), :]
        s = jnp.zeros((tq, tk), F32)
        for h in range(IDX_HEADS):
            s = s + jnp.maximum(_dot_t(qi_ref[0, h], kc), 0.0) * w[:, h:h + 1]
        s = jnp.where(c * tk + k_off <= t_idx, s, -jnp.inf)
        u_ref[:, chunk_at(c)] = _sortable(s)
        return carry

    lax.fori_loop(0, n_act, fill, 0)

    def count(pred):
        def body(c, acc):
            hit = pred(u_ref[:, chunk_at(c)]).astype(I32)
            for i in range(sub):
                acc = acc + hit[:, i * LANES:(i + 1) * LANES]
            return acc
        acc = lax.fori_loop(0, n_act, body, jnp.zeros((tq, LANES), I32))
        return jnp.sum(acc, axis=1, keepdims=True)

    base = jnp.where(count(lambda u: u >= 0) >= topk, I32(0), I32(INT_MIN))

    def radix(i, base):
        cand = base | jnp.left_shift(I32(1), 30 - i)
        return jnp.where(count(lambda u: u >= cand) >= topk, cand, base)

    thr = lax.fori_loop(0, 31, radix, base)
    need = (topk - count(lambda u: u > thr)).astype(F32)
    r_i = lax.broadcasted_iota(I32, (LANES, LANES), 0)
    c_i = lax.broadcasted_iota(I32, (LANES, LANES), 1)
    tri = jnp.where(r_i <= c_i, 1.0, 0.0).astype(BF16)

    def emit(c, carry):
        for i in range(sub):
            at = pl.ds(pl.multiple_of(c * tk + i * LANES, LANES), LANES)
            u = u_ref[:, at]
            tie = u == thr
            tie_f = jnp.where(tie, 1.0, 0.0)
            inc = _dot(tie_f.astype(BF16), tri)
            sel = (u > thr) | (tie & (carry + inc - tie_f < need))
            keep = sel & (u > KEY_NEG_INF) & (u < KEY_POS_INF)
            mask_ref[0, :, at] = jnp.where(keep, 1.0, 0.0).astype(BF16)
            carry = carry + inc[:, LANES - 1:LANES]
        return carry

    lax.fori_loop(0, n_act, emit, jnp.zeros((tq, 1), F32))

    def clear(c, carry):
        mask_ref[0, :, chunk_at(c)] = jnp.zeros((tq, tk), BF16)
        return carry

    lax.fori_loop(n_act, t // tk, clear, 0)


def _dsa_select_prompt(qi, ki, wi, tq=128, tk=512):
    bsz, _, t, _ = qi.shape
    tq, tk = min(tq, t), min(tk, t)
    topk = min(IDX_TOPK, t // 4)
    return pl.pallas_call(
        functools.partial(_dsa_select_prompt_kernel, tq=tq, t=t, tk=tk, topk=topk),
        grid=(bsz, t // tq),
        in_specs=[pl.BlockSpec((1, IDX_HEADS, tq, IDX_DIM), lambda b, i: (b, 0, i, 0)),
                  pl.BlockSpec((1, t, IDX_DIM), lambda b, i: (b, 0, 0)),
                  pl.BlockSpec((1, tq, IDX_HEADS), lambda b, i: (b, i, 0))],
        out_specs=pl.BlockSpec((1, tq, t), lambda b, i: (b, i, 0)),
        out_shape=jax.ShapeDtypeStruct((bsz, t, t), BF16),
        scratch_shapes=[pltpu.VMEM((tq, t), I32)],
        compiler_params=_params("parallel", "parallel"),
        name="dsa_select_prompt",
    )(qi, ki, wi)


def _dsa_select_sample_kernel(pt_ref, qi_ref, wi_ref, *refs, n_pg, n_steps, tpad, n_new, topk):
    pages = refs[:n_pg]
    new_ref, mask_ref, u_ref = refs[n_pg:]
    j = pl.program_id(1)
    q = qi_ref[0]
    w = wi_ref[0]

    def scores(kc):
        rel = jnp.maximum(_dot(q, kc.astype(BF16)), 0.0) * w
        return rel.reshape(tpad, IDX_HEADS, LANES).sum(axis=1)

    @pl.when(j < n_steps - 1)
    def _():
        for i in range(n_pg):
            start = pl.multiple_of((j * n_pg + i) * LANES, LANES)
            u_ref[:, pl.ds(start, LANES)] = _sortable(scores(pages[i][0]))

    @pl.when(j == n_steps - 1)
    def _():
        s = scores(new_ref[0])
        t_idx = lax.broadcasted_iota(I32, (tpad, LANES), 0)
        c_idx = lax.broadcasted_iota(I32, (tpad, LANES), 1)
        s = jnp.where((c_idx <= t_idx) & (c_idx < n_new), s, -jnp.inf)
        base = (n_steps - 1) * n_pg * LANES
        u_ref[:, base:base + LANES] = _sortable(s)
        for i in range(1, n_pg):
            u_ref[:, base + i * LANES:base + (i + 1) * LANES] = jnp.full((tpad, LANES), KEY_NEG_INF, I32)

        def write(ci, sel, u):
            keep = sel & (u > KEY_NEG_INF) & (u < KEY_POS_INF)
            mask_ref[0, :, ci * LANES:(ci + 1) * LANES] = jnp.where(keep, 1.0, 0.0)

        _emit_selection(u_ref, topk, write)


def _dsa_select_sample(page_table, qi, wi, pool_kidx, new_ki, n_new, n_pg=8):
    bsz, n_pages = page_table.shape
    tpad = qi.shape[1] // IDX_HEADS
    n_steps = n_pages // n_pg + 1
    width = n_steps * n_pg * LANES
    topk = min(IDX_TOPK, (n_pages * PAGE_SIZE + n_new) // 4)
    fix = lambda b, j, pt: (b, 0, 0)
    grid_spec = pltpu.PrefetchScalarGridSpec(
        num_scalar_prefetch=1,
        grid=(bsz, n_steps),
        in_specs=[pl.BlockSpec((1, tpad * IDX_HEADS, IDX_DIM), fix), pl.BlockSpec((1, tpad * IDX_HEADS, 1), fix)]
        + [pl.BlockSpec((1, IDX_DIM, PAGE_SIZE), _page_map(i, n_pg, n_pages)) for i in range(n_pg)]
        + [pl.BlockSpec((1, IDX_DIM, PAGE_SIZE), fix)],
        out_specs=pl.BlockSpec((1, tpad, width), fix),
        scratch_shapes=[pltpu.VMEM((tpad, width), I32)],
    )
    return pl.pallas_call(
        functools.partial(_dsa_select_sample_kernel, n_pg=n_pg, n_steps=n_steps, tpad=tpad, n_new=n_new, topk=topk),
        grid_spec=grid_spec,
        out_shape=jax.ShapeDtypeStruct((bsz, tpad, width), F32),
        compiler_params=_params("parallel", "arbitrary"),
        name="dsa_select_sample",
    )(page_table, qi, wi, *([pool_kidx] * n_pg), new_ki)


def _page_map(i, n_pg, n_pages):
    return lambda b, j, pt: (pt[b, jnp.minimum(j * n_pg + i, n_pages - 1)], 0, 0)


def _page_map4(i, n_pg, n_pages):
    return lambda b, j, pt: (pt[b, jnp.minimum(j * n_pg + i, n_pages - 1)], 0, 0, 0)


def _paged_flash_kernel(pt_ref, q_ref, *refs, n_pg, n_steps):
    pages = refs[:n_pg]
    new_ref, bias_ref, o_ref, m_ref, l_ref, acc_ref = refs[n_pg:]
    j = pl.program_id(1)
    q = q_ref[0]

    @pl.when(j == 0)
    def _():
        m_ref[...] = jnp.full(m_ref.shape, NEG, F32)
        l_ref[...] = jnp.zeros(l_ref.shape, F32)
        acc_ref[...] = jnp.zeros(acc_ref.shape, F32)

    def update(page_refs, n):
        s = [_dot(q, page_refs[i][0, 0].astype(BF16)) + bias_ref[0, :, i * LANES:(i + 1) * LANES] for i in range(n)]
        m_prev = m_ref[...]
        m_new = m_prev
        for si in s:
            m_new = jnp.maximum(m_new, jnp.max(si, axis=1, keepdims=True))
        alpha = jnp.exp(m_prev - m_new)
        l_new = alpha * l_ref[...]
        acc = acc_ref[...] * pltpu.repeat(alpha, KV_COLS // LANES, axis=1)
        for i, si in enumerate(s):
            p = jnp.where(si > 0.5 * NEG, jnp.exp(si - m_new), 0.0)
            l_new = l_new + jnp.sum(p, axis=1, keepdims=True)
            acc = acc + _dot_t(p.astype(BF16), page_refs[i][0, 1].astype(BF16))
        l_ref[...] = l_new
        acc_ref[...] = acc
        m_ref[...] = m_new

    @pl.when(j < n_steps - 1)
    def _():
        update(pages, n_pg)

    @pl.when(j == n_steps - 1)
    def _():
        update([new_ref], 1)
        o_ref[0] = acc_ref[...] / jnp.maximum(pltpu.repeat(l_ref[...], KV_COLS // LANES, axis=1), 1e-30)


def _paged_flash(page_table, q_bd, pool_t, new_page_t, bias, n_pg):
    bsz, n_pages = page_table.shape
    n_steps = n_pages // n_pg + 1
    rows = q_bd.shape[1]
    fix = lambda b, j, pt: (b, 0, 0)
    fix4 = lambda b, j, pt: (b, 0, 0, 0)
    page_block = (1, 2, KV_COLS, PAGE_SIZE)
    grid_spec = pltpu.PrefetchScalarGridSpec(
        num_scalar_prefetch=1,
        grid=(bsz, n_steps),
        in_specs=[pl.BlockSpec((1, rows, KV_COLS), fix)]
        + [pl.BlockSpec(page_block, _page_map4(i, n_pg, n_pages)) for i in range(n_pg)]
        + [pl.BlockSpec(page_block, fix4),
           pl.BlockSpec((1, rows, n_pg * LANES), lambda b, j, pt: (b, 0, j))],
        out_specs=pl.BlockSpec((1, rows, KV_COLS), fix),
        scratch_shapes=[pltpu.VMEM((rows, LANES), F32), pltpu.VMEM((rows, LANES), F32),
                        pltpu.VMEM((rows, KV_COLS), F32)],
    )
    return pl.pallas_call(
        functools.partial(_paged_flash_kernel, n_pg=n_pg, n_steps=n_steps),
        grid_spec=grid_spec,
        out_shape=jax.ShapeDtypeStruct((bsz, rows, KV_COLS), F32),
        compiler_params=_params("parallel", "arbitrary"),
        name="paged_flash",
    )(page_table, q_bd, *([pool_t] * n_pg), new_page_t, bias)


def _cumsum_kernel(pt_ref, *refs, n_pg, n_steps):
    pages = refs[:n_pg]
    new_ref, o_ref, carry_ref = refs[n_pg:]
    j = pl.program_id(1)
    r_i = lax.broadcasted_iota(I32, (LANES, LANES), 0)
    c_i = lax.broadcasted_iota(I32, (LANES, LANES), 1)
    tri = jnp.where(r_i <= c_i, 1.0, 0.0).astype(BF16)

    @pl.when(j == 0)
    def _():
        carry_ref[...] = jnp.zeros(carry_ref.shape, F32)

    def step(x, i):
        c = _dot_hp(x, tri) + carry_ref[...]
        o_ref[0, :, i * LANES:(i + 1) * LANES] = c
        carry_ref[...] = jnp.broadcast_to(c[:, LANES - 1:LANES], c.shape)

    @pl.when(j < n_steps - 1)
    def _():
        for i in range(n_pg):
            step(pages[i][0], i)

    @pl.when(j == n_steps - 1)
    def _():
        step(new_ref[0], 0)
        for i in range(1, n_pg):
            o_ref[0, :, i * LANES:(i + 1) * LANES] = jnp.zeros((N_HEADS, LANES), F32)


def _paged_cumsum(page_table, pool_t, new_t, n_pg):
    bsz, n_pages = page_table.shape
    n_steps = n_pages // n_pg + 1
    fix = lambda b, j, pt: (b, 0, 0)
    grid_spec = pltpu.PrefetchScalarGridSpec(
        num_scalar_prefetch=1,
        grid=(bsz, n_steps),
        in_specs=[pl.BlockSpec((1, N_HEADS, LANES), _page_map(i, n_pg, n_pages)) for i in range(n_pg)]
        + [pl.BlockSpec((1, N_HEADS, LANES), fix)],
        out_specs=pl.BlockSpec((1, N_HEADS, n_pg * LANES), lambda b, j, pt: (b, 0, j)),
        scratch_shapes=[pltpu.VMEM((N_HEADS, LANES), F32)],
    )
    return pl.pallas_call(
        functools.partial(_cumsum_kernel, n_pg=n_pg, n_steps=n_steps),
        grid_spec=grid_spec,
        out_shape=jax.ShapeDtypeStruct((bsz, N_HEADS, n_steps * n_pg * LANES), F32),
        compiler_params=_params("parallel", "arbitrary"),
        name="fox_cumsum",
    )(page_table, *([pool_t] * n_pg), new_t)


def _gelu_tanh(x):
    return 0.5 * x * (1.0 + jnp.tanh(0.7978845608028654 * (x + 0.044715 * x * x * x)))


def _nsa_compress_kernel(pt_ref, *refs, n_pg, n_steps, nc):
    pages = refs[:n_pg]
    pe_ref, w1_ref, w2_ref, o_ref, x_ref = refs[n_pg:]
    c = pl.program_id(1) // N_KV_HEADS
    j = pl.program_id(2)
    per = PAGE_SIZE // CMP_STRIDE
    for i in range(n_pg):
        start = pl.multiple_of((j * n_pg + i) * per, per)
        x_ref[pl.ds(start, per), :] = pages[i][0, 0].astype(F32)

    @pl.when(j == n_steps - 1)
    def _():
        n_chunk = x_ref.shape[0]
        half = CMP_STRIDE * HEAD_DIM
        w1 = w1_ref[c]
        part = _dot(x_ref[...].astype(BF16), w1)
        pe = pe_ref[c]
        pe_term = _dot(pe[:, :half], w1)[:, :CMP_HID] + _dot(pe[:, half:], w1)[:, CMP_HID:]
        h = pe_term[0:1, :] + part[:, :CMP_HID] + pltpu.roll(part[:, CMP_HID:], n_chunk - 1, 0)
        out = _dot(_gelu_tanh(h).astype(BF16), w2_ref[c])
        row = lax.broadcasted_iota(I32, out.shape, 0)
        o_ref[0, 0] = jnp.where(row < nc, out, 0.0)


def _nsa_compress(page_table, pool_t, pe8, w1cat, w2, n_pg=8):
    bsz, n_pages = page_table.shape
    per = PAGE_SIZE // CMP_STRIDE
    n_chunk = n_pages * per
    nc = n_chunk - CMP_LEN // CMP_STRIDE + 1
    n_steps = n_pages // n_pg
    width = CMP_STRIDE * HEAD_DIM

    def page_map(i):
        return lambda b, ck, j, pt: (pt[b, j * n_pg + i], ck, 0, 0)

    fix3 = lambda b, ck, j, pt: (0, 0, 0)
    grid_spec = pltpu.PrefetchScalarGridSpec(
        num_scalar_prefetch=1,
        grid=(bsz, 2 * N_KV_HEADS, n_steps),
        in_specs=[pl.BlockSpec((1, 1, per, width), page_map(i)) for i in range(n_pg)]
        + [pl.BlockSpec(pe8.shape, fix3), pl.BlockSpec(w1cat.shape, fix3), pl.BlockSpec(w2.shape, fix3)],
        out_specs=pl.BlockSpec((1, 1, n_chunk, HEAD_DIM), lambda b, ck, j, pt: (b, ck, 0, 0)),
        scratch_shapes=[pltpu.VMEM((n_chunk, width), F32)],
    )
    return pl.pallas_call(
        functools.partial(_nsa_compress_kernel, n_pg=n_pg, n_steps=n_steps, nc=nc),
        grid_spec=grid_spec,
        out_shape=jax.ShapeDtypeStruct((bsz, 2 * N_KV_HEADS, n_chunk, HEAD_DIM), F32),
        compiler_params=_params("parallel", "parallel", "arbitrary"),
        name="nsa_compress",
    )(page_table, *([pool_t] * n_pg), pe8, w1cat, w2), nc


def _nsa_cmp_kernel(q_ref, ck_ref, cv_ref, cover_ref, o_ref, sel_ref, u_ref, *, tq, pos0, nc, n_sel):
    t0 = pos0 + pl.program_id(1) * tq
    ncp = ck_ref.shape[2]
    nsp = cover_ref.shape[1]
    rows = GROUP * tq
    n_idx = lax.broadcasted_iota(I32, (tq, ncp), 1)
    t_idx = t0 + lax.broadcasted_iota(I32, (tq, ncp), 0)
    c_ok = ((n_idx * CMP_STRIDE + CMP_LEN - 1 <= t_idx) & (n_idx < nc))[None]
    blk = lax.broadcasted_iota(I32, (tq, nsp), 1)
    cur = (t0 + lax.broadcasted_iota(I32, (tq, nsp), 0)) // SEL_BLOCK
    forced = (blk == 0) | (blk == cur) | (blk == cur - 1)
    for g in range(N_KV_HEADS):
        q = q_ref[0, g].reshape(rows, HEAD_DIM)
        s3 = jnp.where(c_ok, _dot_t(q, ck_ref[0, g]).reshape(GROUP, tq, ncp), NEG)
        m = jnp.max(s3, axis=-1, keepdims=True)
        e = jnp.where(c_ok, jnp.exp(s3 - m), 0.0)
        p = e / jnp.maximum(jnp.sum(e, axis=-1, keepdims=True), 1e-30)
        o = _dot(p.reshape(rows, ncp).astype(BF16), cv_ref[0, g])
        o_ref[0, g] = o.reshape(GROUP, tq, HEAD_DIM)
        imp = _dot_hp(p[0] + p[1] + p[2] + p[3], cover_ref[...])
        imp = jnp.where(forced, jnp.inf, imp)
        imp = jnp.where(blk <= cur, imp, -jnp.inf)
        u_ref[g * tq:(g + 1) * tq, :] = _sortable(imp)

    def write(ci, sel, u):
        keep = jnp.where(sel & (u > KEY_NEG_INF), 1.0, 0.0)
        sel_ref[0, :, :, ci * LANES:(ci + 1) * LANES] = keep.reshape(N_KV_HEADS, tq, LANES)

    _emit_selection(u_ref, n_sel, write)


def _nsa_cmp_select(q, cmp_k, cmp_v, pos0, nc, n_keys, tq):
    bsz, _, _, t, _ = q.shape
    ncp = cmp_k.shape[2]
    ns = -(-n_keys // SEL_BLOCK)
    nsp = -(-ns // LANES) * LANES
    n_sel = min(SEL_TOPN, ns)
    c0 = np.arange(ncp)[:, None] * CMP_STRIDE
    s0 = np.arange(nsp)[None, :] * SEL_BLOCK
    cover = (c0 <= s0 + SEL_BLOCK - 1) & (c0 + CMP_LEN - 1 >= s0) & (np.arange(ncp)[:, None] < nc) & (np.arange(nsp)[None, :] < ns)
    cover = jnp.asarray(cover.astype(np.float32), dtype=BF16)
    qspec = pl.BlockSpec((1, N_KV_HEADS, GROUP, tq, HEAD_DIM), lambda b, i: (b, 0, 0, i, 0))
    cspec = pl.BlockSpec((1, N_KV_HEADS, ncp, HEAD_DIM), lambda b, i: (b, 0, 0, 0))
    return pl.pallas_call(
        functools.partial(_nsa_cmp_kernel, tq=tq, pos0=pos0, nc=nc, n_sel=n_sel),
        grid=(bsz, t // tq),
        in_specs=[qspec, cspec, cspec, pl.BlockSpec((ncp, nsp), lambda b, i: (0, 0))],
        out_specs=[qspec, pl.BlockSpec((1, N_KV_HEADS, tq, nsp), lambda b, i: (b, 0, i, 0))],
        out_shape=[jax.ShapeDtypeStruct(q.shape, F32), jax.ShapeDtypeStruct((bsz, N_KV_HEADS, t, nsp), F32)],
        scratch_shapes=[pltpu.VMEM((N_KV_HEADS * tq, nsp), I32)],
        compiler_params=_params("parallel", "parallel"),
        name="nsa_cmp_select",
    )(q, cmp_k, cmp_v, cover)


def _kmean_kernel(*refs):
    o_ref = refs[-1]
    tot = jnp.sum(refs[0][0], axis=0, keepdims=True)
    for r in refs[1:-1]:
        tot = tot + jnp.sum(r[0], axis=0, keepdims=True)
    o_ref[0, 0] = tot * (1.0 / MOBA_BLOCK)


def _kmean_prompt(kv):
    bsz, t, _ = kv.shape
    nb = t // MOBA_BLOCK
    return pl.pallas_call(
        _kmean_kernel,
        grid=(bsz, nb),
        in_specs=[pl.BlockSpec((1, MOBA_BLOCK, KV_COLS), lambda b, i: (b, i, 0))],
        out_specs=pl.BlockSpec((1, 1, 1, KV_COLS), lambda b, i: (b, i, 0, 0)),
        out_shape=jax.ShapeDtypeStruct((bsz, nb, 1, KV_COLS), F32),
        compiler_params=_params("parallel", "parallel"),
        name="kmean_prompt",
    )(kv)


def _kmean_sample_kernel(pt_ref, *refs, per):
    o_ref = refs[-1]
    ones = jnp.ones((8, PAGE_SIZE), BF16)
    for n in range(len(refs[:-1]) // per):
        tot = jnp.zeros((8, KV_COLS), F32)
        for r in refs[n * per:(n + 1) * per]:
            x = r[0, 0]
            hi = x.astype(BF16)
            r1 = x - hi.astype(F32)
            mid = r1.astype(BF16)
            lo = (r1 - mid.astype(F32)).astype(BF16)
            tot = tot + _dot_t(ones, hi) + _dot_t(ones, mid) + _dot_t(ones, lo)
        o_ref[0, n] = tot[0:1] * (1.0 / MOBA_BLOCK)


def _kmean_sample(page_table, pool_t, n_pg):
    bsz, n_pages = page_table.shape
    per = MOBA_BLOCK // PAGE_SIZE
    nb = n_pages // per
    grid_spec = pltpu.PrefetchScalarGridSpec(
        num_scalar_prefetch=1,
        grid=(bsz, n_pages // n_pg),
        in_specs=[pl.BlockSpec((1, 1, KV_COLS, PAGE_SIZE), (lambda i: (lambda b, n, pt: (pt[b, n * n_pg + i], 0, 0, 0)))(i))
                  for i in range(n_pg)],
        out_specs=pl.BlockSpec((1, n_pg // per, 1, KV_COLS), lambda b, n, pt: (b, n, 0, 0)),
    )
    return pl.pallas_call(
        functools.partial(_kmean_sample_kernel, per=per),
        grid_spec=grid_spec,
        out_shape=jax.ShapeDtypeStruct((bsz, nb, 1, KV_COLS), F32),
        compiler_params=_params("parallel", "parallel"),
        name="kmean_sample",
    )(page_table, *([pool_t] * n_pg))


def _moba_select_kernel(q_ref, km_ref, sel_ref, *, tq, pos0, k_top):
    t0 = pos0 + pl.program_id(1) * tq
    nbp = km_ref.shape[2]
    rows = GROUP * tq
    blk = lax.broadcasted_iota(I32, (nbp, rows), 0)
    n_past = (t0 + (lax.broadcasted_iota(I32, (nbp, rows), 1) & (tq - 1))) // MOBA_BLOCK
    for g in range(N_KV_HEADS):
        q = q_ref[0, g].reshape(rows, HEAD_DIM)
        s = jnp.where(blk < n_past, _dot_t(km_ref[0, g], q), -jnp.inf)
        sel = jnp.zeros((nbp, rows), F32)
        for _ in range(k_top):
            m = jnp.max(s, axis=0, keepdims=True)
            first = jnp.min(jnp.where(s == m, blk, nbp), axis=0, keepdims=True)
            pick = blk == first
            sel = jnp.where(pick & (m > -jnp.inf), 1.0, sel)
            s = jnp.where(pick, -jnp.inf, s)
        sel_ref[0, 0, g] = sel


def _moba_select(q, kmean, pos0, nb, tq):
    bsz, _, _, t, _ = q.shape
    nbp = kmean.shape[2]
    nq = t // tq
    qspec = pl.BlockSpec((1, N_KV_HEADS, GROUP, tq, HEAD_DIM), lambda b, i: (b, 0, 0, i, 0))
    sel = pl.pallas_call(
        functools.partial(_moba_select_kernel, tq=tq, pos0=pos0, k_top=min(MOBA_TOPK, nb)),
        grid=(bsz, nq),
        in_specs=[qspec, pl.BlockSpec((1, N_KV_HEADS, nbp, HEAD_DIM), lambda b, i: (b, 0, 0, 0))],
        out_specs=pl.BlockSpec((1, 1, N_KV_HEADS, nbp, GROUP * tq), lambda b, i: (b, i, 0, 0, 0)),
        out_shape=jax.ShapeDtypeStruct((bsz, nq, N_KV_HEADS, nbp, GROUP * tq), F32),
        compiler_params=_params("parallel", "parallel"),
        name="moba_select",
    )(q, kmean)
    sel = sel.reshape(bsz, nq, N_KV_HEADS, nbp, GROUP, tq).transpose(0, 2, 4, 1, 5, 3)
    return sel.reshape(bsz, N_KV_HEADS, GROUP, t, nbp)


def _rope_tables(pos):
    half = HEAD_DIM // 2
    inv = ROPE_THETA ** (-jnp.arange(half, dtype=F32) / half)
    ang = pos.astype(F32)[:, None] * inv[None, :]
    return jnp.cos(ang)[:, None, :], jnp.sin(ang)[:, None, :]


def _rope(x, cs):
    cos, sin = cs
    half = x.shape[-1] // 2
    x1, x2 = x[..., :half], x[..., half:]
    return jnp.concatenate([x1 * cos - x2 * sin, x1 * sin + x2 * cos], axis=-1)


def _pad_to(x, axis, size):
    pad = [(0, 0)] * x.ndim
    pad[axis] = (0, size - x.shape[axis])
    return jnp.pad(x, pad)


def _q_groups(q, scale=ATTN_SCALE * LOG2E):
    b, t = q.shape[:2]
    return (q * scale).transpose(0, 2, 1, 3).reshape(b, N_KV_HEADS, GROUP, t, HEAD_DIM).astype(BF16)


def _kv_major(kv):
    b, t = kv.shape[:2]
    k = kv[:, :, 0].transpose(0, 2, 1, 3).astype(BF16)
    v = kv[:, :, 1].transpose(0, 2, 1, 3).astype(BF16)
    ones = jnp.ones((b, N_KV_HEADS, t, 1), BF16)
    zeros = jnp.zeros((b, N_KV_HEADS, t, LANES - HEAD_DIM - 1), BF16)
    return k, jnp.concatenate([v, ones, zeros], axis=-1)


def _pages_t(kv):
    return kv.transpose(0, 2, 3, 4, 1).reshape(kv.shape[0], 2, KV_COLS, PAGE_SIZE)


def _fit(x, width):
    return x[..., :width] if x.shape[-1] >= width else _pad_to(x, x.ndim - 1, width)


def _merge_heads(o):
    b, _, _, t, _ = o.shape
    return o.reshape(b, N_HEADS, t, HEAD_DIM).transpose(0, 2, 1, 3).reshape(b * t, Q_DIM)


_HEAD_TO_GROUP = np.equal(np.arange(N_HEADS)[:, None] // GROUP, np.arange(N_KV_HEADS)[None, :]).astype(np.float32)


def _q_block_diag(q):
    s, tn = q.shape[:2]
    qb = (q * ATTN_SCALE)[:, :, :, None, :] * _HEAD_TO_GROUP[None, None, :, :, None]
    return qb.reshape(s, tn * N_HEADS, KV_COLS).astype(BF16)


def _extract_block_diag(o, tn):
    s = o.shape[0]
    o5 = o.reshape(s, tn, N_HEADS, N_KV_HEADS, HEAD_DIM) * _HEAD_TO_GROUP[None, None, :, :, None]
    return o5.sum(axis=3).reshape(s * tn, Q_DIM)


def _new_page(kv_new):
    return _pages_t(_pad_to(kv_new, 1, PAGE_SIZE))


def _pick_pages(n_pages):
    for n in (8, 4, 2, 1):
        if n_pages % n == 0:
            return n


def _identity_pages(bsz, n_pages):
    return jnp.arange(bsz * n_pages, dtype=I32).reshape(bsz, n_pages)


def _split(proj, sizes):
    out, o = [], 0
    for s in sizes:
        out.append(proj[:, o:o + s])
        o += s
    return out


def _rows_th(ok, s, tn):
    return jnp.where(ok, 0.0, NEG).astype(F32).reshape(s, tn * N_HEADS, ok.shape[-1])


def _dsa_project(x, bsz, t, w_in, cs):
    q, k, v, qi, ki, wi = _split(_matmul(x, w_in), [Q_DIM, KV_COLS, KV_COLS, IDX_HEADS * IDX_DIM, IDX_DIM, IDX_HEADS])
    q = _rope(q.reshape(bsz, t, N_HEADS, HEAD_DIM), cs)
    k = _rope(k.reshape(bsz, t, N_KV_HEADS, HEAD_DIM), cs)
    kv = jnp.stack([k, v.reshape(bsz, t, N_KV_HEADS, HEAD_DIM)], axis=2)
    qi = _rope(qi.reshape(bsz, t, IDX_HEADS, IDX_DIM), cs)
    ki = _rope(ki.reshape(bsz, t, 1, IDX_DIM), cs)[:, :, 0]
    return q, qi, wi.reshape(bsz, t, IDX_HEADS) * IDX_SCALE, kv, ki


def _dsa_prompt(x, bsz, t, w_in, cs):
    q, qi, wi, kv, ki = _dsa_project(x, bsz, t, w_in, cs)
    mask = _dsa_select_prompt(qi.transpose(0, 2, 1, 3).astype(BF16), ki.astype(BF16), wi)
    kh, vh = _kv_major(kv)
    o = _flash_prompt("dsa", _q_groups(q), kh, vh, [mask])
    return _merge_heads(o), kv, ki


def _dsa_sample(x, s, tn, w_in, cs, page_table, cache_kv, cache_kidx, n_pg):
    q, qi, wi, kv, ki = _dsa_project(x, s, tn, w_in, cs)
    tpad = 8
    qi_p = _pad_to(qi, 1, tpad).reshape(s, tpad * IDX_HEADS, IDX_DIM).astype(BF16)
    wi_p = _pad_to(wi, 1, tpad).reshape(s, tpad * IDX_HEADS, 1)
    mask = _dsa_select_sample(page_table, qi_p, wi_p, cache_kidx.transpose(0, 2, 1),
                              _pad_to(ki, 1, PAGE_SIZE).transpose(0, 2, 1), tn, n_pg)
    ok = jnp.broadcast_to(mask[:, :tn, None, :] > 0.5, (s, tn, N_HEADS, mask.shape[-1]))
    o = _paged_flash(page_table, _q_block_diag(q), _pages_t(cache_kv), _new_page(kv), _rows_th(ok, s, tn), n_pg)
    return _extract_block_diag(o, tn), kv, ki


def _fox_project(x, bsz, t, w_in, b_f):
    q, k, v, f = _split(_matmul(x, w_in), [Q_DIM, KV_COLS, KV_COLS, N_HEADS])
    logf = jax.nn.log_sigmoid(f.reshape(bsz, t, N_HEADS) + b_f)
    kv = jnp.stack([k.reshape(bsz, t, N_KV_HEADS, HEAD_DIM), v.reshape(bsz, t, N_KV_HEADS, HEAD_DIM)], axis=2)
    return q.reshape(bsz, t, N_HEADS, HEAD_DIM), kv, logf


def _fox_prompt(x, bsz, t, w_in, b_f):
    q, kv, logf = _fox_project(x, bsz, t, w_in, b_f)
    n_pages = t // LANES
    pool_t = logf.reshape(bsz, n_pages, LANES, N_HEADS).transpose(0, 1, 3, 2).reshape(bsz * n_pages, N_HEADS, LANES)
    c = _paged_cumsum(_identity_pages(bsz, n_pages), pool_t, jnp.zeros((bsz, N_HEADS, LANES), F32), _pick_pages(n_pages))
    c = (c[:, :, :t] * LOG2E).reshape(bsz, N_KV_HEADS, GROUP, t)
    kh, vh = _kv_major(kv)
    o = _flash_prompt("fox", _q_groups(q), kh, vh, [c])
    return _merge_heads(o), kv, logf


def _fox_sample(x, s, tn, w_in, b_f, page_table, cache_kv, cache_logf, n_pg):
    q, kv, logf = _fox_project(x, s, tn, w_in, b_f)
    past = page_table.shape[1] * PAGE_SIZE
    c = _paged_cumsum(page_table, cache_logf.transpose(0, 2, 1), _pad_to(logf.transpose(0, 2, 1), 2, LANES), n_pg)
    col = jnp.arange(c.shape[-1])
    valid = (col[None, :] < past) | ((col[None, :] - past <= jnp.arange(tn)[:, None]) & (col[None, :] < past + tn))
    bias = jnp.where(valid[None, :, None, :], -c[:, None, :, :], NEG).reshape(s, tn * N_HEADS, c.shape[-1])
    o = _paged_flash(page_table, _q_block_diag(q), _pages_t(cache_kv), _new_page(kv), bias, n_pg)
    return _extract_block_diag(o, tn), kv, logf


def _nsa_project(x, bsz, t, w_in, b_gate, cs):
    q, kc, vc, ks, vs, kw, vw, g = _split(_matmul(x, w_in), [Q_DIM] + [KV_COLS] * 6 + [3 * N_HEADS])
    hd = lambda a: a.reshape(bsz, t, N_KV_HEADS, HEAD_DIM)
    q = q.reshape(bsz, t, N_HEADS, HEAD_DIM)
    kv_cmp = jnp.stack([hd(kc), hd(vc)], axis=2)
    kv_slc = jnp.stack([_rope(hd(ks), cs), hd(vs)], axis=2)
    kv_win = jnp.stack([_rope(hd(kw), cs), hd(vw)], axis=2)
    gate = jax.nn.sigmoid(g.reshape(bsz, t, 3 * N_HEADS) + b_gate).reshape(bsz, t, 3, N_HEADS)
    return q, _rope(q, cs), gate, kv_cmp, kv_slc, kv_win


def _nsa_weights(pe, w1, w2):
    r = CMP_LEN // CMP_STRIDE
    w1cat = w1.reshape(2, r, CMP_STRIDE * HEAD_DIM, CMP_HID).transpose(0, 2, 1, 3).reshape(2, CMP_STRIDE * HEAD_DIM, r * CMP_HID)
    pe8 = jnp.broadcast_to(pe.reshape(2, 1, CMP_LEN * HEAD_DIM), (2, 8, CMP_LEN * HEAD_DIM))
    return pe8.astype(BF16), w1cat.astype(BF16), w2.astype(BF16)


def _chunk_pages(kv):
    n = kv.shape[0]
    per = PAGE_SIZE // CMP_STRIDE
    x = kv.reshape(n, per, CMP_STRIDE, 2 * N_KV_HEADS, HEAD_DIM).transpose(0, 3, 1, 2, 4)
    return x.reshape(n, 2 * N_KV_HEADS, per, CMP_STRIDE * HEAD_DIM).astype(BF16)


def _gate_mix(gate, o_c, o_s, o_w):
    b, t = gate.shape[:2]
    g = gate.transpose(0, 2, 3, 1).reshape(b, 3, N_KV_HEADS, GROUP, t, 1)
    return g[:, 0] * o_c + g[:, 1] * o_s + g[:, 2] * o_w


def _nsa_prompt(x, bsz, t, w_in, b_gate, cmp_w, cs, tq=128, tk=512):
    q, q_rot, gate, kv_cmp, kv_slc, kv_win = _nsa_project(x, bsz, t, w_in, b_gate, cs)
    n_pages = t // PAGE_SIZE
    cmp, nc = _nsa_compress(_identity_pages(bsz, n_pages), _chunk_pages(kv_cmp.reshape(bsz * n_pages, PAGE_SIZE, 2, N_KV_HEADS, HEAD_DIM)),
                            *cmp_w, n_pg=_pick_pages(n_pages))
    cmp = cmp.astype(BF16)
    tq, tk = min(tq, t), min(tk, t)
    o_c, selblk = _nsa_cmp_select(_q_groups(q, ATTN_SCALE), cmp[:, :N_KV_HEADS], cmp[:, N_KV_HEADS:], 0, nc, t, tq)
    qg = _q_groups(q_rot)
    e3 = _expand_matrix(selblk.shape[-1], SEL_BLOCK, t, tk)
    o_s = _flash_prompt("nsa_sel", qg, *_kv_major(kv_slc), [selblk, e3], tq, tk)
    o_w = _flash_prompt("nsa_win", qg, *_kv_major(kv_win), [], tq, tk)
    o = _gate_mix(gate, o_c, o_s, o_w)
    return _merge_heads(o), kv_cmp, kv_slc, kv_win[:, -min(WINDOW, t):]


def _nsa_sample(x, s, tn, w_in, b_gate, cmp_w, cs, page_table, cache_cmp, cache_slc, state_win, n_pg):
    q, q_rot, gate, kv_cmp, kv_slc, kv_win = _nsa_project(x, s, tn, w_in, b_gate, cs)
    past = page_table.shape[1] * PAGE_SIZE
    tpad = 8
    cmp, nc = _nsa_compress(page_table, _chunk_pages(cache_cmp), *cmp_w, n_pg=n_pg)
    cmp = cmp.astype(BF16)
    o_c, selblk = _nsa_cmp_select(_q_groups(_pad_to(q, 1, tpad), ATTN_SCALE), cmp[:, :N_KV_HEADS], cmp[:, N_KV_HEADS:],
                                  past, nc, past + tn, tpad)
    o_c = o_c[:, :, :, :tn]
    qbd = _q_block_diag(q_rot)
    width = (page_table.shape[1] // n_pg + 1) * n_pg * LANES
    col = jnp.arange(width)
    pos = past + jnp.arange(tn)
    sel_key = jnp.repeat(selblk[:, :, :tn, :-(-width // SEL_BLOCK)], SEL_BLOCK, axis=-1)[..., :width]
    ok = (sel_key > 0.5) & (col[None, :] <= pos[:, None])[None, None]
    ok = jnp.broadcast_to(ok.transpose(0, 2, 1, 3)[:, :, :, None, :], (s, tn, N_KV_HEADS, GROUP, width))
    o_s = _paged_flash(page_table, qbd, _pages_t(cache_slc), _new_page(kv_slc),
                       _rows_th(ok.reshape(s, tn, N_HEADS, width), s, tn), n_pg)
    win_buf = state_win.shape[1]
    n_wp = win_buf // PAGE_SIZE
    wcol = jnp.arange(2 * n_wp * LANES)
    win_pos = jnp.where(wcol < win_buf, past - win_buf + wcol, jnp.where(wcol < win_buf + tn, past + wcol - win_buf, -1))
    w_ok = (win_pos[None, :] <= pos[:, None]) & (win_pos[None, :] >= pos[:, None] - WINDOW) & (win_pos[None, :] >= 0)
    w_ok = jnp.broadcast_to(w_ok[None, :, None, :], (s, tn, N_HEADS, wcol.shape[0]))
    win_pages = _pages_t(state_win.reshape(s * n_wp, PAGE_SIZE, 2, N_KV_HEADS, HEAD_DIM))
    o_w = _paged_flash(_identity_pages(s, n_wp), qbd, win_pages, _new_page(kv_win), _rows_th(w_ok, s, tn), n_wp)
    unbd = lambda o: _extract_block_diag(o, tn).reshape(s, tn, N_KV_HEADS, GROUP, HEAD_DIM).transpose(0, 2, 3, 1, 4)
    o = _gate_mix(gate, o_c, unbd(o_s), unbd(o_w))
    win = jnp.concatenate([state_win, kv_win], axis=1)[:, -win_buf:]
    return _merge_heads(o), kv_cmp, kv_slc, win


def _moba_project(x, bsz, t, w_in, cs):
    q, k, v = _split(_matmul(x, w_in), [Q_DIM, KV_COLS, KV_COLS])
    q = _rope(q.reshape(bsz, t, N_HEADS, HEAD_DIM), cs)
    kv = jnp.stack([_rope(k.reshape(bsz, t, N_KV_HEADS, HEAD_DIM), cs), v.reshape(bsz, t, N_KV_HEADS, HEAD_DIM)], axis=2)
    return q, kv


def _kmean_heads(km):
    b, nb = km.shape[:2]
    return _pad_to(km.reshape(b, nb, N_KV_HEADS, HEAD_DIM).transpose(0, 2, 1, 3), 2, -(-nb // 16) * 16).astype(BF16)


def _moba_prompt(x, bsz, t, w_in, cs, tq=128, tk=512):
    q, kv = _moba_project(x, bsz, t, w_in, cs)
    tq, tk = min(tq, t), min(tk, t)
    nb = -(-t // MOBA_BLOCK)
    km = _kmean_heads(_kmean_prompt(kv.reshape(bsz, t, 2 * KV_COLS)))
    qg = _q_groups(q)
    sel = _moba_select(qg, km, 0, nb, tq)
    e3 = _expand_matrix(km.shape[2], MOBA_BLOCK, t, tk)
    o = _flash_prompt("moba", qg, *_kv_major(kv), [sel, e3], tq, tk)
    return _merge_heads(o), kv


def _moba_sample(x, s, tn, w_in, cs, page_table, cache_kv, n_pg):
    q, kv = _moba_project(x, s, tn, w_in, cs)
    past = page_table.shape[1] * PAGE_SIZE
    tpad = 8
    pool = _pages_t(cache_kv)
    nb = -(-(past + tn) // MOBA_BLOCK)
    km = _kmean_heads(_kmean_sample(page_table, pool, n_pg))
    sel = _moba_select(_q_groups(_pad_to(q, 1, tpad)), km, past, nb, tpad)
    width = (page_table.shape[1] // n_pg + 1) * n_pg * LANES
    col = jnp.arange(width)
    pos = past + jnp.arange(tn)
    sel_key = _fit(jnp.repeat(sel[:, :, :, :tn], MOBA_BLOCK, axis=-1), width)
    own = (col[None, :] // MOBA_BLOCK == pos[:, None] // MOBA_BLOCK) & (col[None, :] <= pos[:, None])
    ok = (sel_key > 0.5) | own[None, None, None]
    ok = ok.transpose(0, 3, 1, 2, 4).reshape(s, tn, N_HEADS, width)
    o = _paged_flash(page_table, _q_block_diag(q), pool, _new_page(kv), _rows_th(ok, s, tn), n_pg)
    return _extract_block_diag(o, tn), kv


def _cast_w(w):
    return _pad_to(w, 1, -(-w.shape[1] // LANES) * LANES).astype(BF16)


def kernel(x_prompt, x_sample, cache_a_kv, cache_a_kidx, cache_b_kv, cache_b_logf, cache_c_cmp_kv, cache_c_slc_kv, state_c_win_kv, cache_d_kv, page_table, a_w_in, a_w_out, b_w_in, b_b_f, b_w_out, c_w_in, c_b_gate, c_cmp_pe, c_cmp_w1, c_cmp_w2, c_w_out, d_w_in, d_w_out, ln_g, ln_b, ffn_w_gu, ffn_w_down):
    bsz, t, d = x_prompt.shape
    s, tn, _ = x_sample.shape
    n_pages = page_table.shape[1]
    past = n_pages * PAGE_SIZE
    n_pg = _pick_pages(n_pages)
    cs_p = _rope_tables(jnp.arange(t, dtype=I32))
    cs_s = _rope_tables(past + jnp.arange(tn, dtype=I32))
    xp = x_prompt.reshape(bsz * t, d)
    xs = x_sample.reshape(s * tn, d)
    cmp_w = _nsa_weights(c_cmp_pe, c_cmp_w1, c_cmp_w2)
    w_out = [_cast_w(w) for w in (a_w_out, b_w_out, c_w_out, d_w_out)]

    op, a_kv_p, a_kidx_p = _dsa_prompt(xp, bsz, t, _cast_w(a_w_in), cs_p)
    os_, a_kv_s, a_kidx_s = _dsa_sample(xs, s, tn, _cast_w(a_w_in), cs_s, page_table, cache_a_kv, cache_a_kidx, n_pg)

    def finish(i, xp, xs, op, os_):
        xp = _out_ln(op, w_out[i], xp, ln_g[i, 0], ln_b[i, 0])
        xs = _out_ln(os_, w_out[i], xs, ln_g[i, 0], ln_b[i, 0])
        wgu, wd = ffn_w_gu[i].astype(BF16), ffn_w_down[i].astype(BF16)
        xp = _ffn_ln(xp, wgu, wd, ln_g[i, 1], ln_b[i, 1])
        xs = _ffn_ln(xs, wgu, wd, ln_g[i, 1], ln_b[i, 1])
        return xp, xs

    xp, xs = finish(0, xp, xs, op, os_)

    op, b_kv_p, b_logf_p = _fox_prompt(xp, bsz, t, _cast_w(b_w_in), b_b_f)
    os_, b_kv_s, b_logf_s = _fox_sample(xs, s, tn, _cast_w(b_w_in), b_b_f, page_table, cache_b_kv, cache_b_logf, n_pg)
    xp, xs = finish(1, xp, xs, op, os_)

    op, c_cmp_kv_p, c_slc_kv_p, c_win_kv_p = _nsa_prompt(xp, bsz, t, _cast_w(c_w_in), c_b_gate, cmp_w, cs_p)
    os_, c_cmp_kv_s, c_slc_kv_s, c_win_kv_s = _nsa_sample(xs, s, tn, _cast_w(c_w_in), c_b_gate, cmp_w, cs_s, page_table,
                                                          cache_c_cmp_kv, cache_c_slc_kv, state_c_win_kv, n_pg)
    xp, xs = finish(2, xp, xs, op, os_)

    op, d_kv_p = _moba_prompt(xp, bsz, t, _cast_w(d_w_in), cs_p)
    os_, d_kv_s = _moba_sample(xs, s, tn, _cast_w(d_w_in), cs_s, page_table, cache_d_kv, n_pg)
    xp, xs = finish(3, xp, xs, op, os_)

    return (xp.reshape(bsz, t, d), xs.reshape(s, tn, d), a_kv_p, a_kv_s, a_kidx_p, a_kidx_s, b_kv_p, b_kv_s,
            b_logf_p, b_logf_s, c_cmp_kv_p, c_cmp_kv_s, c_slc_kv_p, c_slc_kv_s, c_win_kv_p, c_win_kv_s, d_kv_p, d_kv_s)
```

```python
import functools

import numpy as np
import jax
import jax.numpy as jnp
from jax import lax
from jax.experimental import pallas as pl
from jax.experimental.pallas import tpu as pltpu

F32 = jnp.float32
BF16 = jnp.bfloat16
I32 = jnp.int32

N_HEADS = 16
HEAD_DIM = 64
N_KV_HEADS = 4
GROUP = N_HEADS // N_KV_HEADS
Q_DIM = N_HEADS * HEAD_DIM
KV_COLS = N_KV_HEADS * HEAD_DIM
DEPTH = 4
PAGE_SIZE = 128
ROPE_THETA = 10000.0
LN_EPS = 1e-5
ALPHA = (2 * DEPTH) ** 0.25
ATTN_SCALE = HEAD_DIM ** -0.5
LOG2E = 1.4426950408889634
IDX_HEADS = 8
IDX_DIM = 64
IDX_TOPK = 256
IDX_SCALE = (IDX_HEADS * IDX_DIM) ** -0.5
CMP_LEN = 32
CMP_STRIDE = 16
CMP_HID = 2 * HEAD_DIM
SEL_BLOCK = 64
SEL_TOPN = 16
WINDOW = 512
MOBA_BLOCK = 256
MOBA_TOPK = 3

LANES = 128
VMEM_LIMIT = 56 * 2 ** 20
NEG = -1e30
KEY_NEG_INF = -2139095041
KEY_POS_INF = 2139095040
INT_MIN = -2 ** 31


def _params(*sem):
    return pltpu.CompilerParams(dimension_semantics=sem, vmem_limit_bytes=VMEM_LIMIT)


def _dot_t(a, b):
    return lax.dot_general(a, b, (((1,), (1,)), ((), ())), preferred_element_type=F32)


def _dot(a, b):
    return jnp.dot(a, b, preferred_element_type=F32)


def _dot_hp(a, b):
    hi = a.astype(BF16)
    r1 = a - hi.astype(F32)
    mid = r1.astype(BF16)
    lo = (r1 - mid.astype(F32)).astype(BF16)
    return _dot(hi, b) + _dot(mid, b) + _dot(lo, b)


def _mm_kernel(x_ref, w_ref, o_ref):
    o_ref[...] = _dot(x_ref[...].astype(BF16), w_ref[...])


def _matmul(x, w):
    m, k = x.shape
    n = w.shape[1]
    tm = min(m, 512)
    return pl.pallas_call(
        _mm_kernel,
        grid=(m // tm,),
        in_specs=[pl.BlockSpec((tm, k), lambda i: (i, 0)), pl.BlockSpec((k, n), lambda i: (0, 0))],
        out_specs=pl.BlockSpec((tm, n), lambda i: (i, 0)),
        out_shape=jax.ShapeDtypeStruct((m, n), F32),
        compiler_params=_params("parallel"),
        name="in_proj",
    )(x, w)


def _layer_norm(y, g, b):
    mu = jnp.mean(y, axis=-1, keepdims=True)
    d = y - mu
    var = jnp.mean(d * d, axis=-1, keepdims=True)
    return d * lax.rsqrt(var + LN_EPS) * g + b


def _out_ln_kernel(o_ref, w_ref, x_ref, g_ref, b_ref, y_ref):
    y = ALPHA * x_ref[...] + _dot(o_ref[...].astype(BF16), w_ref[...])
    y_ref[...] = _layer_norm(y, g_ref[...], b_ref[...])


def _out_ln(o, w, x, g, b):
    m, d = x.shape
    k = o.shape[1]
    tm = min(m, 512)
    row = lambda i: (i, 0)
    fix = lambda i: (0, 0)
    return pl.pallas_call(
        _out_ln_kernel,
        grid=(m // tm,),
        in_specs=[pl.BlockSpec((tm, k), row), pl.BlockSpec((k, d), fix), pl.BlockSpec((tm, d), row),
                  pl.BlockSpec((1, d), fix), pl.BlockSpec((1, d), fix)],
        out_specs=pl.BlockSpec((tm, d), row),
        out_shape=jax.ShapeDtypeStruct((m, d), F32),
        compiler_params=_params("parallel"),
        name="out_proj_ln",
    )(o, w, x, g.reshape(1, d), b.reshape(1, d))


def _ffn_ln_kernel(x_ref, wgu_ref, wd_ref, g_ref, b_ref, y_ref, *, d_ff, chunk):
    x = x_ref[...]
    xb = x.astype(BF16)
    acc = jnp.zeros(x.shape, F32)
    for c in range(d_ff // chunk):
        gate = _dot(xb, wgu_ref[:, c * chunk:(c + 1) * chunk])
        up = _dot(xb, wgu_ref[:, d_ff + c * chunk:d_ff + (c + 1) * chunk])
        h = gate * (1.0 / (1.0 + jnp.exp(-gate))) * up
        acc = acc + _dot(h.astype(BF16), wd_ref[c * chunk:(c + 1) * chunk, :])
    y_ref[...] = _layer_norm(ALPHA * x + acc, g_ref[...], b_ref[...])


def _ffn_ln(x, wgu, wd, g, b):
    m, d = x.shape
    d_ff = wd.shape[0]
    tm = min(m, 256)
    row = lambda i: (i, 0)
    fix = lambda i: (0, 0)
    return pl.pallas_call(
        functools.partial(_ffn_ln_kernel, d_ff=d_ff, chunk=256),
        grid=(m // tm,),
        in_specs=[pl.BlockSpec((tm, d), row), pl.BlockSpec((d, 2 * d_ff), fix), pl.BlockSpec((d_ff, d), fix),
                  pl.BlockSpec((1, d), fix), pl.BlockSpec((1, d), fix)],
        out_specs=pl.BlockSpec((tm, d), row),
        out_shape=jax.ShapeDtypeStruct((m, d), F32),
        compiler_params=_params("parallel"),
        name="ffn_ln",
    )(x, wgu, wd, g.reshape(1, d), b.reshape(1, d))


def _sortable(x):
    x = jnp.where(x == 0.0, 0.0, x)
    b = lax.bitcast_convert_type(x, I32)
    return b ^ ((b >> 31) & I32(0x7FFFFFFF))


def _kth_largest_key(u_ref, k):
    rows = u_ref.shape[0]

    def count_ge(cand):
        return jnp.sum((u_ref[...] >= cand).astype(I32), axis=1, keepdims=True)

    base = jnp.where(count_ge(jnp.zeros((rows, 1), I32)) >= k, I32(0), I32(INT_MIN))

    def body(i, base):
        cand = base | jnp.left_shift(I32(1), 30 - i)
        return jnp.where(count_ge(cand) >= k, cand, base)

    return lax.fori_loop(0, 31, body, base)


def _emit_selection(u_ref, k, write):
    rows, n = u_ref.shape
    thr = _kth_largest_key(u_ref, k)
    n_gt = jnp.sum((u_ref[...] > thr).astype(I32), axis=1, keepdims=True)
    need = (k - n_gt).astype(F32)
    r_i = lax.broadcasted_iota(I32, (LANES, LANES), 0)
    c_i = lax.broadcasted_iota(I32, (LANES, LANES), 1)
    tri = jnp.where(r_i <= c_i, 1.0, 0.0).astype(BF16)
    carry = jnp.zeros((rows, 1), F32)
    for ci in range(n // LANES):
        u = u_ref[:, ci * LANES:(ci + 1) * LANES]
        tie = u == thr
        tie_f = jnp.where(tie, 1.0, 0.0)
        inc = _dot(tie_f.astype(BF16), tri)
        rank = carry + inc - tie_f
        sel = (u > thr) | (tie & (rank < need))
        write(ci, sel, u)
        carry = carry + inc[:, LANES - 1:LANES]


def _topk_mask_cols(u_ref, k):
    n, r = u_ref.shape

    def count(pred):
        return jnp.sum(pred(u_ref[...]).astype(I32), axis=0, keepdims=True)

    base = jnp.where(count(lambda u: u >= 0) >= k, I32(0), I32(INT_MIN))

    def radix(i, base):
        cand = base | jnp.left_shift(I32(1), 30 - i)
        return jnp.where(count(lambda u: u >= cand) >= k, cand, base)

    thr = lax.fori_loop(0, 31, radix, base)
    need = (k - count(lambda u: u > thr)).astype(F32)
    u = u_ref[...]
    tie = u == thr
    lower = jnp.where(lax.broadcasted_iota(I32, (n, n), 0) > lax.broadcasted_iota(I32, (n, n), 1), 1.0, 0.0)
    rank = _dot(lower.astype(BF16), jnp.where(tie, 1.0, 0.0).astype(BF16))
    return (u > thr) | (tie & (rank < need)), u


def _flash_prompt_kernel(*refs, kind, tq, tk, n_extra):
    q_ref, k_ref, v_ref = refs[:3]
    extra = refs[3:3 + n_extra]
    o_ref, m_ref, acc_ref, s_ref = refs[3 + n_extra:]
    g = pl.program_id(1)
    q0 = pl.program_id(2) * tq
    rows = GROUP * tq
    q = q_ref[0, 0].reshape(rows, HEAD_DIM)
    m_ref[...] = jnp.full(m_ref.shape, NEG, F32)
    acc_ref[...] = jnp.zeros(acc_ref.shape, F32)
    c_diag = q0 // tk
    c_lo = jnp.maximum(q0 - WINDOW, 0) // tk if kind == "nsa_win" else 0
    if kind == "nsa_sel":
        selb = extra[0][0, 0].astype(BF16)
    if kind == "moba":
        selb = extra[0][0, 0].reshape(rows, extra[0].shape[-1]).astype(BF16)

    def scores(c):
        return _dot_t(q, k_ref[0, g, pl.ds(pl.multiple_of(c * tk, tk), tk), :])

    def chunk(c, diag):
        if kind in ("nsa_sel", "moba"):
            hit = _dot(selb, extra[1][c]) > 0.5
        s3 = s_ref[c % 2].reshape(GROUP, tq, tk)
        if not diag:
            s_ref[(c + 1) % 2] = scores(c + 1)
        start = pl.multiple_of(c * tk, tk)
        v = v_ref[0, g, pl.ds(start, tk), :]
        ok = None
        if diag or kind == "nsa_win":
            t_idx = q0 + lax.broadcasted_iota(I32, (GROUP, tq, tk), 1)
            s_idx = start + lax.broadcasted_iota(I32, (GROUP, tq, tk), 2)
            causal = s_idx <= t_idx
        if kind == "fox":
            ck = extra[0][0, 0, :, pl.ds(start, tk)]
            s3 = s3 - ck[:, None, :]
            ok = causal if diag else None
        elif kind == "dsa":
            msk = extra[0][0, :, pl.ds(start, tk)]
            ok = jnp.broadcast_to((msk > 0)[None], (GROUP, tq, tk))
        elif kind == "nsa_sel":
            ok = jnp.broadcast_to(hit[None], (GROUP, tq, tk))
            ok = (ok & causal) if diag else ok
        elif kind == "nsa_win":
            ok = causal & (s_idx >= t_idx - WINDOW)
        else:
            ok = hit.reshape(GROUP, tq, tk)
            if diag:
                ok = ok | (causal & ((s_idx // MOBA_BLOCK) == (t_idx // MOBA_BLOCK)))
        if ok is not None:
            s3 = jnp.where(ok, s3, NEG)
        s = s3.reshape(rows, tk)
        m_prev = m_ref[...]
        m_new = jnp.maximum(m_prev, jnp.max(s, axis=1, keepdims=True))
        alpha = jnp.exp2(m_prev - m_new)
        p = jnp.exp2(s - pltpu.repeat(m_new, tk // LANES, axis=1))
        acc_ref[...] = acc_ref[...] * alpha + _dot(p.astype(BF16), v)
        m_ref[...] = m_new

    def body(c, carry):
        chunk(c, False)
        return carry

    s_ref[c_lo % 2] = scores(c_lo)
    lax.fori_loop(c_lo, c_diag, body, 0)
    chunk(c_diag, True)
    acc = acc_ref[...]
    o = acc[:, :HEAD_DIM] / jnp.maximum(acc[:, HEAD_DIM:HEAD_DIM + 1], 1e-30)
    o_ref[0, 0] = o.reshape(GROUP, tq, HEAD_DIM)


def _flash_prompt(kind, q, k, v, extra, tq=128, tk=512):
    bsz, _, _, t, _ = q.shape
    tk = min(tk, t)
    tq = min(tq, t)
    qspec = pl.BlockSpec((1, 1, GROUP, tq, HEAD_DIM), lambda b, g, i: (b, g, 0, i, 0))
    kvspec = pl.BlockSpec((1, N_KV_HEADS, t, HEAD_DIM), lambda b, g, i: (b, 0, 0, 0))
    vspec = pl.BlockSpec((1, N_KV_HEADS, t, LANES), lambda b, g, i: (b, 0, 0, 0))
    if kind == "fox":
        especs = [pl.BlockSpec((1, 1, GROUP, t), lambda b, g, i: (b, g, 0, 0))]
    elif kind == "dsa":
        especs = [pl.BlockSpec((1, tq, t), lambda b, g, i: (b, i, 0))]
    elif kind == "nsa_sel":
        especs = [pl.BlockSpec((1, 1, tq, extra[0].shape[-1]), lambda b, g, i: (b, g, i, 0)),
                  pl.BlockSpec(extra[1].shape, lambda b, g, i: (0, 0, 0))]
    elif kind == "moba":
        especs = [pl.BlockSpec((1, 1, GROUP, tq, extra[0].shape[-1]), lambda b, g, i: (b, g, 0, i, 0)),
                  pl.BlockSpec(extra[1].shape, lambda b, g, i: (0, 0, 0))]
    else:
        especs = []
    rows = GROUP * tq
    return pl.pallas_call(
        functools.partial(_flash_prompt_kernel, kind=kind, tq=tq, tk=tk, n_extra=len(extra)),
        grid=(bsz, N_KV_HEADS, t // tq),
        in_specs=[qspec, kvspec, vspec] + especs,
        out_specs=qspec,
        out_shape=jax.ShapeDtypeStruct(q.shape, F32),
        scratch_shapes=[pltpu.VMEM((rows, LANES), F32), pltpu.VMEM((rows, LANES), F32),
                        pltpu.VMEM((2, rows, tk), F32)],
        compiler_params=_params("parallel", "parallel", "parallel"),
        name="flash_prompt_" + kind,
    )(q, k, v, *extra)


def _expand_matrix(n_blocks_padded, block, t, tk):
    s = np.arange(t)
    e = (s[None, :] // block == np.arange(n_blocks_padded)[:, None]).astype(np.float32)
    e = e.reshape(n_blocks_padded, t // tk, tk).transpose(1, 0, 2)
    return jnp.asarray(e, dtype=BF16)


def _dsa_select_prompt_kernel(qi_ref, ki_ref, wi_ref, mask_ref, u_ref, *, tq, t, tk, topk):
    q0 = pl.program_id(1) * tq
    n_act = q0 // tk + 1
    sub = tk // LANES
    w = wi_ref[0]
    t_idx = q0 + lax.broadcasted_iota(I32, (tq, tk), 0)
    k_off = lax.broadcasted_iota(I32, (tq, tk), 1)

    def chunk_at(c):
        return pl.ds(pl.multiple_of(c * tk, tk), tk)

    def fill(c, carry):
        kc = ki_ref[0, chunk_at(c), :]
        s = jnp.zeros((tq, tk), F32)
        for h in range(IDX_HEADS):
            s = s + jnp.maximum(_dot_t(qi_ref[0, h], kc), 0.0) * w[:, h:h + 1]
        s = jnp.where(c * tk + k_off <= t_idx, s, -jnp.inf)
        u_ref[:, chunk_at(c)] = _sortable(s)
        return carry

    lax.fori_loop(0, n_act, fill, 0)

    def count(pred):
        def body(c, acc):
            hit = pred(u_ref[:, chunk_at(c)]).astype(I32)
            for i in range(sub):
                acc = acc + hit[:, i * LANES:(i + 1) * LANES]
            return acc
        acc = lax.fori_loop(0, n_act, body, jnp.zeros((tq, LANES), I32))
        return jnp.sum(acc, axis=1, keepdims=True)

    base = jnp.where(count(lambda u: u >= 0) >= topk, I32(0), I32(INT_MIN))

    def radix(i, base):
        cand = base | jnp.left_shift(I32(1), 30 - i)
        return jnp.where(count(lambda u: u >= cand) >= topk, cand, base)

    thr = lax.fori_loop(0, 31, radix, base)
    need = (topk - count(lambda u: u > thr)).astype(F32)
    r_i = lax.broadcasted_iota(I32, (LANES, LANES), 0)
    c_i = lax.broadcasted_iota(I32, (LANES, LANES), 1)
    tri = jnp.where(r_i <= c_i, 1.0, 0.0).astype(BF16)

    def emit(c, carry):
        for i in range(sub):
            at = pl.ds(pl.multiple_of(c * tk + i * LANES, LANES), LANES)
            u = u_ref[:, at]
            tie = u == thr
            tie_f = jnp.where(tie, 1.0, 0.0)
            inc = _dot(tie_f.astype(BF16), tri)
            sel = (u > thr) | (tie & (carry + inc - tie_f < need))
            keep = sel & (u > KEY_NEG_INF) & (u < KEY_POS_INF)
            mask_ref[0, :, at] = jnp.where(keep, 1.0, 0.0).astype(BF16)
            carry = carry + inc[:, LANES - 1:LANES]
        return carry

    lax.fori_loop(0, n_act, emit, jnp.zeros((tq, 1), F32))

    def clear(c, carry):
        mask_ref[0, :, chunk_at(c)] = jnp.zeros((tq, tk), BF16)
        return carry

    lax.fori_loop(n_act, t // tk, clear, 0)


def _dsa_select_prompt(qi, ki, wi, tq=256, tk=512):
    bsz, _, t, _ = qi.shape
    tq, tk = min(tq, t), min(tk, t)
    topk = min(IDX_TOPK, t // 4)
    return pl.pallas_call(
        functools.partial(_dsa_select_prompt_kernel, tq=tq, t=t, tk=tk, topk=topk),
        grid=(bsz, t // tq),
        in_specs=[pl.BlockSpec((1, IDX_HEADS, tq, IDX_DIM), lambda b, i: (b, 0, i, 0)),
                  pl.BlockSpec((1, t, IDX_DIM), lambda b, i: (b, 0, 0)),
                  pl.BlockSpec((1, tq, IDX_HEADS), lambda b, i: (b, i, 0))],
        out_specs=pl.BlockSpec((1, tq, t), lambda b, i: (b, i, 0)),
        out_shape=jax.ShapeDtypeStruct((bsz, t, t), BF16),
        scratch_shapes=[pltpu.VMEM((tq, t), I32)],
        compiler_params=_params("parallel", "parallel"),
        name="dsa_select_prompt",
    )(qi, ki, wi)


def _dsa_select_sample_kernel(pt_ref, qi_ref, wi_ref, *refs, n_pg, n_steps, tpad, n_new, topk):
    pages = refs[:n_pg]
    new_ref, mask_ref, u_ref = refs[n_pg:]
    j = pl.program_id(1)
    q = qi_ref[0]
    w = wi_ref[0]

    def scores(kc):
        rel = jnp.maximum(_dot(q, kc.astype(BF16)), 0.0) * w
        return rel.reshape(tpad, IDX_HEADS, LANES).sum(axis=1)

    @pl.when(j < n_steps - 1)
    def _():
        for i in range(n_pg):
            start = pl.multiple_of((j * n_pg + i) * LANES, LANES)
            u_ref[:, pl.ds(start, LANES)] = _sortable(scores(pages[i][0]))

    @pl.when(j == n_steps - 1)
    def _():
        s = scores(new_ref[0])
        t_idx = lax.broadcasted_iota(I32, (tpad, LANES), 0)
        c_idx = lax.broadcasted_iota(I32, (tpad, LANES), 1)
        s = jnp.where((c_idx <= t_idx) & (c_idx < n_new), s, -jnp.inf)
        base = (n_steps - 1) * n_pg * LANES
        u_ref[:, base:base + LANES] = _sortable(s)
        for i in range(1, n_pg):
            u_ref[:, base + i * LANES:base + (i + 1) * LANES] = jnp.full((tpad, LANES), KEY_NEG_INF, I32)

        def write(ci, sel, u):
            keep = sel & (u > KEY_NEG_INF) & (u < KEY_POS_INF)
            mask_ref[0, :, ci * LANES:(ci + 1) * LANES] = jnp.where(keep, 1.0, 0.0)

        _emit_selection(u_ref, topk, write)


def _dsa_select_sample(page_table, qi, wi, pool_kidx, new_ki, n_new, n_pg=8):
    bsz, n_pages = page_table.shape
    tpad = qi.shape[1] // IDX_HEADS
    n_steps = n_pages // n_pg + 1
    width = n_steps * n_pg * LANES
    topk = min(IDX_TOPK, (n_pages * PAGE_SIZE + n_new) // 4)
    fix = lambda b, j, pt: (b, 0, 0)
    grid_spec = pltpu.PrefetchScalarGridSpec(
        num_scalar_prefetch=1,
        grid=(bsz, n_steps),
        in_specs=[pl.BlockSpec((1, tpad * IDX_HEADS, IDX_DIM), fix), pl.BlockSpec((1, tpad * IDX_HEADS, 1), fix)]
        + [pl.BlockSpec((1, IDX_DIM, PAGE_SIZE), _page_map(i, n_pg, n_pages)) for i in range(n_pg)]
        + [pl.BlockSpec((1, IDX_DIM, PAGE_SIZE), fix)],
        out_specs=pl.BlockSpec((1, tpad, width), fix),
        scratch_shapes=[pltpu.VMEM((tpad, width), I32)],
    )
    return pl.pallas_call(
        functools.partial(_dsa_select_sample_kernel, n_pg=n_pg, n_steps=n_steps, tpad=tpad, n_new=n_new, topk=topk),
        grid_spec=grid_spec,
        out_shape=jax.ShapeDtypeStruct((bsz, tpad, width), F32),
        compiler_params=_params("parallel", "arbitrary"),
        name="dsa_select_sample",
    )(page_table, qi, wi, *([pool_kidx] * n_pg), new_ki)


def _page_map(i, n_pg, n_pages):
    return lambda b, j, pt: (pt[b, jnp.minimum(j * n_pg + i, n_pages - 1)], 0, 0)


def _page_map4(i, n_pg, n_pages):
    return lambda b, j, pt: (pt[b, jnp.minimum(j * n_pg + i, n_pages - 1)], 0, 0, 0)


def _paged_flash_kernel(pt_ref, q_ref, *refs, n_pg, n_steps, nb):
    pages = refs[:nb * n_pg]
    new_ref, bias_ref, o_ref, m_ref, l_ref, acc_ref = refs[nb * n_pg:]
    j = pl.program_id(1)

    @pl.when(j == 0)
    def _():
        m_ref[...] = jnp.full(m_ref.shape, NEG, F32)
        l_ref[...] = jnp.zeros(l_ref.shape, F32)
        acc_ref[...] = jnp.zeros(acc_ref.shape, F32)

    def update(bi, kv):
        q = q_ref[bi]
        s = [_dot(q, k().astype(BF16)) + bias_ref[bi, :, i * LANES:(i + 1) * LANES] for i, (k, _) in enumerate(kv)]
        m_prev = m_ref[bi]
        m_new = m_prev
        for si in s:
            m_new = jnp.maximum(m_new, jnp.max(si, axis=1, keepdims=True))
        alpha = jnp.exp(m_prev - m_new)
        l_new = alpha * l_ref[bi]
        acc = acc_ref[bi] * pltpu.repeat(alpha, KV_COLS // LANES, axis=1)
        for si, (_, v) in zip(s, kv):
            p = jnp.where(si > 0.5 * NEG, jnp.exp(si - m_new), 0.0)
            l_new = l_new + jnp.sum(p, axis=1, keepdims=True)
            acc = acc + _dot_t(p.astype(BF16), v().astype(BF16))
        l_ref[bi] = l_new
        acc_ref[bi] = acc
        m_ref[bi] = m_new

    def loaders(ref, lead):
        return (lambda: ref[lead, 0]), (lambda: ref[lead, 1])

    @pl.when(j < n_steps - 1)
    def _():
        for bi in range(nb):
            update(bi, [loaders(pages[bi * n_pg + i], 0) for i in range(n_pg)])

    @pl.when(j == n_steps - 1)
    def _():
        for bi in range(nb):
            update(bi, [loaders(new_ref, bi)])
        o_ref[...] = acc_ref[...] / jnp.maximum(jnp.concatenate([l_ref[...]] * (KV_COLS // LANES), axis=-1), 1e-30)


def _paged_flash(page_table, q_bd, pool_t, new_page_t, bias, n_pg):
    bsz, n_pages = page_table.shape
    n_steps = n_pages // n_pg + 1
    rows = q_bd.shape[1]
    nb = 4 if bsz % 4 == 0 else (2 if bsz % 2 == 0 else 1)
    fix = lambda b, j, pt: (b, 0, 0)
    fix4 = lambda b, j, pt: (b, 0, 0, 0)
    page_block = (1, 2, KV_COLS, PAGE_SIZE)

    def page_map(bi, i):
        return lambda b, j, pt: (pt[b * nb + bi, jnp.minimum(j * n_pg + i, n_pages - 1)], 0, 0, 0)

    grid_spec = pltpu.PrefetchScalarGridSpec(
        num_scalar_prefetch=1,
        grid=(bsz // nb, n_steps),
        in_specs=[pl.BlockSpec((nb, rows, KV_COLS), fix)]
        + [pl.BlockSpec(page_block, page_map(bi, i)) for bi in range(nb) for i in range(n_pg)]
        + [pl.BlockSpec((nb, 2, KV_COLS, PAGE_SIZE), fix4),
           pl.BlockSpec((nb, rows, n_pg * LANES), lambda b, j, pt: (b, 0, j))],
        out_specs=pl.BlockSpec((nb, rows, KV_COLS), fix),
        scratch_shapes=[pltpu.VMEM((nb, rows, LANES), F32), pltpu.VMEM((nb, rows, LANES), F32),
                        pltpu.VMEM((nb, rows, KV_COLS), F32)],
    )
    return pl.pallas_call(
        functools.partial(_paged_flash_kernel, n_pg=n_pg, n_steps=n_steps, nb=nb),
        grid_spec=grid_spec,
        out_shape=jax.ShapeDtypeStruct((bsz, rows, KV_COLS), F32),
        compiler_params=_params("parallel", "arbitrary"),
        name="paged_flash",
    )(page_table, q_bd, *([pool_t] * (nb * n_pg)), new_page_t, bias)


def _cumsum_kernel(pt_ref, *refs, n_pg, n_steps):
    pages = refs[:n_pg]
    new_ref, o_ref, carry_ref = refs[n_pg:]
    j = pl.program_id(1)
    r_i = lax.broadcasted_iota(I32, (LANES, LANES), 0)
    c_i = lax.broadcasted_iota(I32, (LANES, LANES), 1)
    tri = jnp.where(r_i <= c_i, 1.0, 0.0).astype(BF16)

    @pl.when(j == 0)
    def _():
        carry_ref[...] = jnp.zeros(carry_ref.shape, F32)

    def step(x, i):
        c = _dot_hp(x, tri) + carry_ref[...]
        o_ref[0, :, i * LANES:(i + 1) * LANES] = c
        carry_ref[...] = jnp.broadcast_to(c[:, LANES - 1:LANES], c.shape)

    @pl.when(j < n_steps - 1)
    def _():
        for i in range(n_pg):
            step(pages[i][0], i)

    @pl.when(j == n_steps - 1)
    def _():
        step(new_ref[0], 0)
        for i in range(1, n_pg):
            o_ref[0, :, i * LANES:(i + 1) * LANES] = jnp.zeros((N_HEADS, LANES), F32)


def _paged_cumsum(page_table, pool_t, new_t, n_pg):
    bsz, n_pages = page_table.shape
    n_steps = n_pages // n_pg + 1
    fix = lambda b, j, pt: (b, 0, 0)
    grid_spec = pltpu.PrefetchScalarGridSpec(
        num_scalar_prefetch=1,
        grid=(bsz, n_steps),
        in_specs=[pl.BlockSpec((1, N_HEADS, LANES), _page_map(i, n_pg, n_pages)) for i in range(n_pg)]
        + [pl.BlockSpec((1, N_HEADS, LANES), fix)],
        out_specs=pl.BlockSpec((1, N_HEADS, n_pg * LANES), lambda b, j, pt: (b, 0, j)),
        scratch_shapes=[pltpu.VMEM((N_HEADS, LANES), F32)],
    )
    return pl.pallas_call(
        functools.partial(_cumsum_kernel, n_pg=n_pg, n_steps=n_steps),
        grid_spec=grid_spec,
        out_shape=jax.ShapeDtypeStruct((bsz, N_HEADS, n_steps * n_pg * LANES), F32),
        compiler_params=_params("parallel", "arbitrary"),
        name="fox_cumsum",
    )(page_table, *([pool_t] * n_pg), new_t)


def _gelu_tanh(x):
    return 0.5 * x * (1.0 + jnp.tanh(0.7978845608028654 * (x + 0.044715 * x * x * x)))


def _nsa_compress_kernel(pt_ref, *refs, n_pg, n_steps, nc):
    pages = refs[:n_pg]
    pe_ref, w1_ref, w2_ref, o_ref, x_ref = refs[n_pg:]
    j = pl.program_id(1)
    per = PAGE_SIZE // CMP_STRIDE
    n_chunk = x_ref.shape[1]
    for i in range(0, n_pg, 2):
        start = pl.multiple_of((j * n_pg + i) * per, 2 * per)
        for ck in range(2 * N_KV_HEADS):
            x_ref[ck, pl.ds(start, 2 * per), :] = jnp.concatenate([pages[i][0, ck], pages[i + 1][0, ck]], axis=0)

    @pl.when(j == n_steps - 1)
    def _():
        half = CMP_STRIDE * HEAD_DIM
        rows = N_KV_HEADS * n_chunk
        row = lax.broadcasted_iota(I32, (N_KV_HEADS, n_chunk, HEAD_DIM), 1)
        for c in range(2):
            w1 = w1_ref[c]
            part = _dot(x_ref[c * N_KV_HEADS:(c + 1) * N_KV_HEADS].reshape(rows, half), w1)
            pe = pe_ref[c]
            pe_term = _dot(pe[:, :half], w1)[:, :CMP_HID] + _dot(pe[:, half:], w1)[:, CMP_HID:]
            h = pe_term[0:1, :] + part[:, :CMP_HID] + pltpu.roll(part[:, CMP_HID:], rows - 1, 0)
            out = _dot(_gelu_tanh(h).astype(BF16), w2_ref[c]).reshape(N_KV_HEADS, n_chunk, HEAD_DIM)
            o_ref[0, c * N_KV_HEADS:(c + 1) * N_KV_HEADS] = jnp.where(row < nc, out, 0.0)


def _nsa_compress(page_table, pool_t, pe8, w1cat, w2, n_pg=8):
    bsz, n_pages = page_table.shape
    per = PAGE_SIZE // CMP_STRIDE
    n_chunk = n_pages * per
    nc = n_chunk - CMP_LEN // CMP_STRIDE + 1
    n_steps = n_pages // n_pg
    width = CMP_STRIDE * HEAD_DIM

    def page_map(i):
        return lambda b, j, pt: (pt[b, j * n_pg + i], 0, 0, 0)

    fix3 = lambda b, j, pt: (0, 0, 0)
    grid_spec = pltpu.PrefetchScalarGridSpec(
        num_scalar_prefetch=1,
        grid=(bsz, n_steps),
        in_specs=[pl.BlockSpec((1, 2 * N_KV_HEADS, per, width), page_map(i)) for i in range(n_pg)]
        + [pl.BlockSpec(pe8.shape, fix3), pl.BlockSpec(w1cat.shape, fix3), pl.BlockSpec(w2.shape, fix3)],
        out_specs=pl.BlockSpec((1, 2 * N_KV_HEADS, n_chunk, HEAD_DIM), lambda b, j, pt: (b, 0, 0, 0)),
        scratch_shapes=[pltpu.VMEM((2 * N_KV_HEADS, n_chunk, width), BF16)],
    )
    return pl.pallas_call(
        functools.partial(_nsa_compress_kernel, n_pg=n_pg, n_steps=n_steps, nc=nc),
        grid_spec=grid_spec,
        out_shape=jax.ShapeDtypeStruct((bsz, 2 * N_KV_HEADS, n_chunk, HEAD_DIM), F32),
        compiler_params=_params("parallel", "arbitrary"),
        name="nsa_compress",
    )(page_table, *([pool_t] * n_pg), pe8, w1cat, w2), nc


def _nsa_cmp_kernel(q_ref, ck_ref, cv_ref, cover_ref, o_ref, sel_ref, u_ref, *, tq, pos0, nc, n_sel):
    t0 = pos0 + pl.program_id(1) * tq
    ncp = ck_ref.shape[2]
    nsp = cover_ref.shape[0]
    rows = GROUP * tq
    n_idx = lax.broadcasted_iota(I32, (tq, ncp), 1)
    t_idx = t0 + lax.broadcasted_iota(I32, (tq, ncp), 0)
    c_ok = ((n_idx * CMP_STRIDE + CMP_LEN - 1 <= t_idx) & (n_idx < nc))[None]
    blk = lax.broadcasted_iota(I32, (nsp, tq), 0)
    cur = (t0 + lax.broadcasted_iota(I32, (nsp, tq), 1)) // SEL_BLOCK
    forced = (blk == 0) | (blk == cur) | (blk == cur - 1)
    cover_t = cover_ref[...]
    for g in range(N_KV_HEADS):
        q = q_ref[0, g].reshape(rows, HEAD_DIM)
        s3 = jnp.where(c_ok, _dot_t(q, ck_ref[0, g]).reshape(GROUP, tq, ncp), NEG)
        m = jnp.max(s3, axis=-1, keepdims=True)
        e = jnp.where(c_ok, jnp.exp(s3 - m), 0.0)
        p = e / jnp.maximum(jnp.sum(e, axis=-1, keepdims=True), 1e-30)
        o = _dot(p.reshape(rows, ncp).astype(BF16), cv_ref[0, g])
        o_ref[0, g] = o.reshape(GROUP, tq, HEAD_DIM)
        psum = p[0] + p[1] + p[2] + p[3]
        hi = psum.astype(BF16)
        r1 = psum - hi.astype(F32)
        mid = r1.astype(BF16)
        lo = (r1 - mid.astype(F32)).astype(BF16)
        imp = _dot_t(cover_t, hi) + _dot_t(cover_t, mid) + _dot_t(cover_t, lo)
        imp = jnp.where(forced, jnp.inf, imp)
        imp = jnp.where(blk <= cur, imp, -jnp.inf)
        u_ref[:, g * tq:(g + 1) * tq] = _sortable(imp)
    sel, u = _topk_mask_cols(u_ref, n_sel)
    sel_ref[0, 0] = jnp.where(sel & (u > KEY_NEG_INF), 1.0, 0.0)


def _nsa_cmp_select(q, cmp_k, cmp_v, pos0, nc, n_keys, tq):
    bsz, _, _, t, _ = q.shape
    ncp = cmp_k.shape[2]
    ns = -(-n_keys // SEL_BLOCK)
    nsp = -(-ns // LANES) * LANES
    n_sel = min(SEL_TOPN, ns)
    c0 = np.arange(ncp)[:, None] * CMP_STRIDE
    s0 = np.arange(nsp)[None, :] * SEL_BLOCK
    cover = (c0 <= s0 + SEL_BLOCK - 1) & (c0 + CMP_LEN - 1 >= s0) & (np.arange(ncp)[:, None] < nc) & (np.arange(nsp)[None, :] < ns)
    cover_t = jnp.asarray(cover.T.astype(np.float32), dtype=BF16)
    nq = t // tq
    qspec = pl.BlockSpec((1, N_KV_HEADS, GROUP, tq, HEAD_DIM), lambda b, i: (b, 0, 0, i, 0))
    cspec = pl.BlockSpec((1, N_KV_HEADS, ncp, HEAD_DIM), lambda b, i: (b, 0, 0, 0))
    o_c, sel = pl.pallas_call(
        functools.partial(_nsa_cmp_kernel, tq=tq, pos0=pos0, nc=nc, n_sel=n_sel),
        grid=(bsz, nq),
        in_specs=[qspec, cspec, cspec, pl.BlockSpec((nsp, ncp), lambda b, i: (0, 0))],
        out_specs=[qspec, pl.BlockSpec((1, 1, nsp, N_KV_HEADS * tq), lambda b, i: (b, i, 0, 0))],
        out_shape=[jax.ShapeDtypeStruct(q.shape, F32), jax.ShapeDtypeStruct((bsz, nq, nsp, N_KV_HEADS * tq), F32)],
        scratch_shapes=[pltpu.VMEM((nsp, N_KV_HEADS * tq), I32)],
        compiler_params=_params("parallel", "parallel"),
        name="nsa_cmp_select",
    )(q, cmp_k, cmp_v, cover_t)
    sel = sel.reshape(bsz, nq, nsp, N_KV_HEADS, tq).transpose(0, 3, 1, 4, 2).reshape(bsz, N_KV_HEADS, t, nsp)
    return o_c, sel


def _kmean_kernel(*refs):
    o_ref = refs[-1]
    tot = jnp.sum(refs[0][0], axis=0, keepdims=True)
    for r in refs[1:-1]:
        tot = tot + jnp.sum(r[0], axis=0, keepdims=True)
    o_ref[0, 0] = tot * (1.0 / MOBA_BLOCK)


def _kmean_prompt(kv):
    bsz, t, _ = kv.shape
    nb = t // MOBA_BLOCK
    return pl.pallas_call(
        _kmean_kernel,
        grid=(bsz, nb),
        in_specs=[pl.BlockSpec((1, MOBA_BLOCK, KV_COLS), lambda b, i: (b, i, 0))],
        out_specs=pl.BlockSpec((1, 1, 1, KV_COLS), lambda b, i: (b, i, 0, 0)),
        out_shape=jax.ShapeDtypeStruct((bsz, nb, 1, KV_COLS), F32),
        compiler_params=_params("parallel", "parallel"),
        name="kmean_prompt",
    )(kv)


def _kmean_sample_kernel(pt_ref, *refs, per):
    o_ref = refs[-1]
    ones = jnp.ones((8, PAGE_SIZE), BF16)
    for n in range(len(refs[:-1]) // per):
        tot = jnp.zeros((8, KV_COLS), F32)
        for r in refs[n * per:(n + 1) * per]:
            x = r[0, 0]
            hi = x.astype(BF16)
            r1 = x - hi.astype(F32)
            mid = r1.astype(BF16)
            lo = (r1 - mid.astype(F32)).astype(BF16)
            tot = tot + _dot_t(ones, hi) + _dot_t(ones, mid) + _dot_t(ones, lo)
        o_ref[0, n] = tot[0:1] * (1.0 / MOBA_BLOCK)


def _kmean_sample(page_table, pool_t, n_pg):
    bsz, n_pages = page_table.shape
    per = MOBA_BLOCK // PAGE_SIZE
    nb = n_pages // per
    grid_spec = pltpu.PrefetchScalarGridSpec(
        num_scalar_prefetch=1,
        grid=(bsz, n_pages // n_pg),
        in_specs=[pl.BlockSpec((1, 1, KV_COLS, PAGE_SIZE), (lambda i: (lambda b, n, pt: (pt[b, n * n_pg + i], 0, 0, 0)))(i))
                  for i in range(n_pg)],
        out_specs=pl.BlockSpec((1, n_pg // per, 1, KV_COLS), lambda b, n, pt: (b, n, 0, 0)),
    )
    return pl.pallas_call(
        functools.partial(_kmean_sample_kernel, per=per),
        grid_spec=grid_spec,
        out_shape=jax.ShapeDtypeStruct((bsz, nb, 1, KV_COLS), F32),
        compiler_params=_params("parallel", "parallel"),
        name="kmean_sample",
    )(page_table, *([pool_t] * n_pg))


def _moba_select_kernel(q_ref, km_ref, sel_ref, *, tq, pos0, k_top):
    t0 = pos0 + pl.program_id(1) * tq
    nbp = km_ref.shape[2]
    rows = GROUP * tq
    blk = lax.broadcasted_iota(I32, (nbp, rows), 0)
    n_past = (t0 + (lax.broadcasted_iota(I32, (nbp, rows), 1) & (tq - 1))) // MOBA_BLOCK
    for g in range(N_KV_HEADS):
        q = q_ref[0, g].reshape(rows, HEAD_DIM)
        s = jnp.where(blk < n_past, _dot_t(km_ref[0, g], q), -jnp.inf)
        sel = jnp.zeros((nbp, rows), F32)
        for _ in range(k_top):
            m = jnp.max(s, axis=0, keepdims=True)
            first = jnp.min(jnp.where(s == m, blk, nbp), axis=0, keepdims=True)
            pick = blk == first
            sel = jnp.where(pick & (m > -jnp.inf), 1.0, sel)
            s = jnp.where(pick, -jnp.inf, s)
        sel_ref[0, 0, g] = sel


def _moba_select(q, kmean, pos0, nb, tq):
    bsz, _, _, t, _ = q.shape
    nbp = kmean.shape[2]
    nq = t // tq
    qspec = pl.BlockSpec((1, N_KV_HEADS, GROUP, tq, HEAD_DIM), lambda b, i: (b, 0, 0, i, 0))
    sel = pl.pallas_call(
        functools.partial(_moba_select_kernel, tq=tq, pos0=pos0, k_top=min(MOBA_TOPK, nb)),
        grid=(bsz, nq),
        in_specs=[qspec, pl.BlockSpec((1, N_KV_HEADS, nbp, HEAD_DIM), lambda b, i: (b, 0, 0, 0))],
        out_specs=pl.BlockSpec((1, 1, N_KV_HEADS, nbp, GROUP * tq), lambda b, i: (b, i, 0, 0, 0)),
        out_shape=jax.ShapeDtypeStruct((bsz, nq, N_KV_HEADS, nbp, GROUP * tq), F32),
        compiler_params=_params("parallel", "parallel"),
        name="moba_select",
    )(q, kmean)
    sel = sel.reshape(bsz, nq, N_KV_HEADS, nbp, GROUP, tq).transpose(0, 2, 4, 1, 5, 3)
    return sel.reshape(bsz, N_KV_HEADS, GROUP, t, nbp)


def _rope_tables(pos):
    half = HEAD_DIM // 2
    inv = ROPE_THETA ** (-jnp.arange(half, dtype=F32) / half)
    ang = pos.astype(F32)[:, None] * inv[None, :]
    return jnp.cos(ang)[:, None, :], jnp.sin(ang)[:, None, :]


def _rope(x, cs):
    cos, sin = cs
    half = x.shape[-1] // 2
    x1, x2 = x[..., :half], x[..., half:]
    return jnp.concatenate([x1 * cos - x2 * sin, x1 * sin + x2 * cos], axis=-1)


def _pad_to(x, axis, size):
    pad = [(0, 0)] * x.ndim
    pad[axis] = (0, size - x.shape[axis])
    return jnp.pad(x, pad)


def _q_groups(q, scale=ATTN_SCALE * LOG2E):
    b, t = q.shape[:2]
    return (q * scale).transpose(0, 2, 1, 3).reshape(b, N_KV_HEADS, GROUP, t, HEAD_DIM).astype(BF16)


def _kv_major(kv):
    b, t = kv.shape[:2]
    k = kv[:, :, 0].transpose(0, 2, 1, 3).astype(BF16)
    v = kv[:, :, 1].transpose(0, 2, 1, 3).astype(BF16)
    ones = jnp.ones((b, N_KV_HEADS, t, 1), BF16)
    zeros = jnp.zeros((b, N_KV_HEADS, t, LANES - HEAD_DIM - 1), BF16)
    return k, jnp.concatenate([v, ones, zeros], axis=-1)


def _pages_t(kv):
    return kv.transpose(0, 2, 3, 4, 1).reshape(kv.shape[0], 2, KV_COLS, PAGE_SIZE)


def _fit(x, width):
    return x[..., :width] if x.shape[-1] >= width else _pad_to(x, x.ndim - 1, width)


def _merge_heads(o):
    b, _, _, t, _ = o.shape
    return o.reshape(b, N_HEADS, t, HEAD_DIM).transpose(0, 2, 1, 3).reshape(b * t, Q_DIM)


_HEAD_TO_GROUP = np.equal(np.arange(N_HEADS)[:, None] // GROUP, np.arange(N_KV_HEADS)[None, :]).astype(np.float32)


def _q_block_diag(q):
    s, tn = q.shape[:2]
    qb = (q * ATTN_SCALE)[:, :, :, None, :] * _HEAD_TO_GROUP[None, None, :, :, None]
    return qb.reshape(s, tn * N_HEADS, KV_COLS).astype(BF16)


def _extract_block_diag(o, tn):
    s = o.shape[0]
    o5 = o.reshape(s, tn, N_HEADS, N_KV_HEADS, HEAD_DIM) * _HEAD_TO_GROUP[None, None, :, :, None]
    return o5.sum(axis=3).reshape(s * tn, Q_DIM)


def _new_page(kv_new):
    return _pages_t(_pad_to(kv_new, 1, PAGE_SIZE))


def _pick_pages(n_pages):
    for n in (8, 4, 2, 1):
        if n_pages % n == 0:
            return n


def _identity_pages(bsz, n_pages):
    return jnp.arange(bsz * n_pages, dtype=I32).reshape(bsz, n_pages)


def _split(proj, sizes):
    out, o = [], 0
    for s in sizes:
        out.append(proj[:, o:o + s])
        o += s
    return out


def _rows_th(ok, s, tn):
    return jnp.where(ok, 0.0, NEG).astype(F32).reshape(s, tn * N_HEADS, ok.shape[-1])


def _dsa_project(x, bsz, t, w_in, cs):
    q, k, v, qi, ki, wi = _split(_matmul(x, w_in), [Q_DIM, KV_COLS, KV_COLS, IDX_HEADS * IDX_DIM, IDX_DIM, IDX_HEADS])
    q = _rope(q.reshape(bsz, t, N_HEADS, HEAD_DIM), cs)
    k = _rope(k.reshape(bsz, t, N_KV_HEADS, HEAD_DIM), cs)
    kv = jnp.stack([k, v.reshape(bsz, t, N_KV_HEADS, HEAD_DIM)], axis=2)
    qi = _rope(qi.reshape(bsz, t, IDX_HEADS, IDX_DIM), cs)
    ki = _rope(ki.reshape(bsz, t, 1, IDX_DIM), cs)[:, :, 0]
    return q, qi, wi.reshape(bsz, t, IDX_HEADS) * IDX_SCALE, kv, ki


def _dsa_prompt(x, bsz, t, w_in, cs):
    q, qi, wi, kv, ki = _dsa_project(x, bsz, t, w_in, cs)
    mask = _dsa_select_prompt(qi.transpose(0, 2, 1, 3).astype(BF16), ki.astype(BF16), wi)
    kh, vh = _kv_major(kv)
    o = _flash_prompt("dsa", _q_groups(q), kh, vh, [mask])
    return _merge_heads(o), kv, ki


def _dsa_sample(x, s, tn, w_in, cs, page_table, cache_kv, cache_kidx, n_pg):
    q, qi, wi, kv, ki = _dsa_project(x, s, tn, w_in, cs)
    tpad = 8
    qi_p = _pad_to(qi, 1, tpad).reshape(s, tpad * IDX_HEADS, IDX_DIM).astype(BF16)
    wi_p = _pad_to(wi, 1, tpad).reshape(s, tpad * IDX_HEADS, 1)
    mask = _dsa_select_sample(page_table, qi_p, wi_p, cache_kidx.transpose(0, 2, 1),
                              _pad_to(ki, 1, PAGE_SIZE).transpose(0, 2, 1), tn, n_pg)
    ok = jnp.broadcast_to(mask[:, :tn, None, :] > 0.5, (s, tn, N_HEADS, mask.shape[-1]))
    o = _paged_flash(page_table, _q_block_diag(q), _pages_t(cache_kv), _new_page(kv), _rows_th(ok, s, tn), n_pg)
    return _extract_block_diag(o, tn), kv, ki


def _fox_project(x, bsz, t, w_in, b_f):
    q, k, v, f = _split(_matmul(x, w_in), [Q_DIM, KV_COLS, KV_COLS, N_HEADS])
    logf = jax.nn.log_sigmoid(f.reshape(bsz, t, N_HEADS) + b_f)
    kv = jnp.stack([k.reshape(bsz, t, N_KV_HEADS, HEAD_DIM), v.reshape(bsz, t, N_KV_HEADS, HEAD_DIM)], axis=2)
    return q.reshape(bsz, t, N_HEADS, HEAD_DIM), kv, logf


def _fox_prompt(x, bsz, t, w_in, b_f):
    q, kv, logf = _fox_project(x, bsz, t, w_in, b_f)
    n_pages = t // LANES
    pool_t = logf.reshape(bsz, n_pages, LANES, N_HEADS).transpose(0, 1, 3, 2).reshape(bsz * n_pages, N_HEADS, LANES)
    c = _paged_cumsum(_identity_pages(bsz, n_pages), pool_t, jnp.zeros((bsz, N_HEADS, LANES), F32), _pick_pages(n_pages))
    c = (c[:, :, :t] * LOG2E).reshape(bsz, N_KV_HEADS, GROUP, t)
    kh, vh = _kv_major(kv)
    o = _flash_prompt("fox", _q_groups(q), kh, vh, [c])
    return _merge_heads(o), kv, logf


def _fox_sample(x, s, tn, w_in, b_f, page_table, cache_kv, cache_logf, n_pg):
    q, kv, logf = _fox_project(x, s, tn, w_in, b_f)
    past = page_table.shape[1] * PAGE_SIZE
    c = _paged_cumsum(page_table, cache_logf.transpose(0, 2, 1), _pad_to(logf.transpose(0, 2, 1), 2, LANES), n_pg)
    col = jnp.arange(c.shape[-1])
    valid = (col[None, :] < past) | ((col[None, :] - past <= jnp.arange(tn)[:, None]) & (col[None, :] < past + tn))
    bias = jnp.where(valid[None, :, None, :], -c[:, None, :, :], NEG).reshape(s, tn * N_HEADS, c.shape[-1])
    o = _paged_flash(page_table, _q_block_diag(q), _pages_t(cache_kv), _new_page(kv), bias, n_pg)
    return _extract_block_diag(o, tn), kv, logf


def _nsa_project(x, bsz, t, w_in, b_gate, cs):
    q, kc, vc, ks, vs, kw, vw, g = _split(_matmul(x, w_in), [Q_DIM] + [KV_COLS] * 6 + [3 * N_HEADS])
    hd = lambda a: a.reshape(bsz, t, N_KV_HEADS, HEAD_DIM)
    q = q.reshape(bsz, t, N_HEADS, HEAD_DIM)
    kv_cmp = jnp.stack([hd(kc), hd(vc)], axis=2)
    kv_slc = jnp.stack([_rope(hd(ks), cs), hd(vs)], axis=2)
    kv_win = jnp.stack([_rope(hd(kw), cs), hd(vw)], axis=2)
    gate = jax.nn.sigmoid(g.reshape(bsz, t, 3 * N_HEADS) + b_gate).reshape(bsz, t, 3, N_HEADS)
    return q, _rope(q, cs), gate, kv_cmp, kv_slc, kv_win


def _nsa_weights(pe, w1, w2):
    r = CMP_LEN // CMP_STRIDE
    w1cat = w1.reshape(2, r, CMP_STRIDE * HEAD_DIM, CMP_HID).transpose(0, 2, 1, 3).reshape(2, CMP_STRIDE * HEAD_DIM, r * CMP_HID)
    pe8 = jnp.broadcast_to(pe.reshape(2, 1, CMP_LEN * HEAD_DIM), (2, 8, CMP_LEN * HEAD_DIM))
    return pe8.astype(BF16), w1cat.astype(BF16), w2.astype(BF16)


def _chunk_pages(kv):
    n = kv.shape[0]
    per = PAGE_SIZE // CMP_STRIDE
    x = kv.reshape(n, per, CMP_STRIDE, 2 * N_KV_HEADS, HEAD_DIM).transpose(0, 3, 1, 2, 4)
    return x.reshape(n, 2 * N_KV_HEADS, per, CMP_STRIDE * HEAD_DIM).astype(BF16)


def _gate_mix(gate, o_c, o_s, o_w):
    b, t = gate.shape[:2]
    g = gate.transpose(0, 2, 3, 1).reshape(b, 3, N_KV_HEADS, GROUP, t, 1)
    return g[:, 0] * o_c + g[:, 1] * o_s + g[:, 2] * o_w


def _nsa_prompt(x, bsz, t, w_in, b_gate, cmp_w, cs, tq=128, tk=512):
    q, q_rot, gate, kv_cmp, kv_slc, kv_win = _nsa_project(x, bsz, t, w_in, b_gate, cs)
    n_pages = t // PAGE_SIZE
    cmp, nc = _nsa_compress(_identity_pages(bsz, n_pages), _chunk_pages(kv_cmp.reshape(bsz * n_pages, PAGE_SIZE, 2, N_KV_HEADS, HEAD_DIM)),
                            *cmp_w, n_pg=_pick_pages(n_pages))
    cmp = cmp.astype(BF16)
    tq, tk = min(tq, t), min(tk, t)
    o_c, selblk = _nsa_cmp_select(_q_groups(q, ATTN_SCALE), cmp[:, :N_KV_HEADS], cmp[:, N_KV_HEADS:], 0, nc, t, tq)
    qg = _q_groups(q_rot)
    e3 = _expand_matrix(selblk.shape[-1], SEL_BLOCK, t, tk)
    o_s = _flash_prompt("nsa_sel", qg, *_kv_major(kv_slc), [selblk, e3], tq, tk)
    o_w = _flash_prompt("nsa_win", qg, *_kv_major(kv_win), [], tq, tk)
    o = _gate_mix(gate, o_c, o_s, o_w)
    return _merge_heads(o), kv_cmp, kv_slc, kv_win[:, -min(WINDOW, t):]


def _nsa_sample(x, s, tn, w_in, b_gate, cmp_w, cs, page_table, cache_cmp, cache_slc, state_win, n_pg):
    q, q_rot, gate, kv_cmp, kv_slc, kv_win = _nsa_project(x, s, tn, w_in, b_gate, cs)
    past = page_table.shape[1] * PAGE_SIZE
    tpad = 8
    cmp, nc = _nsa_compress(page_table, _chunk_pages(cache_cmp), *cmp_w, n_pg=n_pg)
    cmp = cmp.astype(BF16)
    o_c, selblk = _nsa_cmp_select(_q_groups(_pad_to(q, 1, tpad), ATTN_SCALE), cmp[:, :N_KV_HEADS], cmp[:, N_KV_HEADS:],
                                  past, nc, past + tn, tpad)
    o_c = o_c[:, :, :, :tn]
    qbd = _q_block_diag(q_rot)
    width = (page_table.shape[1] // n_pg + 1) * n_pg * LANES
    col = jnp.arange(width)
    pos = past + jnp.arange(tn)
    sel_key = jnp.repeat(selblk[:, :, :tn, :-(-width // SEL_BLOCK)], SEL_BLOCK, axis=-1)[..., :width]
    ok = (sel_key > 0.5) & (col[None, :] <= pos[:, None])[None, None]
    ok = jnp.broadcast_to(ok.transpose(0, 2, 1, 3)[:, :, :, None, :], (s, tn, N_KV_HEADS, GROUP, width))
    o_s = _paged_flash(page_table, qbd, _pages_t(cache_slc), _new_page(kv_slc),
                       _rows_th(ok.reshape(s, tn, N_HEADS, width), s, tn), n_pg)
    win_buf = state_win.shape[1]
    n_wp = win_buf // PAGE_SIZE
    wcol = jnp.arange(2 * n_wp * LANES)
    win_pos = jnp.where(wcol < win_buf, past - win_buf + wcol, jnp.where(wcol < win_buf + tn, past + wcol - win_buf, -1))
    w_ok = (win_pos[None, :] <= pos[:, None]) & (win_pos[None, :] >= pos[:, None] - WINDOW) & (win_pos[None, :] >= 0)
    w_ok = jnp.broadcast_to(w_ok[None, :, None, :], (s, tn, N_HEADS, wcol.shape[0]))
    win_pages = _pages_t(state_win.reshape(s * n_wp, PAGE_SIZE, 2, N_KV_HEADS, HEAD_DIM))
    o_w = _paged_flash(_identity_pages(s, n_wp), qbd, win_pages, _new_page(kv_win), _rows_th(w_ok, s, tn), n_wp)
    unbd = lambda o: _extract_block_diag(o, tn).reshape(s, tn, N_KV_HEADS, GROUP, HEAD_DIM).transpose(0, 2, 3, 1, 4)
    o = _gate_mix(gate, o_c, unbd(o_s), unbd(o_w))
    win = jnp.concatenate([state_win, kv_win], axis=1)[:, -win_buf:]
    return _merge_heads(o), kv_cmp, kv_slc, win


def _moba_project(x, bsz, t, w_in, cs):
    q, k, v = _split(_matmul(x, w_in), [Q_DIM, KV_COLS, KV_COLS])
    q = _rope(q.reshape(bsz, t, N_HEADS, HEAD_DIM), cs)
    kv = jnp.stack([_rope(k.reshape(bsz, t, N_KV_HEADS, HEAD_DIM), cs), v.reshape(bsz, t, N_KV_HEADS, HEAD_DIM)], axis=2)
    return q, kv


def _kmean_heads(km):
    b, nb = km.shape[:2]
    return _pad_to(km.reshape(b, nb, N_KV_HEADS, HEAD_DIM).transpose(0, 2, 1, 3), 2, -(-nb // 16) * 16).astype(BF16)


def _moba_prompt(x, bsz, t, w_in, cs, tq=128, tk=512):
    q, kv = _moba_project(x, bsz, t, w_in, cs)
    tq, tk = min(tq, t), min(tk, t)
    nb = -(-t // MOBA_BLOCK)
    km = _kmean_heads(_kmean_prompt(kv.reshape(bsz, t, 2 * KV_COLS)))
    qg = _q_groups(q)
    sel = _moba_select(qg, km, 0, nb, tq)
    e3 = _expand_matrix(km.shape[2], MOBA_BLOCK, t, tk)
    o = _flash_prompt("moba", qg, *_kv_major(kv), [sel, e3], tq, tk)
    return _merge_heads(o), kv


def _moba_sample(x, s, tn, w_in, cs, page_table, cache_kv, n_pg):
    q, kv = _moba_project(x, s, tn, w_in, cs)
    past = page_table.shape[1] * PAGE_SIZE
    tpad = 8
    pool = _pages_t(cache_kv)
    nb = -(-(past + tn) // MOBA_BLOCK)
    km = _kmean_heads(_kmean_sample(page_table, pool, n_pg))
    sel = _moba_select(_q_groups(_pad_to(q, 1, tpad)), km, past, nb, tpad)
    width = (page_table.shape[1] // n_pg + 1) * n_pg * LANES
    col = jnp.arange(width)
    pos = past + jnp.arange(tn)
    sel_key = _fit(jnp.repeat(sel[:, :, :, :tn], MOBA_BLOCK, axis=-1), width)
    own = (col[None, :] // MOBA_BLOCK == pos[:, None] // MOBA_BLOCK) & (col[None, :] <= pos[:, None])
    ok = (sel_key > 0.5) | own[None, None, None]
    ok = ok.transpose(0, 3, 1, 2, 4).reshape(s, tn, N_HEADS, width)
    o = _paged_flash(page_table, _q_block_diag(q), pool, _new_page(kv), _rows_th(ok, s, tn), n_pg)
    return _extract_block_diag(o, tn), kv


def _cast_w(w):
    return _pad_to(w, 1, -(-w.shape[1] // LANES) * LANES).astype(BF16)


def kernel(x_prompt, x_sample, cache_a_kv, cache_a_kidx, cache_b_kv, cache_b_logf, cache_c_cmp_kv, cache_c_slc_kv, state_c_win_kv, cache_d_kv, page_table, a_w_in, a_w_out, b_w_in, b_b_f, b_w_out, c_w_in, c_b_gate, c_cmp_pe, c_cmp_w1, c_cmp_w2, c_w_out, d_w_in, d_w_out, ln_g, ln_b, ffn_w_gu, ffn_w_down):
    bsz, t, d = x_prompt.shape
    s, tn, _ = x_sample.shape
    n_pages = page_table.shape[1]
    past = n_pages * PAGE_SIZE
    n_pg = _pick_pages(n_pages)
    cs_p = _rope_tables(jnp.arange(t, dtype=I32))
    cs_s = _rope_tables(past + jnp.arange(tn, dtype=I32))
    xp = x_prompt.reshape(bsz * t, d)
    xs = x_sample.reshape(s * tn, d)
    cmp_w = _nsa_weights(c_cmp_pe, c_cmp_w1, c_cmp_w2)
    w_out = [_cast_w(w) for w in (a_w_out, b_w_out, c_w_out, d_w_out)]

    op, a_kv_p, a_kidx_p = _dsa_prompt(xp, bsz, t, _cast_w(a_w_in), cs_p)
    os_, a_kv_s, a_kidx_s = _dsa_sample(xs, s, tn, _cast_w(a_w_in), cs_s, page_table, cache_a_kv, cache_a_kidx, n_pg)

    def finish(i, xp, xs, op, os_):
        xp = _out_ln(op, w_out[i], xp, ln_g[i, 0], ln_b[i, 0])
        xs = _out_ln(os_, w_out[i], xs, ln_g[i, 0], ln_b[i, 0])
        wgu, wd = ffn_w_gu[i].astype(BF16), ffn_w_down[i].astype(BF16)
        xp = _ffn_ln(xp, wgu, wd, ln_g[i, 1], ln_b[i, 1])
        xs = _ffn_ln(xs, wgu, wd, ln_g[i, 1], ln_b[i, 1])
        return xp, xs

    xp, xs = finish(0, xp, xs, op, os_)

    op, b_kv_p, b_logf_p = _fox_prompt(xp, bsz, t, _cast_w(b_w_in), b_b_f)
    os_, b_kv_s, b_logf_s = _fox_sample(xs, s, tn, _cast_w(b_w_in), b_b_f, page_table, cache_b_kv, cache_b_logf, n_pg)
    xp, xs = finish(1, xp, xs, op, os_)

    op, c_cmp_kv_p, c_slc_kv_p, c_win_kv_p = _nsa_prompt(xp, bsz, t, _cast_w(c_w_in), c_b_gate, cmp_w, cs_p)
    os_, c_cmp_kv_s, c_slc_kv_s, c_win_kv_s = _nsa_sample(xs, s, tn, _cast_w(c_w_in), c_b_gate, cmp_w, cs_s, page_table,
                                                          cache_c_cmp_kv, cache_c_slc_kv, state_c_win_kv, n_pg)
    xp, xs = finish(2, xp, xs, op, os_)

    op, d_kv_p = _moba_prompt(xp, bsz, t, _cast_w(d_w_in), cs_p)
    os_, d_kv_s = _moba_sample(xs, s, tn, _cast_w(d_w_in), cs_s, page_table, cache_d_kv, n_pg)
    xp, xs = finish(3, xp, xs, op, os_)

    return (xp.reshape(bsz, t, d), xs.reshape(s, tn, d), a_kv_p, a_kv_s, a_kidx_p, a_kidx_s, b_kv_p, b_kv_s,
            b_logf_p, b_logf_s, c_cmp_kv_p, c_cmp_kv_s, c_slc_kv_p, c_slc_kv_s, c_win_kv_p, c_win_kv_s, d_kv_p, d_kv_s)
```

```python
import functools

import numpy as np
import jax
import jax.numpy as jnp
from jax import lax
from jax.experimental import pallas as pl
from jax.experimental.pallas import tpu as pltpu

F32 = jnp.float32
BF16 = jnp.bfloat16
I32 = jnp.int32

N_HEADS = 16
HEAD_DIM = 64
N_KV_HEADS = 4
GROUP = N_HEADS // N_KV_HEADS
Q_DIM = N_HEADS * HEAD_DIM
KV_COLS = N_KV_HEADS * HEAD_DIM
DEPTH = 4
PAGE_SIZE = 128
ROPE_THETA = 10000.0
LN_EPS = 1e-5
ALPHA = (2 * DEPTH) ** 0.25
ATTN_SCALE = HEAD_DIM ** -0.5
LOG2E = 1.4426950408889634
IDX_HEADS = 8
IDX_DIM = 64
IDX_TOPK = 256
IDX_SCALE = (IDX_HEADS * IDX_DIM) ** -0.5
CMP_LEN = 32
CMP_STRIDE = 16
CMP_HID = 2 * HEAD_DIM
SEL_BLOCK = 64
SEL_TOPN = 16
WINDOW = 512
MOBA_BLOCK = 256
MOBA_TOPK = 3

LANES = 128
VMEM_LIMIT = 56 * 2 ** 20
NEG = -1e30
KEY_NEG_INF = -2139095041
KEY_POS_INF = 2139095040
INT_MIN = -2 ** 31


def _params(*sem):
    return pltpu.CompilerParams(dimension_semantics=sem, vmem_limit_bytes=VMEM_LIMIT)


def _dot_t(a, b):
    return lax.dot_general(a, b, (((1,), (1,)), ((), ())), preferred_element_type=F32)


def _dot(a, b):
    return jnp.dot(a, b, preferred_element_type=F32)


def _dot_hp(a, b):
    hi = a.astype(BF16)
    r1 = a - hi.astype(F32)
    mid = r1.astype(BF16)
    lo = (r1 - mid.astype(F32)).astype(BF16)
    return _dot(hi, b) + _dot(mid, b) + _dot(lo, b)


def _mm_kernel(x_ref, w_ref, o_ref):
    o_ref[...] = _dot(x_ref[...].astype(BF16), w_ref[...])


def _matmul(x, w):
    m, k = x.shape
    n = w.shape[1]
    tm = min(m, 512)
    return pl.pallas_call(
        _mm_kernel,
        grid=(m // tm,),
        in_specs=[pl.BlockSpec((tm, k), lambda i: (i, 0)), pl.BlockSpec((k, n), lambda i: (0, 0))],
        out_specs=pl.BlockSpec((tm, n), lambda i: (i, 0)),
        out_shape=jax.ShapeDtypeStruct((m, n), F32),
        compiler_params=_params("parallel"),
        name="in_proj",
    )(x, w)


def _proj_kernel(x_ref, w_ref, cos_ref, sin_ref, *out_refs, segs):
    acc = _dot(x_ref[...].astype(BF16), w_ref[...])
    tm = acc.shape[0]
    first_half = (lax.broadcasted_iota(I32, (tm, LANES), 1) & (HEAD_DIM - 1)) < HEAD_DIM // 2
    for (c0, width, rope, scale), o_ref in zip(segs, out_refs):
        if c0 % LANES:
            o_ref[...] = (acc[:, c0:c0 + width] * scale).astype(o_ref.dtype)
            continue
        for j in range(-(-width // LANES)):
            x = acc[:, c0 + j * LANES:c0 + (j + 1) * LANES]
            if rope:
                swapped = jnp.where(first_half, pltpu.roll(x, LANES - HEAD_DIM // 2, 1), pltpu.roll(x, HEAD_DIM // 2, 1))
                x = x * cos_ref[...] + swapped * sin_ref[...]
            if scale != 1.0:
                x = x * scale
            wj = min(LANES, width - j * LANES)
            o_ref[:, j * LANES:j * LANES + wj] = x[:, :wj].astype(o_ref.dtype)


def _proj(x, w, cs, segs):
    m, k = x.shape
    n = w.shape[1]
    tm = min(m, 512)
    cos_t, sin_t = cs
    r_blocks = cos_t.shape[0] // tm
    tab = pl.BlockSpec((tm, LANES), lambda i: (i % r_blocks, 0))
    return pl.pallas_call(
        functools.partial(_proj_kernel, segs=tuple(s[:4] for s in segs)),
        grid=(m // tm,),
        in_specs=[pl.BlockSpec((tm, k), lambda i: (i, 0)), pl.BlockSpec((k, n), lambda i: (0, 0)), tab, tab],
        out_specs=[pl.BlockSpec((tm, s[1]), lambda i: (i, 0)) for s in segs],
        out_shape=[jax.ShapeDtypeStruct((m, s[1]), s[4]) for s in segs],
        compiler_params=_params("parallel"),
        name="in_proj",
    )(x, w, cos_t, sin_t)


def _layer_norm(y, g, b):
    mu = jnp.mean(y, axis=-1, keepdims=True)
    d = y - mu
    var = jnp.mean(d * d, axis=-1, keepdims=True)
    return d * lax.rsqrt(var + LN_EPS) * g + b


def _out_ln_kernel(o_ref, w_ref, x_ref, g_ref, b_ref, y_ref):
    y = ALPHA * x_ref[...] + _dot(o_ref[...].astype(BF16), w_ref[...])
    y_ref[...] = _layer_norm(y, g_ref[...], b_ref[...])


def _out_ln(o, w, x, g, b):
    m, d = x.shape
    k = o.shape[1]
    tm = min(m, 512)
    row = lambda i: (i, 0)
    fix = lambda i: (0, 0)
    return pl.pallas_call(
        _out_ln_kernel,
        grid=(m // tm,),
        in_specs=[pl.BlockSpec((tm, k), row), pl.BlockSpec((k, d), fix), pl.BlockSpec((tm, d), row),
                  pl.BlockSpec((1, d), fix), pl.BlockSpec((1, d), fix)],
        out_specs=pl.BlockSpec((tm, d), row),
        out_shape=jax.ShapeDtypeStruct((m, d), F32),
        compiler_params=_params("parallel"),
        name="out_proj_ln",
    )(o, w, x, g.reshape(1, d), b.reshape(1, d))


def _ffn_ln_kernel(x_ref, wgu_ref, wd_ref, g_ref, b_ref, y_ref, *, d_ff, chunk):
    x = x_ref[...]
    xb = x.astype(BF16)
    acc = jnp.zeros(x.shape, F32)
    for c in range(d_ff // chunk):
        gate = _dot(xb, wgu_ref[:, c * chunk:(c + 1) * chunk])
        up = _dot(xb, wgu_ref[:, d_ff + c * chunk:d_ff + (c + 1) * chunk])
        h = gate * (1.0 / (1.0 + jnp.exp(-gate))) * up
        acc = acc + _dot(h.astype(BF16), wd_ref[c * chunk:(c + 1) * chunk, :])
    y_ref[...] = _layer_norm(ALPHA * x + acc, g_ref[...], b_ref[...])


def _ffn_ln(x, wgu, wd, g, b):
    m, d = x.shape
    d_ff = wd.shape[0]
    tm = min(m, 256)
    row = lambda i: (i, 0)
    fix = lambda i: (0, 0)
    return pl.pallas_call(
        functools.partial(_ffn_ln_kernel, d_ff=d_ff, chunk=256),
        grid=(m // tm,),
        in_specs=[pl.BlockSpec((tm, d), row), pl.BlockSpec((d, 2 * d_ff), fix), pl.BlockSpec((d_ff, d), fix),
                  pl.BlockSpec((1, d), fix), pl.BlockSpec((1, d), fix)],
        out_specs=pl.BlockSpec((tm, d), row),
        out_shape=jax.ShapeDtypeStruct((m, d), F32),
        compiler_params=_params("parallel"),
        name="ffn_ln",
    )(x, wgu, wd, g.reshape(1, d), b.reshape(1, d))


def _sortable(x):
    x = jnp.where(x == 0.0, 0.0, x)
    b = lax.bitcast_convert_type(x, I32)
    return b ^ ((b >> 31) & I32(0x7FFFFFFF))


def _kth_largest_key(u_ref, k):
    rows = u_ref.shape[0]

    def count_ge(cand):
        return jnp.sum((u_ref[...] >= cand).astype(I32), axis=1, keepdims=True)

    base = jnp.where(count_ge(jnp.zeros((rows, 1), I32)) >= k, I32(0), I32(INT_MIN))

    def body(i, base):
        cand = base | jnp.left_shift(I32(1), 30 - i)
        return jnp.where(count_ge(cand) >= k, cand, base)

    return lax.fori_loop(0, 31, body, base)


def _emit_selection(u_ref, k, write):
    rows, n = u_ref.shape
    thr = _kth_largest_key(u_ref, k)
    n_gt = jnp.sum((u_ref[...] > thr).astype(I32), axis=1, keepdims=True)
    need = (k - n_gt).astype(F32)
    r_i = lax.broadcasted_iota(I32, (LANES, LANES), 0)
    c_i = lax.broadcasted_iota(I32, (LANES, LANES), 1)
    tri = jnp.where(r_i <= c_i, 1.0, 0.0).astype(BF16)
    carry = jnp.zeros((rows, 1), F32)
    for ci in range(n // LANES):
        u = u_ref[:, ci * LANES:(ci + 1) * LANES]
        tie = u == thr
        tie_f = jnp.where(tie, 1.0, 0.0)
        inc = _dot(tie_f.astype(BF16), tri)
        rank = carry + inc - tie_f
        sel = (u > thr) | (tie & (rank < need))
        write(ci, sel, u)
        carry = carry + inc[:, LANES - 1:LANES]


def _topk_mask_cols(u_ref, k):
    n, r = u_ref.shape

    def count(pred):
        return jnp.sum(pred(u_ref[...]).astype(I32), axis=0, keepdims=True)

    base = jnp.where(count(lambda u: u >= 0) >= k, I32(0), I32(INT_MIN))

    def radix(i, base):
        cand = base | jnp.left_shift(I32(1), 30 - i)
        return jnp.where(count(lambda u: u >= cand) >= k, cand, base)

    thr = lax.fori_loop(0, 31, radix, base)
    need = (k - count(lambda u: u > thr)).astype(F32)
    u = u_ref[...]
    tie = u == thr
    lower = jnp.where(lax.broadcasted_iota(I32, (n, n), 0) > lax.broadcasted_iota(I32, (n, n), 1), 1.0, 0.0)
    rank = _dot(lower.astype(BF16), jnp.where(tie, 1.0, 0.0).astype(BF16))
    return (u > thr) | (tie & (rank < need)), u


def _flash_prompt_kernel(*refs, kind, tq, tk, n_extra):
    q_ref, k_ref, v_ref = refs[:3]
    extra = refs[3:3 + n_extra]
    o_ref, m_ref, acc_ref, s_ref = refs[3 + n_extra:]
    g = pl.program_id(1)
    q0 = pl.program_id(2) * tq
    rows = GROUP * tq
    q = q_ref[0, 0].reshape(rows, HEAD_DIM)
    m_ref[...] = jnp.full(m_ref.shape, NEG, F32)
    acc_ref[...] = jnp.zeros(acc_ref.shape, F32)
    c_diag = q0 // tk
    c_lo = jnp.maximum(q0 - WINDOW, 0) // tk if kind == "nsa_win" else 0
    if kind == "nsa_sel":
        selb = extra[0][0, 0].astype(BF16)
    if kind == "moba":
        selb = extra[0][0, 0].reshape(rows, extra[0].shape[-1]).astype(BF16)

    def scores(c):
        return _dot(q, k_ref[0, g, :, pl.ds(pl.multiple_of(c * tk, tk), tk)])

    def chunk(c, diag):
        if kind in ("nsa_sel", "moba"):
            hit = _dot(selb, extra[1][c]) > 0.5
        s3 = s_ref[c % 2].reshape(GROUP, tq, tk)
        if not diag:
            s_ref[(c + 1) % 2] = scores(c + 1)
        start = pl.multiple_of(c * tk, tk)
        v = v_ref[0, g, pl.ds(start, tk), :]
        ok = None
        if diag or kind == "nsa_win":
            t_idx = q0 + lax.broadcasted_iota(I32, (GROUP, tq, tk), 1)
            s_idx = start + lax.broadcasted_iota(I32, (GROUP, tq, tk), 2)
            causal = s_idx <= t_idx
        if kind == "fox":
            ck = extra[0][0, 0, :, pl.ds(start, tk)]
            s3 = s3 - ck[:, None, :]
            ok = causal if diag else None
        elif kind == "dsa":
            msk = extra[0][0, :, pl.ds(start, tk)]
            ok = jnp.broadcast_to((msk > 0)[None], (GROUP, tq, tk))
        elif kind == "nsa_sel":
            ok = jnp.broadcast_to(hit[None], (GROUP, tq, tk))
            ok = (ok & causal) if diag else ok
        elif kind == "nsa_win":
            ok = causal & (s_idx >= t_idx - WINDOW)
        else:
            ok = hit.reshape(GROUP, tq, tk)
            if diag:
                ok = ok | (causal & ((s_idx // MOBA_BLOCK) == (t_idx // MOBA_BLOCK)))
        if ok is not None:
            s3 = jnp.where(ok, s3, NEG)
        s = s3.reshape(rows, tk)
        m_prev = m_ref[...]
        m_new = jnp.maximum(m_prev, jnp.max(s, axis=1, keepdims=True))
        alpha = jnp.exp2(m_prev - m_new)
        p = jnp.exp2(s - pltpu.repeat(m_new, tk // LANES, axis=1))
        acc_ref[...] = acc_ref[...] * alpha + _dot(p.astype(BF16), v)
        m_ref[...] = m_new

    def body(c, carry):
        chunk(c, False)
        return carry

    s_ref[c_lo % 2] = scores(c_lo)
    lax.fori_loop(c_lo, c_diag, body, 0)
    chunk(c_diag, True)
    acc = acc_ref[...]
    o = acc[:, :HEAD_DIM] / jnp.maximum(acc[:, HEAD_DIM:HEAD_DIM + 1], 1e-30)
    o_ref[0, 0] = o.reshape(GROUP, tq, HEAD_DIM)


def _flash_prompt(kind, q, k, v, extra, tq=128, tk=512):
    bsz, _, _, t, _ = q.shape
    tk = min(tk, t)
    tq = min(tq, t)
    qspec = pl.BlockSpec((1, 1, GROUP, tq, HEAD_DIM), lambda b, g, i: (b, g, 0, i, 0))
    kvspec = pl.BlockSpec((1, N_KV_HEADS, HEAD_DIM, t), lambda b, g, i: (b, 0, 0, 0))
    vspec = pl.BlockSpec((1, N_KV_HEADS, t, LANES), lambda b, g, i: (b, 0, 0, 0))
    if kind == "fox":
        especs = [pl.BlockSpec((1, 1, GROUP, t), lambda b, g, i: (b, g, 0, 0))]
    elif kind == "dsa":
        especs = [pl.BlockSpec((1, tq, t), lambda b, g, i: (b, i, 0))]
    elif kind == "nsa_sel":
        especs = [pl.BlockSpec((1, 1, tq, extra[0].shape[-1]), lambda b, g, i: (b, g, i, 0)),
                  pl.BlockSpec(extra[1].shape, lambda b, g, i: (0, 0, 0))]
    elif kind == "moba":
        especs = [pl.BlockSpec((1, 1, GROUP, tq, extra[0].shape[-1]), lambda b, g, i: (b, g, 0, i, 0)),
                  pl.BlockSpec(extra[1].shape, lambda b, g, i: (0, 0, 0))]
    else:
        especs = []
    rows = GROUP * tq
    return pl.pallas_call(
        functools.partial(_flash_prompt_kernel, kind=kind, tq=tq, tk=tk, n_extra=len(extra)),
        grid=(bsz, N_KV_HEADS, t // tq),
        in_specs=[qspec, kvspec, vspec] + especs,
        out_specs=qspec,
        out_shape=jax.ShapeDtypeStruct(q.shape, F32),
        scratch_shapes=[pltpu.VMEM((rows, LANES), F32), pltpu.VMEM((rows, LANES), F32),
                        pltpu.VMEM((2, rows, tk), F32)],
        compiler_params=_params("parallel", "parallel", "parallel"),
        name="flash_prompt_" + kind,
    )(q, jnp.swapaxes(k, 2, 3), v, *extra)


def _expand_matrix(n_blocks_padded, block, t, tk):
    s = np.arange(t)
    e = (s[None, :] // block == np.arange(n_blocks_padded)[:, None]).astype(np.float32)
    e = e.reshape(n_blocks_padded, t // tk, tk).transpose(1, 0, 2)
    return jnp.asarray(e, dtype=BF16)


def _dsa_select_prompt_kernel(qi_ref, ki_ref, wi_ref, mask_ref, u_ref, *, tq, t, tk, topk):
    q0 = pl.program_id(1) * tq
    n_act = q0 // tk + 1
    sub = tk // LANES
    w = wi_ref[0]
    t_idx = q0 + lax.broadcasted_iota(I32, (tq, tk), 0)
    k_off = lax.broadcasted_iota(I32, (tq, tk), 1)

    def chunk_at(c):
        return pl.ds(pl.multiple_of(c * tk, tk), tk)

    def fill(c, carry):
        kc = ki_ref[0, chunk_at(c), :]
        s = jnp.zeros((tq, tk), F32)
        for h in range(IDX_HEADS):
            s = s + jnp.maximum(_dot_t(qi_ref[0, h], kc), 0.0) * w[:, h:h + 1]
        s = jnp.where(c * tk + k_off <= t_idx, s, -jnp.inf)
        u_ref[:, chunk_at(c)] = _sortable(s)
        return carry

    lax.fori_loop(0, n_act, fill, 0)

    def count(pred):
        def body(c, acc):
            hit = pred(u_ref[:, chunk_at(c)]).astype(I32)
            for i in range(sub):
                acc = acc + hit[:, i * LANES:(i + 1) * LANES]
            return acc
        acc = lax.fori_loop(0, n_act, body, jnp.zeros((tq, LANES), I32))
        return jnp.sum(acc, axis=1, keepdims=True)

    base = jnp.where(count(lambda u: u >= 0) >= topk, I32(0), I32(INT_MIN))

    def radix(i, base):
        cand = base | jnp.left_shift(I32(1), 30 - i)
        return jnp.where(count(lambda u: u >= cand) >= topk, cand, base)

    thr = lax.fori_loop(0, 31, radix, base)
    need = (topk - count(lambda u: u > thr)).astype(F32)
    r_i = lax.broadcasted_iota(I32, (LANES, LANES), 0)
    c_i = lax.broadcasted_iota(I32, (LANES, LANES), 1)
    tri = jnp.where(r_i <= c_i, 1.0, 0.0).astype(BF16)

    def emit(c, carry):
        for i in range(sub):
            at = pl.ds(pl.multiple_of(c * tk + i * LANES, LANES), LANES)
            u = u_ref[:, at]
            tie = u == thr
            tie_f = jnp.where(tie, 1.0, 0.0)
            inc = _dot(tie_f.astype(BF16), tri)
            sel = (u > thr) | (tie & (carry + inc - tie_f < need))
            keep = sel & (u > KEY_NEG_INF) & (u < KEY_POS_INF)
            mask_ref[0, :, at] = jnp.where(keep, 1.0, 0.0).astype(BF16)
            carry = carry + inc[:, LANES - 1:LANES]
        return carry

    lax.fori_loop(0, n_act, emit, jnp.zeros((tq, 1), F32))

    def clear(c, carry):
        mask_ref[0, :, chunk_at(c)] = jnp.zeros((tq, tk), BF16)
        return carry

    lax.fori_loop(n_act, t // tk, clear, 0)


def _dsa_select_prompt(qi, ki, wi, tq=256, tk=512):
    bsz, _, t, _ = qi.shape
    tq, tk = min(tq, t), min(tk, t)
    topk = min(IDX_TOPK, t // 4)
    return pl.pallas_call(
        functools.partial(_dsa_select_prompt_kernel, tq=tq, t=t, tk=tk, topk=topk),
        grid=(bsz, t // tq),
        in_specs=[pl.BlockSpec((1, IDX_HEADS, tq, IDX_DIM), lambda b, i: (b, 0, i, 0)),
                  pl.BlockSpec((1, t, IDX_DIM), lambda b, i: (b, 0, 0)),
                  pl.BlockSpec((1, tq, IDX_HEADS), lambda b, i: (b, i, 0))],
        out_specs=pl.BlockSpec((1, tq, t), lambda b, i: (b, i, 0)),
        out_shape=jax.ShapeDtypeStruct((bsz, t, t), BF16),
        scratch_shapes=[pltpu.VMEM((tq, t), I32)],
        compiler_params=_params("parallel", "parallel"),
        name="dsa_select_prompt",
    )(qi, ki, wi)


def _dsa_select_sample_kernel(pt_ref, qi_ref, wi_ref, *refs, n_pg, n_steps, tpad, n_new, topk):
    pages = refs[:n_pg]
    new_ref, mask_ref, u_ref = refs[n_pg:]
    j = pl.program_id(1)
    q = qi_ref[0]
    w = wi_ref[0]

    def scores(kc):
        rel = jnp.maximum(_dot(q, kc.astype(BF16)), 0.0) * w
        return rel.reshape(tpad, IDX_HEADS, LANES).sum(axis=1)

    @pl.when(j < n_steps - 1)
    def _():
        for i in range(n_pg):
            start = pl.multiple_of((j * n_pg + i) * LANES, LANES)
            u_ref[:, pl.ds(start, LANES)] = _sortable(scores(pages[i][0]))

    @pl.when(j == n_steps - 1)
    def _():
        s = scores(new_ref[0])
        t_idx = lax.broadcasted_iota(I32, (tpad, LANES), 0)
        c_idx = lax.broadcasted_iota(I32, (tpad, LANES), 1)
        s = jnp.where((c_idx <= t_idx) & (c_idx < n_new), s, -jnp.inf)
        base = (n_steps - 1) * n_pg * LANES
        u_ref[:, base:base + LANES] = _sortable(s)
        for i in range(1, n_pg):
            u_ref[:, base + i * LANES:base + (i + 1) * LANES] = jnp.full((tpad, LANES), KEY_NEG_INF, I32)

        def write(ci, sel, u):
            keep = sel & (u > KEY_NEG_INF) & (u < KEY_POS_INF)
            mask_ref[0, :, ci * LANES:(ci + 1) * LANES] = jnp.where(keep, 1.0, 0.0)

        _emit_selection(u_ref, topk, write)


def _dsa_select_sample(page_table, qi, wi, pool_kidx, new_ki, n_new, n_pg=8):
    bsz, n_pages = page_table.shape
    tpad = qi.shape[1] // IDX_HEADS
    n_steps = n_pages // n_pg + 1
    width = n_steps * n_pg * LANES
    topk = min(IDX_TOPK, (n_pages * PAGE_SIZE + n_new) // 4)
    fix = lambda b, j, pt: (b, 0, 0)
    grid_spec = pltpu.PrefetchScalarGridSpec(
        num_scalar_prefetch=1,
        grid=(bsz, n_steps),
        in_specs=[pl.BlockSpec((1, tpad * IDX_HEADS, IDX_DIM), fix), pl.BlockSpec((1, tpad * IDX_HEADS, 1), fix)]
        + [pl.BlockSpec((1, IDX_DIM, PAGE_SIZE), _page_map(i, n_pg, n_pages)) for i in range(n_pg)]
        + [pl.BlockSpec((1, IDX_DIM, PAGE_SIZE), fix)],
        out_specs=pl.BlockSpec((1, tpad, width), fix),
        scratch_shapes=[pltpu.VMEM((tpad, width), I32)],
    )
    return pl.pallas_call(
        functools.partial(_dsa_select_sample_kernel, n_pg=n_pg, n_steps=n_steps, tpad=tpad, n_new=n_new, topk=topk),
        grid_spec=grid_spec,
        out_shape=jax.ShapeDtypeStruct((bsz, tpad, width), F32),
        compiler_params=_params("parallel", "arbitrary"),
        name="dsa_select_sample",
    )(page_table, qi, wi, *([pool_kidx] * n_pg), new_ki)


def _page_map(i, n_pg, n_pages):
    return lambda b, j, pt: (pt[b, jnp.minimum(j * n_pg + i, n_pages - 1)], 0, 0)


def _page_map4(i, n_pg, n_pages):
    return lambda b, j, pt: (pt[b, jnp.minimum(j * n_pg + i, n_pages - 1)], 0, 0, 0)


def _paged_flash_kernel(pt_ref, q_ref, *refs, n_pg, n_steps, nb):
    pages = refs[:nb * n_pg]
    new_ref, bias_ref, o_ref, m_ref, l_ref, acc_ref = refs[nb * n_pg:]
    j = pl.program_id(1)

    @pl.when(j == 0)
    def _():
        m_ref[...] = jnp.full(m_ref.shape, NEG, F32)
        l_ref[...] = jnp.zeros(l_ref.shape, F32)
        acc_ref[...] = jnp.zeros(acc_ref.shape, F32)

    def update(bi, kv):
        q = q_ref[bi]
        s = [_dot(q, k().astype(BF16)) + bias_ref[bi, :, i * LANES:(i + 1) * LANES] for i, (k, _) in enumerate(kv)]
        m_prev = m_ref[bi]
        m_new = m_prev
        for si in s:
            m_new = jnp.maximum(m_new, jnp.max(si, axis=1, keepdims=True))
        alpha = jnp.exp(m_prev - m_new)
        l_new = alpha * l_ref[bi]
        acc = acc_ref[bi] * pltpu.repeat(alpha, KV_COLS // LANES, axis=1)
        for si, (_, v) in zip(s, kv):
            p = jnp.where(si > 0.5 * NEG, jnp.exp(si - m_new), 0.0)
            l_new = l_new + jnp.sum(p, axis=1, keepdims=True)
            acc = acc + _dot_t(p.astype(BF16), v().astype(BF16))
        l_ref[bi] = l_new
        acc_ref[bi] = acc
        m_ref[bi] = m_new

    def loaders(ref, lead):
        return (lambda: ref[lead, 0]), (lambda: ref[lead, 1])

    @pl.when(j < n_steps - 1)
    def _():
        for bi in range(nb):
            update(bi, [loaders(pages[bi * n_pg + i], 0) for i in range(n_pg)])

    @pl.when(j == n_steps - 1)
    def _():
        for bi in range(nb):
            update(bi, [loaders(new_ref, bi)])
        o_ref[...] = acc_ref[...] / jnp.maximum(jnp.concatenate([l_ref[...]] * (KV_COLS // LANES), axis=-1), 1e-30)


def _paged_flash(page_table, q_bd, pool_t, new_page_t, bias, n_pg):
    bsz, n_pages = page_table.shape
    n_steps = n_pages // n_pg + 1
    rows = q_bd.shape[1]
    nb = 4 if bsz % 4 == 0 else (2 if bsz % 2 == 0 else 1)
    fix = lambda b, j, pt: (b, 0, 0)
    fix4 = lambda b, j, pt: (b, 0, 0, 0)
    page_block = (1, 2, KV_COLS, PAGE_SIZE)

    def page_map(bi, i):
        return lambda b, j, pt: (pt[b * nb + bi, jnp.minimum(j * n_pg + i, n_pages - 1)], 0, 0, 0)

    grid_spec = pltpu.PrefetchScalarGridSpec(
        num_scalar_prefetch=1,
        grid=(bsz // nb, n_steps),
        in_specs=[pl.BlockSpec((nb, rows, KV_COLS), fix)]
        + [pl.BlockSpec(page_block, page_map(bi, i)) for bi in range(nb) for i in range(n_pg)]
        + [pl.BlockSpec((nb, 2, KV_COLS, PAGE_SIZE), fix4),
           pl.BlockSpec((nb, rows, n_pg * LANES), lambda b, j, pt: (b, 0, j))],
        out_specs=pl.BlockSpec((nb, rows, KV_COLS), fix),
        scratch_shapes=[pltpu.VMEM((nb, rows, LANES), F32), pltpu.VMEM((nb, rows, LANES), F32),
                        pltpu.VMEM((nb, rows, KV_COLS), F32)],
    )
    return pl.pallas_call(
        functools.partial(_paged_flash_kernel, n_pg=n_pg, n_steps=n_steps, nb=nb),
        grid_spec=grid_spec,
        out_shape=jax.ShapeDtypeStruct((bsz, rows, KV_COLS), F32),
        compiler_params=_params("parallel", "arbitrary"),
        name="paged_flash",
    )(page_table, q_bd, *([pool_t] * (nb * n_pg)), new_page_t, bias)


def _cumsum_kernel(pt_ref, *refs, n_pg, n_steps):
    pages = refs[:n_pg]
    new_ref, o_ref, carry_ref = refs[n_pg:]
    j = pl.program_id(1)
    r_i = lax.broadcasted_iota(I32, (LANES, LANES), 0)
    c_i = lax.broadcasted_iota(I32, (LANES, LANES), 1)
    tri = jnp.where(r_i <= c_i, 1.0, 0.0).astype(BF16)

    @pl.when(j == 0)
    def _():
        carry_ref[...] = jnp.zeros(carry_ref.shape, F32)

    def step(x, i):
        c = _dot_hp(x, tri) + carry_ref[...]
        o_ref[0, :, i * LANES:(i + 1) * LANES] = c
        carry_ref[...] = jnp.broadcast_to(c[:, LANES - 1:LANES], c.shape)

    @pl.when(j < n_steps - 1)
    def _():
        for i in range(n_pg):
            step(pages[i][0], i)

    @pl.when(j == n_steps - 1)
    def _():
        step(new_ref[0], 0)
        for i in range(1, n_pg):
            o_ref[0, :, i * LANES:(i + 1) * LANES] = jnp.zeros((N_HEADS, LANES), F32)


def _paged_cumsum(page_table, pool_t, new_t, n_pg):
    bsz, n_pages = page_table.shape
    n_steps = n_pages // n_pg + 1
    fix = lambda b, j, pt: (b, 0, 0)
    grid_spec = pltpu.PrefetchScalarGridSpec(
        num_scalar_prefetch=1,
        grid=(bsz, n_steps),
        in_specs=[pl.BlockSpec((1, N_HEADS, LANES), _page_map(i, n_pg, n_pages)) for i in range(n_pg)]
        + [pl.BlockSpec((1, N_HEADS, LANES), fix)],
        out_specs=pl.BlockSpec((1, N_HEADS, n_pg * LANES), lambda b, j, pt: (b, 0, j)),
        scratch_shapes=[pltpu.VMEM((N_HEADS, LANES), F32)],
    )
    return pl.pallas_call(
        functools.partial(_cumsum_kernel, n_pg=n_pg, n_steps=n_steps),
        grid_spec=grid_spec,
        out_shape=jax.ShapeDtypeStruct((bsz, N_HEADS, n_steps * n_pg * LANES), F32),
        compiler_params=_params("parallel", "arbitrary"),
        name="fox_cumsum",
    )(page_table, *([pool_t] * n_pg), new_t)


def _gelu_tanh(x):
    return 0.5 * x * (1.0 + jnp.tanh(0.7978845608028654 * (x + 0.044715 * x * x * x)))


def _nsa_compress_kernel(pt_ref, *refs, n_pg, n_steps, nc):
    pages = refs[:n_pg]
    pe_ref, w1_ref, w2_ref, o_ref, x_ref = refs[n_pg:]
    j = pl.program_id(1)
    per = PAGE_SIZE // CMP_STRIDE
    n_chunk = x_ref.shape[1]
    for i in range(0, n_pg, 2):
        start = pl.multiple_of((j * n_pg + i) * per, 2 * per)
        for ck in range(2 * N_KV_HEADS):
            x_ref[ck, pl.ds(start, 2 * per), :] = jnp.concatenate([pages[i][0, ck], pages[i + 1][0, ck]], axis=0)

    @pl.when(j == n_steps - 1)
    def _():
        half = CMP_STRIDE * HEAD_DIM
        rows = N_KV_HEADS * n_chunk
        row = lax.broadcasted_iota(I32, (N_KV_HEADS, n_chunk, HEAD_DIM), 1)
        for c in range(2):
            w1 = w1_ref[c]
            part = _dot(x_ref[c * N_KV_HEADS:(c + 1) * N_KV_HEADS].reshape(rows, half), w1)
            pe = pe_ref[c]
            pe_term = _dot(pe[:, :half], w1)[:, :CMP_HID] + _dot(pe[:, half:], w1)[:, CMP_HID:]
            h = pe_term[0:1, :] + part[:, :CMP_HID] + pltpu.roll(part[:, CMP_HID:], rows - 1, 0)
            out = _dot(_gelu_tanh(h).astype(BF16), w2_ref[c]).reshape(N_KV_HEADS, n_chunk, HEAD_DIM)
            o_ref[0, c * N_KV_HEADS:(c + 1) * N_KV_HEADS] = jnp.where(row < nc, out, 0.0)


def _nsa_compress(page_table, pool_t, pe8, w1cat, w2, n_pg=8):
    bsz, n_pages = page_table.shape
    per = PAGE_SIZE // CMP_STRIDE
    n_chunk = n_pages * per
    nc = n_chunk - CMP_LEN // CMP_STRIDE + 1
    n_steps = n_pages // n_pg
    width = CMP_STRIDE * HEAD_DIM

    def page_map(i):
        return lambda b, j, pt: (pt[b, j * n_pg + i], 0, 0, 0)

    fix3 = lambda b, j, pt: (0, 0, 0)
    grid_spec = pltpu.PrefetchScalarGridSpec(
        num_scalar_prefetch=1,
        grid=(bsz, n_steps),
        in_specs=[pl.BlockSpec((1, 2 * N_KV_HEADS, per, width), page_map(i)) for i in range(n_pg)]
        + [pl.BlockSpec(pe8.shape, fix3), pl.BlockSpec(w1cat.shape, fix3), pl.BlockSpec(w2.shape, fix3)],
        out_specs=pl.BlockSpec((1, 2 * N_KV_HEADS, n_chunk, HEAD_DIM), lambda b, j, pt: (b, 0, 0, 0)),
        scratch_shapes=[pltpu.VMEM((2 * N_KV_HEADS, n_chunk, width), BF16)],
    )
    return pl.pallas_call(
        functools.partial(_nsa_compress_kernel, n_pg=n_pg, n_steps=n_steps, nc=nc),
        grid_spec=grid_spec,
        out_shape=jax.ShapeDtypeStruct((bsz, 2 * N_KV_HEADS, n_chunk, HEAD_DIM), F32),
        compiler_params=_params("parallel", "arbitrary"),
        name="nsa_compress",
    )(page_table, *([pool_t] * n_pg), pe8, w1cat, w2), nc


def _nsa_cmp_kernel(q_ref, ck_ref, cv_ref, cover_ref, o_ref, sel_ref, u_ref, *, tq, pos0, nc, n_sel):
    t0 = pos0 + pl.program_id(1) * tq
    ncp = ck_ref.shape[2]
    nsp = cover_ref.shape[0]
    rows = GROUP * tq
    n_idx = lax.broadcasted_iota(I32, (tq, ncp), 1)
    t_idx = t0 + lax.broadcasted_iota(I32, (tq, ncp), 0)
    c_ok = ((n_idx * CMP_STRIDE + CMP_LEN - 1 <= t_idx) & (n_idx < nc))[None]
    blk = lax.broadcasted_iota(I32, (nsp, tq), 0)
    cur = (t0 + lax.broadcasted_iota(I32, (nsp, tq), 1)) // SEL_BLOCK
    forced = (blk == 0) | (blk == cur) | (blk == cur - 1)
    cover_t = cover_ref[...]
    for g in range(N_KV_HEADS):
        q = q_ref[0, g].reshape(rows, HEAD_DIM)
        s3 = jnp.where(c_ok, _dot_t(q, ck_ref[0, g]).reshape(GROUP, tq, ncp), NEG)
        m = jnp.max(s3, axis=-1, keepdims=True)
        e = jnp.where(c_ok, jnp.exp(s3 - m), 0.0)
        p = e / jnp.maximum(jnp.sum(e, axis=-1, keepdims=True), 1e-30)
        o = _dot(p.reshape(rows, ncp).astype(BF16), cv_ref[0, g])
        o_ref[0, g] = o.reshape(GROUP, tq, HEAD_DIM)
        psum = p[0] + p[1] + p[2] + p[3]
        hi = psum.astype(BF16)
        r1 = psum - hi.astype(F32)
        mid = r1.astype(BF16)
        lo = (r1 - mid.astype(F32)).astype(BF16)
        imp = _dot_t(cover_t, hi) + _dot_t(cover_t, mid) + _dot_t(cover_t, lo)
        imp = jnp.where(forced, jnp.inf, imp)
        imp = jnp.where(blk <= cur, imp, -jnp.inf)
        u_ref[:, g * tq:(g + 1) * tq] = _sortable(imp)
    sel, u = _topk_mask_cols(u_ref, n_sel)
    sel_ref[0, 0] = jnp.where(sel & (u > KEY_NEG_INF), 1.0, 0.0)


def _nsa_cmp_select(q, cmp_k, cmp_v, pos0, nc, n_keys, tq):
    bsz, _, _, t, _ = q.shape
    ncp = cmp_k.shape[2]
    ns = -(-n_keys // SEL_BLOCK)
    nsp = -(-ns // LANES) * LANES
    n_sel = min(SEL_TOPN, ns)
    c0 = np.arange(ncp)[:, None] * CMP_STRIDE
    s0 = np.arange(nsp)[None, :] * SEL_BLOCK
    cover = (c0 <= s0 + SEL_BLOCK - 1) & (c0 + CMP_LEN - 1 >= s0) & (np.arange(ncp)[:, None] < nc) & (np.arange(nsp)[None, :] < ns)
    cover_t = jnp.asarray(cover.T.astype(np.float32), dtype=BF16)
    nq = t // tq
    qspec = pl.BlockSpec((1, N_KV_HEADS, GROUP, tq, HEAD_DIM), lambda b, i: (b, 0, 0, i, 0))
    cspec = pl.BlockSpec((1, N_KV_HEADS, ncp, HEAD_DIM), lambda b, i: (b, 0, 0, 0))
    o_c, sel = pl.pallas_call(
        functools.partial(_nsa_cmp_kernel, tq=tq, pos0=pos0, nc=nc, n_sel=n_sel),
        grid=(bsz, nq),
        in_specs=[qspec, cspec, cspec, pl.BlockSpec((nsp, ncp), lambda b, i: (0, 0))],
        out_specs=[qspec, pl.BlockSpec((1, 1, nsp, N_KV_HEADS * tq), lambda b, i: (b, i, 0, 0))],
        out_shape=[jax.ShapeDtypeStruct(q.shape, F32), jax.ShapeDtypeStruct((bsz, nq, nsp, N_KV_HEADS * tq), F32)],
        scratch_shapes=[pltpu.VMEM((nsp, N_KV_HEADS * tq), I32)],
        compiler_params=_params("parallel", "parallel"),
        name="nsa_cmp_select",
    )(q, cmp_k, cmp_v, cover_t)
    sel = sel.reshape(bsz, nq, nsp, N_KV_HEADS, tq).transpose(0, 3, 1, 4, 2).reshape(bsz, N_KV_HEADS, t, nsp)
    return o_c, sel


def _kmean_kernel(*refs):
    o_ref = refs[-1]
    tot = jnp.sum(refs[0][0], axis=0, keepdims=True)
    for r in refs[1:-1]:
        tot = tot + jnp.sum(r[0], axis=0, keepdims=True)
    o_ref[0, 0] = tot * (1.0 / MOBA_BLOCK)


def _kmean_prompt(kv):
    bsz, t, _ = kv.shape
    nb = t // MOBA_BLOCK
    return pl.pallas_call(
        _kmean_kernel,
        grid=(bsz, nb),
        in_specs=[pl.BlockSpec((1, MOBA_BLOCK, KV_COLS), lambda b, i: (b, i, 0))],
        out_specs=pl.BlockSpec((1, 1, 1, KV_COLS), lambda b, i: (b, i, 0, 0)),
        out_shape=jax.ShapeDtypeStruct((bsz, nb, 1, KV_COLS), F32),
        compiler_params=_params("parallel", "parallel"),
        name="kmean_prompt",
    )(kv)


def _kmean_sample_kernel(pt_ref, *refs, per):
    o_ref = refs[-1]
    ones = jnp.ones((8, PAGE_SIZE), BF16)
    for n in range(len(refs[:-1]) // per):
        tot = jnp.zeros((8, KV_COLS), F32)
        for r in refs[n * per:(n + 1) * per]:
            x = r[0, 0]
            hi = x.astype(BF16)
            r1 = x - hi.astype(F32)
            mid = r1.astype(BF16)
            lo = (r1 - mid.astype(F32)).astype(BF16)
            tot = tot + _dot_t(ones, hi) + _dot_t(ones, mid) + _dot_t(ones, lo)
        o_ref[0, n] = tot[0:1] * (1.0 / MOBA_BLOCK)


def _kmean_sample(page_table, pool_t, n_pg):
    bsz, n_pages = page_table.shape
    per = MOBA_BLOCK // PAGE_SIZE
    nb = n_pages // per
    grid_spec = pltpu.PrefetchScalarGridSpec(
        num_scalar_prefetch=1,
        grid=(bsz, n_pages // n_pg),
        in_specs=[pl.BlockSpec((1, 1, KV_COLS, PAGE_SIZE), (lambda i: (lambda b, n, pt: (pt[b, n * n_pg + i], 0, 0, 0)))(i))
                  for i in range(n_pg)],
        out_specs=pl.BlockSpec((1, n_pg // per, 1, KV_COLS), lambda b, n, pt: (b, n, 0, 0)),
    )
    return pl.pallas_call(
        functools.partial(_kmean_sample_kernel, per=per),
        grid_spec=grid_spec,
        out_shape=jax.ShapeDtypeStruct((bsz, nb, 1, KV_COLS), F32),
        compiler_params=_params("parallel", "parallel"),
        name="kmean_sample",
    )(page_table, *([pool_t] * n_pg))


def _moba_select_kernel(q_ref, km_ref, sel_ref, *, tq, pos0, k_top):
    t0 = pos0 + pl.program_id(1) * tq
    nbp = km_ref.shape[2]
    rows = GROUP * tq
    blk = lax.broadcasted_iota(I32, (nbp, rows), 0)
    n_past = (t0 + (lax.broadcasted_iota(I32, (nbp, rows), 1) & (tq - 1))) // MOBA_BLOCK
    for g in range(N_KV_HEADS):
        q = q_ref[0, g].reshape(rows, HEAD_DIM)
        s = jnp.where(blk < n_past, _dot_t(km_ref[0, g], q), -jnp.inf)
        sel = jnp.zeros((nbp, rows), F32)
        for _ in range(k_top):
            m = jnp.max(s, axis=0, keepdims=True)
            first = jnp.min(jnp.where(s == m, blk, nbp), axis=0, keepdims=True)
            pick = blk == first
            sel = jnp.where(pick & (m > -jnp.inf), 1.0, sel)
            s = jnp.where(pick, -jnp.inf, s)
        sel_ref[0, 0, g] = sel


def _moba_select(q, kmean, pos0, nb, tq):
    bsz, _, _, t, _ = q.shape
    nbp = kmean.shape[2]
    nq = t // tq
    qspec = pl.BlockSpec((1, N_KV_HEADS, GROUP, tq, HEAD_DIM), lambda b, i: (b, 0, 0, i, 0))
    sel = pl.pallas_call(
        functools.partial(_moba_select_kernel, tq=tq, pos0=pos0, k_top=min(MOBA_TOPK, nb)),
        grid=(bsz, nq),
        in_specs=[qspec, pl.BlockSpec((1, N_KV_HEADS, nbp, HEAD_DIM), lambda b, i: (b, 0, 0, 0))],
        out_specs=pl.BlockSpec((1, 1, N_KV_HEADS, nbp, GROUP * tq), lambda b, i: (b, i, 0, 0, 0)),
        out_shape=jax.ShapeDtypeStruct((bsz, nq, N_KV_HEADS, nbp, GROUP * tq), F32),
        compiler_params=_params("parallel", "parallel"),
        name="moba_select",
    )(q, kmean)
    sel = sel.reshape(bsz, nq, N_KV_HEADS, nbp, GROUP, tq).transpose(0, 2, 4, 1, 5, 3)
    return sel.reshape(bsz, N_KV_HEADS, GROUP, t, nbp)


def _rope_tables(pos):
    half = HEAD_DIM // 2
    inv = ROPE_THETA ** (-jnp.arange(half, dtype=F32) / half)
    ang = pos.astype(F32)[:, None] * inv[None, :]
    cos, sin = jnp.cos(ang), jnp.sin(ang)
    rep = LANES // HEAD_DIM
    return jnp.concatenate([cos, cos] * rep, axis=1), jnp.concatenate([-sin, sin] * rep, axis=1)


def _pad_to(x, axis, size):
    pad = [(0, 0)] * x.ndim
    pad[axis] = (0, size - x.shape[axis])
    return jnp.pad(x, pad)


def _q_groups(q):
    b, t = q.shape[:2]
    return q.transpose(0, 2, 1, 3).reshape(b, N_KV_HEADS, GROUP, t, HEAD_DIM)


def _kv_major(kv):
    b, t = kv.shape[:2]
    k = kv[:, :, 0].transpose(0, 2, 1, 3).astype(BF16)
    v = kv[:, :, 1].transpose(0, 2, 1, 3).astype(BF16)
    ones = jnp.ones((b, N_KV_HEADS, t, 1), BF16)
    zeros = jnp.zeros((b, N_KV_HEADS, t, LANES - HEAD_DIM - 1), BF16)
    return k, jnp.concatenate([v, ones, zeros], axis=-1)


def _pages_t(kv):
    return kv.transpose(0, 2, 3, 4, 1).reshape(kv.shape[0], 2, KV_COLS, PAGE_SIZE)


def _fit(x, width):
    return x[..., :width] if x.shape[-1] >= width else _pad_to(x, x.ndim - 1, width)


def _merge_heads(o):
    b, _, _, t, _ = o.shape
    return o.reshape(b, N_HEADS, t, HEAD_DIM).transpose(0, 2, 1, 3).reshape(b * t, Q_DIM)


_HEAD_TO_GROUP = np.equal(np.arange(N_HEADS)[:, None] // GROUP, np.arange(N_KV_HEADS)[None, :]).astype(np.float32)


def _q_block_diag(q):
    s, tn = q.shape[:2]
    qb = q[:, :, :, None, :] * jnp.asarray(_HEAD_TO_GROUP, dtype=q.dtype)[None, None, :, :, None]
    return qb.reshape(s, tn * N_HEADS, KV_COLS)


def _extract_block_diag(o, tn):
    s = o.shape[0]
    o5 = o.reshape(s, tn, N_HEADS, N_KV_HEADS, HEAD_DIM) * _HEAD_TO_GROUP[None, None, :, :, None]
    return o5.sum(axis=3).reshape(s * tn, Q_DIM)


def _new_page(kv_new):
    return _pages_t(_pad_to(kv_new, 1, PAGE_SIZE))


def _pick_pages(n_pages):
    for n in (8, 4, 2, 1):
        if n_pages % n == 0:
            return n


def _identity_pages(bsz, n_pages):
    return jnp.arange(bsz * n_pages, dtype=I32).reshape(bsz, n_pages)


def _split(proj, sizes):
    out, o = [], 0
    for s in sizes:
        out.append(proj[:, o:o + s])
        o += s
    return out


def _rows_th(ok, s, tn):
    return jnp.where(ok, 0.0, NEG).astype(F32).reshape(s, tn * N_HEADS, ok.shape[-1])


def _kv_stack(k, v, bsz, t):
    return jnp.stack([k.reshape(bsz, t, N_KV_HEADS, HEAD_DIM), v.reshape(bsz, t, N_KV_HEADS, HEAD_DIM)], axis=2)


def _dsa_project(x, bsz, t, w_in, cs, q_scale):
    o = 0
    segs = []
    for width, rope, scale, dt in ((Q_DIM, True, q_scale, BF16), (KV_COLS, True, 1.0, F32), (KV_COLS, False, 1.0, F32),
                                   (IDX_HEADS * IDX_DIM, True, 1.0, BF16), (IDX_DIM, True, 1.0, F32),
                                   (IDX_HEADS, False, IDX_SCALE, F32)):
        segs.append((o, width, rope, scale, dt))
        o += width
    q, k, v, qi, ki, wi = _proj(x, w_in, cs, segs)
    return (q.reshape(bsz, t, N_HEADS, HEAD_DIM), qi.reshape(bsz, t, IDX_HEADS, IDX_DIM), wi.reshape(bsz, t, IDX_HEADS),
            _kv_stack(k, v, bsz, t), ki.reshape(bsz, t, IDX_DIM))


def _dsa_prompt(x, bsz, t, w_in, cs):
    q, qi, wi, kv, ki = _dsa_project(x, bsz, t, w_in, cs, ATTN_SCALE * LOG2E)
    mask = _dsa_select_prompt(qi.transpose(0, 2, 1, 3), ki.astype(BF16), wi)
    kh, vh = _kv_major(kv)
    o = _flash_prompt("dsa", _q_groups(q), kh, vh, [mask])
    return _merge_heads(o), kv, ki


def _dsa_sample(x, s, tn, w_in, cs, page_table, cache_kv, cache_kidx, n_pg):
    q, qi, wi, kv, ki = _dsa_project(x, s, tn, w_in, cs, ATTN_SCALE)
    tpad = 8
    qi_p = _pad_to(qi, 1, tpad).reshape(s, tpad * IDX_HEADS, IDX_DIM)
    wi_p = _pad_to(wi, 1, tpad).reshape(s, tpad * IDX_HEADS, 1)
    mask = _dsa_select_sample(page_table, qi_p, wi_p, cache_kidx.transpose(0, 2, 1),
                              _pad_to(ki, 1, PAGE_SIZE).transpose(0, 2, 1), tn, n_pg)
    ok = jnp.broadcast_to(mask[:, :tn, None, :] > 0.5, (s, tn, N_HEADS, mask.shape[-1]))
    o = _paged_flash(page_table, _q_block_diag(q), _pages_t(cache_kv), _new_page(kv), _rows_th(ok, s, tn), n_pg)
    return _extract_block_diag(o, tn), kv, ki


def _fox_project(x, bsz, t, w_in, b_f, cs, q_scale):
    segs = [(0, Q_DIM, False, q_scale, BF16), (Q_DIM, KV_COLS, False, 1.0, F32), (Q_DIM + KV_COLS, KV_COLS, False, 1.0, F32),
            (Q_DIM + 2 * KV_COLS, N_HEADS, False, 1.0, F32)]
    q, k, v, f = _proj(x, w_in, cs, segs)
    logf = jax.nn.log_sigmoid(f.reshape(bsz, t, N_HEADS) + b_f)
    return q.reshape(bsz, t, N_HEADS, HEAD_DIM), _kv_stack(k, v, bsz, t), logf


def _fox_prompt(x, bsz, t, w_in, b_f, cs):
    q, kv, logf = _fox_project(x, bsz, t, w_in, b_f, cs, ATTN_SCALE * LOG2E)
    n_pages = t // LANES
    pool_t = logf.reshape(bsz, n_pages, LANES, N_HEADS).transpose(0, 1, 3, 2).reshape(bsz * n_pages, N_HEADS, LANES)
    c = _paged_cumsum(_identity_pages(bsz, n_pages), pool_t, jnp.zeros((bsz, N_HEADS, LANES), F32), _pick_pages(n_pages))
    c = (c[:, :, :t] * LOG2E).reshape(bsz, N_KV_HEADS, GROUP, t)
    kh, vh = _kv_major(kv)
    o = _flash_prompt("fox", _q_groups(q), kh, vh, [c])
    return _merge_heads(o), kv, logf


def _fox_sample(x, s, tn, w_in, b_f, cs, page_table, cache_kv, cache_logf, n_pg):
    q, kv, logf = _fox_project(x, s, tn, w_in, b_f, cs, ATTN_SCALE)
    past = page_table.shape[1] * PAGE_SIZE
    c = _paged_cumsum(page_table, cache_logf.transpose(0, 2, 1), _pad_to(logf.transpose(0, 2, 1), 2, LANES), n_pg)
    col = jnp.arange(c.shape[-1])
    valid = (col[None, :] < past) | ((col[None, :] - past <= jnp.arange(tn)[:, None]) & (col[None, :] < past + tn))
    bias = jnp.where(valid[None, :, None, :], -c[:, None, :, :], NEG).reshape(s, tn * N_HEADS, c.shape[-1])
    o = _paged_flash(page_table, _q_block_diag(q), _pages_t(cache_kv), _new_page(kv), bias, n_pg)
    return _extract_block_diag(o, tn), kv, logf


def _nsa_project(x, bsz, t, w_in, b_gate, cs, q_scale):
    c = Q_DIM
    segs = [(0, Q_DIM, False, ATTN_SCALE, BF16), (0, Q_DIM, True, q_scale, BF16)]
    for rope in (False, False, True, False, True, False):
        segs.append((c, KV_COLS, rope, 1.0, F32))
        c += KV_COLS
    segs.append((c, 3 * N_HEADS, False, 1.0, F32))
    q, q_rot, kc, vc, ks, vs, kw, vw, g = _proj(x, w_in, cs, segs)
    gate = jax.nn.sigmoid(g.reshape(bsz, t, 3 * N_HEADS) + b_gate).reshape(bsz, t, 3, N_HEADS)
    return (q.reshape(bsz, t, N_HEADS, HEAD_DIM), q_rot.reshape(bsz, t, N_HEADS, HEAD_DIM), gate,
            _kv_stack(kc, vc, bsz, t), _kv_stack(ks, vs, bsz, t), _kv_stack(kw, vw, bsz, t))


def _nsa_weights(pe, w1, w2):
    r = CMP_LEN // CMP_STRIDE
    w1cat = w1.reshape(2, r, CMP_STRIDE * HEAD_DIM, CMP_HID).transpose(0, 2, 1, 3).reshape(2, CMP_STRIDE * HEAD_DIM, r * CMP_HID)
    pe8 = jnp.broadcast_to(pe.reshape(2, 1, CMP_LEN * HEAD_DIM), (2, 8, CMP_LEN * HEAD_DIM))
    return pe8.astype(BF16), w1cat.astype(BF16), w2.astype(BF16)


def _chunk_pages(kv):
    n = kv.shape[0]
    per = PAGE_SIZE // CMP_STRIDE
    x = kv.reshape(n, per, CMP_STRIDE, 2 * N_KV_HEADS, HEAD_DIM).transpose(0, 3, 1, 2, 4)
    return x.reshape(n, 2 * N_KV_HEADS, per, CMP_STRIDE * HEAD_DIM).astype(BF16)


def _gate_mix(gate, o_c, o_s, o_w):
    b, t = gate.shape[:2]
    g = gate.transpose(0, 2, 3, 1).reshape(b, 3, N_KV_HEADS, GROUP, t, 1)
    return g[:, 0] * o_c + g[:, 1] * o_s + g[:, 2] * o_w


def _nsa_prompt(x, bsz, t, w_in, b_gate, cmp_w, cs, tq=128, tk=512):
    q, q_rot, gate, kv_cmp, kv_slc, kv_win = _nsa_project(x, bsz, t, w_in, b_gate, cs, ATTN_SCALE * LOG2E)
    n_pages = t // PAGE_SIZE
    cmp, nc = _nsa_compress(_identity_pages(bsz, n_pages), _chunk_pages(kv_cmp.reshape(bsz * n_pages, PAGE_SIZE, 2, N_KV_HEADS, HEAD_DIM)),
                            *cmp_w, n_pg=_pick_pages(n_pages))
    cmp = cmp.astype(BF16)
    tq, tk = min(tq, t), min(tk, t)
    o_c, selblk = _nsa_cmp_select(_q_groups(q), cmp[:, :N_KV_HEADS], cmp[:, N_KV_HEADS:], 0, nc, t, tq)
    qg = _q_groups(q_rot)
    e3 = _expand_matrix(selblk.shape[-1], SEL_BLOCK, t, tk)
    o_s = _flash_prompt("nsa_sel", qg, *_kv_major(kv_slc), [selblk, e3], tq=2 * tq, tk=tk)
    o_w = _flash_prompt("nsa_win", qg, *_kv_major(kv_win), [], tk=tk)
    o = _gate_mix(gate, o_c, o_s, o_w)
    return _merge_heads(o), kv_cmp, kv_slc, kv_win[:, -min(WINDOW, t):]


def _nsa_sample(x, s, tn, w_in, b_gate, cmp_w, cs, page_table, cache_cmp, cache_slc, state_win, n_pg):
    q, q_rot, gate, kv_cmp, kv_slc, kv_win = _nsa_project(x, s, tn, w_in, b_gate, cs, ATTN_SCALE)
    past = page_table.shape[1] * PAGE_SIZE
    tpad = 8
    cmp, nc = _nsa_compress(page_table, _chunk_pages(cache_cmp), *cmp_w, n_pg=n_pg)
    cmp = cmp.astype(BF16)
    o_c, selblk = _nsa_cmp_select(_q_groups(_pad_to(q, 1, tpad)), cmp[:, :N_KV_HEADS], cmp[:, N_KV_HEADS:],
                                  past, nc, past + tn, tpad)
    o_c = o_c[:, :, :, :tn]
    qbd = _q_block_diag(q_rot)
    width = (page_table.shape[1] // n_pg + 1) * n_pg * LANES
    col = jnp.arange(width)
    pos = past + jnp.arange(tn)
    sel_key = jnp.repeat(selblk[:, :, :tn, :-(-width // SEL_BLOCK)], SEL_BLOCK, axis=-1)[..., :width]
    ok = (sel_key > 0.5) & (col[None, :] <= pos[:, None])[None, None]
    ok = jnp.broadcast_to(ok.transpose(0, 2, 1, 3)[:, :, :, None, :], (s, tn, N_KV_HEADS, GROUP, width))
    o_s = _paged_flash(page_table, qbd, _pages_t(cache_slc), _new_page(kv_slc),
                       _rows_th(ok.reshape(s, tn, N_HEADS, width), s, tn), n_pg)
    win_buf = state_win.shape[1]
    n_wp = win_buf // PAGE_SIZE
    wcol = jnp.arange(2 * n_wp * LANES)
    win_pos = jnp.where(wcol < win_buf, past - win_buf + wcol, jnp.where(wcol < win_buf + tn, past + wcol - win_buf, -1))
    w_ok = (win_pos[None, :] <= pos[:, None]) & (win_pos[None, :] >= pos[:, None] - WINDOW) & (win_pos[None, :] >= 0)
    w_ok = jnp.broadcast_to(w_ok[None, :, None, :], (s, tn, N_HEADS, wcol.shape[0]))
    win_pages = _pages_t(state_win.reshape(s * n_wp, PAGE_SIZE, 2, N_KV_HEADS, HEAD_DIM))
    o_w = _paged_flash(_identity_pages(s, n_wp), qbd, win_pages, _new_page(kv_win), _rows_th(w_ok, s, tn), n_wp)
    unbd = lambda o: _extract_block_diag(o, tn).reshape(s, tn, N_KV_HEADS, GROUP, HEAD_DIM).transpose(0, 2, 3, 1, 4)
    o = _gate_mix(gate, o_c, unbd(o_s), unbd(o_w))
    win = jnp.concatenate([state_win, kv_win], axis=1)[:, -win_buf:]
    return _merge_heads(o), kv_cmp, kv_slc, win


def _moba_project(x, bsz, t, w_in, cs, q_scale):
    segs = [(0, Q_DIM, True, q_scale, BF16), (Q_DIM, KV_COLS, True, 1.0, F32), (Q_DIM + KV_COLS, KV_COLS, False, 1.0, F32)]
    q, k, v = _proj(x, w_in, cs, segs)
    return q.reshape(bsz, t, N_HEADS, HEAD_DIM), _kv_stack(k, v, bsz, t)


def _kmean_heads(km):
    b, nb = km.shape[:2]
    return _pad_to(km.reshape(b, nb, N_KV_HEADS, HEAD_DIM).transpose(0, 2, 1, 3), 2, -(-nb // 16) * 16).astype(BF16)


def _moba_prompt(x, bsz, t, w_in, cs, tq=128, tk=512):
    q, kv = _moba_project(x, bsz, t, w_in, cs, ATTN_SCALE * LOG2E)
    tq, tk = min(tq, t), min(tk, t)
    nb = -(-t // MOBA_BLOCK)
    km = _kmean_heads(_kmean_prompt(kv.reshape(bsz, t, 2 * KV_COLS)))
    qg = _q_groups(q)
    sel = _moba_select(qg, km, 0, nb, tq)
    e3 = _expand_matrix(km.shape[2], MOBA_BLOCK, t, tk)
    o = _flash_prompt("moba", qg, *_kv_major(kv), [sel, e3], tq=2 * tq, tk=tk)
    return _merge_heads(o), kv


def _moba_sample(x, s, tn, w_in, cs, page_table, cache_kv, n_pg):
    q, kv = _moba_project(x, s, tn, w_in, cs, ATTN_SCALE)
    past = page_table.shape[1] * PAGE_SIZE
    tpad = 8
    pool = _pages_t(cache_kv)
    nb = -(-(past + tn) // MOBA_BLOCK)
    km = _kmean_heads(_kmean_sample(page_table, pool, n_pg))
    sel = _moba_select(_q_groups(_pad_to(q, 1, tpad)), km, past, nb, tpad)
    width = (page_table.shape[1] // n_pg + 1) * n_pg * LANES
    col = jnp.arange(width)
    pos = past + jnp.arange(tn)
    sel_key = _fit(jnp.repeat(sel[:, :, :, :tn], MOBA_BLOCK, axis=-1), width)
    own = (col[None, :] // MOBA_BLOCK == pos[:, None] // MOBA_BLOCK) & (col[None, :] <= pos[:, None])
    ok = (sel_key > 0.5) | own[None, None, None]
    ok = ok.transpose(0, 3, 1, 2, 4).reshape(s, tn, N_HEADS, width)
    o = _paged_flash(page_table, _q_block_diag(q), pool, _new_page(kv), _rows_th(ok, s, tn), n_pg)
    return _extract_block_diag(o, tn), kv


def _cast_w(w):
    return _pad_to(w, 1, -(-w.shape[1] // LANES) * LANES).astype(BF16)


def kernel(x_prompt, x_sample, cache_a_kv, cache_a_kidx, cache_b_kv, cache_b_logf, cache_c_cmp_kv, cache_c_slc_kv, state_c_win_kv, cache_d_kv, page_table, a_w_in, a_w_out, b_w_in, b_b_f, b_w_out, c_w_in, c_b_gate, c_cmp_pe, c_cmp_w1, c_cmp_w2, c_w_out, d_w_in, d_w_out, ln_g, ln_b, ffn_w_gu, ffn_w_down):
    bsz, t, d = x_prompt.shape
    s, tn, _ = x_sample.shape
    n_pages = page_table.shape[1]
    past = n_pages * PAGE_SIZE
    n_pg = _pick_pages(n_pages)
    cs_p = _rope_tables(jnp.arange(t, dtype=I32))
    cs_s = _rope_tables(jnp.tile(past + jnp.arange(tn, dtype=I32), s))
    xp = x_prompt.reshape(bsz * t, d)
    xs = x_sample.reshape(s * tn, d)
    cmp_w = _nsa_weights(c_cmp_pe, c_cmp_w1, c_cmp_w2)
    w_out = [_cast_w(w) for w in (a_w_out, b_w_out, c_w_out, d_w_out)]

    op, a_kv_p, a_kidx_p = _dsa_prompt(xp, bsz, t, _cast_w(a_w_in), cs_p)
    os_, a_kv_s, a_kidx_s = _dsa_sample(xs, s, tn, _cast_w(a_w_in), cs_s, page_table, cache_a_kv, cache_a_kidx, n_pg)

    def finish(i, xp, xs, op, os_):
        xp = _out_ln(op, w_out[i], xp, ln_g[i, 0], ln_b[i, 0])
        xs = _out_ln(os_, w_out[i], xs, ln_g[i, 0], ln_b[i, 0])
        wgu, wd = ffn_w_gu[i].astype(BF16), ffn_w_down[i].astype(BF16)
        xp = _ffn_ln(xp, wgu, wd, ln_g[i, 1], ln_b[i, 1])
        xs = _ffn_ln(xs, wgu, wd, ln_g[i, 1], ln_b[i, 1])
        return xp, xs

    xp, xs = finish(0, xp, xs, op, os_)

    op, b_kv_p, b_logf_p = _fox_prompt(xp, bsz, t, _cast_w(b_w_in), b_b_f, cs_p)
    os_, b_kv_s, b_logf_s = _fox_sample(xs, s, tn, _cast_w(b_w_in), b_b_f, cs_s, page_table, cache_b_kv, cache_b_logf, n_pg)
    xp, xs = finish(1, xp, xs, op, os_)

    op, c_cmp_kv_p, c_slc_kv_p, c_win_kv_p = _nsa_prompt(xp, bsz, t, _cast_w(c_w_in), c_b_gate, cmp_w, cs_p)
    os_, c_cmp_kv_s, c_slc_kv_s, c_win_kv_s = _nsa_sample(xs, s, tn, _cast_w(c_w_in), c_b_gate, cmp_w, cs_s, page_table,
                                                          cache_c_cmp_kv, cache_c_slc_kv, state_c_win_kv, n_pg)
    xp, xs = finish(2, xp, xs, op, os_)

    op, d_kv_p = _moba_prompt(xp, bsz, t, _cast_w(d_w_in), cs_p)
    os_, d_kv_s = _moba_sample(xs, s, tn, _cast_w(d_w_in), cs_s, page_table, cache_d_kv, n_pg)
    xp, xs = finish(3, xp, xs, op, os_)

    return (xp.reshape(bsz, t, d), xs.reshape(s, tn, d), a_kv_p, a_kv_s, a_kidx_p, a_kidx_s, b_kv_p, b_kv_s,
            b_logf_p, b_logf_s, c_cmp_kv_p, c_cmp_kv_s, c_slc_kv_p, c_slc_kv_s, c_win_kv_p, c_win_kv_s, d_kv_p, d_kv_s)
```

```python
import functools

import numpy as np
import jax
import jax.numpy as jnp
from jax import lax
from jax.experimental import pallas as pl
from jax.experimental.pallas import tpu as pltpu

F32 = jnp.float32
BF16 = jnp.bfloat16
I32 = jnp.int32

N_HEADS = 16
HEAD_DIM = 64
N_KV_HEADS = 4
GROUP = N_HEADS // N_KV_HEADS
Q_DIM = N_HEADS * HEAD_DIM
KV_COLS = N_KV_HEADS * HEAD_DIM
DEPTH = 4
PAGE_SIZE = 128
ROPE_THETA = 10000.0
LN_EPS = 1e-5
ALPHA = (2 * DEPTH) ** 0.25
ATTN_SCALE = HEAD_DIM ** -0.5
LOG2E = 1.4426950408889634
IDX_HEADS = 8
IDX_DIM = 64
IDX_TOPK = 256
IDX_SCALE = (IDX_HEADS * IDX_DIM) ** -0.5
CMP_LEN = 32
CMP_STRIDE = 16
CMP_HID = 2 * HEAD_DIM
SEL_BLOCK = 64
SEL_TOPN = 16
WINDOW = 512
MOBA_BLOCK = 256
MOBA_TOPK = 3

LANES = 128
VMEM_LIMIT = 56 * 2 ** 20
NEG = -1e30
KEY_NEG_INF = -2139095041
KEY_POS_INF = 2139095040
INT_MIN = -2 ** 31


def _params(*sem):
    return pltpu.CompilerParams(dimension_semantics=sem, vmem_limit_bytes=VMEM_LIMIT)


def _dot_t(a, b):
    return lax.dot_general(a, b, (((1,), (1,)), ((), ())), preferred_element_type=F32)


def _dot(a, b):
    return jnp.dot(a, b, preferred_element_type=F32)


def _dot_hp(a, b):
    hi = a.astype(BF16)
    r1 = a - hi.astype(F32)
    mid = r1.astype(BF16)
    lo = (r1 - mid.astype(F32)).astype(BF16)
    return _dot(hi, b) + _dot(mid, b) + _dot(lo, b)


def _mm_kernel(x_ref, w_ref, o_ref):
    o_ref[...] = _dot(x_ref[...].astype(BF16), w_ref[...])


def _matmul(x, w):
    m, k = x.shape
    n = w.shape[1]
    tm = min(m, 512)
    return pl.pallas_call(
        _mm_kernel,
        grid=(m // tm,),
        in_specs=[pl.BlockSpec((tm, k), lambda i: (i, 0)), pl.BlockSpec((k, n), lambda i: (0, 0))],
        out_specs=pl.BlockSpec((tm, n), lambda i: (i, 0)),
        out_shape=jax.ShapeDtypeStruct((m, n), F32),
        compiler_params=_params("parallel"),
        name="in_proj",
    )(x, w)


def _proj_kernel(x_ref, w_ref, cos_ref, sin_ref, *out_refs, segs):
    acc = _dot(x_ref[...].astype(BF16), w_ref[...])
    tm = acc.shape[0]
    lane = lax.broadcasted_iota(I32, (tm, LANES), 1)
    first_half = (lane & (HEAD_DIM - 1)) < HEAD_DIM // 2
    ones_col = jnp.where(lane == HEAD_DIM, 1.0, 0.0)
    for (c0, width, rope, scale, form), o_ref in zip(segs, out_refs):
        if c0 % LANES:
            o_ref[...] = (acc[:, c0:c0 + width] * scale).astype(o_ref.dtype)
            continue
        for j in range(-(-width // LANES)):
            x = acc[:, c0 + j * LANES:c0 + (j + 1) * LANES]
            if rope:
                swapped = jnp.where(first_half, pltpu.roll(x, LANES - HEAD_DIM // 2, 1), pltpu.roll(x, HEAD_DIM // 2, 1))
                x = x * cos_ref[...] + swapped * sin_ref[...]
            if scale != 1.0:
                x = x * scale
            if form == "tok":
                wj = min(LANES, width - j * LANES)
                o_ref[:, j * LANES:j * LANES + wj] = x[:, :wj].astype(o_ref.dtype)
            elif form == "heads":
                o_ref[0, 2 * j] = x[:, :HEAD_DIM].astype(o_ref.dtype)
                o_ref[0, 2 * j + 1] = x[:, HEAD_DIM:].astype(o_ref.dtype)
            elif form == "keys_t":
                xt = x.T
                o_ref[0, 2 * j] = xt[:HEAD_DIM].astype(o_ref.dtype)
                o_ref[0, 2 * j + 1] = xt[HEAD_DIM:].astype(o_ref.dtype)
            else:
                o_ref[0, 2 * j] = jnp.where(lane < HEAD_DIM, x, ones_col).astype(o_ref.dtype)
                o_ref[0, 2 * j + 1] = jnp.where(lane < HEAD_DIM, pltpu.roll(x, HEAD_DIM, 1), ones_col).astype(o_ref.dtype)


def _proj(x, w, cs, segs, bsz=None):
    m, k = x.shape
    n = w.shape[1]
    tm = min(m, 512)
    cos_t, sin_t = cs
    r_blocks = cos_t.shape[0] // tm
    tab = pl.BlockSpec((tm, LANES), lambda i: (i % r_blocks, 0))
    specs, shapes = [], []
    for _, width, _, _, dt, form in segs:
        nh = width // HEAD_DIM
        if form == "tok":
            specs.append(pl.BlockSpec((tm, width), lambda i: (i, 0)))
            shapes.append(jax.ShapeDtypeStruct((m, width), dt))
            continue
        t = m // bsz
        tpb = t // tm
        if form == "keys_t":
            specs.append(pl.BlockSpec((1, nh, HEAD_DIM, tm), lambda i: (i // tpb, 0, 0, i % tpb)))
            shapes.append(jax.ShapeDtypeStruct((bsz, nh, HEAD_DIM, t), dt))
        else:
            last = HEAD_DIM if form == "heads" else LANES
            specs.append(pl.BlockSpec((1, nh, tm, last), lambda i: (i // tpb, 0, i % tpb, 0)))
            shapes.append(jax.ShapeDtypeStruct((bsz, nh, t, last), dt))
    return pl.pallas_call(
        functools.partial(_proj_kernel, segs=tuple(s[:4] + (s[5],) for s in segs)),
        grid=(m // tm,),
        in_specs=[pl.BlockSpec((tm, k), lambda i: (i, 0)), pl.BlockSpec((k, n), lambda i: (0, 0)), tab, tab],
        out_specs=specs,
        out_shape=shapes,
        compiler_params=_params("parallel"),
        name="in_proj",
    )(x, w, cos_t, sin_t)


def _layer_norm(y, g, b):
    mu = jnp.mean(y, axis=-1, keepdims=True)
    d = y - mu
    var = jnp.mean(d * d, axis=-1, keepdims=True)
    return d * lax.rsqrt(var + LN_EPS) * g + b


def _out_ln_kernel(o_ref, w_ref, x_ref, g_ref, b_ref, y_ref):
    y = ALPHA * x_ref[...] + _dot(o_ref[...].astype(BF16), w_ref[...])
    y_ref[...] = _layer_norm(y, g_ref[...], b_ref[...])


def _out_ln(o, w, x, g, b):
    m, d = x.shape
    k = o.shape[1]
    tm = min(m, 512)
    row = lambda i: (i, 0)
    fix = lambda i: (0, 0)
    return pl.pallas_call(
        _out_ln_kernel,
        grid=(m // tm,),
        in_specs=[pl.BlockSpec((tm, k), row), pl.BlockSpec((k, d), fix), pl.BlockSpec((tm, d), row),
                  pl.BlockSpec((1, d), fix), pl.BlockSpec((1, d), fix)],
        out_specs=pl.BlockSpec((tm, d), row),
        out_shape=jax.ShapeDtypeStruct((m, d), F32),
        compiler_params=_params("parallel"),
        name="out_proj_ln",
    )(o, w, x, g.reshape(1, d), b.reshape(1, d))


def _ffn_ln_kernel(x_ref, wgu_ref, wd_ref, g_ref, b_ref, y_ref, *, d_ff, chunk):
    x = x_ref[...]
    xb = x.astype(BF16)
    acc = jnp.zeros(x.shape, F32)
    for c in range(d_ff // chunk):
        gate = _dot(xb, wgu_ref[:, c * chunk:(c + 1) * chunk])
        up = _dot(xb, wgu_ref[:, d_ff + c * chunk:d_ff + (c + 1) * chunk])
        h = gate * (1.0 / (1.0 + jnp.exp(-gate))) * up
        acc = acc + _dot(h.astype(BF16), wd_ref[c * chunk:(c + 1) * chunk, :])
    y_ref[...] = _layer_norm(ALPHA * x + acc, g_ref[...], b_ref[...])


def _ffn_ln(x, wgu, wd, g, b):
    m, d = x.shape
    d_ff = wd.shape[0]
    tm = min(m, 256)
    row = lambda i: (i, 0)
    fix = lambda i: (0, 0)
    return pl.pallas_call(
        functools.partial(_ffn_ln_kernel, d_ff=d_ff, chunk=256),
        grid=(m // tm,),
        in_specs=[pl.BlockSpec((tm, d), row), pl.BlockSpec((d, 2 * d_ff), fix), pl.BlockSpec((d_ff, d), fix),
                  pl.BlockSpec((1, d), fix), pl.BlockSpec((1, d), fix)],
        out_specs=pl.BlockSpec((tm, d), row),
        out_shape=jax.ShapeDtypeStruct((m, d), F32),
        compiler_params=_params("parallel"),
        name="ffn_ln",
    )(x, wgu, wd, g.reshape(1, d), b.reshape(1, d))


def _sortable(x):
    x = jnp.where(x == 0.0, 0.0, x)
    b = lax.bitcast_convert_type(x, I32)
    return b ^ ((b >> 31) & I32(0x7FFFFFFF))


def _kth_largest_key(u_ref, k):
    rows = u_ref.shape[0]

    def count_ge(cand):
        return jnp.sum((u_ref[...] >= cand).astype(I32), axis=1, keepdims=True)

    base = jnp.where(count_ge(jnp.zeros((rows, 1), I32)) >= k, I32(0), I32(INT_MIN))

    def body(i, base):
        cand = base | jnp.left_shift(I32(1), 30 - i)
        return jnp.where(count_ge(cand) >= k, cand, base)

    return lax.fori_loop(0, 31, body, base)


def _emit_selection(u_ref, k, write):
    rows, n = u_ref.shape
    thr = _kth_largest_key(u_ref, k)
    n_gt = jnp.sum((u_ref[...] > thr).astype(I32), axis=1, keepdims=True)
    need = (k - n_gt).astype(F32)
    r_i = lax.broadcasted_iota(I32, (LANES, LANES), 0)
    c_i = lax.broadcasted_iota(I32, (LANES, LANES), 1)
    tri = jnp.where(r_i <= c_i, 1.0, 0.0).astype(BF16)
    carry = jnp.zeros((rows, 1), F32)
    for ci in range(n // LANES):
        u = u_ref[:, ci * LANES:(ci + 1) * LANES]
        tie = u == thr
        tie_f = jnp.where(tie, 1.0, 0.0)
        inc = _dot(tie_f.astype(BF16), tri)
        rank = carry + inc - tie_f
        sel = (u > thr) | (tie & (rank < need))
        write(ci, sel, u)
        carry = carry + inc[:, LANES - 1:LANES]


def _topk_mask_cols(u_ref, k):
    n, r = u_ref.shape

    def count(pred):
        return jnp.sum(pred(u_ref[...]).astype(I32), axis=0, keepdims=True)

    base = jnp.where(count(lambda u: u >= 0) >= k, I32(0), I32(INT_MIN))

    def radix(i, base):
        cand = base | jnp.left_shift(I32(1), 30 - i)
        return jnp.where(count(lambda u: u >= cand) >= k, cand, base)

    thr = lax.fori_loop(0, 31, radix, base)
    need = (k - count(lambda u: u > thr)).astype(F32)
    u = u_ref[...]
    tie = u == thr
    lower = jnp.where(lax.broadcasted_iota(I32, (n, n), 0) > lax.broadcasted_iota(I32, (n, n), 1), 1.0, 0.0)
    rank = _dot(lower.astype(BF16), jnp.where(tie, 1.0, 0.0).astype(BF16))
    return (u > thr) | (tie & (rank < need)), u


def _flash_prompt_kernel(*refs, kind, tq, tk, n_extra):
    q_ref, k_ref, v_ref = refs[:3]
    extra = refs[3:3 + n_extra]
    o_ref, m_ref, acc_ref, s_ref = refs[3 + n_extra:]
    g = pl.program_id(1)
    q0 = pl.program_id(2) * tq
    rows = GROUP * tq
    q = q_ref[0, 0].reshape(rows, HEAD_DIM)
    m_ref[...] = jnp.full(m_ref.shape, NEG, F32)
    acc_ref[...] = jnp.zeros(acc_ref.shape, F32)
    c_diag = q0 // tk
    c_lo = jnp.maximum(q0 - WINDOW, 0) // tk if kind == "nsa_win" else 0
    if kind == "nsa_sel":
        selb = extra[0][0, 0].astype(BF16)
    if kind == "moba":
        selb = extra[0][0, 0].reshape(rows, extra[0].shape[-1]).astype(BF16)

    def scores(c):
        return _dot(q, k_ref[0, g, :, pl.ds(pl.multiple_of(c * tk, tk), tk)])

    def chunk(c, diag):
        if kind in ("nsa_sel", "moba"):
            hit = _dot(selb, extra[1][c]) > 0.5
        s3 = s_ref[c % 2].reshape(GROUP, tq, tk)
        if not diag:
            s_ref[(c + 1) % 2] = scores(c + 1)
        start = pl.multiple_of(c * tk, tk)
        v = v_ref[0, g, pl.ds(start, tk), :]
        ok = None
        if diag or kind == "nsa_win":
            t_idx = q0 + lax.broadcasted_iota(I32, (GROUP, tq, tk), 1)
            s_idx = start + lax.broadcasted_iota(I32, (GROUP, tq, tk), 2)
            causal = s_idx <= t_idx
        if kind == "fox":
            ck = extra[0][0, 0, :, pl.ds(start, tk)]
            s3 = s3 - ck[:, None, :]
            ok = causal if diag else None
        elif kind == "dsa":
            msk = extra[0][0, :, pl.ds(start, tk)]
            ok = jnp.broadcast_to((msk > 0)[None], (GROUP, tq, tk))
        elif kind == "nsa_sel":
            ok = jnp.broadcast_to(hit[None], (GROUP, tq, tk))
            ok = (ok & causal) if diag else ok
        elif kind == "nsa_win":
            ok = causal & (s_idx >= t_idx - WINDOW)
        else:
            ok = hit.reshape(GROUP, tq, tk)
            if diag:
                ok = ok | (causal & ((s_idx // MOBA_BLOCK) == (t_idx // MOBA_BLOCK)))
        if ok is not None:
            s3 = jnp.where(ok, s3, NEG)
        s = s3.reshape(rows, tk)
        m_prev = m_ref[...]
        m_new = jnp.maximum(m_prev, jnp.max(s, axis=1, keepdims=True))
        alpha = jnp.exp2(m_prev - m_new)
        p = jnp.exp2(s - pltpu.repeat(m_new, tk // LANES, axis=1))
        acc_ref[...] = acc_ref[...] * alpha + _dot(p.astype(BF16), v)
        m_ref[...] = m_new

    def body(c, carry):
        chunk(c, False)
        return carry

    s_ref[c_lo % 2] = scores(c_lo)
    lax.fori_loop(c_lo, c_diag, body, 0)
    chunk(c_diag, True)
    acc = acc_ref[...]
    o = acc[:, :HEAD_DIM] / jnp.maximum(acc[:, HEAD_DIM:HEAD_DIM + 1], 1e-30)
    o_ref[0, 0] = o.reshape(GROUP, tq, HEAD_DIM)


def _flash_prompt(kind, q, k, v, extra, tq=128, tk=512):
    bsz, _, _, t, _ = q.shape
    tk = min(tk, t)
    tq = min(tq, t)
    qspec = pl.BlockSpec((1, 1, GROUP, tq, HEAD_DIM), lambda b, g, i: (b, g, 0, i, 0))
    kvspec = pl.BlockSpec((1, N_KV_HEADS, HEAD_DIM, t), lambda b, g, i: (b, 0, 0, 0))
    vspec = pl.BlockSpec((1, N_KV_HEADS, t, LANES), lambda b, g, i: (b, 0, 0, 0))
    if kind == "fox":
        especs = [pl.BlockSpec((1, 1, GROUP, t), lambda b, g, i: (b, g, 0, 0))]
    elif kind == "dsa":
        especs = [pl.BlockSpec((1, tq, t), lambda b, g, i: (b, i, 0))]
    elif kind == "nsa_sel":
        especs = [pl.BlockSpec((1, 1, tq, extra[0].shape[-1]), lambda b, g, i: (b, g, i, 0)),
                  pl.BlockSpec(extra[1].shape, lambda b, g, i: (0, 0, 0))]
    elif kind == "moba":
        especs = [pl.BlockSpec((1, 1, GROUP, tq, extra[0].shape[-1]), lambda b, g, i: (b, g, 0, i, 0)),
                  pl.BlockSpec(extra[1].shape, lambda b, g, i: (0, 0, 0))]
    else:
        especs = []
    rows = GROUP * tq
    return pl.pallas_call(
        functools.partial(_flash_prompt_kernel, kind=kind, tq=tq, tk=tk, n_extra=len(extra)),
        grid=(bsz, N_KV_HEADS, t // tq),
        in_specs=[qspec, kvspec, vspec] + especs,
        out_specs=qspec,
        out_shape=jax.ShapeDtypeStruct(q.shape, F32),
        scratch_shapes=[pltpu.VMEM((rows, LANES), F32), pltpu.VMEM((rows, LANES), F32),
                        pltpu.VMEM((2, rows, tk), F32)],
        compiler_params=_params("parallel", "parallel", "parallel"),
        name="flash_prompt_" + kind,
    )(q, k, v, *extra)


def _expand_matrix(n_blocks_padded, block, t, tk):
    s = np.arange(t)
    e = (s[None, :] // block == np.arange(n_blocks_padded)[:, None]).astype(np.float32)
    e = e.reshape(n_blocks_padded, t // tk, tk).transpose(1, 0, 2)
    return jnp.asarray(e, dtype=BF16)


def _dsa_select_prompt_kernel(qi_ref, ki_ref, wi_ref, mask_ref, u_ref, *, tq, t, tk, topk):
    q0 = pl.program_id(1) * tq
    n_act = q0 // tk + 1
    sub = tk // LANES
    w = wi_ref[0]
    t_idx = q0 + lax.broadcasted_iota(I32, (tq, tk), 0)
    k_off = lax.broadcasted_iota(I32, (tq, tk), 1)

    def chunk_at(c):
        return pl.ds(pl.multiple_of(c * tk, tk), tk)

    def fill(c, carry):
        kc = ki_ref[0, chunk_at(c), :]
        s = jnp.zeros((tq, tk), F32)
        for h in range(IDX_HEADS):
            s = s + jnp.maximum(_dot_t(qi_ref[0, h], kc), 0.0) * w[:, h:h + 1]
        s = jnp.where(c * tk + k_off <= t_idx, s, -jnp.inf)
        u_ref[:, chunk_at(c)] = _sortable(s)
        return carry

    lax.fori_loop(0, n_act, fill, 0)

    def count(pred):
        def body(c, acc):
            hit = pred(u_ref[:, chunk_at(c)]).astype(I32)
            for i in range(sub):
                acc = acc + hit[:, i * LANES:(i + 1) * LANES]
            return acc
        acc = lax.fori_loop(0, n_act, body, jnp.zeros((tq, LANES), I32))
        return jnp.sum(acc, axis=1, keepdims=True)

    base = jnp.where(count(lambda u: u >= 0) >= topk, I32(0), I32(INT_MIN))

    def radix(i, base):
        cand = base | jnp.left_shift(I32(1), 30 - i)
        return jnp.where(count(lambda u: u >= cand) >= topk, cand, base)

    thr = lax.fori_loop(0, 31, radix, base)
    need = (topk - count(lambda u: u > thr)).astype(F32)
    r_i = lax.broadcasted_iota(I32, (LANES, LANES), 0)
    c_i = lax.broadcasted_iota(I32, (LANES, LANES), 1)
    tri = jnp.where(r_i <= c_i, 1.0, 0.0).astype(BF16)

    def emit(c, carry):
        for i in range(sub):
            at = pl.ds(pl.multiple_of(c * tk + i * LANES, LANES), LANES)
            u = u_ref[:, at]
            tie = u == thr
            tie_f = jnp.where(tie, 1.0, 0.0)
            inc = _dot(tie_f.astype(BF16), tri)
            sel = (u > thr) | (tie & (carry + inc - tie_f < need))
            keep = sel & (u > KEY_NEG_INF) & (u < KEY_POS_INF)
            mask_ref[0, :, at] = jnp.where(keep, 1.0, 0.0).astype(BF16)
            carry = carry + inc[:, LANES - 1:LANES]
        return carry

    lax.fori_loop(0, n_act, emit, jnp.zeros((tq, 1), F32))

    def clear(c, carry):
        mask_ref[0, :, chunk_at(c)] = jnp.zeros((tq, tk), BF16)
        return carry

    lax.fori_loop(n_act, t // tk, clear, 0)


def _dsa_select_prompt(qi, ki, wi, tq=256, tk=512):
    bsz, _, t, _ = qi.shape
    tq, tk = min(tq, t), min(tk, t)
    topk = min(IDX_TOPK, t // 4)
    return pl.pallas_call(
        functools.partial(_dsa_select_prompt_kernel, tq=tq, t=t, tk=tk, topk=topk),
        grid=(bsz, t // tq),
        in_specs=[pl.BlockSpec((1, IDX_HEADS, tq, IDX_DIM), lambda b, i: (b, 0, i, 0)),
                  pl.BlockSpec((1, t, IDX_DIM), lambda b, i: (b, 0, 0)),
                  pl.BlockSpec((1, tq, IDX_HEADS), lambda b, i: (b, i, 0))],
        out_specs=pl.BlockSpec((1, tq, t), lambda b, i: (b, i, 0)),
        out_shape=jax.ShapeDtypeStruct((bsz, t, t), BF16),
        scratch_shapes=[pltpu.VMEM((tq, t), I32)],
        compiler_params=_params("parallel", "parallel"),
        name="dsa_select_prompt",
    )(qi, ki, wi)


def _dsa_select_sample_kernel(pt_ref, qi_ref, wi_ref, *refs, n_pg, n_steps, tpad, n_new, topk, nb):
    pages = refs[:nb * n_pg]
    new_ref, mask_ref, u_ref = refs[nb * n_pg:]
    j = pl.program_id(1)

    def scores(bi, kc):
        rel = jnp.maximum(_dot(qi_ref[bi], kc.astype(BF16)), 0.0) * wi_ref[bi]
        return rel.reshape(tpad, IDX_HEADS, LANES).sum(axis=1)

    @pl.when(j < n_steps - 1)
    def _():
        for bi in range(nb):
            for i in range(n_pg):
                start = pl.multiple_of((j * n_pg + i) * LANES, LANES)
                u_ref[bi * tpad:(bi + 1) * tpad, pl.ds(start, LANES)] = _sortable(scores(bi, pages[bi * n_pg + i][0]))

    @pl.when(j == n_steps - 1)
    def _():
        t_idx = lax.broadcasted_iota(I32, (tpad, LANES), 0)
        c_idx = lax.broadcasted_iota(I32, (tpad, LANES), 1)
        base = (n_steps - 1) * n_pg * LANES
        for bi in range(nb):
            s = jnp.where((c_idx <= t_idx) & (c_idx < n_new), scores(bi, new_ref[bi]), -jnp.inf)
            u_ref[bi * tpad:(bi + 1) * tpad, base:base + LANES] = _sortable(s)
        for i in range(1, n_pg):
            u_ref[:, base + i * LANES:base + (i + 1) * LANES] = jnp.full((nb * tpad, LANES), KEY_NEG_INF, I32)

        def write(ci, sel, u):
            keep = sel & (u > KEY_NEG_INF) & (u < KEY_POS_INF)
            mask_ref[:, :, ci * LANES:(ci + 1) * LANES] = jnp.where(keep, 1.0, 0.0).reshape(nb, tpad, LANES)

        _emit_selection(u_ref, topk, write)


def _dsa_select_sample(page_table, qi, wi, pool_kidx, new_ki, n_new, n_pg=8):
    bsz, n_pages = page_table.shape
    tpad = qi.shape[1] // IDX_HEADS
    n_steps = n_pages // n_pg + 1
    width = n_steps * n_pg * LANES
    topk = min(IDX_TOPK, (n_pages * PAGE_SIZE + n_new) // 4)
    nb = _batch_rows(bsz)
    fix = lambda b, j, pt: (b, 0, 0)

    def page_map(bi, i):
        return lambda b, j, pt: (pt[b * nb + bi, jnp.minimum(j * n_pg + i, n_pages - 1)], 0, 0)

    grid_spec = pltpu.PrefetchScalarGridSpec(
        num_scalar_prefetch=1,
        grid=(bsz // nb, n_steps),
        in_specs=[pl.BlockSpec((nb, tpad * IDX_HEADS, IDX_DIM), fix), pl.BlockSpec((nb, tpad * IDX_HEADS, 1), fix)]
        + [pl.BlockSpec((1, IDX_DIM, PAGE_SIZE), page_map(bi, i)) for bi in range(nb) for i in range(n_pg)]
        + [pl.BlockSpec((nb, IDX_DIM, PAGE_SIZE), fix)],
        out_specs=pl.BlockSpec((nb, tpad, width), fix),
        scratch_shapes=[pltpu.VMEM((nb * tpad, width), I32)],
    )
    return pl.pallas_call(
        functools.partial(_dsa_select_sample_kernel, n_pg=n_pg, n_steps=n_steps, tpad=tpad, n_new=n_new, topk=topk, nb=nb),
        grid_spec=grid_spec,
        out_shape=jax.ShapeDtypeStruct((bsz, tpad, width), F32),
        compiler_params=_params("parallel", "arbitrary"),
        name="dsa_select_sample",
    )(page_table, qi, wi, *([pool_kidx] * (nb * n_pg)), new_ki)


def _batch_rows(bsz):
    return 4 if bsz % 4 == 0 else (2 if bsz % 2 == 0 else 1)


def _page_map(i, n_pg, n_pages):
    return lambda b, j, pt: (pt[b, jnp.minimum(j * n_pg + i, n_pages - 1)], 0, 0)


def _page_map4(i, n_pg, n_pages):
    return lambda b, j, pt: (pt[b, jnp.minimum(j * n_pg + i, n_pages - 1)], 0, 0, 0)


def _paged_flash_kernel(pt_ref, q_ref, *refs, n_pg, n_steps, nb):
    pages = refs[:nb * n_pg]
    new_ref, bias_ref, o_ref, m_ref, l_ref, acc_ref = refs[nb * n_pg:]
    j = pl.program_id(1)

    @pl.when(j == 0)
    def _():
        m_ref[...] = jnp.full(m_ref.shape, NEG, F32)
        l_ref[...] = jnp.zeros(l_ref.shape, F32)
        acc_ref[...] = jnp.zeros(acc_ref.shape, F32)

    def update(bi, kv):
        q = q_ref[bi]
        s = [_dot(q, k().astype(BF16)) + bias_ref[bi, :, i * LANES:(i + 1) * LANES] for i, (k, _) in enumerate(kv)]
        m_prev = m_ref[bi]
        m_new = m_prev
        for si in s:
            m_new = jnp.maximum(m_new, jnp.max(si, axis=1, keepdims=True))
        alpha = jnp.exp(m_prev - m_new)
        l_new = alpha * l_ref[bi]
        acc = acc_ref[bi] * pltpu.repeat(alpha, KV_COLS // LANES, axis=1)
        for si, (_, v) in zip(s, kv):
            p = jnp.where(si > 0.5 * NEG, jnp.exp(si - m_new), 0.0)
            l_new = l_new + jnp.sum(p, axis=1, keepdims=True)
            acc = acc + _dot_t(p.astype(BF16), v().astype(BF16))
        l_ref[bi] = l_new
        acc_ref[bi] = acc
        m_ref[bi] = m_new

    def loaders(ref, lead):
        return (lambda: ref[lead, 0]), (lambda: ref[lead, 1])

    @pl.when(j < n_steps - 1)
    def _():
        for bi in range(nb):
            update(bi, [loaders(pages[bi * n_pg + i], 0) for i in range(n_pg)])

    @pl.when(j == n_steps - 1)
    def _():
        for bi in range(nb):
            update(bi, [loaders(new_ref, bi)])
        o_ref[...] = acc_ref[...] / jnp.maximum(jnp.concatenate([l_ref[...]] * (KV_COLS // LANES), axis=-1), 1e-30)


def _paged_flash(page_table, q_bd, pool_t, new_page_t, bias, n_pg):
    bsz, n_pages = page_table.shape
    n_steps = n_pages // n_pg + 1
    rows = q_bd.shape[1]
    nb = _batch_rows(bsz)
    fix = lambda b, j, pt: (b, 0, 0)
    fix4 = lambda b, j, pt: (b, 0, 0, 0)
    page_block = (1, 2, KV_COLS, PAGE_SIZE)

    def page_map(bi, i):
        return lambda b, j, pt: (pt[b * nb + bi, jnp.minimum(j * n_pg + i, n_pages - 1)], 0, 0, 0)

    grid_spec = pltpu.PrefetchScalarGridSpec(
        num_scalar_prefetch=1,
        grid=(bsz // nb, n_steps),
        in_specs=[pl.BlockSpec((nb, rows, KV_COLS), fix)]
        + [pl.BlockSpec(page_block, page_map(bi, i)) for bi in range(nb) for i in range(n_pg)]
        + [pl.BlockSpec((nb, 2, KV_COLS, PAGE_SIZE), fix4),
           pl.BlockSpec((nb, rows, n_pg * LANES), lambda b, j, pt: (b, 0, j))],
        out_specs=pl.BlockSpec((nb, rows, KV_COLS), fix),
        scratch_shapes=[pltpu.VMEM((nb, rows, LANES), F32), pltpu.VMEM((nb, rows, LANES), F32),
                        pltpu.VMEM((nb, rows, KV_COLS), F32)],
    )
    return pl.pallas_call(
        functools.partial(_paged_flash_kernel, n_pg=n_pg, n_steps=n_steps, nb=nb),
        grid_spec=grid_spec,
        out_shape=jax.ShapeDtypeStruct((bsz, rows, KV_COLS), F32),
        compiler_params=_params("parallel", "arbitrary"),
        name="paged_flash",
    )(page_table, q_bd, *([pool_t] * (nb * n_pg)), new_page_t, bias)


def _cumsum_kernel(pt_ref, *refs, n_pg, n_steps):
    pages = refs[:n_pg]
    new_ref, o_ref, carry_ref = refs[n_pg:]
    j = pl.program_id(1)
    r_i = lax.broadcasted_iota(I32, (LANES, LANES), 0)
    c_i = lax.broadcasted_iota(I32, (LANES, LANES), 1)
    tri = jnp.where(r_i <= c_i, 1.0, 0.0).astype(BF16)

    @pl.when(j == 0)
    def _():
        carry_ref[...] = jnp.zeros(carry_ref.shape, F32)

    ones = jnp.ones((LANES, LANES), BF16)

    def steps(xs):
        local = [_dot_hp(x, tri) for x in xs]
        total = [_dot_hp(x, ones) for x in xs]
        carry = carry_ref[...]
        for i in range(len(xs)):
            o_ref[0, :, i * LANES:(i + 1) * LANES] = local[i] + carry
            carry = carry + total[i]
        carry_ref[...] = carry

    @pl.when(j < n_steps - 1)
    def _():
        steps([pages[i][0] for i in range(n_pg)])

    @pl.when(j == n_steps - 1)
    def _():
        steps([new_ref[0]])
        for i in range(1, n_pg):
            o_ref[0, :, i * LANES:(i + 1) * LANES] = jnp.zeros((N_HEADS, LANES), F32)


def _paged_cumsum(page_table, pool_t, new_t, n_pg):
    bsz, n_pages = page_table.shape
    n_steps = n_pages // n_pg + 1
    fix = lambda b, j, pt: (b, 0, 0)
    grid_spec = pltpu.PrefetchScalarGridSpec(
        num_scalar_prefetch=1,
        grid=(bsz, n_steps),
        in_specs=[pl.BlockSpec((1, N_HEADS, LANES), _page_map(i, n_pg, n_pages)) for i in range(n_pg)]
        + [pl.BlockSpec((1, N_HEADS, LANES), fix)],
        out_specs=pl.BlockSpec((1, N_HEADS, n_pg * LANES), lambda b, j, pt: (b, 0, j)),
        scratch_shapes=[pltpu.VMEM((N_HEADS, LANES), F32)],
    )
    return pl.pallas_call(
        functools.partial(_cumsum_kernel, n_pg=n_pg, n_steps=n_steps),
        grid_spec=grid_spec,
        out_shape=jax.ShapeDtypeStruct((bsz, N_HEADS, n_steps * n_pg * LANES), F32),
        compiler_params=_params("parallel", "arbitrary"),
        name="fox_cumsum",
    )(page_table, *([pool_t] * n_pg), new_t)


def _gelu_tanh(x):
    return 0.5 * x * (1.0 + jnp.tanh(0.7978845608028654 * (x + 0.044715 * x * x * x)))


def _nsa_compress_kernel(pt_ref, *refs, n_pg, n_steps, nc):
    pages = refs[:n_pg]
    pe_ref, w1_ref, w2_ref, o_ref, x_ref = refs[n_pg:]
    j = pl.program_id(1)
    per = PAGE_SIZE // CMP_STRIDE
    n_chunk = x_ref.shape[1]
    for i in range(0, n_pg, 2):
        start = pl.multiple_of((j * n_pg + i) * per, 2 * per)
        for ck in range(2 * N_KV_HEADS):
            x_ref[ck, pl.ds(start, 2 * per), :] = jnp.concatenate([pages[i][0, ck], pages[i + 1][0, ck]], axis=0)

    @pl.when(j == n_steps - 1)
    def _():
        half = CMP_STRIDE * HEAD_DIM
        rows = N_KV_HEADS * n_chunk
        row = lax.broadcasted_iota(I32, (N_KV_HEADS, n_chunk, HEAD_DIM), 1)
        for c in range(2):
            w1 = w1_ref[c]
            part = _dot(x_ref[c * N_KV_HEADS:(c + 1) * N_KV_HEADS].reshape(rows, half), w1)
            pe = pe_ref[c]
            pe_term = _dot(pe[:, :half], w1)[:, :CMP_HID] + _dot(pe[:, half:], w1)[:, CMP_HID:]
            h = pe_term[0:1, :] + part[:, :CMP_HID] + pltpu.roll(part[:, CMP_HID:], rows - 1, 0)
            out = _dot(_gelu_tanh(h).astype(BF16), w2_ref[c]).reshape(N_KV_HEADS, n_chunk, HEAD_DIM)
            o_ref[0, c * N_KV_HEADS:(c + 1) * N_KV_HEADS] = jnp.where(row < nc, out, 0.0)


def _nsa_compress(page_table, pool_t, pe8, w1cat, w2, n_pg=8):
    bsz, n_pages = page_table.shape
    per = PAGE_SIZE // CMP_STRIDE
    n_chunk = n_pages * per
    nc = n_chunk - CMP_LEN // CMP_STRIDE + 1
    n_steps = n_pages // n_pg
    width = CMP_STRIDE * HEAD_DIM

    def page_map(i):
        return lambda b, j, pt: (pt[b, j * n_pg + i], 0, 0, 0)

    fix3 = lambda b, j, pt: (0, 0, 0)
    grid_spec = pltpu.PrefetchScalarGridSpec(
        num_scalar_prefetch=1,
        grid=(bsz, n_steps),
        in_specs=[pl.BlockSpec((1, 2 * N_KV_HEADS, per, width), page_map(i)) for i in range(n_pg)]
        + [pl.BlockSpec(pe8.shape, fix3), pl.BlockSpec(w1cat.shape, fix3), pl.BlockSpec(w2.shape, fix3)],
        out_specs=pl.BlockSpec((1, 2 * N_KV_HEADS, n_chunk, HEAD_DIM), lambda b, j, pt: (b, 0, 0, 0)),
        scratch_shapes=[pltpu.VMEM((2 * N_KV_HEADS, n_chunk, width), BF16)],
    )
    return pl.pallas_call(
        functools.partial(_nsa_compress_kernel, n_pg=n_pg, n_steps=n_steps, nc=nc),
        grid_spec=grid_spec,
        out_shape=jax.ShapeDtypeStruct((bsz, 2 * N_KV_HEADS, n_chunk, HEAD_DIM), F32),
        compiler_params=_params("parallel", "arbitrary"),
        name="nsa_compress",
    )(page_table, *([pool_t] * n_pg), pe8, w1cat, w2), nc


def _nsa_cmp_kernel(q_ref, ck_ref, cv_ref, cover_ref, o_ref, sel_ref, u_ref, *, tq, pos0, nc, n_sel):
    t0 = pos0 + pl.program_id(1) * tq
    ncp = ck_ref.shape[2]
    nsp = cover_ref.shape[0]
    rows = GROUP * tq
    n_idx = lax.broadcasted_iota(I32, (tq, ncp), 1)
    t_idx = t0 + lax.broadcasted_iota(I32, (tq, ncp), 0)
    c_ok = ((n_idx * CMP_STRIDE + CMP_LEN - 1 <= t_idx) & (n_idx < nc))[None]
    blk = lax.broadcasted_iota(I32, (nsp, tq), 0)
    cur = (t0 + lax.broadcasted_iota(I32, (nsp, tq), 1)) // SEL_BLOCK
    forced = (blk == 0) | (blk == cur) | (blk == cur - 1)
    cover_t = cover_ref[...]
    for g in range(N_KV_HEADS):
        q = q_ref[0, g].reshape(rows, HEAD_DIM)
        s3 = jnp.where(c_ok, _dot_t(q, ck_ref[0, g]).reshape(GROUP, tq, ncp), NEG)
        m = jnp.max(s3, axis=-1, keepdims=True)
        e = jnp.where(c_ok, jnp.exp(s3 - m), 0.0)
        p = e / jnp.maximum(jnp.sum(e, axis=-1, keepdims=True), 1e-30)
        o = _dot(p.reshape(rows, ncp).astype(BF16), cv_ref[0, g])
        o_ref[0, g] = o.reshape(GROUP, tq, HEAD_DIM)
        psum = p[0] + p[1] + p[2] + p[3]
        hi = psum.astype(BF16)
        r1 = psum - hi.astype(F32)
        mid = r1.astype(BF16)
        lo = (r1 - mid.astype(F32)).astype(BF16)
        imp = _dot_t(cover_t, hi) + _dot_t(cover_t, mid) + _dot_t(cover_t, lo)
        imp = jnp.where(forced, jnp.inf, imp)
        imp = jnp.where(blk <= cur, imp, -jnp.inf)
        u_ref[:, g * tq:(g + 1) * tq] = _sortable(imp)
    sel, u = _topk_mask_cols(u_ref, n_sel)
    sel_ref[0, 0] = jnp.where(sel & (u > KEY_NEG_INF), 1.0, 0.0)


def _nsa_cmp_select(q, cmp_k, cmp_v, pos0, nc, n_keys, tq):
    bsz, _, _, t, _ = q.shape
    ncp = cmp_k.shape[2]
    ns = -(-n_keys // SEL_BLOCK)
    nsp = -(-ns // LANES) * LANES
    n_sel = min(SEL_TOPN, ns)
    c0 = np.arange(ncp)[:, None] * CMP_STRIDE
    s0 = np.arange(nsp)[None, :] * SEL_BLOCK
    cover = (c0 <= s0 + SEL_BLOCK - 1) & (c0 + CMP_LEN - 1 >= s0) & (np.arange(ncp)[:, None] < nc) & (np.arange(nsp)[None, :] < ns)
    cover_t = jnp.asarray(cover.T.astype(np.float32), dtype=BF16)
    nq = t // tq
    qspec = pl.BlockSpec((1, N_KV_HEADS, GROUP, tq, HEAD_DIM), lambda b, i: (b, 0, 0, i, 0))
    cspec = pl.BlockSpec((1, N_KV_HEADS, ncp, HEAD_DIM), lambda b, i: (b, 0, 0, 0))
    o_c, sel = pl.pallas_call(
        functools.partial(_nsa_cmp_kernel, tq=tq, pos0=pos0, nc=nc, n_sel=n_sel),
        grid=(bsz, nq),
        in_specs=[qspec, cspec, cspec, pl.BlockSpec((nsp, ncp), lambda b, i: (0, 0))],
        out_specs=[qspec, pl.BlockSpec((1, 1, nsp, N_KV_HEADS * tq), lambda b, i: (b, i, 0, 0))],
        out_shape=[jax.ShapeDtypeStruct(q.shape, F32), jax.ShapeDtypeStruct((bsz, nq, nsp, N_KV_HEADS * tq), F32)],
        scratch_shapes=[pltpu.VMEM((nsp, N_KV_HEADS * tq), I32)],
        compiler_params=_params("parallel", "parallel"),
        name="nsa_cmp_select",
    )(q, cmp_k, cmp_v, cover_t)
    sel = sel.reshape(bsz, nq, nsp, N_KV_HEADS, tq).transpose(0, 3, 1, 4, 2).reshape(bsz, N_KV_HEADS, t, nsp)
    return o_c, sel


def _kmean_kernel(*refs):
    o_ref = refs[-1]
    tot = jnp.sum(refs[0][0], axis=0, keepdims=True)
    for r in refs[1:-1]:
        tot = tot + jnp.sum(r[0], axis=0, keepdims=True)
    o_ref[0, 0] = tot * (1.0 / MOBA_BLOCK)


def _kmean_prompt(kv):
    bsz, t, _ = kv.shape
    nb = t // MOBA_BLOCK
    return pl.pallas_call(
        _kmean_kernel,
        grid=(bsz, nb),
        in_specs=[pl.BlockSpec((1, MOBA_BLOCK, KV_COLS), lambda b, i: (b, i, 0))],
        out_specs=pl.BlockSpec((1, 1, 1, KV_COLS), lambda b, i: (b, i, 0, 0)),
        out_shape=jax.ShapeDtypeStruct((bsz, nb, 1, KV_COLS), F32),
        compiler_params=_params("parallel", "parallel"),
        name="kmean_prompt",
    )(kv)


def _kmean_sample_kernel(pt_ref, *refs, per):
    o_ref = refs[-1]
    ones = jnp.ones((8, PAGE_SIZE), BF16)
    for n in range(len(refs[:-1]) // per):
        tot = jnp.zeros((8, KV_COLS), F32)
        for r in refs[n * per:(n + 1) * per]:
            x = r[0, 0]
            hi = x.astype(BF16)
            r1 = x - hi.astype(F32)
            mid = r1.astype(BF16)
            lo = (r1 - mid.astype(F32)).astype(BF16)
            tot = tot + _dot_t(ones, hi) + _dot_t(ones, mid) + _dot_t(ones, lo)
        o_ref[0, n] = tot[0:1] * (1.0 / MOBA_BLOCK)


def _kmean_sample(page_table, pool_t, n_pg):
    bsz, n_pages = page_table.shape
    per = MOBA_BLOCK // PAGE_SIZE
    nb = n_pages // per
    grid_spec = pltpu.PrefetchScalarGridSpec(
        num_scalar_prefetch=1,
        grid=(bsz, n_pages // n_pg),
        in_specs=[pl.BlockSpec((1, 1, KV_COLS, PAGE_SIZE), (lambda i: (lambda b, n, pt: (pt[b, n * n_pg + i], 0, 0, 0)))(i))
                  for i in range(n_pg)],
        out_specs=pl.BlockSpec((1, n_pg // per, 1, KV_COLS), lambda b, n, pt: (b, n, 0, 0)),
    )
    return pl.pallas_call(
        functools.partial(_kmean_sample_kernel, per=per),
        grid_spec=grid_spec,
        out_shape=jax.ShapeDtypeStruct((bsz, nb, 1, KV_COLS), F32),
        compiler_params=_params("parallel", "parallel"),
        name="kmean_sample",
    )(page_table, *([pool_t] * n_pg))


def _moba_select_kernel(q_ref, km_ref, sel_ref, *, tq, pos0, k_top):
    t0 = pos0 + pl.program_id(1) * tq
    nbp = km_ref.shape[2]
    rows = GROUP * tq
    blk = lax.broadcasted_iota(I32, (nbp, rows), 0)
    n_past = (t0 + (lax.broadcasted_iota(I32, (nbp, rows), 1) & (tq - 1))) // MOBA_BLOCK
    for g in range(N_KV_HEADS):
        q = q_ref[0, g].reshape(rows, HEAD_DIM)
        s = jnp.where(blk < n_past, _dot_t(km_ref[0, g], q), -jnp.inf)
        sel = jnp.zeros((nbp, rows), F32)
        for _ in range(k_top):
            m = jnp.max(s, axis=0, keepdims=True)
            first = jnp.min(jnp.where(s == m, blk, nbp), axis=0, keepdims=True)
            pick = blk == first
            sel = jnp.where(pick & (m > -jnp.inf), 1.0, sel)
            s = jnp.where(pick, -jnp.inf, s)
        sel_ref[0, 0, g] = sel


def _moba_select(q, kmean, pos0, nb, tq):
    bsz, _, _, t, _ = q.shape
    nbp = kmean.shape[2]
    nq = t // tq
    qspec = pl.BlockSpec((1, N_KV_HEADS, GROUP, tq, HEAD_DIM), lambda b, i: (b, 0, 0, i, 0))
    sel = pl.pallas_call(
        functools.partial(_moba_select_kernel, tq=tq, pos0=pos0, k_top=min(MOBA_TOPK, nb)),
        grid=(bsz, nq),
        in_specs=[qspec, pl.BlockSpec((1, N_KV_HEADS, nbp, HEAD_DIM), lambda b, i: (b, 0, 0, 0))],
        out_specs=pl.BlockSpec((1, 1, N_KV_HEADS, nbp, GROUP * tq), lambda b, i: (b, i, 0, 0, 0)),
        out_shape=jax.ShapeDtypeStruct((bsz, nq, N_KV_HEADS, nbp, GROUP * tq), F32),
        compiler_params=_params("parallel", "parallel"),
        name="moba_select",
    )(q, kmean)
    sel = sel.reshape(bsz, nq, N_KV_HEADS, nbp, GROUP, tq).transpose(0, 2, 4, 1, 5, 3)
    return sel.reshape(bsz, N_KV_HEADS, GROUP, t, nbp)


def _rope_tables(pos):
    half = HEAD_DIM // 2
    inv = ROPE_THETA ** (-jnp.arange(half, dtype=F32) / half)
    ang = pos.astype(F32)[:, None] * inv[None, :]
    cos, sin = jnp.cos(ang), jnp.sin(ang)
    rep = LANES // HEAD_DIM
    return jnp.concatenate([cos, cos] * rep, axis=1), jnp.concatenate([-sin, sin] * rep, axis=1)


def _pad_to(x, axis, size):
    pad = [(0, 0)] * x.ndim
    pad[axis] = (0, size - x.shape[axis])
    return jnp.pad(x, pad)


def _q_groups(q):
    b, t = q.shape[:2]
    return q.transpose(0, 2, 1, 3).reshape(b, N_KV_HEADS, GROUP, t, HEAD_DIM)


def _kv_major(kv):
    b, t = kv.shape[:2]
    k = kv[:, :, 0].transpose(0, 2, 1, 3).astype(BF16)
    v = kv[:, :, 1].transpose(0, 2, 1, 3).astype(BF16)
    ones = jnp.ones((b, N_KV_HEADS, t, 1), BF16)
    zeros = jnp.zeros((b, N_KV_HEADS, t, LANES - HEAD_DIM - 1), BF16)
    return k, jnp.concatenate([v, ones, zeros], axis=-1)


def _pages_t(kv):
    return kv.transpose(0, 2, 3, 4, 1).reshape(kv.shape[0], 2, KV_COLS, PAGE_SIZE)


def _fit(x, width):
    return x[..., :width] if x.shape[-1] >= width else _pad_to(x, x.ndim - 1, width)


def _merge_heads(o):
    b, _, _, t, _ = o.shape
    return o.reshape(b, N_HEADS, t, HEAD_DIM).transpose(0, 2, 1, 3).reshape(b * t, Q_DIM)


_HEAD_TO_GROUP = np.equal(np.arange(N_HEADS)[:, None] // GROUP, np.arange(N_KV_HEADS)[None, :]).astype(np.float32)


def _q_block_diag(q):
    s, tn = q.shape[:2]
    qb = q[:, :, :, None, :] * jnp.asarray(_HEAD_TO_GROUP, dtype=q.dtype)[None, None, :, :, None]
    return qb.reshape(s, tn * N_HEADS, KV_COLS)


def _extract_block_diag(o, tn):
    s = o.shape[0]
    o5 = o.reshape(s, tn, N_HEADS, N_KV_HEADS, HEAD_DIM) * _HEAD_TO_GROUP[None, None, :, :, None]
    return o5.sum(axis=3).reshape(s * tn, Q_DIM)


def _new_page(kv_new):
    return _pages_t(_pad_to(kv_new, 1, PAGE_SIZE))


def _pick_pages(n_pages):
    for n in (8, 4, 2, 1):
        if n_pages % n == 0:
            return n


def _identity_pages(bsz, n_pages):
    return jnp.arange(bsz * n_pages, dtype=I32).reshape(bsz, n_pages)


def _split(proj, sizes):
    out, o = [], 0
    for s in sizes:
        out.append(proj[:, o:o + s])
        o += s
    return out


def _rows_th(ok, s, tn):
    return jnp.where(ok, 0.0, NEG).astype(F32).reshape(s, tn * N_HEADS, ok.shape[-1])


def _kv_stack(k, v, bsz, t):
    return jnp.stack([k.reshape(bsz, t, N_KV_HEADS, HEAD_DIM), v.reshape(bsz, t, N_KV_HEADS, HEAD_DIM)], axis=2)


def _kv_forms(c_k, rope):
    return [(c_k, KV_COLS, rope, 1.0, BF16, "keys_t"), (c_k + KV_COLS, KV_COLS, False, 1.0, BF16, "values_1")]


def _dsa_project(x, bsz, t, w_in, cs, q_scale, prompt):
    hm = "heads" if prompt else "tok"
    c_qi = Q_DIM + 2 * KV_COLS
    c_ki = c_qi + IDX_HEADS * IDX_DIM
    segs = [(0, Q_DIM, True, q_scale, BF16, hm), (Q_DIM, KV_COLS, True, 1.0, F32, "tok"),
            (Q_DIM + KV_COLS, KV_COLS, False, 1.0, F32, "tok"), (c_qi, IDX_HEADS * IDX_DIM, True, 1.0, BF16, hm),
            (c_ki, IDX_DIM, True, 1.0, F32, "tok"), (c_ki + IDX_DIM, IDX_HEADS, False, IDX_SCALE, F32, "tok")]
    if prompt:
        segs += _kv_forms(Q_DIM, True)
    q, k, v, qi, ki, wi, *flash_kv = _proj(x, w_in, cs, segs, bsz)
    if not prompt:
        q, qi = q.reshape(bsz, t, N_HEADS, HEAD_DIM), qi.reshape(bsz, t, IDX_HEADS, IDX_DIM)
    return q, qi, wi.reshape(bsz, t, IDX_HEADS), _kv_stack(k, v, bsz, t), ki.reshape(bsz, t, IDX_DIM), flash_kv


def _head_groups(q):
    return q.reshape(q.shape[0], N_KV_HEADS, GROUP, q.shape[2], HEAD_DIM)


def _dsa_prompt(x, bsz, t, w_in, cs):
    q, qi, wi, kv, ki, (k_t, v1) = _dsa_project(x, bsz, t, w_in, cs, ATTN_SCALE * LOG2E, True)
    mask = _dsa_select_prompt(qi, ki.astype(BF16), wi)
    o = _flash_prompt("dsa", _head_groups(q), k_t, v1, [mask])
    return _merge_heads(o), kv, ki


def _dsa_sample(x, s, tn, w_in, cs, page_table, cache_kv, cache_kidx, n_pg):
    q, qi, wi, kv, ki, _ = _dsa_project(x, s, tn, w_in, cs, ATTN_SCALE, False)
    tpad = 8
    qi_p = _pad_to(qi, 1, tpad).reshape(s, tpad * IDX_HEADS, IDX_DIM)
    wi_p = _pad_to(wi, 1, tpad).reshape(s, tpad * IDX_HEADS, 1)
    mask = _dsa_select_sample(page_table, qi_p, wi_p, cache_kidx.transpose(0, 2, 1),
                              _pad_to(ki, 1, PAGE_SIZE).transpose(0, 2, 1), tn, n_pg)
    ok = jnp.broadcast_to(mask[:, :tn, None, :] > 0.5, (s, tn, N_HEADS, mask.shape[-1]))
    o = _paged_flash(page_table, _q_block_diag(q), _pages_t(cache_kv), _new_page(kv), _rows_th(ok, s, tn), n_pg)
    return _extract_block_diag(o, tn), kv, ki


def _fox_project(x, bsz, t, w_in, b_f, cs, q_scale, prompt):
    segs = [(0, Q_DIM, False, q_scale, BF16, "heads" if prompt else "tok"), (Q_DIM, KV_COLS, False, 1.0, F32, "tok"),
            (Q_DIM + KV_COLS, KV_COLS, False, 1.0, F32, "tok"), (Q_DIM + 2 * KV_COLS, N_HEADS, False, 1.0, F32, "tok")]
    if prompt:
        segs += _kv_forms(Q_DIM, False)
    q, k, v, f, *flash_kv = _proj(x, w_in, cs, segs, bsz)
    logf = jax.nn.log_sigmoid(f.reshape(bsz, t, N_HEADS) + b_f)
    if not prompt:
        q = q.reshape(bsz, t, N_HEADS, HEAD_DIM)
    return q, _kv_stack(k, v, bsz, t), logf, flash_kv


def _fox_prompt(x, bsz, t, w_in, b_f, cs):
    q, kv, logf, (k_t, v1) = _fox_project(x, bsz, t, w_in, b_f, cs, ATTN_SCALE * LOG2E, True)
    n_pages = t // LANES
    pool_t = logf.reshape(bsz, n_pages, LANES, N_HEADS).transpose(0, 1, 3, 2).reshape(bsz * n_pages, N_HEADS, LANES)
    c = _paged_cumsum(_identity_pages(bsz, n_pages), pool_t, jnp.zeros((bsz, N_HEADS, LANES), F32), _pick_pages(n_pages))
    c = (c[:, :, :t] * LOG2E).reshape(bsz, N_KV_HEADS, GROUP, t)
    o = _flash_prompt("fox", _head_groups(q), k_t, v1, [c])
    return _merge_heads(o), kv, logf


def _fox_sample(x, s, tn, w_in, b_f, cs, page_table, cache_kv, cache_logf, n_pg):
    q, kv, logf, _ = _fox_project(x, s, tn, w_in, b_f, cs, ATTN_SCALE, False)
    past = page_table.shape[1] * PAGE_SIZE
    c = _paged_cumsum(page_table, cache_logf.transpose(0, 2, 1), _pad_to(logf.transpose(0, 2, 1), 2, LANES), n_pg)
    col = jnp.arange(c.shape[-1])
    valid = (col[None, :] < past) | ((col[None, :] - past <= jnp.arange(tn)[:, None]) & (col[None, :] < past + tn))
    bias = jnp.where(valid[None, :, None, :], -c[:, None, :, :], NEG).reshape(s, tn * N_HEADS, c.shape[-1])
    o = _paged_flash(page_table, _q_block_diag(q), _pages_t(cache_kv), _new_page(kv), bias, n_pg)
    return _extract_block_diag(o, tn), kv, logf


def _nsa_project(x, bsz, t, w_in, b_gate, cs, q_scale, prompt):
    hm = "heads" if prompt else "tok"
    c = Q_DIM
    segs = [(0, Q_DIM, False, ATTN_SCALE, BF16, hm), (0, Q_DIM, True, q_scale, BF16, hm)]
    for rope in (False, False, True, False, True, False):
        segs.append((c, KV_COLS, rope, 1.0, F32, "tok"))
        c += KV_COLS
    segs.append((c, 3 * N_HEADS, False, 1.0, F32, "tok"))
    if prompt:
        segs += _kv_forms(Q_DIM + 2 * KV_COLS, True) + _kv_forms(Q_DIM + 4 * KV_COLS, True)
    q, q_rot, kc, vc, ks, vs, kw, vw, g, *flash_kv = _proj(x, w_in, cs, segs, bsz)
    gate = jax.nn.sigmoid(g.reshape(bsz, t, 3 * N_HEADS) + b_gate).reshape(bsz, t, 3, N_HEADS)
    if not prompt:
        q, q_rot = q.reshape(bsz, t, N_HEADS, HEAD_DIM), q_rot.reshape(bsz, t, N_HEADS, HEAD_DIM)
    return (q, q_rot, gate, _kv_stack(kc, vc, bsz, t), _kv_stack(ks, vs, bsz, t), _kv_stack(kw, vw, bsz, t), flash_kv)


def _nsa_weights(pe, w1, w2):
    r = CMP_LEN // CMP_STRIDE
    w1cat = w1.reshape(2, r, CMP_STRIDE * HEAD_DIM, CMP_HID).transpose(0, 2, 1, 3).reshape(2, CMP_STRIDE * HEAD_DIM, r * CMP_HID)
    pe8 = jnp.broadcast_to(pe.reshape(2, 1, CMP_LEN * HEAD_DIM), (2, 8, CMP_LEN * HEAD_DIM))
    return pe8.astype(BF16), w1cat.astype(BF16), w2.astype(BF16)


def _chunk_pages(kv):
    n = kv.shape[0]
    per = PAGE_SIZE // CMP_STRIDE
    x = kv.reshape(n, per, CMP_STRIDE, 2 * N_KV_HEADS, HEAD_DIM).transpose(0, 3, 1, 2, 4)
    return x.reshape(n, 2 * N_KV_HEADS, per, CMP_STRIDE * HEAD_DIM).astype(BF16)


def _gate_mix(gate, o_c, o_s, o_w):
    b, t = gate.shape[:2]
    g = gate.transpose(0, 2, 3, 1).reshape(b, 3, N_KV_HEADS, GROUP, t, 1)
    return g[:, 0] * o_c + g[:, 1] * o_s + g[:, 2] * o_w


def _nsa_prompt(x, bsz, t, w_in, b_gate, cmp_w, cs, tq=128, tk=512):
    q, q_rot, gate, kv_cmp, kv_slc, kv_win, (ks_t, vs1, kw_t, vw1) = _nsa_project(x, bsz, t, w_in, b_gate, cs,
                                                                                  ATTN_SCALE * LOG2E, True)
    n_pages = t // PAGE_SIZE
    cmp, nc = _nsa_compress(_identity_pages(bsz, n_pages), _chunk_pages(kv_cmp.reshape(bsz * n_pages, PAGE_SIZE, 2, N_KV_HEADS, HEAD_DIM)),
                            *cmp_w, n_pg=_pick_pages(n_pages))
    cmp = cmp.astype(BF16)
    tq, tk = min(tq, t), min(tk, t)
    o_c, selblk = _nsa_cmp_select(_head_groups(q), cmp[:, :N_KV_HEADS], cmp[:, N_KV_HEADS:], 0, nc, t, tq)
    qg = _head_groups(q_rot)
    e3 = _expand_matrix(selblk.shape[-1], SEL_BLOCK, t, tk)
    o_s = _flash_prompt("nsa_sel", qg, ks_t, vs1, [selblk, e3], tq=2 * tq, tk=tk)
    o_w = _flash_prompt("nsa_win", qg, kw_t, vw1, [], tk=tk)
    o = _gate_mix(gate, o_c, o_s, o_w)
    return _merge_heads(o), kv_cmp, kv_slc, kv_win[:, -min(WINDOW, t):]


def _nsa_sample(x, s, tn, w_in, b_gate, cmp_w, cs, page_table, cache_cmp, cache_slc, state_win, n_pg):
    q, q_rot, gate, kv_cmp, kv_slc, kv_win, _ = _nsa_project(x, s, tn, w_in, b_gate, cs, ATTN_SCALE, False)
    past = page_table.shape[1] * PAGE_SIZE
    tpad = 8
    cmp, nc = _nsa_compress(page_table, _chunk_pages(cache_cmp), *cmp_w, n_pg=n_pg)
    cmp = cmp.astype(BF16)
    o_c, selblk = _nsa_cmp_select(_q_groups(_pad_to(q, 1, tpad)), cmp[:, :N_KV_HEADS], cmp[:, N_KV_HEADS:],
                                  past, nc, past + tn, tpad)
    o_c = o_c[:, :, :, :tn]
    qbd = _q_block_diag(q_rot)
    width = (page_table.shape[1] // n_pg + 1) * n_pg * LANES
    col = jnp.arange(width)
    pos = past + jnp.arange(tn)
    sel_key = jnp.repeat(selblk[:, :, :tn, :-(-width // SEL_BLOCK)], SEL_BLOCK, axis=-1)[..., :width]
    ok = (sel_key > 0.5) & (col[None, :] <= pos[:, None])[None, None]
    ok = jnp.broadcast_to(ok.transpose(0, 2, 1, 3)[:, :, :, None, :], (s, tn, N_KV_HEADS, GROUP, width))
    o_s = _paged_flash(page_table, qbd, _pages_t(cache_slc), _new_page(kv_slc),
                       _rows_th(ok.reshape(s, tn, N_HEADS, width), s, tn), n_pg)
    win_buf = state_win.shape[1]
    n_wp = win_buf // PAGE_SIZE
    wcol = jnp.arange(2 * n_wp * LANES)
    win_pos = jnp.where(wcol < win_buf, past - win_buf + wcol, jnp.where(wcol < win_buf + tn, past + wcol - win_buf, -1))
    w_ok = (win_pos[None, :] <= pos[:, None]) & (win_pos[None, :] >= pos[:, None] - WINDOW) & (win_pos[None, :] >= 0)
    w_ok = jnp.broadcast_to(w_ok[None, :, None, :], (s, tn, N_HEADS, wcol.shape[0]))
    win_pages = _pages_t(state_win.reshape(s * n_wp, PAGE_SIZE, 2, N_KV_HEADS, HEAD_DIM))
    o_w = _paged_flash(_identity_pages(s, n_wp), qbd, win_pages, _new_page(kv_win), _rows_th(w_ok, s, tn), n_wp)
    unbd = lambda o: _extract_block_diag(o, tn).reshape(s, tn, N_KV_HEADS, GROUP, HEAD_DIM).transpose(0, 2, 3, 1, 4)
    o = _gate_mix(gate, o_c, unbd(o_s), unbd(o_w))
    win = jnp.concatenate([state_win, kv_win], axis=1)[:, -win_buf:]
    return _merge_heads(o), kv_cmp, kv_slc, win


def _moba_project(x, bsz, t, w_in, cs, q_scale, prompt):
    segs = [(0, Q_DIM, True, q_scale, BF16, "heads" if prompt else "tok"), (Q_DIM, KV_COLS, True, 1.0, F32, "tok"),
            (Q_DIM + KV_COLS, KV_COLS, False, 1.0, F32, "tok")]
    if prompt:
        segs += _kv_forms(Q_DIM, True)
    q, k, v, *flash_kv = _proj(x, w_in, cs, segs, bsz)
    if not prompt:
        q = q.reshape(bsz, t, N_HEADS, HEAD_DIM)
    return q, _kv_stack(k, v, bsz, t), flash_kv


def _kmean_heads(km):
    b, nb = km.shape[:2]
    return _pad_to(km.reshape(b, nb, N_KV_HEADS, HEAD_DIM).transpose(0, 2, 1, 3), 2, -(-nb // 16) * 16).astype(BF16)


def _moba_prompt(x, bsz, t, w_in, cs, tq=128, tk=512):
    q, kv, (k_t, v1) = _moba_project(x, bsz, t, w_in, cs, ATTN_SCALE * LOG2E, True)
    tq, tk = min(tq, t), min(tk, t)
    nb = -(-t // MOBA_BLOCK)
    km = _kmean_heads(_kmean_prompt(kv.reshape(bsz, t, 2 * KV_COLS)))
    qg = _head_groups(q)
    sel = _moba_select(qg, km, 0, nb, tq)
    e3 = _expand_matrix(km.shape[2], MOBA_BLOCK, t, tk)
    o = _flash_prompt("moba", qg, k_t, v1, [sel, e3], tq=2 * tq, tk=tk)
    return _merge_heads(o), kv


def _moba_sample(x, s, tn, w_in, cs, page_table, cache_kv, n_pg):
    q, kv, _ = _moba_project(x, s, tn, w_in, cs, ATTN_SCALE, False)
    past = page_table.shape[1] * PAGE_SIZE
    tpad = 8
    pool = _pages_t(cache_kv)
    nb = -(-(past + tn) // MOBA_BLOCK)
    km = _kmean_heads(_kmean_sample(page_table, pool, n_pg))
    sel = _moba_select(_q_groups(_pad_to(q, 1, tpad)), km, past, nb, tpad)
    width = (page_table.shape[1] // n_pg + 1) * n_pg * LANES
    col = jnp.arange(width)
    pos = past + jnp.arange(tn)
    sel_key = _fit(jnp.repeat(sel[:, :, :, :tn], MOBA_BLOCK, axis=-1), width)
    own = (col[None, :] // MOBA_BLOCK == pos[:, None] // MOBA_BLOCK) & (col[None, :] <= pos[:, None])
    ok = (sel_key > 0.5) | own[None, None, None]
    ok = ok.transpose(0, 3, 1, 2, 4).reshape(s, tn, N_HEADS, width)
    o = _paged_flash(page_table, _q_block_diag(q), pool, _new_page(kv), _rows_th(ok, s, tn), n_pg)
    return _extract_block_diag(o, tn), kv


def _cast_w(w):
    return _pad_to(w, 1, -(-w.shape[1] // LANES) * LANES).astype(BF16)


def kernel(x_prompt, x_sample, cache_a_kv, cache_a_kidx, cache_b_kv, cache_b_logf, cache_c_cmp_kv, cache_c_slc_kv, state_c_win_kv, cache_d_kv, page_table, a_w_in, a_w_out, b_w_in, b_b_f, b_w_out, c_w_in, c_b_gate, c_cmp_pe, c_cmp_w1, c_cmp_w2, c_w_out, d_w_in, d_w_out, ln_g, ln_b, ffn_w_gu, ffn_w_down):
    bsz, t, d = x_prompt.shape
    s, tn, _ = x_sample.shape
    n_pages = page_table.shape[1]
    past = n_pages * PAGE_SIZE
    n_pg = _pick_pages(n_pages)
    cs_p = _rope_tables(jnp.arange(t, dtype=I32))
    cs_s = _rope_tables(jnp.tile(past + jnp.arange(tn, dtype=I32), s))
    xp = x_prompt.reshape(bsz * t, d)
    xs = x_sample.reshape(s * tn, d)
    cmp_w = _nsa_weights(c_cmp_pe, c_cmp_w1, c_cmp_w2)
    w_out = [_cast_w(w) for w in (a_w_out, b_w_out, c_w_out, d_w_out)]

    op, a_kv_p, a_kidx_p = _dsa_prompt(xp, bsz, t, _cast_w(a_w_in), cs_p)
    os_, a_kv_s, a_kidx_s = _dsa_sample(xs, s, tn, _cast_w(a_w_in), cs_s, page_table, cache_a_kv, cache_a_kidx, n_pg)

    def finish(i, xp, xs, op, os_):
        xp = _out_ln(op, w_out[i], xp, ln_g[i, 0], ln_b[i, 0])
        xs = _out_ln(os_, w_out[i], xs, ln_g[i, 0], ln_b[i, 0])
        wgu, wd = ffn_w_gu[i].astype(BF16), ffn_w_down[i].astype(BF16)
        xp = _ffn_ln(xp, wgu, wd, ln_g[i, 1], ln_b[i, 1])
        xs = _ffn_ln(xs, wgu, wd, ln_g[i, 1], ln_b[i, 1])
        return xp, xs

    xp, xs = finish(0, xp, xs, op, os_)

    op, b_kv_p, b_logf_p = _fox_prompt(xp, bsz, t, _cast_w(b_w_in), b_b_f, cs_p)
    os_, b_kv_s, b_logf_s = _fox_sample(xs, s, tn, _cast_w(b_w_in), b_b_f, cs_s, page_table, cache_b_kv, cache_b_logf, n_pg)
    xp, xs = finish(1, xp, xs, op, os_)

    op, c_cmp_kv_p, c_slc_kv_p, c_win_kv_p = _nsa_prompt(xp, bsz, t, _cast_w(c_w_in), c_b_gate, cmp_w, cs_p)
    os_, c_cmp_kv_s, c_slc_kv_s, c_win_kv_s = _nsa_sample(xs, s, tn, _cast_w(c_w_in), c_b_gate, cmp_w, cs_s, page_table,
                                                          cache_c_cmp_kv, cache_c_slc_kv, state_c_win_kv, n_pg)
    xp, xs = finish(2, xp, xs, op, os_)

    op, d_kv_p = _moba_prompt(xp, bsz, t, _cast_w(d_w_in), cs_p)
    os_, d_kv_s = _moba_sample(xs, s, tn, _cast_w(d_w_in), cs_s, page_table, cache_d_kv, n_pg)
    xp, xs = finish(3, xp, xs, op, os_)

    return (xp.reshape(bsz, t, d), xs.reshape(s, tn, d), a_kv_p, a_kv_s, a_kidx_p, a_kidx_s, b_kv_p, b_kv_s,
            b_logf_p, b_logf_s, c_cmp_kv_p, c_cmp_kv_s, c_slc_kv_p, c_slc_kv_s, c_win_kv_p, c_win_kv_s, d_kv_p, d_kv_s)
```

```python
import functools

import numpy as np
import jax
import jax.numpy as jnp
from jax import lax
from jax.experimental import pallas as pl
from jax.experimental.pallas import tpu as pltpu

F32 = jnp.float32
BF16 = jnp.bfloat16
I32 = jnp.int32

N_HEADS = 16
HEAD_DIM = 64
N_KV_HEADS = 4
GROUP = N_HEADS // N_KV_HEADS
Q_DIM = N_HEADS * HEAD_DIM
KV_COLS = N_KV_HEADS * HEAD_DIM
DEPTH = 4
PAGE_SIZE = 128
ROPE_THETA = 10000.0
LN_EPS = 1e-5
ALPHA = (2 * DEPTH) ** 0.25
ATTN_SCALE = HEAD_DIM ** -0.5
LOG2E = 1.4426950408889634
IDX_HEADS = 8
IDX_DIM = 64
IDX_TOPK = 256
IDX_SCALE = (IDX_HEADS * IDX_DIM) ** -0.5
CMP_LEN = 32
CMP_STRIDE = 16
CMP_HID = 2 * HEAD_DIM
SEL_BLOCK = 64
SEL_TOPN = 16
WINDOW = 512
MOBA_BLOCK = 256
MOBA_TOPK = 3

LANES = 128
VMEM_LIMIT = 56 * 2 ** 20
NEG = -1e30
KEY_NEG_INF = -2139095041
KEY_POS_INF = 2139095040
INT_MIN = -2 ** 31


def _params(*sem):
    return pltpu.CompilerParams(dimension_semantics=sem, vmem_limit_bytes=VMEM_LIMIT)


def _dot_t(a, b):
    return lax.dot_general(a, b, (((1,), (1,)), ((), ())), preferred_element_type=F32)


def _dot(a, b):
    return jnp.dot(a, b, preferred_element_type=F32)


def _dot_hp(a, b):
    hi = a.astype(BF16)
    r1 = a - hi.astype(F32)
    mid = r1.astype(BF16)
    lo = (r1 - mid.astype(F32)).astype(BF16)
    return _dot(hi, b) + _dot(mid, b) + _dot(lo, b)


def _proj_kernel(x_ref, w_ref, cos_ref, sin_ref, *out_refs, segs):
    acc = _dot(x_ref[...].astype(BF16), w_ref[...])
    tm = acc.shape[0]
    lane = lax.broadcasted_iota(I32, (tm, LANES), 1)
    first_half = (lane & (HEAD_DIM - 1)) < HEAD_DIM // 2
    ones_col = jnp.where(lane == HEAD_DIM, 1.0, 0.0)
    for (c0, width, rope, scale, form), o_ref in zip(segs, out_refs):
        if c0 % LANES:
            o_ref[...] = (acc[:, c0:c0 + width] * scale).astype(o_ref.dtype)
            continue
        for j in range(-(-width // LANES)):
            x = acc[:, c0 + j * LANES:c0 + (j + 1) * LANES]
            if rope:
                swapped = jnp.where(first_half, pltpu.roll(x, LANES - HEAD_DIM // 2, 1), pltpu.roll(x, HEAD_DIM // 2, 1))
                x = x * cos_ref[...] + swapped * sin_ref[...]
            if scale != 1.0:
                x = x * scale
            if form == "tok":
                wj = min(LANES, width - j * LANES)
                o_ref[:, j * LANES:j * LANES + wj] = x[:, :wj].astype(o_ref.dtype)
            elif form == "heads":
                o_ref[0, 2 * j] = x[:, :HEAD_DIM].astype(o_ref.dtype)
                o_ref[0, 2 * j + 1] = x[:, HEAD_DIM:].astype(o_ref.dtype)
            elif form == "keys_t":
                xt = x.T
                o_ref[0, 2 * j] = xt[:HEAD_DIM].astype(o_ref.dtype)
                o_ref[0, 2 * j + 1] = xt[HEAD_DIM:].astype(o_ref.dtype)
            else:
                o_ref[0, 2 * j] = jnp.where(lane < HEAD_DIM, x, ones_col).astype(o_ref.dtype)
                o_ref[0, 2 * j + 1] = jnp.where(lane < HEAD_DIM, pltpu.roll(x, HEAD_DIM, 1), ones_col).astype(o_ref.dtype)


def _proj(x, w, cs, segs, bsz=None):
    m, k = x.shape
    n = w.shape[1]
    tm = min(m, 512)
    cos_t, sin_t = cs
    r_blocks = cos_t.shape[0] // tm
    tab = pl.BlockSpec((tm, LANES), lambda i: (i % r_blocks, 0))
    specs, shapes = [], []
    for _, width, _, _, dt, form in segs:
        nh = width // HEAD_DIM
        if form == "tok":
            specs.append(pl.BlockSpec((tm, width), lambda i: (i, 0)))
            shapes.append(jax.ShapeDtypeStruct((m, width), dt))
            continue
        t = m // bsz
        tpb = t // tm
        if form == "keys_t":
            specs.append(pl.BlockSpec((1, nh, HEAD_DIM, tm), lambda i: (i // tpb, 0, 0, i % tpb)))
            shapes.append(jax.ShapeDtypeStruct((bsz, nh, HEAD_DIM, t), dt))
        else:
            last = HEAD_DIM if form == "heads" else LANES
            specs.append(pl.BlockSpec((1, nh, tm, last), lambda i: (i // tpb, 0, i % tpb, 0)))
            shapes.append(jax.ShapeDtypeStruct((bsz, nh, t, last), dt))
    return pl.pallas_call(
        functools.partial(_proj_kernel, segs=tuple(s[:4] + (s[5],) for s in segs)),
        grid=(m // tm,),
        in_specs=[pl.BlockSpec((tm, k), lambda i: (i, 0)), pl.BlockSpec((k, n), lambda i: (0, 0)), tab, tab],
        out_specs=specs,
        out_shape=shapes,
        compiler_params=_params("parallel"),
        name="in_proj",
    )(x, w, cos_t, sin_t)


def _layer_norm(y, g, b):
    mu = jnp.mean(y, axis=-1, keepdims=True)
    d = y - mu
    var = jnp.mean(d * d, axis=-1, keepdims=True)
    return d * lax.rsqrt(var + LN_EPS) * g + b


def _out_ln_kernel(o_ref, w_ref, x_ref, g_ref, b_ref, y_ref):
    y = ALPHA * x_ref[...] + _dot(o_ref[...].astype(BF16), w_ref[...])
    y_ref[...] = _layer_norm(y, g_ref[...], b_ref[...])


def _out_ln(o, w, x, g, b):
    m, d = x.shape
    k = o.shape[1]
    tm = min(m, 512)
    row = lambda i: (i, 0)
    fix = lambda i: (0, 0)
    return pl.pallas_call(
        _out_ln_kernel,
        grid=(m // tm,),
        in_specs=[pl.BlockSpec((tm, k), row), pl.BlockSpec((k, d), fix), pl.BlockSpec((tm, d), row),
                  pl.BlockSpec((1, d), fix), pl.BlockSpec((1, d), fix)],
        out_specs=pl.BlockSpec((tm, d), row),
        out_shape=jax.ShapeDtypeStruct((m, d), F32),
        compiler_params=_params("parallel"),
        name="out_proj_ln",
    )(o, w, x, g.reshape(1, d), b.reshape(1, d))


def _ffn_ln_kernel(x_ref, wgu_ref, wd_ref, g_ref, b_ref, y_ref, *, d_ff, chunk):
    x = x_ref[...]
    xb = x.astype(BF16)
    acc = jnp.zeros(x.shape, F32)
    for c in range(d_ff // chunk):
        gate = _dot(xb, wgu_ref[:, c * chunk:(c + 1) * chunk])
        up = _dot(xb, wgu_ref[:, d_ff + c * chunk:d_ff + (c + 1) * chunk])
        h = gate * (1.0 / (1.0 + jnp.exp(-gate))) * up
        acc = acc + _dot(h.astype(BF16), wd_ref[c * chunk:(c + 1) * chunk, :])
    y_ref[...] = _layer_norm(ALPHA * x + acc, g_ref[...], b_ref[...])


def _ffn_ln(x, wgu, wd, g, b):
    m, d = x.shape
    d_ff = wd.shape[0]
    tm = min(m, 256)
    row = lambda i: (i, 0)
    fix = lambda i: (0, 0)
    return pl.pallas_call(
        functools.partial(_ffn_ln_kernel, d_ff=d_ff, chunk=256),
        grid=(m // tm,),
        in_specs=[pl.BlockSpec((tm, d), row), pl.BlockSpec((d, 2 * d_ff), fix), pl.BlockSpec((d_ff, d), fix),
                  pl.BlockSpec((1, d), fix), pl.BlockSpec((1, d), fix)],
        out_specs=pl.BlockSpec((tm, d), row),
        out_shape=jax.ShapeDtypeStruct((m, d), F32),
        compiler_params=_params("parallel"),
        name="ffn_ln",
    )(x, wgu, wd, g.reshape(1, d), b.reshape(1, d))


def _sortable(x):
    x = jnp.where(x == 0.0, 0.0, x)
    b = lax.bitcast_convert_type(x, I32)
    return b ^ ((b >> 31) & I32(0x7FFFFFFF))


def _kth_largest_key(u_ref, k):
    rows = u_ref.shape[0]

    def count_ge(cand):
        return jnp.sum((u_ref[...] >= cand).astype(I32), axis=1, keepdims=True)

    base = jnp.where(count_ge(jnp.zeros((rows, 1), I32)) >= k, I32(0), I32(INT_MIN))

    def body(i, base):
        cand = base | jnp.left_shift(I32(1), 30 - i)
        return jnp.where(count_ge(cand) >= k, cand, base)

    return lax.fori_loop(0, 31, body, base)


def _emit_selection(u_ref, k, write):
    rows, n = u_ref.shape
    thr = _kth_largest_key(u_ref, k)
    n_gt = jnp.sum((u_ref[...] > thr).astype(I32), axis=1, keepdims=True)
    need = (k - n_gt).astype(F32)
    r_i = lax.broadcasted_iota(I32, (LANES, LANES), 0)
    c_i = lax.broadcasted_iota(I32, (LANES, LANES), 1)
    tri = jnp.where(r_i <= c_i, 1.0, 0.0).astype(BF16)
    carry = jnp.zeros((rows, 1), F32)
    for ci in range(n // LANES):
        u = u_ref[:, ci * LANES:(ci + 1) * LANES]
        tie = u == thr
        tie_f = jnp.where(tie, 1.0, 0.0)
        inc = _dot(tie_f.astype(BF16), tri)
        rank = carry + inc - tie_f
        sel = (u > thr) | (tie & (rank < need))
        write(ci, sel, u)
        carry = carry + inc[:, LANES - 1:LANES]


def _topk_mask_cols(u_ref, k):
    n, r = u_ref.shape

    def count(pred):
        return jnp.sum(pred(u_ref[...]).astype(I32), axis=0, keepdims=True)

    base = jnp.where(count(lambda u: u >= 0) >= k, I32(0), I32(INT_MIN))

    def radix(i, base):
        cand = base | jnp.left_shift(I32(1), 30 - i)
        return jnp.where(count(lambda u: u >= cand) >= k, cand, base)

    thr = lax.fori_loop(0, 31, radix, base)
    need = (k - count(lambda u: u > thr)).astype(F32)
    u = u_ref[...]
    tie = u == thr
    lower = jnp.where(lax.broadcasted_iota(I32, (n, n), 0) > lax.broadcasted_iota(I32, (n, n), 1), 1.0, 0.0)
    rank = _dot(lower.astype(BF16), jnp.where(tie, 1.0, 0.0).astype(BF16))
    return (u > thr) | (tie & (rank < need)), u


def _flash_prompt_kernel(*refs, kind, tq, tk, n_extra):
    q_ref, k_ref, v_ref = refs[:3]
    extra = refs[3:3 + n_extra]
    o_ref, m_ref, acc_ref, s_ref = refs[3 + n_extra:]
    g = pl.program_id(1)
    q0 = pl.program_id(2) * tq
    rows = GROUP * tq
    q = q_ref[0, 0].reshape(rows, HEAD_DIM)
    m_ref[...] = jnp.full(m_ref.shape, NEG, F32)
    acc_ref[...] = jnp.zeros(acc_ref.shape, F32)
    c_diag = q0 // tk
    c_lo = jnp.maximum(q0 - WINDOW, 0) // tk if kind == "nsa_win" else 0
    if kind == "nsa_sel":
        selb = extra[0][0, 0].astype(BF16)
    if kind == "moba":
        selb = extra[0][0, 0].reshape(rows, extra[0].shape[-1]).astype(BF16)

    def scores(c):
        return _dot(q, k_ref[0, g, :, pl.ds(pl.multiple_of(c * tk, tk), tk)])

    def chunk(c, diag):
        if kind in ("nsa_sel", "moba"):
            hit = _dot(selb, extra[1][c]) > 0.5
        s3 = s_ref[c % 2].reshape(GROUP, tq, tk)
        if not diag:
            s_ref[(c + 1) % 2] = scores(c + 1)
        start = pl.multiple_of(c * tk, tk)
        v = v_ref[0, g, pl.ds(start, tk), :]
        ok = None
        if diag or kind == "nsa_win":
            t_idx = q0 + lax.broadcasted_iota(I32, (GROUP, tq, tk), 1)
            s_idx = start + lax.broadcasted_iota(I32, (GROUP, tq, tk), 2)
            causal = s_idx <= t_idx
        if kind == "fox":
            ck = extra[0][0, 0, :, pl.ds(start, tk)]
            s3 = s3 - ck[:, None, :]
            ok = causal if diag else None
        elif kind == "dsa":
            msk = extra[0][0, :, pl.ds(start, tk)]
            ok = jnp.broadcast_to((msk > 0)[None], (GROUP, tq, tk))
        elif kind == "nsa_sel":
            ok = jnp.broadcast_to(hit[None], (GROUP, tq, tk))
            ok = (ok & causal) if diag else ok
        elif kind == "nsa_win":
            ok = causal & (s_idx >= t_idx - WINDOW)
        else:
            ok = hit.reshape(GROUP, tq, tk)
            if diag:
                ok = ok | (causal & ((s_idx // MOBA_BLOCK) == (t_idx // MOBA_BLOCK)))
        if ok is not None:
            s3 = jnp.where(ok, s3, NEG)
        s = s3.reshape(rows, tk)
        m_prev = m_ref[...]
        m_new = jnp.maximum(m_prev, jnp.max(s, axis=1, keepdims=True))
        alpha = jnp.exp2(m_prev - m_new)
        p = jnp.exp2(s - pltpu.repeat(m_new, tk // LANES, axis=1))
        acc_ref[...] = acc_ref[...] * alpha + _dot(p.astype(BF16), v)
        m_ref[...] = m_new

    def body(c, carry):
        chunk(c, False)
        return carry

    s_ref[c_lo % 2] = scores(c_lo)
    lax.fori_loop(c_lo, c_diag, body, 0)
    chunk(c_diag, True)
    acc = acc_ref[...]
    o = acc[:, :HEAD_DIM] / jnp.maximum(acc[:, HEAD_DIM:HEAD_DIM + 1], 1e-30)
    o_ref[0] = jnp.concatenate([o[j * tq:(j + 1) * tq] for j in range(GROUP)], axis=1)


def _flash_prompt(kind, q, k, v, extra, tq=128, tk=512):
    bsz, _, _, t, _ = q.shape
    tk = min(tk, t)
    tq = min(tq, t)
    qspec = pl.BlockSpec((1, 1, GROUP, tq, HEAD_DIM), lambda b, g, i: (b, g, 0, i, 0))
    kvspec = pl.BlockSpec((1, N_KV_HEADS, HEAD_DIM, t), lambda b, g, i: (b, 0, 0, 0))
    vspec = pl.BlockSpec((1, N_KV_HEADS, t, LANES), lambda b, g, i: (b, 0, 0, 0))
    if kind == "fox":
        especs = [pl.BlockSpec((1, 1, GROUP, t), lambda b, g, i: (b, g, 0, 0))]
    elif kind == "dsa":
        especs = [pl.BlockSpec((1, tq, t), lambda b, g, i: (b, i, 0))]
    elif kind == "nsa_sel":
        especs = [pl.BlockSpec((1, 1, tq, extra[0].shape[-1]), lambda b, g, i: (b, g, i, 0)),
                  pl.BlockSpec(extra[1].shape, lambda b, g, i: (0, 0, 0))]
    elif kind == "moba":
        especs = [pl.BlockSpec((1, 1, GROUP, tq, extra[0].shape[-1]), lambda b, g, i: (b, g, 0, i, 0)),
                  pl.BlockSpec(extra[1].shape, lambda b, g, i: (0, 0, 0))]
    else:
        especs = []
    rows = GROUP * tq
    return pl.pallas_call(
        functools.partial(_flash_prompt_kernel, kind=kind, tq=tq, tk=tk, n_extra=len(extra)),
        grid=(bsz, N_KV_HEADS, t // tq),
        in_specs=[qspec, kvspec, vspec] + especs,
        out_specs=pl.BlockSpec((1, tq, GROUP * HEAD_DIM), lambda b, g, i: (b, i, g)),
        out_shape=jax.ShapeDtypeStruct((bsz, t, Q_DIM), F32),
        scratch_shapes=[pltpu.VMEM((rows, LANES), F32), pltpu.VMEM((rows, LANES), F32),
                        pltpu.VMEM((2, rows, tk), F32)],
        compiler_params=_params("parallel", "parallel", "parallel"),
        name="flash_prompt_" + kind,
    )(q, k, v, *extra)


def _expand_matrix(n_blocks_padded, block, t, tk):
    s = np.arange(t)
    e = (s[None, :] // block == np.arange(n_blocks_padded)[:, None]).astype(np.float32)
    e = e.reshape(n_blocks_padded, t // tk, tk).transpose(1, 0, 2)
    return jnp.asarray(e, dtype=BF16)


def _dsa_select_prompt_kernel(qi_ref, ki_ref, wi_ref, mask_ref, u_ref, *, tq, t, tk, topk):
    q0 = pl.program_id(1) * tq
    n_act = q0 // tk + 1
    sub = tk // LANES
    w = wi_ref[0]
    t_idx = q0 + lax.broadcasted_iota(I32, (tq, tk), 0)
    k_off = lax.broadcasted_iota(I32, (tq, tk), 1)

    def chunk_at(c):
        return pl.ds(pl.multiple_of(c * tk, tk), tk)

    def fill(c, carry):
        kc = ki_ref[0, chunk_at(c), :]
        s = jnp.zeros((tq, tk), F32)
        for h in range(IDX_HEADS):
            s = s + jnp.maximum(_dot_t(qi_ref[0, h], kc), 0.0) * w[:, h:h + 1]
        s = jnp.where(c * tk + k_off <= t_idx, s, -jnp.inf)
        u_ref[:, chunk_at(c)] = _sortable(s)
        return carry

    lax.fori_loop(0, n_act, fill, 0)

    def count(pred):
        def body(c, acc):
            hit = pred(u_ref[:, chunk_at(c)]).astype(I32)
            for i in range(sub):
                acc = acc + hit[:, i * LANES:(i + 1) * LANES]
            return acc
        acc = lax.fori_loop(0, n_act, body, jnp.zeros((tq, LANES), I32))
        return jnp.sum(acc, axis=1, keepdims=True)

    base = jnp.where(count(lambda u: u >= 0) >= topk, I32(0), I32(INT_MIN))

    def radix(i, base):
        cand = base | jnp.left_shift(I32(1), 30 - i)
        return jnp.where(count(lambda u: u >= cand) >= topk, cand, base)

    thr = lax.fori_loop(0, 31, radix, base)
    need = (topk - count(lambda u: u > thr)).astype(F32)
    r_i = lax.broadcasted_iota(I32, (LANES, LANES), 0)
    c_i = lax.broadcasted_iota(I32, (LANES, LANES), 1)
    tri = jnp.where(r_i <= c_i, 1.0, 0.0).astype(BF16)

    def emit(c, carry):
        for i in range(sub):
            at = pl.ds(pl.multiple_of(c * tk + i * LANES, LANES), LANES)
            u = u_ref[:, at]
            tie = u == thr
            tie_f = jnp.where(tie, 1.0, 0.0)
            inc = _dot(tie_f.astype(BF16), tri)
            sel = (u > thr) | (tie & (carry + inc - tie_f < need))
            keep = sel & (u > KEY_NEG_INF) & (u < KEY_POS_INF)
            mask_ref[0, :, at] = jnp.where(keep, 1.0, 0.0).astype(BF16)
            carry = carry + inc[:, LANES - 1:LANES]
        return carry

    lax.fori_loop(0, n_act, emit, jnp.zeros((tq, 1), F32))

    def clear(c, carry):
        mask_ref[0, :, chunk_at(c)] = jnp.zeros((tq, tk), BF16)
        return carry

    lax.fori_loop(n_act, t // tk, clear, 0)


def _dsa_select_prompt(qi, ki, wi, tq=256, tk=512):
    bsz, _, t, _ = qi.shape
    tq, tk = min(tq, t), min(tk, t)
    topk = min(IDX_TOPK, t // 4)
    return pl.pallas_call(
        functools.partial(_dsa_select_prompt_kernel, tq=tq, t=t, tk=tk, topk=topk),
        grid=(bsz, t // tq),
        in_specs=[pl.BlockSpec((1, IDX_HEADS, tq, IDX_DIM), lambda b, i: (b, 0, i, 0)),
                  pl.BlockSpec((1, t, IDX_DIM), lambda b, i: (b, 0, 0)),
                  pl.BlockSpec((1, tq, IDX_HEADS), lambda b, i: (b, i, 0))],
        out_specs=pl.BlockSpec((1, tq, t), lambda b, i: (b, i, 0)),
        out_shape=jax.ShapeDtypeStruct((bsz, t, t), BF16),
        scratch_shapes=[pltpu.VMEM((tq, t), I32)],
        compiler_params=_params("parallel", "parallel"),
        name="dsa_select_prompt",
    )(qi, ki, wi)


def _dsa_select_sample_kernel(pt_ref, qi_ref, wi_ref, *refs, n_pg, n_steps, tpad, n_new, topk, nb):
    pages = refs[:nb * n_pg]
    new_ref, mask_ref, u_ref = refs[nb * n_pg:]
    j = pl.program_id(1)

    def scores(bi, kc):
        rel = jnp.maximum(_dot(qi_ref[bi], kc.astype(BF16)), 0.0) * wi_ref[bi]
        return rel.reshape(tpad, IDX_HEADS, LANES).sum(axis=1)

    @pl.when(j < n_steps - 1)
    def _():
        for bi in range(nb):
            for i in range(n_pg):
                start = pl.multiple_of((j * n_pg + i) * LANES, LANES)
                u_ref[bi * tpad:(bi + 1) * tpad, pl.ds(start, LANES)] = _sortable(scores(bi, pages[bi * n_pg + i][0]))

    @pl.when(j == n_steps - 1)
    def _():
        t_idx = lax.broadcasted_iota(I32, (tpad, LANES), 0)
        c_idx = lax.broadcasted_iota(I32, (tpad, LANES), 1)
        base = (n_steps - 1) * n_pg * LANES
        for bi in range(nb):
            s = jnp.where((c_idx <= t_idx) & (c_idx < n_new), scores(bi, new_ref[bi]), -jnp.inf)
            u_ref[bi * tpad:(bi + 1) * tpad, base:base + LANES] = _sortable(s)
        for i in range(1, n_pg):
            u_ref[:, base + i * LANES:base + (i + 1) * LANES] = jnp.full((nb * tpad, LANES), KEY_NEG_INF, I32)

        def write(ci, sel, u):
            keep = sel & (u > KEY_NEG_INF) & (u < KEY_POS_INF)
            mask_ref[:, :, ci * LANES:(ci + 1) * LANES] = jnp.where(keep, 1.0, 0.0).reshape(nb, tpad, LANES)

        _emit_selection(u_ref, topk, write)


def _dsa_select_sample(page_table, qi, wi, pool_kidx, new_ki, n_new, n_pg=8):
    bsz, n_pages = page_table.shape
    tpad = qi.shape[1] // IDX_HEADS
    n_steps = n_pages // n_pg + 1
    width = n_steps * n_pg * LANES
    topk = min(IDX_TOPK, (n_pages * PAGE_SIZE + n_new) // 4)
    nb = _batch_rows(bsz)
    fix = lambda b, j, pt: (b, 0, 0)

    def page_map(bi, i):
        return lambda b, j, pt: (pt[b * nb + bi, jnp.minimum(j * n_pg + i, n_pages - 1)], 0, 0)

    grid_spec = pltpu.PrefetchScalarGridSpec(
        num_scalar_prefetch=1,
        grid=(bsz // nb, n_steps),
        in_specs=[pl.BlockSpec((nb, tpad * IDX_HEADS, IDX_DIM), fix), pl.BlockSpec((nb, tpad * IDX_HEADS, 1), fix)]
        + [pl.BlockSpec((1, IDX_DIM, PAGE_SIZE), page_map(bi, i)) for bi in range(nb) for i in range(n_pg)]
        + [pl.BlockSpec((nb, IDX_DIM, PAGE_SIZE), fix)],
        out_specs=pl.BlockSpec((nb, tpad, width), fix),
        scratch_shapes=[pltpu.VMEM((nb * tpad, width), I32)],
    )
    return pl.pallas_call(
        functools.partial(_dsa_select_sample_kernel, n_pg=n_pg, n_steps=n_steps, tpad=tpad, n_new=n_new, topk=topk, nb=nb),
        grid_spec=grid_spec,
        out_shape=jax.ShapeDtypeStruct((bsz, tpad, width), F32),
        compiler_params=_params("parallel", "arbitrary"),
        name="dsa_select_sample",
    )(page_table, qi, wi, *([pool_kidx] * (nb * n_pg)), new_ki)


def _batch_rows(bsz):
    return 4 if bsz % 4 == 0 else (2 if bsz % 2 == 0 else 1)


def _page_map(i, n_pg, n_pages):
    return lambda b, j, pt: (pt[b, jnp.minimum(j * n_pg + i, n_pages - 1)], 0, 0)


def _page_map4(i, n_pg, n_pages):
    return lambda b, j, pt: (pt[b, jnp.minimum(j * n_pg + i, n_pages - 1)], 0, 0, 0)


def _paged_flash_kernel(pt_ref, q_ref, *refs, n_pg, n_steps, nb):
    pages = refs[:nb * n_pg]
    new_ref, bias_ref, o_ref, m_ref, l_ref, acc_ref = refs[nb * n_pg:]
    j = pl.program_id(1)

    @pl.when(j == 0)
    def _():
        m_ref[...] = jnp.full(m_ref.shape, NEG, F32)
        l_ref[...] = jnp.zeros(l_ref.shape, F32)
        acc_ref[...] = jnp.zeros(acc_ref.shape, F32)

    def update(bi, kv):
        q = q_ref[bi]
        s = [_dot(q, k().astype(BF16)) + bias_ref[bi, :, i * LANES:(i + 1) * LANES] for i, (k, _) in enumerate(kv)]
        m_prev = m_ref[bi]
        m_new = m_prev
        for si in s:
            m_new = jnp.maximum(m_new, jnp.max(si, axis=1, keepdims=True))
        alpha = jnp.exp(m_prev - m_new)
        l_new = alpha * l_ref[bi]
        acc = acc_ref[bi] * pltpu.repeat(alpha, KV_COLS // LANES, axis=1)
        for si, (_, v) in zip(s, kv):
            p = jnp.where(si > 0.5 * NEG, jnp.exp(si - m_new), 0.0)
            l_new = l_new + jnp.sum(p, axis=1, keepdims=True)
            acc = acc + _dot_t(p.astype(BF16), v().astype(BF16))
        l_ref[bi] = l_new
        acc_ref[bi] = acc
        m_ref[bi] = m_new

    def loaders(ref, lead):
        return (lambda: ref[lead, 0]), (lambda: ref[lead, 1])

    @pl.when(j < n_steps - 1)
    def _():
        for bi in range(nb):
            update(bi, [loaders(pages[bi * n_pg + i], 0) for i in range(n_pg)])

    @pl.when(j == n_steps - 1)
    def _():
        for bi in range(nb):
            update(bi, [loaders(new_ref, bi)])
        o_ref[...] = acc_ref[...] / jnp.maximum(jnp.concatenate([l_ref[...]] * (KV_COLS // LANES), axis=-1), 1e-30)


def _paged_flash(page_table, q_bd, pool_t, new_page_t, bias, n_pg):
    bsz, n_pages = page_table.shape
    n_steps = n_pages // n_pg + 1
    rows = q_bd.shape[1]
    nb = _batch_rows(bsz)
    fix = lambda b, j, pt: (b, 0, 0)
    fix4 = lambda b, j, pt: (b, 0, 0, 0)
    page_block = (1, 2, KV_COLS, PAGE_SIZE)

    def page_map(bi, i):
        return lambda b, j, pt: (pt[b * nb + bi, jnp.minimum(j * n_pg + i, n_pages - 1)], 0, 0, 0)

    grid_spec = pltpu.PrefetchScalarGridSpec(
        num_scalar_prefetch=1,
        grid=(bsz // nb, n_steps),
        in_specs=[pl.BlockSpec((nb, rows, KV_COLS), fix)]
        + [pl.BlockSpec(page_block, page_map(bi, i)) for bi in range(nb) for i in range(n_pg)]
        + [pl.BlockSpec((nb, 2, KV_COLS, PAGE_SIZE), fix4),
           pl.BlockSpec((nb, rows, n_pg * LANES), lambda b, j, pt: (b, 0, j))],
        out_specs=pl.BlockSpec((nb, rows, KV_COLS), fix),
        scratch_shapes=[pltpu.VMEM((nb, rows, LANES), F32), pltpu.VMEM((nb, rows, LANES), F32),
                        pltpu.VMEM((nb, rows, KV_COLS), F32)],
    )
    return pl.pallas_call(
        functools.partial(_paged_flash_kernel, n_pg=n_pg, n_steps=n_steps, nb=nb),
        grid_spec=grid_spec,
        out_shape=jax.ShapeDtypeStruct((bsz, rows, KV_COLS), F32),
        compiler_params=_params("parallel", "arbitrary"),
        name="paged_flash",
    )(page_table, q_bd, *([pool_t] * (nb * n_pg)), new_page_t, bias)


def _cumsum_kernel(pt_ref, *refs, n_pg, n_steps):
    pages = refs[:n_pg]
    new_ref, o_ref, carry_ref = refs[n_pg:]
    j = pl.program_id(1)
    r_i = lax.broadcasted_iota(I32, (LANES, LANES), 0)
    c_i = lax.broadcasted_iota(I32, (LANES, LANES), 1)
    tri = jnp.where(r_i <= c_i, 1.0, 0.0).astype(BF16)

    @pl.when(j == 0)
    def _():
        carry_ref[...] = jnp.zeros(carry_ref.shape, F32)

    ones = jnp.ones((LANES, LANES), BF16)

    def steps(xs):
        local = [_dot_hp(x, tri) for x in xs]
        total = [_dot_hp(x, ones) for x in xs]
        carry = carry_ref[...]
        for i in range(len(xs)):
            o_ref[0, :, i * LANES:(i + 1) * LANES] = local[i] + carry
            carry = carry + total[i]
        carry_ref[...] = carry

    @pl.when(j < n_steps - 1)
    def _():
        steps([pages[i][0] for i in range(n_pg)])

    @pl.when(j == n_steps - 1)
    def _():
        steps([new_ref[0]])
        for i in range(1, n_pg):
            o_ref[0, :, i * LANES:(i + 1) * LANES] = jnp.zeros((N_HEADS, LANES), F32)


def _paged_cumsum(page_table, pool_t, new_t, n_pg):
    bsz, n_pages = page_table.shape
    n_steps = n_pages // n_pg + 1
    fix = lambda b, j, pt: (b, 0, 0)
    grid_spec = pltpu.PrefetchScalarGridSpec(
        num_scalar_prefetch=1,
        grid=(bsz, n_steps),
        in_specs=[pl.BlockSpec((1, N_HEADS, LANES), _page_map(i, n_pg, n_pages)) for i in range(n_pg)]
        + [pl.BlockSpec((1, N_HEADS, LANES), fix)],
        out_specs=pl.BlockSpec((1, N_HEADS, n_pg * LANES), lambda b, j, pt: (b, 0, j)),
        scratch_shapes=[pltpu.VMEM((N_HEADS, LANES), F32)],
    )
    return pl.pallas_call(
        functools.partial(_cumsum_kernel, n_pg=n_pg, n_steps=n_steps),
        grid_spec=grid_spec,
        out_shape=jax.ShapeDtypeStruct((bsz, N_HEADS, n_steps * n_pg * LANES), F32),
        compiler_params=_params("parallel", "arbitrary"),
        name="fox_cumsum",
    )(page_table, *([pool_t] * n_pg), new_t)


def _gelu_tanh(x):
    return 0.5 * x * (1.0 + jnp.tanh(0.7978845608028654 * (x + 0.044715 * x * x * x)))


def _nsa_compress_kernel(pt_ref, *refs, n_pg, n_steps, nc):
    pages = refs[:n_pg]
    pe_ref, w1_ref, w2_ref, o_ref, x_ref = refs[n_pg:]
    j = pl.program_id(1)
    per = PAGE_SIZE // CMP_STRIDE
    n_chunk = x_ref.shape[1]
    for i in range(0, n_pg, 2):
        start = pl.multiple_of((j * n_pg + i) * per, 2 * per)
        for ck in range(2 * N_KV_HEADS):
            x_ref[ck, pl.ds(start, 2 * per), :] = jnp.concatenate([pages[i][0, ck], pages[i + 1][0, ck]], axis=0)

    @pl.when(j == n_steps - 1)
    def _():
        half = CMP_STRIDE * HEAD_DIM
        rows = N_KV_HEADS * n_chunk
        row = lax.broadcasted_iota(I32, (N_KV_HEADS, n_chunk, HEAD_DIM), 1)
        for c in range(2):
            w1 = w1_ref[c]
            part = _dot(x_ref[c * N_KV_HEADS:(c + 1) * N_KV_HEADS].reshape(rows, half), w1)
            pe = pe_ref[c]
            pe_term = _dot(pe[:, :half], w1)[:, :CMP_HID] + _dot(pe[:, half:], w1)[:, CMP_HID:]
            h = pe_term[0:1, :] + part[:, :CMP_HID] + pltpu.roll(part[:, CMP_HID:], rows - 1, 0)
            out = _dot(_gelu_tanh(h).astype(BF16), w2_ref[c]).reshape(N_KV_HEADS, n_chunk, HEAD_DIM)
            o_ref[0, c * N_KV_HEADS:(c + 1) * N_KV_HEADS] = jnp.where(row < nc, out, 0.0)


def _nsa_compress(page_table, pool_t, pe8, w1cat, w2, n_pg=8):
    bsz, n_pages = page_table.shape
    per = PAGE_SIZE // CMP_STRIDE
    n_chunk = n_pages * per
    nc = n_chunk - CMP_LEN // CMP_STRIDE + 1
    n_steps = n_pages // n_pg
    width = CMP_STRIDE * HEAD_DIM

    def page_map(i):
        return lambda b, j, pt: (pt[b, j * n_pg + i], 0, 0, 0)

    fix3 = lambda b, j, pt: (0, 0, 0)
    grid_spec = pltpu.PrefetchScalarGridSpec(
        num_scalar_prefetch=1,
        grid=(bsz, n_steps),
        in_specs=[pl.BlockSpec((1, 2 * N_KV_HEADS, per, width), page_map(i)) for i in range(n_pg)]
        + [pl.BlockSpec(pe8.shape, fix3), pl.BlockSpec(w1cat.shape, fix3), pl.BlockSpec(w2.shape, fix3)],
        out_specs=pl.BlockSpec((1, 2 * N_KV_HEADS, n_chunk, HEAD_DIM), lambda b, j, pt: (b, 0, 0, 0)),
        scratch_shapes=[pltpu.VMEM((2 * N_KV_HEADS, n_chunk, width), BF16)],
    )
    return pl.pallas_call(
        functools.partial(_nsa_compress_kernel, n_pg=n_pg, n_steps=n_steps, nc=nc),
        grid_spec=grid_spec,
        out_shape=jax.ShapeDtypeStruct((bsz, 2 * N_KV_HEADS, n_chunk, HEAD_DIM), F32),
        compiler_params=_params("parallel", "arbitrary"),
        name="nsa_compress",
    )(page_table, *([pool_t] * n_pg), pe8, w1cat, w2), nc


def _nsa_cmp_kernel(q_ref, ck_ref, cv_ref, cover_ref, o_ref, sel_ref, u_ref, *, tq, pos0, nc, n_sel):
    t0 = pos0 + pl.program_id(1) * tq
    ncp = ck_ref.shape[2]
    nsp = cover_ref.shape[0]
    rows = GROUP * tq
    n_idx = lax.broadcasted_iota(I32, (tq, ncp), 1)
    t_idx = t0 + lax.broadcasted_iota(I32, (tq, ncp), 0)
    c_ok = ((n_idx * CMP_STRIDE + CMP_LEN - 1 <= t_idx) & (n_idx < nc))[None]
    blk = lax.broadcasted_iota(I32, (nsp, tq), 0)
    cur = (t0 + lax.broadcasted_iota(I32, (nsp, tq), 1)) // SEL_BLOCK
    forced = (blk == 0) | (blk == cur) | (blk == cur - 1)
    cover_t = cover_ref[...]
    for g in range(N_KV_HEADS):
        q = q_ref[0, g].reshape(rows, HEAD_DIM)
        s3 = jnp.where(c_ok, _dot_t(q, ck_ref[0, g]).reshape(GROUP, tq, ncp), NEG)
        m = jnp.max(s3, axis=-1, keepdims=True)
        e = jnp.where(c_ok, jnp.exp(s3 - m), 0.0)
        p = e / jnp.maximum(jnp.sum(e, axis=-1, keepdims=True), 1e-30)
        o = _dot(p.reshape(rows, ncp).astype(BF16), cv_ref[0, g])
        o_ref[0, :, g * GROUP * HEAD_DIM:(g + 1) * GROUP * HEAD_DIM] = jnp.concatenate(
            [o[j * tq:(j + 1) * tq] for j in range(GROUP)], axis=1)
        psum = p[0] + p[1] + p[2] + p[3]
        hi = psum.astype(BF16)
        r1 = psum - hi.astype(F32)
        mid = r1.astype(BF16)
        lo = (r1 - mid.astype(F32)).astype(BF16)
        imp = _dot_t(cover_t, hi) + _dot_t(cover_t, mid) + _dot_t(cover_t, lo)
        imp = jnp.where(forced, jnp.inf, imp)
        imp = jnp.where(blk <= cur, imp, -jnp.inf)
        u_ref[:, g * tq:(g + 1) * tq] = _sortable(imp)
    sel, u = _topk_mask_cols(u_ref, n_sel)
    sel_ref[0, 0] = jnp.where(sel & (u > KEY_NEG_INF), 1.0, 0.0)


def _nsa_cmp_select(q, cmp_k, cmp_v, pos0, nc, n_keys, tq):
    bsz, _, _, t, _ = q.shape
    ncp = cmp_k.shape[2]
    ns = -(-n_keys // SEL_BLOCK)
    nsp = -(-ns // LANES) * LANES
    n_sel = min(SEL_TOPN, ns)
    c0 = np.arange(ncp)[:, None] * CMP_STRIDE
    s0 = np.arange(nsp)[None, :] * SEL_BLOCK
    cover = (c0 <= s0 + SEL_BLOCK - 1) & (c0 + CMP_LEN - 1 >= s0) & (np.arange(ncp)[:, None] < nc) & (np.arange(nsp)[None, :] < ns)
    cover_t = jnp.asarray(cover.T.astype(np.float32), dtype=BF16)
    nq = t // tq
    qspec = pl.BlockSpec((1, N_KV_HEADS, GROUP, tq, HEAD_DIM), lambda b, i: (b, 0, 0, i, 0))
    cspec = pl.BlockSpec((1, N_KV_HEADS, ncp, HEAD_DIM), lambda b, i: (b, 0, 0, 0))
    o_c, sel = pl.pallas_call(
        functools.partial(_nsa_cmp_kernel, tq=tq, pos0=pos0, nc=nc, n_sel=n_sel),
        grid=(bsz, nq),
        in_specs=[qspec, cspec, cspec, pl.BlockSpec((nsp, ncp), lambda b, i: (0, 0))],
        out_specs=[pl.BlockSpec((1, tq, Q_DIM), lambda b, i: (b, i, 0)),
                   pl.BlockSpec((1, 1, nsp, N_KV_HEADS * tq), lambda b, i: (b, i, 0, 0))],
        out_shape=[jax.ShapeDtypeStruct((bsz, t, Q_DIM), F32), jax.ShapeDtypeStruct((bsz, nq, nsp, N_KV_HEADS * tq), F32)],
        scratch_shapes=[pltpu.VMEM((nsp, N_KV_HEADS * tq), I32)],
        compiler_params=_params("parallel", "parallel"),
        name="nsa_cmp_select",
    )(q, cmp_k, cmp_v, cover_t)
    sel = sel.reshape(bsz, nq, nsp, N_KV_HEADS, tq).transpose(0, 3, 1, 4, 2).reshape(bsz, N_KV_HEADS, t, nsp)
    return o_c, sel


def _kmean_kernel(*refs):
    o_ref = refs[-1]
    tot = jnp.sum(refs[0][0], axis=0, keepdims=True)
    for r in refs[1:-1]:
        tot = tot + jnp.sum(r[0], axis=0, keepdims=True)
    o_ref[0, 0] = tot * (1.0 / MOBA_BLOCK)


def _kmean_prompt(kv):
    bsz, t, _ = kv.shape
    nb = t // MOBA_BLOCK
    return pl.pallas_call(
        _kmean_kernel,
        grid=(bsz, nb),
        in_specs=[pl.BlockSpec((1, MOBA_BLOCK, KV_COLS), lambda b, i: (b, i, 0))],
        out_specs=pl.BlockSpec((1, 1, 1, KV_COLS), lambda b, i: (b, i, 0, 0)),
        out_shape=jax.ShapeDtypeStruct((bsz, nb, 1, KV_COLS), F32),
        compiler_params=_params("parallel", "parallel"),
        name="kmean_prompt",
    )(kv)


def _kmean_sample_kernel(pt_ref, *refs, per):
    o_ref = refs[-1]
    ones = jnp.ones((8, PAGE_SIZE), BF16)
    for n in range(len(refs[:-1]) // per):
        tot = jnp.zeros((8, KV_COLS), F32)
        for r in refs[n * per:(n + 1) * per]:
            x = r[0, 0]
            hi = x.astype(BF16)
            r1 = x - hi.astype(F32)
            mid = r1.astype(BF16)
            lo = (r1 - mid.astype(F32)).astype(BF16)
            tot = tot + _dot_t(ones, hi) + _dot_t(ones, mid) + _dot_t(ones, lo)
        o_ref[0, n] = tot[0:1] * (1.0 / MOBA_BLOCK)


def _kmean_sample(page_table, pool_t, n_pg):
    bsz, n_pages = page_table.shape
    per = MOBA_BLOCK // PAGE_SIZE
    nb = n_pages // per
    grid_spec = pltpu.PrefetchScalarGridSpec(
        num_scalar_prefetch=1,
        grid=(bsz, n_pages // n_pg),
        in_specs=[pl.BlockSpec((1, 1, KV_COLS, PAGE_SIZE), (lambda i: (lambda b, n, pt: (pt[b, n * n_pg + i], 0, 0, 0)))(i))
                  for i in range(n_pg)],
        out_specs=pl.BlockSpec((1, n_pg // per, 1, KV_COLS), lambda b, n, pt: (b, n, 0, 0)),
    )
    return pl.pallas_call(
        functools.partial(_kmean_sample_kernel, per=per),
        grid_spec=grid_spec,
        out_shape=jax.ShapeDtypeStruct((bsz, nb, 1, KV_COLS), F32),
        compiler_params=_params("parallel", "parallel"),
        name="kmean_sample",
    )(page_table, *([pool_t] * n_pg))


def _moba_select_kernel(q_ref, km_ref, sel_ref, *, tq, pos0, k_top):
    t0 = pos0 + pl.program_id(1) * tq
    nbp = km_ref.shape[2]
    rows = GROUP * tq
    blk = lax.broadcasted_iota(I32, (nbp, rows), 0)
    n_past = (t0 + (lax.broadcasted_iota(I32, (nbp, rows), 1) & (tq - 1))) // MOBA_BLOCK
    for g in range(N_KV_HEADS):
        q = q_ref[0, g].reshape(rows, HEAD_DIM)
        s = jnp.where(blk < n_past, _dot_t(km_ref[0, g], q), -jnp.inf)
        sel = jnp.zeros((nbp, rows), F32)
        for _ in range(k_top):
            m = jnp.max(s, axis=0, keepdims=True)
            first = jnp.min(jnp.where(s == m, blk, nbp), axis=0, keepdims=True)
            pick = blk == first
            sel = jnp.where(pick & (m > -jnp.inf), 1.0, sel)
            s = jnp.where(pick, -jnp.inf, s)
        sel_ref[0, 0, g] = sel


def _moba_select(q, kmean, pos0, nb, tq):
    bsz, _, _, t, _ = q.shape
    nbp = kmean.shape[2]
    nq = t // tq
    qspec = pl.BlockSpec((1, N_KV_HEADS, GROUP, tq, HEAD_DIM), lambda b, i: (b, 0, 0, i, 0))
    sel = pl.pallas_call(
        functools.partial(_moba_select_kernel, tq=tq, pos0=pos0, k_top=min(MOBA_TOPK, nb)),
        grid=(bsz, nq),
        in_specs=[qspec, pl.BlockSpec((1, N_KV_HEADS, nbp, HEAD_DIM), lambda b, i: (b, 0, 0, 0))],
        out_specs=pl.BlockSpec((1, 1, N_KV_HEADS, nbp, GROUP * tq), lambda b, i: (b, i, 0, 0, 0)),
        out_shape=jax.ShapeDtypeStruct((bsz, nq, N_KV_HEADS, nbp, GROUP * tq), F32),
        compiler_params=_params("parallel", "parallel"),
        name="moba_select",
    )(q, kmean)
    sel = sel.reshape(bsz, nq, N_KV_HEADS, nbp, GROUP, tq).transpose(0, 2, 4, 1, 5, 3)
    return sel.reshape(bsz, N_KV_HEADS, GROUP, t, nbp)


def _rope_tables(pos):
    half = HEAD_DIM // 2
    inv = ROPE_THETA ** (-jnp.arange(half, dtype=F32) / half)
    ang = pos.astype(F32)[:, None] * inv[None, :]
    cos, sin = jnp.cos(ang), jnp.sin(ang)
    rep = LANES // HEAD_DIM
    return jnp.concatenate([cos, cos] * rep, axis=1), jnp.concatenate([-sin, sin] * rep, axis=1)


def _pad_to(x, axis, size):
    pad = [(0, 0)] * x.ndim
    pad[axis] = (0, size - x.shape[axis])
    return jnp.pad(x, pad)


def _q_groups(q):
    b, t = q.shape[:2]
    return q.transpose(0, 2, 1, 3).reshape(b, N_KV_HEADS, GROUP, t, HEAD_DIM)


def _pages_t(kv):
    return kv.transpose(0, 2, 3, 4, 1).reshape(kv.shape[0], 2, KV_COLS, PAGE_SIZE)


def _fit(x, width):
    return x[..., :width] if x.shape[-1] >= width else _pad_to(x, x.ndim - 1, width)


_HEAD_TO_GROUP = np.equal(np.arange(N_HEADS)[:, None] // GROUP, np.arange(N_KV_HEADS)[None, :]).astype(np.float32)


def _q_block_diag(q):
    s, tn = q.shape[:2]
    qb = q[:, :, :, None, :] * jnp.asarray(_HEAD_TO_GROUP, dtype=q.dtype)[None, None, :, :, None]
    return qb.reshape(s, tn * N_HEADS, KV_COLS)


def _extract_block_diag(o, tn):
    s = o.shape[0]
    o5 = o.reshape(s, tn, N_HEADS, N_KV_HEADS, HEAD_DIM) * _HEAD_TO_GROUP[None, None, :, :, None]
    return o5.sum(axis=3).reshape(s * tn, Q_DIM)


def _new_page(kv_new):
    return _pages_t(_pad_to(kv_new, 1, PAGE_SIZE))


def _pick_pages(n_pages):
    for n in (8, 4, 2, 1):
        if n_pages % n == 0:
            return n


def _identity_pages(bsz, n_pages):
    return jnp.arange(bsz * n_pages, dtype=I32).reshape(bsz, n_pages)


def _rows_th(ok, s, tn):
    return jnp.where(ok, 0.0, NEG).astype(F32).reshape(s, tn * N_HEADS, ok.shape[-1])


def _kv_stack(k, v, bsz, t):
    return jnp.stack([k.reshape(bsz, t, N_KV_HEADS, HEAD_DIM), v.reshape(bsz, t, N_KV_HEADS, HEAD_DIM)], axis=2)


def _kv_forms(c_k, rope):
    return [(c_k, KV_COLS, rope, 1.0, BF16, "keys_t"), (c_k + KV_COLS, KV_COLS, False, 1.0, BF16, "values_1")]


def _dsa_project(x, bsz, t, w_in, cs, q_scale, prompt):
    hm = "heads" if prompt else "tok"
    c_qi = Q_DIM + 2 * KV_COLS
    c_ki = c_qi + IDX_HEADS * IDX_DIM
    segs = [(0, Q_DIM, True, q_scale, BF16, hm), (Q_DIM, KV_COLS, True, 1.0, F32, "tok"),
            (Q_DIM + KV_COLS, KV_COLS, False, 1.0, F32, "tok"), (c_qi, IDX_HEADS * IDX_DIM, True, 1.0, BF16, hm),
            (c_ki, IDX_DIM, True, 1.0, F32, "tok"), (c_ki + IDX_DIM, IDX_HEADS, False, IDX_SCALE, F32, "tok")]
    if prompt:
        segs += _kv_forms(Q_DIM, True)
    q, k, v, qi, ki, wi, *flash_kv = _proj(x, w_in, cs, segs, bsz)
    if not prompt:
        q, qi = q.reshape(bsz, t, N_HEADS, HEAD_DIM), qi.reshape(bsz, t, IDX_HEADS, IDX_DIM)
    return q, qi, wi.reshape(bsz, t, IDX_HEADS), _kv_stack(k, v, bsz, t), ki.reshape(bsz, t, IDX_DIM), flash_kv


def _head_groups(q):
    return q.reshape(q.shape[0], N_KV_HEADS, GROUP, q.shape[2], HEAD_DIM)


def _dsa_prompt(x, bsz, t, w_in, cs):
    q, qi, wi, kv, ki, (k_t, v1) = _dsa_project(x, bsz, t, w_in, cs, ATTN_SCALE * LOG2E, True)
    mask = _dsa_select_prompt(qi, ki.astype(BF16), wi)
    o = _flash_prompt("dsa", _head_groups(q), k_t, v1, [mask])
    return o.reshape(bsz * t, Q_DIM), kv, ki


def _dsa_sample(x, s, tn, w_in, cs, page_table, cache_kv, cache_kidx, n_pg):
    q, qi, wi, kv, ki, _ = _dsa_project(x, s, tn, w_in, cs, ATTN_SCALE, False)
    tpad = 8
    qi_p = _pad_to(qi, 1, tpad).reshape(s, tpad * IDX_HEADS, IDX_DIM)
    wi_p = _pad_to(wi, 1, tpad).reshape(s, tpad * IDX_HEADS, 1)
    mask = _dsa_select_sample(page_table, qi_p, wi_p, cache_kidx.transpose(0, 2, 1),
                              _pad_to(ki, 1, PAGE_SIZE).transpose(0, 2, 1), tn, n_pg)
    ok = jnp.broadcast_to(mask[:, :tn, None, :] > 0.5, (s, tn, N_HEADS, mask.shape[-1]))
    o = _paged_flash(page_table, _q_block_diag(q), _pages_t(cache_kv), _new_page(kv), _rows_th(ok, s, tn), n_pg)
    return _extract_block_diag(o, tn), kv, ki


def _fox_project(x, bsz, t, w_in, b_f, cs, q_scale, prompt):
    segs = [(0, Q_DIM, False, q_scale, BF16, "heads" if prompt else "tok"), (Q_DIM, KV_COLS, False, 1.0, F32, "tok"),
            (Q_DIM + KV_COLS, KV_COLS, False, 1.0, F32, "tok"), (Q_DIM + 2 * KV_COLS, N_HEADS, False, 1.0, F32, "tok")]
    if prompt:
        segs += _kv_forms(Q_DIM, False)
    q, k, v, f, *flash_kv = _proj(x, w_in, cs, segs, bsz)
    logf = jax.nn.log_sigmoid(f.reshape(bsz, t, N_HEADS) + b_f)
    if not prompt:
        q = q.reshape(bsz, t, N_HEADS, HEAD_DIM)
    return q, _kv_stack(k, v, bsz, t), logf, flash_kv


def _fox_prompt(x, bsz, t, w_in, b_f, cs):
    q, kv, logf, (k_t, v1) = _fox_project(x, bsz, t, w_in, b_f, cs, ATTN_SCALE * LOG2E, True)
    n_pages = t // LANES
    pool_t = logf.reshape(bsz, n_pages, LANES, N_HEADS).transpose(0, 1, 3, 2).reshape(bsz * n_pages, N_HEADS, LANES)
    c = _paged_cumsum(_identity_pages(bsz, n_pages), pool_t, jnp.zeros((bsz, N_HEADS, LANES), F32), _pick_pages(n_pages))
    c = (c[:, :, :t] * LOG2E).reshape(bsz, N_KV_HEADS, GROUP, t)
    o = _flash_prompt("fox", _head_groups(q), k_t, v1, [c])
    return o.reshape(bsz * t, Q_DIM), kv, logf


def _fox_sample(x, s, tn, w_in, b_f, cs, page_table, cache_kv, cache_logf, n_pg):
    q, kv, logf, _ = _fox_project(x, s, tn, w_in, b_f, cs, ATTN_SCALE, False)
    past = page_table.shape[1] * PAGE_SIZE
    c = _paged_cumsum(page_table, cache_logf.transpose(0, 2, 1), _pad_to(logf.transpose(0, 2, 1), 2, LANES), n_pg)
    col = jnp.arange(c.shape[-1])
    valid = (col[None, :] < past) | ((col[None, :] - past <= jnp.arange(tn)[:, None]) & (col[None, :] < past + tn))
    bias = jnp.where(valid[None, :, None, :], -c[:, None, :, :], NEG).reshape(s, tn * N_HEADS, c.shape[-1])
    o = _paged_flash(page_table, _q_block_diag(q), _pages_t(cache_kv), _new_page(kv), bias, n_pg)
    return _extract_block_diag(o, tn), kv, logf


def _nsa_project(x, bsz, t, w_in, b_gate, cs, q_scale, prompt):
    hm = "heads" if prompt else "tok"
    c = Q_DIM
    segs = [(0, Q_DIM, False, ATTN_SCALE, BF16, hm), (0, Q_DIM, True, q_scale, BF16, hm)]
    for rope in (False, False, True, False, True, False):
        segs.append((c, KV_COLS, rope, 1.0, F32, "tok"))
        c += KV_COLS
    segs.append((c, 3 * N_HEADS, False, 1.0, F32, "tok"))
    if prompt:
        segs += _kv_forms(Q_DIM + 2 * KV_COLS, True) + _kv_forms(Q_DIM + 4 * KV_COLS, True)
    q, q_rot, kc, vc, ks, vs, kw, vw, g, *flash_kv = _proj(x, w_in, cs, segs, bsz)
    gate = jax.nn.sigmoid(g.reshape(bsz, t, 3 * N_HEADS) + b_gate).reshape(bsz, t, 3, N_HEADS)
    if not prompt:
        q, q_rot = q.reshape(bsz, t, N_HEADS, HEAD_DIM), q_rot.reshape(bsz, t, N_HEADS, HEAD_DIM)
    return (q, q_rot, gate, _kv_stack(kc, vc, bsz, t), _kv_stack(ks, vs, bsz, t), _kv_stack(kw, vw, bsz, t), flash_kv)


def _nsa_weights(pe, w1, w2):
    r = CMP_LEN // CMP_STRIDE
    w1cat = w1.reshape(2, r, CMP_STRIDE * HEAD_DIM, CMP_HID).transpose(0, 2, 1, 3).reshape(2, CMP_STRIDE * HEAD_DIM, r * CMP_HID)
    pe8 = jnp.broadcast_to(pe.reshape(2, 1, CMP_LEN * HEAD_DIM), (2, 8, CMP_LEN * HEAD_DIM))
    return pe8.astype(BF16), w1cat.astype(BF16), w2.astype(BF16)


def _chunk_pages(kv):
    n = kv.shape[0]
    per = PAGE_SIZE // CMP_STRIDE
    x = kv.reshape(n, per, CMP_STRIDE, 2 * N_KV_HEADS, HEAD_DIM).transpose(0, 3, 1, 2, 4)
    return x.reshape(n, 2 * N_KV_HEADS, per, CMP_STRIDE * HEAD_DIM).astype(BF16)


def _gate_mix(gate, o_c, o_s, o_w):
    b, t = gate.shape[:2]
    g = jnp.repeat(gate, HEAD_DIM, axis=-1)
    return (g[:, :, 0] * o_c + g[:, :, 1] * o_s + g[:, :, 2] * o_w).reshape(b * t, Q_DIM)


def _nsa_prompt(x, bsz, t, w_in, b_gate, cmp_w, cs, tq=128, tk=512):
    q, q_rot, gate, kv_cmp, kv_slc, kv_win, (ks_t, vs1, kw_t, vw1) = _nsa_project(x, bsz, t, w_in, b_gate, cs,
                                                                                  ATTN_SCALE * LOG2E, True)
    n_pages = t // PAGE_SIZE
    cmp, nc = _nsa_compress(_identity_pages(bsz, n_pages), _chunk_pages(kv_cmp.reshape(bsz * n_pages, PAGE_SIZE, 2, N_KV_HEADS, HEAD_DIM)),
                            *cmp_w, n_pg=_pick_pages(n_pages))
    cmp = cmp.astype(BF16)
    tq, tk = min(tq, t), min(tk, t)
    o_c, selblk = _nsa_cmp_select(_head_groups(q), cmp[:, :N_KV_HEADS], cmp[:, N_KV_HEADS:], 0, nc, t, tq)
    qg = _head_groups(q_rot)
    e3 = _expand_matrix(selblk.shape[-1], SEL_BLOCK, t, tk)
    o_s = _flash_prompt("nsa_sel", qg, ks_t, vs1, [selblk, e3], tq=2 * tq, tk=tk)
    o_w = _flash_prompt("nsa_win", qg, kw_t, vw1, [], tk=tk)
    return _gate_mix(gate, o_c, o_s, o_w), kv_cmp, kv_slc, kv_win[:, -min(WINDOW, t):]


def _nsa_sample(x, s, tn, w_in, b_gate, cmp_w, cs, page_table, cache_cmp, cache_slc, state_win, n_pg):
    q, q_rot, gate, kv_cmp, kv_slc, kv_win, _ = _nsa_project(x, s, tn, w_in, b_gate, cs, ATTN_SCALE, False)
    past = page_table.shape[1] * PAGE_SIZE
    tpad = 8
    cmp, nc = _nsa_compress(page_table, _chunk_pages(cache_cmp), *cmp_w, n_pg=n_pg)
    cmp = cmp.astype(BF16)
    o_c, selblk = _nsa_cmp_select(_q_groups(_pad_to(q, 1, tpad)), cmp[:, :N_KV_HEADS], cmp[:, N_KV_HEADS:],
                                  past, nc, past + tn, tpad)
    o_c = o_c[:, :tn]
    qbd = _q_block_diag(q_rot)
    width = (page_table.shape[1] // n_pg + 1) * n_pg * LANES
    col = jnp.arange(width)
    pos = past + jnp.arange(tn)
    sel_key = jnp.repeat(selblk[:, :, :tn, :-(-width // SEL_BLOCK)], SEL_BLOCK, axis=-1)[..., :width]
    ok = (sel_key > 0.5) & (col[None, :] <= pos[:, None])[None, None]
    ok = jnp.broadcast_to(ok.transpose(0, 2, 1, 3)[:, :, :, None, :], (s, tn, N_KV_HEADS, GROUP, width))
    o_s = _paged_flash(page_table, qbd, _pages_t(cache_slc), _new_page(kv_slc),
                       _rows_th(ok.reshape(s, tn, N_HEADS, width), s, tn), n_pg)
    win_buf = state_win.shape[1]
    n_wp = win_buf // PAGE_SIZE
    wcol = jnp.arange(2 * n_wp * LANES)
    win_pos = jnp.where(wcol < win_buf, past - win_buf + wcol, jnp.where(wcol < win_buf + tn, past + wcol - win_buf, -1))
    w_ok = (win_pos[None, :] <= pos[:, None]) & (win_pos[None, :] >= pos[:, None] - WINDOW) & (win_pos[None, :] >= 0)
    w_ok = jnp.broadcast_to(w_ok[None, :, None, :], (s, tn, N_HEADS, wcol.shape[0]))
    win_pages = _pages_t(state_win.reshape(s * n_wp, PAGE_SIZE, 2, N_KV_HEADS, HEAD_DIM))
    o_w = _paged_flash(_identity_pages(s, n_wp), qbd, win_pages, _new_page(kv_win), _rows_th(w_ok, s, tn), n_wp)
    unbd = lambda o: _extract_block_diag(o, tn).reshape(s, tn, Q_DIM)
    win = jnp.concatenate([state_win, kv_win], axis=1)[:, -win_buf:]
    return _gate_mix(gate, o_c, unbd(o_s), unbd(o_w)), kv_cmp, kv_slc, win


def _moba_project(x, bsz, t, w_in, cs, q_scale, prompt):
    segs = [(0, Q_DIM, True, q_scale, BF16, "heads" if prompt else "tok"), (Q_DIM, KV_COLS, True, 1.0, F32, "tok"),
            (Q_DIM + KV_COLS, KV_COLS, False, 1.0, F32, "tok")]
    if prompt:
        segs += _kv_forms(Q_DIM, True)
    q, k, v, *flash_kv = _proj(x, w_in, cs, segs, bsz)
    if not prompt:
        q = q.reshape(bsz, t, N_HEADS, HEAD_DIM)
    return q, _kv_stack(k, v, bsz, t), flash_kv


def _kmean_heads(km):
    b, nb = km.shape[:2]
    return _pad_to(km.reshape(b, nb, N_KV_HEADS, HEAD_DIM).transpose(0, 2, 1, 3), 2, -(-nb // 16) * 16).astype(BF16)


def _moba_prompt(x, bsz, t, w_in, cs, tq=128, tk=512):
    q, kv, (k_t, v1) = _moba_project(x, bsz, t, w_in, cs, ATTN_SCALE * LOG2E, True)
    tq, tk = min(tq, t), min(tk, t)
    nb = -(-t // MOBA_BLOCK)
    km = _kmean_heads(_kmean_prompt(kv.reshape(bsz, t, 2 * KV_COLS)))
    qg = _head_groups(q)
    sel = _moba_select(qg, km, 0, nb, tq)
    e3 = _expand_matrix(km.shape[2], MOBA_BLOCK, t, tk)
    o = _flash_prompt("moba", qg, k_t, v1, [sel, e3], tq=2 * tq, tk=tk)
    return o.reshape(bsz * t, Q_DIM), kv


def _moba_sample(x, s, tn, w_in, cs, page_table, cache_kv, n_pg):
    q, kv, _ = _moba_project(x, s, tn, w_in, cs, ATTN_SCALE, False)
    past = page_table.shape[1] * PAGE_SIZE
    tpad = 8
    pool = _pages_t(cache_kv)
    nb = -(-(past + tn) // MOBA_BLOCK)
    km = _kmean_heads(_kmean_sample(page_table, pool, n_pg))
    sel = _moba_select(_q_groups(_pad_to(q, 1, tpad)), km, past, nb, tpad)
    width = (page_table.shape[1] // n_pg + 1) * n_pg * LANES
    col = jnp.arange(width)
    pos = past + jnp.arange(tn)
    sel_key = _fit(jnp.repeat(sel[:, :, :, :tn], MOBA_BLOCK, axis=-1), width)
    own = (col[None, :] // MOBA_BLOCK == pos[:, None] // MOBA_BLOCK) & (col[None, :] <= pos[:, None])
    ok = (sel_key > 0.5) | own[None, None, None]
    ok = ok.transpose(0, 3, 1, 2, 4).reshape(s, tn, N_HEADS, width)
    o = _paged_flash(page_table, _q_block_diag(q), pool, _new_page(kv), _rows_th(ok, s, tn), n_pg)
    return _extract_block_diag(o, tn), kv


def _cast_w(w):
    return _pad_to(w, 1, -(-w.shape[1] // LANES) * LANES).astype(BF16)


def kernel(x_prompt, x_sample, cache_a_kv, cache_a_kidx, cache_b_kv, cache_b_logf, cache_c_cmp_kv, cache_c_slc_kv, state_c_win_kv, cache_d_kv, page_table, a_w_in, a_w_out, b_w_in, b_b_f, b_w_out, c_w_in, c_b_gate, c_cmp_pe, c_cmp_w1, c_cmp_w2, c_w_out, d_w_in, d_w_out, ln_g, ln_b, ffn_w_gu, ffn_w_down):
    bsz, t, d = x_prompt.shape
    s, tn, _ = x_sample.shape
    n_pages = page_table.shape[1]
    past = n_pages * PAGE_SIZE
    n_pg = _pick_pages(n_pages)
    cs_p = _rope_tables(jnp.arange(t, dtype=I32))
    cs_s = _rope_tables(jnp.tile(past + jnp.arange(tn, dtype=I32), s))
    xp = x_prompt.reshape(bsz * t, d)
    xs = x_sample.reshape(s * tn, d)
    cmp_w = _nsa_weights(c_cmp_pe, c_cmp_w1, c_cmp_w2)
    w_out = [_cast_w(w) for w in (a_w_out, b_w_out, c_w_out, d_w_out)]

    op, a_kv_p, a_kidx_p = _dsa_prompt(xp, bsz, t, _cast_w(a_w_in), cs_p)
    os_, a_kv_s, a_kidx_s = _dsa_sample(xs, s, tn, _cast_w(a_w_in), cs_s, page_table, cache_a_kv, cache_a_kidx, n_pg)

    def finish(i, xp, xs, op, os_):
        xp = _out_ln(op, w_out[i], xp, ln_g[i, 0], ln_b[i, 0])
        xs = _out_ln(os_, w_out[i], xs, ln_g[i, 0], ln_b[i, 0])
        wgu, wd = ffn_w_gu[i].astype(BF16), ffn_w_down[i].astype(BF16)
        xp = _ffn_ln(xp, wgu, wd, ln_g[i, 1], ln_b[i, 1])
        xs = _ffn_ln(xs, wgu, wd, ln_g[i, 1], ln_b[i, 1])
        return xp, xs

    xp, xs = finish(0, xp, xs, op, os_)

    op, b_kv_p, b_logf_p = _fox_prompt(xp, bsz, t, _cast_w(b_w_in), b_b_f, cs_p)
    os_, b_kv_s, b_logf_s = _fox_sample(xs, s, tn, _cast_w(b_w_in), b_b_f, cs_s, page_table, cache_b_kv, cache_b_logf, n_pg)
    xp, xs = finish(1, xp, xs, op, os_)

    op, c_cmp_kv_p, c_slc_kv_p, c_win_kv_p = _nsa_prompt(xp, bsz, t, _cast_w(c_w_in), c_b_gate, cmp_w, cs_p)
    os_, c_cmp_kv_s, c_slc_kv_s, c_win_kv_s = _nsa_sample(xs, s, tn, _cast_w(c_w_in), c_b_gate, cmp_w, cs_s, page_table,
                                                          cache_c_cmp_kv, cache_c_slc_kv, state_c_win_kv, n_pg)
    xp, xs = finish(2, xp, xs, op, os_)

    op, d_kv_p = _moba_prompt(xp, bsz, t, _cast_w(d_w_in), cs_p)
    os_, d_kv_s = _moba_sample(xs, s, tn, _cast_w(d_w_in), cs_s, page_table, cache_d_kv, n_pg)
    xp, xs = finish(3, xp, xs, op, os_)

    return (xp.reshape(bsz, t, d), xs.reshape(s, tn, d), a_kv_p, a_kv_s, a_kidx_p, a_kidx_s, b_kv_p, b_kv_s,
            b_logf_p, b_logf_s, c_cmp_kv_p, c_cmp_kv_s, c_slc_kv_p, c_slc_kv_s, c_win_kv_p, c_win_kv_s, d_kv_p, d_kv_s)
```

```python
import functools

import numpy as np
import jax
import jax.numpy as jnp
from jax import lax
from jax.experimental import pallas as pl
from jax.experimental.pallas import tpu as pltpu

F32 = jnp.float32
BF16 = jnp.bfloat16
I32 = jnp.int32

N_HEADS = 16
HEAD_DIM = 64
N_KV_HEADS = 4
GROUP = N_HEADS // N_KV_HEADS
Q_DIM = N_HEADS * HEAD_DIM
KV_COLS = N_KV_HEADS * HEAD_DIM
DEPTH = 4
PAGE_SIZE = 128
ROPE_THETA = 10000.0
LN_EPS = 1e-5
ALPHA = (2 * DEPTH) ** 0.25
ATTN_SCALE = HEAD_DIM ** -0.5
LOG2E = 1.4426950408889634
IDX_HEADS = 8
IDX_DIM = 64
IDX_TOPK = 256
IDX_SCALE = (IDX_HEADS * IDX_DIM) ** -0.5
CMP_LEN = 32
CMP_STRIDE = 16
CMP_HID = 2 * HEAD_DIM
SEL_BLOCK = 64
SEL_TOPN = 16
WINDOW = 512
MOBA_BLOCK = 256
MOBA_TOPK = 3

LANES = 128
VMEM_LIMIT = 56 * 2 ** 20
NEG = -1e30
KEY_NEG_INF = -2139095041
KEY_POS_INF = 2139095040
INT_MIN = -2 ** 31


def _params(*sem):
    return pltpu.CompilerParams(dimension_semantics=sem, vmem_limit_bytes=VMEM_LIMIT)


def _dot_t(a, b):
    return lax.dot_general(a, b, (((1,), (1,)), ((), ())), preferred_element_type=F32)


def _dot(a, b):
    return jnp.dot(a, b, preferred_element_type=F32)


def _dot_hp(a, b):
    hi = a.astype(BF16)
    r1 = a - hi.astype(F32)
    mid = r1.astype(BF16)
    lo = (r1 - mid.astype(F32)).astype(BF16)
    return _dot(hi, b) + _dot(mid, b) + _dot(lo, b)


def _proj_kernel(x_ref, w_ref, cos_ref, sin_ref, *out_refs, segs):
    acc = _dot(x_ref[...].astype(BF16), w_ref[...])
    tm = acc.shape[0]
    lane = lax.broadcasted_iota(I32, (tm, LANES), 1)
    first_half = (lane & (HEAD_DIM - 1)) < HEAD_DIM // 2
    ones_col = jnp.where(lane == HEAD_DIM, 1.0, 0.0)
    for (c0, width, rope, scale, form), o_ref in zip(segs, out_refs):
        if c0 % LANES:
            o_ref[...] = (acc[:, c0:c0 + width] * scale).astype(o_ref.dtype)
            continue
        for j in range(-(-width // LANES)):
            x = acc[:, c0 + j * LANES:c0 + (j + 1) * LANES]
            if rope:
                swapped = jnp.where(first_half, pltpu.roll(x, LANES - HEAD_DIM // 2, 1), pltpu.roll(x, HEAD_DIM // 2, 1))
                x = x * cos_ref[...] + swapped * sin_ref[...]
            if scale != 1.0:
                x = x * scale
            if form == "tok":
                wj = min(LANES, width - j * LANES)
                o_ref[:, j * LANES:j * LANES + wj] = x[:, :wj].astype(o_ref.dtype)
            elif form == "heads":
                o_ref[0, 2 * j] = x[:, :HEAD_DIM].astype(o_ref.dtype)
                o_ref[0, 2 * j + 1] = x[:, HEAD_DIM:].astype(o_ref.dtype)
            elif form == "keys_t":
                xt = x.T
                o_ref[0, 2 * j] = xt[:HEAD_DIM].astype(o_ref.dtype)
                o_ref[0, 2 * j + 1] = xt[HEAD_DIM:].astype(o_ref.dtype)
            else:
                o_ref[0, 2 * j] = jnp.where(lane < HEAD_DIM, x, ones_col).astype(o_ref.dtype)
                o_ref[0, 2 * j + 1] = jnp.where(lane < HEAD_DIM, pltpu.roll(x, HEAD_DIM, 1), ones_col).astype(o_ref.dtype)


def _proj(x, w, cs, segs, bsz=None):
    m, k = x.shape
    n = w.shape[1]
    tm = min(m, 512)
    cos_t, sin_t = cs
    r_blocks = cos_t.shape[0] // tm
    tab = pl.BlockSpec((tm, LANES), lambda i: (i % r_blocks, 0))
    specs, shapes = [], []
    for _, width, _, _, dt, form in segs:
        nh = width // HEAD_DIM
        if form == "tok":
            specs.append(pl.BlockSpec((tm, width), lambda i: (i, 0)))
            shapes.append(jax.ShapeDtypeStruct((m, width), dt))
            continue
        t = m // bsz
        tpb = t // tm
        if form == "keys_t":
            specs.append(pl.BlockSpec((1, nh, HEAD_DIM, tm), lambda i: (i // tpb, 0, 0, i % tpb)))
            shapes.append(jax.ShapeDtypeStruct((bsz, nh, HEAD_DIM, t), dt))
        else:
            last = HEAD_DIM if form == "heads" else LANES
            specs.append(pl.BlockSpec((1, nh, tm, last), lambda i: (i // tpb, 0, i % tpb, 0)))
            shapes.append(jax.ShapeDtypeStruct((bsz, nh, t, last), dt))
    return pl.pallas_call(
        functools.partial(_proj_kernel, segs=tuple(s[:4] + (s[5],) for s in segs)),
        grid=(m // tm,),
        in_specs=[pl.BlockSpec((tm, k), lambda i: (i, 0)), pl.BlockSpec((k, n), lambda i: (0, 0)), tab, tab],
        out_specs=specs,
        out_shape=shapes,
        compiler_params=_params("parallel"),
        name="in_proj",
    )(x, w, cos_t, sin_t)


def _layer_norm(y, g, b):
    mu = jnp.mean(y, axis=-1, keepdims=True)
    d = y - mu
    var = jnp.mean(d * d, axis=-1, keepdims=True)
    return d * lax.rsqrt(var + LN_EPS) * g + b


def _out_ln_kernel(o_ref, w_ref, x_ref, g_ref, b_ref, y_ref):
    y = ALPHA * x_ref[...] + _dot(o_ref[...].astype(BF16), w_ref[...])
    y_ref[...] = _layer_norm(y, g_ref[...], b_ref[...])


def _out_ln(o, w, x, g, b):
    m, d = x.shape
    k = o.shape[1]
    tm = min(m, 512)
    row = lambda i: (i, 0)
    fix = lambda i: (0, 0)
    return pl.pallas_call(
        _out_ln_kernel,
        grid=(m // tm,),
        in_specs=[pl.BlockSpec((tm, k), row), pl.BlockSpec((k, d), fix), pl.BlockSpec((tm, d), row),
                  pl.BlockSpec((1, d), fix), pl.BlockSpec((1, d), fix)],
        out_specs=pl.BlockSpec((tm, d), row),
        out_shape=jax.ShapeDtypeStruct((m, d), F32),
        compiler_params=_params("parallel"),
        name="out_proj_ln",
    )(o, w, x, g.reshape(1, d), b.reshape(1, d))


def _ffn_ln_kernel(x_ref, wgu_ref, wd_ref, g_ref, b_ref, y_ref, *, d_ff, chunk):
    x = x_ref[...]
    xb = x.astype(BF16)
    acc = jnp.zeros(x.shape, F32)
    for c in range(d_ff // chunk):
        gate = _dot(xb, wgu_ref[:, c * chunk:(c + 1) * chunk])
        up = _dot(xb, wgu_ref[:, d_ff + c * chunk:d_ff + (c + 1) * chunk])
        h = gate * (1.0 / (1.0 + jnp.exp(-gate))) * up
        acc = acc + _dot(h.astype(BF16), wd_ref[c * chunk:(c + 1) * chunk, :])
    y_ref[...] = _layer_norm(ALPHA * x + acc, g_ref[...], b_ref[...])


def _ffn_ln(x, wgu, wd, g, b):
    m, d = x.shape
    d_ff = wd.shape[0]
    tm = min(m, 256)
    row = lambda i: (i, 0)
    fix = lambda i: (0, 0)
    return pl.pallas_call(
        functools.partial(_ffn_ln_kernel, d_ff=d_ff, chunk=256),
        grid=(m // tm,),
        in_specs=[pl.BlockSpec((tm, d), row), pl.BlockSpec((d, 2 * d_ff), fix), pl.BlockSpec((d_ff, d), fix),
                  pl.BlockSpec((1, d), fix), pl.BlockSpec((1, d), fix)],
        out_specs=pl.BlockSpec((tm, d), row),
        out_shape=jax.ShapeDtypeStruct((m, d), F32),
        compiler_params=_params("parallel"),
        name="ffn_ln",
    )(x, wgu, wd, g.reshape(1, d), b.reshape(1, d))


def _sortable(x):
    x = jnp.where(x == 0.0, 0.0, x)
    b = lax.bitcast_convert_type(x, I32)
    return b ^ ((b >> 31) & I32(0x7FFFFFFF))


def _kth_largest_key(u_ref, k):
    rows = u_ref.shape[0]

    def count_ge(cand):
        return jnp.sum((u_ref[...] >= cand).astype(I32), axis=1, keepdims=True)

    base = jnp.where(count_ge(jnp.zeros((rows, 1), I32)) >= k, I32(0), I32(INT_MIN))

    def body(i, base):
        cand = base | jnp.left_shift(I32(1), 30 - i)
        return jnp.where(count_ge(cand) >= k, cand, base)

    return lax.fori_loop(0, 31, body, base)


def _emit_selection(u_ref, k, write):
    rows, n = u_ref.shape
    thr = _kth_largest_key(u_ref, k)
    n_gt = jnp.sum((u_ref[...] > thr).astype(I32), axis=1, keepdims=True)
    need = (k - n_gt).astype(F32)
    r_i = lax.broadcasted_iota(I32, (LANES, LANES), 0)
    c_i = lax.broadcasted_iota(I32, (LANES, LANES), 1)
    tri = jnp.where(r_i <= c_i, 1.0, 0.0).astype(BF16)
    carry = jnp.zeros((rows, 1), F32)
    for ci in range(n // LANES):
        u = u_ref[:, ci * LANES:(ci + 1) * LANES]
        tie = u == thr
        tie_f = jnp.where(tie, 1.0, 0.0)
        inc = _dot(tie_f.astype(BF16), tri)
        rank = carry + inc - tie_f
        sel = (u > thr) | (tie & (rank < need))
        write(ci, sel, u)
        carry = carry + inc[:, LANES - 1:LANES]


def _topk_mask_cols(u_ref, k):
    n, r = u_ref.shape

    def count(pred):
        return jnp.sum(pred(u_ref[...]).astype(I32), axis=0, keepdims=True)

    base = jnp.where(count(lambda u: u >= 0) >= k, I32(0), I32(INT_MIN))

    def radix(i, base):
        cand = base | jnp.left_shift(I32(1), 30 - i)
        return jnp.where(count(lambda u: u >= cand) >= k, cand, base)

    thr = lax.fori_loop(0, 31, radix, base)
    need = (k - count(lambda u: u > thr)).astype(F32)
    u = u_ref[...]
    tie = u == thr
    lower = jnp.where(lax.broadcasted_iota(I32, (n, n), 0) > lax.broadcasted_iota(I32, (n, n), 1), 1.0, 0.0)
    rank = _dot(lower.astype(BF16), jnp.where(tie, 1.0, 0.0).astype(BF16))
    return (u > thr) | (tie & (rank < need)), u


def _flash_prompt_kernel(*refs, kind, tq, tk, n_extra):
    q_ref, k_ref, v_ref = refs[:3]
    extra = refs[3:3 + n_extra]
    o_ref, m_ref, acc_ref, s_ref = refs[3 + n_extra:]
    g = pl.program_id(1)
    q0 = pl.program_id(2) * tq
    rows = GROUP * tq
    q = q_ref[0, 0].reshape(rows, HEAD_DIM)
    m_ref[...] = jnp.full(m_ref.shape, NEG, F32)
    acc_ref[...] = jnp.zeros(acc_ref.shape, F32)
    c_diag = q0 // tk
    c_lo = jnp.maximum(q0 - WINDOW, 0) // tk if kind == "nsa_win" else 0
    if kind == "nsa_sel":
        selb = extra[0][0, 0].astype(BF16)
    if kind == "moba":
        selb = extra[0][0, 0].reshape(rows, extra[0].shape[-1]).astype(BF16)

    def scores(c):
        return _dot(q, k_ref[0, g, :, pl.ds(pl.multiple_of(c * tk, tk), tk)])

    def chunk(c, diag, width=tk):
        if kind in ("nsa_sel", "moba"):
            hit = _dot(selb, extra[1][c][:, :width]) > 0.5
        s3 = s_ref[c % 2, :, :width].reshape(GROUP, tq, width)
        if not diag:
            s_ref[(c + 1) % 2] = scores(c + 1)
        start = pl.multiple_of(c * tk, tk)
        v = v_ref[0, g, pl.ds(start, width), :]
        ok = None
        if diag or kind == "nsa_win":
            t_idx = q0 + lax.broadcasted_iota(I32, (GROUP, tq, width), 1)
            s_idx = start + lax.broadcasted_iota(I32, (GROUP, tq, width), 2)
            causal = s_idx <= t_idx
        if kind == "fox":
            ck = extra[0][0, 0, :, pl.ds(start, width)]
            s3 = s3 - ck[:, None, :]
            ok = causal if diag else None
        elif kind == "dsa":
            msk = extra[0][0, :, pl.ds(start, width)]
            ok = jnp.broadcast_to((msk > 0)[None], (GROUP, tq, width))
        elif kind == "nsa_sel":
            ok = jnp.broadcast_to(hit[None], (GROUP, tq, width))
            ok = (ok & causal) if diag else ok
        elif kind == "nsa_win":
            ok = causal & (s_idx >= t_idx - WINDOW)
        else:
            ok = hit.reshape(GROUP, tq, width)
            if diag:
                ok = ok | (causal & ((s_idx // MOBA_BLOCK) == (t_idx // MOBA_BLOCK)))
        if ok is not None:
            s3 = jnp.where(ok, s3, NEG)
        s = s3.reshape(rows, width)
        m_prev = m_ref[...]
        m_new = jnp.maximum(m_prev, jnp.max(s, axis=1, keepdims=True))
        alpha = jnp.exp2(m_prev - m_new)
        p = jnp.exp2(s - (m_new if width == LANES else pltpu.repeat(m_new, width // LANES, axis=1)))
        acc_ref[...] = acc_ref[...] * alpha + _dot(p.astype(BF16), v)
        m_ref[...] = m_new

    def body(c, carry):
        chunk(c, False)
        return carry

    s_ref[c_lo % 2] = scores(c_lo)
    lax.fori_loop(c_lo, c_diag, body, 0)
    groups = (q0 % tk + tq) // LANES
    for n_groups in range(tq // LANES, tk // LANES + 1, tq // LANES):
        pl.when(groups == n_groups)(functools.partial(chunk, c_diag, True, n_groups * LANES))
    acc = acc_ref[...]
    o = acc[:, :HEAD_DIM] / jnp.maximum(acc[:, HEAD_DIM:HEAD_DIM + 1], 1e-30)
    o_ref[0] = jnp.concatenate([o[j * tq:(j + 1) * tq] for j in range(GROUP)], axis=1)


def _flash_prompt(kind, q, k, v, extra, tq=128, tk=512):
    bsz, _, _, t, _ = q.shape
    tk = min(tk, t)
    tq = min(tq, t)
    qspec = pl.BlockSpec((1, 1, GROUP, tq, HEAD_DIM), lambda b, g, i: (b, g, 0, i, 0))
    kvspec = pl.BlockSpec((1, N_KV_HEADS, HEAD_DIM, t), lambda b, g, i: (b, 0, 0, 0))
    vspec = pl.BlockSpec((1, N_KV_HEADS, t, LANES), lambda b, g, i: (b, 0, 0, 0))
    if kind == "fox":
        especs = [pl.BlockSpec((1, 1, GROUP, t), lambda b, g, i: (b, g, 0, 0))]
    elif kind == "dsa":
        especs = [pl.BlockSpec((1, tq, t), lambda b, g, i: (b, i, 0))]
    elif kind == "nsa_sel":
        especs = [pl.BlockSpec((1, 1, tq, extra[0].shape[-1]), lambda b, g, i: (b, g, i, 0)),
                  pl.BlockSpec(extra[1].shape, lambda b, g, i: (0, 0, 0))]
    elif kind == "moba":
        especs = [pl.BlockSpec((1, 1, GROUP, tq, extra[0].shape[-1]), lambda b, g, i: (b, g, 0, i, 0)),
                  pl.BlockSpec(extra[1].shape, lambda b, g, i: (0, 0, 0))]
    else:
        especs = []
    rows = GROUP * tq
    return pl.pallas_call(
        functools.partial(_flash_prompt_kernel, kind=kind, tq=tq, tk=tk, n_extra=len(extra)),
        grid=(bsz, N_KV_HEADS, t // tq),
        in_specs=[qspec, kvspec, vspec] + especs,
        out_specs=pl.BlockSpec((1, tq, GROUP * HEAD_DIM), lambda b, g, i: (b, i, g)),
        out_shape=jax.ShapeDtypeStruct((bsz, t, Q_DIM), F32),
        scratch_shapes=[pltpu.VMEM((rows, LANES), F32), pltpu.VMEM((rows, LANES), F32),
                        pltpu.VMEM((2, rows, tk), F32)],
        compiler_params=_params("parallel", "parallel", "parallel"),
        name="flash_prompt_" + kind,
    )(q, k, v, *extra)


def _expand_matrix(n_blocks_padded, block, t, tk):
    s = np.arange(t)
    e = (s[None, :] // block == np.arange(n_blocks_padded)[:, None]).astype(np.float32)
    e = e.reshape(n_blocks_padded, t // tk, tk).transpose(1, 0, 2)
    return jnp.asarray(e, dtype=BF16)


def _dsa_select_prompt_kernel(qi_ref, ki_ref, wi_ref, mask_ref, u_ref, *, tq, t, tk, topk):
    q0 = pl.program_id(1) * tq
    n_act = q0 // tk + 1
    sub = tk // LANES
    w = wi_ref[0]
    t_idx = q0 + lax.broadcasted_iota(I32, (tq, tk), 0)
    k_off = lax.broadcasted_iota(I32, (tq, tk), 1)

    def chunk_at(c):
        return pl.ds(pl.multiple_of(c * tk, tk), tk)

    def fill(c, carry):
        kc = ki_ref[0, chunk_at(c), :]
        s = jnp.zeros((tq, tk), F32)
        for h in range(IDX_HEADS):
            s = s + jnp.maximum(_dot_t(qi_ref[0, h], kc), 0.0) * w[:, h:h + 1]
        s = jnp.where(c * tk + k_off <= t_idx, s, -jnp.inf)
        u_ref[:, chunk_at(c)] = _sortable(s)
        return carry

    lax.fori_loop(0, n_act, fill, 0)

    def count(pred):
        def body(c, acc):
            hit = pred(u_ref[:, chunk_at(c)]).astype(I32)
            for i in range(sub):
                acc = acc + hit[:, i * LANES:(i + 1) * LANES]
            return acc
        acc = lax.fori_loop(0, n_act, body, jnp.zeros((tq, LANES), I32))
        return jnp.sum(acc, axis=1, keepdims=True)

    base = jnp.where(count(lambda u: u >= 0) >= topk, I32(0), I32(INT_MIN))

    def radix(i, base):
        cand = base | jnp.left_shift(I32(1), 30 - i)
        return jnp.where(count(lambda u: u >= cand) >= topk, cand, base)

    thr = lax.fori_loop(0, 31, radix, base)
    need = (topk - count(lambda u: u > thr)).astype(F32)
    r_i = lax.broadcasted_iota(I32, (LANES, LANES), 0)
    c_i = lax.broadcasted_iota(I32, (LANES, LANES), 1)
    tri = jnp.where(r_i <= c_i, 1.0, 0.0).astype(BF16)

    def emit(c, carry):
        for i in range(sub):
            at = pl.ds(pl.multiple_of(c * tk + i * LANES, LANES), LANES)
            u = u_ref[:, at]
            tie = u == thr
            tie_f = jnp.where(tie, 1.0, 0.0)
            inc = _dot(tie_f.astype(BF16), tri)
            sel = (u > thr) | (tie & (carry + inc - tie_f < need))
            keep = sel & (u > KEY_NEG_INF) & (u < KEY_POS_INF)
            mask_ref[0, :, at] = jnp.where(keep, 1.0, 0.0).astype(BF16)
            carry = carry + inc[:, LANES - 1:LANES]
        return carry

    lax.fori_loop(0, n_act, emit, jnp.zeros((tq, 1), F32))

    def clear(c, carry):
        mask_ref[0, :, chunk_at(c)] = jnp.zeros((tq, tk), BF16)
        return carry

    lax.fori_loop(n_act, t // tk, clear, 0)


def _dsa_select_prompt(qi, ki, wi, tq=256, tk=512):
    bsz, _, t, _ = qi.shape
    tq, tk = min(tq, t), min(tk, t)
    topk = min(IDX_TOPK, t // 4)
    return pl.pallas_call(
        functools.partial(_dsa_select_prompt_kernel, tq=tq, t=t, tk=tk, topk=topk),
        grid=(bsz, t // tq),
        in_specs=[pl.BlockSpec((1, IDX_HEADS, tq, IDX_DIM), lambda b, i: (b, 0, i, 0)),
                  pl.BlockSpec((1, t, IDX_DIM), lambda b, i: (b, 0, 0)),
                  pl.BlockSpec((1, tq, IDX_HEADS), lambda b, i: (b, i, 0))],
        out_specs=pl.BlockSpec((1, tq, t), lambda b, i: (b, i, 0)),
        out_shape=jax.ShapeDtypeStruct((bsz, t, t), BF16),
        scratch_shapes=[pltpu.VMEM((tq, t), I32)],
        compiler_params=_params("parallel", "parallel"),
        name="dsa_select_prompt",
    )(qi, ki, wi)


def _dsa_select_sample_kernel(pt_ref, qi_ref, wi_ref, *refs, n_pg, n_steps, tpad, n_new, topk, nb):
    pages = refs[:nb * n_pg]
    new_ref, mask_ref, u_ref = refs[nb * n_pg:]
    j = pl.program_id(1)

    def scores(bi, kc):
        rel = jnp.maximum(_dot(qi_ref[bi], kc.astype(BF16)), 0.0) * wi_ref[bi]
        return rel.reshape(tpad, IDX_HEADS, LANES).sum(axis=1)

    @pl.when(j < n_steps - 1)
    def _():
        for bi in range(nb):
            for i in range(n_pg):
                start = pl.multiple_of((j * n_pg + i) * LANES, LANES)
                u_ref[bi * tpad:(bi + 1) * tpad, pl.ds(start, LANES)] = _sortable(scores(bi, pages[bi * n_pg + i][0]))

    @pl.when(j == n_steps - 1)
    def _():
        t_idx = lax.broadcasted_iota(I32, (tpad, LANES), 0)
        c_idx = lax.broadcasted_iota(I32, (tpad, LANES), 1)
        base = (n_steps - 1) * n_pg * LANES
        for bi in range(nb):
            s = jnp.where((c_idx <= t_idx) & (c_idx < n_new), scores(bi, new_ref[bi]), -jnp.inf)
            u_ref[bi * tpad:(bi + 1) * tpad, base:base + LANES] = _sortable(s)
        for i in range(1, n_pg):
            u_ref[:, base + i * LANES:base + (i + 1) * LANES] = jnp.full((nb * tpad, LANES), KEY_NEG_INF, I32)

        def write(ci, sel, u):
            keep = sel & (u > KEY_NEG_INF) & (u < KEY_POS_INF)
            mask_ref[:, :, ci * LANES:(ci + 1) * LANES] = jnp.where(keep, 1.0, 0.0).reshape(nb, tpad, LANES)

        _emit_selection(u_ref, topk, write)


def _dsa_select_sample(page_table, qi, wi, pool_kidx, new_ki, n_new, n_pg=8):
    bsz, n_pages = page_table.shape
    tpad = qi.shape[1] // IDX_HEADS
    n_steps = n_pages // n_pg + 1
    width = n_steps * n_pg * LANES
    topk = min(IDX_TOPK, (n_pages * PAGE_SIZE + n_new) // 4)
    nb = _batch_rows(bsz)
    fix = lambda b, j, pt: (b, 0, 0)

    def page_map(bi, i):
        return lambda b, j, pt: (pt[b * nb + bi, jnp.minimum(j * n_pg + i, n_pages - 1)], 0, 0)

    grid_spec = pltpu.PrefetchScalarGridSpec(
        num_scalar_prefetch=1,
        grid=(bsz // nb, n_steps),
        in_specs=[pl.BlockSpec((nb, tpad * IDX_HEADS, IDX_DIM), fix), pl.BlockSpec((nb, tpad * IDX_HEADS, 1), fix)]
        + [pl.BlockSpec((1, IDX_DIM, PAGE_SIZE), page_map(bi, i)) for bi in range(nb) for i in range(n_pg)]
        + [pl.BlockSpec((nb, IDX_DIM, PAGE_SIZE), fix)],
        out_specs=pl.BlockSpec((nb, tpad, width), fix),
        scratch_shapes=[pltpu.VMEM((nb * tpad, width), I32)],
    )
    return pl.pallas_call(
        functools.partial(_dsa_select_sample_kernel, n_pg=n_pg, n_steps=n_steps, tpad=tpad, n_new=n_new, topk=topk, nb=nb),
        grid_spec=grid_spec,
        out_shape=jax.ShapeDtypeStruct((bsz, tpad, width), F32),
        compiler_params=_params("parallel", "arbitrary"),
        name="dsa_select_sample",
    )(page_table, qi, wi, *([pool_kidx] * (nb * n_pg)), new_ki)


def _batch_rows(bsz):
    return 4 if bsz % 4 == 0 else (2 if bsz % 2 == 0 else 1)


def _page_map(i, n_pg, n_pages):
    return lambda b, j, pt: (pt[b, jnp.minimum(j * n_pg + i, n_pages - 1)], 0, 0)


def _page_map4(i, n_pg, n_pages):
    return lambda b, j, pt: (pt[b, jnp.minimum(j * n_pg + i, n_pages - 1)], 0, 0, 0)


def _paged_flash_kernel(pt_ref, q_ref, *refs, n_pg, n_steps, nb):
    pages = refs[:nb * n_pg]
    new_ref, bias_ref, o_ref, m_ref, l_ref, acc_ref = refs[nb * n_pg:]
    j = pl.program_id(1)

    @pl.when(j == 0)
    def _():
        m_ref[...] = jnp.full(m_ref.shape, NEG, F32)
        l_ref[...] = jnp.zeros(l_ref.shape, F32)
        acc_ref[...] = jnp.zeros(acc_ref.shape, F32)

    def update(bi, kv):
        q = q_ref[bi]
        s = [_dot(q, k().astype(BF16)) + bias_ref[bi, :, i * LANES:(i + 1) * LANES] for i, (k, _) in enumerate(kv)]
        m_prev = m_ref[bi]
        m_new = m_prev
        for si in s:
            m_new = jnp.maximum(m_new, jnp.max(si, axis=1, keepdims=True))
        alpha = jnp.exp(m_prev - m_new)
        l_new = alpha * l_ref[bi]
        acc = acc_ref[bi] * pltpu.repeat(alpha, KV_COLS // LANES, axis=1)
        for si, (_, v) in zip(s, kv):
            p = jnp.where(si > 0.5 * NEG, jnp.exp(si - m_new), 0.0)
            l_new = l_new + jnp.sum(p, axis=1, keepdims=True)
            acc = acc + _dot_t(p.astype(BF16), v().astype(BF16))
        l_ref[bi] = l_new
        acc_ref[bi] = acc
        m_ref[bi] = m_new

    def loaders(ref, lead):
        return (lambda: ref[lead, 0]), (lambda: ref[lead, 1])

    @pl.when(j < n_steps - 1)
    def _():
        for bi in range(nb):
            update(bi, [loaders(pages[bi * n_pg + i], 0) for i in range(n_pg)])

    @pl.when(j == n_steps - 1)
    def _():
        for bi in range(nb):
            update(bi, [loaders(new_ref, bi)])
        o_ref[...] = acc_ref[...] / jnp.maximum(jnp.concatenate([l_ref[...]] * (KV_COLS // LANES), axis=-1), 1e-30)


def _paged_flash(page_table, q_bd, pool_t, new_page_t, bias, n_pg):
    bsz, n_pages = page_table.shape
    n_steps = n_pages // n_pg + 1
    rows = q_bd.shape[1]
    nb = _batch_rows(bsz)
    fix = lambda b, j, pt: (b, 0, 0)
    fix4 = lambda b, j, pt: (b, 0, 0, 0)
    page_block = (1, 2, KV_COLS, PAGE_SIZE)

    def page_map(bi, i):
        return lambda b, j, pt: (pt[b * nb + bi, jnp.minimum(j * n_pg + i, n_pages - 1)], 0, 0, 0)

    grid_spec = pltpu.PrefetchScalarGridSpec(
        num_scalar_prefetch=1,
        grid=(bsz // nb, n_steps),
        in_specs=[pl.BlockSpec((nb, rows, KV_COLS), fix)]
        + [pl.BlockSpec(page_block, page_map(bi, i)) for bi in range(nb) for i in range(n_pg)]
        + [pl.BlockSpec((nb, 2, KV_COLS, PAGE_SIZE), fix4),
           pl.BlockSpec((nb, rows, n_pg * LANES), lambda b, j, pt: (b, 0, j))],
        out_specs=pl.BlockSpec((nb, rows, KV_COLS), fix),
        scratch_shapes=[pltpu.VMEM((nb, rows, LANES), F32), pltpu.VMEM((nb, rows, LANES), F32),
                        pltpu.VMEM((nb, rows, KV_COLS), F32)],
    )
    return pl.pallas_call(
        functools.partial(_paged_flash_kernel, n_pg=n_pg, n_steps=n_steps, nb=nb),
        grid_spec=grid_spec,
        out_shape=jax.ShapeDtypeStruct((bsz, rows, KV_COLS), F32),
        compiler_params=_params("parallel", "arbitrary"),
        name="paged_flash",
    )(page_table, q_bd, *([pool_t] * (nb * n_pg)), new_page_t, bias)


def _cumsum_kernel(pt_ref, *refs, n_pg, n_steps, nb):
    pages = refs[:nb * n_pg]
    new_ref, o_ref, carry_ref = refs[nb * n_pg:]
    j = pl.program_id(1)
    r_i = lax.broadcasted_iota(I32, (LANES, LANES), 0)
    c_i = lax.broadcasted_iota(I32, (LANES, LANES), 1)
    tri = jnp.where(r_i <= c_i, 1.0, 0.0).astype(BF16)

    @pl.when(j == 0)
    def _():
        carry_ref[...] = jnp.zeros(carry_ref.shape, F32)

    ones = jnp.ones((LANES, LANES), BF16)

    def steps(bi, xs):
        local = [_dot_hp(x, tri) for x in xs]
        total = [_dot_hp(x, ones) for x in xs]
        carry = carry_ref[bi]
        for i in range(len(xs)):
            o_ref[bi, :, i * LANES:(i + 1) * LANES] = local[i] + carry
            carry = carry + total[i]
        carry_ref[bi] = carry

    @pl.when(j < n_steps - 1)
    def _():
        for bi in range(nb):
            steps(bi, [pages[bi * n_pg + i][0] for i in range(n_pg)])

    @pl.when(j == n_steps - 1)
    def _():
        for bi in range(nb):
            steps(bi, [new_ref[bi]])
        for i in range(1, n_pg):
            o_ref[:, :, i * LANES:(i + 1) * LANES] = jnp.zeros((nb, N_HEADS, LANES), F32)


def _paged_cumsum(page_table, pool_t, new_t, n_pg):
    bsz, n_pages = page_table.shape
    n_steps = n_pages // n_pg + 1
    nb = _batch_rows(bsz)
    fix = lambda b, j, pt: (b, 0, 0)

    def page_map(bi, i):
        return lambda b, j, pt: (pt[b * nb + bi, jnp.minimum(j * n_pg + i, n_pages - 1)], 0, 0)

    grid_spec = pltpu.PrefetchScalarGridSpec(
        num_scalar_prefetch=1,
        grid=(bsz // nb, n_steps),
        in_specs=[pl.BlockSpec((1, N_HEADS, LANES), page_map(bi, i)) for bi in range(nb) for i in range(n_pg)]
        + [pl.BlockSpec((nb, N_HEADS, LANES), fix)],
        out_specs=pl.BlockSpec((nb, N_HEADS, n_pg * LANES), lambda b, j, pt: (b, 0, j)),
        scratch_shapes=[pltpu.VMEM((nb, N_HEADS, LANES), F32)],
    )
    return pl.pallas_call(
        functools.partial(_cumsum_kernel, n_pg=n_pg, n_steps=n_steps, nb=nb),
        grid_spec=grid_spec,
        out_shape=jax.ShapeDtypeStruct((bsz, N_HEADS, n_steps * n_pg * LANES), F32),
        compiler_params=_params("parallel", "arbitrary"),
        name="fox_cumsum",
    )(page_table, *([pool_t] * (nb * n_pg)), new_t)


def _gelu_tanh(x):
    return 0.5 * x * (1.0 + jnp.tanh(0.7978845608028654 * (x + 0.044715 * x * x * x)))


def _nsa_compress_kernel(pt_ref, *refs, n_pg, n_steps, nc):
    pages = refs[:n_pg]
    pe_ref, w1_ref, w2_ref, o_ref, x_ref = refs[n_pg:]
    j = pl.program_id(1)
    per = PAGE_SIZE // CMP_STRIDE
    n_chunk = x_ref.shape[1]
    for i in range(0, n_pg, 2):
        start = pl.multiple_of((j * n_pg + i) * per, 2 * per)
        for ck in range(2 * N_KV_HEADS):
            x_ref[ck, pl.ds(start, 2 * per), :] = jnp.concatenate([pages[i][0, ck], pages[i + 1][0, ck]], axis=0)

    @pl.when(j == n_steps - 1)
    def _():
        half = CMP_STRIDE * HEAD_DIM
        rows = N_KV_HEADS * n_chunk
        row = lax.broadcasted_iota(I32, (N_KV_HEADS, n_chunk, HEAD_DIM), 1)
        for c in range(2):
            w1 = w1_ref[c]
            part = _dot(x_ref[c * N_KV_HEADS:(c + 1) * N_KV_HEADS].reshape(rows, half), w1)
            pe = pe_ref[c]
            pe_term = _dot(pe[:, :half], w1)[:, :CMP_HID] + _dot(pe[:, half:], w1)[:, CMP_HID:]
            h = pe_term[0:1, :] + part[:, :CMP_HID] + pltpu.roll(part[:, CMP_HID:], rows - 1, 0)
            out = _dot(_gelu_tanh(h).astype(BF16), w2_ref[c]).reshape(N_KV_HEADS, n_chunk, HEAD_DIM)
            o_ref[0, c * N_KV_HEADS:(c + 1) * N_KV_HEADS] = jnp.where(row < nc, out, 0.0)


def _nsa_compress(page_table, pool_t, pe8, w1cat, w2, n_pg=8):
    bsz, n_pages = page_table.shape
    per = PAGE_SIZE // CMP_STRIDE
    n_chunk = n_pages * per
    nc = n_chunk - CMP_LEN // CMP_STRIDE + 1
    n_steps = n_pages // n_pg
    width = CMP_STRIDE * HEAD_DIM

    def page_map(i):
        return lambda b, j, pt: (pt[b, j * n_pg + i], 0, 0, 0)

    fix3 = lambda b, j, pt: (0, 0, 0)
    grid_spec = pltpu.PrefetchScalarGridSpec(
        num_scalar_prefetch=1,
        grid=(bsz, n_steps),
        in_specs=[pl.BlockSpec((1, 2 * N_KV_HEADS, per, width), page_map(i)) for i in range(n_pg)]
        + [pl.BlockSpec(pe8.shape, fix3), pl.BlockSpec(w1cat.shape, fix3), pl.BlockSpec(w2.shape, fix3)],
        out_specs=pl.BlockSpec((1, 2 * N_KV_HEADS, n_chunk, HEAD_DIM), lambda b, j, pt: (b, 0, 0, 0)),
        scratch_shapes=[pltpu.VMEM((2 * N_KV_HEADS, n_chunk, width), BF16)],
    )
    return pl.pallas_call(
        functools.partial(_nsa_compress_kernel, n_pg=n_pg, n_steps=n_steps, nc=nc),
        grid_spec=grid_spec,
        out_shape=jax.ShapeDtypeStruct((bsz, 2 * N_KV_HEADS, n_chunk, HEAD_DIM), F32),
        compiler_params=_params("parallel", "arbitrary"),
        name="nsa_compress",
    )(page_table, *([pool_t] * n_pg), pe8, w1cat, w2), nc


def _nsa_cmp_kernel(q_ref, ck_ref, cv_ref, cover_ref, o_ref, sel_ref, u_ref, *, tq, pos0, nc, n_sel):
    t0 = pos0 + pl.program_id(1) * tq
    ncp = ck_ref.shape[2]
    nsp = cover_ref.shape[0]
    rows = GROUP * tq
    n_idx = lax.broadcasted_iota(I32, (tq, ncp), 1)
    t_idx = t0 + lax.broadcasted_iota(I32, (tq, ncp), 0)
    c_ok = ((n_idx * CMP_STRIDE + CMP_LEN - 1 <= t_idx) & (n_idx < nc))[None]
    blk = lax.broadcasted_iota(I32, (nsp, tq), 0)
    cur = (t0 + lax.broadcasted_iota(I32, (nsp, tq), 1)) // SEL_BLOCK
    forced = (blk == 0) | (blk == cur) | (blk == cur - 1)
    cover_t = cover_ref[...]
    for g in range(N_KV_HEADS):
        q = q_ref[0, g].reshape(rows, HEAD_DIM)
        s3 = jnp.where(c_ok, _dot_t(q, ck_ref[0, g]).reshape(GROUP, tq, ncp), NEG)
        m = jnp.max(s3, axis=-1, keepdims=True)
        e = jnp.where(c_ok, jnp.exp(s3 - m), 0.0)
        p = e / jnp.maximum(jnp.sum(e, axis=-1, keepdims=True), 1e-30)
        o = _dot(p.reshape(rows, ncp).astype(BF16), cv_ref[0, g])
        o_ref[0, :, g * GROUP * HEAD_DIM:(g + 1) * GROUP * HEAD_DIM] = jnp.concatenate(
            [o[j * tq:(j + 1) * tq] for j in range(GROUP)], axis=1)
        psum = p[0] + p[1] + p[2] + p[3]
        hi = psum.astype(BF16)
        r1 = psum - hi.astype(F32)
        mid = r1.astype(BF16)
        lo = (r1 - mid.astype(F32)).astype(BF16)
        imp = _dot_t(cover_t, hi) + _dot_t(cover_t, mid) + _dot_t(cover_t, lo)
        imp = jnp.where(forced, jnp.inf, imp)
        imp = jnp.where(blk <= cur, imp, -jnp.inf)
        u_ref[:, g * tq:(g + 1) * tq] = _sortable(imp)
    sel, u = _topk_mask_cols(u_ref, n_sel)
    sel_ref[0, 0] = jnp.where(sel & (u > KEY_NEG_INF), 1.0, 0.0)


def _nsa_cmp_select(q, cmp_k, cmp_v, pos0, nc, n_keys, tq):
    bsz, _, _, t, _ = q.shape
    ncp = cmp_k.shape[2]
    ns = -(-n_keys // SEL_BLOCK)
    nsp = -(-ns // LANES) * LANES
    n_sel = min(SEL_TOPN, ns)
    c0 = np.arange(ncp)[:, None] * CMP_STRIDE
    s0 = np.arange(nsp)[None, :] * SEL_BLOCK
    cover = (c0 <= s0 + SEL_BLOCK - 1) & (c0 + CMP_LEN - 1 >= s0) & (np.arange(ncp)[:, None] < nc) & (np.arange(nsp)[None, :] < ns)
    cover_t = jnp.asarray(cover.T.astype(np.float32), dtype=BF16)
    nq = t // tq
    qspec = pl.BlockSpec((1, N_KV_HEADS, GROUP, tq, HEAD_DIM), lambda b, i: (b, 0, 0, i, 0))
    cspec = pl.BlockSpec((1, N_KV_HEADS, ncp, HEAD_DIM), lambda b, i: (b, 0, 0, 0))
    o_c, sel = pl.pallas_call(
        functools.partial(_nsa_cmp_kernel, tq=tq, pos0=pos0, nc=nc, n_sel=n_sel),
        grid=(bsz, nq),
        in_specs=[qspec, cspec, cspec, pl.BlockSpec((nsp, ncp), lambda b, i: (0, 0))],
        out_specs=[pl.BlockSpec((1, tq, Q_DIM), lambda b, i: (b, i, 0)),
                   pl.BlockSpec((1, 1, nsp, N_KV_HEADS * tq), lambda b, i: (b, i, 0, 0))],
        out_shape=[jax.ShapeDtypeStruct((bsz, t, Q_DIM), F32), jax.ShapeDtypeStruct((bsz, nq, nsp, N_KV_HEADS * tq), F32)],
        scratch_shapes=[pltpu.VMEM((nsp, N_KV_HEADS * tq), I32)],
        compiler_params=_params("parallel", "parallel"),
        name="nsa_cmp_select",
    )(q, cmp_k, cmp_v, cover_t)
    sel = sel.reshape(bsz, nq, nsp, N_KV_HEADS, tq).transpose(0, 3, 1, 4, 2).reshape(bsz, N_KV_HEADS, t, nsp)
    return o_c, sel


def _kmean_kernel(*refs):
    o_ref = refs[-1]
    tot = jnp.sum(refs[0][0], axis=0, keepdims=True)
    for r in refs[1:-1]:
        tot = tot + jnp.sum(r[0], axis=0, keepdims=True)
    o_ref[0, 0] = tot * (1.0 / MOBA_BLOCK)


def _kmean_prompt(kv):
    bsz, t, _ = kv.shape
    nb = t // MOBA_BLOCK
    return pl.pallas_call(
        _kmean_kernel,
        grid=(bsz, nb),
        in_specs=[pl.BlockSpec((1, MOBA_BLOCK, KV_COLS), lambda b, i: (b, i, 0))],
        out_specs=pl.BlockSpec((1, 1, 1, KV_COLS), lambda b, i: (b, i, 0, 0)),
        out_shape=jax.ShapeDtypeStruct((bsz, nb, 1, KV_COLS), F32),
        compiler_params=_params("parallel", "parallel"),
        name="kmean_prompt",
    )(kv)


def _kmean_sample_kernel(pt_ref, *refs, per, nb):
    o_ref = refs[-1]
    ones = jnp.ones((8, PAGE_SIZE), BF16)
    blocks = len(refs[:-1]) // (per * nb)
    for n in range(nb * blocks):
        tot = jnp.zeros((8, KV_COLS), F32)
        for r in refs[n * per:(n + 1) * per]:
            x = r[0, 0]
            hi = x.astype(BF16)
            r1 = x - hi.astype(F32)
            mid = r1.astype(BF16)
            lo = (r1 - mid.astype(F32)).astype(BF16)
            tot = tot + _dot_t(ones, hi) + _dot_t(ones, mid) + _dot_t(ones, lo)
        o_ref[n // blocks, n % blocks] = tot[0:1] * (1.0 / MOBA_BLOCK)


def _kmean_sample(page_table, pool_t, n_pg):
    bsz, n_pages = page_table.shape
    per = MOBA_BLOCK // PAGE_SIZE
    n_blocks = n_pages // per
    nb = _batch_rows(bsz)

    def page_map(bi, i):
        return lambda b, n, pt: (pt[b * nb + bi, n * n_pg + i], 0, 0, 0)

    grid_spec = pltpu.PrefetchScalarGridSpec(
        num_scalar_prefetch=1,
        grid=(bsz // nb, n_pages // n_pg),
        in_specs=[pl.BlockSpec((1, 1, KV_COLS, PAGE_SIZE), page_map(bi, i)) for bi in range(nb) for i in range(n_pg)],
        out_specs=pl.BlockSpec((nb, n_pg // per, 1, KV_COLS), lambda b, n, pt: (b, n, 0, 0)),
    )
    return pl.pallas_call(
        functools.partial(_kmean_sample_kernel, per=per, nb=nb),
        grid_spec=grid_spec,
        out_shape=jax.ShapeDtypeStruct((bsz, n_blocks, 1, KV_COLS), F32),
        compiler_params=_params("parallel", "parallel"),
        name="kmean_sample",
    )(page_table, *([pool_t] * (nb * n_pg)))


def _moba_select_kernel(q_ref, km_ref, sel_ref, *, tq, pos0, k_top):
    t0 = pos0 + pl.program_id(1) * tq
    nbp = km_ref.shape[2]
    rows = GROUP * tq
    blk = lax.broadcasted_iota(I32, (nbp, rows), 0)
    n_past = (t0 + (lax.broadcasted_iota(I32, (nbp, rows), 1) & (tq - 1))) // MOBA_BLOCK
    for g in range(N_KV_HEADS):
        q = q_ref[0, g].reshape(rows, HEAD_DIM)
        s = jnp.where(blk < n_past, _dot_t(km_ref[0, g], q), -jnp.inf)
        sel = jnp.zeros((nbp, rows), F32)
        for _ in range(k_top):
            m = jnp.max(s, axis=0, keepdims=True)
            first = jnp.min(jnp.where(s == m, blk, nbp), axis=0, keepdims=True)
            pick = blk == first
            sel = jnp.where(pick & (m > -jnp.inf), 1.0, sel)
            s = jnp.where(pick, -jnp.inf, s)
        sel_ref[0, 0, g] = sel


def _moba_select(q, kmean, pos0, nb, tq):
    bsz, _, _, t, _ = q.shape
    nbp = kmean.shape[2]
    nq = t // tq
    qspec = pl.BlockSpec((1, N_KV_HEADS, GROUP, tq, HEAD_DIM), lambda b, i: (b, 0, 0, i, 0))
    sel = pl.pallas_call(
        functools.partial(_moba_select_kernel, tq=tq, pos0=pos0, k_top=min(MOBA_TOPK, nb)),
        grid=(bsz, nq),
        in_specs=[qspec, pl.BlockSpec((1, N_KV_HEADS, nbp, HEAD_DIM), lambda b, i: (b, 0, 0, 0))],
        out_specs=pl.BlockSpec((1, 1, N_KV_HEADS, nbp, GROUP * tq), lambda b, i: (b, i, 0, 0, 0)),
        out_shape=jax.ShapeDtypeStruct((bsz, nq, N_KV_HEADS, nbp, GROUP * tq), F32),
        compiler_params=_params("parallel", "parallel"),
        name="moba_select",
    )(q, kmean)
    sel = sel.reshape(bsz, nq, N_KV_HEADS, nbp, GROUP, tq).transpose(0, 2, 4, 1, 5, 3)
    return sel.reshape(bsz, N_KV_HEADS, GROUP, t, nbp)


def _rope_tables(pos):
    half = HEAD_DIM // 2
    inv = ROPE_THETA ** (-jnp.arange(half, dtype=F32) / half)
    ang = pos.astype(F32)[:, None] * inv[None, :]
    cos, sin = jnp.cos(ang), jnp.sin(ang)
    rep = LANES // HEAD_DIM
    return jnp.concatenate([cos, cos] * rep, axis=1), jnp.concatenate([-sin, sin] * rep, axis=1)


def _pad_to(x, axis, size):
    pad = [(0, 0)] * x.ndim
    pad[axis] = (0, size - x.shape[axis])
    return jnp.pad(x, pad)


def _q_groups(q):
    b, t = q.shape[:2]
    return q.transpose(0, 2, 1, 3).reshape(b, N_KV_HEADS, GROUP, t, HEAD_DIM)


def _pages_t(kv):
    return kv.transpose(0, 2, 3, 4, 1).reshape(kv.shape[0], 2, KV_COLS, PAGE_SIZE)


def _fit(x, width):
    return x[..., :width] if x.shape[-1] >= width else _pad_to(x, x.ndim - 1, width)


_HEAD_TO_GROUP = np.equal(np.arange(N_HEADS)[:, None] // GROUP, np.arange(N_KV_HEADS)[None, :]).astype(np.float32)


def _q_block_diag(q):
    s, tn = q.shape[:2]
    qb = q[:, :, :, None, :] * jnp.asarray(_HEAD_TO_GROUP, dtype=q.dtype)[None, None, :, :, None]
    return qb.reshape(s, tn * N_HEADS, KV_COLS)


def _extract_block_diag(o, tn):
    s = o.shape[0]
    o5 = o.reshape(s, tn, N_HEADS, N_KV_HEADS, HEAD_DIM) * _HEAD_TO_GROUP[None, None, :, :, None]
    return o5.sum(axis=3).reshape(s * tn, Q_DIM)


def _new_page(kv_new):
    return _pages_t(_pad_to(kv_new, 1, PAGE_SIZE))


def _pick_pages(n_pages):
    for n in (8, 4, 2, 1):
        if n_pages % n == 0:
            return n


def _identity_pages(bsz, n_pages):
    return jnp.arange(bsz * n_pages, dtype=I32).reshape(bsz, n_pages)


def _rows_th(ok, s, tn):
    return jnp.where(ok, 0.0, NEG).astype(F32).reshape(s, tn * N_HEADS, ok.shape[-1])


def _kv_stack(k, v, bsz, t):
    return jnp.stack([k.reshape(bsz, t, N_KV_HEADS, HEAD_DIM), v.reshape(bsz, t, N_KV_HEADS, HEAD_DIM)], axis=2)


def _kv_forms(c_k, rope):
    return [(c_k, KV_COLS, rope, 1.0, BF16, "keys_t"), (c_k + KV_COLS, KV_COLS, False, 1.0, BF16, "values_1")]


def _dsa_project(x, bsz, t, w_in, cs, q_scale, prompt):
    hm = "heads" if prompt else "tok"
    c_qi = Q_DIM + 2 * KV_COLS
    c_ki = c_qi + IDX_HEADS * IDX_DIM
    segs = [(0, Q_DIM, True, q_scale, BF16, hm), (Q_DIM, KV_COLS, True, 1.0, F32, "tok"),
            (Q_DIM + KV_COLS, KV_COLS, False, 1.0, F32, "tok"), (c_qi, IDX_HEADS * IDX_DIM, True, 1.0, BF16, hm),
            (c_ki, IDX_DIM, True, 1.0, F32, "tok"), (c_ki + IDX_DIM, IDX_HEADS, False, IDX_SCALE, F32, "tok")]
    if prompt:
        segs += _kv_forms(Q_DIM, True)
    q, k, v, qi, ki, wi, *flash_kv = _proj(x, w_in, cs, segs, bsz)
    if not prompt:
        q, qi = q.reshape(bsz, t, N_HEADS, HEAD_DIM), qi.reshape(bsz, t, IDX_HEADS, IDX_DIM)
    return q, qi, wi.reshape(bsz, t, IDX_HEADS), _kv_stack(k, v, bsz, t), ki.reshape(bsz, t, IDX_DIM), flash_kv


def _head_groups(q):
    return q.reshape(q.shape[0], N_KV_HEADS, GROUP, q.shape[2], HEAD_DIM)


def _dsa_prompt(x, bsz, t, w_in, cs):
    q, qi, wi, kv, ki, (k_t, v1) = _dsa_project(x, bsz, t, w_in, cs, ATTN_SCALE * LOG2E, True)
    mask = _dsa_select_prompt(qi, ki.astype(BF16), wi)
    o = _flash_prompt("dsa", _head_groups(q), k_t, v1, [mask])
    return o.reshape(bsz * t, Q_DIM), kv, ki


def _dsa_sample(x, s, tn, w_in, cs, page_table, cache_kv, cache_kidx, n_pg):
    q, qi, wi, kv, ki, _ = _dsa_project(x, s, tn, w_in, cs, ATTN_SCALE, False)
    tpad = 8
    qi_p = _pad_to(qi, 1, tpad).reshape(s, tpad * IDX_HEADS, IDX_DIM)
    wi_p = _pad_to(wi, 1, tpad).reshape(s, tpad * IDX_HEADS, 1)
    mask = _dsa_select_sample(page_table, qi_p, wi_p, cache_kidx.transpose(0, 2, 1),
                              _pad_to(ki, 1, PAGE_SIZE).transpose(0, 2, 1), tn, n_pg)
    ok = jnp.broadcast_to(mask[:, :tn, None, :] > 0.5, (s, tn, N_HEADS, mask.shape[-1]))
    o = _paged_flash(page_table, _q_block_diag(q), _pages_t(cache_kv), _new_page(kv), _rows_th(ok, s, tn), n_pg)
    return _extract_block_diag(o, tn), kv, ki


def _fox_project(x, bsz, t, w_in, b_f, cs, q_scale, prompt):
    segs = [(0, Q_DIM, False, q_scale, BF16, "heads" if prompt else "tok"), (Q_DIM, KV_COLS, False, 1.0, F32, "tok"),
            (Q_DIM + KV_COLS, KV_COLS, False, 1.0, F32, "tok"), (Q_DIM + 2 * KV_COLS, N_HEADS, False, 1.0, F32, "tok")]
    if prompt:
        segs += _kv_forms(Q_DIM, False)
    q, k, v, f, *flash_kv = _proj(x, w_in, cs, segs, bsz)
    logf = jax.nn.log_sigmoid(f.reshape(bsz, t, N_HEADS) + b_f)
    if not prompt:
        q = q.reshape(bsz, t, N_HEADS, HEAD_DIM)
    return q, _kv_stack(k, v, bsz, t), logf, flash_kv


def _fox_prompt(x, bsz, t, w_in, b_f, cs):
    q, kv, logf, (k_t, v1) = _fox_project(x, bsz, t, w_in, b_f, cs, ATTN_SCALE * LOG2E, True)
    n_pages = t // LANES
    pool_t = logf.reshape(bsz, n_pages, LANES, N_HEADS).transpose(0, 1, 3, 2).reshape(bsz * n_pages, N_HEADS, LANES)
    c = _paged_cumsum(_identity_pages(bsz, n_pages), pool_t, jnp.zeros((bsz, N_HEADS, LANES), F32), _pick_pages(n_pages))
    c = (c[:, :, :t] * LOG2E).reshape(bsz, N_KV_HEADS, GROUP, t)
    o = _flash_prompt("fox", _head_groups(q), k_t, v1, [c])
    return o.reshape(bsz * t, Q_DIM), kv, logf


def _fox_sample(x, s, tn, w_in, b_f, cs, page_table, cache_kv, cache_logf, n_pg):
    q, kv, logf, _ = _fox_project(x, s, tn, w_in, b_f, cs, ATTN_SCALE, False)
    past = page_table.shape[1] * PAGE_SIZE
    c = _paged_cumsum(page_table, cache_logf.transpose(0, 2, 1), _pad_to(logf.transpose(0, 2, 1), 2, LANES), n_pg)
    col = jnp.arange(c.shape[-1])
    valid = (col[None, :] < past) | ((col[None, :] - past <= jnp.arange(tn)[:, None]) & (col[None, :] < past + tn))
    bias = jnp.where(valid[None, :, None, :], -c[:, None, :, :], NEG).reshape(s, tn * N_HEADS, c.shape[-1])
    o = _paged_flash(page_table, _q_block_diag(q), _pages_t(cache_kv), _new_page(kv), bias, n_pg)
    return _extract_block_diag(o, tn), kv, logf


def _nsa_project(x, bsz, t, w_in, b_gate, cs, q_scale, prompt):
    hm = "heads" if prompt else "tok"
    c = Q_DIM
    segs = [(0, Q_DIM, False, ATTN_SCALE, BF16, hm), (0, Q_DIM, True, q_scale, BF16, hm)]
    for rope in (False, False, True, False, True, False):
        segs.append((c, KV_COLS, rope, 1.0, F32, "tok"))
        c += KV_COLS
    segs.append((c, 3 * N_HEADS, False, 1.0, F32, "tok"))
    if prompt:
        segs += _kv_forms(Q_DIM + 2 * KV_COLS, True) + _kv_forms(Q_DIM + 4 * KV_COLS, True)
    q, q_rot, kc, vc, ks, vs, kw, vw, g, *flash_kv = _proj(x, w_in, cs, segs, bsz)
    gate = jax.nn.sigmoid(g.reshape(bsz, t, 3 * N_HEADS) + b_gate).reshape(bsz, t, 3, N_HEADS)
    if not prompt:
        q, q_rot = q.reshape(bsz, t, N_HEADS, HEAD_DIM), q_rot.reshape(bsz, t, N_HEADS, HEAD_DIM)
    return (q, q_rot, gate, _kv_stack(kc, vc, bsz, t), _kv_stack(ks, vs, bsz, t), _kv_stack(kw, vw, bsz, t), flash_kv)


def _nsa_weights(pe, w1, w2):
    r = CMP_LEN // CMP_STRIDE
    w1cat = w1.reshape(2, r, CMP_STRIDE * HEAD_DIM, CMP_HID).transpose(0, 2, 1, 3).reshape(2, CMP_STRIDE * HEAD_DIM, r * CMP_HID)
    pe8 = jnp.broadcast_to(pe.reshape(2, 1, CMP_LEN * HEAD_DIM), (2, 8, CMP_LEN * HEAD_DIM))
    return pe8.astype(BF16), w1cat.astype(BF16), w2.astype(BF16)


def _chunk_pages(kv):
    n = kv.shape[0]
    per = PAGE_SIZE // CMP_STRIDE
    x = kv.reshape(n, per, CMP_STRIDE, 2 * N_KV_HEADS, HEAD_DIM).transpose(0, 3, 1, 2, 4)
    return x.reshape(n, 2 * N_KV_HEADS, per, CMP_STRIDE * HEAD_DIM).astype(BF16)


def _gate_mix(gate, o_c, o_s, o_w):
    b, t = gate.shape[:2]
    g = jnp.repeat(gate, HEAD_DIM, axis=-1)
    return (g[:, :, 0] * o_c + g[:, :, 1] * o_s + g[:, :, 2] * o_w).reshape(b * t, Q_DIM)


def _nsa_prompt(x, bsz, t, w_in, b_gate, cmp_w, cs, tq=128, tk=512):
    q, q_rot, gate, kv_cmp, kv_slc, kv_win, (ks_t, vs1, kw_t, vw1) = _nsa_project(x, bsz, t, w_in, b_gate, cs,
                                                                                  ATTN_SCALE * LOG2E, True)
    n_pages = t // PAGE_SIZE
    cmp, nc = _nsa_compress(_identity_pages(bsz, n_pages), _chunk_pages(kv_cmp.reshape(bsz * n_pages, PAGE_SIZE, 2, N_KV_HEADS, HEAD_DIM)),
                            *cmp_w, n_pg=_pick_pages(n_pages))
    cmp = cmp.astype(BF16)
    tq, tk = min(tq, t), min(tk, t)
    o_c, selblk = _nsa_cmp_select(_head_groups(q), cmp[:, :N_KV_HEADS], cmp[:, N_KV_HEADS:], 0, nc, t, tq)
    qg = _head_groups(q_rot)
    e3 = _expand_matrix(selblk.shape[-1], SEL_BLOCK, t, tk)
    o_s = _flash_prompt("nsa_sel", qg, ks_t, vs1, [selblk, e3], tq=2 * tq, tk=tk)
    o_w = _flash_prompt("nsa_win", qg, kw_t, vw1, [], tk=tk)
    return _gate_mix(gate, o_c, o_s, o_w), kv_cmp, kv_slc, kv_win[:, -min(WINDOW, t):]


def _nsa_sample(x, s, tn, w_in, b_gate, cmp_w, cs, page_table, cache_cmp, cache_slc, state_win, n_pg):
    q, q_rot, gate, kv_cmp, kv_slc, kv_win, _ = _nsa_project(x, s, tn, w_in, b_gate, cs, ATTN_SCALE, False)
    past = page_table.shape[1] * PAGE_SIZE
    tpad = 8
    cmp, nc = _nsa_compress(page_table, _chunk_pages(cache_cmp), *cmp_w, n_pg=n_pg)
    cmp = cmp.astype(BF16)
    o_c, selblk = _nsa_cmp_select(_q_groups(_pad_to(q, 1, tpad)), cmp[:, :N_KV_HEADS], cmp[:, N_KV_HEADS:],
                                  past, nc, past + tn, tpad)
    o_c = o_c[:, :tn]
    qbd = _q_block_diag(q_rot)
    width = (page_table.shape[1] // n_pg + 1) * n_pg * LANES
    col = jnp.arange(width)
    pos = past + jnp.arange(tn)
    sel_key = jnp.repeat(selblk[:, :, :tn, :-(-width // SEL_BLOCK)], SEL_BLOCK, axis=-1)[..., :width]
    ok = (sel_key > 0.5) & (col[None, :] <= pos[:, None])[None, None]
    ok = jnp.broadcast_to(ok.transpose(0, 2, 1, 3)[:, :, :, None, :], (s, tn, N_KV_HEADS, GROUP, width))
    o_s = _paged_flash(page_table, qbd, _pages_t(cache_slc), _new_page(kv_slc),
                       _rows_th(ok.reshape(s, tn, N_HEADS, width), s, tn), n_pg)
    win_buf = state_win.shape[1]
    n_wp = win_buf // PAGE_SIZE
    wcol = jnp.arange(2 * n_wp * LANES)
    win_pos = jnp.where(wcol < win_buf, past - win_buf + wcol, jnp.where(wcol < win_buf + tn, past + wcol - win_buf, -1))
    w_ok = (win_pos[None, :] <= pos[:, None]) & (win_pos[None, :] >= pos[:, None] - WINDOW) & (win_pos[None, :] >= 0)
    w_ok = jnp.broadcast_to(w_ok[None, :, None, :], (s, tn, N_HEADS, wcol.shape[0]))
    win_pages = _pages_t(state_win.reshape(s * n_wp, PAGE_SIZE, 2, N_KV_HEADS, HEAD_DIM))
    o_w = _paged_flash(_identity_pages(s, n_wp), qbd, win_pages, _new_page(kv_win), _rows_th(w_ok, s, tn), n_wp)
    unbd = lambda o: _extract_block_diag(o, tn).reshape(s, tn, Q_DIM)
    win = jnp.concatenate([state_win, kv_win], axis=1)[:, -win_buf:]
    return _gate_mix(gate, o_c, unbd(o_s), unbd(o_w)), kv_cmp, kv_slc, win


def _moba_project(x, bsz, t, w_in, cs, q_scale, prompt):
    segs = [(0, Q_DIM, True, q_scale, BF16, "heads" if prompt else "tok"), (Q_DIM, KV_COLS, True, 1.0, F32, "tok"),
            (Q_DIM + KV_COLS, KV_COLS, False, 1.0, F32, "tok")]
    if prompt:
        segs += _kv_forms(Q_DIM, True)
    q, k, v, *flash_kv = _proj(x, w_in, cs, segs, bsz)
    if not prompt:
        q = q.reshape(bsz, t, N_HEADS, HEAD_DIM)
    return q, _kv_stack(k, v, bsz, t), flash_kv


def _kmean_heads(km):
    b, nb = km.shape[:2]
    return _pad_to(km.reshape(b, nb, N_KV_HEADS, HEAD_DIM).transpose(0, 2, 1, 3), 2, -(-nb // 16) * 16).astype(BF16)


def _moba_prompt(x, bsz, t, w_in, cs, tq=128, tk=512):
    q, kv, (k_t, v1) = _moba_project(x, bsz, t, w_in, cs, ATTN_SCALE * LOG2E, True)
    tq, tk = min(tq, t), min(tk, t)
    nb = -(-t // MOBA_BLOCK)
    km = _kmean_heads(_kmean_prompt(kv.reshape(bsz, t, 2 * KV_COLS)))
    qg = _head_groups(q)
    sel = _moba_select(qg, km, 0, nb, tq)
    e3 = _expand_matrix(km.shape[2], MOBA_BLOCK, t, tk)
    o = _flash_prompt("moba", qg, k_t, v1, [sel, e3], tq=2 * tq, tk=tk)
    return o.reshape(bsz * t, Q_DIM), kv


def _moba_sample(x, s, tn, w_in, cs, page_table, cache_kv, n_pg):
    q, kv, _ = _moba_project(x, s, tn, w_in, cs, ATTN_SCALE, False)
    past = page_table.shape[1] * PAGE_SIZE
    tpad = 8
    pool = _pages_t(cache_kv)
    nb = -(-(past + tn) // MOBA_BLOCK)
    km = _kmean_heads(_kmean_sample(page_table, pool, n_pg))
    sel = _moba_select(_q_groups(_pad_to(q, 1, tpad)), km, past, nb, tpad)
    width = (page_table.shape[1] // n_pg + 1) * n_pg * LANES
    col = jnp.arange(width)
    pos = past + jnp.arange(tn)
    sel_key = _fit(jnp.repeat(sel[:, :, :, :tn], MOBA_BLOCK, axis=-1), width)
    own = (col[None, :] // MOBA_BLOCK == pos[:, None] // MOBA_BLOCK) & (col[None, :] <= pos[:, None])
    ok = (sel_key > 0.5) | own[None, None, None]
    ok = ok.transpose(0, 3, 1, 2, 4).reshape(s, tn, N_HEADS, width)
    o = _paged_flash(page_table, _q_block_diag(q), pool, _new_page(kv), _rows_th(ok, s, tn), n_pg)
    return _extract_block_diag(o, tn), kv


def _cast_w(w):
    return _pad_to(w, 1, -(-w.shape[1] // LANES) * LANES).astype(BF16)


def kernel(x_prompt, x_sample, cache_a_kv, cache_a_kidx, cache_b_kv, cache_b_logf, cache_c_cmp_kv, cache_c_slc_kv, state_c_win_kv, cache_d_kv, page_table, a_w_in, a_w_out, b_w_in, b_b_f, b_w_out, c_w_in, c_b_gate, c_cmp_pe, c_cmp_w1, c_cmp_w2, c_w_out, d_w_in, d_w_out, ln_g, ln_b, ffn_w_gu, ffn_w_down):
    bsz, t, d = x_prompt.shape
    s, tn, _ = x_sample.shape
    n_pages = page_table.shape[1]
    past = n_pages * PAGE_SIZE
    n_pg = _pick_pages(n_pages)
    cs_p = _rope_tables(jnp.arange(t, dtype=I32))
    cs_s = _rope_tables(jnp.tile(past + jnp.arange(tn, dtype=I32), s))
    xp = x_prompt.reshape(bsz * t, d)
    xs = x_sample.reshape(s * tn, d)
    cmp_w = _nsa_weights(c_cmp_pe, c_cmp_w1, c_cmp_w2)
    w_out = [_cast_w(w) for w in (a_w_out, b_w_out, c_w_out, d_w_out)]

    op, a_kv_p, a_kidx_p = _dsa_prompt(xp, bsz, t, _cast_w(a_w_in), cs_p)
    os_, a_kv_s, a_kidx_s = _dsa_sample(xs, s, tn, _cast_w(a_w_in), cs_s, page_table, cache_a_kv, cache_a_kidx, n_pg)

    def finish(i, xp, xs, op, os_):
        xp = _out_ln(op, w_out[i], xp, ln_g[i, 0], ln_b[i, 0])
        xs = _out_ln(os_, w_out[i], xs, ln_g[i, 0], ln_b[i, 0])
        wgu, wd = ffn_w_gu[i].astype(BF16), ffn_w_down[i].astype(BF16)
        xp = _ffn_ln(xp, wgu, wd, ln_g[i, 1], ln_b[i, 1])
        xs = _ffn_ln(xs, wgu, wd, ln_g[i, 1], ln_b[i, 1])
        return xp, xs

    xp, xs = finish(0, xp, xs, op, os_)

    op, b_kv_p, b_logf_p = _fox_prompt(xp, bsz, t, _cast_w(b_w_in), b_b_f, cs_p)
    os_, b_kv_s, b_logf_s = _fox_sample(xs, s, tn, _cast_w(b_w_in), b_b_f, cs_s, page_table, cache_b_kv, cache_b_logf, n_pg)
    xp, xs = finish(1, xp, xs, op, os_)

    op, c_cmp_kv_p, c_slc_kv_p, c_win_kv_p = _nsa_prompt(xp, bsz, t, _cast_w(c_w_in), c_b_gate, cmp_w, cs_p)
    os_, c_cmp_kv_s, c_slc_kv_s, c_win_kv_s = _nsa_sample(xs, s, tn, _cast_w(c_w_in), c_b_gate, cmp_w, cs_s, page_table,
                                                          cache_c_cmp_kv, cache_c_slc_kv, state_c_win_kv, n_pg)
    xp, xs = finish(2, xp, xs, op, os_)

    op, d_kv_p = _moba_prompt(xp, bsz, t, _cast_w(d_w_in), cs_p)
    os_, d_kv_s = _moba_sample(xs, s, tn, _cast_w(d_w_in), cs_s, page_table, cache_d_kv, n_pg)
    xp, xs = finish(3, xp, xs, op, os_)

    return (xp.reshape(bsz, t, d), xs.reshape(s, tn, d), a_kv_p, a_kv_s, a_kidx_p, a_kidx_s, b_kv_p, b_kv_s,
            b_logf_p, b_logf_s, c_cmp_kv_p, c_cmp_kv_s, c_slc_kv_p, c_slc_kv_s, c_win_kv_p, c_win_kv_s, d_kv_p, d_kv_s)
```

```python
import functools

import numpy as np
import jax
import jax.numpy as jnp
from jax import lax
from jax.experimental import pallas as pl
from jax.experimental.pallas import tpu as pltpu

F32 = jnp.float32
BF16 = jnp.bfloat16
I32 = jnp.int32

N_HEADS = 16
HEAD_DIM = 64
N_KV_HEADS = 4
GROUP = N_HEADS // N_KV_HEADS
Q_DIM = N_HEADS * HEAD_DIM
KV_COLS = N_KV_HEADS * HEAD_DIM
DEPTH = 4
PAGE_SIZE = 128
ROPE_THETA = 10000.0
LN_EPS = 1e-5
ALPHA = (2 * DEPTH) ** 0.25
ATTN_SCALE = HEAD_DIM ** -0.5
LOG2E = 1.4426950408889634
IDX_HEADS = 8
IDX_DIM = 64
IDX_TOPK = 256
IDX_SCALE = (IDX_HEADS * IDX_DIM) ** -0.5
CMP_LEN = 32
CMP_STRIDE = 16
CMP_HID = 2 * HEAD_DIM
SEL_BLOCK = 64
SEL_TOPN = 16
WINDOW = 512
MOBA_BLOCK = 256
MOBA_TOPK = 3

LANES = 128
VMEM_LIMIT = 56 * 2 ** 20
NEG = -1e30
KEY_NEG_INF = -2139095041
KEY_POS_INF = 2139095040
INT_MIN = -2 ** 31


def _params(*sem):
    return pltpu.CompilerParams(dimension_semantics=sem, vmem_limit_bytes=VMEM_LIMIT)


def _dot_t(a, b):
    return lax.dot_general(a, b, (((1,), (1,)), ((), ())), preferred_element_type=F32)


def _dot(a, b):
    return jnp.dot(a, b, preferred_element_type=F32)


def _dot_hp(a, b):
    hi = a.astype(BF16)
    r1 = a - hi.astype(F32)
    mid = r1.astype(BF16)
    lo = (r1 - mid.astype(F32)).astype(BF16)
    return _dot(hi, b) + _dot(mid, b) + _dot(lo, b)


def _proj_kernel(x_ref, w_ref, cos_ref, sin_ref, *out_refs, segs):
    acc = _dot(x_ref[...].astype(BF16), w_ref[...])
    tm = acc.shape[0]
    lane = lax.broadcasted_iota(I32, (tm, LANES), 1)
    first_half = (lane & (HEAD_DIM - 1)) < HEAD_DIM // 2
    ones_col = jnp.where(lane == HEAD_DIM, 1.0, 0.0)
    for (c0, width, rope, scale, form), o_ref in zip(segs, out_refs):
        if c0 % LANES:
            o_ref[...] = (acc[:, c0:c0 + width] * scale).astype(o_ref.dtype)
            continue
        for j in range(-(-width // LANES)):
            x = acc[:, c0 + j * LANES:c0 + (j + 1) * LANES]
            if rope:
                swapped = jnp.where(first_half, pltpu.roll(x, LANES - HEAD_DIM // 2, 1), pltpu.roll(x, HEAD_DIM // 2, 1))
                x = x * cos_ref[...] + swapped * sin_ref[...]
            if scale != 1.0:
                x = x * scale
            if form == "tok":
                wj = min(LANES, width - j * LANES)
                o_ref[:, j * LANES:j * LANES + wj] = x[:, :wj].astype(o_ref.dtype)
            elif form == "heads":
                o_ref[0, 2 * j] = x[:, :HEAD_DIM].astype(o_ref.dtype)
                o_ref[0, 2 * j + 1] = x[:, HEAD_DIM:].astype(o_ref.dtype)
            elif form == "keys_t":
                xt = x.T
                o_ref[0, 2 * j] = xt[:HEAD_DIM].astype(o_ref.dtype)
                o_ref[0, 2 * j + 1] = xt[HEAD_DIM:].astype(o_ref.dtype)
            else:
                o_ref[0, 2 * j] = jnp.where(lane < HEAD_DIM, x, ones_col).astype(o_ref.dtype)
                o_ref[0, 2 * j + 1] = jnp.where(lane < HEAD_DIM, pltpu.roll(x, HEAD_DIM, 1), ones_col).astype(o_ref.dtype)


def _proj(x, w, cs, segs, bsz=None):
    m, k = x.shape
    n = w.shape[1]
    tm = min(m, 512)
    cos_t, sin_t = cs
    r_blocks = cos_t.shape[0] // tm
    tab = pl.BlockSpec((tm, LANES), lambda i: (i % r_blocks, 0))
    specs, shapes = [], []
    for _, width, _, _, dt, form in segs:
        nh = width // HEAD_DIM
        if form == "tok":
            specs.append(pl.BlockSpec((tm, width), lambda i: (i, 0)))
            shapes.append(jax.ShapeDtypeStruct((m, width), dt))
            continue
        t = m // bsz
        tpb = t // tm
        if form == "keys_t":
            specs.append(pl.BlockSpec((1, nh, HEAD_DIM, tm), lambda i: (i // tpb, 0, 0, i % tpb)))
            shapes.append(jax.ShapeDtypeStruct((bsz, nh, HEAD_DIM, t), dt))
        else:
            last = HEAD_DIM if form == "heads" else LANES
            specs.append(pl.BlockSpec((1, nh, tm, last), lambda i: (i // tpb, 0, i % tpb, 0)))
            shapes.append(jax.ShapeDtypeStruct((bsz, nh, t, last), dt))
    return pl.pallas_call(
        functools.partial(_proj_kernel, segs=tuple(s[:4] + (s[5],) for s in segs)),
        grid=(m // tm,),
        in_specs=[pl.BlockSpec((tm, k), lambda i: (i, 0)), pl.BlockSpec((k, n), lambda i: (0, 0)), tab, tab],
        out_specs=specs,
        out_shape=shapes,
        compiler_params=_params("parallel"),
        name="in_proj",
    )(x, w, cos_t, sin_t)


def _layer_norm(y, g, b):
    mu = jnp.mean(y, axis=-1, keepdims=True)
    d = y - mu
    var = jnp.mean(d * d, axis=-1, keepdims=True)
    return d * lax.rsqrt(var + LN_EPS) * g + b


def _out_ln_kernel(o_ref, w_ref, x_ref, g_ref, b_ref, y_ref):
    y = ALPHA * x_ref[...] + _dot(o_ref[...].astype(BF16), w_ref[...])
    y_ref[...] = _layer_norm(y, g_ref[...], b_ref[...])


def _out_ln_gated_kernel(gate_ref, oc_ref, os_ref, ow_ref, w_ref, x_ref, g_ref, b_ref, y_ref):
    gate = gate_ref[...]
    heads = []
    for h in range(N_HEADS):
        cols = slice(h * HEAD_DIM, (h + 1) * HEAD_DIM)
        heads.append(gate[:, h:h + 1] * oc_ref[:, cols] + gate[:, N_HEADS + h:N_HEADS + h + 1] * os_ref[:, cols]
                     + gate[:, 2 * N_HEADS + h:2 * N_HEADS + h + 1] * ow_ref[:, cols])
    o = jnp.concatenate(heads, axis=1)
    y = ALPHA * x_ref[...] + _dot(o.astype(BF16), w_ref[...])
    y_ref[...] = _layer_norm(y, g_ref[...], b_ref[...])


def _out_ln(o, w, x, g, b):
    m, d = x.shape
    tm = min(m, 512)
    row = lambda i: (i, 0)
    fix = lambda i: (0, 0)
    gated = isinstance(o, tuple)
    o_in = list(o) if gated else [o]
    return pl.pallas_call(
        _out_ln_gated_kernel if gated else _out_ln_kernel,
        grid=(m // tm,),
        in_specs=[pl.BlockSpec((tm, a.shape[1]), row) for a in o_in]
        + [pl.BlockSpec(w.shape, fix), pl.BlockSpec((tm, d), row), pl.BlockSpec((1, d), fix), pl.BlockSpec((1, d), fix)],
        out_specs=pl.BlockSpec((tm, d), row),
        out_shape=jax.ShapeDtypeStruct((m, d), F32),
        compiler_params=_params("parallel"),
        name="out_proj_ln",
    )(*o_in, w, x, g.reshape(1, d), b.reshape(1, d))


def _ffn_ln_kernel(x_ref, wgu_ref, wd_ref, g_ref, b_ref, y_ref, *, d_ff, chunk):
    x = x_ref[...]
    xb = x.astype(BF16)
    acc = jnp.zeros(x.shape, F32)
    for c in range(d_ff // chunk):
        gate = _dot(xb, wgu_ref[:, c * chunk:(c + 1) * chunk])
        up = _dot(xb, wgu_ref[:, d_ff + c * chunk:d_ff + (c + 1) * chunk])
        h = gate * (1.0 / (1.0 + jnp.exp(-gate))) * up
        acc = acc + _dot(h.astype(BF16), wd_ref[c * chunk:(c + 1) * chunk, :])
    y_ref[...] = _layer_norm(ALPHA * x + acc, g_ref[...], b_ref[...])


def _ffn_ln(x, wgu, wd, g, b):
    m, d = x.shape
    d_ff = wd.shape[0]
    tm = min(m, 256)
    row = lambda i: (i, 0)
    fix = lambda i: (0, 0)
    return pl.pallas_call(
        functools.partial(_ffn_ln_kernel, d_ff=d_ff, chunk=256),
        grid=(m // tm,),
        in_specs=[pl.BlockSpec((tm, d), row), pl.BlockSpec((d, 2 * d_ff), fix), pl.BlockSpec((d_ff, d), fix),
                  pl.BlockSpec((1, d), fix), pl.BlockSpec((1, d), fix)],
        out_specs=pl.BlockSpec((tm, d), row),
        out_shape=jax.ShapeDtypeStruct((m, d), F32),
        compiler_params=_params("parallel"),
        name="ffn_ln",
    )(x, wgu, wd, g.reshape(1, d), b.reshape(1, d))


def _sortable(x):
    x = jnp.where(x == 0.0, 0.0, x)
    b = lax.bitcast_convert_type(x, I32)
    return b ^ ((b >> 31) & I32(0x7FFFFFFF))


def _kth_largest_key(u_ref, k):
    rows = u_ref.shape[0]

    def count_ge(cand):
        return jnp.sum((u_ref[...] >= cand).astype(I32), axis=1, keepdims=True)

    base = jnp.where(count_ge(jnp.zeros((rows, 1), I32)) >= k, I32(0), I32(INT_MIN))

    def body(i, base):
        cand = base | jnp.left_shift(I32(1), 30 - i)
        return jnp.where(count_ge(cand) >= k, cand, base)

    return lax.fori_loop(0, 31, body, base)


def _emit_selection(u_ref, k, write):
    rows, n = u_ref.shape
    thr = _kth_largest_key(u_ref, k)
    n_gt = jnp.sum((u_ref[...] > thr).astype(I32), axis=1, keepdims=True)
    need = (k - n_gt).astype(F32)
    r_i = lax.broadcasted_iota(I32, (LANES, LANES), 0)
    c_i = lax.broadcasted_iota(I32, (LANES, LANES), 1)
    tri = jnp.where(r_i <= c_i, 1.0, 0.0).astype(BF16)
    carry = jnp.zeros((rows, 1), F32)
    for ci in range(n // LANES):
        u = u_ref[:, ci * LANES:(ci + 1) * LANES]
        tie = u == thr
        tie_f = jnp.where(tie, 1.0, 0.0)
        inc = _dot(tie_f.astype(BF16), tri)
        rank = carry + inc - tie_f
        sel = (u > thr) | (tie & (rank < need))
        write(ci, sel, u)
        carry = carry + inc[:, LANES - 1:LANES]


def _topk_mask_cols(u_ref, k):
    n, r = u_ref.shape

    def count(pred):
        return jnp.sum(pred(u_ref[...]).astype(I32), axis=0, keepdims=True)

    base = jnp.where(count(lambda u: u >= 0) >= k, I32(0), I32(INT_MIN))

    def radix(i, base):
        cand = base | jnp.left_shift(I32(1), 30 - i)
        return jnp.where(count(lambda u: u >= cand) >= k, cand, base)

    thr = lax.fori_loop(0, 31, radix, base)
    need = (k - count(lambda u: u > thr)).astype(F32)
    u = u_ref[...]
    tie = u == thr
    lower = jnp.where(lax.broadcasted_iota(I32, (n, n), 0) > lax.broadcasted_iota(I32, (n, n), 1), 1.0, 0.0)
    rank = _dot(lower.astype(BF16), jnp.where(tie, 1.0, 0.0).astype(BF16))
    return (u > thr) | (tie & (rank < need)), u


def _flash_prompt_kernel(*refs, kind, tq, tk, n_extra):
    q_ref, k_ref, v_ref = refs[:3]
    extra = refs[3:3 + n_extra]
    o_ref, m_ref, acc_ref, s_ref = refs[3 + n_extra:]
    g = pl.program_id(1)
    q0 = pl.program_id(2) * tq
    rows = GROUP * tq
    q = q_ref[0, 0].reshape(rows, HEAD_DIM)
    m_ref[...] = jnp.full(m_ref.shape, NEG, F32)
    acc_ref[...] = jnp.zeros(acc_ref.shape, F32)
    c_diag = q0 // tk
    c_lo = jnp.maximum(q0 - WINDOW, 0) // tk if kind == "nsa_win" else 0
    if kind == "nsa_sel":
        selb = extra[0][0, 0].astype(BF16)
    if kind == "moba":
        selb = extra[0][0, 0].reshape(rows, extra[0].shape[-1]).astype(BF16)

    def scores(c):
        return _dot(q, k_ref[0, g, :, pl.ds(pl.multiple_of(c * tk, tk), tk)])

    def chunk(c, diag):
        if kind in ("nsa_sel", "moba"):
            hit = _dot(selb, extra[1][c]) > 0.5
        s3 = s_ref[c % 2].reshape(GROUP, tq, tk)
        if not diag:
            s_ref[(c + 1) % 2] = scores(c + 1)
        start = pl.multiple_of(c * tk, tk)
        v = v_ref[0, g, pl.ds(start, tk), :]
        ok = None
        if diag or kind == "nsa_win":
            t_idx = q0 + lax.broadcasted_iota(I32, (GROUP, tq, tk), 1)
            s_idx = start + lax.broadcasted_iota(I32, (GROUP, tq, tk), 2)
            causal = s_idx <= t_idx
        if kind == "fox":
            ck = extra[0][0, 0, :, pl.ds(start, tk)]
            s3 = s3 - ck[:, None, :]
            ok = causal if diag else None
        elif kind == "dsa":
            msk = extra[0][0, :, pl.ds(start, tk)]
            ok = jnp.broadcast_to((msk > 0)[None], (GROUP, tq, tk))
        elif kind == "nsa_sel":
            ok = jnp.broadcast_to(hit[None], (GROUP, tq, tk))
            ok = (ok & causal) if diag else ok
        elif kind == "nsa_win":
            ok = causal & (s_idx >= t_idx - WINDOW)
        else:
            ok = hit.reshape(GROUP, tq, tk)
            if diag:
                ok = ok | (causal & ((s_idx // MOBA_BLOCK) == (t_idx // MOBA_BLOCK)))
        if ok is not None:
            s3 = jnp.where(ok, s3, NEG)
        s = s3.reshape(rows, tk)
        m_prev = m_ref[...]
        m_new = jnp.maximum(m_prev, jnp.max(s, axis=1, keepdims=True))
        alpha = jnp.exp2(m_prev - m_new)
        p = jnp.exp2(s - pltpu.repeat(m_new, tk // LANES, axis=1))
        acc_ref[...] = acc_ref[...] * alpha + _dot(p.astype(BF16), v)
        m_ref[...] = m_new

    def body(c, carry):
        chunk(c, False)
        return carry

    s_ref[c_lo % 2] = scores(c_lo)
    lax.fori_loop(c_lo, c_diag, body, 0)
    chunk(c_diag, True)
    acc = acc_ref[...]
    o = acc[:, :HEAD_DIM] / jnp.maximum(acc[:, HEAD_DIM:HEAD_DIM + 1], 1e-30)
    o_ref[0] = jnp.concatenate([o[j * tq:(j + 1) * tq] for j in range(GROUP)], axis=1)


def _flash_prompt(kind, q, k, v, extra, tq=128, tk=512):
    bsz, _, _, t, _ = q.shape
    tk = min(tk, t)
    tq = min(tq, t)
    qspec = pl.BlockSpec((1, 1, GROUP, tq, HEAD_DIM), lambda b, g, i: (b, g, 0, i, 0))
    kvspec = pl.BlockSpec((1, N_KV_HEADS, HEAD_DIM, t), lambda b, g, i: (b, 0, 0, 0))
    vspec = pl.BlockSpec((1, N_KV_HEADS, t, LANES), lambda b, g, i: (b, 0, 0, 0))
    if kind == "fox":
        especs = [pl.BlockSpec((1, 1, GROUP, t), lambda b, g, i: (b, g, 0, 0))]
    elif kind == "dsa":
        especs = [pl.BlockSpec((1, tq, t), lambda b, g, i: (b, i, 0))]
    elif kind == "nsa_sel":
        especs = [pl.BlockSpec((1, 1, tq, extra[0].shape[-1]), lambda b, g, i: (b, g, i, 0)),
                  pl.BlockSpec(extra[1].shape, lambda b, g, i: (0, 0, 0))]
    elif kind == "moba":
        especs = [pl.BlockSpec((1, 1, GROUP, tq, extra[0].shape[-1]), lambda b, g, i: (b, g, 0, i, 0)),
                  pl.BlockSpec(extra[1].shape, lambda b, g, i: (0, 0, 0))]
    else:
        especs = []
    rows = GROUP * tq
    return pl.pallas_call(
        functools.partial(_flash_prompt_kernel, kind=kind, tq=tq, tk=tk, n_extra=len(extra)),
        grid=(bsz, N_KV_HEADS, t // tq),
        in_specs=[qspec, kvspec, vspec] + especs,
        out_specs=pl.BlockSpec((1, tq, GROUP * HEAD_DIM), lambda b, g, i: (b, i, g)),
        out_shape=jax.ShapeDtypeStruct((bsz, t, Q_DIM), F32),
        scratch_shapes=[pltpu.VMEM((rows, LANES), F32), pltpu.VMEM((rows, LANES), F32),
                        pltpu.VMEM((2, rows, tk), F32)],
        compiler_params=_params("parallel", "parallel", "parallel"),
        name="flash_prompt_" + kind,
    )(q, k, v, *extra)


def _expand_matrix(n_blocks_padded, block, t, tk):
    s = np.arange(t)
    e = (s[None, :] // block == np.arange(n_blocks_padded)[:, None]).astype(np.float32)
    e = e.reshape(n_blocks_padded, t // tk, tk).transpose(1, 0, 2)
    return jnp.asarray(e, dtype=BF16)


def _dsa_select_prompt_kernel(qi_ref, ki_ref, wi_ref, mask_ref, u_ref, *, tq, t, tk, topk):
    q0 = pl.program_id(1) * tq
    n_act = q0 // tk + 1
    sub = tk // LANES
    w = wi_ref[0]
    t_idx = q0 + lax.broadcasted_iota(I32, (tq, tk), 0)
    k_off = lax.broadcasted_iota(I32, (tq, tk), 1)

    def chunk_at(c):
        return pl.ds(pl.multiple_of(c * tk, tk), tk)

    def fill(c, carry):
        kc = ki_ref[0, chunk_at(c), :]
        s = jnp.zeros((tq, tk), F32)
        for h in range(IDX_HEADS):
            s = s + jnp.maximum(_dot_t(qi_ref[0, h], kc), 0.0) * w[:, h:h + 1]
        s = jnp.where(c * tk + k_off <= t_idx, s, -jnp.inf)
        u_ref[:, chunk_at(c)] = _sortable(s)
        return carry

    lax.fori_loop(0, n_act, fill, 0)

    def count(pred):
        def body(c, acc):
            hit = pred(u_ref[:, chunk_at(c)]).astype(I32)
            for i in range(sub):
                acc = acc + hit[:, i * LANES:(i + 1) * LANES]
            return acc
        acc = lax.fori_loop(0, n_act, body, jnp.zeros((tq, LANES), I32))
        return jnp.sum(acc, axis=1, keepdims=True)

    base = jnp.where(count(lambda u: u >= 0) >= topk, I32(0), I32(INT_MIN))

    def radix(i, base):
        cand = base | jnp.left_shift(I32(1), 30 - i)
        return jnp.where(count(lambda u: u >= cand) >= topk, cand, base)

    thr = lax.fori_loop(0, 31, radix, base)
    need = (topk - count(lambda u: u > thr)).astype(F32)
    r_i = lax.broadcasted_iota(I32, (LANES, LANES), 0)
    c_i = lax.broadcasted_iota(I32, (LANES, LANES), 1)
    tri = jnp.where(r_i <= c_i, 1.0, 0.0).astype(BF16)

    def emit(c, carry):
        for i in range(sub):
            at = pl.ds(pl.multiple_of(c * tk + i * LANES, LANES), LANES)
            u = u_ref[:, at]
            tie = u == thr
            tie_f = jnp.where(tie, 1.0, 0.0)
            inc = _dot(tie_f.astype(BF16), tri)
            sel = (u > thr) | (tie & (carry + inc - tie_f < need))
            keep = sel & (u > KEY_NEG_INF) & (u < KEY_POS_INF)
            mask_ref[0, :, at] = jnp.where(keep, 1.0, 0.0).astype(BF16)
            carry = carry + inc[:, LANES - 1:LANES]
        return carry

    lax.fori_loop(0, n_act, emit, jnp.zeros((tq, 1), F32))

    def clear(c, carry):
        mask_ref[0, :, chunk_at(c)] = jnp.zeros((tq, tk), BF16)
        return carry

    lax.fori_loop(n_act, t // tk, clear, 0)


def _dsa_select_prompt(qi, ki, wi, tq=256, tk=512):
    bsz, _, t, _ = qi.shape
    tq, tk = min(tq, t), min(tk, t)
    topk = min(IDX_TOPK, t // 4)
    return pl.pallas_call(
        functools.partial(_dsa_select_prompt_kernel, tq=tq, t=t, tk=tk, topk=topk),
        grid=(bsz, t // tq),
        in_specs=[pl.BlockSpec((1, IDX_HEADS, tq, IDX_DIM), lambda b, i: (b, 0, i, 0)),
                  pl.BlockSpec((1, t, IDX_DIM), lambda b, i: (b, 0, 0)),
                  pl.BlockSpec((1, tq, IDX_HEADS), lambda b, i: (b, i, 0))],
        out_specs=pl.BlockSpec((1, tq, t), lambda b, i: (b, i, 0)),
        out_shape=jax.ShapeDtypeStruct((bsz, t, t), BF16),
        scratch_shapes=[pltpu.VMEM((tq, t), I32)],
        compiler_params=_params("parallel", "parallel"),
        name="dsa_select_prompt",
    )(qi, ki, wi)


def _dsa_select_sample_kernel(pt_ref, qi_ref, wi_ref, *refs, n_pg, n_steps, tpad, n_new, topk, nb):
    pages = refs[:nb * n_pg]
    new_ref, mask_ref, u_ref = refs[nb * n_pg:]
    j = pl.program_id(1)

    def scores(bi, kc):
        rel = jnp.maximum(_dot(qi_ref[bi], kc.astype(BF16)), 0.0) * wi_ref[bi]
        return rel.reshape(tpad, IDX_HEADS, LANES).sum(axis=1)

    @pl.when(j < n_steps - 1)
    def _():
        for bi in range(nb):
            for i in range(n_pg):
                start = pl.multiple_of((j * n_pg + i) * LANES, LANES)
                u_ref[bi * tpad:(bi + 1) * tpad, pl.ds(start, LANES)] = _sortable(scores(bi, pages[bi * n_pg + i][0]))

    @pl.when(j == n_steps - 1)
    def _():
        t_idx = lax.broadcasted_iota(I32, (tpad, LANES), 0)
        c_idx = lax.broadcasted_iota(I32, (tpad, LANES), 1)
        base = (n_steps - 1) * n_pg * LANES
        for bi in range(nb):
            s = jnp.where((c_idx <= t_idx) & (c_idx < n_new), scores(bi, new_ref[bi]), -jnp.inf)
            u_ref[bi * tpad:(bi + 1) * tpad, base:base + LANES] = _sortable(s)
        for i in range(1, n_pg):
            u_ref[:, base + i * LANES:base + (i + 1) * LANES] = jnp.full((nb * tpad, LANES), KEY_NEG_INF, I32)

        def write(ci, sel, u):
            keep = sel & (u > KEY_NEG_INF) & (u < KEY_POS_INF)
            mask_ref[:, :, ci * LANES:(ci + 1) * LANES] = jnp.where(keep, 1.0, 0.0).reshape(nb, tpad, LANES)

        _emit_selection(u_ref, topk, write)


def _dsa_select_sample(page_table, qi, wi, pool_kidx, new_ki, n_new, n_pg=8):
    bsz, n_pages = page_table.shape
    tpad = qi.shape[1] // IDX_HEADS
    n_steps = n_pages // n_pg + 1
    width = n_steps * n_pg * LANES
    topk = min(IDX_TOPK, (n_pages * PAGE_SIZE + n_new) // 4)
    nb = _batch_rows(bsz)
    fix = lambda b, j, pt: (b, 0, 0)

    def page_map(bi, i):
        return lambda b, j, pt: (pt[b * nb + bi, jnp.minimum(j * n_pg + i, n_pages - 1)], 0, 0)

    grid_spec = pltpu.PrefetchScalarGridSpec(
        num_scalar_prefetch=1,
        grid=(bsz // nb, n_steps),
        in_specs=[pl.BlockSpec((nb, tpad * IDX_HEADS, IDX_DIM), fix), pl.BlockSpec((nb, tpad * IDX_HEADS, 1), fix)]
        + [pl.BlockSpec((1, IDX_DIM, PAGE_SIZE), page_map(bi, i)) for bi in range(nb) for i in range(n_pg)]
        + [pl.BlockSpec((nb, IDX_DIM, PAGE_SIZE), fix)],
        out_specs=pl.BlockSpec((nb, tpad, width), fix),
        scratch_shapes=[pltpu.VMEM((nb * tpad, width), I32)],
    )
    return pl.pallas_call(
        functools.partial(_dsa_select_sample_kernel, n_pg=n_pg, n_steps=n_steps, tpad=tpad, n_new=n_new, topk=topk, nb=nb),
        grid_spec=grid_spec,
        out_shape=jax.ShapeDtypeStruct((bsz, tpad, width), F32),
        compiler_params=_params("parallel", "arbitrary"),
        name="dsa_select_sample",
    )(page_table, qi, wi, *([pool_kidx] * (nb * n_pg)), new_ki)


def _batch_rows(bsz):
    return 4 if bsz % 4 == 0 else (2 if bsz % 2 == 0 else 1)


def _page_map(i, n_pg, n_pages):
    return lambda b, j, pt: (pt[b, jnp.minimum(j * n_pg + i, n_pages - 1)], 0, 0)


def _page_map4(i, n_pg, n_pages):
    return lambda b, j, pt: (pt[b, jnp.minimum(j * n_pg + i, n_pages - 1)], 0, 0, 0)


def _paged_flash_kernel(pt_ref, q_ref, *refs, n_pg, n_steps, nb):
    pages = refs[:nb * n_pg]
    new_ref, bias_ref, o_ref, m_ref, l_ref, acc_ref = refs[nb * n_pg:]
    j = pl.program_id(1)

    @pl.when(j == 0)
    def _():
        m_ref[...] = jnp.full(m_ref.shape, NEG, F32)
        l_ref[...] = jnp.zeros(l_ref.shape, F32)
        acc_ref[...] = jnp.zeros(acc_ref.shape, F32)

    def update(bi, kv):
        q = q_ref[bi]
        s = [_dot(q, k().astype(BF16)) + bias_ref[bi, :, i * LANES:(i + 1) * LANES] for i, (k, _) in enumerate(kv)]
        m_prev = m_ref[bi]
        m_new = m_prev
        for si in s:
            m_new = jnp.maximum(m_new, jnp.max(si, axis=1, keepdims=True))
        alpha = jnp.exp(m_prev - m_new)
        l_new = alpha * l_ref[bi]
        acc = acc_ref[bi] * pltpu.repeat(alpha, KV_COLS // LANES, axis=1)
        for si, (_, v) in zip(s, kv):
            p = jnp.where(si > 0.5 * NEG, jnp.exp(si - m_new), 0.0)
            l_new = l_new + jnp.sum(p, axis=1, keepdims=True)
            acc = acc + _dot_t(p.astype(BF16), v().astype(BF16))
        l_ref[bi] = l_new
        acc_ref[bi] = acc
        m_ref[bi] = m_new

    def loaders(ref, lead):
        return (lambda: ref[lead, 0]), (lambda: ref[lead, 1])

    @pl.when(j < n_steps - 1)
    def _():
        for bi in range(nb):
            update(bi, [loaders(pages[bi * n_pg + i], 0) for i in range(n_pg)])

    @pl.when(j == n_steps - 1)
    def _():
        for bi in range(nb):
            update(bi, [loaders(new_ref, bi)])
        o_ref[...] = acc_ref[...] / jnp.maximum(jnp.concatenate([l_ref[...]] * (KV_COLS // LANES), axis=-1), 1e-30)


def _paged_flash(page_table, q_bd, pool_t, new_page_t, bias, n_pg):
    bsz, n_pages = page_table.shape
    n_steps = n_pages // n_pg + 1
    rows = q_bd.shape[1]
    nb = _batch_rows(bsz)
    fix = lambda b, j, pt: (b, 0, 0)
    fix4 = lambda b, j, pt: (b, 0, 0, 0)
    page_block = (1, 2, KV_COLS, PAGE_SIZE)

    def page_map(bi, i):
        return lambda b, j, pt: (pt[b * nb + bi, jnp.minimum(j * n_pg + i, n_pages - 1)], 0, 0, 0)

    grid_spec = pltpu.PrefetchScalarGridSpec(
        num_scalar_prefetch=1,
        grid=(bsz // nb, n_steps),
        in_specs=[pl.BlockSpec((nb, rows, KV_COLS), fix)]
        + [pl.BlockSpec(page_block, page_map(bi, i)) for bi in range(nb) for i in range(n_pg)]
        + [pl.BlockSpec((nb, 2, KV_COLS, PAGE_SIZE), fix4),
           pl.BlockSpec((nb, rows, n_pg * LANES), lambda b, j, pt: (b, 0, j))],
        out_specs=pl.BlockSpec((nb, rows, KV_COLS), fix),
        scratch_shapes=[pltpu.VMEM((nb, rows, LANES), F32), pltpu.VMEM((nb, rows, LANES), F32),
                        pltpu.VMEM((nb, rows, KV_COLS), F32)],
    )
    return pl.pallas_call(
        functools.partial(_paged_flash_kernel, n_pg=n_pg, n_steps=n_steps, nb=nb),
        grid_spec=grid_spec,
        out_shape=jax.ShapeDtypeStruct((bsz, rows, KV_COLS), F32),
        compiler_params=_params("parallel", "arbitrary"),
        name="paged_flash",
    )(page_table, q_bd, *([pool_t] * (nb * n_pg)), new_page_t, bias)


def _cumsum_kernel(pt_ref, *refs, n_pg, n_steps, nb):
    pages = refs[:nb * n_pg]
    new_ref, o_ref, carry_ref = refs[nb * n_pg:]
    j = pl.program_id(1)
    r_i = lax.broadcasted_iota(I32, (LANES, LANES), 0)
    c_i = lax.broadcasted_iota(I32, (LANES, LANES), 1)
    tri = jnp.where(r_i <= c_i, 1.0, 0.0).astype(BF16)

    @pl.when(j == 0)
    def _():
        carry_ref[...] = jnp.zeros(carry_ref.shape, F32)

    ones = jnp.ones((LANES, LANES), BF16)

    def steps(bi, xs):
        local = [_dot_hp(x, tri) for x in xs]
        total = [_dot_hp(x, ones) for x in xs]
        carry = carry_ref[bi]
        for i in range(len(xs)):
            o_ref[bi, :, i * LANES:(i + 1) * LANES] = local[i] + carry
            carry = carry + total[i]
        carry_ref[bi] = carry

    @pl.when(j < n_steps - 1)
    def _():
        for bi in range(nb):
            steps(bi, [pages[bi * n_pg + i][0] for i in range(n_pg)])

    @pl.when(j == n_steps - 1)
    def _():
        for bi in range(nb):
            steps(bi, [new_ref[bi]])
        for i in range(1, n_pg):
            o_ref[:, :, i * LANES:(i + 1) * LANES] = jnp.zeros((nb, N_HEADS, LANES), F32)


def _paged_cumsum(page_table, pool_t, new_t, n_pg):
    bsz, n_pages = page_table.shape
    n_steps = n_pages // n_pg + 1
    nb = _batch_rows(bsz)
    fix = lambda b, j, pt: (b, 0, 0)

    def page_map(bi, i):
        return lambda b, j, pt: (pt[b * nb + bi, jnp.minimum(j * n_pg + i, n_pages - 1)], 0, 0)

    grid_spec = pltpu.PrefetchScalarGridSpec(
        num_scalar_prefetch=1,
        grid=(bsz // nb, n_steps),
        in_specs=[pl.BlockSpec((1, N_HEADS, LANES), page_map(bi, i)) for bi in range(nb) for i in range(n_pg)]
        + [pl.BlockSpec((nb, N_HEADS, LANES), fix)],
        out_specs=pl.BlockSpec((nb, N_HEADS, n_pg * LANES), lambda b, j, pt: (b, 0, j)),
        scratch_shapes=[pltpu.VMEM((nb, N_HEADS, LANES), F32)],
    )
    return pl.pallas_call(
        functools.partial(_cumsum_kernel, n_pg=n_pg, n_steps=n_steps, nb=nb),
        grid_spec=grid_spec,
        out_shape=jax.ShapeDtypeStruct((bsz, N_HEADS, n_steps * n_pg * LANES), F32),
        compiler_params=_params("parallel", "arbitrary"),
        name="fox_cumsum",
    )(page_table, *([pool_t] * (nb * n_pg)), new_t)


def _gelu_tanh(x):
    return 0.5 * x * (1.0 + jnp.tanh(0.7978845608028654 * (x + 0.044715 * x * x * x)))


def _nsa_compress_kernel(pt_ref, *refs, n_pg, n_steps, nc):
    pages = refs[:n_pg]
    pe_ref, w1_ref, w2_ref, o_ref, x_ref = refs[n_pg:]
    j = pl.program_id(1)
    per = PAGE_SIZE // CMP_STRIDE
    n_chunk = x_ref.shape[1]
    for i in range(0, n_pg, 2):
        start = pl.multiple_of((j * n_pg + i) * per, 2 * per)
        for ck in range(2 * N_KV_HEADS):
            x_ref[ck, pl.ds(start, 2 * per), :] = jnp.concatenate([pages[i][0, ck], pages[i + 1][0, ck]], axis=0)

    @pl.when(j == n_steps - 1)
    def _():
        half = CMP_STRIDE * HEAD_DIM
        rows = N_KV_HEADS * n_chunk
        row = lax.broadcasted_iota(I32, (N_KV_HEADS, n_chunk, HEAD_DIM), 1)
        for c in range(2):
            w1 = w1_ref[c]
            part = _dot(x_ref[c * N_KV_HEADS:(c + 1) * N_KV_HEADS].reshape(rows, half), w1)
            pe = pe_ref[c]
            pe_term = _dot(pe[:, :half], w1)[:, :CMP_HID] + _dot(pe[:, half:], w1)[:, CMP_HID:]
            h = pe_term[0:1, :] + part[:, :CMP_HID] + pltpu.roll(part[:, CMP_HID:], rows - 1, 0)
            out = _dot(_gelu_tanh(h).astype(BF16), w2_ref[c]).reshape(N_KV_HEADS, n_chunk, HEAD_DIM)
            o_ref[0, c * N_KV_HEADS:(c + 1) * N_KV_HEADS] = jnp.where(row < nc, out, 0.0)


def _nsa_compress(page_table, pool_t, pe8, w1cat, w2, n_pg=8):
    bsz, n_pages = page_table.shape
    per = PAGE_SIZE // CMP_STRIDE
    n_chunk = n_pages * per
    nc = n_chunk - CMP_LEN // CMP_STRIDE + 1
    n_steps = n_pages // n_pg
    width = CMP_STRIDE * HEAD_DIM

    def page_map(i):
        return lambda b, j, pt: (pt[b, j * n_pg + i], 0, 0, 0)

    fix3 = lambda b, j, pt: (0, 0, 0)
    grid_spec = pltpu.PrefetchScalarGridSpec(
        num_scalar_prefetch=1,
        grid=(bsz, n_steps),
        in_specs=[pl.BlockSpec((1, 2 * N_KV_HEADS, per, width), page_map(i)) for i in range(n_pg)]
        + [pl.BlockSpec(pe8.shape, fix3), pl.BlockSpec(w1cat.shape, fix3), pl.BlockSpec(w2.shape, fix3)],
        out_specs=pl.BlockSpec((1, 2 * N_KV_HEADS, n_chunk, HEAD_DIM), lambda b, j, pt: (b, 0, 0, 0)),
        scratch_shapes=[pltpu.VMEM((2 * N_KV_HEADS, n_chunk, width), BF16)],
    )
    return pl.pallas_call(
        functools.partial(_nsa_compress_kernel, n_pg=n_pg, n_steps=n_steps, nc=nc),
        grid_spec=grid_spec,
        out_shape=jax.ShapeDtypeStruct((bsz, 2 * N_KV_HEADS, n_chunk, HEAD_DIM), F32),
        compiler_params=_params("parallel", "arbitrary"),
        name="nsa_compress",
    )(page_table, *([pool_t] * n_pg), pe8, w1cat, w2), nc


def _nsa_cmp_kernel(q_ref, ck_ref, cv_ref, cover_ref, o_ref, sel_ref, u_ref, *, tq, pos0, nc, n_sel):
    t0 = pos0 + pl.program_id(1) * tq
    ncp = ck_ref.shape[2]
    nsp = cover_ref.shape[0]
    rows = GROUP * tq
    n_idx = lax.broadcasted_iota(I32, (tq, ncp), 1)
    t_idx = t0 + lax.broadcasted_iota(I32, (tq, ncp), 0)
    c_ok = ((n_idx * CMP_STRIDE + CMP_LEN - 1 <= t_idx) & (n_idx < nc))[None]
    blk = lax.broadcasted_iota(I32, (nsp, tq), 0)
    cur = (t0 + lax.broadcasted_iota(I32, (nsp, tq), 1)) // SEL_BLOCK
    forced = (blk == 0) | (blk == cur) | (blk == cur - 1)
    cover_t = cover_ref[...]
    for g in range(N_KV_HEADS):
        q = q_ref[0, g].reshape(rows, HEAD_DIM)
        s3 = jnp.where(c_ok, _dot_t(q, ck_ref[0, g]).reshape(GROUP, tq, ncp), NEG)
        m = jnp.max(s3, axis=-1, keepdims=True)
        e = jnp.where(c_ok, jnp.exp(s3 - m), 0.0)
        p = e / jnp.maximum(jnp.sum(e, axis=-1, keepdims=True), 1e-30)
        o = _dot(p.reshape(rows, ncp).astype(BF16), cv_ref[0, g])
        o_ref[0, :, g * GROUP * HEAD_DIM:(g + 1) * GROUP * HEAD_DIM] = jnp.concatenate(
            [o[j * tq:(j + 1) * tq] for j in range(GROUP)], axis=1)
        psum = p[0] + p[1] + p[2] + p[3]
        hi = psum.astype(BF16)
        r1 = psum - hi.astype(F32)
        mid = r1.astype(BF16)
        lo = (r1 - mid.astype(F32)).astype(BF16)
        imp = _dot_t(cover_t, hi) + _dot_t(cover_t, mid) + _dot_t(cover_t, lo)
        imp = jnp.where(forced, jnp.inf, imp)
        imp = jnp.where(blk <= cur, imp, -jnp.inf)
        u_ref[:, g * tq:(g + 1) * tq] = _sortable(imp)
    sel, u = _topk_mask_cols(u_ref, n_sel)
    sel_ref[0, 0] = jnp.where(sel & (u > KEY_NEG_INF), 1.0, 0.0)


def _nsa_cmp_select(q, cmp_k, cmp_v, pos0, nc, n_keys, tq):
    bsz, _, _, t, _ = q.shape
    ncp = cmp_k.shape[2]
    ns = -(-n_keys // SEL_BLOCK)
    nsp = -(-ns // LANES) * LANES
    n_sel = min(SEL_TOPN, ns)
    c0 = np.arange(ncp)[:, None] * CMP_STRIDE
    s0 = np.arange(nsp)[None, :] * SEL_BLOCK
    cover = (c0 <= s0 + SEL_BLOCK - 1) & (c0 + CMP_LEN - 1 >= s0) & (np.arange(ncp)[:, None] < nc) & (np.arange(nsp)[None, :] < ns)
    cover_t = jnp.asarray(cover.T.astype(np.float32), dtype=BF16)
    nq = t // tq
    qspec = pl.BlockSpec((1, N_KV_HEADS, GROUP, tq, HEAD_DIM), lambda b, i: (b, 0, 0, i, 0))
    cspec = pl.BlockSpec((1, N_KV_HEADS, ncp, HEAD_DIM), lambda b, i: (b, 0, 0, 0))
    o_c, sel = pl.pallas_call(
        functools.partial(_nsa_cmp_kernel, tq=tq, pos0=pos0, nc=nc, n_sel=n_sel),
        grid=(bsz, nq),
        in_specs=[qspec, cspec, cspec, pl.BlockSpec((nsp, ncp), lambda b, i: (0, 0))],
        out_specs=[pl.BlockSpec((1, tq, Q_DIM), lambda b, i: (b, i, 0)),
                   pl.BlockSpec((1, 1, nsp, N_KV_HEADS * tq), lambda b, i: (b, i, 0, 0))],
        out_shape=[jax.ShapeDtypeStruct((bsz, t, Q_DIM), F32), jax.ShapeDtypeStruct((bsz, nq, nsp, N_KV_HEADS * tq), F32)],
        scratch_shapes=[pltpu.VMEM((nsp, N_KV_HEADS * tq), I32)],
        compiler_params=_params("parallel", "parallel"),
        name="nsa_cmp_select",
    )(q, cmp_k, cmp_v, cover_t)
    sel = sel.reshape(bsz, nq, nsp, N_KV_HEADS, tq).transpose(0, 3, 1, 4, 2).reshape(bsz, N_KV_HEADS, t, nsp)
    return o_c, sel


def _kmean_kernel(*refs):
    o_ref = refs[-1]
    tot = jnp.sum(refs[0][0], axis=0, keepdims=True)
    for r in refs[1:-1]:
        tot = tot + jnp.sum(r[0], axis=0, keepdims=True)
    o_ref[0, 0] = tot * (1.0 / MOBA_BLOCK)


def _kmean_prompt(kv):
    bsz, t, _ = kv.shape
    nb = t // MOBA_BLOCK
    return pl.pallas_call(
        _kmean_kernel,
        grid=(bsz, nb),
        in_specs=[pl.BlockSpec((1, MOBA_BLOCK, KV_COLS), lambda b, i: (b, i, 0))],
        out_specs=pl.BlockSpec((1, 1, 1, KV_COLS), lambda b, i: (b, i, 0, 0)),
        out_shape=jax.ShapeDtypeStruct((bsz, nb, 1, KV_COLS), F32),
        compiler_params=_params("parallel", "parallel"),
        name="kmean_prompt",
    )(kv)


def _kmean_sample_kernel(pt_ref, *refs, per, nb):
    o_ref = refs[-1]
    ones = jnp.ones((8, PAGE_SIZE), BF16)
    blocks = len(refs[:-1]) // (per * nb)
    for n in range(nb * blocks):
        tot = jnp.zeros((8, KV_COLS), F32)
        for r in refs[n * per:(n + 1) * per]:
            x = r[0, 0]
            hi = x.astype(BF16)
            r1 = x - hi.astype(F32)
            mid = r1.astype(BF16)
            lo = (r1 - mid.astype(F32)).astype(BF16)
            tot = tot + _dot_t(ones, hi) + _dot_t(ones, mid) + _dot_t(ones, lo)
        o_ref[n // blocks, n % blocks] = tot[0:1] * (1.0 / MOBA_BLOCK)


def _kmean_sample(page_table, pool_t, n_pg):
    bsz, n_pages = page_table.shape
    per = MOBA_BLOCK // PAGE_SIZE
    n_blocks = n_pages // per
    nb = _batch_rows(bsz)

    def page_map(bi, i):
        return lambda b, n, pt: (pt[b * nb + bi, n * n_pg + i], 0, 0, 0)

    grid_spec = pltpu.PrefetchScalarGridSpec(
        num_scalar_prefetch=1,
        grid=(bsz // nb, n_pages // n_pg),
        in_specs=[pl.BlockSpec((1, 1, KV_COLS, PAGE_SIZE), page_map(bi, i)) for bi in range(nb) for i in range(n_pg)],
        out_specs=pl.BlockSpec((nb, n_pg // per, 1, KV_COLS), lambda b, n, pt: (b, n, 0, 0)),
    )
    return pl.pallas_call(
        functools.partial(_kmean_sample_kernel, per=per, nb=nb),
        grid_spec=grid_spec,
        out_shape=jax.ShapeDtypeStruct((bsz, n_blocks, 1, KV_COLS), F32),
        compiler_params=_params("parallel", "parallel"),
        name="kmean_sample",
    )(page_table, *([pool_t] * (nb * n_pg)))


def _moba_select_kernel(q_ref, km_ref, sel_ref, *, tq, pos0, k_top):
    t0 = pos0 + pl.program_id(1) * tq
    nbp = km_ref.shape[2]
    rows = GROUP * tq
    blk = lax.broadcasted_iota(I32, (nbp, rows), 0)
    n_past = (t0 + (lax.broadcasted_iota(I32, (nbp, rows), 1) & (tq - 1))) // MOBA_BLOCK
    for g in range(N_KV_HEADS):
        q = q_ref[0, g].reshape(rows, HEAD_DIM)
        s = jnp.where(blk < n_past, _dot_t(km_ref[0, g], q), -jnp.inf)
        sel = jnp.zeros((nbp, rows), F32)
        for _ in range(k_top):
            m = jnp.max(s, axis=0, keepdims=True)
            first = jnp.min(jnp.where(s == m, blk, nbp), axis=0, keepdims=True)
            pick = blk == first
            sel = jnp.where(pick & (m > -jnp.inf), 1.0, sel)
            s = jnp.where(pick, -jnp.inf, s)
        sel_ref[0, 0, g] = sel


def _moba_select(q, kmean, pos0, nb, tq):
    bsz, _, _, t, _ = q.shape
    nbp = kmean.shape[2]
    nq = t // tq
    qspec = pl.BlockSpec((1, N_KV_HEADS, GROUP, tq, HEAD_DIM), lambda b, i: (b, 0, 0, i, 0))
    sel = pl.pallas_call(
        functools.partial(_moba_select_kernel, tq=tq, pos0=pos0, k_top=min(MOBA_TOPK, nb)),
        grid=(bsz, nq),
        in_specs=[qspec, pl.BlockSpec((1, N_KV_HEADS, nbp, HEAD_DIM), lambda b, i: (b, 0, 0, 0))],
        out_specs=pl.BlockSpec((1, 1, N_KV_HEADS, nbp, GROUP * tq), lambda b, i: (b, i, 0, 0, 0)),
        out_shape=jax.ShapeDtypeStruct((bsz, nq, N_KV_HEADS, nbp, GROUP * tq), F32),
        compiler_params=_params("parallel", "parallel"),
        name="moba_select",
    )(q, kmean)
    sel = sel.reshape(bsz, nq, N_KV_HEADS, nbp, GROUP, tq).transpose(0, 2, 4, 1, 5, 3)
    return sel.reshape(bsz, N_KV_HEADS, GROUP, t, nbp)


def _rope_tables(pos):
    half = HEAD_DIM // 2
    inv = ROPE_THETA ** (-jnp.arange(half, dtype=F32) / half)
    ang = pos.astype(F32)[:, None] * inv[None, :]
    cos, sin = jnp.cos(ang), jnp.sin(ang)
    rep = LANES // HEAD_DIM
    return jnp.concatenate([cos, cos] * rep, axis=1), jnp.concatenate([-sin, sin] * rep, axis=1)


def _pad_to(x, axis, size):
    pad = [(0, 0)] * x.ndim
    pad[axis] = (0, size - x.shape[axis])
    return jnp.pad(x, pad)


def _q_groups(q):
    b, t = q.shape[:2]
    return q.transpose(0, 2, 1, 3).reshape(b, N_KV_HEADS, GROUP, t, HEAD_DIM)


def _pages_t(kv):
    return kv.transpose(0, 2, 3, 4, 1).reshape(kv.shape[0], 2, KV_COLS, PAGE_SIZE)


def _fit(x, width):
    return x[..., :width] if x.shape[-1] >= width else _pad_to(x, x.ndim - 1, width)


_HEAD_TO_GROUP = np.equal(np.arange(N_HEADS)[:, None] // GROUP, np.arange(N_KV_HEADS)[None, :]).astype(np.float32)


def _q_block_diag(q):
    s, tn = q.shape[:2]
    qb = q[:, :, :, None, :] * jnp.asarray(_HEAD_TO_GROUP, dtype=q.dtype)[None, None, :, :, None]
    return qb.reshape(s, tn * N_HEADS, KV_COLS)


def _extract_block_diag(o, tn):
    s = o.shape[0]
    o5 = o.reshape(s, tn, N_HEADS, N_KV_HEADS, HEAD_DIM) * _HEAD_TO_GROUP[None, None, :, :, None]
    return o5.sum(axis=3).reshape(s * tn, Q_DIM)


def _new_page(kv_new):
    return _pages_t(_pad_to(kv_new, 1, PAGE_SIZE))


def _pick_pages(n_pages):
    for n in (8, 4, 2, 1):
        if n_pages % n == 0:
            return n


def _identity_pages(bsz, n_pages):
    return jnp.arange(bsz * n_pages, dtype=I32).reshape(bsz, n_pages)


def _rows_th(ok, s, tn):
    return jnp.where(ok, 0.0, NEG).astype(F32).reshape(s, tn * N_HEADS, ok.shape[-1])


def _kv_stack(k, v, bsz, t):
    return jnp.stack([k.reshape(bsz, t, N_KV_HEADS, HEAD_DIM), v.reshape(bsz, t, N_KV_HEADS, HEAD_DIM)], axis=2)


def _kv_forms(c_k, rope):
    return [(c_k, KV_COLS, rope, 1.0, BF16, "keys_t"), (c_k + KV_COLS, KV_COLS, False, 1.0, BF16, "values_1")]


def _dsa_project(x, bsz, t, w_in, cs, q_scale, prompt):
    hm = "heads" if prompt else "tok"
    c_qi = Q_DIM + 2 * KV_COLS
    c_ki = c_qi + IDX_HEADS * IDX_DIM
    segs = [(0, Q_DIM, True, q_scale, BF16, hm), (Q_DIM, KV_COLS, True, 1.0, F32, "tok"),
            (Q_DIM + KV_COLS, KV_COLS, False, 1.0, F32, "tok"), (c_qi, IDX_HEADS * IDX_DIM, True, 1.0, BF16, hm),
            (c_ki, IDX_DIM, True, 1.0, F32, "tok"), (c_ki + IDX_DIM, IDX_HEADS, False, IDX_SCALE, F32, "tok")]
    if prompt:
        segs += _kv_forms(Q_DIM, True)
    q, k, v, qi, ki, wi, *flash_kv = _proj(x, w_in, cs, segs, bsz)
    if not prompt:
        q, qi = q.reshape(bsz, t, N_HEADS, HEAD_DIM), qi.reshape(bsz, t, IDX_HEADS, IDX_DIM)
    return q, qi, wi.reshape(bsz, t, IDX_HEADS), _kv_stack(k, v, bsz, t), ki.reshape(bsz, t, IDX_DIM), flash_kv


def _head_groups(q):
    return q.reshape(q.shape[0], N_KV_HEADS, GROUP, q.shape[2], HEAD_DIM)


def _dsa_prompt(x, bsz, t, w_in, cs):
    q, qi, wi, kv, ki, (k_t, v1) = _dsa_project(x, bsz, t, w_in, cs, ATTN_SCALE * LOG2E, True)
    mask = _dsa_select_prompt(qi, ki.astype(BF16), wi)
    o = _flash_prompt("dsa", _head_groups(q), k_t, v1, [mask])
    return o.reshape(bsz * t, Q_DIM), kv, ki


def _dsa_sample(x, s, tn, w_in, cs, page_table, cache_kv, cache_kidx, n_pg):
    q, qi, wi, kv, ki, _ = _dsa_project(x, s, tn, w_in, cs, ATTN_SCALE, False)
    tpad = 8
    qi_p = _pad_to(qi, 1, tpad).reshape(s, tpad * IDX_HEADS, IDX_DIM)
    wi_p = _pad_to(wi, 1, tpad).reshape(s, tpad * IDX_HEADS, 1)
    mask = _dsa_select_sample(page_table, qi_p, wi_p, cache_kidx.transpose(0, 2, 1),
                              _pad_to(ki, 1, PAGE_SIZE).transpose(0, 2, 1), tn, n_pg)
    ok = jnp.broadcast_to(mask[:, :tn, None, :] > 0.5, (s, tn, N_HEADS, mask.shape[-1]))
    o = _paged_flash(page_table, _q_block_diag(q), _pages_t(cache_kv), _new_page(kv), _rows_th(ok, s, tn), n_pg)
    return _extract_block_diag(o, tn), kv, ki


def _fox_project(x, bsz, t, w_in, b_f, cs, q_scale, prompt):
    segs = [(0, Q_DIM, False, q_scale, BF16, "heads" if prompt else "tok"), (Q_DIM, KV_COLS, False, 1.0, F32, "tok"),
            (Q_DIM + KV_COLS, KV_COLS, False, 1.0, F32, "tok"), (Q_DIM + 2 * KV_COLS, N_HEADS, False, 1.0, F32, "tok")]
    if prompt:
        segs += _kv_forms(Q_DIM, False)
    q, k, v, f, *flash_kv = _proj(x, w_in, cs, segs, bsz)
    logf = jax.nn.log_sigmoid(f.reshape(bsz, t, N_HEADS) + b_f)
    if not prompt:
        q = q.reshape(bsz, t, N_HEADS, HEAD_DIM)
    return q, _kv_stack(k, v, bsz, t), logf, flash_kv


def _fox_prompt(x, bsz, t, w_in, b_f, cs):
    q, kv, logf, (k_t, v1) = _fox_project(x, bsz, t, w_in, b_f, cs, ATTN_SCALE * LOG2E, True)
    n_pages = t // LANES
    pool_t = logf.reshape(bsz, n_pages, LANES, N_HEADS).transpose(0, 1, 3, 2).reshape(bsz * n_pages, N_HEADS, LANES)
    c = _paged_cumsum(_identity_pages(bsz, n_pages), pool_t, jnp.zeros((bsz, N_HEADS, LANES), F32), _pick_pages(n_pages))
    c = (c[:, :, :t] * LOG2E).reshape(bsz, N_KV_HEADS, GROUP, t)
    o = _flash_prompt("fox", _head_groups(q), k_t, v1, [c])
    return o.reshape(bsz * t, Q_DIM), kv, logf


def _fox_sample(x, s, tn, w_in, b_f, cs, page_table, cache_kv, cache_logf, n_pg):
    q, kv, logf, _ = _fox_project(x, s, tn, w_in, b_f, cs, ATTN_SCALE, False)
    past = page_table.shape[1] * PAGE_SIZE
    c = _paged_cumsum(page_table, cache_logf.transpose(0, 2, 1), _pad_to(logf.transpose(0, 2, 1), 2, LANES), n_pg)
    col = jnp.arange(c.shape[-1])
    valid = (col[None, :] < past) | ((col[None, :] - past <= jnp.arange(tn)[:, None]) & (col[None, :] < past + tn))
    bias = jnp.where(valid[None, :, None, :], -c[:, None, :, :], NEG).reshape(s, tn * N_HEADS, c.shape[-1])
    o = _paged_flash(page_table, _q_block_diag(q), _pages_t(cache_kv), _new_page(kv), bias, n_pg)
    return _extract_block_diag(o, tn), kv, logf


def _nsa_project(x, bsz, t, w_in, b_gate, cs, q_scale, prompt):
    hm = "heads" if prompt else "tok"
    c = Q_DIM
    segs = [(0, Q_DIM, False, ATTN_SCALE, BF16, hm), (0, Q_DIM, True, q_scale, BF16, hm)]
    for rope in (False, False, True, False, True, False):
        segs.append((c, KV_COLS, rope, 1.0, F32, "tok"))
        c += KV_COLS
    segs.append((c, 3 * N_HEADS, False, 1.0, F32, "tok"))
    if prompt:
        segs += _kv_forms(Q_DIM + 2 * KV_COLS, True) + _kv_forms(Q_DIM + 4 * KV_COLS, True)
    q, q_rot, kc, vc, ks, vs, kw, vw, g, *flash_kv = _proj(x, w_in, cs, segs, bsz)
    gate = jax.nn.sigmoid(g.reshape(bsz, t, 3 * N_HEADS) + b_gate).reshape(bsz, t, 3, N_HEADS)
    if not prompt:
        q, q_rot = q.reshape(bsz, t, N_HEADS, HEAD_DIM), q_rot.reshape(bsz, t, N_HEADS, HEAD_DIM)
    return (q, q_rot, gate, _kv_stack(kc, vc, bsz, t), _kv_stack(ks, vs, bsz, t), _kv_stack(kw, vw, bsz, t), flash_kv)


def _nsa_weights(pe, w1, w2):
    r = CMP_LEN // CMP_STRIDE
    w1cat = w1.reshape(2, r, CMP_STRIDE * HEAD_DIM, CMP_HID).transpose(0, 2, 1, 3).reshape(2, CMP_STRIDE * HEAD_DIM, r * CMP_HID)
    pe8 = jnp.broadcast_to(pe.reshape(2, 1, CMP_LEN * HEAD_DIM), (2, 8, CMP_LEN * HEAD_DIM))
    return pe8.astype(BF16), w1cat.astype(BF16), w2.astype(BF16)


def _chunk_pages(kv):
    n = kv.shape[0]
    per = PAGE_SIZE // CMP_STRIDE
    x = kv.reshape(n, per, CMP_STRIDE, 2 * N_KV_HEADS, HEAD_DIM).transpose(0, 3, 1, 2, 4)
    return x.reshape(n, 2 * N_KV_HEADS, per, CMP_STRIDE * HEAD_DIM).astype(BF16)


def _gate_mix(gate, o_c, o_s, o_w):
    m = gate.shape[0] * gate.shape[1]
    return (gate.reshape(m, 3 * N_HEADS), o_c.reshape(m, Q_DIM), o_s.reshape(m, Q_DIM), o_w.reshape(m, Q_DIM))


def _nsa_prompt(x, bsz, t, w_in, b_gate, cmp_w, cs, tq=128, tk=512):
    q, q_rot, gate, kv_cmp, kv_slc, kv_win, (ks_t, vs1, kw_t, vw1) = _nsa_project(x, bsz, t, w_in, b_gate, cs,
                                                                                  ATTN_SCALE * LOG2E, True)
    n_pages = t // PAGE_SIZE
    cmp, nc = _nsa_compress(_identity_pages(bsz, n_pages), _chunk_pages(kv_cmp.reshape(bsz * n_pages, PAGE_SIZE, 2, N_KV_HEADS, HEAD_DIM)),
                            *cmp_w, n_pg=_pick_pages(n_pages))
    cmp = cmp.astype(BF16)
    tq, tk = min(tq, t), min(tk, t)
    o_c, selblk = _nsa_cmp_select(_head_groups(q), cmp[:, :N_KV_HEADS], cmp[:, N_KV_HEADS:], 0, nc, t, tq)
    qg = _head_groups(q_rot)
    e3 = _expand_matrix(selblk.shape[-1], SEL_BLOCK, t, tk)
    o_s = _flash_prompt("nsa_sel", qg, ks_t, vs1, [selblk, e3], tq=2 * tq, tk=tk)
    o_w = _flash_prompt("nsa_win", qg, kw_t, vw1, [], tk=tk)
    return _gate_mix(gate, o_c, o_s, o_w), kv_cmp, kv_slc, kv_win[:, -min(WINDOW, t):]


def _nsa_sample(x, s, tn, w_in, b_gate, cmp_w, cs, page_table, cache_cmp, cache_slc, state_win, n_pg):
    q, q_rot, gate, kv_cmp, kv_slc, kv_win, _ = _nsa_project(x, s, tn, w_in, b_gate, cs, ATTN_SCALE, False)
    past = page_table.shape[1] * PAGE_SIZE
    tpad = 8
    cmp, nc = _nsa_compress(page_table, _chunk_pages(cache_cmp), *cmp_w, n_pg=n_pg)
    cmp = cmp.astype(BF16)
    o_c, selblk = _nsa_cmp_select(_q_groups(_pad_to(q, 1, tpad)), cmp[:, :N_KV_HEADS], cmp[:, N_KV_HEADS:],
                                  past, nc, past + tn, tpad)
    o_c = o_c[:, :tn]
    qbd = _q_block_diag(q_rot)
    width = (page_table.shape[1] // n_pg + 1) * n_pg * LANES
    col = jnp.arange(width)
    pos = past + jnp.arange(tn)
    sel_key = jnp.repeat(selblk[:, :, :tn, :-(-width // SEL_BLOCK)], SEL_BLOCK, axis=-1)[..., :width]
    ok = (sel_key > 0.5) & (col[None, :] <= pos[:, None])[None, None]
    ok = jnp.broadcast_to(ok.transpose(0, 2, 1, 3)[:, :, :, None, :], (s, tn, N_KV_HEADS, GROUP, width))
    o_s = _paged_flash(page_table, qbd, _pages_t(cache_slc), _new_page(kv_slc),
                       _rows_th(ok.reshape(s, tn, N_HEADS, width), s, tn), n_pg)
    win_buf = state_win.shape[1]
    n_wp = win_buf // PAGE_SIZE
    wcol = jnp.arange(2 * n_wp * LANES)
    win_pos = jnp.where(wcol < win_buf, past - win_buf + wcol, jnp.where(wcol < win_buf + tn, past + wcol - win_buf, -1))
    w_ok = (win_pos[None, :] <= pos[:, None]) & (win_pos[None, :] >= pos[:, None] - WINDOW) & (win_pos[None, :] >= 0)
    w_ok = jnp.broadcast_to(w_ok[None, :, None, :], (s, tn, N_HEADS, wcol.shape[0]))
    win_pages = _pages_t(state_win.reshape(s * n_wp, PAGE_SIZE, 2, N_KV_HEADS, HEAD_DIM))
    o_w = _paged_flash(_identity_pages(s, n_wp), qbd, win_pages, _new_page(kv_win), _rows_th(w_ok, s, tn), n_wp)
    unbd = lambda o: _extract_block_diag(o, tn).reshape(s, tn, Q_DIM)
    win = jnp.concatenate([state_win, kv_win], axis=1)[:, -win_buf:]
    return _gate_mix(gate, o_c, unbd(o_s), unbd(o_w)), kv_cmp, kv_slc, win


def _moba_project(x, bsz, t, w_in, cs, q_scale, prompt):
    segs = [(0, Q_DIM, True, q_scale, BF16, "heads" if prompt else "tok"), (Q_DIM, KV_COLS, True, 1.0, F32, "tok"),
            (Q_DIM + KV_COLS, KV_COLS, False, 1.0, F32, "tok")]
    if prompt:
        segs += _kv_forms(Q_DIM, True)
    q, k, v, *flash_kv = _proj(x, w_in, cs, segs, bsz)
    if not prompt:
        q = q.reshape(bsz, t, N_HEADS, HEAD_DIM)
    return q, _kv_stack(k, v, bsz, t), flash_kv


def _kmean_heads(km):
    b, nb = km.shape[:2]
    return _pad_to(km.reshape(b, nb, N_KV_HEADS, HEAD_DIM).transpose(0, 2, 1, 3), 2, -(-nb // 16) * 16).astype(BF16)


def _moba_prompt(x, bsz, t, w_in, cs, tq=128, tk=512):
    q, kv, (k_t, v1) = _moba_project(x, bsz, t, w_in, cs, ATTN_SCALE * LOG2E, True)
    tq, tk = min(tq, t), min(tk, t)
    nb = -(-t // MOBA_BLOCK)
    km = _kmean_heads(_kmean_prompt(kv.reshape(bsz, t, 2 * KV_COLS)))
    qg = _head_groups(q)
    sel = _moba_select(qg, km, 0, nb, tq)
    e3 = _expand_matrix(km.shape[2], MOBA_BLOCK, t, tk)
    o = _flash_prompt("moba", qg, k_t, v1, [sel, e3], tq=2 * tq, tk=tk)
    return o.reshape(bsz * t, Q_DIM), kv


def _moba_sample(x, s, tn, w_in, cs, page_table, cache_kv, n_pg):
    q, kv, _ = _moba_project(x, s, tn, w_in, cs, ATTN_SCALE, False)
    past = page_table.shape[1] * PAGE_SIZE
    tpad = 8
    pool = _pages_t(cache_kv)
    nb = -(-(past + tn) // MOBA_BLOCK)
    km = _kmean_heads(_kmean_sample(page_table, pool, n_pg))
    sel = _moba_select(_q_groups(_pad_to(q, 1, tpad)), km, past, nb, tpad)
    width = (page_table.shape[1] // n_pg + 1) * n_pg * LANES
    col = jnp.arange(width)
    pos = past + jnp.arange(tn)
    sel_key = _fit(jnp.repeat(sel[:, :, :, :tn], MOBA_BLOCK, axis=-1), width)
    own = (col[None, :] // MOBA_BLOCK == pos[:, None] // MOBA_BLOCK) & (col[None, :] <= pos[:, None])
    ok = (sel_key > 0.5) | own[None, None, None]
    ok = ok.transpose(0, 3, 1, 2, 4).reshape(s, tn, N_HEADS, width)
    o = _paged_flash(page_table, _q_block_diag(q), pool, _new_page(kv), _rows_th(ok, s, tn), n_pg)
    return _extract_block_diag(o, tn), kv


def _cast_w(w):
    return _pad_to(w, 1, -(-w.shape[1] // LANES) * LANES).astype(BF16)


def kernel(x_prompt, x_sample, cache_a_kv, cache_a_kidx, cache_b_kv, cache_b_logf, cache_c_cmp_kv, cache_c_slc_kv, state_c_win_kv, cache_d_kv, page_table, a_w_in, a_w_out, b_w_in, b_b_f, b_w_out, c_w_in, c_b_gate, c_cmp_pe, c_cmp_w1, c_cmp_w2, c_w_out, d_w_in, d_w_out, ln_g, ln_b, ffn_w_gu, ffn_w_down):
    bsz, t, d = x_prompt.shape
    s, tn, _ = x_sample.shape
    n_pages = page_table.shape[1]
    past = n_pages * PAGE_SIZE
    n_pg = _pick_pages(n_pages)
    cs_p = _rope_tables(jnp.arange(t, dtype=I32))
    cs_s = _rope_tables(jnp.tile(past + jnp.arange(tn, dtype=I32), s))
    xp = x_prompt.reshape(bsz * t, d)
    xs = x_sample.reshape(s * tn, d)
    cmp_w = _nsa_weights(c_cmp_pe, c_cmp_w1, c_cmp_w2)
    w_out = [_cast_w(w) for w in (a_w_out, b_w_out, c_w_out, d_w_out)]

    op, a_kv_p, a_kidx_p = _dsa_prompt(xp, bsz, t, _cast_w(a_w_in), cs_p)
    os_, a_kv_s, a_kidx_s = _dsa_sample(xs, s, tn, _cast_w(a_w_in), cs_s, page_table, cache_a_kv, cache_a_kidx, n_pg)

    def finish(i, xp, xs, op, os_):
        xp = _out_ln(op, w_out[i], xp, ln_g[i, 0], ln_b[i, 0])
        xs = _out_ln(os_, w_out[i], xs, ln_g[i, 0], ln_b[i, 0])
        wgu, wd = ffn_w_gu[i].astype(BF16), ffn_w_down[i].astype(BF16)
        xp = _ffn_ln(xp, wgu, wd, ln_g[i, 1], ln_b[i, 1])
        xs = _ffn_ln(xs, wgu, wd, ln_g[i, 1], ln_b[i, 1])
        return xp, xs

    xp, xs = finish(0, xp, xs, op, os_)

    op, b_kv_p, b_logf_p = _fox_prompt(xp, bsz, t, _cast_w(b_w_in), b_b_f, cs_p)
    os_, b_kv_s, b_logf_s = _fox_sample(xs, s, tn, _cast_w(b_w_in), b_b_f, cs_s, page_table, cache_b_kv, cache_b_logf, n_pg)
    xp, xs = finish(1, xp, xs, op, os_)

    op, c_cmp_kv_p, c_slc_kv_p, c_win_kv_p = _nsa_prompt(xp, bsz, t, _cast_w(c_w_in), c_b_gate, cmp_w, cs_p)
    os_, c_cmp_kv_s, c_slc_kv_s, c_win_kv_s = _nsa_sample(xs, s, tn, _cast_w(c_w_in), c_b_gate, cmp_w, cs_s, page_table,
                                                          cache_c_cmp_kv, cache_c_slc_kv, state_c_win_kv, n_pg)
    xp, xs = finish(2, xp, xs, op, os_)

    op, d_kv_p = _moba_prompt(xp, bsz, t, _cast_w(d_w_in), cs_p)
    os_, d_kv_s = _moba_sample(xs, s, tn, _cast_w(d_w_in), cs_s, page_table, cache_d_kv, n_pg)
    xp, xs = finish(3, xp, xs, op, os_)

    return (xp.reshape(bsz, t, d), xs.reshape(s, tn, d), a_kv_p, a_kv_s, a_kidx_p, a_kidx_s, b_kv_p, b_kv_s,
            b_logf_p, b_logf_s, c_cmp_kv_p, c_cmp_kv_s, c_slc_kv_p, c_slc_kv_s, c_win_kv_p, c_win_kv_s, d_kv_p, d_kv_s)
```

```python
import functools

import numpy as np
import jax
import jax.numpy as jnp
from jax import lax
from jax.experimental import pallas as pl
from jax.experimental.pallas import tpu as pltpu

F32 = jnp.float32
BF16 = jnp.bfloat16
I32 = jnp.int32

N_HEADS = 16
HEAD_DIM = 64
N_KV_HEADS = 4
GROUP = N_HEADS // N_KV_HEADS
Q_DIM = N_HEADS * HEAD_DIM
KV_COLS = N_KV_HEADS * HEAD_DIM
DEPTH = 4
PAGE_SIZE = 128
ROPE_THETA = 10000.0
LN_EPS = 1e-5
ALPHA = (2 * DEPTH) ** 0.25
ATTN_SCALE = HEAD_DIM ** -0.5
LOG2E = 1.4426950408889634
IDX_HEADS = 8
IDX_DIM = 64
IDX_TOPK = 256
IDX_SCALE = (IDX_HEADS * IDX_DIM) ** -0.5
CMP_LEN = 32
CMP_STRIDE = 16
CMP_HID = 2 * HEAD_DIM
SEL_BLOCK = 64
SEL_TOPN = 16
WINDOW = 512
MOBA_BLOCK = 256
MOBA_TOPK = 3

LANES = 128
VMEM_LIMIT = 56 * 2 ** 20
NEG = -1e30
KEY_NEG_INF = -2139095041
KEY_POS_INF = 2139095040
INT_MIN = -2 ** 31


def _params(*sem):
    return pltpu.CompilerParams(dimension_semantics=sem, vmem_limit_bytes=VMEM_LIMIT)


def _dot_t(a, b):
    return lax.dot_general(a, b, (((1,), (1,)), ((), ())), preferred_element_type=F32)


def _dot(a, b):
    return jnp.dot(a, b, preferred_element_type=F32)


def _dot_hp(a, b):
    hi = a.astype(BF16)
    r1 = a - hi.astype(F32)
    mid = r1.astype(BF16)
    lo = (r1 - mid.astype(F32)).astype(BF16)
    return _dot(hi, b) + _dot(mid, b) + _dot(lo, b)


def _proj_kernel(x_ref, w_ref, cos_ref, sin_ref, *out_refs, segs):
    acc = _dot(x_ref[...].astype(BF16), w_ref[...])
    tm = acc.shape[0]
    lane = lax.broadcasted_iota(I32, (tm, LANES), 1)
    first_half = (lane & (HEAD_DIM - 1)) < HEAD_DIM // 2
    ones_col = jnp.where(lane == HEAD_DIM, 1.0, 0.0)
    for (c0, width, rope, scale, form), o_ref in zip(segs, out_refs):
        if c0 % LANES:
            o_ref[...] = (acc[:, c0:c0 + width] * scale).astype(o_ref.dtype)
            continue
        for j in range(-(-width // LANES)):
            x = acc[:, c0 + j * LANES:c0 + (j + 1) * LANES]
            if rope:
                swapped = jnp.where(first_half, pltpu.roll(x, LANES - HEAD_DIM // 2, 1), pltpu.roll(x, HEAD_DIM // 2, 1))
                x = x * cos_ref[...] + swapped * sin_ref[...]
            if scale != 1.0:
                x = x * scale
            if form == "tok":
                wj = min(LANES, width - j * LANES)
                o_ref[:, j * LANES:j * LANES + wj] = x[:, :wj].astype(o_ref.dtype)
            elif form == "heads":
                o_ref[0, 2 * j] = x[:, :HEAD_DIM].astype(o_ref.dtype)
                o_ref[0, 2 * j + 1] = x[:, HEAD_DIM:].astype(o_ref.dtype)
            elif form == "keys_t":
                xt = x.T
                o_ref[0, 2 * j] = xt[:HEAD_DIM].astype(o_ref.dtype)
                o_ref[0, 2 * j + 1] = xt[HEAD_DIM:].astype(o_ref.dtype)
            else:
                o_ref[0, 2 * j] = jnp.where(lane < HEAD_DIM, x, ones_col).astype(o_ref.dtype)
                o_ref[0, 2 * j + 1] = jnp.where(lane < HEAD_DIM, pltpu.roll(x, HEAD_DIM, 1), ones_col).astype(o_ref.dtype)


def _proj(x, w, cs, segs, bsz=None):
    m, k = x.shape
    n = w.shape[1]
    tm = min(m, 512)
    cos_t, sin_t = cs
    r_blocks = cos_t.shape[0] // tm
    tab = pl.BlockSpec((tm, LANES), lambda i: (i % r_blocks, 0))
    specs, shapes = [], []
    for _, width, _, _, dt, form in segs:
        nh = width // HEAD_DIM
        if form == "tok":
            specs.append(pl.BlockSpec((tm, width), lambda i: (i, 0)))
            shapes.append(jax.ShapeDtypeStruct((m, width), dt))
            continue
        t = m // bsz
        tpb = t // tm
        if form == "keys_t":
            specs.append(pl.BlockSpec((1, nh, HEAD_DIM, tm), lambda i: (i // tpb, 0, 0, i % tpb)))
            shapes.append(jax.ShapeDtypeStruct((bsz, nh, HEAD_DIM, t), dt))
        else:
            last = HEAD_DIM if form == "heads" else LANES
            specs.append(pl.BlockSpec((1, nh, tm, last), lambda i: (i // tpb, 0, i % tpb, 0)))
            shapes.append(jax.ShapeDtypeStruct((bsz, nh, t, last), dt))
    return pl.pallas_call(
        functools.partial(_proj_kernel, segs=tuple(s[:4] + (s[5],) for s in segs)),
        grid=(m // tm,),
        in_specs=[pl.BlockSpec((tm, k), lambda i: (i, 0)), pl.BlockSpec((k, n), lambda i: (0, 0)), tab, tab],
        out_specs=specs,
        out_shape=shapes,
        compiler_params=_params("parallel"),
        name="in_proj",
    )(x, w, cos_t, sin_t)


def _layer_norm(y, g, b):
    mu = jnp.mean(y, axis=-1, keepdims=True)
    d = y - mu
    var = jnp.mean(d * d, axis=-1, keepdims=True)
    return d * lax.rsqrt(var + LN_EPS) * g + b


def _out_ln_kernel(o_ref, w_ref, x_ref, g_ref, b_ref, y_ref):
    y = ALPHA * x_ref[...] + _dot(o_ref[...].astype(BF16), w_ref[...])
    y_ref[...] = _layer_norm(y, g_ref[...], b_ref[...])


def _out_ln_gated_kernel(gate_ref, oc_ref, os_ref, ow_ref, w_ref, x_ref, g_ref, b_ref, y_ref):
    gate = gate_ref[...]
    heads = []
    for h in range(N_HEADS):
        cols = slice(h * HEAD_DIM, (h + 1) * HEAD_DIM)
        heads.append(gate[:, h:h + 1] * oc_ref[:, cols] + gate[:, N_HEADS + h:N_HEADS + h + 1] * os_ref[:, cols]
                     + gate[:, 2 * N_HEADS + h:2 * N_HEADS + h + 1] * ow_ref[:, cols])
    o = jnp.concatenate(heads, axis=1)
    y = ALPHA * x_ref[...] + _dot(o.astype(BF16), w_ref[...])
    y_ref[...] = _layer_norm(y, g_ref[...], b_ref[...])


def _out_ln(o, w, x, g, b):
    m, d = x.shape
    tm = min(m, 512)
    row = lambda i: (i, 0)
    fix = lambda i: (0, 0)
    gated = isinstance(o, tuple)
    o_in = list(o) if gated else [o]
    return pl.pallas_call(
        _out_ln_gated_kernel if gated else _out_ln_kernel,
        grid=(m // tm,),
        in_specs=[pl.BlockSpec((tm, a.shape[1]), row) for a in o_in]
        + [pl.BlockSpec(w.shape, fix), pl.BlockSpec((tm, d), row), pl.BlockSpec((1, d), fix), pl.BlockSpec((1, d), fix)],
        out_specs=pl.BlockSpec((tm, d), row),
        out_shape=jax.ShapeDtypeStruct((m, d), F32),
        compiler_params=_params("parallel"),
        name="out_proj_ln",
    )(*o_in, w, x, g.reshape(1, d), b.reshape(1, d))


def _ffn_ln_kernel(x_ref, wgu_ref, wd_ref, g_ref, b_ref, y_ref, *, d_ff, chunk):
    x = x_ref[...]
    xb = x.astype(BF16)
    acc = jnp.zeros(x.shape, F32)
    for c in range(d_ff // chunk):
        gate = _dot(xb, wgu_ref[:, c * chunk:(c + 1) * chunk])
        up = _dot(xb, wgu_ref[:, d_ff + c * chunk:d_ff + (c + 1) * chunk])
        h = gate * (1.0 / (1.0 + jnp.exp(-gate))) * up
        acc = acc + _dot(h.astype(BF16), wd_ref[c * chunk:(c + 1) * chunk, :])
    y_ref[...] = _layer_norm(ALPHA * x + acc, g_ref[...], b_ref[...])


def _ffn_ln(x, wgu, wd, g, b):
    m, d = x.shape
    d_ff = wd.shape[0]
    tm = min(m, 256)
    row = lambda i: (i, 0)
    fix = lambda i: (0, 0)
    return pl.pallas_call(
        functools.partial(_ffn_ln_kernel, d_ff=d_ff, chunk=256),
        grid=(m // tm,),
        in_specs=[pl.BlockSpec((tm, d), row), pl.BlockSpec((d, 2 * d_ff), fix), pl.BlockSpec((d_ff, d), fix),
                  pl.BlockSpec((1, d), fix), pl.BlockSpec((1, d), fix)],
        out_specs=pl.BlockSpec((tm, d), row),
        out_shape=jax.ShapeDtypeStruct((m, d), F32),
        compiler_params=_params("parallel"),
        name="ffn_ln",
    )(x, wgu, wd, g.reshape(1, d), b.reshape(1, d))


def _sortable(x):
    x = jnp.where(x == 0.0, 0.0, x)
    b = lax.bitcast_convert_type(x, I32)
    return b ^ ((b >> 31) & I32(0x7FFFFFFF))


def _kth_largest_key(u_ref, k):
    rows = u_ref.shape[0]

    def count_ge(cand):
        return jnp.sum((u_ref[...] >= cand).astype(I32), axis=1, keepdims=True)

    base = jnp.where(count_ge(jnp.zeros((rows, 1), I32)) >= k, I32(0), I32(INT_MIN))

    def body(i, base):
        cand = base | jnp.left_shift(I32(1), 30 - i)
        return jnp.where(count_ge(cand) >= k, cand, base)

    return lax.fori_loop(0, 31, body, base)


def _emit_selection(u_ref, k, write):
    rows, n = u_ref.shape
    thr = _kth_largest_key(u_ref, k)
    n_gt = jnp.sum((u_ref[...] > thr).astype(I32), axis=1, keepdims=True)
    need = (k - n_gt).astype(F32)
    r_i = lax.broadcasted_iota(I32, (LANES, LANES), 0)
    c_i = lax.broadcasted_iota(I32, (LANES, LANES), 1)
    tri = jnp.where(r_i <= c_i, 1.0, 0.0).astype(BF16)
    carry = jnp.zeros((rows, 1), F32)
    for ci in range(n // LANES):
        u = u_ref[:, ci * LANES:(ci + 1) * LANES]
        tie = u == thr
        tie_f = jnp.where(tie, 1.0, 0.0)
        inc = _dot(tie_f.astype(BF16), tri)
        rank = carry + inc - tie_f
        sel = (u > thr) | (tie & (rank < need))
        write(ci, sel, u)
        carry = carry + inc[:, LANES - 1:LANES]


def _topk_mask_cols(u_ref, k):
    n, r = u_ref.shape

    def count(pred):
        return jnp.sum(pred(u_ref[...]).astype(I32), axis=0, keepdims=True)

    base = jnp.where(count(lambda u: u >= 0) >= k, I32(0), I32(INT_MIN))

    def radix(i, base):
        cand = base | jnp.left_shift(I32(1), 30 - i)
        return jnp.where(count(lambda u: u >= cand) >= k, cand, base)

    thr = lax.fori_loop(0, 31, radix, base)
    need = (k - count(lambda u: u > thr)).astype(F32)
    u = u_ref[...]
    tie = u == thr
    lower = jnp.where(lax.broadcasted_iota(I32, (n, n), 0) > lax.broadcasted_iota(I32, (n, n), 1), 1.0, 0.0)
    rank = _dot(lower.astype(BF16), jnp.where(tie, 1.0, 0.0).astype(BF16))
    return (u > thr) | (tie & (rank < need)), u


def _flash_prompt_kernel(*refs, kind, tq, tk, n_extra):
    q_ref, k_ref, v_ref = refs[:3]
    extra = refs[3:3 + n_extra]
    o_ref, m_ref, acc_ref, s_ref = refs[3 + n_extra:]
    g = pl.program_id(1)
    q0 = pl.program_id(2) * tq
    rows = GROUP * tq
    q = q_ref[0, 0].reshape(rows, HEAD_DIM)
    m_ref[...] = jnp.full(m_ref.shape, NEG, F32)
    acc_ref[...] = jnp.zeros(acc_ref.shape, F32)
    c_diag = q0 // tk
    c_lo = jnp.maximum(q0 - WINDOW, 0) // tk if kind == "nsa_win" else 0
    if kind == "nsa_sel":
        selb = extra[0][0, 0].astype(BF16)
    if kind == "moba":
        selb = extra[0][0, 0].reshape(rows, extra[0].shape[-1]).astype(BF16)

    def scores(c):
        return _dot(q, k_ref[0, g, :, pl.ds(pl.multiple_of(c * tk, tk), tk)])

    def chunk(c, diag):
        if kind in ("nsa_sel", "moba"):
            hit = _dot(selb, extra[1][c]) > 0.5
        s3 = s_ref[c % 2].reshape(GROUP, tq, tk)
        if not diag:
            s_ref[(c + 1) % 2] = scores(c + 1)
        start = pl.multiple_of(c * tk, tk)
        v = v_ref[0, g, pl.ds(start, tk), :]
        ok = None
        if diag or kind == "nsa_win":
            t_idx = q0 + lax.broadcasted_iota(I32, (GROUP, tq, tk), 1)
            s_idx = start + lax.broadcasted_iota(I32, (GROUP, tq, tk), 2)
            causal = s_idx <= t_idx
        if kind == "fox":
            ck = extra[0][0, 0, :, pl.ds(start, tk)]
            s3 = s3 - ck[:, None, :]
            ok = causal if diag else None
        elif kind == "dsa":
            msk = extra[0][0, :, pl.ds(start, tk)]
            ok = jnp.broadcast_to((msk > 0)[None], (GROUP, tq, tk))
        elif kind == "nsa_sel":
            ok = jnp.broadcast_to(hit[None], (GROUP, tq, tk))
            ok = (ok & causal) if diag else ok
        elif kind == "nsa_win":
            ok = causal & (s_idx >= t_idx - WINDOW)
        else:
            ok = hit.reshape(GROUP, tq, tk)
            if diag:
                ok = ok | (causal & ((s_idx // MOBA_BLOCK) == (t_idx // MOBA_BLOCK)))
        if ok is not None:
            s3 = jnp.where(ok, s3, NEG)
        s = s3.reshape(rows, tk)
        m_prev = m_ref[...]
        m_new = jnp.maximum(m_prev, jnp.max(s, axis=1, keepdims=True))
        alpha = jnp.exp2(m_prev - m_new)
        p = jnp.exp2(s - pltpu.repeat(m_new, tk // LANES, axis=1))
        acc_ref[...] = acc_ref[...] * alpha + _dot(p.astype(BF16), v)
        m_ref[...] = m_new

    def body(c, carry):
        chunk(c, False)
        return carry

    s_ref[c_lo % 2] = scores(c_lo)
    lax.fori_loop(c_lo, c_diag, body, 0)
    chunk(c_diag, True)
    acc = acc_ref[...]
    o = acc[:, :HEAD_DIM] / jnp.maximum(acc[:, HEAD_DIM:HEAD_DIM + 1], 1e-30)
    o_ref[0] = jnp.concatenate([o[j * tq:(j + 1) * tq] for j in range(GROUP)], axis=1)


def _flash_prompt(kind, q, k, v, extra, tq=128, tk=512):
    bsz, _, _, t, _ = q.shape
    tk = min(tk, t)
    tq = min(tq, t)
    qspec = pl.BlockSpec((1, 1, GROUP, tq, HEAD_DIM), lambda b, g, i: (b, g, 0, i, 0))
    kvspec = pl.BlockSpec((1, N_KV_HEADS, HEAD_DIM, t), lambda b, g, i: (b, 0, 0, 0))
    vspec = pl.BlockSpec((1, N_KV_HEADS, t, LANES), lambda b, g, i: (b, 0, 0, 0))
    if kind == "fox":
        especs = [pl.BlockSpec((1, 1, GROUP, t), lambda b, g, i: (b, g, 0, 0))]
    elif kind == "dsa":
        especs = [pl.BlockSpec((1, tq, t), lambda b, g, i: (b, i, 0))]
    elif kind == "nsa_sel":
        especs = [pl.BlockSpec((1, 1, tq, extra[0].shape[-1]), lambda b, g, i: (b, g, i, 0)),
                  pl.BlockSpec(extra[1].shape, lambda b, g, i: (0, 0, 0))]
    elif kind == "moba":
        especs = [pl.BlockSpec((1, 1, GROUP, tq, extra[0].shape[-1]), lambda b, g, i: (b, g, 0, i, 0)),
                  pl.BlockSpec(extra[1].shape, lambda b, g, i: (0, 0, 0))]
    else:
        especs = []
    rows = GROUP * tq
    return pl.pallas_call(
        functools.partial(_flash_prompt_kernel, kind=kind, tq=tq, tk=tk, n_extra=len(extra)),
        grid=(bsz, N_KV_HEADS, t // tq),
        in_specs=[qspec, kvspec, vspec] + especs,
        out_specs=pl.BlockSpec((1, tq, GROUP * HEAD_DIM), lambda b, g, i: (b, i, g)),
        out_shape=jax.ShapeDtypeStruct((bsz, t, Q_DIM), F32),
        scratch_shapes=[pltpu.VMEM((rows, LANES), F32), pltpu.VMEM((rows, LANES), F32),
                        pltpu.VMEM((2, rows, tk), F32)],
        compiler_params=_params("parallel", "parallel", "parallel"),
        name="flash_prompt_" + kind,
    )(q, k, v, *extra)


def _expand_matrix(n_blocks_padded, block, t, tk):
    s = np.arange(t)
    e = (s[None, :] // block == np.arange(n_blocks_padded)[:, None]).astype(np.float32)
    e = e.reshape(n_blocks_padded, t // tk, tk).transpose(1, 0, 2)
    return jnp.asarray(e, dtype=BF16)


def _dsa_select_prompt_kernel(qi_ref, ki_ref, wi_ref, mask_ref, u_ref, *, tq, t, tk, topk):
    q0 = pl.program_id(1) * tq
    n_act = q0 // tk + 1
    sub = tk // LANES
    w = wi_ref[0]
    t_idx = q0 + lax.broadcasted_iota(I32, (tq, tk), 0)
    k_off = lax.broadcasted_iota(I32, (tq, tk), 1)

    def chunk_at(c):
        return pl.ds(pl.multiple_of(c * tk, tk), tk)

    def fill(c, carry):
        kc = ki_ref[0, chunk_at(c), :]
        s = jnp.zeros((tq, tk), F32)
        for h in range(IDX_HEADS):
            s = s + jnp.maximum(_dot_t(qi_ref[0, h], kc), 0.0) * w[:, h:h + 1]
        s = jnp.where(c * tk + k_off <= t_idx, s, -jnp.inf)
        u_ref[:, chunk_at(c)] = _sortable(s)
        return carry

    lax.fori_loop(0, n_act, fill, 0)

    def count(pred):
        def body(c, acc):
            hit = pred(u_ref[:, chunk_at(c)]).astype(I32)
            for i in range(sub):
                acc = acc + hit[:, i * LANES:(i + 1) * LANES]
            return acc
        acc = lax.fori_loop(0, n_act, body, jnp.zeros((tq, LANES), I32))
        return jnp.sum(acc, axis=1, keepdims=True)

    base = jnp.where(count(lambda u: u >= 0) >= topk, I32(0), I32(INT_MIN))

    def radix(i, base):
        cand = base | jnp.left_shift(I32(1), 30 - i)
        return jnp.where(count(lambda u: u >= cand) >= topk, cand, base)

    thr = lax.fori_loop(0, 31, radix, base)
    need = (topk - count(lambda u: u > thr)).astype(F32)
    r_i = lax.broadcasted_iota(I32, (LANES, LANES), 0)
    c_i = lax.broadcasted_iota(I32, (LANES, LANES), 1)
    tri = jnp.where(r_i <= c_i, 1.0, 0.0).astype(BF16)

    def emit(c, carry):
        for i in range(sub):
            at = pl.ds(pl.multiple_of(c * tk + i * LANES, LANES), LANES)
            u = u_ref[:, at]
            tie = u == thr
            tie_f = jnp.where(tie, 1.0, 0.0)
            inc = _dot(tie_f.astype(BF16), tri)
            sel = (u > thr) | (tie & (carry + inc - tie_f < need))
            keep = sel & (u > KEY_NEG_INF) & (u < KEY_POS_INF)
            mask_ref[0, :, at] = jnp.where(keep, 1.0, 0.0).astype(BF16)
            carry = carry + inc[:, LANES - 1:LANES]
        return carry

    lax.fori_loop(0, n_act, emit, jnp.zeros((tq, 1), F32))

    def clear(c, carry):
        mask_ref[0, :, chunk_at(c)] = jnp.zeros((tq, tk), BF16)
        return carry

    lax.fori_loop(n_act, t // tk, clear, 0)


def _dsa_select_prompt(qi, ki, wi, tq=256, tk=512):
    bsz, _, t, _ = qi.shape
    tq, tk = min(tq, t), min(tk, t)
    topk = min(IDX_TOPK, t // 4)
    return pl.pallas_call(
        functools.partial(_dsa_select_prompt_kernel, tq=tq, t=t, tk=tk, topk=topk),
        grid=(bsz, t // tq),
        in_specs=[pl.BlockSpec((1, IDX_HEADS, tq, IDX_DIM), lambda b, i: (b, 0, i, 0)),
                  pl.BlockSpec((1, t, IDX_DIM), lambda b, i: (b, 0, 0)),
                  pl.BlockSpec((1, tq, IDX_HEADS), lambda b, i: (b, i, 0))],
        out_specs=pl.BlockSpec((1, tq, t), lambda b, i: (b, i, 0)),
        out_shape=jax.ShapeDtypeStruct((bsz, t, t), BF16),
        scratch_shapes=[pltpu.VMEM((tq, t), I32)],
        compiler_params=_params("parallel", "parallel"),
        name="dsa_select_prompt",
    )(qi, ki, wi)


def _dsa_select_sample_kernel(pt_ref, qi_ref, wi_ref, *refs, n_pg, n_steps, tpad, n_new, topk, nb):
    pages = refs[:nb * n_pg]
    new_ref, mask_ref, u_ref = refs[nb * n_pg:]
    j = pl.program_id(1)

    def scores(bi, kc):
        rel = jnp.maximum(_dot(qi_ref[bi], kc.astype(BF16)), 0.0) * wi_ref[bi]
        return rel.reshape(tpad, IDX_HEADS, LANES).sum(axis=1)

    @pl.when(j < n_steps - 1)
    def _():
        for bi in range(nb):
            for i in range(n_pg):
                start = pl.multiple_of((j * n_pg + i) * LANES, LANES)
                u_ref[bi * tpad:(bi + 1) * tpad, pl.ds(start, LANES)] = _sortable(scores(bi, pages[bi * n_pg + i][0]))

    @pl.when(j == n_steps - 1)
    def _():
        t_idx = lax.broadcasted_iota(I32, (tpad, LANES), 0)
        c_idx = lax.broadcasted_iota(I32, (tpad, LANES), 1)
        base = (n_steps - 1) * n_pg * LANES
        for bi in range(nb):
            s = jnp.where((c_idx <= t_idx) & (c_idx < n_new), scores(bi, new_ref[bi]), -jnp.inf)
            u_ref[bi * tpad:(bi + 1) * tpad, base:base + LANES] = _sortable(s)
        for i in range(1, n_pg):
            u_ref[:, base + i * LANES:base + (i + 1) * LANES] = jnp.full((nb * tpad, LANES), KEY_NEG_INF, I32)

        def write(ci, sel, u):
            keep = sel & (u > KEY_NEG_INF) & (u < KEY_POS_INF)
            mask_ref[:, :, ci * LANES:(ci + 1) * LANES] = jnp.where(keep, 1.0, 0.0).reshape(nb, tpad, LANES)

        _emit_selection(u_ref, topk, write)


def _dsa_select_sample(page_table, qi, wi, pool_kidx, new_ki, n_new, n_pg=8):
    bsz, n_pages = page_table.shape
    tpad = qi.shape[1] // IDX_HEADS
    n_steps = n_pages // n_pg + 1
    width = n_steps * n_pg * LANES
    topk = min(IDX_TOPK, (n_pages * PAGE_SIZE + n_new) // 4)
    nb = _batch_rows(bsz)
    fix = lambda b, j, pt: (b, 0, 0)

    def page_map(bi, i):
        return lambda b, j, pt: (pt[b * nb + bi, jnp.minimum(j * n_pg + i, n_pages - 1)], 0, 0)

    grid_spec = pltpu.PrefetchScalarGridSpec(
        num_scalar_prefetch=1,
        grid=(bsz // nb, n_steps),
        in_specs=[pl.BlockSpec((nb, tpad * IDX_HEADS, IDX_DIM), fix), pl.BlockSpec((nb, tpad * IDX_HEADS, 1), fix)]
        + [pl.BlockSpec((1, IDX_DIM, PAGE_SIZE), page_map(bi, i)) for bi in range(nb) for i in range(n_pg)]
        + [pl.BlockSpec((nb, IDX_DIM, PAGE_SIZE), fix)],
        out_specs=pl.BlockSpec((nb, tpad, width), fix),
        scratch_shapes=[pltpu.VMEM((nb * tpad, width), I32)],
    )
    return pl.pallas_call(
        functools.partial(_dsa_select_sample_kernel, n_pg=n_pg, n_steps=n_steps, tpad=tpad, n_new=n_new, topk=topk, nb=nb),
        grid_spec=grid_spec,
        out_shape=jax.ShapeDtypeStruct((bsz, tpad, width), F32),
        compiler_params=_params("parallel", "arbitrary"),
        name="dsa_select_sample",
    )(page_table, qi, wi, *([pool_kidx] * (nb * n_pg)), new_ki)


def _batch_rows(bsz):
    return 4 if bsz % 4 == 0 else (2 if bsz % 2 == 0 else 1)


def _page_map(i, n_pg, n_pages):
    return lambda b, j, pt: (pt[b, jnp.minimum(j * n_pg + i, n_pages - 1)], 0, 0)


def _page_map4(i, n_pg, n_pages):
    return lambda b, j, pt: (pt[b, jnp.minimum(j * n_pg + i, n_pages - 1)], 0, 0, 0)


def _paged_flash_kernel(pt_ref, q_ref, *refs, n_pg, n_steps, nb):
    pages = refs[:nb * n_pg]
    new_ref, bias_ref, o_ref, m_ref, l_ref, acc_ref = refs[nb * n_pg:]
    j = pl.program_id(1)

    @pl.when(j == 0)
    def _():
        m_ref[...] = jnp.full(m_ref.shape, NEG, F32)
        l_ref[...] = jnp.zeros(l_ref.shape, F32)
        acc_ref[...] = jnp.zeros(acc_ref.shape, F32)

    def update(bi, kv):
        q = q_ref[bi]
        s = [_dot(q, k().astype(BF16)) + bias_ref[bi, :, i * LANES:(i + 1) * LANES].astype(F32)
             for i, (k, _) in enumerate(kv)]
        m_prev = m_ref[bi]
        m_new = m_prev
        for si in s:
            m_new = jnp.maximum(m_new, jnp.max(si, axis=1, keepdims=True))
        alpha = jnp.exp(m_prev - m_new)
        l_new = alpha * l_ref[bi]
        acc = acc_ref[bi] * pltpu.repeat(alpha, KV_COLS // LANES, axis=1)
        for si, (_, v) in zip(s, kv):
            p = jnp.where(si > 0.5 * NEG, jnp.exp(si - m_new), 0.0)
            l_new = l_new + jnp.sum(p, axis=1, keepdims=True)
            acc = acc + _dot_t(p.astype(BF16), v().astype(BF16))
        l_ref[bi] = l_new
        acc_ref[bi] = acc
        m_ref[bi] = m_new

    def loaders(ref, lead):
        return (lambda: ref[lead, 0]), (lambda: ref[lead, 1])

    @pl.when(j < n_steps - 1)
    def _():
        for bi in range(nb):
            update(bi, [loaders(pages[bi * n_pg + i], 0) for i in range(n_pg)])

    @pl.when(j == n_steps - 1)
    def _():
        for bi in range(nb):
            update(bi, [loaders(new_ref, bi)])
        o_ref[...] = acc_ref[...] / jnp.maximum(jnp.concatenate([l_ref[...]] * (KV_COLS // LANES), axis=-1), 1e-30)


def _paged_flash(page_table, q_bd, pool_t, new_page_t, bias, n_pg):
    bsz, n_pages = page_table.shape
    n_steps = n_pages // n_pg + 1
    rows = q_bd.shape[1]
    nb = _batch_rows(bsz)
    fix = lambda b, j, pt: (b, 0, 0)
    fix4 = lambda b, j, pt: (b, 0, 0, 0)
    page_block = (1, 2, KV_COLS, PAGE_SIZE)

    def page_map(bi, i):
        return lambda b, j, pt: (pt[b * nb + bi, jnp.minimum(j * n_pg + i, n_pages - 1)], 0, 0, 0)

    grid_spec = pltpu.PrefetchScalarGridSpec(
        num_scalar_prefetch=1,
        grid=(bsz // nb, n_steps),
        in_specs=[pl.BlockSpec((nb, rows, KV_COLS), fix)]
        + [pl.BlockSpec(page_block, page_map(bi, i)) for bi in range(nb) for i in range(n_pg)]
        + [pl.BlockSpec((nb, 2, KV_COLS, PAGE_SIZE), fix4),
           pl.BlockSpec((nb, rows, n_pg * LANES), lambda b, j, pt: (b, 0, j))],
        out_specs=pl.BlockSpec((nb, rows, KV_COLS), fix),
        scratch_shapes=[pltpu.VMEM((nb, rows, LANES), F32), pltpu.VMEM((nb, rows, LANES), F32),
                        pltpu.VMEM((nb, rows, KV_COLS), F32)],
    )
    return pl.pallas_call(
        functools.partial(_paged_flash_kernel, n_pg=n_pg, n_steps=n_steps, nb=nb),
        grid_spec=grid_spec,
        out_shape=jax.ShapeDtypeStruct((bsz, rows, KV_COLS), F32),
        compiler_params=_params("parallel", "arbitrary"),
        name="paged_flash",
    )(page_table, q_bd, *([pool_t] * (nb * n_pg)), new_page_t, bias)


def _cumsum_kernel(pt_ref, *refs, n_pg, n_steps, nb):
    pages = refs[:nb * n_pg]
    new_ref, o_ref, carry_ref = refs[nb * n_pg:]
    j = pl.program_id(1)
    r_i = lax.broadcasted_iota(I32, (LANES, LANES), 0)
    c_i = lax.broadcasted_iota(I32, (LANES, LANES), 1)
    tri = jnp.where(r_i <= c_i, 1.0, 0.0).astype(BF16)

    @pl.when(j == 0)
    def _():
        carry_ref[...] = jnp.zeros(carry_ref.shape, F32)

    ones = jnp.ones((LANES, LANES), BF16)

    def steps(bi, xs):
        local = [_dot_hp(x, tri) for x in xs]
        total = [_dot_hp(x, ones) for x in xs]
        carry = carry_ref[bi]
        for i in range(len(xs)):
            o_ref[bi, :, i * LANES:(i + 1) * LANES] = local[i] + carry
            carry = carry + total[i]
        carry_ref[bi] = carry

    @pl.when(j < n_steps - 1)
    def _():
        for bi in range(nb):
            steps(bi, [pages[bi * n_pg + i][0] for i in range(n_pg)])

    @pl.when(j == n_steps - 1)
    def _():
        for bi in range(nb):
            steps(bi, [new_ref[bi]])
        for i in range(1, n_pg):
            o_ref[:, :, i * LANES:(i + 1) * LANES] = jnp.zeros((nb, N_HEADS, LANES), F32)


def _paged_cumsum(page_table, pool_t, new_t, n_pg):
    bsz, n_pages = page_table.shape
    n_steps = n_pages // n_pg + 1
    nb = _batch_rows(bsz)
    fix = lambda b, j, pt: (b, 0, 0)

    def page_map(bi, i):
        return lambda b, j, pt: (pt[b * nb + bi, jnp.minimum(j * n_pg + i, n_pages - 1)], 0, 0)

    grid_spec = pltpu.PrefetchScalarGridSpec(
        num_scalar_prefetch=1,
        grid=(bsz // nb, n_steps),
        in_specs=[pl.BlockSpec((1, N_HEADS, LANES), page_map(bi, i)) for bi in range(nb) for i in range(n_pg)]
        + [pl.BlockSpec((nb, N_HEADS, LANES), fix)],
        out_specs=pl.BlockSpec((nb, N_HEADS, n_pg * LANES), lambda b, j, pt: (b, 0, j)),
        scratch_shapes=[pltpu.VMEM((nb, N_HEADS, LANES), F32)],
    )
    return pl.pallas_call(
        functools.partial(_cumsum_kernel, n_pg=n_pg, n_steps=n_steps, nb=nb),
        grid_spec=grid_spec,
        out_shape=jax.ShapeDtypeStruct((bsz, N_HEADS, n_steps * n_pg * LANES), F32),
        compiler_params=_params("parallel", "arbitrary"),
        name="fox_cumsum",
    )(page_table, *([pool_t] * (nb * n_pg)), new_t)


def _gelu_tanh(x):
    return 0.5 * x * (1.0 + jnp.tanh(0.7978845608028654 * (x + 0.044715 * x * x * x)))


def _nsa_compress_kernel(pt_ref, *refs, n_pg, n_steps, nc):
    pages = refs[:n_pg]
    pe_ref, w1_ref, w2_ref, o_ref, x_ref = refs[n_pg:]
    j = pl.program_id(1)
    per = PAGE_SIZE // CMP_STRIDE
    n_chunk = x_ref.shape[1]
    for i in range(0, n_pg, 2):
        start = pl.multiple_of((j * n_pg + i) * per, 2 * per)
        for ck in range(2 * N_KV_HEADS):
            x_ref[ck, pl.ds(start, 2 * per), :] = jnp.concatenate([pages[i][0, ck], pages[i + 1][0, ck]], axis=0)

    @pl.when(j == n_steps - 1)
    def _():
        half = CMP_STRIDE * HEAD_DIM
        rows = N_KV_HEADS * n_chunk
        row = lax.broadcasted_iota(I32, (N_KV_HEADS, n_chunk, HEAD_DIM), 1)
        for c in range(2):
            w1 = w1_ref[c]
            part = _dot(x_ref[c * N_KV_HEADS:(c + 1) * N_KV_HEADS].reshape(rows, half), w1)
            pe = pe_ref[c]
            pe_term = _dot(pe[:, :half], w1)[:, :CMP_HID] + _dot(pe[:, half:], w1)[:, CMP_HID:]
            h = pe_term[0:1, :] + part[:, :CMP_HID] + pltpu.roll(part[:, CMP_HID:], rows - 1, 0)
            out = _dot(_gelu_tanh(h).astype(BF16), w2_ref[c]).reshape(N_KV_HEADS, n_chunk, HEAD_DIM)
            o_ref[0, c * N_KV_HEADS:(c + 1) * N_KV_HEADS] = jnp.where(row < nc, out, 0.0)


def _nsa_compress(page_table, pool_t, pe8, w1cat, w2, n_pg=8):
    bsz, n_pages = page_table.shape
    per = PAGE_SIZE // CMP_STRIDE
    n_chunk = n_pages * per
    nc = n_chunk - CMP_LEN // CMP_STRIDE + 1
    n_steps = n_pages // n_pg
    width = CMP_STRIDE * HEAD_DIM

    def page_map(i):
        return lambda b, j, pt: (pt[b, j * n_pg + i], 0, 0, 0)

    fix3 = lambda b, j, pt: (0, 0, 0)
    grid_spec = pltpu.PrefetchScalarGridSpec(
        num_scalar_prefetch=1,
        grid=(bsz, n_steps),
        in_specs=[pl.BlockSpec((1, 2 * N_KV_HEADS, per, width), page_map(i)) for i in range(n_pg)]
        + [pl.BlockSpec(pe8.shape, fix3), pl.BlockSpec(w1cat.shape, fix3), pl.BlockSpec(w2.shape, fix3)],
        out_specs=pl.BlockSpec((1, 2 * N_KV_HEADS, n_chunk, HEAD_DIM), lambda b, j, pt: (b, 0, 0, 0)),
        scratch_shapes=[pltpu.VMEM((2 * N_KV_HEADS, n_chunk, width), BF16)],
    )
    return pl.pallas_call(
        functools.partial(_nsa_compress_kernel, n_pg=n_pg, n_steps=n_steps, nc=nc),
        grid_spec=grid_spec,
        out_shape=jax.ShapeDtypeStruct((bsz, 2 * N_KV_HEADS, n_chunk, HEAD_DIM), F32),
        compiler_params=_params("parallel", "arbitrary"),
        name="nsa_compress",
    )(page_table, *([pool_t] * n_pg), pe8, w1cat, w2), nc


def _nsa_cmp_kernel(q_ref, ck_ref, cv_ref, cover_ref, o_ref, sel_ref, u_ref, *, tq, pos0, nc, n_sel):
    t0 = pos0 + pl.program_id(1) * tq
    ncp = ck_ref.shape[2]
    nsp = cover_ref.shape[0]
    rows = GROUP * tq
    n_idx = lax.broadcasted_iota(I32, (tq, ncp), 1)
    t_idx = t0 + lax.broadcasted_iota(I32, (tq, ncp), 0)
    c_ok = ((n_idx * CMP_STRIDE + CMP_LEN - 1 <= t_idx) & (n_idx < nc))[None]
    blk = lax.broadcasted_iota(I32, (nsp, tq), 0)
    cur = (t0 + lax.broadcasted_iota(I32, (nsp, tq), 1)) // SEL_BLOCK
    forced = (blk == 0) | (blk == cur) | (blk == cur - 1)
    cover_t = cover_ref[...]
    for g in range(N_KV_HEADS):
        q = q_ref[0, g].reshape(rows, HEAD_DIM)
        s3 = jnp.where(c_ok, _dot_t(q, ck_ref[0, g]).reshape(GROUP, tq, ncp), NEG)
        m = jnp.max(s3, axis=-1, keepdims=True)
        e = jnp.where(c_ok, jnp.exp(s3 - m), 0.0)
        p = e / jnp.maximum(jnp.sum(e, axis=-1, keepdims=True), 1e-30)
        o = _dot(p.reshape(rows, ncp).astype(BF16), cv_ref[0, g])
        o_ref[0, :, g * GROUP * HEAD_DIM:(g + 1) * GROUP * HEAD_DIM] = jnp.concatenate(
            [o[j * tq:(j + 1) * tq] for j in range(GROUP)], axis=1)
        psum = p[0] + p[1] + p[2] + p[3]
        hi = psum.astype(BF16)
        r1 = psum - hi.astype(F32)
        mid = r1.astype(BF16)
        lo = (r1 - mid.astype(F32)).astype(BF16)
        imp = _dot_t(cover_t, hi) + _dot_t(cover_t, mid) + _dot_t(cover_t, lo)
        imp = jnp.where(forced, jnp.inf, imp)
        imp = jnp.where(blk <= cur, imp, -jnp.inf)
        u_ref[:, g * tq:(g + 1) * tq] = _sortable(imp)
    sel, u = _topk_mask_cols(u_ref, n_sel)
    sel_ref[0, 0] = jnp.where(sel & (u > KEY_NEG_INF), 1.0, 0.0)


def _nsa_cmp_select(q, cmp_k, cmp_v, pos0, nc, n_keys, tq):
    bsz, _, _, t, _ = q.shape
    ncp = cmp_k.shape[2]
    ns = -(-n_keys // SEL_BLOCK)
    nsp = -(-ns // LANES) * LANES
    n_sel = min(SEL_TOPN, ns)
    c0 = np.arange(ncp)[:, None] * CMP_STRIDE
    s0 = np.arange(nsp)[None, :] * SEL_BLOCK
    cover = (c0 <= s0 + SEL_BLOCK - 1) & (c0 + CMP_LEN - 1 >= s0) & (np.arange(ncp)[:, None] < nc) & (np.arange(nsp)[None, :] < ns)
    cover_t = jnp.asarray(cover.T.astype(np.float32), dtype=BF16)
    nq = t // tq
    qspec = pl.BlockSpec((1, N_KV_HEADS, GROUP, tq, HEAD_DIM), lambda b, i: (b, 0, 0, i, 0))
    cspec = pl.BlockSpec((1, N_KV_HEADS, ncp, HEAD_DIM), lambda b, i: (b, 0, 0, 0))
    o_c, sel = pl.pallas_call(
        functools.partial(_nsa_cmp_kernel, tq=tq, pos0=pos0, nc=nc, n_sel=n_sel),
        grid=(bsz, nq),
        in_specs=[qspec, cspec, cspec, pl.BlockSpec((nsp, ncp), lambda b, i: (0, 0))],
        out_specs=[pl.BlockSpec((1, tq, Q_DIM), lambda b, i: (b, i, 0)),
                   pl.BlockSpec((1, 1, nsp, N_KV_HEADS * tq), lambda b, i: (b, i, 0, 0))],
        out_shape=[jax.ShapeDtypeStruct((bsz, t, Q_DIM), F32), jax.ShapeDtypeStruct((bsz, nq, nsp, N_KV_HEADS * tq), F32)],
        scratch_shapes=[pltpu.VMEM((nsp, N_KV_HEADS * tq), I32)],
        compiler_params=_params("parallel", "parallel"),
        name="nsa_cmp_select",
    )(q, cmp_k, cmp_v, cover_t)
    sel = sel.reshape(bsz, nq, nsp, N_KV_HEADS, tq).transpose(0, 3, 1, 4, 2).reshape(bsz, N_KV_HEADS, t, nsp)
    return o_c, sel


def _kmean_kernel(*refs):
    o_ref = refs[-1]
    tot = jnp.sum(refs[0][0], axis=0, keepdims=True)
    for r in refs[1:-1]:
        tot = tot + jnp.sum(r[0], axis=0, keepdims=True)
    o_ref[0, 0] = tot * (1.0 / MOBA_BLOCK)


def _kmean_prompt(kv):
    bsz, t, _ = kv.shape
    nb = t // MOBA_BLOCK
    return pl.pallas_call(
        _kmean_kernel,
        grid=(bsz, nb),
        in_specs=[pl.BlockSpec((1, MOBA_BLOCK, KV_COLS), lambda b, i: (b, i, 0))],
        out_specs=pl.BlockSpec((1, 1, 1, KV_COLS), lambda b, i: (b, i, 0, 0)),
        out_shape=jax.ShapeDtypeStruct((bsz, nb, 1, KV_COLS), F32),
        compiler_params=_params("parallel", "parallel"),
        name="kmean_prompt",
    )(kv)


def _kmean_sample_kernel(pt_ref, *refs, per, nb):
    o_ref = refs[-1]
    ones = jnp.ones((8, PAGE_SIZE), BF16)
    blocks = len(refs[:-1]) // (per * nb)
    for n in range(nb * blocks):
        tot = jnp.zeros((8, KV_COLS), F32)
        for r in refs[n * per:(n + 1) * per]:
            x = r[0, 0]
            hi = x.astype(BF16)
            r1 = x - hi.astype(F32)
            mid = r1.astype(BF16)
            lo = (r1 - mid.astype(F32)).astype(BF16)
            tot = tot + _dot_t(ones, hi) + _dot_t(ones, mid) + _dot_t(ones, lo)
        o_ref[n // blocks, n % blocks] = tot[0:1] * (1.0 / MOBA_BLOCK)


def _kmean_sample(page_table, pool_t, n_pg):
    bsz, n_pages = page_table.shape
    per = MOBA_BLOCK // PAGE_SIZE
    n_blocks = n_pages // per
    nb = _batch_rows(bsz)

    def page_map(bi, i):
        return lambda b, n, pt: (pt[b * nb + bi, n * n_pg + i], 0, 0, 0)

    grid_spec = pltpu.PrefetchScalarGridSpec(
        num_scalar_prefetch=1,
        grid=(bsz // nb, n_pages // n_pg),
        in_specs=[pl.BlockSpec((1, 1, KV_COLS, PAGE_SIZE), page_map(bi, i)) for bi in range(nb) for i in range(n_pg)],
        out_specs=pl.BlockSpec((nb, n_pg // per, 1, KV_COLS), lambda b, n, pt: (b, n, 0, 0)),
    )
    return pl.pallas_call(
        functools.partial(_kmean_sample_kernel, per=per, nb=nb),
        grid_spec=grid_spec,
        out_shape=jax.ShapeDtypeStruct((bsz, n_blocks, 1, KV_COLS), F32),
        compiler_params=_params("parallel", "parallel"),
        name="kmean_sample",
    )(page_table, *([pool_t] * (nb * n_pg)))


def _moba_select_kernel(q_ref, km_ref, sel_ref, *, tq, pos0, k_top):
    t0 = pos0 + pl.program_id(1) * tq
    nbp = km_ref.shape[2]
    rows = GROUP * tq
    blk = lax.broadcasted_iota(I32, (nbp, rows), 0)
    n_past = (t0 + (lax.broadcasted_iota(I32, (nbp, rows), 1) & (tq - 1))) // MOBA_BLOCK
    for g in range(N_KV_HEADS):
        q = q_ref[0, g].reshape(rows, HEAD_DIM)
        s = jnp.where(blk < n_past, _dot_t(km_ref[0, g], q), -jnp.inf)
        sel = jnp.zeros((nbp, rows), F32)
        for _ in range(k_top):
            m = jnp.max(s, axis=0, keepdims=True)
            first = jnp.min(jnp.where(s == m, blk, nbp), axis=0, keepdims=True)
            pick = blk == first
            sel = jnp.where(pick & (m > -jnp.inf), 1.0, sel)
            s = jnp.where(pick, -jnp.inf, s)
        sel_ref[0, 0, g] = sel


def _moba_select(q, kmean, pos0, nb, tq):
    bsz, _, _, t, _ = q.shape
    nbp = kmean.shape[2]
    nq = t // tq
    qspec = pl.BlockSpec((1, N_KV_HEADS, GROUP, tq, HEAD_DIM), lambda b, i: (b, 0, 0, i, 0))
    sel = pl.pallas_call(
        functools.partial(_moba_select_kernel, tq=tq, pos0=pos0, k_top=min(MOBA_TOPK, nb)),
        grid=(bsz, nq),
        in_specs=[qspec, pl.BlockSpec((1, N_KV_HEADS, nbp, HEAD_DIM), lambda b, i: (b, 0, 0, 0))],
        out_specs=pl.BlockSpec((1, 1, N_KV_HEADS, nbp, GROUP * tq), lambda b, i: (b, i, 0, 0, 0)),
        out_shape=jax.ShapeDtypeStruct((bsz, nq, N_KV_HEADS, nbp, GROUP * tq), F32),
        compiler_params=_params("parallel", "parallel"),
        name="moba_select",
    )(q, kmean)
    sel = sel.reshape(bsz, nq, N_KV_HEADS, nbp, GROUP, tq).transpose(0, 2, 4, 1, 5, 3)
    return sel.reshape(bsz, N_KV_HEADS, GROUP, t, nbp)


def _rope_tables(pos):
    half = HEAD_DIM // 2
    inv = ROPE_THETA ** (-jnp.arange(half, dtype=F32) / half)
    ang = pos.astype(F32)[:, None] * inv[None, :]
    cos, sin = jnp.cos(ang), jnp.sin(ang)
    rep = LANES // HEAD_DIM
    return jnp.concatenate([cos, cos] * rep, axis=1), jnp.concatenate([-sin, sin] * rep, axis=1)


def _pad_to(x, axis, size):
    pad = [(0, 0)] * x.ndim
    pad[axis] = (0, size - x.shape[axis])
    return jnp.pad(x, pad)


def _q_groups(q):
    b, t = q.shape[:2]
    return q.transpose(0, 2, 1, 3).reshape(b, N_KV_HEADS, GROUP, t, HEAD_DIM)


def _pages_t(kv):
    return kv.transpose(0, 2, 3, 4, 1).reshape(kv.shape[0], 2, KV_COLS, PAGE_SIZE)


def _fit(x, width):
    return x[..., :width] if x.shape[-1] >= width else _pad_to(x, x.ndim - 1, width)


_HEAD_TO_GROUP = np.equal(np.arange(N_HEADS)[:, None] // GROUP, np.arange(N_KV_HEADS)[None, :]).astype(np.float32)


def _q_block_diag(q):
    s, tn = q.shape[:2]
    qb = q[:, :, :, None, :] * jnp.asarray(_HEAD_TO_GROUP, dtype=q.dtype)[None, None, :, :, None]
    return qb.reshape(s, tn * N_HEADS, KV_COLS)


def _extract_block_diag(o, tn):
    s = o.shape[0]
    o5 = o.reshape(s, tn, N_HEADS, N_KV_HEADS, HEAD_DIM) * _HEAD_TO_GROUP[None, None, :, :, None]
    return o5.sum(axis=3).reshape(s * tn, Q_DIM)


def _new_page(kv_new):
    return _pages_t(_pad_to(kv_new, 1, PAGE_SIZE))


def _pick_pages(n_pages):
    for n in (8, 4, 2, 1):
        if n_pages % n == 0:
            return n


def _identity_pages(bsz, n_pages):
    return jnp.arange(bsz * n_pages, dtype=I32).reshape(bsz, n_pages)


def _rows_th(ok, s, tn):
    return jnp.where(ok, 0.0, NEG).astype(BF16).reshape(s, tn * N_HEADS, ok.shape[-1])


def _kv_stack(k, v, bsz, t):
    return jnp.stack([k.reshape(bsz, t, N_KV_HEADS, HEAD_DIM), v.reshape(bsz, t, N_KV_HEADS, HEAD_DIM)], axis=2)


def _kv_forms(c_k, rope):
    return [(c_k, KV_COLS, rope, 1.0, BF16, "keys_t"), (c_k + KV_COLS, KV_COLS, False, 1.0, BF16, "values_1")]


def _dsa_project(x, bsz, t, w_in, cs, q_scale, prompt):
    hm = "heads" if prompt else "tok"
    c_qi = Q_DIM + 2 * KV_COLS
    c_ki = c_qi + IDX_HEADS * IDX_DIM
    segs = [(0, Q_DIM, True, q_scale, BF16, hm), (Q_DIM, KV_COLS, True, 1.0, F32, "tok"),
            (Q_DIM + KV_COLS, KV_COLS, False, 1.0, F32, "tok"), (c_qi, IDX_HEADS * IDX_DIM, True, 1.0, BF16, hm),
            (c_ki, IDX_DIM, True, 1.0, F32, "tok"), (c_ki + IDX_DIM, IDX_HEADS, False, IDX_SCALE, F32, "tok")]
    if prompt:
        segs += _kv_forms(Q_DIM, True)
    q, k, v, qi, ki, wi, *flash_kv = _proj(x, w_in, cs, segs, bsz)
    if not prompt:
        q, qi = q.reshape(bsz, t, N_HEADS, HEAD_DIM), qi.reshape(bsz, t, IDX_HEADS, IDX_DIM)
    return q, qi, wi.reshape(bsz, t, IDX_HEADS), _kv_stack(k, v, bsz, t), ki.reshape(bsz, t, IDX_DIM), flash_kv


def _head_groups(q):
    return q.reshape(q.shape[0], N_KV_HEADS, GROUP, q.shape[2], HEAD_DIM)


def _dsa_prompt(x, bsz, t, w_in, cs):
    q, qi, wi, kv, ki, (k_t, v1) = _dsa_project(x, bsz, t, w_in, cs, ATTN_SCALE * LOG2E, True)
    mask = _dsa_select_prompt(qi, ki.astype(BF16), wi)
    o = _flash_prompt("dsa", _head_groups(q), k_t, v1, [mask])
    return o.reshape(bsz * t, Q_DIM), kv, ki


def _dsa_sample(x, s, tn, w_in, cs, page_table, cache_kv, cache_kidx, n_pg):
    q, qi, wi, kv, ki, _ = _dsa_project(x, s, tn, w_in, cs, ATTN_SCALE, False)
    tpad = 8
    qi_p = _pad_to(qi, 1, tpad).reshape(s, tpad * IDX_HEADS, IDX_DIM)
    wi_p = _pad_to(wi, 1, tpad).reshape(s, tpad * IDX_HEADS, 1)
    mask = _dsa_select_sample(page_table, qi_p, wi_p, cache_kidx.transpose(0, 2, 1),
                              _pad_to(ki, 1, PAGE_SIZE).transpose(0, 2, 1), tn, n_pg)
    ok = jnp.broadcast_to(mask[:, :tn, None, :] > 0.5, (s, tn, N_HEADS, mask.shape[-1]))
    o = _paged_flash(page_table, _q_block_diag(q), _pages_t(cache_kv), _new_page(kv), _rows_th(ok, s, tn), n_pg)
    return _extract_block_diag(o, tn), kv, ki


def _fox_project(x, bsz, t, w_in, b_f, cs, q_scale, prompt):
    segs = [(0, Q_DIM, False, q_scale, BF16, "heads" if prompt else "tok"), (Q_DIM, KV_COLS, False, 1.0, F32, "tok"),
            (Q_DIM + KV_COLS, KV_COLS, False, 1.0, F32, "tok"), (Q_DIM + 2 * KV_COLS, N_HEADS, False, 1.0, F32, "tok")]
    if prompt:
        segs += _kv_forms(Q_DIM, False)
    q, k, v, f, *flash_kv = _proj(x, w_in, cs, segs, bsz)
    logf = jax.nn.log_sigmoid(f.reshape(bsz, t, N_HEADS) + b_f)
    if not prompt:
        q = q.reshape(bsz, t, N_HEADS, HEAD_DIM)
    return q, _kv_stack(k, v, bsz, t), logf, flash_kv


def _fox_prompt(x, bsz, t, w_in, b_f, cs):
    q, kv, logf, (k_t, v1) = _fox_project(x, bsz, t, w_in, b_f, cs, ATTN_SCALE * LOG2E, True)
    n_pages = t // LANES
    pool_t = logf.reshape(bsz, n_pages, LANES, N_HEADS).transpose(0, 1, 3, 2).reshape(bsz * n_pages, N_HEADS, LANES)
    c = _paged_cumsum(_identity_pages(bsz, n_pages), pool_t, jnp.zeros((bsz, N_HEADS, LANES), F32), _pick_pages(n_pages))
    c = (c[:, :, :t] * LOG2E).reshape(bsz, N_KV_HEADS, GROUP, t)
    o = _flash_prompt("fox", _head_groups(q), k_t, v1, [c])
    return o.reshape(bsz * t, Q_DIM), kv, logf


def _fox_sample(x, s, tn, w_in, b_f, cs, page_table, cache_kv, cache_logf, n_pg):
    q, kv, logf, _ = _fox_project(x, s, tn, w_in, b_f, cs, ATTN_SCALE, False)
    past = page_table.shape[1] * PAGE_SIZE
    c = _paged_cumsum(page_table, cache_logf.transpose(0, 2, 1), _pad_to(logf.transpose(0, 2, 1), 2, LANES), n_pg)
    col = jnp.arange(c.shape[-1])
    valid = (col[None, :] < past) | ((col[None, :] - past <= jnp.arange(tn)[:, None]) & (col[None, :] < past + tn))
    bias = jnp.where(valid[None, :, None, :], -c[:, None, :, :], NEG).reshape(s, tn * N_HEADS, c.shape[-1])
    o = _paged_flash(page_table, _q_block_diag(q), _pages_t(cache_kv), _new_page(kv), bias, n_pg)
    return _extract_block_diag(o, tn), kv, logf


def _nsa_project(x, bsz, t, w_in, b_gate, cs, q_scale, prompt):
    hm = "heads" if prompt else "tok"
    c = Q_DIM
    segs = [(0, Q_DIM, False, ATTN_SCALE, BF16, hm), (0, Q_DIM, True, q_scale, BF16, hm)]
    for rope in (False, False, True, False, True, False):
        segs.append((c, KV_COLS, rope, 1.0, F32, "tok"))
        c += KV_COLS
    segs.append((c, 3 * N_HEADS, False, 1.0, F32, "tok"))
    if prompt:
        segs += _kv_forms(Q_DIM + 2 * KV_COLS, True) + _kv_forms(Q_DIM + 4 * KV_COLS, True)
    q, q_rot, kc, vc, ks, vs, kw, vw, g, *flash_kv = _proj(x, w_in, cs, segs, bsz)
    gate = jax.nn.sigmoid(g.reshape(bsz, t, 3 * N_HEADS) + b_gate).reshape(bsz, t, 3, N_HEADS)
    if not prompt:
        q, q_rot = q.reshape(bsz, t, N_HEADS, HEAD_DIM), q_rot.reshape(bsz, t, N_HEADS, HEAD_DIM)
    return (q, q_rot, gate, _kv_stack(kc, vc, bsz, t), _kv_stack(ks, vs, bsz, t), _kv_stack(kw, vw, bsz, t), flash_kv)


def _nsa_weights(pe, w1, w2):
    r = CMP_LEN // CMP_STRIDE
    w1cat = w1.reshape(2, r, CMP_STRIDE * HEAD_DIM, CMP_HID).transpose(0, 2, 1, 3).reshape(2, CMP_STRIDE * HEAD_DIM, r * CMP_HID)
    pe8 = jnp.broadcast_to(pe.reshape(2, 1, CMP_LEN * HEAD_DIM), (2, 8, CMP_LEN * HEAD_DIM))
    return pe8.astype(BF16), w1cat.astype(BF16), w2.astype(BF16)


def _chunk_pages(kv):
    n = kv.shape[0]
    per = PAGE_SIZE // CMP_STRIDE
    x = kv.reshape(n, per, CMP_STRIDE, 2 * N_KV_HEADS, HEAD_DIM).transpose(0, 3, 1, 2, 4)
    return x.reshape(n, 2 * N_KV_HEADS, per, CMP_STRIDE * HEAD_DIM).astype(BF16)


def _gate_mix(gate, o_c, o_s, o_w):
    m = gate.shape[0] * gate.shape[1]
    return (gate.reshape(m, 3 * N_HEADS), o_c.reshape(m, Q_DIM), o_s.reshape(m, Q_DIM), o_w.reshape(m, Q_DIM))


def _nsa_prompt(x, bsz, t, w_in, b_gate, cmp_w, cs, tq=128, tk=512):
    q, q_rot, gate, kv_cmp, kv_slc, kv_win, (ks_t, vs1, kw_t, vw1) = _nsa_project(x, bsz, t, w_in, b_gate, cs,
                                                                                  ATTN_SCALE * LOG2E, True)
    n_pages = t // PAGE_SIZE
    cmp, nc = _nsa_compress(_identity_pages(bsz, n_pages), _chunk_pages(kv_cmp.reshape(bsz * n_pages, PAGE_SIZE, 2, N_KV_HEADS, HEAD_DIM)),
                            *cmp_w, n_pg=_pick_pages(n_pages))
    cmp = cmp.astype(BF16)
    tq, tk = min(tq, t), min(tk, t)
    o_c, selblk = _nsa_cmp_select(_head_groups(q), cmp[:, :N_KV_HEADS], cmp[:, N_KV_HEADS:], 0, nc, t, tq)
    qg = _head_groups(q_rot)
    e3 = _expand_matrix(selblk.shape[-1], SEL_BLOCK, t, tk)
    o_s = _flash_prompt("nsa_sel", qg, ks_t, vs1, [selblk, e3], tq=2 * tq, tk=tk)
    o_w = _flash_prompt("nsa_win", qg, kw_t, vw1, [], tk=tk)
    return _gate_mix(gate, o_c, o_s, o_w), kv_cmp, kv_slc, kv_win[:, -min(WINDOW, t):]


def _nsa_sample(x, s, tn, w_in, b_gate, cmp_w, cs, page_table, cache_cmp, cache_slc, state_win, n_pg):
    q, q_rot, gate, kv_cmp, kv_slc, kv_win, _ = _nsa_project(x, s, tn, w_in, b_gate, cs, ATTN_SCALE, False)
    past = page_table.shape[1] * PAGE_SIZE
    tpad = 8
    cmp, nc = _nsa_compress(page_table, _chunk_pages(cache_cmp), *cmp_w, n_pg=n_pg)
    cmp = cmp.astype(BF16)
    o_c, selblk = _nsa_cmp_select(_q_groups(_pad_to(q, 1, tpad)), cmp[:, :N_KV_HEADS], cmp[:, N_KV_HEADS:],
                                  past, nc, past + tn, tpad)
    o_c = o_c[:, :tn]
    qbd = _q_block_diag(q_rot)
    width = (page_table.shape[1] // n_pg + 1) * n_pg * LANES
    col = jnp.arange(width)
    pos = past + jnp.arange(tn)
    sel_key = jnp.repeat(selblk[:, :, :tn, :-(-width // SEL_BLOCK)] > 0.5, SEL_BLOCK, axis=-1)[..., :width]
    ok = sel_key & (col[None, :] <= pos[:, None])[None, None]
    ok = jnp.broadcast_to(ok.transpose(0, 2, 1, 3)[:, :, :, None, :], (s, tn, N_KV_HEADS, GROUP, width))
    o_s = _paged_flash(page_table, qbd, _pages_t(cache_slc), _new_page(kv_slc),
                       _rows_th(ok.reshape(s, tn, N_HEADS, width), s, tn), n_pg)
    win_buf = state_win.shape[1]
    n_wp = win_buf // PAGE_SIZE
    wcol = jnp.arange(2 * n_wp * LANES)
    win_pos = jnp.where(wcol < win_buf, past - win_buf + wcol, jnp.where(wcol < win_buf + tn, past + wcol - win_buf, -1))
    w_ok = (win_pos[None, :] <= pos[:, None]) & (win_pos[None, :] >= pos[:, None] - WINDOW) & (win_pos[None, :] >= 0)
    w_ok = jnp.broadcast_to(w_ok[None, :, None, :], (s, tn, N_HEADS, wcol.shape[0]))
    win_pages = _pages_t(state_win.reshape(s * n_wp, PAGE_SIZE, 2, N_KV_HEADS, HEAD_DIM))
    o_w = _paged_flash(_identity_pages(s, n_wp), qbd, win_pages, _new_page(kv_win), _rows_th(w_ok, s, tn), n_wp)
    unbd = lambda o: _extract_block_diag(o, tn).reshape(s, tn, Q_DIM)
    win = jnp.concatenate([state_win, kv_win], axis=1)[:, -win_buf:]
    return _gate_mix(gate, o_c, unbd(o_s), unbd(o_w)), kv_cmp, kv_slc, win


def _moba_project(x, bsz, t, w_in, cs, q_scale, prompt):
    segs = [(0, Q_DIM, True, q_scale, BF16, "heads" if prompt else "tok"), (Q_DIM, KV_COLS, True, 1.0, F32, "tok"),
            (Q_DIM + KV_COLS, KV_COLS, False, 1.0, F32, "tok")]
    if prompt:
        segs += _kv_forms(Q_DIM, True)
    q, k, v, *flash_kv = _proj(x, w_in, cs, segs, bsz)
    if not prompt:
        q = q.reshape(bsz, t, N_HEADS, HEAD_DIM)
    return q, _kv_stack(k, v, bsz, t), flash_kv


def _kmean_heads(km):
    b, nb = km.shape[:2]
    return _pad_to(km.reshape(b, nb, N_KV_HEADS, HEAD_DIM).transpose(0, 2, 1, 3), 2, -(-nb // 16) * 16).astype(BF16)


def _moba_prompt(x, bsz, t, w_in, cs, tq=128, tk=512):
    q, kv, (k_t, v1) = _moba_project(x, bsz, t, w_in, cs, ATTN_SCALE * LOG2E, True)
    tq, tk = min(tq, t), min(tk, t)
    nb = -(-t // MOBA_BLOCK)
    km = _kmean_heads(_kmean_prompt(kv.reshape(bsz, t, 2 * KV_COLS)))
    qg = _head_groups(q)
    sel = _moba_select(qg, km, 0, nb, tq)
    e3 = _expand_matrix(km.shape[2], MOBA_BLOCK, t, tk)
    o = _flash_prompt("moba", qg, k_t, v1, [sel, e3], tq=2 * tq, tk=tk)
    return o.reshape(bsz * t, Q_DIM), kv


def _moba_sample(x, s, tn, w_in, cs, page_table, cache_kv, n_pg):
    q, kv, _ = _moba_project(x, s, tn, w_in, cs, ATTN_SCALE, False)
    past = page_table.shape[1] * PAGE_SIZE
    tpad = 8
    pool = _pages_t(cache_kv)
    nb = -(-(past + tn) // MOBA_BLOCK)
    km = _kmean_heads(_kmean_sample(page_table, pool, n_pg))
    sel = _moba_select(_q_groups(_pad_to(q, 1, tpad)), km, past, nb, tpad)
    width = (page_table.shape[1] // n_pg + 1) * n_pg * LANES
    col = jnp.arange(width)
    pos = past + jnp.arange(tn)
    sel_key = _fit(jnp.repeat(sel[:, :, :, :tn] > 0.5, MOBA_BLOCK, axis=-1), width)
    own = (col[None, :] // MOBA_BLOCK == pos[:, None] // MOBA_BLOCK) & (col[None, :] <= pos[:, None])
    ok = sel_key | own[None, None, None]
    ok = ok.transpose(0, 3, 1, 2, 4).reshape(s, tn, N_HEADS, width)
    o = _paged_flash(page_table, _q_block_diag(q), pool, _new_page(kv), _rows_th(ok, s, tn), n_pg)
    return _extract_block_diag(o, tn), kv


def _cast_w(w):
    return _pad_to(w, 1, -(-w.shape[1] // LANES) * LANES).astype(BF16)


def kernel(x_prompt, x_sample, cache_a_kv, cache_a_kidx, cache_b_kv, cache_b_logf, cache_c_cmp_kv, cache_c_slc_kv, state_c_win_kv, cache_d_kv, page_table, a_w_in, a_w_out, b_w_in, b_b_f, b_w_out, c_w_in, c_b_gate, c_cmp_pe, c_cmp_w1, c_cmp_w2, c_w_out, d_w_in, d_w_out, ln_g, ln_b, ffn_w_gu, ffn_w_down):
    bsz, t, d = x_prompt.shape
    s, tn, _ = x_sample.shape
    n_pages = page_table.shape[1]
    past = n_pages * PAGE_SIZE
    n_pg = _pick_pages(n_pages)
    cs_p = _rope_tables(jnp.arange(t, dtype=I32))
    cs_s = _rope_tables(jnp.tile(past + jnp.arange(tn, dtype=I32), s))
    xp = x_prompt.reshape(bsz * t, d)
    xs = x_sample.reshape(s * tn, d)
    cmp_w = _nsa_weights(c_cmp_pe, c_cmp_w1, c_cmp_w2)
    w_out = [_cast_w(w) for w in (a_w_out, b_w_out, c_w_out, d_w_out)]

    op, a_kv_p, a_kidx_p = _dsa_prompt(xp, bsz, t, _cast_w(a_w_in), cs_p)
    os_, a_kv_s, a_kidx_s = _dsa_sample(xs, s, tn, _cast_w(a_w_in), cs_s, page_table, cache_a_kv, cache_a_kidx, n_pg)

    def finish(i, xp, xs, op, os_):
        xp = _out_ln(op, w_out[i], xp, ln_g[i, 0], ln_b[i, 0])
        xs = _out_ln(os_, w_out[i], xs, ln_g[i, 0], ln_b[i, 0])
        wgu, wd = ffn_w_gu[i].astype(BF16), ffn_w_down[i].astype(BF16)
        xp = _ffn_ln(xp, wgu, wd, ln_g[i, 1], ln_b[i, 1])
        xs = _ffn_ln(xs, wgu, wd, ln_g[i, 1], ln_b[i, 1])
        return xp, xs

    xp, xs = finish(0, xp, xs, op, os_)

    op, b_kv_p, b_logf_p = _fox_prompt(xp, bsz, t, _cast_w(b_w_in), b_b_f, cs_p)
    os_, b_kv_s, b_logf_s = _fox_sample(xs, s, tn, _cast_w(b_w_in), b_b_f, cs_s, page_table, cache_b_kv, cache_b_logf, n_pg)
    xp, xs = finish(1, xp, xs, op, os_)

    op, c_cmp_kv_p, c_slc_kv_p, c_win_kv_p = _nsa_prompt(xp, bsz, t, _cast_w(c_w_in), c_b_gate, cmp_w, cs_p)
    os_, c_cmp_kv_s, c_slc_kv_s, c_win_kv_s = _nsa_sample(xs, s, tn, _cast_w(c_w_in), c_b_gate, cmp_w, cs_s, page_table,
                                                          cache_c_cmp_kv, cache_c_slc_kv, state_c_win_kv, n_pg)
    xp, xs = finish(2, xp, xs, op, os_)

    op, d_kv_p = _moba_prompt(xp, bsz, t, _cast_w(d_w_in), cs_p)
    os_, d_kv_s = _moba_sample(xs, s, tn, _cast_w(d_w_in), cs_s, page_table, cache_d_kv, n_pg)
    xp, xs = finish(3, xp, xs, op, os_)

    return (xp.reshape(bsz, t, d), xs.reshape(s, tn, d), a_kv_p, a_kv_s, a_kidx_p, a_kidx_s, b_kv_p, b_kv_s,
            b_logf_p, b_logf_s, c_cmp_kv_p, c_cmp_kv_s, c_slc_kv_p, c_slc_kv_s, c_win_kv_p, c_win_kv_s, d_kv_p, d_kv_s)
```

```python
import functools

import numpy as np
import jax
import jax.numpy as jnp
from jax import lax
from jax.experimental import pallas as pl
from jax.experimental.pallas import tpu as pltpu

F32 = jnp.float32
BF16 = jnp.bfloat16
I32 = jnp.int32

N_HEADS = 16
HEAD_DIM = 64
N_KV_HEADS = 4
GROUP = N_HEADS // N_KV_HEADS
Q_DIM = N_HEADS * HEAD_DIM
KV_COLS = N_KV_HEADS * HEAD_DIM
DEPTH = 4
PAGE_SIZE = 128
ROPE_THETA = 10000.0
LN_EPS = 1e-5
ALPHA = (2 * DEPTH) ** 0.25
ATTN_SCALE = HEAD_DIM ** -0.5
LOG2E = 1.4426950408889634
IDX_HEADS = 8
IDX_DIM = 64
IDX_TOPK = 256
IDX_SCALE = (IDX_HEADS * IDX_DIM) ** -0.5
CMP_LEN = 32
CMP_STRIDE = 16
CMP_HID = 2 * HEAD_DIM
SEL_BLOCK = 64
SEL_TOPN = 16
WINDOW = 512
MOBA_BLOCK = 256
MOBA_TOPK = 3

LANES = 128
VMEM_LIMIT = 56 * 2 ** 20
NEG = -1e30
KEY_NEG_INF = -2139095041
KEY_POS_INF = 2139095040
INT_MIN = -2 ** 31


def _params(*sem):
    return pltpu.CompilerParams(dimension_semantics=sem, vmem_limit_bytes=VMEM_LIMIT)


def _dot_t(a, b):
    return lax.dot_general(a, b, (((1,), (1,)), ((), ())), preferred_element_type=F32)


def _dot(a, b):
    return jnp.dot(a, b, preferred_element_type=F32)


def _dot_hp(a, b):
    hi = a.astype(BF16)
    r1 = a - hi.astype(F32)
    mid = r1.astype(BF16)
    lo = (r1 - mid.astype(F32)).astype(BF16)
    return _dot(hi, b) + _dot(mid, b) + _dot(lo, b)


def _proj_kernel(x_ref, w_ref, cos_ref, sin_ref, *out_refs, segs):
    acc = _dot(x_ref[...].astype(BF16), w_ref[...])
    tm = acc.shape[0]
    lane = lax.broadcasted_iota(I32, (tm, LANES), 1)
    first_half = (lane & (HEAD_DIM - 1)) < HEAD_DIM // 2
    ones_col = jnp.where(lane == HEAD_DIM, 1.0, 0.0)
    for (c0, width, rope, scale, form), o_ref in zip(segs, out_refs):
        if c0 % LANES:
            o_ref[...] = (acc[:, c0:c0 + width] * scale).astype(o_ref.dtype)
            continue
        for j in range(-(-width // LANES)):
            x = acc[:, c0 + j * LANES:c0 + (j + 1) * LANES]
            if j * LANES < (width if rope is True else int(rope)):
                swapped = jnp.where(first_half, pltpu.roll(x, LANES - HEAD_DIM // 2, 1), pltpu.roll(x, HEAD_DIM // 2, 1))
                x = x * cos_ref[...] + swapped * sin_ref[...]
            if scale != 1.0:
                x = x * scale
            if form == "tok":
                wj = min(LANES, width - j * LANES)
                o_ref[:, j * LANES:j * LANES + wj] = x[:, :wj].astype(o_ref.dtype)
            elif form == "heads":
                o_ref[0, 2 * j] = x[:, :HEAD_DIM].astype(o_ref.dtype)
                o_ref[0, 2 * j + 1] = x[:, HEAD_DIM:].astype(o_ref.dtype)
            elif form == "keys_t":
                xt = x.T
                o_ref[0, 2 * j] = xt[:HEAD_DIM].astype(o_ref.dtype)
                o_ref[0, 2 * j + 1] = xt[HEAD_DIM:].astype(o_ref.dtype)
            else:
                o_ref[0, 2 * j] = jnp.where(lane < HEAD_DIM, x, ones_col).astype(o_ref.dtype)
                o_ref[0, 2 * j + 1] = jnp.where(lane < HEAD_DIM, pltpu.roll(x, HEAD_DIM, 1), ones_col).astype(o_ref.dtype)


def _proj(x, w, cs, segs, bsz=None):
    m, k = x.shape
    n = w.shape[1]
    tm = min(m, 512)
    cos_t, sin_t = cs
    r_blocks = cos_t.shape[0] // tm
    tab = pl.BlockSpec((tm, LANES), lambda i: (i % r_blocks, 0))
    specs, shapes = [], []
    for _, width, _, _, dt, form in segs:
        nh = width // HEAD_DIM
        if form == "tok":
            specs.append(pl.BlockSpec((tm, width), lambda i: (i, 0)))
            shapes.append(jax.ShapeDtypeStruct((m, width), dt))
            continue
        t = m // bsz
        tpb = t // tm
        if form == "keys_t":
            specs.append(pl.BlockSpec((1, nh, HEAD_DIM, tm), lambda i: (i // tpb, 0, 0, i % tpb)))
            shapes.append(jax.ShapeDtypeStruct((bsz, nh, HEAD_DIM, t), dt))
        else:
            last = HEAD_DIM if form == "heads" else LANES
            specs.append(pl.BlockSpec((1, nh, tm, last), lambda i: (i // tpb, 0, i % tpb, 0)))
            shapes.append(jax.ShapeDtypeStruct((bsz, nh, t, last), dt))
    return pl.pallas_call(
        functools.partial(_proj_kernel, segs=tuple(s[:4] + (s[5],) for s in segs)),
        grid=(m // tm,),
        in_specs=[pl.BlockSpec((tm, k), lambda i: (i, 0)), pl.BlockSpec((k, n), lambda i: (0, 0)), tab, tab],
        out_specs=specs,
        out_shape=shapes,
        compiler_params=_params("parallel"),
        name="in_proj",
    )(x, w, cos_t, sin_t)


def _layer_norm(y, g, b):
    mu = jnp.mean(y, axis=-1, keepdims=True)
    d = y - mu
    var = jnp.mean(d * d, axis=-1, keepdims=True)
    return d * lax.rsqrt(var + LN_EPS) * g + b


def _out_ln_kernel(o_ref, w_ref, x_ref, g_ref, b_ref, y_ref):
    y = ALPHA * x_ref[...] + _dot(o_ref[...].astype(BF16), w_ref[...])
    y_ref[...] = _layer_norm(y, g_ref[...], b_ref[...])


def _out_ln_gated_kernel(gate_ref, oc_ref, os_ref, ow_ref, w_ref, x_ref, g_ref, b_ref, y_ref):
    gate = gate_ref[...]
    heads = []
    for h in range(N_HEADS):
        cols = slice(h * HEAD_DIM, (h + 1) * HEAD_DIM)
        heads.append(gate[:, h:h + 1] * oc_ref[:, cols] + gate[:, N_HEADS + h:N_HEADS + h + 1] * os_ref[:, cols]
                     + gate[:, 2 * N_HEADS + h:2 * N_HEADS + h + 1] * ow_ref[:, cols])
    o = jnp.concatenate(heads, axis=1)
    y = ALPHA * x_ref[...] + _dot(o.astype(BF16), w_ref[...])
    y_ref[...] = _layer_norm(y, g_ref[...], b_ref[...])


def _out_ln(o, w, x, g, b):
    m, d = x.shape
    tm = min(m, 512)
    row = lambda i: (i, 0)
    fix = lambda i: (0, 0)
    gated = isinstance(o, tuple)
    o_in = list(o) if gated else [o]
    return pl.pallas_call(
        _out_ln_gated_kernel if gated else _out_ln_kernel,
        grid=(m // tm,),
        in_specs=[pl.BlockSpec((tm, a.shape[1]), row) for a in o_in]
        + [pl.BlockSpec(w.shape, fix), pl.BlockSpec((tm, d), row), pl.BlockSpec((1, d), fix), pl.BlockSpec((1, d), fix)],
        out_specs=pl.BlockSpec((tm, d), row),
        out_shape=jax.ShapeDtypeStruct((m, d), F32),
        compiler_params=_params("parallel"),
        name="out_proj_ln",
    )(*o_in, w, x, g.reshape(1, d), b.reshape(1, d))


def _ffn_ln_kernel(x_ref, wgu_ref, wd_ref, g_ref, b_ref, y_ref, *, d_ff, chunk):
    x = x_ref[...]
    xb = x.astype(BF16)
    acc = jnp.zeros(x.shape, F32)
    for c in range(d_ff // chunk):
        gate = _dot(xb, wgu_ref[:, c * chunk:(c + 1) * chunk])
        up = _dot(xb, wgu_ref[:, d_ff + c * chunk:d_ff + (c + 1) * chunk])
        h = gate * (1.0 / (1.0 + jnp.exp(-gate))) * up
        acc = acc + _dot(h.astype(BF16), wd_ref[c * chunk:(c + 1) * chunk, :])
    y_ref[...] = _layer_norm(ALPHA * x + acc, g_ref[...], b_ref[...])


def _ffn_ln(x, wgu, wd, g, b):
    m, d = x.shape
    d_ff = wd.shape[0]
    tm = min(m, 256)
    row = lambda i: (i, 0)
    fix = lambda i: (0, 0)
    return pl.pallas_call(
        functools.partial(_ffn_ln_kernel, d_ff=d_ff, chunk=256),
        grid=(m // tm,),
        in_specs=[pl.BlockSpec((tm, d), row), pl.BlockSpec((d, 2 * d_ff), fix), pl.BlockSpec((d_ff, d), fix),
                  pl.BlockSpec((1, d), fix), pl.BlockSpec((1, d), fix)],
        out_specs=pl.BlockSpec((tm, d), row),
        out_shape=jax.ShapeDtypeStruct((m, d), F32),
        compiler_params=_params("parallel"),
        name="ffn_ln",
    )(x, wgu, wd, g.reshape(1, d), b.reshape(1, d))


def _sortable(x):
    x = jnp.where(x == 0.0, 0.0, x)
    b = lax.bitcast_convert_type(x, I32)
    return b ^ ((b >> 31) & I32(0x7FFFFFFF))


def _kth_largest_key(u_ref, k):
    rows = u_ref.shape[0]

    def count_ge(cand):
        return jnp.sum((u_ref[...] >= cand).astype(I32), axis=1, keepdims=True)

    base = jnp.where(count_ge(jnp.zeros((rows, 1), I32)) >= k, I32(0), I32(INT_MIN))

    def body(i, base):
        cand = base | jnp.left_shift(I32(1), 30 - i)
        return jnp.where(count_ge(cand) >= k, cand, base)

    return lax.fori_loop(0, 31, body, base)


def _emit_selection(u_ref, k, write):
    rows, n = u_ref.shape
    thr = _kth_largest_key(u_ref, k)
    n_gt = jnp.sum((u_ref[...] > thr).astype(I32), axis=1, keepdims=True)
    need = (k - n_gt).astype(F32)
    r_i = lax.broadcasted_iota(I32, (LANES, LANES), 0)
    c_i = lax.broadcasted_iota(I32, (LANES, LANES), 1)
    tri = jnp.where(r_i <= c_i, 1.0, 0.0).astype(BF16)
    carry = jnp.zeros((rows, 1), F32)
    for ci in range(n // LANES):
        u = u_ref[:, ci * LANES:(ci + 1) * LANES]
        tie = u == thr
        tie_f = jnp.where(tie, 1.0, 0.0)
        inc = _dot(tie_f.astype(BF16), tri)
        rank = carry + inc - tie_f
        sel = (u > thr) | (tie & (rank < need))
        write(ci, sel, u)
        carry = carry + inc[:, LANES - 1:LANES]


def _topk_mask_cols(u_ref, k):
    n, r = u_ref.shape

    def count(pred):
        return jnp.sum(pred(u_ref[...]).astype(I32), axis=0, keepdims=True)

    base = jnp.where(count(lambda u: u >= 0) >= k, I32(0), I32(INT_MIN))

    def radix(i, base):
        cand = base | jnp.left_shift(I32(1), 30 - i)
        return jnp.where(count(lambda u: u >= cand) >= k, cand, base)

    thr = lax.fori_loop(0, 31, radix, base)
    need = (k - count(lambda u: u > thr)).astype(F32)
    u = u_ref[...]
    tie = u == thr
    lower = jnp.where(lax.broadcasted_iota(I32, (n, n), 0) > lax.broadcasted_iota(I32, (n, n), 1), 1.0, 0.0)
    rank = _dot(lower.astype(BF16), jnp.where(tie, 1.0, 0.0).astype(BF16))
    return (u > thr) | (tie & (rank < need)), u


def _flash_prompt_kernel(*refs, kind, tq, tk, n_extra):
    q_ref, k_ref, v_ref = refs[:3]
    extra = refs[3:3 + n_extra]
    o_ref, m_ref, acc_ref, s_ref = refs[3 + n_extra:]
    g = pl.program_id(1)
    q0 = pl.program_id(2) * tq
    rows = GROUP * tq
    q = q_ref[0, 0].reshape(rows, HEAD_DIM)
    m_ref[...] = jnp.full(m_ref.shape, NEG, F32)
    acc_ref[...] = jnp.zeros(acc_ref.shape, F32)
    c_diag = q0 // tk
    c_lo = jnp.maximum(q0 - WINDOW, 0) // tk if kind == "nsa_win" else 0
    if kind == "nsa_sel":
        selb = extra[0][0, 0].astype(BF16)
    if kind == "moba":
        selb = extra[0][0, 0].reshape(rows, extra[0].shape[-1]).astype(BF16)

    def scores(c):
        return _dot(q, k_ref[0, g, :, pl.ds(pl.multiple_of(c * tk, tk), tk)])

    def chunk(c, diag):
        if kind in ("nsa_sel", "moba"):
            hit = _dot(selb, extra[1][c]) > 0.5
        s3 = s_ref[c % 2].reshape(GROUP, tq, tk)
        if not diag:
            s_ref[(c + 1) % 2] = scores(c + 1)
        start = pl.multiple_of(c * tk, tk)
        v = v_ref[0, g, pl.ds(start, tk), :]
        ok = None
        if diag or kind == "nsa_win":
            t_idx = q0 + lax.broadcasted_iota(I32, (GROUP, tq, tk), 1)
            s_idx = start + lax.broadcasted_iota(I32, (GROUP, tq, tk), 2)
            causal = s_idx <= t_idx
        if kind == "fox":
            ck = extra[0][0, 0, :, pl.ds(start, tk)]
            s3 = s3 - ck[:, None, :]
            ok = causal if diag else None
        elif kind == "dsa":
            msk = extra[0][0, :, pl.ds(start, tk)]
            ok = jnp.broadcast_to((msk > 0)[None], (GROUP, tq, tk))
        elif kind == "nsa_sel":
            ok = jnp.broadcast_to(hit[None], (GROUP, tq, tk))
            ok = (ok & causal) if diag else ok
        elif kind == "nsa_win":
            ok = causal & (s_idx >= t_idx - WINDOW)
        else:
            ok = hit.reshape(GROUP, tq, tk)
            if diag:
                ok = ok | (causal & ((s_idx // MOBA_BLOCK) == (t_idx // MOBA_BLOCK)))
        if ok is not None:
            s3 = jnp.where(ok, s3, NEG)
        s = s3.reshape(rows, tk)
        m_prev = m_ref[...]
        m_new = jnp.maximum(m_prev, jnp.max(s, axis=1, keepdims=True))
        alpha = jnp.exp2(m_prev - m_new)
        p = jnp.exp2(s - pltpu.repeat(m_new, tk // LANES, axis=1))
        acc_ref[...] = acc_ref[...] * alpha + _dot(p.astype(BF16), v)
        m_ref[...] = m_new

    def body(c, carry):
        chunk(c, False)
        return carry

    s_ref[c_lo % 2] = scores(c_lo)
    lax.fori_loop(c_lo, c_diag, body, 0)
    chunk(c_diag, True)
    acc = acc_ref[...]
    o = acc[:, :HEAD_DIM] / jnp.maximum(acc[:, HEAD_DIM:HEAD_DIM + 1], 1e-30)
    o_ref[0] = jnp.concatenate([o[j * tq:(j + 1) * tq] for j in range(GROUP)], axis=1)


def _flash_prompt(kind, q, k, v, extra, tq=128, tk=512):
    bsz, _, _, t, _ = q.shape
    tk = min(tk, t)
    tq = min(tq, t)
    qspec = pl.BlockSpec((1, 1, GROUP, tq, HEAD_DIM), lambda b, g, i: (b, g, 0, i, 0))
    kvspec = pl.BlockSpec((1, N_KV_HEADS, HEAD_DIM, t), lambda b, g, i: (b, 0, 0, 0))
    vspec = pl.BlockSpec((1, N_KV_HEADS, t, LANES), lambda b, g, i: (b, 0, 0, 0))
    if kind == "fox":
        especs = [pl.BlockSpec((1, 1, GROUP, t), lambda b, g, i: (b, g, 0, 0))]
    elif kind == "dsa":
        especs = [pl.BlockSpec((1, tq, t), lambda b, g, i: (b, i, 0))]
    elif kind == "nsa_sel":
        especs = [pl.BlockSpec((1, 1, tq, extra[0].shape[-1]), lambda b, g, i: (b, g, i, 0)),
                  pl.BlockSpec(extra[1].shape, lambda b, g, i: (0, 0, 0))]
    elif kind == "moba":
        especs = [pl.BlockSpec((1, 1, GROUP, tq, extra[0].shape[-1]), lambda b, g, i: (b, g, 0, i, 0)),
                  pl.BlockSpec(extra[1].shape, lambda b, g, i: (0, 0, 0))]
    else:
        especs = []
    rows = GROUP * tq
    return pl.pallas_call(
        functools.partial(_flash_prompt_kernel, kind=kind, tq=tq, tk=tk, n_extra=len(extra)),
        grid=(bsz, N_KV_HEADS, t // tq),
        in_specs=[qspec, kvspec, vspec] + especs,
        out_specs=pl.BlockSpec((1, tq, GROUP * HEAD_DIM), lambda b, g, i: (b, i, g)),
        out_shape=jax.ShapeDtypeStruct((bsz, t, Q_DIM), F32),
        scratch_shapes=[pltpu.VMEM((rows, LANES), F32), pltpu.VMEM((rows, LANES), F32),
                        pltpu.VMEM((2, rows, tk), F32)],
        compiler_params=_params("parallel", "parallel", "parallel"),
        name="flash_prompt_" + kind,
    )(q, k, v, *extra)


def _expand_matrix(n_blocks_padded, block, t, tk):
    s = np.arange(t)
    e = (s[None, :] // block == np.arange(n_blocks_padded)[:, None]).astype(np.float32)
    e = e.reshape(n_blocks_padded, t // tk, tk).transpose(1, 0, 2)
    return jnp.asarray(e, dtype=BF16)


def _dsa_select_prompt_kernel(qi_ref, ki_ref, wi_ref, mask_ref, u_ref, *, tq, t, tk, topk):
    q0 = pl.program_id(1) * tq
    n_act = q0 // tk + 1
    sub = tk // LANES
    w = wi_ref[0]
    t_idx = q0 + lax.broadcasted_iota(I32, (tq, tk), 0)
    k_off = lax.broadcasted_iota(I32, (tq, tk), 1)

    def chunk_at(c):
        return pl.ds(pl.multiple_of(c * tk, tk), tk)

    def fill(c, carry):
        kc = ki_ref[0, chunk_at(c), :]
        s = jnp.zeros((tq, tk), F32)
        for h in range(IDX_HEADS):
            s = s + jnp.maximum(_dot_t(qi_ref[0, h], kc), 0.0) * w[:, h:h + 1]
        s = jnp.where(c * tk + k_off <= t_idx, s, -jnp.inf)
        u_ref[:, chunk_at(c)] = _sortable(s)
        return carry

    lax.fori_loop(0, n_act, fill, 0)

    def count(pred):
        def body(c, acc):
            hit = pred(u_ref[:, chunk_at(c)]).astype(I32)
            for i in range(sub):
                acc = acc + hit[:, i * LANES:(i + 1) * LANES]
            return acc
        acc = lax.fori_loop(0, n_act, body, jnp.zeros((tq, LANES), I32))
        return jnp.sum(acc, axis=1, keepdims=True)

    base = jnp.where(count(lambda u: u >= 0) >= topk, I32(0), I32(INT_MIN))

    def radix(i, base):
        cand = base | jnp.left_shift(I32(1), 30 - i)
        return jnp.where(count(lambda u: u >= cand) >= topk, cand, base)

    thr = lax.fori_loop(0, 31, radix, base)
    need = (topk - count(lambda u: u > thr)).astype(F32)
    r_i = lax.broadcasted_iota(I32, (LANES, LANES), 0)
    c_i = lax.broadcasted_iota(I32, (LANES, LANES), 1)
    tri = jnp.where(r_i <= c_i, 1.0, 0.0).astype(BF16)

    def emit(c, carry):
        for i in range(sub):
            at = pl.ds(pl.multiple_of(c * tk + i * LANES, LANES), LANES)
            u = u_ref[:, at]
            tie = u == thr
            tie_f = jnp.where(tie, 1.0, 0.0)
            inc = _dot(tie_f.astype(BF16), tri)
            sel = (u > thr) | (tie & (carry + inc - tie_f < need))
            keep = sel & (u > KEY_NEG_INF) & (u < KEY_POS_INF)
            mask_ref[0, :, at] = jnp.where(keep, 1.0, 0.0).astype(BF16)
            carry = carry + inc[:, LANES - 1:LANES]
        return carry

    lax.fori_loop(0, n_act, emit, jnp.zeros((tq, 1), F32))

    def clear(c, carry):
        mask_ref[0, :, chunk_at(c)] = jnp.zeros((tq, tk), BF16)
        return carry

    lax.fori_loop(n_act, t // tk, clear, 0)


def _dsa_select_prompt(qi, ki, wi, tq=256, tk=512):
    bsz, _, t, _ = qi.shape
    tq, tk = min(tq, t), min(tk, t)
    topk = min(IDX_TOPK, t // 4)
    return pl.pallas_call(
        functools.partial(_dsa_select_prompt_kernel, tq=tq, t=t, tk=tk, topk=topk),
        grid=(bsz, t // tq),
        in_specs=[pl.BlockSpec((1, IDX_HEADS, tq, IDX_DIM), lambda b, i: (b, 0, i, 0)),
                  pl.BlockSpec((1, t, IDX_DIM), lambda b, i: (b, 0, 0)),
                  pl.BlockSpec((1, tq, IDX_HEADS), lambda b, i: (b, i, 0))],
        out_specs=pl.BlockSpec((1, tq, t), lambda b, i: (b, i, 0)),
        out_shape=jax.ShapeDtypeStruct((bsz, t, t), BF16),
        scratch_shapes=[pltpu.VMEM((tq, t), I32)],
        compiler_params=_params("parallel", "parallel"),
        name="dsa_select_prompt",
    )(qi, ki, wi)


def _dsa_select_sample_kernel(pt_ref, qi_ref, wi_ref, *refs, n_pg, n_steps, tpad, n_new, topk, nb):
    pages = refs[:nb * n_pg]
    new_ref, mask_ref, u_ref = refs[nb * n_pg:]
    j = pl.program_id(1)

    def scores(bi, kc):
        rel = jnp.maximum(_dot(qi_ref[bi], kc.astype(BF16)), 0.0) * wi_ref[bi]
        return rel.reshape(tpad, IDX_HEADS, LANES).sum(axis=1)

    @pl.when(j < n_steps - 1)
    def _():
        for bi in range(nb):
            for i in range(n_pg):
                start = pl.multiple_of((j * n_pg + i) * LANES, LANES)
                u_ref[bi * tpad:(bi + 1) * tpad, pl.ds(start, LANES)] = _sortable(scores(bi, pages[bi * n_pg + i][0]))

    @pl.when(j == n_steps - 1)
    def _():
        t_idx = lax.broadcasted_iota(I32, (tpad, LANES), 0)
        c_idx = lax.broadcasted_iota(I32, (tpad, LANES), 1)
        base = (n_steps - 1) * n_pg * LANES
        for bi in range(nb):
            s = jnp.where((c_idx <= t_idx) & (c_idx < n_new), scores(bi, new_ref[bi]), -jnp.inf)
            u_ref[bi * tpad:(bi + 1) * tpad, base:base + LANES] = _sortable(s)
        for i in range(1, n_pg):
            u_ref[:, base + i * LANES:base + (i + 1) * LANES] = jnp.full((nb * tpad, LANES), KEY_NEG_INF, I32)

        def write(ci, sel, u):
            keep = sel & (u > KEY_NEG_INF) & (u < KEY_POS_INF)
            mask_ref[:, :, ci * LANES:(ci + 1) * LANES] = jnp.where(keep, 1.0, 0.0).reshape(nb, tpad, LANES)

        _emit_selection(u_ref, topk, write)


def _dsa_select_sample(page_table, qi, wi, pool_kidx, new_ki, n_new, n_pg=8):
    bsz, n_pages = page_table.shape
    tpad = qi.shape[1] // IDX_HEADS
    n_steps = n_pages // n_pg + 1
    width = n_steps * n_pg * LANES
    topk = min(IDX_TOPK, (n_pages * PAGE_SIZE + n_new) // 4)
    nb = _batch_rows(bsz)
    fix = lambda b, j, pt: (b, 0, 0)

    def page_map(bi, i):
        return lambda b, j, pt: (pt[b * nb + bi, jnp.minimum(j * n_pg + i, n_pages - 1)], 0, 0)

    grid_spec = pltpu.PrefetchScalarGridSpec(
        num_scalar_prefetch=1,
        grid=(bsz // nb, n_steps),
        in_specs=[pl.BlockSpec((nb, tpad * IDX_HEADS, IDX_DIM), fix), pl.BlockSpec((nb, tpad * IDX_HEADS, 1), fix)]
        + [pl.BlockSpec((1, IDX_DIM, PAGE_SIZE), page_map(bi, i)) for bi in range(nb) for i in range(n_pg)]
        + [pl.BlockSpec((nb, IDX_DIM, PAGE_SIZE), fix)],
        out_specs=pl.BlockSpec((nb, tpad, width), fix),
        scratch_shapes=[pltpu.VMEM((nb * tpad, width), I32)],
    )
    return pl.pallas_call(
        functools.partial(_dsa_select_sample_kernel, n_pg=n_pg, n_steps=n_steps, tpad=tpad, n_new=n_new, topk=topk, nb=nb),
        grid_spec=grid_spec,
        out_shape=jax.ShapeDtypeStruct((bsz, tpad, width), F32),
        compiler_params=_params("parallel", "arbitrary"),
        name="dsa_select_sample",
    )(page_table, qi, wi, *([pool_kidx] * (nb * n_pg)), new_ki)


def _batch_rows(bsz):
    return 4 if bsz % 4 == 0 else (2 if bsz % 2 == 0 else 1)


def _page_map(i, n_pg, n_pages):
    return lambda b, j, pt: (pt[b, jnp.minimum(j * n_pg + i, n_pages - 1)], 0, 0)


def _page_map4(i, n_pg, n_pages):
    return lambda b, j, pt: (pt[b, jnp.minimum(j * n_pg + i, n_pages - 1)], 0, 0, 0)


def _paged_flash_kernel(pt_ref, q_ref, *refs, n_pg, n_steps, nb):
    pages = refs[:nb * n_pg]
    new_ref, bias_ref, o_ref, m_ref, l_ref, acc_ref = refs[nb * n_pg:]
    j = pl.program_id(1)

    @pl.when(j == 0)
    def _():
        m_ref[...] = jnp.full(m_ref.shape, NEG, F32)
        l_ref[...] = jnp.zeros(l_ref.shape, F32)
        acc_ref[...] = jnp.zeros(acc_ref.shape, F32)

    def update(bi, kv):
        q = q_ref[bi]
        s = [_dot(q, k().astype(BF16)) + bias_ref[bi, :, i * LANES:(i + 1) * LANES].astype(F32)
             for i, (k, _) in enumerate(kv)]
        m_prev = m_ref[bi]
        m_new = m_prev
        for si in s:
            m_new = jnp.maximum(m_new, jnp.max(si, axis=1, keepdims=True))
        alpha = jnp.exp(m_prev - m_new)
        l_new = alpha * l_ref[bi]
        acc = acc_ref[bi] * pltpu.repeat(alpha, KV_COLS // LANES, axis=1)
        for si, (_, v) in zip(s, kv):
            p = jnp.where(si > 0.5 * NEG, jnp.exp(si - m_new), 0.0)
            l_new = l_new + jnp.sum(p, axis=1, keepdims=True)
            acc = acc + _dot_t(p.astype(BF16), v().astype(BF16))
        l_ref[bi] = l_new
        acc_ref[bi] = acc
        m_ref[bi] = m_new

    def loaders(ref, lead):
        return (lambda: ref[lead, 0]), (lambda: ref[lead, 1])

    @pl.when(j < n_steps - 1)
    def _():
        for bi in range(nb):
            update(bi, [loaders(pages[bi * n_pg + i], 0) for i in range(n_pg)])

    @pl.when(j == n_steps - 1)
    def _():
        for bi in range(nb):
            update(bi, [loaders(new_ref, bi)])
        o_ref[...] = acc_ref[...] / jnp.maximum(jnp.concatenate([l_ref[...]] * (KV_COLS // LANES), axis=-1), 1e-30)


def _paged_flash(page_table, q_bd, pool_t, new_page_t, bias, n_pg):
    bsz, n_pages = page_table.shape
    n_steps = n_pages // n_pg + 1
    rows = q_bd.shape[1]
    nb = _batch_rows(bsz)
    fix = lambda b, j, pt: (b, 0, 0)
    fix4 = lambda b, j, pt: (b, 0, 0, 0)
    page_block = (1, 2, KV_COLS, PAGE_SIZE)

    def page_map(bi, i):
        return lambda b, j, pt: (pt[b * nb + bi, jnp.minimum(j * n_pg + i, n_pages - 1)], 0, 0, 0)

    grid_spec = pltpu.PrefetchScalarGridSpec(
        num_scalar_prefetch=1,
        grid=(bsz // nb, n_steps),
        in_specs=[pl.BlockSpec((nb, rows, KV_COLS), fix)]
        + [pl.BlockSpec(page_block, page_map(bi, i)) for bi in range(nb) for i in range(n_pg)]
        + [pl.BlockSpec((nb, 2, KV_COLS, PAGE_SIZE), fix4),
           pl.BlockSpec((nb, rows, n_pg * LANES), lambda b, j, pt: (b, 0, j))],
        out_specs=pl.BlockSpec((nb, rows, KV_COLS), fix),
        scratch_shapes=[pltpu.VMEM((nb, rows, LANES), F32), pltpu.VMEM((nb, rows, LANES), F32),
                        pltpu.VMEM((nb, rows, KV_COLS), F32)],
    )
    return pl.pallas_call(
        functools.partial(_paged_flash_kernel, n_pg=n_pg, n_steps=n_steps, nb=nb),
        grid_spec=grid_spec,
        out_shape=jax.ShapeDtypeStruct((bsz, rows, KV_COLS), F32),
        compiler_params=_params("parallel", "arbitrary"),
        name="paged_flash",
    )(page_table, q_bd, *([pool_t] * (nb * n_pg)), new_page_t, bias)


def _cumsum_kernel(pt_ref, *refs, n_pg, n_steps, nb):
    pages = refs[:nb * n_pg]
    new_ref, o_ref, carry_ref = refs[nb * n_pg:]
    j = pl.program_id(1)
    r_i = lax.broadcasted_iota(I32, (LANES, LANES), 0)
    c_i = lax.broadcasted_iota(I32, (LANES, LANES), 1)
    tri = jnp.where(r_i <= c_i, 1.0, 0.0).astype(BF16)

    @pl.when(j == 0)
    def _():
        carry_ref[...] = jnp.zeros(carry_ref.shape, F32)

    ones = jnp.ones((LANES, LANES), BF16)

    def steps(bi, xs):
        local = [_dot_hp(x, tri) for x in xs]
        total = [_dot_hp(x, ones) for x in xs]
        carry = carry_ref[bi]
        for i in range(len(xs)):
            o_ref[bi, :, i * LANES:(i + 1) * LANES] = local[i] + carry
            carry = carry + total[i]
        carry_ref[bi] = carry

    @pl.when(j < n_steps - 1)
    def _():
        for bi in range(nb):
            steps(bi, [pages[bi * n_pg + i][0] for i in range(n_pg)])

    @pl.when(j == n_steps - 1)
    def _():
        for bi in range(nb):
            steps(bi, [new_ref[bi]])
        for i in range(1, n_pg):
            o_ref[:, :, i * LANES:(i + 1) * LANES] = jnp.zeros((nb, N_HEADS, LANES), F32)


def _paged_cumsum(page_table, pool_t, new_t, n_pg):
    bsz, n_pages = page_table.shape
    n_steps = n_pages // n_pg + 1
    nb = _batch_rows(bsz)
    fix = lambda b, j, pt: (b, 0, 0)

    def page_map(bi, i):
        return lambda b, j, pt: (pt[b * nb + bi, jnp.minimum(j * n_pg + i, n_pages - 1)], 0, 0)

    grid_spec = pltpu.PrefetchScalarGridSpec(
        num_scalar_prefetch=1,
        grid=(bsz // nb, n_steps),
        in_specs=[pl.BlockSpec((1, N_HEADS, LANES), page_map(bi, i)) for bi in range(nb) for i in range(n_pg)]
        + [pl.BlockSpec((nb, N_HEADS, LANES), fix)],
        out_specs=pl.BlockSpec((nb, N_HEADS, n_pg * LANES), lambda b, j, pt: (b, 0, j)),
        scratch_shapes=[pltpu.VMEM((nb, N_HEADS, LANES), F32)],
    )
    return pl.pallas_call(
        functools.partial(_cumsum_kernel, n_pg=n_pg, n_steps=n_steps, nb=nb),
        grid_spec=grid_spec,
        out_shape=jax.ShapeDtypeStruct((bsz, N_HEADS, n_steps * n_pg * LANES), F32),
        compiler_params=_params("parallel", "arbitrary"),
        name="fox_cumsum",
    )(page_table, *([pool_t] * (nb * n_pg)), new_t)


def _gelu_tanh(x):
    return 0.5 * x * (1.0 + jnp.tanh(0.7978845608028654 * (x + 0.044715 * x * x * x)))


def _nsa_compress_kernel(pt_ref, *refs, n_pg, n_steps, nc):
    pages = refs[:n_pg]
    pe_ref, w1_ref, w2_ref, o_ref, x_ref = refs[n_pg:]
    j = pl.program_id(1)
    per = PAGE_SIZE // CMP_STRIDE
    n_chunk = x_ref.shape[1]
    for i in range(0, n_pg, 2):
        start = pl.multiple_of((j * n_pg + i) * per, 2 * per)
        for ck in range(2 * N_KV_HEADS):
            x_ref[ck, pl.ds(start, 2 * per), :] = jnp.concatenate([pages[i][0, ck], pages[i + 1][0, ck]], axis=0)

    @pl.when(j == n_steps - 1)
    def _():
        half = CMP_STRIDE * HEAD_DIM
        rows = N_KV_HEADS * n_chunk
        row = lax.broadcasted_iota(I32, (N_KV_HEADS, n_chunk, HEAD_DIM), 1)
        for c in range(2):
            w1 = w1_ref[c]
            part = _dot(x_ref[c * N_KV_HEADS:(c + 1) * N_KV_HEADS].reshape(rows, half), w1)
            pe = pe_ref[c]
            pe_term = _dot(pe[:, :half], w1)[:, :CMP_HID] + _dot(pe[:, half:], w1)[:, CMP_HID:]
            h = pe_term[0:1, :] + part[:, :CMP_HID] + pltpu.roll(part[:, CMP_HID:], rows - 1, 0)
            out = _dot(_gelu_tanh(h).astype(BF16), w2_ref[c]).reshape(N_KV_HEADS, n_chunk, HEAD_DIM)
            o_ref[0, c * N_KV_HEADS:(c + 1) * N_KV_HEADS] = jnp.where(row < nc, out, 0.0)


def _nsa_compress(page_table, pool_t, pe8, w1cat, w2, n_pg=8):
    bsz, n_pages = page_table.shape
    per = PAGE_SIZE // CMP_STRIDE
    n_chunk = n_pages * per
    nc = n_chunk - CMP_LEN // CMP_STRIDE + 1
    n_steps = n_pages // n_pg
    width = CMP_STRIDE * HEAD_DIM

    def page_map(i):
        return lambda b, j, pt: (pt[b, j * n_pg + i], 0, 0, 0)

    fix3 = lambda b, j, pt: (0, 0, 0)
    grid_spec = pltpu.PrefetchScalarGridSpec(
        num_scalar_prefetch=1,
        grid=(bsz, n_steps),
        in_specs=[pl.BlockSpec((1, 2 * N_KV_HEADS, per, width), page_map(i)) for i in range(n_pg)]
        + [pl.BlockSpec(pe8.shape, fix3), pl.BlockSpec(w1cat.shape, fix3), pl.BlockSpec(w2.shape, fix3)],
        out_specs=pl.BlockSpec((1, 2 * N_KV_HEADS, n_chunk, HEAD_DIM), lambda b, j, pt: (b, 0, 0, 0)),
        scratch_shapes=[pltpu.VMEM((2 * N_KV_HEADS, n_chunk, width), BF16)],
    )
    return pl.pallas_call(
        functools.partial(_nsa_compress_kernel, n_pg=n_pg, n_steps=n_steps, nc=nc),
        grid_spec=grid_spec,
        out_shape=jax.ShapeDtypeStruct((bsz, 2 * N_KV_HEADS, n_chunk, HEAD_DIM), F32),
        compiler_params=_params("parallel", "arbitrary"),
        name="nsa_compress",
    )(page_table, *([pool_t] * n_pg), pe8, w1cat, w2), nc


def _nsa_cmp_kernel(q_ref, ck_ref, cv_ref, cover_ref, o_ref, sel_ref, u_ref, *, tq, pos0, nc, n_sel):
    t0 = pos0 + pl.program_id(1) * tq
    ncp = ck_ref.shape[2]
    nsp = cover_ref.shape[0]
    rows = GROUP * tq
    n_idx = lax.broadcasted_iota(I32, (tq, ncp), 1)
    t_idx = t0 + lax.broadcasted_iota(I32, (tq, ncp), 0)
    c_ok = ((n_idx * CMP_STRIDE + CMP_LEN - 1 <= t_idx) & (n_idx < nc))[None]
    blk = lax.broadcasted_iota(I32, (nsp, tq), 0)
    cur = (t0 + lax.broadcasted_iota(I32, (nsp, tq), 1)) // SEL_BLOCK
    forced = (blk == 0) | (blk == cur) | (blk == cur - 1)
    cover_t = cover_ref[...]
    for g in range(N_KV_HEADS):
        q = q_ref[0, g].reshape(rows, HEAD_DIM)
        s3 = jnp.where(c_ok, _dot_t(q, ck_ref[0, g]).reshape(GROUP, tq, ncp), NEG)
        m = jnp.max(s3, axis=-1, keepdims=True)
        e = jnp.where(c_ok, jnp.exp(s3 - m), 0.0)
        p = e / jnp.maximum(jnp.sum(e, axis=-1, keepdims=True), 1e-30)
        o = _dot(p.reshape(rows, ncp).astype(BF16), cv_ref[0, g])
        o_ref[0, :, g * GROUP * HEAD_DIM:(g + 1) * GROUP * HEAD_DIM] = jnp.concatenate(
            [o[j * tq:(j + 1) * tq] for j in range(GROUP)], axis=1)
        psum = p[0] + p[1] + p[2] + p[3]
        hi = psum.astype(BF16)
        r1 = psum - hi.astype(F32)
        mid = r1.astype(BF16)
        lo = (r1 - mid.astype(F32)).astype(BF16)
        imp = _dot_t(cover_t, hi) + _dot_t(cover_t, mid) + _dot_t(cover_t, lo)
        imp = jnp.where(forced, jnp.inf, imp)
        imp = jnp.where(blk <= cur, imp, -jnp.inf)
        u_ref[:, g * tq:(g + 1) * tq] = _sortable(imp)
    sel, u = _topk_mask_cols(u_ref, n_sel)
    sel_ref[0, 0] = jnp.where(sel & (u > KEY_NEG_INF), 1.0, 0.0)


def _nsa_cmp_select(q, cmp_k, cmp_v, pos0, nc, n_keys, tq):
    bsz, _, _, t, _ = q.shape
    ncp = cmp_k.shape[2]
    ns = -(-n_keys // SEL_BLOCK)
    nsp = -(-ns // LANES) * LANES
    n_sel = min(SEL_TOPN, ns)
    c0 = np.arange(ncp)[:, None] * CMP_STRIDE
    s0 = np.arange(nsp)[None, :] * SEL_BLOCK
    cover = (c0 <= s0 + SEL_BLOCK - 1) & (c0 + CMP_LEN - 1 >= s0) & (np.arange(ncp)[:, None] < nc) & (np.arange(nsp)[None, :] < ns)
    cover_t = jnp.asarray(cover.T.astype(np.float32), dtype=BF16)
    nq = t // tq
    qspec = pl.BlockSpec((1, N_KV_HEADS, GROUP, tq, HEAD_DIM), lambda b, i: (b, 0, 0, i, 0))
    cspec = pl.BlockSpec((1, N_KV_HEADS, ncp, HEAD_DIM), lambda b, i: (b, 0, 0, 0))
    o_c, sel = pl.pallas_call(
        functools.partial(_nsa_cmp_kernel, tq=tq, pos0=pos0, nc=nc, n_sel=n_sel),
        grid=(bsz, nq),
        in_specs=[qspec, cspec, cspec, pl.BlockSpec((nsp, ncp), lambda b, i: (0, 0))],
        out_specs=[pl.BlockSpec((1, tq, Q_DIM), lambda b, i: (b, i, 0)),
                   pl.BlockSpec((1, 1, nsp, N_KV_HEADS * tq), lambda b, i: (b, i, 0, 0))],
        out_shape=[jax.ShapeDtypeStruct((bsz, t, Q_DIM), F32), jax.ShapeDtypeStruct((bsz, nq, nsp, N_KV_HEADS * tq), F32)],
        scratch_shapes=[pltpu.VMEM((nsp, N_KV_HEADS * tq), I32)],
        compiler_params=_params("parallel", "parallel"),
        name="nsa_cmp_select",
    )(q, cmp_k, cmp_v, cover_t)
    sel = sel.reshape(bsz, nq, nsp, N_KV_HEADS, tq).transpose(0, 3, 1, 4, 2).reshape(bsz, N_KV_HEADS, t, nsp)
    return o_c, sel


def _kmean_kernel(*refs):
    o_ref = refs[-1]
    tot = jnp.sum(refs[0][0], axis=0, keepdims=True)
    for r in refs[1:-1]:
        tot = tot + jnp.sum(r[0], axis=0, keepdims=True)
    o_ref[0, 0] = tot * (1.0 / MOBA_BLOCK)


def _kmean_prompt(kv):
    bsz, t, _ = kv.shape
    nb = t // MOBA_BLOCK
    return pl.pallas_call(
        _kmean_kernel,
        grid=(bsz, nb),
        in_specs=[pl.BlockSpec((1, MOBA_BLOCK, KV_COLS), lambda b, i: (b, i, 0))],
        out_specs=pl.BlockSpec((1, 1, 1, KV_COLS), lambda b, i: (b, i, 0, 0)),
        out_shape=jax.ShapeDtypeStruct((bsz, nb, 1, KV_COLS), F32),
        compiler_params=_params("parallel", "parallel"),
        name="kmean_prompt",
    )(kv)


def _kmean_sample_kernel(pt_ref, *refs, per, nb):
    o_ref = refs[-1]
    ones = jnp.ones((8, PAGE_SIZE), BF16)
    blocks = len(refs[:-1]) // (per * nb)
    for n in range(nb * blocks):
        tot = jnp.zeros((8, KV_COLS), F32)
        for r in refs[n * per:(n + 1) * per]:
            x = r[0, 0]
            hi = x.astype(BF16)
            r1 = x - hi.astype(F32)
            mid = r1.astype(BF16)
            lo = (r1 - mid.astype(F32)).astype(BF16)
            tot = tot + _dot_t(ones, hi) + _dot_t(ones, mid) + _dot_t(ones, lo)
        o_ref[n // blocks, n % blocks] = tot[0:1] * (1.0 / MOBA_BLOCK)


def _kmean_sample(page_table, pool_t, n_pg):
    bsz, n_pages = page_table.shape
    per = MOBA_BLOCK // PAGE_SIZE
    n_blocks = n_pages // per
    nb = _batch_rows(bsz)

    def page_map(bi, i):
        return lambda b, n, pt: (pt[b * nb + bi, n * n_pg + i], 0, 0, 0)

    grid_spec = pltpu.PrefetchScalarGridSpec(
        num_scalar_prefetch=1,
        grid=(bsz // nb, n_pages // n_pg),
        in_specs=[pl.BlockSpec((1, 1, KV_COLS, PAGE_SIZE), page_map(bi, i)) for bi in range(nb) for i in range(n_pg)],
        out_specs=pl.BlockSpec((nb, n_pg // per, 1, KV_COLS), lambda b, n, pt: (b, n, 0, 0)),
    )
    return pl.pallas_call(
        functools.partial(_kmean_sample_kernel, per=per, nb=nb),
        grid_spec=grid_spec,
        out_shape=jax.ShapeDtypeStruct((bsz, n_blocks, 1, KV_COLS), F32),
        compiler_params=_params("parallel", "parallel"),
        name="kmean_sample",
    )(page_table, *([pool_t] * (nb * n_pg)))


def _moba_select_kernel(q_ref, km_ref, sel_ref, *, tq, pos0, k_top):
    t0 = pos0 + pl.program_id(1) * tq
    nbp = km_ref.shape[2]
    rows = GROUP * tq
    blk = lax.broadcasted_iota(I32, (nbp, rows), 0)
    n_past = (t0 + (lax.broadcasted_iota(I32, (nbp, rows), 1) & (tq - 1))) // MOBA_BLOCK
    for g in range(N_KV_HEADS):
        q = q_ref[0, g].reshape(rows, HEAD_DIM)
        s = jnp.where(blk < n_past, _dot_t(km_ref[0, g], q), -jnp.inf)
        sel = jnp.zeros((nbp, rows), F32)
        for _ in range(k_top):
            m = jnp.max(s, axis=0, keepdims=True)
            first = jnp.min(jnp.where(s == m, blk, nbp), axis=0, keepdims=True)
            pick = blk == first
            sel = jnp.where(pick & (m > -jnp.inf), 1.0, sel)
            s = jnp.where(pick, -jnp.inf, s)
        sel_ref[0, 0, g] = sel


def _moba_select(q, kmean, pos0, nb, tq):
    bsz, _, _, t, _ = q.shape
    nbp = kmean.shape[2]
    nq = t // tq
    qspec = pl.BlockSpec((1, N_KV_HEADS, GROUP, tq, HEAD_DIM), lambda b, i: (b, 0, 0, i, 0))
    sel = pl.pallas_call(
        functools.partial(_moba_select_kernel, tq=tq, pos0=pos0, k_top=min(MOBA_TOPK, nb)),
        grid=(bsz, nq),
        in_specs=[qspec, pl.BlockSpec((1, N_KV_HEADS, nbp, HEAD_DIM), lambda b, i: (b, 0, 0, 0))],
        out_specs=pl.BlockSpec((1, 1, N_KV_HEADS, nbp, GROUP * tq), lambda b, i: (b, i, 0, 0, 0)),
        out_shape=jax.ShapeDtypeStruct((bsz, nq, N_KV_HEADS, nbp, GROUP * tq), F32),
        compiler_params=_params("parallel", "parallel"),
        name="moba_select",
    )(q, kmean)
    sel = sel.reshape(bsz, nq, N_KV_HEADS, nbp, GROUP, tq).transpose(0, 2, 4, 1, 5, 3)
    return sel.reshape(bsz, N_KV_HEADS, GROUP, t, nbp)


def _rope_tables(pos):
    half = HEAD_DIM // 2
    inv = ROPE_THETA ** (-jnp.arange(half, dtype=F32) / half)
    ang = pos.astype(F32)[:, None] * inv[None, :]
    cos, sin = jnp.cos(ang), jnp.sin(ang)
    rep = LANES // HEAD_DIM
    return jnp.concatenate([cos, cos] * rep, axis=1), jnp.concatenate([-sin, sin] * rep, axis=1)


def _pad_to(x, axis, size):
    pad = [(0, 0)] * x.ndim
    pad[axis] = (0, size - x.shape[axis])
    return jnp.pad(x, pad)


def _q_groups(q):
    b, t = q.shape[:2]
    return q.transpose(0, 2, 1, 3).reshape(b, N_KV_HEADS, GROUP, t, HEAD_DIM)


def _pages_t(kv):
    return kv.transpose(0, 2, 3, 4, 1).reshape(kv.shape[0], 2, KV_COLS, PAGE_SIZE)


def _fit(x, width):
    return x[..., :width] if x.shape[-1] >= width else _pad_to(x, x.ndim - 1, width)


_HEAD_TO_GROUP = np.equal(np.arange(N_HEADS)[:, None] // GROUP, np.arange(N_KV_HEADS)[None, :]).astype(np.float32)


def _q_block_diag(q):
    s, tn = q.shape[:2]
    qb = q[:, :, :, None, :] * jnp.asarray(_HEAD_TO_GROUP, dtype=q.dtype)[None, None, :, :, None]
    return qb.reshape(s, tn * N_HEADS, KV_COLS)


def _extract_block_diag(o, tn):
    s = o.shape[0]
    o5 = o.reshape(s, tn, N_HEADS, N_KV_HEADS, HEAD_DIM) * _HEAD_TO_GROUP[None, None, :, :, None]
    return o5.sum(axis=3).reshape(s * tn, Q_DIM)


def _new_page(kv_new):
    return _pages_t(_pad_to(kv_new, 1, PAGE_SIZE))


def _pick_pages(n_pages):
    for n in (8, 4, 2, 1):
        if n_pages % n == 0:
            return n


def _identity_pages(bsz, n_pages):
    return jnp.arange(bsz * n_pages, dtype=I32).reshape(bsz, n_pages)


def _rows_th(ok, s, tn):
    return jnp.where(ok, 0.0, NEG).astype(BF16).reshape(s, tn * N_HEADS, ok.shape[-1])


def _kv_seg(c_k, rope):
    return (c_k, 2 * KV_COLS, KV_COLS if rope else False, 1.0, F32, "tok")


def _kv_rows(kv, bsz, t):
    return kv.reshape(bsz, t, 2, N_KV_HEADS, HEAD_DIM)


def _kv_forms(c_k, rope):
    return [(c_k, KV_COLS, rope, 1.0, BF16, "keys_t"), (c_k + KV_COLS, KV_COLS, False, 1.0, BF16, "values_1")]


def _dsa_project(x, bsz, t, w_in, cs, q_scale, prompt):
    hm = "heads" if prompt else "tok"
    c_qi = Q_DIM + 2 * KV_COLS
    c_ki = c_qi + IDX_HEADS * IDX_DIM
    segs = [(0, Q_DIM, True, q_scale, BF16, hm), _kv_seg(Q_DIM, True), (c_qi, IDX_HEADS * IDX_DIM, True, 1.0, BF16, hm),
            (c_ki, IDX_DIM, True, 1.0, F32, "tok"), (c_ki + IDX_DIM, IDX_HEADS, False, IDX_SCALE, F32, "tok")]
    if prompt:
        segs += _kv_forms(Q_DIM, True)
    q, kv, qi, ki, wi, *flash_kv = _proj(x, w_in, cs, segs, bsz)
    if not prompt:
        q, qi = q.reshape(bsz, t, N_HEADS, HEAD_DIM), qi.reshape(bsz, t, IDX_HEADS, IDX_DIM)
    return q, qi, wi.reshape(bsz, t, IDX_HEADS), _kv_rows(kv, bsz, t), ki.reshape(bsz, t, IDX_DIM), flash_kv


def _head_groups(q):
    return q.reshape(q.shape[0], N_KV_HEADS, GROUP, q.shape[2], HEAD_DIM)


def _dsa_prompt(x, bsz, t, w_in, cs):
    q, qi, wi, kv, ki, (k_t, v1) = _dsa_project(x, bsz, t, w_in, cs, ATTN_SCALE * LOG2E, True)
    mask = _dsa_select_prompt(qi, ki.astype(BF16), wi)
    o = _flash_prompt("dsa", _head_groups(q), k_t, v1, [mask])
    return o.reshape(bsz * t, Q_DIM), kv, ki


def _dsa_sample(x, s, tn, w_in, cs, page_table, cache_kv, cache_kidx, n_pg):
    q, qi, wi, kv, ki, _ = _dsa_project(x, s, tn, w_in, cs, ATTN_SCALE, False)
    tpad = 8
    qi_p = _pad_to(qi, 1, tpad).reshape(s, tpad * IDX_HEADS, IDX_DIM)
    wi_p = _pad_to(wi, 1, tpad).reshape(s, tpad * IDX_HEADS, 1)
    mask = _dsa_select_sample(page_table, qi_p, wi_p, cache_kidx.transpose(0, 2, 1),
                              _pad_to(ki, 1, PAGE_SIZE).transpose(0, 2, 1), tn, n_pg)
    ok = jnp.broadcast_to(mask[:, :tn, None, :] > 0.5, (s, tn, N_HEADS, mask.shape[-1]))
    o = _paged_flash(page_table, _q_block_diag(q), _pages_t(cache_kv), _new_page(kv), _rows_th(ok, s, tn), n_pg)
    return _extract_block_diag(o, tn), kv, ki


def _fox_project(x, bsz, t, w_in, b_f, cs, q_scale, prompt):
    segs = [(0, Q_DIM, False, q_scale, BF16, "heads" if prompt else "tok"), _kv_seg(Q_DIM, False),
            (Q_DIM + 2 * KV_COLS, N_HEADS, False, 1.0, F32, "tok")]
    if prompt:
        segs += _kv_forms(Q_DIM, False)
    q, kv, f, *flash_kv = _proj(x, w_in, cs, segs, bsz)
    logf = jax.nn.log_sigmoid(f.reshape(bsz, t, N_HEADS) + b_f)
    if not prompt:
        q = q.reshape(bsz, t, N_HEADS, HEAD_DIM)
    return q, _kv_rows(kv, bsz, t), logf, flash_kv


def _fox_prompt(x, bsz, t, w_in, b_f, cs):
    q, kv, logf, (k_t, v1) = _fox_project(x, bsz, t, w_in, b_f, cs, ATTN_SCALE * LOG2E, True)
    n_pages = t // LANES
    pool_t = logf.reshape(bsz, n_pages, LANES, N_HEADS).transpose(0, 1, 3, 2).reshape(bsz * n_pages, N_HEADS, LANES)
    c = _paged_cumsum(_identity_pages(bsz, n_pages), pool_t, jnp.zeros((bsz, N_HEADS, LANES), F32), _pick_pages(n_pages))
    c = (c[:, :, :t] * LOG2E).reshape(bsz, N_KV_HEADS, GROUP, t)
    o = _flash_prompt("fox", _head_groups(q), k_t, v1, [c])
    return o.reshape(bsz * t, Q_DIM), kv, logf


def _fox_sample(x, s, tn, w_in, b_f, cs, page_table, cache_kv, cache_logf, n_pg):
    q, kv, logf, _ = _fox_project(x, s, tn, w_in, b_f, cs, ATTN_SCALE, False)
    past = page_table.shape[1] * PAGE_SIZE
    c = _paged_cumsum(page_table, cache_logf.transpose(0, 2, 1), _pad_to(logf.transpose(0, 2, 1), 2, LANES), n_pg)
    col = jnp.arange(c.shape[-1])
    valid = (col[None, :] < past) | ((col[None, :] - past <= jnp.arange(tn)[:, None]) & (col[None, :] < past + tn))
    bias = jnp.where(valid[None, :, None, :], -c[:, None, :, :], NEG).reshape(s, tn * N_HEADS, c.shape[-1])
    o = _paged_flash(page_table, _q_block_diag(q), _pages_t(cache_kv), _new_page(kv), bias, n_pg)
    return _extract_block_diag(o, tn), kv, logf


def _nsa_project(x, bsz, t, w_in, b_gate, cs, q_scale, prompt):
    hm = "heads" if prompt else "tok"
    c_slc, c_win = Q_DIM + 2 * KV_COLS, Q_DIM + 4 * KV_COLS
    segs = [(0, Q_DIM, False, ATTN_SCALE, BF16, hm), (0, Q_DIM, True, q_scale, BF16, hm),
            _kv_seg(Q_DIM, False), _kv_seg(c_slc, True), _kv_seg(c_win, True),
            (Q_DIM + 6 * KV_COLS, 3 * N_HEADS, False, 1.0, F32, "tok")]
    if prompt:
        segs += _kv_forms(c_slc, True) + _kv_forms(c_win, True)
    q, q_rot, kv_cmp, kv_slc, kv_win, g, *flash_kv = _proj(x, w_in, cs, segs, bsz)
    gate = jax.nn.sigmoid(g.reshape(bsz, t, 3 * N_HEADS) + b_gate).reshape(bsz, t, 3, N_HEADS)
    if not prompt:
        q, q_rot = q.reshape(bsz, t, N_HEADS, HEAD_DIM), q_rot.reshape(bsz, t, N_HEADS, HEAD_DIM)
    return (q, q_rot, gate, _kv_rows(kv_cmp, bsz, t), _kv_rows(kv_slc, bsz, t), _kv_rows(kv_win, bsz, t), flash_kv)


def _nsa_weights(pe, w1, w2):
    r = CMP_LEN // CMP_STRIDE
    w1cat = w1.reshape(2, r, CMP_STRIDE * HEAD_DIM, CMP_HID).transpose(0, 2, 1, 3).reshape(2, CMP_STRIDE * HEAD_DIM, r * CMP_HID)
    pe8 = jnp.broadcast_to(pe.reshape(2, 1, CMP_LEN * HEAD_DIM), (2, 8, CMP_LEN * HEAD_DIM))
    return pe8.astype(BF16), w1cat.astype(BF16), w2.astype(BF16)


def _chunk_pages(kv):
    n = kv.shape[0]
    per = PAGE_SIZE // CMP_STRIDE
    x = kv.reshape(n, per, CMP_STRIDE, 2 * N_KV_HEADS, HEAD_DIM).transpose(0, 3, 1, 2, 4)
    return x.reshape(n, 2 * N_KV_HEADS, per, CMP_STRIDE * HEAD_DIM).astype(BF16)


def _gate_mix(gate, o_c, o_s, o_w):
    m = gate.shape[0] * gate.shape[1]
    return (gate.reshape(m, 3 * N_HEADS), o_c.reshape(m, Q_DIM), o_s.reshape(m, Q_DIM), o_w.reshape(m, Q_DIM))


def _nsa_prompt(x, bsz, t, w_in, b_gate, cmp_w, cs, tq=128, tk=512):
    q, q_rot, gate, kv_cmp, kv_slc, kv_win, (ks_t, vs1, kw_t, vw1) = _nsa_project(x, bsz, t, w_in, b_gate, cs,
                                                                                  ATTN_SCALE * LOG2E, True)
    n_pages = t // PAGE_SIZE
    cmp, nc = _nsa_compress(_identity_pages(bsz, n_pages), _chunk_pages(kv_cmp.reshape(bsz * n_pages, PAGE_SIZE, 2, N_KV_HEADS, HEAD_DIM)),
                            *cmp_w, n_pg=_pick_pages(n_pages))
    cmp = cmp.astype(BF16)
    tq, tk = min(tq, t), min(tk, t)
    o_c, selblk = _nsa_cmp_select(_head_groups(q), cmp[:, :N_KV_HEADS], cmp[:, N_KV_HEADS:], 0, nc, t, tq)
    qg = _head_groups(q_rot)
    e3 = _expand_matrix(selblk.shape[-1], SEL_BLOCK, t, tk)
    o_s = _flash_prompt("nsa_sel", qg, ks_t, vs1, [selblk, e3], tq=2 * tq, tk=tk)
    o_w = _flash_prompt("nsa_win", qg, kw_t, vw1, [], tk=tk)
    return _gate_mix(gate, o_c, o_s, o_w), kv_cmp, kv_slc, kv_win[:, -min(WINDOW, t):]


def _nsa_sample(x, s, tn, w_in, b_gate, cmp_w, cs, page_table, cache_cmp, cache_slc, state_win, n_pg):
    q, q_rot, gate, kv_cmp, kv_slc, kv_win, _ = _nsa_project(x, s, tn, w_in, b_gate, cs, ATTN_SCALE, False)
    past = page_table.shape[1] * PAGE_SIZE
    tpad = 8
    cmp, nc = _nsa_compress(page_table, _chunk_pages(cache_cmp), *cmp_w, n_pg=n_pg)
    cmp = cmp.astype(BF16)
    o_c, selblk = _nsa_cmp_select(_q_groups(_pad_to(q, 1, tpad)), cmp[:, :N_KV_HEADS], cmp[:, N_KV_HEADS:],
                                  past, nc, past + tn, tpad)
    o_c = o_c[:, :tn]
    qbd = _q_block_diag(q_rot)
    width = (page_table.shape[1] // n_pg + 1) * n_pg * LANES
    col = jnp.arange(width)
    pos = past + jnp.arange(tn)
    sel_key = jnp.repeat(selblk[:, :, :tn, :-(-width // SEL_BLOCK)] > 0.5, SEL_BLOCK, axis=-1)[..., :width]
    ok = sel_key & (col[None, :] <= pos[:, None])[None, None]
    ok = jnp.broadcast_to(ok.transpose(0, 2, 1, 3)[:, :, :, None, :], (s, tn, N_KV_HEADS, GROUP, width))
    o_s = _paged_flash(page_table, qbd, _pages_t(cache_slc), _new_page(kv_slc),
                       _rows_th(ok.reshape(s, tn, N_HEADS, width), s, tn), n_pg)
    win_buf = state_win.shape[1]
    n_wp = win_buf // PAGE_SIZE
    wcol = jnp.arange(2 * n_wp * LANES)
    win_pos = jnp.where(wcol < win_buf, past - win_buf + wcol, jnp.where(wcol < win_buf + tn, past + wcol - win_buf, -1))
    w_ok = (win_pos[None, :] <= pos[:, None]) & (win_pos[None, :] >= pos[:, None] - WINDOW) & (win_pos[None, :] >= 0)
    w_ok = jnp.broadcast_to(w_ok[None, :, None, :], (s, tn, N_HEADS, wcol.shape[0]))
    win_pages = _pages_t(state_win.reshape(s * n_wp, PAGE_SIZE, 2, N_KV_HEADS, HEAD_DIM))
    o_w = _paged_flash(_identity_pages(s, n_wp), qbd, win_pages, _new_page(kv_win), _rows_th(w_ok, s, tn), n_wp)
    unbd = lambda o: _extract_block_diag(o, tn).reshape(s, tn, Q_DIM)
    win = jnp.concatenate([state_win, kv_win], axis=1)[:, -win_buf:]
    return _gate_mix(gate, o_c, unbd(o_s), unbd(o_w)), kv_cmp, kv_slc, win


def _moba_project(x, bsz, t, w_in, cs, q_scale, prompt):
    segs = [(0, Q_DIM, True, q_scale, BF16, "heads" if prompt else "tok"), _kv_seg(Q_DIM, True)]
    if prompt:
        segs += _kv_forms(Q_DIM, True)
    q, kv, *flash_kv = _proj(x, w_in, cs, segs, bsz)
    if not prompt:
        q = q.reshape(bsz, t, N_HEADS, HEAD_DIM)
    return q, _kv_rows(kv, bsz, t), flash_kv


def _kmean_heads(km):
    b, nb = km.shape[:2]
    return _pad_to(km.reshape(b, nb, N_KV_HEADS, HEAD_DIM).transpose(0, 2, 1, 3), 2, -(-nb // 16) * 16).astype(BF16)


def _moba_prompt(x, bsz, t, w_in, cs, tq=128, tk=512):
    q, kv, (k_t, v1) = _moba_project(x, bsz, t, w_in, cs, ATTN_SCALE * LOG2E, True)
    tq, tk = min(tq, t), min(tk, t)
    nb = -(-t // MOBA_BLOCK)
    km = _kmean_heads(_kmean_prompt(kv.reshape(bsz, t, 2 * KV_COLS)))
    qg = _head_groups(q)
    sel = _moba_select(qg, km, 0, nb, tq)
    e3 = _expand_matrix(km.shape[2], MOBA_BLOCK, t, tk)
    o = _flash_prompt("moba", qg, k_t, v1, [sel, e3], tq=2 * tq, tk=tk)
    return o.reshape(bsz * t, Q_DIM), kv


def _moba_sample(x, s, tn, w_in, cs, page_table, cache_kv, n_pg):
    q, kv, _ = _moba_project(x, s, tn, w_in, cs, ATTN_SCALE, False)
    past = page_table.shape[1] * PAGE_SIZE
    tpad = 8
    pool = _pages_t(cache_kv)
    nb = -(-(past + tn) // MOBA_BLOCK)
    km = _kmean_heads(_kmean_sample(page_table, pool, n_pg))
    sel = _moba_select(_q_groups(_pad_to(q, 1, tpad)), km, past, nb, tpad)
    width = (page_table.shape[1] // n_pg + 1) * n_pg * LANES
    col = jnp.arange(width)
    pos = past + jnp.arange(tn)
    sel_key = _fit(jnp.repeat(sel[:, :, :, :tn] > 0.5, MOBA_BLOCK, axis=-1), width)
    own = (col[None, :] // MOBA_BLOCK == pos[:, None] // MOBA_BLOCK) & (col[None, :] <= pos[:, None])
    ok = sel_key | own[None, None, None]
    ok = ok.transpose(0, 3, 1, 2, 4).reshape(s, tn, N_HEADS, width)
    o = _paged_flash(page_table, _q_block_diag(q), pool, _new_page(kv), _rows_th(ok, s, tn), n_pg)
    return _extract_block_diag(o, tn), kv


def _cast_w(w):
    return _pad_to(w, 1, -(-w.shape[1] // LANES) * LANES).astype(BF16)


def kernel(x_prompt, x_sample, cache_a_kv, cache_a_kidx, cache_b_kv, cache_b_logf, cache_c_cmp_kv, cache_c_slc_kv, state_c_win_kv, cache_d_kv, page_table, a_w_in, a_w_out, b_w_in, b_b_f, b_w_out, c_w_in, c_b_gate, c_cmp_pe, c_cmp_w1, c_cmp_w2, c_w_out, d_w_in, d_w_out, ln_g, ln_b, ffn_w_gu, ffn_w_down):
    bsz, t, d = x_prompt.shape
    s, tn, _ = x_sample.shape
    n_pages = page_table.shape[1]
    past = n_pages * PAGE_SIZE
    n_pg = _pick_pages(n_pages)
    cs_p = _rope_tables(jnp.arange(t, dtype=I32))
    cs_s = _rope_tables(jnp.tile(past + jnp.arange(tn, dtype=I32), s))
    xp = x_prompt.reshape(bsz * t, d)
    xs = x_sample.reshape(s * tn, d)
    cmp_w = _nsa_weights(c_cmp_pe, c_cmp_w1, c_cmp_w2)
    w_out = [_cast_w(w) for w in (a_w_out, b_w_out, c_w_out, d_w_out)]

    op, a_kv_p, a_kidx_p = _dsa_prompt(xp, bsz, t, _cast_w(a_w_in), cs_p)
    os_, a_kv_s, a_kidx_s = _dsa_sample(xs, s, tn, _cast_w(a_w_in), cs_s, page_table, cache_a_kv, cache_a_kidx, n_pg)

    def finish(i, xp, xs, op, os_):
        xp = _out_ln(op, w_out[i], xp, ln_g[i, 0], ln_b[i, 0])
        xs = _out_ln(os_, w_out[i], xs, ln_g[i, 0], ln_b[i, 0])
        wgu, wd = ffn_w_gu[i].astype(BF16), ffn_w_down[i].astype(BF16)
        xp = _ffn_ln(xp, wgu, wd, ln_g[i, 1], ln_b[i, 1])
        xs = _ffn_ln(xs, wgu, wd, ln_g[i, 1], ln_b[i, 1])
        return xp, xs

    xp, xs = finish(0, xp, xs, op, os_)

    op, b_kv_p, b_logf_p = _fox_prompt(xp, bsz, t, _cast_w(b_w_in), b_b_f, cs_p)
    os_, b_kv_s, b_logf_s = _fox_sample(xs, s, tn, _cast_w(b_w_in), b_b_f, cs_s, page_table, cache_b_kv, cache_b_logf, n_pg)
    xp, xs = finish(1, xp, xs, op, os_)

    op, c_cmp_kv_p, c_slc_kv_p, c_win_kv_p = _nsa_prompt(xp, bsz, t, _cast_w(c_w_in), c_b_gate, cmp_w, cs_p)
    os_, c_cmp_kv_s, c_slc_kv_s, c_win_kv_s = _nsa_sample(xs, s, tn, _cast_w(c_w_in), c_b_gate, cmp_w, cs_s, page_table,
                                                          cache_c_cmp_kv, cache_c_slc_kv, state_c_win_kv, n_pg)
    xp, xs = finish(2, xp, xs, op, os_)

    op, d_kv_p = _moba_prompt(xp, bsz, t, _cast_w(d_w_in), cs_p)
    os_, d_kv_s = _moba_sample(xs, s, tn, _cast_w(d_w_in), cs_s, page_table, cache_d_kv, n_pg)
    xp, xs = finish(3, xp, xs, op, os_)

    return (xp.reshape(bsz, t, d), xs.reshape(s, tn, d), a_kv_p, a_kv_s, a_kidx_p, a_kidx_s, b_kv_p, b_kv_s,
            b_logf_p, b_logf_s, c_cmp_kv_p, c_cmp_kv_s, c_slc_kv_p, c_slc_kv_s, c_win_kv_p, c_win_kv_s, d_kv_p, d_kv_s)
```

```python
import functools

import numpy as np
import jax
import jax.numpy as jnp
from jax import lax
from jax.experimental import pallas as pl
from jax.experimental.pallas import tpu as pltpu

F32 = jnp.float32
BF16 = jnp.bfloat16
I32 = jnp.int32

N_HEADS = 16
HEAD_DIM = 64
N_KV_HEADS = 4
GROUP = N_HEADS // N_KV_HEADS
Q_DIM = N_HEADS * HEAD_DIM
KV_COLS = N_KV_HEADS * HEAD_DIM
DEPTH = 4
PAGE_SIZE = 128
ROPE_THETA = 10000.0
LN_EPS = 1e-5
ALPHA = (2 * DEPTH) ** 0.25
ATTN_SCALE = HEAD_DIM ** -0.5
LOG2E = 1.4426950408889634
IDX_HEADS = 8
IDX_DIM = 64
IDX_TOPK = 256
IDX_SCALE = (IDX_HEADS * IDX_DIM) ** -0.5
CMP_LEN = 32
CMP_STRIDE = 16
CMP_HID = 2 * HEAD_DIM
SEL_BLOCK = 64
SEL_TOPN = 16
WINDOW = 512
MOBA_BLOCK = 256
MOBA_TOPK = 3

LANES = 128
SUBLANES = 8
ROW_TILE = 512
FFN_ROW_TILE = 256
FFN_CHUNK = 256
Q_TILE = 128
K_TILE = 512
VMEM_LIMIT = 56 * 2 ** 20
NEG = -1e30
KEY_NEG_INF = -2139095041
KEY_POS_INF = 2139095040
INT_MIN = -2 ** 31


def _params(*sem):
    return pltpu.CompilerParams(dimension_semantics=sem, vmem_limit_bytes=VMEM_LIMIT)


def _dot_t(a, b):
    return lax.dot_general(a, b, (((1,), (1,)), ((), ())), preferred_element_type=F32)


def _dot(a, b):
    return jnp.dot(a, b, preferred_element_type=F32)


def _dot_hp(a, b):
    hi = a.astype(BF16)
    r1 = a - hi.astype(F32)
    mid = r1.astype(BF16)
    lo = (r1 - mid.astype(F32)).astype(BF16)
    return _dot(hi, b) + _dot(mid, b) + _dot(lo, b)


def _proj_kernel(x_ref, w_ref, cos_ref, sin_ref, *out_refs, segs):
    acc = _dot(x_ref[...].astype(BF16), w_ref[...])
    tm = acc.shape[0]
    lane = lax.broadcasted_iota(I32, (tm, LANES), 1)
    first_half = (lane & (HEAD_DIM - 1)) < HEAD_DIM // 2
    ones_col = jnp.where(lane == HEAD_DIM, 1.0, 0.0)
    for (c0, width, rope, scale, form), o_ref in zip(segs, out_refs):
        if c0 % LANES:
            o_ref[...] = (acc[:, c0:c0 + width] * scale).astype(o_ref.dtype)
            continue
        for j in range(-(-width // LANES)):
            x = acc[:, c0 + j * LANES:c0 + (j + 1) * LANES]
            if j * LANES < (width if rope is True else int(rope)):
                swapped = jnp.where(first_half, pltpu.roll(x, LANES - HEAD_DIM // 2, 1), pltpu.roll(x, HEAD_DIM // 2, 1))
                x = x * cos_ref[...] + swapped * sin_ref[...]
            if scale != 1.0:
                x = x * scale
            if form == "tok":
                wj = min(LANES, width - j * LANES)
                o_ref[:, j * LANES:j * LANES + wj] = x[:, :wj].astype(o_ref.dtype)
            elif form == "heads":
                o_ref[0, 2 * j] = x[:, :HEAD_DIM].astype(o_ref.dtype)
                o_ref[0, 2 * j + 1] = x[:, HEAD_DIM:].astype(o_ref.dtype)
            elif form == "keys_t":
                xt = x.T
                o_ref[0, 2 * j] = xt[:HEAD_DIM].astype(o_ref.dtype)
                o_ref[0, 2 * j + 1] = xt[HEAD_DIM:].astype(o_ref.dtype)
            else:
                o_ref[0, 2 * j] = jnp.where(lane < HEAD_DIM, x, ones_col).astype(o_ref.dtype)
                o_ref[0, 2 * j + 1] = jnp.where(lane < HEAD_DIM, pltpu.roll(x, HEAD_DIM, 1), ones_col).astype(o_ref.dtype)


def _proj(x, w, cs, segs, bsz=None):
    m, k = x.shape
    n = w.shape[1]
    tm = min(m, ROW_TILE)
    cos_t, sin_t = cs
    r_blocks = cos_t.shape[0] // tm
    tab = pl.BlockSpec((tm, LANES), lambda i: (i % r_blocks, 0))
    specs, shapes = [], []
    for _, width, _, _, dt, form in segs:
        nh = width // HEAD_DIM
        if form == "tok":
            specs.append(pl.BlockSpec((tm, width), lambda i: (i, 0)))
            shapes.append(jax.ShapeDtypeStruct((m, width), dt))
            continue
        t = m // bsz
        tpb = t // tm
        if form == "keys_t":
            specs.append(pl.BlockSpec((1, nh, HEAD_DIM, tm), lambda i: (i // tpb, 0, 0, i % tpb)))
            shapes.append(jax.ShapeDtypeStruct((bsz, nh, HEAD_DIM, t), dt))
        else:
            last = HEAD_DIM if form == "heads" else LANES
            specs.append(pl.BlockSpec((1, nh, tm, last), lambda i: (i // tpb, 0, i % tpb, 0)))
            shapes.append(jax.ShapeDtypeStruct((bsz, nh, t, last), dt))
    return pl.pallas_call(
        functools.partial(_proj_kernel, segs=tuple(s[:4] + (s[5],) for s in segs)),
        grid=(m // tm,),
        in_specs=[pl.BlockSpec((tm, k), lambda i: (i, 0)), pl.BlockSpec((k, n), lambda i: (0, 0)), tab, tab],
        out_specs=specs,
        out_shape=shapes,
        compiler_params=_params("parallel"),
        name="in_proj",
    )(x, w, cos_t, sin_t)


def _layer_norm(y, g, b):
    mu = jnp.mean(y, axis=-1, keepdims=True)
    d = y - mu
    var = jnp.mean(d * d, axis=-1, keepdims=True)
    return d * lax.rsqrt(var + LN_EPS) * g + b


def _out_ln_kernel(o_ref, w_ref, x_ref, g_ref, b_ref, y_ref):
    y = ALPHA * x_ref[...] + _dot(o_ref[...].astype(BF16), w_ref[...])
    y_ref[...] = _layer_norm(y, g_ref[...], b_ref[...])


def _out_ln_gated_kernel(gate_ref, oc_ref, os_ref, ow_ref, w_ref, x_ref, g_ref, b_ref, y_ref):
    gate = gate_ref[...]
    heads = []
    for h in range(N_HEADS):
        cols = slice(h * HEAD_DIM, (h + 1) * HEAD_DIM)
        heads.append(gate[:, h:h + 1] * oc_ref[:, cols] + gate[:, N_HEADS + h:N_HEADS + h + 1] * os_ref[:, cols]
                     + gate[:, 2 * N_HEADS + h:2 * N_HEADS + h + 1] * ow_ref[:, cols])
    o = jnp.concatenate(heads, axis=1)
    y = ALPHA * x_ref[...] + _dot(o.astype(BF16), w_ref[...])
    y_ref[...] = _layer_norm(y, g_ref[...], b_ref[...])


def _out_ln(o, w, x, g, b):
    m, d = x.shape
    tm = min(m, ROW_TILE)
    row = lambda i: (i, 0)
    fix = lambda i: (0, 0)
    gated = isinstance(o, tuple)
    o_in = list(o) if gated else [o]
    return pl.pallas_call(
        _out_ln_gated_kernel if gated else _out_ln_kernel,
        grid=(m // tm,),
        in_specs=[pl.BlockSpec((tm, a.shape[1]), row) for a in o_in]
        + [pl.BlockSpec(w.shape, fix), pl.BlockSpec((tm, d), row), pl.BlockSpec((1, d), fix), pl.BlockSpec((1, d), fix)],
        out_specs=pl.BlockSpec((tm, d), row),
        out_shape=jax.ShapeDtypeStruct((m, d), F32),
        compiler_params=_params("parallel"),
        name="out_proj_ln",
    )(*o_in, w, x, g.reshape(1, d), b.reshape(1, d))


def _ffn_ln_kernel(x_ref, wgu_ref, wd_ref, g_ref, b_ref, y_ref, *, d_ff, chunk):
    x = x_ref[...]
    xb = x.astype(BF16)
    acc = jnp.zeros(x.shape, F32)
    for c in range(d_ff // chunk):
        gate = _dot(xb, wgu_ref[:, c * chunk:(c + 1) * chunk])
        up = _dot(xb, wgu_ref[:, d_ff + c * chunk:d_ff + (c + 1) * chunk])
        h = gate * (1.0 / (1.0 + jnp.exp(-gate))) * up
        acc = acc + _dot(h.astype(BF16), wd_ref[c * chunk:(c + 1) * chunk, :])
    y_ref[...] = _layer_norm(ALPHA * x + acc, g_ref[...], b_ref[...])


def _ffn_ln(x, wgu, wd, g, b):
    m, d = x.shape
    d_ff = wd.shape[0]
    tm = min(m, FFN_ROW_TILE)
    row = lambda i: (i, 0)
    fix = lambda i: (0, 0)
    return pl.pallas_call(
        functools.partial(_ffn_ln_kernel, d_ff=d_ff, chunk=FFN_CHUNK),
        grid=(m // tm,),
        in_specs=[pl.BlockSpec((tm, d), row), pl.BlockSpec((d, 2 * d_ff), fix), pl.BlockSpec((d_ff, d), fix),
                  pl.BlockSpec((1, d), fix), pl.BlockSpec((1, d), fix)],
        out_specs=pl.BlockSpec((tm, d), row),
        out_shape=jax.ShapeDtypeStruct((m, d), F32),
        compiler_params=_params("parallel"),
        name="ffn_ln",
    )(x, wgu, wd, g.reshape(1, d), b.reshape(1, d))


def _sortable(x):
    x = jnp.where(x == 0.0, 0.0, x)
    b = lax.bitcast_convert_type(x, I32)
    return b ^ ((b >> 31) & I32(0x7FFFFFFF))


def _kth_largest_key(u_ref, k):
    rows = u_ref.shape[0]

    def count_ge(cand):
        return jnp.sum((u_ref[...] >= cand).astype(I32), axis=1, keepdims=True)

    base = jnp.where(count_ge(jnp.zeros((rows, 1), I32)) >= k, I32(0), I32(INT_MIN))

    def body(i, base):
        cand = base | jnp.left_shift(I32(1), 30 - i)
        return jnp.where(count_ge(cand) >= k, cand, base)

    return lax.fori_loop(0, 31, body, base)


def _emit_selection(u_ref, k, write):
    rows, n = u_ref.shape
    thr = _kth_largest_key(u_ref, k)
    n_gt = jnp.sum((u_ref[...] > thr).astype(I32), axis=1, keepdims=True)
    need = (k - n_gt).astype(F32)
    r_i = lax.broadcasted_iota(I32, (LANES, LANES), 0)
    c_i = lax.broadcasted_iota(I32, (LANES, LANES), 1)
    tri = jnp.where(r_i <= c_i, 1.0, 0.0).astype(BF16)
    carry = jnp.zeros((rows, 1), F32)
    for ci in range(n // LANES):
        u = u_ref[:, ci * LANES:(ci + 1) * LANES]
        tie = u == thr
        tie_f = jnp.where(tie, 1.0, 0.0)
        inc = _dot(tie_f.astype(BF16), tri)
        rank = carry + inc - tie_f
        sel = (u > thr) | (tie & (rank < need))
        write(ci, sel, u)
        carry = carry + inc[:, LANES - 1:LANES]


def _topk_mask_cols(u_ref, k):
    n, r = u_ref.shape

    def count(pred):
        return jnp.sum(pred(u_ref[...]).astype(I32), axis=0, keepdims=True)

    base = jnp.where(count(lambda u: u >= 0) >= k, I32(0), I32(INT_MIN))

    def radix(i, base):
        cand = base | jnp.left_shift(I32(1), 30 - i)
        return jnp.where(count(lambda u: u >= cand) >= k, cand, base)

    thr = lax.fori_loop(0, 31, radix, base)
    need = (k - count(lambda u: u > thr)).astype(F32)
    u = u_ref[...]
    tie = u == thr
    lower = jnp.where(lax.broadcasted_iota(I32, (n, n), 0) > lax.broadcasted_iota(I32, (n, n), 1), 1.0, 0.0)
    rank = _dot(lower.astype(BF16), jnp.where(tie, 1.0, 0.0).astype(BF16))
    return (u > thr) | (tie & (rank < need)), u


def _flash_prompt_kernel(*refs, kind, tq, tk, n_extra):
    q_ref, k_ref, v_ref = refs[:3]
    extra = refs[3:3 + n_extra]
    o_ref, m_ref, acc_ref, s_ref = refs[3 + n_extra:]
    g = pl.program_id(1)
    q0 = pl.program_id(2) * tq
    rows = GROUP * tq
    q = q_ref[0, 0].reshape(rows, HEAD_DIM)
    m_ref[...] = jnp.full(m_ref.shape, NEG, F32)
    acc_ref[...] = jnp.zeros(acc_ref.shape, F32)
    c_diag = q0 // tk
    c_lo = jnp.maximum(q0 - WINDOW, 0) // tk if kind == "nsa_win" else 0
    if kind == "nsa_sel":
        selb = extra[0][0, 0].astype(BF16)
    if kind == "moba":
        selb = extra[0][0, 0].reshape(rows, extra[0].shape[-1]).astype(BF16)

    def scores(c):
        return _dot(q, k_ref[0, g, :, pl.ds(pl.multiple_of(c * tk, tk), tk)])

    def chunk(c, diag):
        if kind in ("nsa_sel", "moba"):
            hit = _dot(selb, extra[1][c]) > 0.5
        s3 = s_ref[c % 2].reshape(GROUP, tq, tk)
        if not diag:
            s_ref[(c + 1) % 2] = scores(c + 1)
        start = pl.multiple_of(c * tk, tk)
        v = v_ref[0, g, pl.ds(start, tk), :]
        ok = None
        if diag or kind == "nsa_win":
            t_idx = q0 + lax.broadcasted_iota(I32, (GROUP, tq, tk), 1)
            s_idx = start + lax.broadcasted_iota(I32, (GROUP, tq, tk), 2)
            causal = s_idx <= t_idx
        if kind == "fox":
            ck = extra[0][0, 0, :, pl.ds(start, tk)]
            s3 = s3 - ck[:, None, :]
            ok = causal if diag else None
        elif kind == "dsa":
            msk = extra[0][0, :, pl.ds(start, tk)]
            ok = jnp.broadcast_to((msk > 0)[None], (GROUP, tq, tk))
        elif kind == "nsa_sel":
            ok = jnp.broadcast_to(hit[None], (GROUP, tq, tk))
            ok = (ok & causal) if diag else ok
        elif kind == "nsa_win":
            ok = causal & (s_idx >= t_idx - WINDOW)
        else:
            ok = hit.reshape(GROUP, tq, tk)
            if diag:
                ok = ok | (causal & ((s_idx // MOBA_BLOCK) == (t_idx // MOBA_BLOCK)))
        if ok is not None:
            s3 = jnp.where(ok, s3, NEG)
        s = s3.reshape(rows, tk)
        m_prev = m_ref[...]
        m_new = jnp.maximum(m_prev, jnp.max(s, axis=1, keepdims=True))
        alpha = jnp.exp2(m_prev - m_new)
        p = jnp.exp2(s - pltpu.repeat(m_new, tk // LANES, axis=1))
        acc_ref[...] = acc_ref[...] * alpha + _dot(p.astype(BF16), v)
        m_ref[...] = m_new

    def body(c, carry):
        chunk(c, False)
        return carry

    s_ref[c_lo % 2] = scores(c_lo)
    lax.fori_loop(c_lo, c_diag, body, 0)
    chunk(c_diag, True)
    acc = acc_ref[...]
    o = acc[:, :HEAD_DIM] / jnp.maximum(acc[:, HEAD_DIM:HEAD_DIM + 1], 1e-30)
    o_ref[0] = jnp.concatenate([o[j * tq:(j + 1) * tq] for j in range(GROUP)], axis=1)


def _flash_prompt(kind, q, k, v, extra, tq=Q_TILE, tk=K_TILE):
    bsz, _, _, t, _ = q.shape
    tk = min(tk, t)
    tq = min(tq, t)
    qspec = pl.BlockSpec((1, 1, GROUP, tq, HEAD_DIM), lambda b, g, i: (b, g, 0, i, 0))
    kvspec = pl.BlockSpec((1, N_KV_HEADS, HEAD_DIM, t), lambda b, g, i: (b, 0, 0, 0))
    vspec = pl.BlockSpec((1, N_KV_HEADS, t, LANES), lambda b, g, i: (b, 0, 0, 0))
    if kind == "fox":
        especs = [pl.BlockSpec((1, 1, GROUP, t), lambda b, g, i: (b, g, 0, 0))]
    elif kind == "dsa":
        especs = [pl.BlockSpec((1, tq, t), lambda b, g, i: (b, i, 0))]
    elif kind == "nsa_sel":
        especs = [pl.BlockSpec((1, 1, tq, extra[0].shape[-1]), lambda b, g, i: (b, g, i, 0)),
                  pl.BlockSpec(extra[1].shape, lambda b, g, i: (0, 0, 0))]
    elif kind == "moba":
        especs = [pl.BlockSpec((1, 1, GROUP, tq, extra[0].shape[-1]), lambda b, g, i: (b, g, 0, i, 0)),
                  pl.BlockSpec(extra[1].shape, lambda b, g, i: (0, 0, 0))]
    else:
        especs = []
    rows = GROUP * tq
    return pl.pallas_call(
        functools.partial(_flash_prompt_kernel, kind=kind, tq=tq, tk=tk, n_extra=len(extra)),
        grid=(bsz, N_KV_HEADS, t // tq),
        in_specs=[qspec, kvspec, vspec] + especs,
        out_specs=pl.BlockSpec((1, tq, GROUP * HEAD_DIM), lambda b, g, i: (b, i, g)),
        out_shape=jax.ShapeDtypeStruct((bsz, t, Q_DIM), F32),
        scratch_shapes=[pltpu.VMEM((rows, LANES), F32), pltpu.VMEM((rows, LANES), F32),
                        pltpu.VMEM((2, rows, tk), F32)],
        compiler_params=_params("parallel", "parallel", "parallel"),
        name="flash_prompt_" + kind,
    )(q, k, v, *extra)


def _expand_matrix(n_blocks_padded, block, t, tk):
    s = np.arange(t)
    e = (s[None, :] // block == np.arange(n_blocks_padded)[:, None]).astype(np.float32)
    e = e.reshape(n_blocks_padded, t // tk, tk).transpose(1, 0, 2)
    return jnp.asarray(e, dtype=BF16)


def _dsa_select_prompt_kernel(qi_ref, ki_ref, wi_ref, mask_ref, u_ref, *, tq, t, tk, topk):
    q0 = pl.program_id(1) * tq
    n_act = q0 // tk + 1
    sub = tk // LANES
    w = wi_ref[0]
    t_idx = q0 + lax.broadcasted_iota(I32, (tq, tk), 0)
    k_off = lax.broadcasted_iota(I32, (tq, tk), 1)

    def chunk_at(c):
        return pl.ds(pl.multiple_of(c * tk, tk), tk)

    def fill(c, carry):
        kc = ki_ref[0, chunk_at(c), :]
        s = jnp.zeros((tq, tk), F32)
        for h in range(IDX_HEADS):
            s = s + jnp.maximum(_dot_t(qi_ref[0, h], kc), 0.0) * w[:, h:h + 1]
        s = jnp.where(c * tk + k_off <= t_idx, s, -jnp.inf)
        u_ref[:, chunk_at(c)] = _sortable(s)
        return carry

    lax.fori_loop(0, n_act, fill, 0)

    def count(pred):
        def body(c, acc):
            hit = pred(u_ref[:, chunk_at(c)]).astype(I32)
            for i in range(sub):
                acc = acc + hit[:, i * LANES:(i + 1) * LANES]
            return acc
        acc = lax.fori_loop(0, n_act, body, jnp.zeros((tq, LANES), I32))
        return jnp.sum(acc, axis=1, keepdims=True)

    base = jnp.where(count(lambda u: u >= 0) >= topk, I32(0), I32(INT_MIN))

    def radix(i, base):
        cand = base | jnp.left_shift(I32(1), 30 - i)
        return jnp.where(count(lambda u: u >= cand) >= topk, cand, base)

    thr = lax.fori_loop(0, 31, radix, base)
    need = (topk - count(lambda u: u > thr)).astype(F32)
    r_i = lax.broadcasted_iota(I32, (LANES, LANES), 0)
    c_i = lax.broadcasted_iota(I32, (LANES, LANES), 1)
    tri = jnp.where(r_i <= c_i, 1.0, 0.0).astype(BF16)

    def emit(c, carry):
        for i in range(sub):
            at = pl.ds(pl.multiple_of(c * tk + i * LANES, LANES), LANES)
            u = u_ref[:, at]
            tie = u == thr
            tie_f = jnp.where(tie, 1.0, 0.0)
            inc = _dot(tie_f.astype(BF16), tri)
            sel = (u > thr) | (tie & (carry + inc - tie_f < need))
            keep = sel & (u > KEY_NEG_INF) & (u < KEY_POS_INF)
            mask_ref[0, :, at] = jnp.where(keep, 1.0, 0.0).astype(BF16)
            carry = carry + inc[:, LANES - 1:LANES]
        return carry

    lax.fori_loop(0, n_act, emit, jnp.zeros((tq, 1), F32))

    def clear(c, carry):
        mask_ref[0, :, chunk_at(c)] = jnp.zeros((tq, tk), BF16)
        return carry

    lax.fori_loop(n_act, t // tk, clear, 0)


def _dsa_select_prompt(qi, ki, wi, tq=2 * Q_TILE, tk=K_TILE):
    bsz, _, t, _ = qi.shape
    tq, tk = min(tq, t), min(tk, t)
    topk = min(IDX_TOPK, t // 4)
    return pl.pallas_call(
        functools.partial(_dsa_select_prompt_kernel, tq=tq, t=t, tk=tk, topk=topk),
        grid=(bsz, t // tq),
        in_specs=[pl.BlockSpec((1, IDX_HEADS, tq, IDX_DIM), lambda b, i: (b, 0, i, 0)),
                  pl.BlockSpec((1, t, IDX_DIM), lambda b, i: (b, 0, 0)),
                  pl.BlockSpec((1, tq, IDX_HEADS), lambda b, i: (b, i, 0))],
        out_specs=pl.BlockSpec((1, tq, t), lambda b, i: (b, i, 0)),
        out_shape=jax.ShapeDtypeStruct((bsz, t, t), BF16),
        scratch_shapes=[pltpu.VMEM((tq, t), I32)],
        compiler_params=_params("parallel", "parallel"),
        name="dsa_select_prompt",
    )(qi, ki, wi)


def _dsa_select_sample_kernel(pt_ref, qi_ref, wi_ref, *refs, n_pg, n_steps, tpad, n_new, topk, nb):
    pages = refs[:nb * n_pg]
    new_ref, mask_ref, u_ref = refs[nb * n_pg:]
    j = pl.program_id(1)

    def scores(bi, kc):
        rel = jnp.maximum(_dot(qi_ref[bi], kc.astype(BF16)), 0.0) * wi_ref[bi]
        return rel.reshape(tpad, IDX_HEADS, LANES).sum(axis=1)

    @pl.when(j < n_steps - 1)
    def _():
        for bi in range(nb):
            for i in range(n_pg):
                start = pl.multiple_of((j * n_pg + i) * LANES, LANES)
                u_ref[bi * tpad:(bi + 1) * tpad, pl.ds(start, LANES)] = _sortable(scores(bi, pages[bi * n_pg + i][0]))

    @pl.when(j == n_steps - 1)
    def _():
        t_idx = lax.broadcasted_iota(I32, (tpad, LANES), 0)
        c_idx = lax.broadcasted_iota(I32, (tpad, LANES), 1)
        base = (n_steps - 1) * n_pg * LANES
        for bi in range(nb):
            s = jnp.where((c_idx <= t_idx) & (c_idx < n_new), scores(bi, new_ref[bi]), -jnp.inf)
            u_ref[bi * tpad:(bi + 1) * tpad, base:base + LANES] = _sortable(s)
        for i in range(1, n_pg):
            u_ref[:, base + i * LANES:base + (i + 1) * LANES] = jnp.full((nb * tpad, LANES), KEY_NEG_INF, I32)

        def write(ci, sel, u):
            keep = sel & (u > KEY_NEG_INF) & (u < KEY_POS_INF)
            mask_ref[:, :, ci * LANES:(ci + 1) * LANES] = jnp.where(keep, 1.0, 0.0).reshape(nb, tpad, LANES)

        _emit_selection(u_ref, topk, write)


def _dsa_select_sample(page_table, qi, wi, pool_kidx, new_ki, n_new, n_pg=8):
    bsz, n_pages = page_table.shape
    tpad = qi.shape[1] // IDX_HEADS
    n_steps = n_pages // n_pg + 1
    width = n_steps * n_pg * LANES
    topk = min(IDX_TOPK, (n_pages * PAGE_SIZE + n_new) // 4)
    nb = _batch_rows(bsz)
    fix = lambda b, j, pt: (b, 0, 0)

    def page_map(bi, i):
        return lambda b, j, pt: (pt[b * nb + bi, jnp.minimum(j * n_pg + i, n_pages - 1)], 0, 0)

    grid_spec = pltpu.PrefetchScalarGridSpec(
        num_scalar_prefetch=1,
        grid=(bsz // nb, n_steps),
        in_specs=[pl.BlockSpec((nb, tpad * IDX_HEADS, IDX_DIM), fix), pl.BlockSpec((nb, tpad * IDX_HEADS, 1), fix)]
        + [pl.BlockSpec((1, IDX_DIM, PAGE_SIZE), page_map(bi, i)) for bi in range(nb) for i in range(n_pg)]
        + [pl.BlockSpec((nb, IDX_DIM, PAGE_SIZE), fix)],
        out_specs=pl.BlockSpec((nb, tpad, width), fix),
        scratch_shapes=[pltpu.VMEM((nb * tpad, width), I32)],
    )
    return pl.pallas_call(
        functools.partial(_dsa_select_sample_kernel, n_pg=n_pg, n_steps=n_steps, tpad=tpad, n_new=n_new, topk=topk, nb=nb),
        grid_spec=grid_spec,
        out_shape=jax.ShapeDtypeStruct((bsz, tpad, width), F32),
        compiler_params=_params("parallel", "arbitrary"),
        name="dsa_select_sample",
    )(page_table, qi, wi, *([pool_kidx] * (nb * n_pg)), new_ki)


def _batch_rows(bsz):
    return 4 if bsz % 4 == 0 else (2 if bsz % 2 == 0 else 1)


def _page_map(i, n_pg, n_pages):
    return lambda b, j, pt: (pt[b, jnp.minimum(j * n_pg + i, n_pages - 1)], 0, 0)


def _page_map4(i, n_pg, n_pages):
    return lambda b, j, pt: (pt[b, jnp.minimum(j * n_pg + i, n_pages - 1)], 0, 0, 0)


def _paged_flash_kernel(pt_ref, q_ref, *refs, n_pg, n_steps, nb):
    pages = refs[:nb * n_pg]
    new_ref, bias_ref, o_ref, m_ref, l_ref, acc_ref = refs[nb * n_pg:]
    j = pl.program_id(1)

    @pl.when(j == 0)
    def _():
        m_ref[...] = jnp.full(m_ref.shape, NEG, F32)
        l_ref[...] = jnp.zeros(l_ref.shape, F32)
        acc_ref[...] = jnp.zeros(acc_ref.shape, F32)

    def update(bi, kv):
        q = q_ref[bi]
        s = [_dot(q, k().astype(BF16)) + bias_ref[bi, :, i * LANES:(i + 1) * LANES].astype(F32)
             for i, (k, _) in enumerate(kv)]
        m_prev = m_ref[bi]
        m_new = m_prev
        for si in s:
            m_new = jnp.maximum(m_new, jnp.max(si, axis=1, keepdims=True))
        alpha = jnp.exp(m_prev - m_new)
        l_new = alpha * l_ref[bi]
        acc = acc_ref[bi] * pltpu.repeat(alpha, KV_COLS // LANES, axis=1)
        for si, (_, v) in zip(s, kv):
            p = jnp.where(si > 0.5 * NEG, jnp.exp(si - m_new), 0.0)
            l_new = l_new + jnp.sum(p, axis=1, keepdims=True)
            acc = acc + _dot_t(p.astype(BF16), v().astype(BF16))
        l_ref[bi] = l_new
        acc_ref[bi] = acc
        m_ref[bi] = m_new

    def loaders(ref, lead):
        return (lambda: ref[lead, 0]), (lambda: ref[lead, 1])

    @pl.when(j < n_steps - 1)
    def _():
        for bi in range(nb):
            update(bi, [loaders(pages[bi * n_pg + i], 0) for i in range(n_pg)])

    @pl.when(j == n_steps - 1)
    def _():
        for bi in range(nb):
            update(bi, [loaders(new_ref, bi)])
        o_ref[...] = acc_ref[...] / jnp.maximum(jnp.concatenate([l_ref[...]] * (KV_COLS // LANES), axis=-1), 1e-30)


def _paged_flash(page_table, q_bd, pool_t, new_page_t, bias, n_pg):
    bsz, n_pages = page_table.shape
    n_steps = n_pages // n_pg + 1
    rows = q_bd.shape[1]
    nb = _batch_rows(bsz)
    fix = lambda b, j, pt: (b, 0, 0)
    fix4 = lambda b, j, pt: (b, 0, 0, 0)
    page_block = (1, 2, KV_COLS, PAGE_SIZE)

    def page_map(bi, i):
        return lambda b, j, pt: (pt[b * nb + bi, jnp.minimum(j * n_pg + i, n_pages - 1)], 0, 0, 0)

    grid_spec = pltpu.PrefetchScalarGridSpec(
        num_scalar_prefetch=1,
        grid=(bsz // nb, n_steps),
        in_specs=[pl.BlockSpec((nb, rows, KV_COLS), fix)]
        + [pl.BlockSpec(page_block, page_map(bi, i)) for bi in range(nb) for i in range(n_pg)]
        + [pl.BlockSpec((nb, 2, KV_COLS, PAGE_SIZE), fix4),
           pl.BlockSpec((nb, rows, n_pg * LANES), lambda b, j, pt: (b, 0, j))],
        out_specs=pl.BlockSpec((nb, rows, KV_COLS), fix),
        scratch_shapes=[pltpu.VMEM((nb, rows, LANES), F32), pltpu.VMEM((nb, rows, LANES), F32),
                        pltpu.VMEM((nb, rows, KV_COLS), F32)],
    )
    return pl.pallas_call(
        functools.partial(_paged_flash_kernel, n_pg=n_pg, n_steps=n_steps, nb=nb),
        grid_spec=grid_spec,
        out_shape=jax.ShapeDtypeStruct((bsz, rows, KV_COLS), F32),
        compiler_params=_params("parallel", "arbitrary"),
        name="paged_flash",
    )(page_table, q_bd, *([pool_t] * (nb * n_pg)), new_page_t, bias)


def _cumsum_kernel(pt_ref, *refs, n_pg, n_steps, nb):
    pages = refs[:nb * n_pg]
    new_ref, o_ref, carry_ref = refs[nb * n_pg:]
    j = pl.program_id(1)
    r_i = lax.broadcasted_iota(I32, (LANES, LANES), 0)
    c_i = lax.broadcasted_iota(I32, (LANES, LANES), 1)
    tri = jnp.where(r_i <= c_i, 1.0, 0.0).astype(BF16)

    @pl.when(j == 0)
    def _():
        carry_ref[...] = jnp.zeros(carry_ref.shape, F32)

    ones = jnp.ones((LANES, LANES), BF16)

    def steps(bi, xs):
        local = [_dot_hp(x, tri) for x in xs]
        total = [_dot_hp(x, ones) for x in xs]
        carry = carry_ref[bi]
        for i in range(len(xs)):
            o_ref[bi, :, i * LANES:(i + 1) * LANES] = local[i] + carry
            carry = carry + total[i]
        carry_ref[bi] = carry

    @pl.when(j < n_steps - 1)
    def _():
        for bi in range(nb):
            steps(bi, [pages[bi * n_pg + i][0] for i in range(n_pg)])

    @pl.when(j == n_steps - 1)
    def _():
        for bi in range(nb):
            steps(bi, [new_ref[bi]])
        for i in range(1, n_pg):
            o_ref[:, :, i * LANES:(i + 1) * LANES] = jnp.zeros((nb, N_HEADS, LANES), F32)


def _paged_cumsum(page_table, pool_t, new_t, n_pg):
    bsz, n_pages = page_table.shape
    n_steps = n_pages // n_pg + 1
    nb = _batch_rows(bsz)
    fix = lambda b, j, pt: (b, 0, 0)

    def page_map(bi, i):
        return lambda b, j, pt: (pt[b * nb + bi, jnp.minimum(j * n_pg + i, n_pages - 1)], 0, 0)

    grid_spec = pltpu.PrefetchScalarGridSpec(
        num_scalar_prefetch=1,
        grid=(bsz // nb, n_steps),
        in_specs=[pl.BlockSpec((1, N_HEADS, LANES), page_map(bi, i)) for bi in range(nb) for i in range(n_pg)]
        + [pl.BlockSpec((nb, N_HEADS, LANES), fix)],
        out_specs=pl.BlockSpec((nb, N_HEADS, n_pg * LANES), lambda b, j, pt: (b, 0, j)),
        scratch_shapes=[pltpu.VMEM((nb, N_HEADS, LANES), F32)],
    )
    return pl.pallas_call(
        functools.partial(_cumsum_kernel, n_pg=n_pg, n_steps=n_steps, nb=nb),
        grid_spec=grid_spec,
        out_shape=jax.ShapeDtypeStruct((bsz, N_HEADS, n_steps * n_pg * LANES), F32),
        compiler_params=_params("parallel", "arbitrary"),
        name="fox_cumsum",
    )(page_table, *([pool_t] * (nb * n_pg)), new_t)


def _gelu_tanh(x):
    return 0.5 * x * (1.0 + jnp.tanh(0.7978845608028654 * (x + 0.044715 * x * x * x)))


def _nsa_compress_kernel(pt_ref, *refs, n_pg, n_steps, nc):
    pages = refs[:n_pg]
    pe_ref, w1_ref, w2_ref, o_ref, x_ref = refs[n_pg:]
    j = pl.program_id(1)
    per = PAGE_SIZE // CMP_STRIDE
    n_chunk = x_ref.shape[1]
    for i in range(0, n_pg, 2):
        start = pl.multiple_of((j * n_pg + i) * per, 2 * per)
        for ck in range(2 * N_KV_HEADS):
            x_ref[ck, pl.ds(start, 2 * per), :] = jnp.concatenate([pages[i][0, ck], pages[i + 1][0, ck]], axis=0)

    @pl.when(j == n_steps - 1)
    def _():
        half = CMP_STRIDE * HEAD_DIM
        rows = N_KV_HEADS * n_chunk
        row = lax.broadcasted_iota(I32, (N_KV_HEADS, n_chunk, HEAD_DIM), 1)
        for c in range(2):
            w1 = w1_ref[c]
            part = _dot(x_ref[c * N_KV_HEADS:(c + 1) * N_KV_HEADS].reshape(rows, half), w1)
            pe = pe_ref[c]
            pe_term = _dot(pe[:, :half], w1)[:, :CMP_HID] + _dot(pe[:, half:], w1)[:, CMP_HID:]
            h = pe_term[0:1, :] + part[:, :CMP_HID] + pltpu.roll(part[:, CMP_HID:], rows - 1, 0)
            out = _dot(_gelu_tanh(h).astype(BF16), w2_ref[c]).reshape(N_KV_HEADS, n_chunk, HEAD_DIM)
            o_ref[0, c * N_KV_HEADS:(c + 1) * N_KV_HEADS] = jnp.where(row < nc, out, 0.0)


def _nsa_compress(page_table, pool_t, pe8, w1cat, w2, n_pg=8):
    bsz, n_pages = page_table.shape
    per = PAGE_SIZE // CMP_STRIDE
    n_chunk = n_pages * per
    nc = n_chunk - CMP_LEN // CMP_STRIDE + 1
    n_steps = n_pages // n_pg
    width = CMP_STRIDE * HEAD_DIM

    def page_map(i):
        return lambda b, j, pt: (pt[b, j * n_pg + i], 0, 0, 0)

    fix3 = lambda b, j, pt: (0, 0, 0)
    grid_spec = pltpu.PrefetchScalarGridSpec(
        num_scalar_prefetch=1,
        grid=(bsz, n_steps),
        in_specs=[pl.BlockSpec((1, 2 * N_KV_HEADS, per, width), page_map(i)) for i in range(n_pg)]
        + [pl.BlockSpec(pe8.shape, fix3), pl.BlockSpec(w1cat.shape, fix3), pl.BlockSpec(w2.shape, fix3)],
        out_specs=pl.BlockSpec((1, 2 * N_KV_HEADS, n_chunk, HEAD_DIM), lambda b, j, pt: (b, 0, 0, 0)),
        scratch_shapes=[pltpu.VMEM((2 * N_KV_HEADS, n_chunk, width), BF16)],
    )
    return pl.pallas_call(
        functools.partial(_nsa_compress_kernel, n_pg=n_pg, n_steps=n_steps, nc=nc),
        grid_spec=grid_spec,
        out_shape=jax.ShapeDtypeStruct((bsz, 2 * N_KV_HEADS, n_chunk, HEAD_DIM), F32),
        compiler_params=_params("parallel", "arbitrary"),
        name="nsa_compress",
    )(page_table, *([pool_t] * n_pg), pe8, w1cat, w2), nc


def _nsa_cmp_kernel(q_ref, ck_ref, cv_ref, cover_ref, o_ref, sel_ref, u_ref, *, tq, pos0, nc, n_sel):
    t0 = pos0 + pl.program_id(1) * tq
    ncp = ck_ref.shape[2]
    nsp = cover_ref.shape[0]
    rows = GROUP * tq
    n_idx = lax.broadcasted_iota(I32, (tq, ncp), 1)
    t_idx = t0 + lax.broadcasted_iota(I32, (tq, ncp), 0)
    c_ok = ((n_idx * CMP_STRIDE + CMP_LEN - 1 <= t_idx) & (n_idx < nc))[None]
    blk = lax.broadcasted_iota(I32, (nsp, tq), 0)
    cur = (t0 + lax.broadcasted_iota(I32, (nsp, tq), 1)) // SEL_BLOCK
    forced = (blk == 0) | (blk == cur) | (blk == cur - 1)
    cover_t = cover_ref[...]
    for g in range(N_KV_HEADS):
        q = q_ref[0, g].reshape(rows, HEAD_DIM)
        s3 = jnp.where(c_ok, _dot_t(q, ck_ref[0, g]).reshape(GROUP, tq, ncp), NEG)
        m = jnp.max(s3, axis=-1, keepdims=True)
        e = jnp.where(c_ok, jnp.exp(s3 - m), 0.0)
        p = e / jnp.maximum(jnp.sum(e, axis=-1, keepdims=True), 1e-30)
        o = _dot(p.reshape(rows, ncp).astype(BF16), cv_ref[0, g])
        o_ref[0, :, g * GROUP * HEAD_DIM:(g + 1) * GROUP * HEAD_DIM] = jnp.concatenate(
            [o[j * tq:(j + 1) * tq] for j in range(GROUP)], axis=1)
        psum = p[0] + p[1] + p[2] + p[3]
        hi = psum.astype(BF16)
        r1 = psum - hi.astype(F32)
        mid = r1.astype(BF16)
        lo = (r1 - mid.astype(F32)).astype(BF16)
        imp = _dot_t(cover_t, hi) + _dot_t(cover_t, mid) + _dot_t(cover_t, lo)
        imp = jnp.where(forced, jnp.inf, imp)
        imp = jnp.where(blk <= cur, imp, -jnp.inf)
        u_ref[:, g * tq:(g + 1) * tq] = _sortable(imp)
    sel, u = _topk_mask_cols(u_ref, n_sel)
    sel_ref[0, 0] = jnp.where(sel & (u > KEY_NEG_INF), 1.0, 0.0)


def _nsa_cmp_select(q, cmp_k, cmp_v, pos0, nc, n_keys, tq):
    bsz, _, _, t, _ = q.shape
    ncp = cmp_k.shape[2]
    ns = -(-n_keys // SEL_BLOCK)
    nsp = -(-ns // LANES) * LANES
    n_sel = min(SEL_TOPN, ns)
    c0 = np.arange(ncp)[:, None] * CMP_STRIDE
    s0 = np.arange(nsp)[None, :] * SEL_BLOCK
    cover = (c0 <= s0 + SEL_BLOCK - 1) & (c0 + CMP_LEN - 1 >= s0) & (np.arange(ncp)[:, None] < nc) & (np.arange(nsp)[None, :] < ns)
    cover_t = jnp.asarray(cover.T.astype(np.float32), dtype=BF16)
    nq = t // tq
    qspec = pl.BlockSpec((1, N_KV_HEADS, GROUP, tq, HEAD_DIM), lambda b, i: (b, 0, 0, i, 0))
    cspec = pl.BlockSpec((1, N_KV_HEADS, ncp, HEAD_DIM), lambda b, i: (b, 0, 0, 0))
    o_c, sel = pl.pallas_call(
        functools.partial(_nsa_cmp_kernel, tq=tq, pos0=pos0, nc=nc, n_sel=n_sel),
        grid=(bsz, nq),
        in_specs=[qspec, cspec, cspec, pl.BlockSpec((nsp, ncp), lambda b, i: (0, 0))],
        out_specs=[pl.BlockSpec((1, tq, Q_DIM), lambda b, i: (b, i, 0)),
                   pl.BlockSpec((1, 1, nsp, N_KV_HEADS * tq), lambda b, i: (b, i, 0, 0))],
        out_shape=[jax.ShapeDtypeStruct((bsz, t, Q_DIM), F32), jax.ShapeDtypeStruct((bsz, nq, nsp, N_KV_HEADS * tq), F32)],
        scratch_shapes=[pltpu.VMEM((nsp, N_KV_HEADS * tq), I32)],
        compiler_params=_params("parallel", "parallel"),
        name="nsa_cmp_select",
    )(q, cmp_k, cmp_v, cover_t)
    sel = sel.reshape(bsz, nq, nsp, N_KV_HEADS, tq).transpose(0, 3, 1, 4, 2).reshape(bsz, N_KV_HEADS, t, nsp)
    return o_c, sel


def _kmean_kernel(*refs):
    o_ref = refs[-1]
    tot = jnp.sum(refs[0][0], axis=0, keepdims=True)
    for r in refs[1:-1]:
        tot = tot + jnp.sum(r[0], axis=0, keepdims=True)
    o_ref[0, 0] = tot * (1.0 / MOBA_BLOCK)


def _kmean_prompt(kv):
    bsz, t, _ = kv.shape
    nb = t // MOBA_BLOCK
    return pl.pallas_call(
        _kmean_kernel,
        grid=(bsz, nb),
        in_specs=[pl.BlockSpec((1, MOBA_BLOCK, KV_COLS), lambda b, i: (b, i, 0))],
        out_specs=pl.BlockSpec((1, 1, 1, KV_COLS), lambda b, i: (b, i, 0, 0)),
        out_shape=jax.ShapeDtypeStruct((bsz, nb, 1, KV_COLS), F32),
        compiler_params=_params("parallel", "parallel"),
        name="kmean_prompt",
    )(kv)


def _kmean_sample_kernel(pt_ref, *refs, per, nb):
    o_ref = refs[-1]
    ones = jnp.ones((SUBLANES, PAGE_SIZE), BF16)
    blocks = len(refs[:-1]) // (per * nb)
    for n in range(nb * blocks):
        tot = jnp.zeros((SUBLANES, KV_COLS), F32)
        for r in refs[n * per:(n + 1) * per]:
            x = r[0, 0]
            hi = x.astype(BF16)
            r1 = x - hi.astype(F32)
            mid = r1.astype(BF16)
            lo = (r1 - mid.astype(F32)).astype(BF16)
            tot = tot + _dot_t(ones, hi) + _dot_t(ones, mid) + _dot_t(ones, lo)
        o_ref[n // blocks, n % blocks] = tot[0:1] * (1.0 / MOBA_BLOCK)


def _kmean_sample(page_table, pool_t, n_pg):
    bsz, n_pages = page_table.shape
    per = MOBA_BLOCK // PAGE_SIZE
    n_blocks = n_pages // per
    nb = _batch_rows(bsz)

    def page_map(bi, i):
        return lambda b, n, pt: (pt[b * nb + bi, n * n_pg + i], 0, 0, 0)

    grid_spec = pltpu.PrefetchScalarGridSpec(
        num_scalar_prefetch=1,
        grid=(bsz // nb, n_pages // n_pg),
        in_specs=[pl.BlockSpec((1, 1, KV_COLS, PAGE_SIZE), page_map(bi, i)) for bi in range(nb) for i in range(n_pg)],
        out_specs=pl.BlockSpec((nb, n_pg // per, 1, KV_COLS), lambda b, n, pt: (b, n, 0, 0)),
    )
    return pl.pallas_call(
        functools.partial(_kmean_sample_kernel, per=per, nb=nb),
        grid_spec=grid_spec,
        out_shape=jax.ShapeDtypeStruct((bsz, n_blocks, 1, KV_COLS), F32),
        compiler_params=_params("parallel", "parallel"),
        name="kmean_sample",
    )(page_table, *([pool_t] * (nb * n_pg)))


def _moba_select_kernel(q_ref, km_ref, sel_ref, *, tq, pos0, k_top):
    t0 = pos0 + pl.program_id(1) * tq
    nbp = km_ref.shape[2]
    rows = GROUP * tq
    blk = lax.broadcasted_iota(I32, (nbp, rows), 0)
    n_past = (t0 + (lax.broadcasted_iota(I32, (nbp, rows), 1) & (tq - 1))) // MOBA_BLOCK
    for g in range(N_KV_HEADS):
        q = q_ref[0, g].reshape(rows, HEAD_DIM)
        s = jnp.where(blk < n_past, _dot_t(km_ref[0, g], q), -jnp.inf)
        sel = jnp.zeros((nbp, rows), F32)
        for _ in range(k_top):
            m = jnp.max(s, axis=0, keepdims=True)
            first = jnp.min(jnp.where(s == m, blk, nbp), axis=0, keepdims=True)
            pick = blk == first
            sel = jnp.where(pick & (m > -jnp.inf), 1.0, sel)
            s = jnp.where(pick, -jnp.inf, s)
        sel_ref[0, 0, g] = sel


def _moba_select(q, kmean, pos0, nb, tq):
    bsz, _, _, t, _ = q.shape
    nbp = kmean.shape[2]
    nq = t // tq
    qspec = pl.BlockSpec((1, N_KV_HEADS, GROUP, tq, HEAD_DIM), lambda b, i: (b, 0, 0, i, 0))
    sel = pl.pallas_call(
        functools.partial(_moba_select_kernel, tq=tq, pos0=pos0, k_top=min(MOBA_TOPK, nb)),
        grid=(bsz, nq),
        in_specs=[qspec, pl.BlockSpec((1, N_KV_HEADS, nbp, HEAD_DIM), lambda b, i: (b, 0, 0, 0))],
        out_specs=pl.BlockSpec((1, 1, N_KV_HEADS, nbp, GROUP * tq), lambda b, i: (b, i, 0, 0, 0)),
        out_shape=jax.ShapeDtypeStruct((bsz, nq, N_KV_HEADS, nbp, GROUP * tq), F32),
        compiler_params=_params("parallel", "parallel"),
        name="moba_select",
    )(q, kmean)
    sel = sel.reshape(bsz, nq, N_KV_HEADS, nbp, GROUP, tq).transpose(0, 2, 4, 1, 5, 3)
    return sel.reshape(bsz, N_KV_HEADS, GROUP, t, nbp)


def _rope_tables(pos):
    half = HEAD_DIM // 2
    inv = ROPE_THETA ** (-jnp.arange(half, dtype=F32) / half)
    ang = pos.astype(F32)[:, None] * inv[None, :]
    cos, sin = jnp.cos(ang), jnp.sin(ang)
    rep = LANES // HEAD_DIM
    return jnp.concatenate([cos, cos] * rep, axis=1), jnp.concatenate([-sin, sin] * rep, axis=1)


def _pad_to(x, axis, size):
    pad = [(0, 0)] * x.ndim
    pad[axis] = (0, size - x.shape[axis])
    return jnp.pad(x, pad)


def _q_groups(q):
    b, t = q.shape[:2]
    return q.transpose(0, 2, 1, 3).reshape(b, N_KV_HEADS, GROUP, t, HEAD_DIM)


def _pages_t(kv):
    return kv.transpose(0, 2, 3, 4, 1).reshape(kv.shape[0], 2, KV_COLS, PAGE_SIZE)


def _fit(x, width):
    return x[..., :width] if x.shape[-1] >= width else _pad_to(x, x.ndim - 1, width)


_HEAD_TO_GROUP = np.equal(np.arange(N_HEADS)[:, None] // GROUP, np.arange(N_KV_HEADS)[None, :]).astype(np.float32)


def _q_block_diag(q):
    s, tn = q.shape[:2]
    qb = q[:, :, :, None, :] * jnp.asarray(_HEAD_TO_GROUP, dtype=q.dtype)[None, None, :, :, None]
    return qb.reshape(s, tn * N_HEADS, KV_COLS)


def _extract_block_diag(o, tn):
    s = o.shape[0]
    o5 = o.reshape(s, tn, N_HEADS, N_KV_HEADS, HEAD_DIM) * _HEAD_TO_GROUP[None, None, :, :, None]
    return o5.sum(axis=3).reshape(s * tn, Q_DIM)


def _new_page(kv_new):
    return _pages_t(_pad_to(kv_new, 1, PAGE_SIZE))


def _pick_pages(n_pages):
    for n in (8, 4, 2, 1):
        if n_pages % n == 0:
            return n


def _identity_pages(bsz, n_pages):
    return jnp.arange(bsz * n_pages, dtype=I32).reshape(bsz, n_pages)


def _rows_th(ok, s, tn):
    return jnp.where(ok, 0.0, NEG).astype(BF16).reshape(s, tn * N_HEADS, ok.shape[-1])


def _kv_seg(c_k, rope):
    return (c_k, 2 * KV_COLS, KV_COLS if rope else False, 1.0, F32, "tok")


def _kv_rows(kv, bsz, t):
    return kv.reshape(bsz, t, 2, N_KV_HEADS, HEAD_DIM)


def _kv_forms(c_k, rope):
    return [(c_k, KV_COLS, rope, 1.0, BF16, "keys_t"), (c_k + KV_COLS, KV_COLS, False, 1.0, BF16, "values_1")]


def _dsa_project(x, bsz, t, w_in, cs, q_scale, prompt):
    hm = "heads" if prompt else "tok"
    c_qi = Q_DIM + 2 * KV_COLS
    c_ki = c_qi + IDX_HEADS * IDX_DIM
    segs = [(0, Q_DIM, True, q_scale, BF16, hm), _kv_seg(Q_DIM, True), (c_qi, IDX_HEADS * IDX_DIM, True, 1.0, BF16, hm),
            (c_ki, IDX_DIM, True, 1.0, F32, "tok"), (c_ki + IDX_DIM, IDX_HEADS, False, IDX_SCALE, F32, "tok")]
    if prompt:
        segs += _kv_forms(Q_DIM, True)
    q, kv, qi, ki, wi, *flash_kv = _proj(x, w_in, cs, segs, bsz)
    if not prompt:
        q, qi = q.reshape(bsz, t, N_HEADS, HEAD_DIM), qi.reshape(bsz, t, IDX_HEADS, IDX_DIM)
    return q, qi, wi.reshape(bsz, t, IDX_HEADS), _kv_rows(kv, bsz, t), ki.reshape(bsz, t, IDX_DIM), flash_kv


def _head_groups(q):
    return q.reshape(q.shape[0], N_KV_HEADS, GROUP, q.shape[2], HEAD_DIM)


def _dsa_prompt(x, bsz, t, w_in, cs):
    q, qi, wi, kv, ki, (k_t, v1) = _dsa_project(x, bsz, t, w_in, cs, ATTN_SCALE * LOG2E, True)
    mask = _dsa_select_prompt(qi, ki.astype(BF16), wi)
    o = _flash_prompt("dsa", _head_groups(q), k_t, v1, [mask])
    return o.reshape(bsz * t, Q_DIM), kv, ki


def _dsa_sample(x, s, tn, w_in, cs, page_table, cache_kv, cache_kidx, n_pg):
    q, qi, wi, kv, ki, _ = _dsa_project(x, s, tn, w_in, cs, ATTN_SCALE, False)
    tpad = SUBLANES
    qi_p = _pad_to(qi, 1, tpad).reshape(s, tpad * IDX_HEADS, IDX_DIM)
    wi_p = _pad_to(wi, 1, tpad).reshape(s, tpad * IDX_HEADS, 1)
    mask = _dsa_select_sample(page_table, qi_p, wi_p, cache_kidx.transpose(0, 2, 1),
                              _pad_to(ki, 1, PAGE_SIZE).transpose(0, 2, 1), tn, n_pg)
    ok = jnp.broadcast_to(mask[:, :tn, None, :] > 0.5, (s, tn, N_HEADS, mask.shape[-1]))
    o = _paged_flash(page_table, _q_block_diag(q), _pages_t(cache_kv), _new_page(kv), _rows_th(ok, s, tn), n_pg)
    return _extract_block_diag(o, tn), kv, ki


def _fox_project(x, bsz, t, w_in, b_f, cs, q_scale, prompt):
    segs = [(0, Q_DIM, False, q_scale, BF16, "heads" if prompt else "tok"), _kv_seg(Q_DIM, False),
            (Q_DIM + 2 * KV_COLS, N_HEADS, False, 1.0, F32, "tok")]
    if prompt:
        segs += _kv_forms(Q_DIM, False)
    q, kv, f, *flash_kv = _proj(x, w_in, cs, segs, bsz)
    logf = jax.nn.log_sigmoid(f.reshape(bsz, t, N_HEADS) + b_f)
    if not prompt:
        q = q.reshape(bsz, t, N_HEADS, HEAD_DIM)
    return q, _kv_rows(kv, bsz, t), logf, flash_kv


def _fox_prompt(x, bsz, t, w_in, b_f, cs):
    q, kv, logf, (k_t, v1) = _fox_project(x, bsz, t, w_in, b_f, cs, ATTN_SCALE * LOG2E, True)
    n_pages = t // LANES
    pool_t = logf.reshape(bsz, n_pages, LANES, N_HEADS).transpose(0, 1, 3, 2).reshape(bsz * n_pages, N_HEADS, LANES)
    c = _paged_cumsum(_identity_pages(bsz, n_pages), pool_t, jnp.zeros((bsz, N_HEADS, LANES), F32), _pick_pages(n_pages))
    c = (c[:, :, :t] * LOG2E).reshape(bsz, N_KV_HEADS, GROUP, t)
    o = _flash_prompt("fox", _head_groups(q), k_t, v1, [c])
    return o.reshape(bsz * t, Q_DIM), kv, logf


def _fox_sample(x, s, tn, w_in, b_f, cs, page_table, cache_kv, cache_logf, n_pg):
    q, kv, logf, _ = _fox_project(x, s, tn, w_in, b_f, cs, ATTN_SCALE, False)
    past = page_table.shape[1] * PAGE_SIZE
    c = _paged_cumsum(page_table, cache_logf.transpose(0, 2, 1), _pad_to(logf.transpose(0, 2, 1), 2, LANES), n_pg)
    col = jnp.arange(c.shape[-1])
    valid = (col[None, :] < past) | ((col[None, :] - past <= jnp.arange(tn)[:, None]) & (col[None, :] < past + tn))
    bias = jnp.where(valid[None, :, None, :], -c[:, None, :, :], NEG).reshape(s, tn * N_HEADS, c.shape[-1])
    o = _paged_flash(page_table, _q_block_diag(q), _pages_t(cache_kv), _new_page(kv), bias, n_pg)
    return _extract_block_diag(o, tn), kv, logf


def _nsa_project(x, bsz, t, w_in, b_gate, cs, q_scale, prompt):
    hm = "heads" if prompt else "tok"
    c_slc, c_win = Q_DIM + 2 * KV_COLS, Q_DIM + 4 * KV_COLS
    segs = [(0, Q_DIM, False, ATTN_SCALE, BF16, hm), (0, Q_DIM, True, q_scale, BF16, hm),
            _kv_seg(Q_DIM, False), _kv_seg(c_slc, True), _kv_seg(c_win, True),
            (Q_DIM + 6 * KV_COLS, 3 * N_HEADS, False, 1.0, F32, "tok")]
    if prompt:
        segs += _kv_forms(c_slc, True) + _kv_forms(c_win, True)
    q, q_rot, kv_cmp, kv_slc, kv_win, g, *flash_kv = _proj(x, w_in, cs, segs, bsz)
    gate = jax.nn.sigmoid(g.reshape(bsz, t, 3 * N_HEADS) + b_gate).reshape(bsz, t, 3, N_HEADS)
    if not prompt:
        q, q_rot = q.reshape(bsz, t, N_HEADS, HEAD_DIM), q_rot.reshape(bsz, t, N_HEADS, HEAD_DIM)
    return (q, q_rot, gate, _kv_rows(kv_cmp, bsz, t), _kv_rows(kv_slc, bsz, t), _kv_rows(kv_win, bsz, t), flash_kv)


def _nsa_weights(pe, w1, w2):
    r = CMP_LEN // CMP_STRIDE
    w1cat = w1.reshape(2, r, CMP_STRIDE * HEAD_DIM, CMP_HID).transpose(0, 2, 1, 3).reshape(2, CMP_STRIDE * HEAD_DIM, r * CMP_HID)
    pe8 = jnp.broadcast_to(pe.reshape(2, 1, CMP_LEN * HEAD_DIM), (2, 8, CMP_LEN * HEAD_DIM))
    return pe8.astype(BF16), w1cat.astype(BF16), w2.astype(BF16)


def _chunk_pages(kv):
    n = kv.shape[0]
    per = PAGE_SIZE // CMP_STRIDE
    x = kv.reshape(n, per, CMP_STRIDE, 2 * N_KV_HEADS, HEAD_DIM).transpose(0, 3, 1, 2, 4)
    return x.reshape(n, 2 * N_KV_HEADS, per, CMP_STRIDE * HEAD_DIM).astype(BF16)


def _gate_mix(gate, o_c, o_s, o_w):
    m = gate.shape[0] * gate.shape[1]
    return (gate.reshape(m, 3 * N_HEADS), o_c.reshape(m, Q_DIM), o_s.reshape(m, Q_DIM), o_w.reshape(m, Q_DIM))


def _nsa_prompt(x, bsz, t, w_in, b_gate, cmp_w, cs, tq=Q_TILE, tk=K_TILE):
    q, q_rot, gate, kv_cmp, kv_slc, kv_win, (ks_t, vs1, kw_t, vw1) = _nsa_project(x, bsz, t, w_in, b_gate, cs,
                                                                                  ATTN_SCALE * LOG2E, True)
    n_pages = t // PAGE_SIZE
    cmp, nc = _nsa_compress(_identity_pages(bsz, n_pages), _chunk_pages(kv_cmp.reshape(bsz * n_pages, PAGE_SIZE, 2, N_KV_HEADS, HEAD_DIM)),
                            *cmp_w, n_pg=_pick_pages(n_pages))
    cmp = cmp.astype(BF16)
    tq, tk = min(tq, t), min(tk, t)
    o_c, selblk = _nsa_cmp_select(_head_groups(q), cmp[:, :N_KV_HEADS], cmp[:, N_KV_HEADS:], 0, nc, t, tq)
    qg = _head_groups(q_rot)
    e3 = _expand_matrix(selblk.shape[-1], SEL_BLOCK, t, tk)
    o_s = _flash_prompt("nsa_sel", qg, ks_t, vs1, [selblk, e3], tq=2 * tq, tk=tk)
    o_w = _flash_prompt("nsa_win", qg, kw_t, vw1, [], tk=tk)
    return _gate_mix(gate, o_c, o_s, o_w), kv_cmp, kv_slc, kv_win[:, -min(WINDOW, t):]


def _nsa_sample(x, s, tn, w_in, b_gate, cmp_w, cs, page_table, cache_cmp, cache_slc, state_win, n_pg):
    q, q_rot, gate, kv_cmp, kv_slc, kv_win, _ = _nsa_project(x, s, tn, w_in, b_gate, cs, ATTN_SCALE, False)
    past = page_table.shape[1] * PAGE_SIZE
    tpad = SUBLANES
    cmp, nc = _nsa_compress(page_table, _chunk_pages(cache_cmp), *cmp_w, n_pg=n_pg)
    cmp = cmp.astype(BF16)
    o_c, selblk = _nsa_cmp_select(_q_groups(_pad_to(q, 1, tpad)), cmp[:, :N_KV_HEADS], cmp[:, N_KV_HEADS:],
                                  past, nc, past + tn, tpad)
    o_c = o_c[:, :tn]
    qbd = _q_block_diag(q_rot)
    width = (page_table.shape[1] // n_pg + 1) * n_pg * LANES
    col = jnp.arange(width)
    pos = past + jnp.arange(tn)
    sel_key = jnp.repeat(selblk[:, :, :tn, :-(-width // SEL_BLOCK)] > 0.5, SEL_BLOCK, axis=-1)[..., :width]
    ok = sel_key & (col[None, :] <= pos[:, None])[None, None]
    ok = jnp.broadcast_to(ok.transpose(0, 2, 1, 3)[:, :, :, None, :], (s, tn, N_KV_HEADS, GROUP, width))
    o_s = _paged_flash(page_table, qbd, _pages_t(cache_slc), _new_page(kv_slc),
                       _rows_th(ok.reshape(s, tn, N_HEADS, width), s, tn), n_pg)
    win_buf = state_win.shape[1]
    n_wp = win_buf // PAGE_SIZE
    wcol = jnp.arange(2 * n_wp * LANES)
    win_pos = jnp.where(wcol < win_buf, past - win_buf + wcol, jnp.where(wcol < win_buf + tn, past + wcol - win_buf, -1))
    w_ok = (win_pos[None, :] <= pos[:, None]) & (win_pos[None, :] >= pos[:, None] - WINDOW) & (win_pos[None, :] >= 0)
    w_ok = jnp.broadcast_to(w_ok[None, :, None, :], (s, tn, N_HEADS, wcol.shape[0]))
    win_pages = _pages_t(state_win.reshape(s * n_wp, PAGE_SIZE, 2, N_KV_HEADS, HEAD_DIM))
    o_w = _paged_flash(_identity_pages(s, n_wp), qbd, win_pages, _new_page(kv_win), _rows_th(w_ok, s, tn), n_wp)
    unbd = lambda o: _extract_block_diag(o, tn).reshape(s, tn, Q_DIM)
    win = jnp.concatenate([state_win, kv_win], axis=1)[:, -win_buf:]
    return _gate_mix(gate, o_c, unbd(o_s), unbd(o_w)), kv_cmp, kv_slc, win


def _moba_project(x, bsz, t, w_in, cs, q_scale, prompt):
    segs = [(0, Q_DIM, True, q_scale, BF16, "heads" if prompt else "tok"), _kv_seg(Q_DIM, True)]
    if prompt:
        segs += _kv_forms(Q_DIM, True)
    q, kv, *flash_kv = _proj(x, w_in, cs, segs, bsz)
    if not prompt:
        q = q.reshape(bsz, t, N_HEADS, HEAD_DIM)
    return q, _kv_rows(kv, bsz, t), flash_kv


def _kmean_heads(km):
    b, nb = km.shape[:2]
    return _pad_to(km.reshape(b, nb, N_KV_HEADS, HEAD_DIM).transpose(0, 2, 1, 3), 2, -(-nb // 16) * 16).astype(BF16)


def _moba_prompt(x, bsz, t, w_in, cs, tq=Q_TILE, tk=K_TILE):
    q, kv, (k_t, v1) = _moba_project(x, bsz, t, w_in, cs, ATTN_SCALE * LOG2E, True)
    tq, tk = min(tq, t), min(tk, t)
    nb = -(-t // MOBA_BLOCK)
    km = _kmean_heads(_kmean_prompt(kv.reshape(bsz, t, 2 * KV_COLS)))
    qg = _head_groups(q)
    sel = _moba_select(qg, km, 0, nb, tq)
    e3 = _expand_matrix(km.shape[2], MOBA_BLOCK, t, tk)
    o = _flash_prompt("moba", qg, k_t, v1, [sel, e3], tq=2 * tq, tk=tk)
    return o.reshape(bsz * t, Q_DIM), kv


def _moba_sample(x, s, tn, w_in, cs, page_table, cache_kv, n_pg):
    q, kv, _ = _moba_project(x, s, tn, w_in, cs, ATTN_SCALE, False)
    past = page_table.shape[1] * PAGE_SIZE
    tpad = SUBLANES
    pool = _pages_t(cache_kv)
    nb = -(-(past + tn) // MOBA_BLOCK)
    km = _kmean_heads(_kmean_sample(page_table, pool, n_pg))
    sel = _moba_select(_q_groups(_pad_to(q, 1, tpad)), km, past, nb, tpad)
    width = (page_table.shape[1] // n_pg + 1) * n_pg * LANES
    col = jnp.arange(width)
    pos = past + jnp.arange(tn)
    sel_key = _fit(jnp.repeat(sel[:, :, :, :tn] > 0.5, MOBA_BLOCK, axis=-1), width)
    own = (col[None, :] // MOBA_BLOCK == pos[:, None] // MOBA_BLOCK) & (col[None, :] <= pos[:, None])
    ok = sel_key | own[None, None, None]
    ok = ok.transpose(0, 3, 1, 2, 4).reshape(s, tn, N_HEADS, width)
    o = _paged_flash(page_table, _q_block_diag(q), pool, _new_page(kv), _rows_th(ok, s, tn), n_pg)
    return _extract_block_diag(o, tn), kv


def _cast_w(w):
    return _pad_to(w, 1, -(-w.shape[1] // LANES) * LANES).astype(BF16)


def kernel(x_prompt, x_sample, cache_a_kv, cache_a_kidx, cache_b_kv, cache_b_logf, cache_c_cmp_kv, cache_c_slc_kv, state_c_win_kv, cache_d_kv, page_table, a_w_in, a_w_out, b_w_in, b_b_f, b_w_out, c_w_in, c_b_gate, c_cmp_pe, c_cmp_w1, c_cmp_w2, c_w_out, d_w_in, d_w_out, ln_g, ln_b, ffn_w_gu, ffn_w_down):
    bsz, t, d = x_prompt.shape
    s, tn, _ = x_sample.shape
    n_pages = page_table.shape[1]
    past = n_pages * PAGE_SIZE
    n_pg = _pick_pages(n_pages)
    cs_p = _rope_tables(jnp.arange(t, dtype=I32))
    cs_s = _rope_tables(jnp.tile(past + jnp.arange(tn, dtype=I32), s))
    xp = x_prompt.reshape(bsz * t, d)
    xs = x_sample.reshape(s * tn, d)
    cmp_w = _nsa_weights(c_cmp_pe, c_cmp_w1, c_cmp_w2)
    w_out = [_cast_w(w) for w in (a_w_out, b_w_out, c_w_out, d_w_out)]

    op, a_kv_p, a_kidx_p = _dsa_prompt(xp, bsz, t, _cast_w(a_w_in), cs_p)
    os_, a_kv_s, a_kidx_s = _dsa_sample(xs, s, tn, _cast_w(a_w_in), cs_s, page_table, cache_a_kv, cache_a_kidx, n_pg)

    def finish(i, xp, xs, op, os_):
        xp = _out_ln(op, w_out[i], xp, ln_g[i, 0], ln_b[i, 0])
        xs = _out_ln(os_, w_out[i], xs, ln_g[i, 0], ln_b[i, 0])
        wgu, wd = ffn_w_gu[i].astype(BF16), ffn_w_down[i].astype(BF16)
        xp = _ffn_ln(xp, wgu, wd, ln_g[i, 1], ln_b[i, 1])
        xs = _ffn_ln(xs, wgu, wd, ln_g[i, 1], ln_b[i, 1])
        return xp, xs

    xp, xs = finish(0, xp, xs, op, os_)

    op, b_kv_p, b_logf_p = _fox_prompt(xp, bsz, t, _cast_w(b_w_in), b_b_f, cs_p)
    os_, b_kv_s, b_logf_s = _fox_sample(xs, s, tn, _cast_w(b_w_in), b_b_f, cs_s, page_table, cache_b_kv, cache_b_logf, n_pg)
    xp, xs = finish(1, xp, xs, op, os_)

    op, c_cmp_kv_p, c_slc_kv_p, c_win_kv_p = _nsa_prompt(xp, bsz, t, _cast_w(c_w_in), c_b_gate, cmp_w, cs_p)
    os_, c_cmp_kv_s, c_slc_kv_s, c_win_kv_s = _nsa_sample(xs, s, tn, _cast_w(c_w_in), c_b_gate, cmp_w, cs_s, page_table,
                                                          cache_c_cmp_kv, cache_c_slc_kv, state_c_win_kv, n_pg)
    xp, xs = finish(2, xp, xs, op, os_)

    op, d_kv_p = _moba_prompt(xp, bsz, t, _cast_w(d_w_in), cs_p)
    os_, d_kv_s = _moba_sample(xs, s, tn, _cast_w(d_w_in), cs_s, page_table, cache_d_kv, n_pg)
    xp, xs = finish(3, xp, xs, op, os_)

    return (xp.reshape(bsz, t, d), xs.reshape(s, tn, d), a_kv_p, a_kv_s, a_kidx_p, a_kidx_s, b_kv_p, b_kv_s,
            b_logf_p, b_logf_s, c_cmp_kv_p, c_cmp_kv_s, c_slc_kv_p, c_slc_kv_s, c_win_kv_p, c_win_kv_s, d_kv_p, d_kv_s)
```

```python
import functools

import numpy as np
import jax
import jax.numpy as jnp
from jax import lax
from jax.experimental import pallas as pl
from jax.experimental.pallas import tpu as pltpu

F32 = jnp.float32
BF16 = jnp.bfloat16
I32 = jnp.int32

N_HEADS = 16
HEAD_DIM = 64
N_KV_HEADS = 4
GROUP = N_HEADS // N_KV_HEADS
Q_DIM = N_HEADS * HEAD_DIM
KV_COLS = N_KV_HEADS * HEAD_DIM
DEPTH = 4
PAGE_SIZE = 128
ROPE_THETA = 10000.0
LN_EPS = 1e-5
ALPHA = (2 * DEPTH) ** 0.25
ATTN_SCALE = HEAD_DIM ** -0.5
LOG2E = 1.4426950408889634
IDX_HEADS = 8
IDX_DIM = 64
IDX_TOPK = 256
IDX_SCALE = (IDX_HEADS * IDX_DIM) ** -0.5
CMP_LEN = 32
CMP_STRIDE = 16
CMP_HID = 2 * HEAD_DIM
SEL_BLOCK = 64
SEL_TOPN = 16
WINDOW = 512
MOBA_BLOCK = 256
MOBA_TOPK = 3

LANES = 128
SUBLANES = 8
ROW_TILE = 512
FFN_ROW_TILE = 512
FFN_CHUNK = 256
Q_TILE = 128
K_TILE = 512
VMEM_LIMIT = 56 * 2 ** 20
NEG = -1e30
KEY_NEG_INF = -2139095041
KEY_POS_INF = 2139095040
INT_MIN = -2 ** 31


def _params(*sem):
    return pltpu.CompilerParams(dimension_semantics=sem, vmem_limit_bytes=VMEM_LIMIT)


def _dot_t(a, b):
    return lax.dot_general(a, b, (((1,), (1,)), ((), ())), preferred_element_type=F32)


def _dot(a, b):
    return jnp.dot(a, b, preferred_element_type=F32)


def _dot_hp(a, b):
    hi = a.astype(BF16)
    r1 = a - hi.astype(F32)
    mid = r1.astype(BF16)
    lo = (r1 - mid.astype(F32)).astype(BF16)
    return _dot(hi, b) + _dot(mid, b) + _dot(lo, b)


def _proj_kernel(x_ref, w_ref, cos_ref, sin_ref, *out_refs, segs):
    acc = _dot(x_ref[...].astype(BF16), w_ref[...])
    tm = acc.shape[0]
    lane = lax.broadcasted_iota(I32, (tm, LANES), 1)
    first_half = (lane & (HEAD_DIM - 1)) < HEAD_DIM // 2
    ones_col = jnp.where(lane == HEAD_DIM, 1.0, 0.0)
    for (c0, width, rope, scale, form), o_ref in zip(segs, out_refs):
        if c0 % LANES:
            o_ref[...] = (acc[:, c0:c0 + width] * scale).astype(o_ref.dtype)
            continue
        for j in range(-(-width // LANES)):
            x = acc[:, c0 + j * LANES:c0 + (j + 1) * LANES]
            if j * LANES < (width if rope is True else int(rope)):
                swapped = jnp.where(first_half, pltpu.roll(x, LANES - HEAD_DIM // 2, 1), pltpu.roll(x, HEAD_DIM // 2, 1))
                x = x * cos_ref[...] + swapped * sin_ref[...]
            if scale != 1.0:
                x = x * scale
            if form == "tok":
                wj = min(LANES, width - j * LANES)
                o_ref[:, j * LANES:j * LANES + wj] = x[:, :wj].astype(o_ref.dtype)
            elif form == "heads":
                o_ref[0, 2 * j] = x[:, :HEAD_DIM].astype(o_ref.dtype)
                o_ref[0, 2 * j + 1] = x[:, HEAD_DIM:].astype(o_ref.dtype)
            elif form == "keys_t":
                xt = x.T
                o_ref[0, 2 * j] = xt[:HEAD_DIM].astype(o_ref.dtype)
                o_ref[0, 2 * j + 1] = xt[HEAD_DIM:].astype(o_ref.dtype)
            else:
                o_ref[0, 2 * j] = jnp.where(lane < HEAD_DIM, x, ones_col).astype(o_ref.dtype)
                o_ref[0, 2 * j + 1] = jnp.where(lane < HEAD_DIM, pltpu.roll(x, HEAD_DIM, 1), ones_col).astype(o_ref.dtype)


def _proj(x, w, cs, segs, bsz=None):
    m, k = x.shape
    n = w.shape[1]
    tm = min(m, ROW_TILE)
    cos_t, sin_t = cs
    r_blocks = cos_t.shape[0] // tm
    tab = pl.BlockSpec((tm, LANES), lambda i: (i % r_blocks, 0))
    specs, shapes = [], []
    for _, width, _, _, dt, form in segs:
        nh = width // HEAD_DIM
        if form == "tok":
            specs.append(pl.BlockSpec((tm, width), lambda i: (i, 0)))
            shapes.append(jax.ShapeDtypeStruct((m, width), dt))
            continue
        t = m // bsz
        tpb = t // tm
        if form == "keys_t":
            specs.append(pl.BlockSpec((1, nh, HEAD_DIM, tm), lambda i: (i // tpb, 0, 0, i % tpb)))
            shapes.append(jax.ShapeDtypeStruct((bsz, nh, HEAD_DIM, t), dt))
        else:
            last = HEAD_DIM if form == "heads" else LANES
            specs.append(pl.BlockSpec((1, nh, tm, last), lambda i: (i // tpb, 0, i % tpb, 0)))
            shapes.append(jax.ShapeDtypeStruct((bsz, nh, t, last), dt))
    return pl.pallas_call(
        functools.partial(_proj_kernel, segs=tuple(s[:4] + (s[5],) for s in segs)),
        grid=(m // tm,),
        in_specs=[pl.BlockSpec((tm, k), lambda i: (i, 0)), pl.BlockSpec((k, n), lambda i: (0, 0)), tab, tab],
        out_specs=specs,
        out_shape=shapes,
        compiler_params=_params("parallel"),
        name="in_proj",
    )(x, w, cos_t, sin_t)


def _layer_norm(y, g, b):
    mu = jnp.mean(y, axis=-1, keepdims=True)
    d = y - mu
    var = jnp.mean(d * d, axis=-1, keepdims=True)
    return d * lax.rsqrt(var + LN_EPS) * g + b


def _out_ln_kernel(o_ref, w_ref, x_ref, g_ref, b_ref, y_ref):
    y = ALPHA * x_ref[...] + _dot(o_ref[...].astype(BF16), w_ref[...])
    y_ref[...] = _layer_norm(y, g_ref[...], b_ref[...])


def _out_ln_gated_kernel(gate_ref, oc_ref, os_ref, ow_ref, w_ref, x_ref, g_ref, b_ref, y_ref):
    gate = gate_ref[...]
    heads = []
    for h in range(N_HEADS):
        cols = slice(h * HEAD_DIM, (h + 1) * HEAD_DIM)
        heads.append(gate[:, h:h + 1] * oc_ref[:, cols] + gate[:, N_HEADS + h:N_HEADS + h + 1] * os_ref[:, cols]
                     + gate[:, 2 * N_HEADS + h:2 * N_HEADS + h + 1] * ow_ref[:, cols])
    o = jnp.concatenate(heads, axis=1)
    y = ALPHA * x_ref[...] + _dot(o.astype(BF16), w_ref[...])
    y_ref[...] = _layer_norm(y, g_ref[...], b_ref[...])


def _out_ln(o, w, x, g, b):
    m, d = x.shape
    tm = min(m, ROW_TILE)
    row = lambda i: (i, 0)
    fix = lambda i: (0, 0)
    gated = isinstance(o, tuple)
    o_in = list(o) if gated else [o]
    return pl.pallas_call(
        _out_ln_gated_kernel if gated else _out_ln_kernel,
        grid=(m // tm,),
        in_specs=[pl.BlockSpec((tm, a.shape[1]), row) for a in o_in]
        + [pl.BlockSpec(w.shape, fix), pl.BlockSpec((tm, d), row), pl.BlockSpec((1, d), fix), pl.BlockSpec((1, d), fix)],
        out_specs=pl.BlockSpec((tm, d), row),
        out_shape=jax.ShapeDtypeStruct((m, d), F32),
        compiler_params=_params("parallel"),
        name="out_proj_ln",
    )(*o_in, w, x, g.reshape(1, d), b.reshape(1, d))


def _ffn_ln_kernel(x_ref, wgu_ref, wd_ref, g_ref, b_ref, y_ref, *, d_ff, chunk):
    x = x_ref[...]
    xb = x.astype(BF16)
    acc = jnp.zeros(x.shape, F32)
    for c in range(d_ff // chunk):
        gate = _dot(xb, wgu_ref[:, c * chunk:(c + 1) * chunk])
        up = _dot(xb, wgu_ref[:, d_ff + c * chunk:d_ff + (c + 1) * chunk])
        h = gate * (1.0 / (1.0 + jnp.exp(-gate))) * up
        acc = acc + _dot(h.astype(BF16), wd_ref[c * chunk:(c + 1) * chunk, :])
    y_ref[...] = _layer_norm(ALPHA * x + acc, g_ref[...], b_ref[...])


def _ffn_ln(x, wgu, wd, g, b):
    m, d = x.shape
    d_ff = wd.shape[0]
    tm = min(m, FFN_ROW_TILE)
    row = lambda i: (i, 0)
    fix = lambda i: (0, 0)
    return pl.pallas_call(
        functools.partial(_ffn_ln_kernel, d_ff=d_ff, chunk=FFN_CHUNK),
        grid=(m // tm,),
        in_specs=[pl.BlockSpec((tm, d), row), pl.BlockSpec((d, 2 * d_ff), fix), pl.BlockSpec((d_ff, d), fix),
                  pl.BlockSpec((1, d), fix), pl.BlockSpec((1, d), fix)],
        out_specs=pl.BlockSpec((tm, d), row),
        out_shape=jax.ShapeDtypeStruct((m, d), F32),
        compiler_params=_params("parallel"),
        name="ffn_ln",
    )(x, wgu, wd, g.reshape(1, d), b.reshape(1, d))


def _sortable(x):
    x = jnp.where(x == 0.0, 0.0, x)
    b = lax.bitcast_convert_type(x, I32)
    return b ^ ((b >> 31) & I32(0x7FFFFFFF))


def _kth_largest_key(u_ref, k):
    rows = u_ref.shape[0]

    def count_ge(cand):
        return jnp.sum((u_ref[...] >= cand).astype(I32), axis=1, keepdims=True)

    base = jnp.where(count_ge(jnp.zeros((rows, 1), I32)) >= k, I32(0), I32(INT_MIN))

    def body(i, base):
        cand = base | jnp.left_shift(I32(1), 30 - i)
        return jnp.where(count_ge(cand) >= k, cand, base)

    return lax.fori_loop(0, 31, body, base)


def _emit_selection(u_ref, k, write):
    rows, n = u_ref.shape
    thr = _kth_largest_key(u_ref, k)
    n_gt = jnp.sum((u_ref[...] > thr).astype(I32), axis=1, keepdims=True)
    need = (k - n_gt).astype(F32)
    r_i = lax.broadcasted_iota(I32, (LANES, LANES), 0)
    c_i = lax.broadcasted_iota(I32, (LANES, LANES), 1)
    tri = jnp.where(r_i <= c_i, 1.0, 0.0).astype(BF16)
    carry = jnp.zeros((rows, 1), F32)
    for ci in range(n // LANES):
        u = u_ref[:, ci * LANES:(ci + 1) * LANES]
        tie = u == thr
        tie_f = jnp.where(tie, 1.0, 0.0)
        inc = _dot(tie_f.astype(BF16), tri)
        rank = carry + inc - tie_f
        sel = (u > thr) | (tie & (rank < need))
        write(ci, sel, u)
        carry = carry + inc[:, LANES - 1:LANES]


def _topk_mask_cols(u_ref, k):
    n, r = u_ref.shape

    def count(pred):
        return jnp.sum(pred(u_ref[...]).astype(I32), axis=0, keepdims=True)

    base = jnp.where(count(lambda u: u >= 0) >= k, I32(0), I32(INT_MIN))

    def radix(i, base):
        cand = base | jnp.left_shift(I32(1), 30 - i)
        return jnp.where(count(lambda u: u >= cand) >= k, cand, base)

    thr = lax.fori_loop(0, 31, radix, base)
    need = (k - count(lambda u: u > thr)).astype(F32)
    u = u_ref[...]
    tie = u == thr
    lower = jnp.where(lax.broadcasted_iota(I32, (n, n), 0) > lax.broadcasted_iota(I32, (n, n), 1), 1.0, 0.0)
    rank = _dot(lower.astype(BF16), jnp.where(tie, 1.0, 0.0).astype(BF16))
    return (u > thr) | (tie & (rank < need)), u


def _flash_prompt_kernel(*refs, kind, tq, tk, n_extra):
    q_ref, k_ref, v_ref = refs[:3]
    extra = refs[3:3 + n_extra]
    o_ref, m_ref, acc_ref, s_ref = refs[3 + n_extra:]
    g = pl.program_id(1)
    q0 = pl.program_id(2) * tq
    rows = GROUP * tq
    q = q_ref[0, 0].reshape(rows, HEAD_DIM)
    m_ref[...] = jnp.full(m_ref.shape, NEG, F32)
    acc_ref[...] = jnp.zeros(acc_ref.shape, F32)
    c_diag = q0 // tk
    c_lo = jnp.maximum(q0 - WINDOW, 0) // tk if kind == "nsa_win" else 0
    if kind == "nsa_sel":
        selb = extra[0][0, 0].astype(BF16)
    if kind == "moba":
        selb = extra[0][0, 0].reshape(rows, extra[0].shape[-1]).astype(BF16)

    def scores(c):
        return _dot(q, k_ref[0, g, :, pl.ds(pl.multiple_of(c * tk, tk), tk)])

    def chunk(c, diag):
        if kind in ("nsa_sel", "moba"):
            hit = _dot(selb, extra[1][c]) > 0.5
        s3 = s_ref[c % 2].reshape(GROUP, tq, tk)
        if not diag:
            s_ref[(c + 1) % 2] = scores(c + 1)
        start = pl.multiple_of(c * tk, tk)
        v = v_ref[0, g, pl.ds(start, tk), :]
        ok = None
        if diag or kind == "nsa_win":
            t_idx = q0 + lax.broadcasted_iota(I32, (GROUP, tq, tk), 1)
            s_idx = start + lax.broadcasted_iota(I32, (GROUP, tq, tk), 2)
            causal = s_idx <= t_idx
        if kind == "fox":
            ck = extra[0][0, 0, :, pl.ds(start, tk)]
            s3 = s3 - ck[:, None, :]
            ok = causal if diag else None
        elif kind == "dsa":
            msk = extra[0][0, :, pl.ds(start, tk)]
            ok = jnp.broadcast_to((msk > 0)[None], (GROUP, tq, tk))
        elif kind == "nsa_sel":
            ok = jnp.broadcast_to(hit[None], (GROUP, tq, tk))
            ok = (ok & causal) if diag else ok
        elif kind == "nsa_win":
            ok = causal & (s_idx >= t_idx - WINDOW)
        else:
            ok = hit.reshape(GROUP, tq, tk)
            if diag:
                ok = ok | (causal & ((s_idx // MOBA_BLOCK) == (t_idx // MOBA_BLOCK)))
        if ok is not None:
            s3 = jnp.where(ok, s3, NEG)
        s = s3.reshape(rows, tk)
        m_prev = m_ref[...]
        m_new = jnp.maximum(m_prev, jnp.max(s, axis=1, keepdims=True))
        alpha = jnp.exp2(m_prev - m_new)
        p = jnp.exp2(s - pltpu.repeat(m_new, tk // LANES, axis=1))
        acc_ref[...] = acc_ref[...] * alpha + _dot(p.astype(BF16), v)
        m_ref[...] = m_new

    def body(c, carry):
        chunk(c, False)
        return carry

    s_ref[c_lo % 2] = scores(c_lo)
    lax.fori_loop(c_lo, c_diag, body, 0)
    chunk(c_diag, True)
    acc = acc_ref[...]
    o = acc[:, :HEAD_DIM] / jnp.maximum(acc[:, HEAD_DIM:HEAD_DIM + 1], 1e-30)
    o_ref[0] = jnp.concatenate([o[j * tq:(j + 1) * tq] for j in range(GROUP)], axis=1)


def _flash_prompt(kind, q, k, v, extra, tq=Q_TILE, tk=K_TILE):
    bsz, _, _, t, _ = q.shape
    tk = min(tk, t)
    tq = min(tq, t)
    qspec = pl.BlockSpec((1, 1, GROUP, tq, HEAD_DIM), lambda b, g, i: (b, g, 0, i, 0))
    kvspec = pl.BlockSpec((1, N_KV_HEADS, HEAD_DIM, t), lambda b, g, i: (b, 0, 0, 0))
    vspec = pl.BlockSpec((1, N_KV_HEADS, t, LANES), lambda b, g, i: (b, 0, 0, 0))
    if kind == "fox":
        especs = [pl.BlockSpec((1, 1, GROUP, t), lambda b, g, i: (b, g, 0, 0))]
    elif kind == "dsa":
        especs = [pl.BlockSpec((1, tq, t), lambda b, g, i: (b, i, 0))]
    elif kind == "nsa_sel":
        especs = [pl.BlockSpec((1, 1, tq, extra[0].shape[-1]), lambda b, g, i: (b, g, i, 0)),
                  pl.BlockSpec(extra[1].shape, lambda b, g, i: (0, 0, 0))]
    elif kind == "moba":
        especs = [pl.BlockSpec((1, 1, GROUP, tq, extra[0].shape[-1]), lambda b, g, i: (b, g, 0, i, 0)),
                  pl.BlockSpec(extra[1].shape, lambda b, g, i: (0, 0, 0))]
    else:
        especs = []
    rows = GROUP * tq
    return pl.pallas_call(
        functools.partial(_flash_prompt_kernel, kind=kind, tq=tq, tk=tk, n_extra=len(extra)),
        grid=(bsz, N_KV_HEADS, t // tq),
        in_specs=[qspec, kvspec, vspec] + especs,
        out_specs=pl.BlockSpec((1, tq, GROUP * HEAD_DIM), lambda b, g, i: (b, i, g)),
        out_shape=jax.ShapeDtypeStruct((bsz, t, Q_DIM), F32),
        scratch_shapes=[pltpu.VMEM((rows, LANES), F32), pltpu.VMEM((rows, LANES), F32),
                        pltpu.VMEM((2, rows, tk), F32)],
        compiler_params=_params("parallel", "parallel", "parallel"),
        name="flash_prompt_" + kind,
    )(q, k, v, *extra)


def _expand_matrix(n_blocks_padded, block, t, tk):
    s = np.arange(t)
    e = (s[None, :] // block == np.arange(n_blocks_padded)[:, None]).astype(np.float32)
    e = e.reshape(n_blocks_padded, t // tk, tk).transpose(1, 0, 2)
    return jnp.asarray(e, dtype=BF16)


def _dsa_select_prompt_kernel(qi_ref, ki_ref, wi_ref, mask_ref, u_ref, *, tq, t, tk, topk):
    q0 = pl.program_id(1) * tq
    n_act = q0 // tk + 1
    sub = tk // LANES
    w = wi_ref[0]
    t_idx = q0 + lax.broadcasted_iota(I32, (tq, tk), 0)
    k_off = lax.broadcasted_iota(I32, (tq, tk), 1)

    def chunk_at(c):
        return pl.ds(pl.multiple_of(c * tk, tk), tk)

    def fill(c, carry):
        kc = ki_ref[0, chunk_at(c), :]
        s = jnp.zeros((tq, tk), F32)
        for h in range(IDX_HEADS):
            s = s + jnp.maximum(_dot_t(qi_ref[0, h], kc), 0.0) * w[:, h:h + 1]
        s = jnp.where(c * tk + k_off <= t_idx, s, -jnp.inf)
        u_ref[:, chunk_at(c)] = _sortable(s)
        return carry

    lax.fori_loop(0, n_act, fill, 0)

    def count(pred):
        def body(c, acc):
            hit = pred(u_ref[:, chunk_at(c)]).astype(I32)
            for i in range(sub):
                acc = acc + hit[:, i * LANES:(i + 1) * LANES]
            return acc
        acc = lax.fori_loop(0, n_act, body, jnp.zeros((tq, LANES), I32))
        return jnp.sum(acc, axis=1, keepdims=True)

    base = jnp.where(count(lambda u: u >= 0) >= topk, I32(0), I32(INT_MIN))

    def radix(i, base):
        cand = base | jnp.left_shift(I32(1), 30 - i)
        return jnp.where(count(lambda u: u >= cand) >= topk, cand, base)

    thr = lax.fori_loop(0, 31, radix, base)
    need = (topk - count(lambda u: u > thr)).astype(F32)
    r_i = lax.broadcasted_iota(I32, (LANES, LANES), 0)
    c_i = lax.broadcasted_iota(I32, (LANES, LANES), 1)
    tri = jnp.where(r_i <= c_i, 1.0, 0.0).astype(BF16)

    def emit(c, carry):
        for i in range(sub):
            at = pl.ds(pl.multiple_of(c * tk + i * LANES, LANES), LANES)
            u = u_ref[:, at]
            tie = u == thr
            tie_f = jnp.where(tie, 1.0, 0.0)
            inc = _dot(tie_f.astype(BF16), tri)
            sel = (u > thr) | (tie & (carry + inc - tie_f < need))
            keep = sel & (u > KEY_NEG_INF) & (u < KEY_POS_INF)
            mask_ref[0, :, at] = jnp.where(keep, 1.0, 0.0).astype(BF16)
            carry = carry + inc[:, LANES - 1:LANES]
        return carry

    lax.fori_loop(0, n_act, emit, jnp.zeros((tq, 1), F32))

    def clear(c, carry):
        mask_ref[0, :, chunk_at(c)] = jnp.zeros((tq, tk), BF16)
        return carry

    lax.fori_loop(n_act, t // tk, clear, 0)


def _dsa_select_prompt(qi, ki, wi, tq=2 * Q_TILE, tk=K_TILE):
    bsz, _, t, _ = qi.shape
    tq, tk = min(tq, t), min(tk, t)
    topk = min(IDX_TOPK, t // 4)
    return pl.pallas_call(
        functools.partial(_dsa_select_prompt_kernel, tq=tq, t=t, tk=tk, topk=topk),
        grid=(bsz, t // tq),
        in_specs=[pl.BlockSpec((1, IDX_HEADS, tq, IDX_DIM), lambda b, i: (b, 0, i, 0)),
                  pl.BlockSpec((1, t, IDX_DIM), lambda b, i: (b, 0, 0)),
                  pl.BlockSpec((1, tq, IDX_HEADS), lambda b, i: (b, i, 0))],
        out_specs=pl.BlockSpec((1, tq, t), lambda b, i: (b, i, 0)),
        out_shape=jax.ShapeDtypeStruct((bsz, t, t), BF16),
        scratch_shapes=[pltpu.VMEM((tq, t), I32)],
        compiler_params=_params("parallel", "parallel"),
        name="dsa_select_prompt",
    )(qi, ki, wi)


def _dsa_select_sample_kernel(pt_ref, qi_ref, wi_ref, *refs, n_pg, n_steps, tpad, n_new, topk, nb):
    pages = refs[:nb * n_pg]
    new_ref, mask_ref, u_ref = refs[nb * n_pg:]
    j = pl.program_id(1)

    def scores(bi, kc):
        rel = jnp.maximum(_dot(qi_ref[bi], kc.astype(BF16)), 0.0) * wi_ref[bi]
        return rel.reshape(tpad, IDX_HEADS, LANES).sum(axis=1)

    @pl.when(j < n_steps - 1)
    def _():
        for bi in range(nb):
            for i in range(n_pg):
                start = pl.multiple_of((j * n_pg + i) * LANES, LANES)
                u_ref[bi * tpad:(bi + 1) * tpad, pl.ds(start, LANES)] = _sortable(scores(bi, pages[bi * n_pg + i][0]))

    @pl.when(j == n_steps - 1)
    def _():
        t_idx = lax.broadcasted_iota(I32, (tpad, LANES), 0)
        c_idx = lax.broadcasted_iota(I32, (tpad, LANES), 1)
        base = (n_steps - 1) * n_pg * LANES
        for bi in range(nb):
            s = jnp.where((c_idx <= t_idx) & (c_idx < n_new), scores(bi, new_ref[bi]), -jnp.inf)
            u_ref[bi * tpad:(bi + 1) * tpad, base:base + LANES] = _sortable(s)
        for i in range(1, n_pg):
            u_ref[:, base + i * LANES:base + (i + 1) * LANES] = jnp.full((nb * tpad, LANES), KEY_NEG_INF, I32)

        def write(ci, sel, u):
            keep = sel & (u > KEY_NEG_INF) & (u < KEY_POS_INF)
            mask_ref[:, :, ci * LANES:(ci + 1) * LANES] = jnp.where(keep, 1.0, 0.0).reshape(nb, tpad, LANES)

        _emit_selection(u_ref, topk, write)


def _dsa_select_sample(page_table, qi, wi, pool_kidx, new_ki, n_new, n_pg=8):
    bsz, n_pages = page_table.shape
    tpad = qi.shape[1] // IDX_HEADS
    n_steps = n_pages // n_pg + 1
    width = n_steps * n_pg * LANES
    topk = min(IDX_TOPK, (n_pages * PAGE_SIZE + n_new) // 4)
    nb = _batch_rows(bsz)
    fix = lambda b, j, pt: (b, 0, 0)

    def page_map(bi, i):
        return lambda b, j, pt: (pt[b * nb + bi, jnp.minimum(j * n_pg + i, n_pages - 1)], 0, 0)

    grid_spec = pltpu.PrefetchScalarGridSpec(
        num_scalar_prefetch=1,
        grid=(bsz // nb, n_steps),
        in_specs=[pl.BlockSpec((nb, tpad * IDX_HEADS, IDX_DIM), fix), pl.BlockSpec((nb, tpad * IDX_HEADS, 1), fix)]
        + [pl.BlockSpec((1, IDX_DIM, PAGE_SIZE), page_map(bi, i)) for bi in range(nb) for i in range(n_pg)]
        + [pl.BlockSpec((nb, IDX_DIM, PAGE_SIZE), fix)],
        out_specs=pl.BlockSpec((nb, tpad, width), fix),
        scratch_shapes=[pltpu.VMEM((nb * tpad, width), I32)],
    )
    return pl.pallas_call(
        functools.partial(_dsa_select_sample_kernel, n_pg=n_pg, n_steps=n_steps, tpad=tpad, n_new=n_new, topk=topk, nb=nb),
        grid_spec=grid_spec,
        out_shape=jax.ShapeDtypeStruct((bsz, tpad, width), F32),
        compiler_params=_params("parallel", "arbitrary"),
        name="dsa_select_sample",
    )(page_table, qi, wi, *([pool_kidx] * (nb * n_pg)), new_ki)


def _batch_rows(bsz):
    return 4 if bsz % 4 == 0 else (2 if bsz % 2 == 0 else 1)


def _page_map(i, n_pg, n_pages):
    return lambda b, j, pt: (pt[b, jnp.minimum(j * n_pg + i, n_pages - 1)], 0, 0)


def _page_map4(i, n_pg, n_pages):
    return lambda b, j, pt: (pt[b, jnp.minimum(j * n_pg + i, n_pages - 1)], 0, 0, 0)


def _paged_flash_kernel(pt_ref, q_ref, *refs, n_pg, n_steps, nb):
    pages = refs[:nb * n_pg]
    new_ref, bias_ref, o_ref, m_ref, l_ref, acc_ref = refs[nb * n_pg:]
    j = pl.program_id(1)

    @pl.when(j == 0)
    def _():
        m_ref[...] = jnp.full(m_ref.shape, NEG, F32)
        l_ref[...] = jnp.zeros(l_ref.shape, F32)
        acc_ref[...] = jnp.zeros(acc_ref.shape, F32)

    def update(bi, kv):
        q = q_ref[bi]
        s = [_dot(q, k().astype(BF16)) + bias_ref[bi, :, i * LANES:(i + 1) * LANES].astype(F32)
             for i, (k, _) in enumerate(kv)]
        m_prev = m_ref[bi]
        m_new = m_prev
        for si in s:
            m_new = jnp.maximum(m_new, jnp.max(si, axis=1, keepdims=True))
        alpha = jnp.exp(m_prev - m_new)
        l_new = alpha * l_ref[bi]
        acc = acc_ref[bi] * pltpu.repeat(alpha, KV_COLS // LANES, axis=1)
        for si, (_, v) in zip(s, kv):
            p = jnp.where(si > 0.5 * NEG, jnp.exp(si - m_new), 0.0)
            l_new = l_new + jnp.sum(p, axis=1, keepdims=True)
            acc = acc + _dot_t(p.astype(BF16), v().astype(BF16))
        l_ref[bi] = l_new
        acc_ref[bi] = acc
        m_ref[bi] = m_new

    def loaders(ref, lead):
        return (lambda: ref[lead, 0]), (lambda: ref[lead, 1])

    @pl.when(j < n_steps - 1)
    def _():
        for bi in range(nb):
            update(bi, [loaders(pages[bi * n_pg + i], 0) for i in range(n_pg)])

    @pl.when(j == n_steps - 1)
    def _():
        for bi in range(nb):
            update(bi, [loaders(new_ref, bi)])
        o_ref[...] = acc_ref[...] / jnp.maximum(jnp.concatenate([l_ref[...]] * (KV_COLS // LANES), axis=-1), 1e-30)


def _paged_flash(page_table, q_bd, pool_t, new_page_t, bias, n_pg):
    bsz, n_pages = page_table.shape
    n_steps = n_pages // n_pg + 1
    rows = q_bd.shape[1]
    nb = _batch_rows(bsz)
    fix = lambda b, j, pt: (b, 0, 0)
    fix4 = lambda b, j, pt: (b, 0, 0, 0)
    page_block = (1, 2, KV_COLS, PAGE_SIZE)

    def page_map(bi, i):
        return lambda b, j, pt: (pt[b * nb + bi, jnp.minimum(j * n_pg + i, n_pages - 1)], 0, 0, 0)

    grid_spec = pltpu.PrefetchScalarGridSpec(
        num_scalar_prefetch=1,
        grid=(bsz // nb, n_steps),
        in_specs=[pl.BlockSpec((nb, rows, KV_COLS), fix)]
        + [pl.BlockSpec(page_block, page_map(bi, i)) for bi in range(nb) for i in range(n_pg)]
        + [pl.BlockSpec((nb, 2, KV_COLS, PAGE_SIZE), fix4),
           pl.BlockSpec((nb, rows, n_pg * LANES), lambda b, j, pt: (b, 0, j))],
        out_specs=pl.BlockSpec((nb, rows, KV_COLS), fix),
        scratch_shapes=[pltpu.VMEM((nb, rows, LANES), F32), pltpu.VMEM((nb, rows, LANES), F32),
                        pltpu.VMEM((nb, rows, KV_COLS), F32)],
    )
    return pl.pallas_call(
        functools.partial(_paged_flash_kernel, n_pg=n_pg, n_steps=n_steps, nb=nb),
        grid_spec=grid_spec,
        out_shape=jax.ShapeDtypeStruct((bsz, rows, KV_COLS), F32),
        compiler_params=_params("parallel", "arbitrary"),
        name="paged_flash",
    )(page_table, q_bd, *([pool_t] * (nb * n_pg)), new_page_t, bias)


def _cumsum_kernel(pt_ref, *refs, n_pg, n_steps, nb):
    pages = refs[:nb * n_pg]
    new_ref, o_ref, carry_ref = refs[nb * n_pg:]
    j = pl.program_id(1)
    r_i = lax.broadcasted_iota(I32, (LANES, LANES), 0)
    c_i = lax.broadcasted_iota(I32, (LANES, LANES), 1)
    tri = jnp.where(r_i <= c_i, 1.0, 0.0).astype(BF16)

    @pl.when(j == 0)
    def _():
        carry_ref[...] = jnp.zeros(carry_ref.shape, F32)

    ones = jnp.ones((LANES, LANES), BF16)

    def steps(bi, xs):
        local = [_dot_hp(x, tri) for x in xs]
        total = [_dot_hp(x, ones) for x in xs]
        carry = carry_ref[bi]
        for i in range(len(xs)):
            o_ref[bi, :, i * LANES:(i + 1) * LANES] = local[i] + carry
            carry = carry + total[i]
        carry_ref[bi] = carry

    @pl.when(j < n_steps - 1)
    def _():
        for bi in range(nb):
            steps(bi, [pages[bi * n_pg + i][0] for i in range(n_pg)])

    @pl.when(j == n_steps - 1)
    def _():
        for bi in range(nb):
            steps(bi, [new_ref[bi]])
        for i in range(1, n_pg):
            o_ref[:, :, i * LANES:(i + 1) * LANES] = jnp.zeros((nb, N_HEADS, LANES), F32)


def _paged_cumsum(page_table, pool_t, new_t, n_pg):
    bsz, n_pages = page_table.shape
    n_steps = n_pages // n_pg + 1
    nb = _batch_rows(bsz)
    fix = lambda b, j, pt: (b, 0, 0)

    def page_map(bi, i):
        return lambda b, j, pt: (pt[b * nb + bi, jnp.minimum(j * n_pg + i, n_pages - 1)], 0, 0)

    grid_spec = pltpu.PrefetchScalarGridSpec(
        num_scalar_prefetch=1,
        grid=(bsz // nb, n_steps),
        in_specs=[pl.BlockSpec((1, N_HEADS, LANES), page_map(bi, i)) for bi in range(nb) for i in range(n_pg)]
        + [pl.BlockSpec((nb, N_HEADS, LANES), fix)],
        out_specs=pl.BlockSpec((nb, N_HEADS, n_pg * LANES), lambda b, j, pt: (b, 0, j)),
        scratch_shapes=[pltpu.VMEM((nb, N_HEADS, LANES), F32)],
    )
    return pl.pallas_call(
        functools.partial(_cumsum_kernel, n_pg=n_pg, n_steps=n_steps, nb=nb),
        grid_spec=grid_spec,
        out_shape=jax.ShapeDtypeStruct((bsz, N_HEADS, n_steps * n_pg * LANES), F32),
        compiler_params=_params("parallel", "arbitrary"),
        name="fox_cumsum",
    )(page_table, *([pool_t] * (nb * n_pg)), new_t)


def _gelu_tanh(x):
    return 0.5 * x * (1.0 + jnp.tanh(0.7978845608028654 * (x + 0.044715 * x * x * x)))


def _nsa_compress_kernel(pt_ref, *refs, n_pg, n_steps, nc):
    pages = refs[:n_pg]
    pe_ref, w1_ref, w2_ref, o_ref, x_ref = refs[n_pg:]
    j = pl.program_id(1)
    per = PAGE_SIZE // CMP_STRIDE
    n_chunk = x_ref.shape[1]
    for i in range(0, n_pg, 2):
        start = pl.multiple_of((j * n_pg + i) * per, 2 * per)
        for ck in range(2 * N_KV_HEADS):
            x_ref[ck, pl.ds(start, 2 * per), :] = jnp.concatenate([pages[i][0, ck], pages[i + 1][0, ck]], axis=0)

    @pl.when(j == n_steps - 1)
    def _():
        half = CMP_STRIDE * HEAD_DIM
        rows = N_KV_HEADS * n_chunk
        row = lax.broadcasted_iota(I32, (N_KV_HEADS, n_chunk, HEAD_DIM), 1)
        for c in range(2):
            w1 = w1_ref[c]
            part = _dot(x_ref[c * N_KV_HEADS:(c + 1) * N_KV_HEADS].reshape(rows, half), w1)
            pe = pe_ref[c]
            pe_term = _dot(pe[:, :half], w1)[:, :CMP_HID] + _dot(pe[:, half:], w1)[:, CMP_HID:]
            h = pe_term[0:1, :] + part[:, :CMP_HID] + pltpu.roll(part[:, CMP_HID:], rows - 1, 0)
            out = _dot(_gelu_tanh(h).astype(BF16), w2_ref[c]).reshape(N_KV_HEADS, n_chunk, HEAD_DIM)
            o_ref[0, c * N_KV_HEADS:(c + 1) * N_KV_HEADS] = jnp.where(row < nc, out, 0.0)


def _nsa_compress(page_table, pool_t, pe8, w1cat, w2, n_pg=8):
    bsz, n_pages = page_table.shape
    per = PAGE_SIZE // CMP_STRIDE
    n_chunk = n_pages * per
    nc = n_chunk - CMP_LEN // CMP_STRIDE + 1
    n_steps = n_pages // n_pg
    width = CMP_STRIDE * HEAD_DIM

    def page_map(i):
        return lambda b, j, pt: (pt[b, j * n_pg + i], 0, 0, 0)

    fix3 = lambda b, j, pt: (0, 0, 0)
    grid_spec = pltpu.PrefetchScalarGridSpec(
        num_scalar_prefetch=1,
        grid=(bsz, n_steps),
        in_specs=[pl.BlockSpec((1, 2 * N_KV_HEADS, per, width), page_map(i)) for i in range(n_pg)]
        + [pl.BlockSpec(pe8.shape, fix3), pl.BlockSpec(w1cat.shape, fix3), pl.BlockSpec(w2.shape, fix3)],
        out_specs=pl.BlockSpec((1, 2 * N_KV_HEADS, n_chunk, HEAD_DIM), lambda b, j, pt: (b, 0, 0, 0)),
        scratch_shapes=[pltpu.VMEM((2 * N_KV_HEADS, n_chunk, width), BF16)],
    )
    return pl.pallas_call(
        functools.partial(_nsa_compress_kernel, n_pg=n_pg, n_steps=n_steps, nc=nc),
        grid_spec=grid_spec,
        out_shape=jax.ShapeDtypeStruct((bsz, 2 * N_KV_HEADS, n_chunk, HEAD_DIM), F32),
        compiler_params=_params("parallel", "arbitrary"),
        name="nsa_compress",
    )(page_table, *([pool_t] * n_pg), pe8, w1cat, w2), nc


def _nsa_cmp_kernel(q_ref, ck_ref, cv_ref, cover_ref, o_ref, sel_ref, u_ref, *, tq, pos0, nc, n_sel):
    t0 = pos0 + pl.program_id(1) * tq
    ncp = ck_ref.shape[2]
    nsp = cover_ref.shape[0]
    rows = GROUP * tq
    n_idx = lax.broadcasted_iota(I32, (tq, ncp), 1)
    t_idx = t0 + lax.broadcasted_iota(I32, (tq, ncp), 0)
    c_ok = ((n_idx * CMP_STRIDE + CMP_LEN - 1 <= t_idx) & (n_idx < nc))[None]
    blk = lax.broadcasted_iota(I32, (nsp, tq), 0)
    cur = (t0 + lax.broadcasted_iota(I32, (nsp, tq), 1)) // SEL_BLOCK
    forced = (blk == 0) | (blk == cur) | (blk == cur - 1)
    cover_t = cover_ref[...]
    for g in range(N_KV_HEADS):
        q = q_ref[0, g].reshape(rows, HEAD_DIM)
        s3 = jnp.where(c_ok, _dot_t(q, ck_ref[0, g]).reshape(GROUP, tq, ncp), NEG)
        m = jnp.max(s3, axis=-1, keepdims=True)
        e = jnp.where(c_ok, jnp.exp(s3 - m), 0.0)
        p = e / jnp.maximum(jnp.sum(e, axis=-1, keepdims=True), 1e-30)
        o = _dot(p.reshape(rows, ncp).astype(BF16), cv_ref[0, g])
        o_ref[0, :, g * GROUP * HEAD_DIM:(g + 1) * GROUP * HEAD_DIM] = jnp.concatenate(
            [o[j * tq:(j + 1) * tq] for j in range(GROUP)], axis=1)
        psum = p[0] + p[1] + p[2] + p[3]
        hi = psum.astype(BF16)
        r1 = psum - hi.astype(F32)
        mid = r1.astype(BF16)
        lo = (r1 - mid.astype(F32)).astype(BF16)
        imp = _dot_t(cover_t, hi) + _dot_t(cover_t, mid) + _dot_t(cover_t, lo)
        imp = jnp.where(forced, jnp.inf, imp)
        imp = jnp.where(blk <= cur, imp, -jnp.inf)
        u_ref[:, g * tq:(g + 1) * tq] = _sortable(imp)
    sel, u = _topk_mask_cols(u_ref, n_sel)
    sel_ref[0, 0] = jnp.where(sel & (u > KEY_NEG_INF), 1.0, 0.0)


def _nsa_cmp_select(q, cmp_k, cmp_v, pos0, nc, n_keys, tq):
    bsz, _, _, t, _ = q.shape
    ncp = cmp_k.shape[2]
    ns = -(-n_keys // SEL_BLOCK)
    nsp = -(-ns // LANES) * LANES
    n_sel = min(SEL_TOPN, ns)
    c0 = np.arange(ncp)[:, None] * CMP_STRIDE
    s0 = np.arange(nsp)[None, :] * SEL_BLOCK
    cover = (c0 <= s0 + SEL_BLOCK - 1) & (c0 + CMP_LEN - 1 >= s0) & (np.arange(ncp)[:, None] < nc) & (np.arange(nsp)[None, :] < ns)
    cover_t = jnp.asarray(cover.T.astype(np.float32), dtype=BF16)
    nq = t // tq
    qspec = pl.BlockSpec((1, N_KV_HEADS, GROUP, tq, HEAD_DIM), lambda b, i: (b, 0, 0, i, 0))
    cspec = pl.BlockSpec((1, N_KV_HEADS, ncp, HEAD_DIM), lambda b, i: (b, 0, 0, 0))
    o_c, sel = pl.pallas_call(
        functools.partial(_nsa_cmp_kernel, tq=tq, pos0=pos0, nc=nc, n_sel=n_sel),
        grid=(bsz, nq),
        in_specs=[qspec, cspec, cspec, pl.BlockSpec((nsp, ncp), lambda b, i: (0, 0))],
        out_specs=[pl.BlockSpec((1, tq, Q_DIM), lambda b, i: (b, i, 0)),
                   pl.BlockSpec((1, 1, nsp, N_KV_HEADS * tq), lambda b, i: (b, i, 0, 0))],
        out_shape=[jax.ShapeDtypeStruct((bsz, t, Q_DIM), F32), jax.ShapeDtypeStruct((bsz, nq, nsp, N_KV_HEADS * tq), F32)],
        scratch_shapes=[pltpu.VMEM((nsp, N_KV_HEADS * tq), I32)],
        compiler_params=_params("parallel", "parallel"),
        name="nsa_cmp_select",
    )(q, cmp_k, cmp_v, cover_t)
    sel = sel.reshape(bsz, nq, nsp, N_KV_HEADS, tq).transpose(0, 3, 1, 4, 2).reshape(bsz, N_KV_HEADS, t, nsp)
    return o_c, sel


def _kmean_kernel(*refs):
    o_ref = refs[-1]
    tot = jnp.sum(refs[0][0], axis=0, keepdims=True)
    for r in refs[1:-1]:
        tot = tot + jnp.sum(r[0], axis=0, keepdims=True)
    o_ref[0, 0] = tot * (1.0 / MOBA_BLOCK)


def _kmean_prompt(kv):
    bsz, t, _ = kv.shape
    nb = t // MOBA_BLOCK
    return pl.pallas_call(
        _kmean_kernel,
        grid=(bsz, nb),
        in_specs=[pl.BlockSpec((1, MOBA_BLOCK, KV_COLS), lambda b, i: (b, i, 0))],
        out_specs=pl.BlockSpec((1, 1, 1, KV_COLS), lambda b, i: (b, i, 0, 0)),
        out_shape=jax.ShapeDtypeStruct((bsz, nb, 1, KV_COLS), F32),
        compiler_params=_params("parallel", "parallel"),
        name="kmean_prompt",
    )(kv)


def _kmean_sample_kernel(pt_ref, *refs, per, nb):
    o_ref = refs[-1]
    ones = jnp.ones((SUBLANES, PAGE_SIZE), BF16)
    blocks = len(refs[:-1]) // (per * nb)
    for n in range(nb * blocks):
        tot = jnp.zeros((SUBLANES, KV_COLS), F32)
        for r in refs[n * per:(n + 1) * per]:
            x = r[0, 0]
            hi = x.astype(BF16)
            r1 = x - hi.astype(F32)
            mid = r1.astype(BF16)
            lo = (r1 - mid.astype(F32)).astype(BF16)
            tot = tot + _dot_t(ones, hi) + _dot_t(ones, mid) + _dot_t(ones, lo)
        o_ref[n // blocks, n % blocks] = tot[0:1] * (1.0 / MOBA_BLOCK)


def _kmean_sample(page_table, pool_t, n_pg):
    bsz, n_pages = page_table.shape
    per = MOBA_BLOCK // PAGE_SIZE
    n_blocks = n_pages // per
    nb = _batch_rows(bsz)

    def page_map(bi, i):
        return lambda b, n, pt: (pt[b * nb + bi, n * n_pg + i], 0, 0, 0)

    grid_spec = pltpu.PrefetchScalarGridSpec(
        num_scalar_prefetch=1,
        grid=(bsz // nb, n_pages // n_pg),
        in_specs=[pl.BlockSpec((1, 1, KV_COLS, PAGE_SIZE), page_map(bi, i)) for bi in range(nb) for i in range(n_pg)],
        out_specs=pl.BlockSpec((nb, n_pg // per, 1, KV_COLS), lambda b, n, pt: (b, n, 0, 0)),
    )
    return pl.pallas_call(
        functools.partial(_kmean_sample_kernel, per=per, nb=nb),
        grid_spec=grid_spec,
        out_shape=jax.ShapeDtypeStruct((bsz, n_blocks, 1, KV_COLS), F32),
        compiler_params=_params("parallel", "parallel"),
        name="kmean_sample",
    )(page_table, *([pool_t] * (nb * n_pg)))


def _moba_select_kernel(q_ref, km_ref, sel_ref, *, tq, pos0, k_top):
    t0 = pos0 + pl.program_id(1) * tq
    nbp = km_ref.shape[2]
    rows = GROUP * tq
    blk = lax.broadcasted_iota(I32, (nbp, rows), 0)
    n_past = (t0 + (lax.broadcasted_iota(I32, (nbp, rows), 1) & (tq - 1))) // MOBA_BLOCK
    for g in range(N_KV_HEADS):
        q = q_ref[0, g].reshape(rows, HEAD_DIM)
        s = jnp.where(blk < n_past, _dot_t(km_ref[0, g], q), -jnp.inf)
        sel = jnp.zeros((nbp, rows), F32)
        for _ in range(k_top):
            m = jnp.max(s, axis=0, keepdims=True)
            first = jnp.min(jnp.where(s == m, blk, nbp), axis=0, keepdims=True)
            pick = blk == first
            sel = jnp.where(pick & (m > -jnp.inf), 1.0, sel)
            s = jnp.where(pick, -jnp.inf, s)
        sel_ref[0, 0, g] = sel


def _moba_select(q, kmean, pos0, nb, tq):
    bsz, _, _, t, _ = q.shape
    nbp = kmean.shape[2]
    nq = t // tq
    qspec = pl.BlockSpec((1, N_KV_HEADS, GROUP, tq, HEAD_DIM), lambda b, i: (b, 0, 0, i, 0))
    sel = pl.pallas_call(
        functools.partial(_moba_select_kernel, tq=tq, pos0=pos0, k_top=min(MOBA_TOPK, nb)),
        grid=(bsz, nq),
        in_specs=[qspec, pl.BlockSpec((1, N_KV_HEADS, nbp, HEAD_DIM), lambda b, i: (b, 0, 0, 0))],
        out_specs=pl.BlockSpec((1, 1, N_KV_HEADS, nbp, GROUP * tq), lambda b, i: (b, i, 0, 0, 0)),
        out_shape=jax.ShapeDtypeStruct((bsz, nq, N_KV_HEADS, nbp, GROUP * tq), F32),
        compiler_params=_params("parallel", "parallel"),
        name="moba_select",
    )(q, kmean)
    sel = sel.reshape(bsz, nq, N_KV_HEADS, nbp, GROUP, tq).transpose(0, 2, 4, 1, 5, 3)
    return sel.reshape(bsz, N_KV_HEADS, GROUP, t, nbp)


def _rope_tables(pos):
    half = HEAD_DIM // 2
    inv = ROPE_THETA ** (-jnp.arange(half, dtype=F32) / half)
    ang = pos.astype(F32)[:, None] * inv[None, :]
    cos, sin = jnp.cos(ang), jnp.sin(ang)
    rep = LANES // HEAD_DIM
    return jnp.concatenate([cos, cos] * rep, axis=1), jnp.concatenate([-sin, sin] * rep, axis=1)


def _pad_to(x, axis, size):
    pad = [(0, 0)] * x.ndim
    pad[axis] = (0, size - x.shape[axis])
    return jnp.pad(x, pad)


def _q_groups(q):
    b, t = q.shape[:2]
    return q.transpose(0, 2, 1, 3).reshape(b, N_KV_HEADS, GROUP, t, HEAD_DIM)


def _pages_t(kv):
    return kv.transpose(0, 2, 3, 4, 1).reshape(kv.shape[0], 2, KV_COLS, PAGE_SIZE)


def _fit(x, width):
    return x[..., :width] if x.shape[-1] >= width else _pad_to(x, x.ndim - 1, width)


_HEAD_TO_GROUP = np.equal(np.arange(N_HEADS)[:, None] // GROUP, np.arange(N_KV_HEADS)[None, :]).astype(np.float32)


def _q_block_diag(q):
    s, tn = q.shape[:2]
    qb = q[:, :, :, None, :] * jnp.asarray(_HEAD_TO_GROUP, dtype=q.dtype)[None, None, :, :, None]
    return qb.reshape(s, tn * N_HEADS, KV_COLS)


def _extract_block_diag(o, tn):
    s = o.shape[0]
    o5 = o.reshape(s, tn, N_HEADS, N_KV_HEADS, HEAD_DIM) * _HEAD_TO_GROUP[None, None, :, :, None]
    return o5.sum(axis=3).reshape(s * tn, Q_DIM)


def _new_page(kv_new):
    return _pages_t(_pad_to(kv_new, 1, PAGE_SIZE))


def _pick_pages(n_pages):
    for n in (8, 4, 2, 1):
        if n_pages % n == 0:
            return n


def _identity_pages(bsz, n_pages):
    return jnp.arange(bsz * n_pages, dtype=I32).reshape(bsz, n_pages)


def _rows_th(ok, s, tn):
    return jnp.where(ok, 0.0, NEG).astype(BF16).reshape(s, tn * N_HEADS, ok.shape[-1])


def _kv_seg(c_k, rope):
    return (c_k, 2 * KV_COLS, KV_COLS if rope else False, 1.0, F32, "tok")


def _kv_rows(kv, bsz, t):
    return kv.reshape(bsz, t, 2, N_KV_HEADS, HEAD_DIM)


def _kv_forms(c_k, rope):
    return [(c_k, KV_COLS, rope, 1.0, BF16, "keys_t"), (c_k + KV_COLS, KV_COLS, False, 1.0, BF16, "values_1")]


def _dsa_project(x, bsz, t, w_in, cs, q_scale, prompt):
    hm = "heads" if prompt else "tok"
    c_qi = Q_DIM + 2 * KV_COLS
    c_ki = c_qi + IDX_HEADS * IDX_DIM
    segs = [(0, Q_DIM, True, q_scale, BF16, hm), _kv_seg(Q_DIM, True), (c_qi, IDX_HEADS * IDX_DIM, True, 1.0, BF16, hm),
            (c_ki, IDX_DIM, True, 1.0, F32, "tok"), (c_ki + IDX_DIM, IDX_HEADS, False, IDX_SCALE, F32, "tok")]
    if prompt:
        segs += _kv_forms(Q_DIM, True)
    q, kv, qi, ki, wi, *flash_kv = _proj(x, w_in, cs, segs, bsz)
    if not prompt:
        q, qi = q.reshape(bsz, t, N_HEADS, HEAD_DIM), qi.reshape(bsz, t, IDX_HEADS, IDX_DIM)
    return q, qi, wi.reshape(bsz, t, IDX_HEADS), _kv_rows(kv, bsz, t), ki.reshape(bsz, t, IDX_DIM), flash_kv


def _head_groups(q):
    return q.reshape(q.shape[0], N_KV_HEADS, GROUP, q.shape[2], HEAD_DIM)


def _dsa_prompt(x, bsz, t, w_in, cs):
    q, qi, wi, kv, ki, (k_t, v1) = _dsa_project(x, bsz, t, w_in, cs, ATTN_SCALE * LOG2E, True)
    mask = _dsa_select_prompt(qi, ki.astype(BF16), wi)
    o = _flash_prompt("dsa", _head_groups(q), k_t, v1, [mask])
    return o.reshape(bsz * t, Q_DIM), kv, ki


def _dsa_sample(x, s, tn, w_in, cs, page_table, cache_kv, cache_kidx, n_pg):
    q, qi, wi, kv, ki, _ = _dsa_project(x, s, tn, w_in, cs, ATTN_SCALE, False)
    tpad = SUBLANES
    qi_p = _pad_to(qi, 1, tpad).reshape(s, tpad * IDX_HEADS, IDX_DIM)
    wi_p = _pad_to(wi, 1, tpad).reshape(s, tpad * IDX_HEADS, 1)
    mask = _dsa_select_sample(page_table, qi_p, wi_p, cache_kidx.transpose(0, 2, 1),
                              _pad_to(ki, 1, PAGE_SIZE).transpose(0, 2, 1), tn, n_pg)
    ok = jnp.broadcast_to(mask[:, :tn, None, :] > 0.5, (s, tn, N_HEADS, mask.shape[-1]))
    o = _paged_flash(page_table, _q_block_diag(q), _pages_t(cache_kv), _new_page(kv), _rows_th(ok, s, tn), n_pg)
    return _extract_block_diag(o, tn), kv, ki


def _fox_project(x, bsz, t, w_in, b_f, cs, q_scale, prompt):
    segs = [(0, Q_DIM, False, q_scale, BF16, "heads" if prompt else "tok"), _kv_seg(Q_DIM, False),
            (Q_DIM + 2 * KV_COLS, N_HEADS, False, 1.0, F32, "tok")]
    if prompt:
        segs += _kv_forms(Q_DIM, False)
    q, kv, f, *flash_kv = _proj(x, w_in, cs, segs, bsz)
    logf = jax.nn.log_sigmoid(f.reshape(bsz, t, N_HEADS) + b_f)
    if not prompt:
        q = q.reshape(bsz, t, N_HEADS, HEAD_DIM)
    return q, _kv_rows(kv, bsz, t), logf, flash_kv


def _fox_prompt(x, bsz, t, w_in, b_f, cs):
    q, kv, logf, (k_t, v1) = _fox_project(x, bsz, t, w_in, b_f, cs, ATTN_SCALE * LOG2E, True)
    n_pages = t // LANES
    pool_t = logf.reshape(bsz, n_pages, LANES, N_HEADS).transpose(0, 1, 3, 2).reshape(bsz * n_pages, N_HEADS, LANES)
    c = _paged_cumsum(_identity_pages(bsz, n_pages), pool_t, jnp.zeros((bsz, N_HEADS, LANES), F32), _pick_pages(n_pages))
    c = (c[:, :, :t] * LOG2E).reshape(bsz, N_KV_HEADS, GROUP, t)
    o = _flash_prompt("fox", _head_groups(q), k_t, v1, [c])
    return o.reshape(bsz * t, Q_DIM), kv, logf


def _fox_sample(x, s, tn, w_in, b_f, cs, page_table, cache_kv, cache_logf, n_pg):
    q, kv, logf, _ = _fox_project(x, s, tn, w_in, b_f, cs, ATTN_SCALE, False)
    past = page_table.shape[1] * PAGE_SIZE
    c = _paged_cumsum(page_table, cache_logf.transpose(0, 2, 1), _pad_to(logf.transpose(0, 2, 1), 2, LANES), n_pg)
    col = jnp.arange(c.shape[-1])
    valid = (col[None, :] < past) | ((col[None, :] - past <= jnp.arange(tn)[:, None]) & (col[None, :] < past + tn))
    bias = jnp.where(valid[None, :, None, :], -c[:, None, :, :], NEG).reshape(s, tn * N_HEADS, c.shape[-1])
    o = _paged_flash(page_table, _q_block_diag(q), _pages_t(cache_kv), _new_page(kv), bias, n_pg)
    return _extract_block_diag(o, tn), kv, logf


def _nsa_project(x, bsz, t, w_in, b_gate, cs, q_scale, prompt):
    hm = "heads" if prompt else "tok"
    c_slc, c_win = Q_DIM + 2 * KV_COLS, Q_DIM + 4 * KV_COLS
    segs = [(0, Q_DIM, False, ATTN_SCALE, BF16, hm), (0, Q_DIM, True, q_scale, BF16, hm),
            _kv_seg(Q_DIM, False), _kv_seg(c_slc, True), _kv_seg(c_win, True),
            (Q_DIM + 6 * KV_COLS, 3 * N_HEADS, False, 1.0, F32, "tok")]
    if prompt:
        segs += _kv_forms(c_slc, True) + _kv_forms(c_win, True)
    q, q_rot, kv_cmp, kv_slc, kv_win, g, *flash_kv = _proj(x, w_in, cs, segs, bsz)
    gate = jax.nn.sigmoid(g.reshape(bsz, t, 3 * N_HEADS) + b_gate).reshape(bsz, t, 3, N_HEADS)
    if not prompt:
        q, q_rot = q.reshape(bsz, t, N_HEADS, HEAD_DIM), q_rot.reshape(bsz, t, N_HEADS, HEAD_DIM)
    return (q, q_rot, gate, _kv_rows(kv_cmp, bsz, t), _kv_rows(kv_slc, bsz, t), _kv_rows(kv_win, bsz, t), flash_kv)


def _nsa_weights(pe, w1, w2):
    r = CMP_LEN // CMP_STRIDE
    w1cat = w1.reshape(2, r, CMP_STRIDE * HEAD_DIM, CMP_HID).transpose(0, 2, 1, 3).reshape(2, CMP_STRIDE * HEAD_DIM, r * CMP_HID)
    pe8 = jnp.broadcast_to(pe.reshape(2, 1, CMP_LEN * HEAD_DIM), (2, 8, CMP_LEN * HEAD_DIM))
    return pe8.astype(BF16), w1cat.astype(BF16), w2.astype(BF16)


def _chunk_pages(kv):
    n = kv.shape[0]
    per = PAGE_SIZE // CMP_STRIDE
    x = kv.reshape(n, per, CMP_STRIDE, 2 * N_KV_HEADS, HEAD_DIM).transpose(0, 3, 1, 2, 4)
    return x.reshape(n, 2 * N_KV_HEADS, per, CMP_STRIDE * HEAD_DIM).astype(BF16)


def _gate_mix(gate, o_c, o_s, o_w):
    m = gate.shape[0] * gate.shape[1]
    return (gate.reshape(m, 3 * N_HEADS), o_c.reshape(m, Q_DIM), o_s.reshape(m, Q_DIM), o_w.reshape(m, Q_DIM))


def _nsa_prompt(x, bsz, t, w_in, b_gate, cmp_w, cs, tq=Q_TILE, tk=K_TILE):
    q, q_rot, gate, kv_cmp, kv_slc, kv_win, (ks_t, vs1, kw_t, vw1) = _nsa_project(x, bsz, t, w_in, b_gate, cs,
                                                                                  ATTN_SCALE * LOG2E, True)
    n_pages = t // PAGE_SIZE
    cmp, nc = _nsa_compress(_identity_pages(bsz, n_pages), _chunk_pages(kv_cmp.reshape(bsz * n_pages, PAGE_SIZE, 2, N_KV_HEADS, HEAD_DIM)),
                            *cmp_w, n_pg=_pick_pages(n_pages))
    cmp = cmp.astype(BF16)
    tq, tk = min(tq, t), min(tk, t)
    o_c, selblk = _nsa_cmp_select(_head_groups(q), cmp[:, :N_KV_HEADS], cmp[:, N_KV_HEADS:], 0, nc, t, tq)
    qg = _head_groups(q_rot)
    e3 = _expand_matrix(selblk.shape[-1], SEL_BLOCK, t, tk)
    o_s = _flash_prompt("nsa_sel", qg, ks_t, vs1, [selblk, e3], tq=2 * tq, tk=tk)
    o_w = _flash_prompt("nsa_win", qg, kw_t, vw1, [], tk=tk)
    return _gate_mix(gate, o_c, o_s, o_w), kv_cmp, kv_slc, kv_win[:, -min(WINDOW, t):]


def _nsa_sample(x, s, tn, w_in, b_gate, cmp_w, cs, page_table, cache_cmp, cache_slc, state_win, n_pg):
    q, q_rot, gate, kv_cmp, kv_slc, kv_win, _ = _nsa_project(x, s, tn, w_in, b_gate, cs, ATTN_SCALE, False)
    past = page_table.shape[1] * PAGE_SIZE
    tpad = SUBLANES
    cmp, nc = _nsa_compress(page_table, _chunk_pages(cache_cmp), *cmp_w, n_pg=n_pg)
    cmp = cmp.astype(BF16)
    o_c, selblk = _nsa_cmp_select(_q_groups(_pad_to(q, 1, tpad)), cmp[:, :N_KV_HEADS], cmp[:, N_KV_HEADS:],
                                  past, nc, past + tn, tpad)
    o_c = o_c[:, :tn]
    qbd = _q_block_diag(q_rot)
    width = (page_table.shape[1] // n_pg + 1) * n_pg * LANES
    col = jnp.arange(width)
    pos = past + jnp.arange(tn)
    sel_key = jnp.repeat(selblk[:, :, :tn, :-(-width // SEL_BLOCK)] > 0.5, SEL_BLOCK, axis=-1)[..., :width]
    ok = sel_key & (col[None, :] <= pos[:, None])[None, None]
    ok = jnp.broadcast_to(ok.transpose(0, 2, 1, 3)[:, :, :, None, :], (s, tn, N_KV_HEADS, GROUP, width))
    o_s = _paged_flash(page_table, qbd, _pages_t(cache_slc), _new_page(kv_slc),
                       _rows_th(ok.reshape(s, tn, N_HEADS, width), s, tn), n_pg)
    win_buf = state_win.shape[1]
    n_wp = win_buf // PAGE_SIZE
    wcol = jnp.arange(2 * n_wp * LANES)
    win_pos = jnp.where(wcol < win_buf, past - win_buf + wcol, jnp.where(wcol < win_buf + tn, past + wcol - win_buf, -1))
    w_ok = (win_pos[None, :] <= pos[:, None]) & (win_pos[None, :] >= pos[:, None] - WINDOW) & (win_pos[None, :] >= 0)
    w_ok = jnp.broadcast_to(w_ok[None, :, None, :], (s, tn, N_HEADS, wcol.shape[0]))
    win_pages = _pages_t(state_win.reshape(s * n_wp, PAGE_SIZE, 2, N_KV_HEADS, HEAD_DIM))
    o_w = _paged_flash(_identity_pages(s, n_wp), qbd, win_pages, _new_page(kv_win), _rows_th(w_ok, s, tn), n_wp)
    unbd = lambda o: _extract_block_diag(o, tn).reshape(s, tn, Q_DIM)
    win = jnp.concatenate([state_win, kv_win], axis=1)[:, -win_buf:]
    return _gate_mix(gate, o_c, unbd(o_s), unbd(o_w)), kv_cmp, kv_slc, win


def _moba_project(x, bsz, t, w_in, cs, q_scale, prompt):
    segs = [(0, Q_DIM, True, q_scale, BF16, "heads" if prompt else "tok"), _kv_seg(Q_DIM, True)]
    if prompt:
        segs += _kv_forms(Q_DIM, True)
    q, kv, *flash_kv = _proj(x, w_in, cs, segs, bsz)
    if not prompt:
        q = q.reshape(bsz, t, N_HEADS, HEAD_DIM)
    return q, _kv_rows(kv, bsz, t), flash_kv


def _kmean_heads(km):
    b, nb = km.shape[:2]
    return _pad_to(km.reshape(b, nb, N_KV_HEADS, HEAD_DIM).transpose(0, 2, 1, 3), 2, -(-nb // 16) * 16).astype(BF16)


def _moba_prompt(x, bsz, t, w_in, cs, tq=Q_TILE, tk=K_TILE):
    q, kv, (k_t, v1) = _moba_project(x, bsz, t, w_in, cs, ATTN_SCALE * LOG2E, True)
    tq, tk = min(tq, t), min(tk, t)
    nb = -(-t // MOBA_BLOCK)
    km = _kmean_heads(_kmean_prompt(kv.reshape(bsz, t, 2 * KV_COLS)))
    qg = _head_groups(q)
    sel = _moba_select(qg, km, 0, nb, tq)
    e3 = _expand_matrix(km.shape[2], MOBA_BLOCK, t, tk)
    o = _flash_prompt("moba", qg, k_t, v1, [sel, e3], tq=2 * tq, tk=tk)
    return o.reshape(bsz * t, Q_DIM), kv


def _moba_sample(x, s, tn, w_in, cs, page_table, cache_kv, n_pg):
    q, kv, _ = _moba_project(x, s, tn, w_in, cs, ATTN_SCALE, False)
    past = page_table.shape[1] * PAGE_SIZE
    tpad = SUBLANES
    pool = _pages_t(cache_kv)
    nb = -(-(past + tn) // MOBA_BLOCK)
    km = _kmean_heads(_kmean_sample(page_table, pool, n_pg))
    sel = _moba_select(_q_groups(_pad_to(q, 1, tpad)), km, past, nb, tpad)
    width = (page_table.shape[1] // n_pg + 1) * n_pg * LANES
    col = jnp.arange(width)
    pos = past + jnp.arange(tn)
    sel_key = _fit(jnp.repeat(sel[:, :, :, :tn] > 0.5, MOBA_BLOCK, axis=-1), width)
    own = (col[None, :] // MOBA_BLOCK == pos[:, None] // MOBA_BLOCK) & (col[None, :] <= pos[:, None])
    ok = sel_key | own[None, None, None]
    ok = ok.transpose(0, 3, 1, 2, 4).reshape(s, tn, N_HEADS, width)
    o = _paged_flash(page_table, _q_block_diag(q), pool, _new_page(kv), _rows_th(ok, s, tn), n_pg)
    return _extract_block_diag(o, tn), kv


def _cast_w(w):
    return _pad_to(w, 1, -(-w.shape[1] // LANES) * LANES).astype(BF16)


def kernel(x_prompt, x_sample, cache_a_kv, cache_a_kidx, cache_b_kv, cache_b_logf, cache_c_cmp_kv, cache_c_slc_kv, state_c_win_kv, cache_d_kv, page_table, a_w_in, a_w_out, b_w_in, b_b_f, b_w_out, c_w_in, c_b_gate, c_cmp_pe, c_cmp_w1, c_cmp_w2, c_w_out, d_w_in, d_w_out, ln_g, ln_b, ffn_w_gu, ffn_w_down):
    bsz, t, d = x_prompt.shape
    s, tn, _ = x_sample.shape
    n_pages = page_table.shape[1]
    past = n_pages * PAGE_SIZE
    n_pg = _pick_pages(n_pages)
    cs_p = _rope_tables(jnp.arange(t, dtype=I32))
    cs_s = _rope_tables(jnp.tile(past + jnp.arange(tn, dtype=I32), s))
    xp = x_prompt.reshape(bsz * t, d)
    xs = x_sample.reshape(s * tn, d)
    cmp_w = _nsa_weights(c_cmp_pe, c_cmp_w1, c_cmp_w2)
    w_out = [_cast_w(w) for w in (a_w_out, b_w_out, c_w_out, d_w_out)]

    op, a_kv_p, a_kidx_p = _dsa_prompt(xp, bsz, t, _cast_w(a_w_in), cs_p)
    os_, a_kv_s, a_kidx_s = _dsa_sample(xs, s, tn, _cast_w(a_w_in), cs_s, page_table, cache_a_kv, cache_a_kidx, n_pg)

    def finish(i, xp, xs, op, os_):
        xp = _out_ln(op, w_out[i], xp, ln_g[i, 0], ln_b[i, 0])
        xs = _out_ln(os_, w_out[i], xs, ln_g[i, 0], ln_b[i, 0])
        wgu, wd = ffn_w_gu[i].astype(BF16), ffn_w_down[i].astype(BF16)
        xp = _ffn_ln(xp, wgu, wd, ln_g[i, 1], ln_b[i, 1])
        xs = _ffn_ln(xs, wgu, wd, ln_g[i, 1], ln_b[i, 1])
        return xp, xs

    xp, xs = finish(0, xp, xs, op, os_)

    op, b_kv_p, b_logf_p = _fox_prompt(xp, bsz, t, _cast_w(b_w_in), b_b_f, cs_p)
    os_, b_kv_s, b_logf_s = _fox_sample(xs, s, tn, _cast_w(b_w_in), b_b_f, cs_s, page_table, cache_b_kv, cache_b_logf, n_pg)
    xp, xs = finish(1, xp, xs, op, os_)

    op, c_cmp_kv_p, c_slc_kv_p, c_win_kv_p = _nsa_prompt(xp, bsz, t, _cast_w(c_w_in), c_b_gate, cmp_w, cs_p)
    os_, c_cmp_kv_s, c_slc_kv_s, c_win_kv_s = _nsa_sample(xs, s, tn, _cast_w(c_w_in), c_b_gate, cmp_w, cs_s, page_table,
                                                          cache_c_cmp_kv, cache_c_slc_kv, state_c_win_kv, n_pg)
    xp, xs = finish(2, xp, xs, op, os_)

    op, d_kv_p = _moba_prompt(xp, bsz, t, _cast_w(d_w_in), cs_p)
    os_, d_kv_s = _moba_sample(xs, s, tn, _cast_w(d_w_in), cs_s, page_table, cache_d_kv, n_pg)
    xp, xs = finish(3, xp, xs, op, os_)

    return (xp.reshape(bsz, t, d), xs.reshape(s, tn, d), a_kv_p, a_kv_s, a_kidx_p, a_kidx_s, b_kv_p, b_kv_s,
            b_logf_p, b_logf_s, c_cmp_kv_p, c_cmp_kv_s, c_slc_kv_p, c_slc_kv_s, c_win_kv_p, c_win_kv_s, d_kv_p, d_kv_s)
```

```python
import functools

import numpy as np
import jax
import jax.numpy as jnp
from jax import lax
from jax.experimental import pallas as pl
from jax.experimental.pallas import tpu as pltpu

F32 = jnp.float32
BF16 = jnp.bfloat16
I32 = jnp.int32

N_HEADS = 16
HEAD_DIM = 64
N_KV_HEADS = 4
GROUP = N_HEADS // N_KV_HEADS
Q_DIM = N_HEADS * HEAD_DIM
KV_COLS = N_KV_HEADS * HEAD_DIM
DEPTH = 4
PAGE_SIZE = 128
ROPE_THETA = 10000.0
LN_EPS = 1e-5
ALPHA = (2 * DEPTH) ** 0.25
ATTN_SCALE = HEAD_DIM ** -0.5
LOG2E = 1.4426950408889634
IDX_HEADS = 8
IDX_DIM = 64
IDX_TOPK = 256
IDX_SCALE = (IDX_HEADS * IDX_DIM) ** -0.5
CMP_LEN = 32
CMP_STRIDE = 16
CMP_HID = 2 * HEAD_DIM
SEL_BLOCK = 64
SEL_TOPN = 16
WINDOW = 512
MOBA_BLOCK = 256
MOBA_TOPK = 3

LANES = 128
SUBLANES = 8
ROW_TILE = 512
FFN_ROW_TILE = 512
FFN_CHUNK = 256
Q_TILE = 128
K_TILE = 512
VMEM_LIMIT = 56 * 2 ** 20
NEG = -1e30
KEY_NEG_INF = -2139095041
KEY_POS_INF = 2139095040
INT_MIN = -2 ** 31


def _params(*sem):
    return pltpu.CompilerParams(dimension_semantics=sem, vmem_limit_bytes=VMEM_LIMIT)


def _dot_t(a, b):
    return lax.dot_general(a, b, (((1,), (1,)), ((), ())), preferred_element_type=F32)


def _dot(a, b):
    return jnp.dot(a, b, preferred_element_type=F32)


def _dot_hp(a, b):
    hi = a.astype(BF16)
    r1 = a - hi.astype(F32)
    mid = r1.astype(BF16)
    lo = (r1 - mid.astype(F32)).astype(BF16)
    return _dot(hi, b) + _dot(mid, b) + _dot(lo, b)


def _proj_kernel(x_ref, w_ref, cos_ref, sin_ref, *out_refs, segs):
    acc = _dot(x_ref[...].astype(BF16), w_ref[...])
    tm = acc.shape[0]
    lane = lax.broadcasted_iota(I32, (tm, LANES), 1)
    first_half = (lane & (HEAD_DIM - 1)) < HEAD_DIM // 2
    ones_col = jnp.where(lane == HEAD_DIM, 1.0, 0.0)
    for (c0, width, rope, scale, form), o_ref in zip(segs, out_refs):
        if c0 % LANES:
            o_ref[...] = (acc[:, c0:c0 + width] * scale).astype(o_ref.dtype)
            continue
        for j in range(-(-width // LANES)):
            x = acc[:, c0 + j * LANES:c0 + (j + 1) * LANES]
            if j * LANES < (width if rope is True else int(rope)):
                swapped = jnp.where(first_half, pltpu.roll(x, LANES - HEAD_DIM // 2, 1), pltpu.roll(x, HEAD_DIM // 2, 1))
                x = x * cos_ref[...] + swapped * sin_ref[...]
            if scale != 1.0:
                x = x * scale
            if form == "tok":
                wj = min(LANES, width - j * LANES)
                o_ref[:, j * LANES:j * LANES + wj] = x[:, :wj].astype(o_ref.dtype)
            elif form == "heads":
                o_ref[0, 2 * j] = x[:, :HEAD_DIM].astype(o_ref.dtype)
                o_ref[0, 2 * j + 1] = x[:, HEAD_DIM:].astype(o_ref.dtype)
            elif form == "keys_t":
                xt = x.T
                o_ref[0, 2 * j] = xt[:HEAD_DIM].astype(o_ref.dtype)
                o_ref[0, 2 * j + 1] = xt[HEAD_DIM:].astype(o_ref.dtype)
            else:
                o_ref[0, 2 * j] = jnp.where(lane < HEAD_DIM, x, ones_col).astype(o_ref.dtype)
                o_ref[0, 2 * j + 1] = jnp.where(lane < HEAD_DIM, pltpu.roll(x, HEAD_DIM, 1), ones_col).astype(o_ref.dtype)


def _proj(x, w, cs, segs, bsz=None):
    m, k = x.shape
    n = w.shape[1]
    tm = min(m, ROW_TILE)
    cos_t, sin_t = cs
    r_blocks = cos_t.shape[0] // tm
    tab = pl.BlockSpec((tm, LANES), lambda i: (i % r_blocks, 0))
    specs, shapes = [], []
    for _, width, _, _, dt, form in segs:
        nh = width // HEAD_DIM
        if form == "tok":
            specs.append(pl.BlockSpec((tm, width), lambda i: (i, 0)))
            shapes.append(jax.ShapeDtypeStruct((m, width), dt))
            continue
        t = m // bsz
        tpb = t // tm
        if form == "keys_t":
            specs.append(pl.BlockSpec((1, nh, HEAD_DIM, tm), lambda i: (i // tpb, 0, 0, i % tpb)))
            shapes.append(jax.ShapeDtypeStruct((bsz, nh, HEAD_DIM, t), dt))
        else:
            last = HEAD_DIM if form == "heads" else LANES
            specs.append(pl.BlockSpec((1, nh, tm, last), lambda i: (i // tpb, 0, i % tpb, 0)))
            shapes.append(jax.ShapeDtypeStruct((bsz, nh, t, last), dt))
    return pl.pallas_call(
        functools.partial(_proj_kernel, segs=tuple(s[:4] + (s[5],) for s in segs)),
        grid=(m // tm,),
        in_specs=[pl.BlockSpec((tm, k), lambda i: (i, 0)), pl.BlockSpec((k, n), lambda i: (0, 0)), tab, tab],
        out_specs=specs,
        out_shape=shapes,
        compiler_params=_params("parallel"),
        name="in_proj",
    )(x, w, cos_t, sin_t)


def _layer_norm(y, g, b):
    mu = jnp.mean(y, axis=-1, keepdims=True)
    d = y - mu
    var = jnp.mean(d * d, axis=-1, keepdims=True)
    return d * lax.rsqrt(var + LN_EPS) * g + b


def _out_ln_kernel(o_ref, w_ref, x_ref, g_ref, b_ref, y_ref):
    y = ALPHA * x_ref[...] + _dot(o_ref[...].astype(BF16), w_ref[...])
    y_ref[...] = _layer_norm(y, g_ref[...], b_ref[...])


def _out_ln_gated_kernel(gate_ref, oc_ref, os_ref, ow_ref, w_ref, x_ref, g_ref, b_ref, y_ref):
    gate = gate_ref[...]
    heads = []
    for h in range(N_HEADS):
        cols = slice(h * HEAD_DIM, (h + 1) * HEAD_DIM)
        heads.append(gate[:, h:h + 1] * oc_ref[:, cols] + gate[:, N_HEADS + h:N_HEADS + h + 1] * os_ref[:, cols]
                     + gate[:, 2 * N_HEADS + h:2 * N_HEADS + h + 1] * ow_ref[:, cols])
    o = jnp.concatenate(heads, axis=1)
    y = ALPHA * x_ref[...] + _dot(o.astype(BF16), w_ref[...])
    y_ref[...] = _layer_norm(y, g_ref[...], b_ref[...])


def _out_ln(o, w, x, g, b):
    m, d = x.shape
    tm = min(m, ROW_TILE)
    row = lambda i: (i, 0)
    fix = lambda i: (0, 0)
    gated = isinstance(o, tuple)
    o_in = list(o) if gated else [o]
    return pl.pallas_call(
        _out_ln_gated_kernel if gated else _out_ln_kernel,
        grid=(m // tm,),
        in_specs=[pl.BlockSpec((tm, a.shape[1]), row) for a in o_in]
        + [pl.BlockSpec(w.shape, fix), pl.BlockSpec((tm, d), row), pl.BlockSpec((1, d), fix), pl.BlockSpec((1, d), fix)],
        out_specs=pl.BlockSpec((tm, d), row),
        out_shape=jax.ShapeDtypeStruct((m, d), F32),
        compiler_params=_params("parallel"),
        name="out_proj_ln",
    )(*o_in, w, x, g.reshape(1, d), b.reshape(1, d))


def _ffn_ln_kernel(x_ref, wgu_ref, wd_ref, g_ref, b_ref, y_ref, *, d_ff, chunk):
    x = x_ref[...]
    xb = x.astype(BF16)
    acc = jnp.zeros(x.shape, F32)
    for c in range(d_ff // chunk):
        gate = _dot(xb, wgu_ref[:, c * chunk:(c + 1) * chunk])
        up = _dot(xb, wgu_ref[:, d_ff + c * chunk:d_ff + (c + 1) * chunk])
        h = gate * (1.0 / (1.0 + jnp.exp(-gate))) * up
        acc = acc + _dot(h.astype(BF16), wd_ref[c * chunk:(c + 1) * chunk, :])
    y_ref[...] = _layer_norm(ALPHA * x + acc, g_ref[...], b_ref[...])


def _ffn_ln(x, wgu, wd, g, b):
    m, d = x.shape
    d_ff = wd.shape[0]
    tm = min(m, FFN_ROW_TILE)
    row = lambda i: (i, 0)
    fix = lambda i: (0, 0)
    return pl.pallas_call(
        functools.partial(_ffn_ln_kernel, d_ff=d_ff, chunk=FFN_CHUNK),
        grid=(m // tm,),
        in_specs=[pl.BlockSpec((tm, d), row), pl.BlockSpec((d, 2 * d_ff), fix), pl.BlockSpec((d_ff, d), fix),
                  pl.BlockSpec((1, d), fix), pl.BlockSpec((1, d), fix)],
        out_specs=pl.BlockSpec((tm, d), row),
        out_shape=jax.ShapeDtypeStruct((m, d), F32),
        compiler_params=_params("parallel"),
        name="ffn_ln",
    )(x, wgu, wd, g.reshape(1, d), b.reshape(1, d))


def _sortable(x):
    x = jnp.where(x == 0.0, 0.0, x)
    b = lax.bitcast_convert_type(x, I32)
    return b ^ ((b >> 31) & I32(0x7FFFFFFF))


def _kth_largest_key(u_ref, k):
    rows = u_ref.shape[0]

    def count_ge(cand):
        return jnp.sum((u_ref[...] >= cand).astype(I32), axis=1, keepdims=True)

    base = jnp.where(count_ge(jnp.zeros((rows, 1), I32)) >= k, I32(0), I32(INT_MIN))

    def body(i, base):
        cand = base | jnp.left_shift(I32(1), 30 - i)
        return jnp.where(count_ge(cand) >= k, cand, base)

    return lax.fori_loop(0, 31, body, base)


def _emit_selection(u_ref, k, write):
    rows, n = u_ref.shape
    thr = _kth_largest_key(u_ref, k)
    n_gt = jnp.sum((u_ref[...] > thr).astype(I32), axis=1, keepdims=True)
    need = (k - n_gt).astype(F32)
    r_i = lax.broadcasted_iota(I32, (LANES, LANES), 0)
    c_i = lax.broadcasted_iota(I32, (LANES, LANES), 1)
    tri = jnp.where(r_i <= c_i, 1.0, 0.0).astype(BF16)
    carry = jnp.zeros((rows, 1), F32)
    for ci in range(n // LANES):
        u = u_ref[:, ci * LANES:(ci + 1) * LANES]
        tie = u == thr
        tie_f = jnp.where(tie, 1.0, 0.0)
        inc = _dot(tie_f.astype(BF16), tri)
        rank = carry + inc - tie_f
        sel = (u > thr) | (tie & (rank < need))
        write(ci, sel, u)
        carry = carry + inc[:, LANES - 1:LANES]


def _topk_mask_cols(u_ref, k):
    n, r = u_ref.shape

    def count(pred):
        return jnp.sum(pred(u_ref[...]).astype(I32), axis=0, keepdims=True)

    base = jnp.where(count(lambda u: u >= 0) >= k, I32(0), I32(INT_MIN))

    def radix(i, base):
        cand = base | jnp.left_shift(I32(1), 30 - i)
        return jnp.where(count(lambda u: u >= cand) >= k, cand, base)

    thr = lax.fori_loop(0, 31, radix, base)
    need = (k - count(lambda u: u > thr)).astype(F32)
    u = u_ref[...]
    tie = u == thr
    lower = jnp.where(lax.broadcasted_iota(I32, (n, n), 0) > lax.broadcasted_iota(I32, (n, n), 1), 1.0, 0.0)
    rank = _dot(lower.astype(BF16), jnp.where(tie, 1.0, 0.0).astype(BF16))
    return (u > thr) | (tie & (rank < need)), u


def _flash_prompt_kernel(*refs, kind, tq, tk, n_extra):
    q_ref, k_ref, v_ref = refs[:3]
    extra = refs[3:3 + n_extra]
    o_ref, m_ref, acc_ref, s_ref = refs[3 + n_extra:]
    g = pl.program_id(1)
    q0 = pl.program_id(2) * tq
    rows = GROUP * tq
    q = q_ref[0, 0].reshape(rows, HEAD_DIM)
    m_ref[...] = jnp.full(m_ref.shape, NEG, F32)
    acc_ref[...] = jnp.zeros(acc_ref.shape, F32)
    c_diag = q0 // tk
    c_lo = jnp.maximum(q0 - WINDOW, 0) // tk if kind == "nsa_win" else 0
    if kind == "nsa_sel":
        selb = extra[0][0, 0].astype(BF16)
    if kind == "moba":
        selb = extra[0][0, 0].reshape(rows, extra[0].shape[-1]).astype(BF16)

    def scores(c):
        return _dot(q, k_ref[0, g, :, pl.ds(pl.multiple_of(c * tk, tk), tk)])

    def chunk(c, diag, slot=None):
        if kind in ("nsa_sel", "moba"):
            hit = _dot(selb, extra[1][c]) > 0.5
        s3 = s_ref[(c - c_lo) % 2 if slot is None else slot].reshape(GROUP, tq, tk)
        if not diag:
            s_ref[1 - slot] = scores(c + 1)
        start = pl.multiple_of(c * tk, tk)
        v = v_ref[0, g, pl.ds(start, tk), :]
        ok = None
        if diag or kind == "nsa_win":
            t_idx = q0 + lax.broadcasted_iota(I32, (GROUP, tq, tk), 1)
            s_idx = start + lax.broadcasted_iota(I32, (GROUP, tq, tk), 2)
            causal = s_idx <= t_idx
        if kind == "fox":
            ck = extra[0][0, 0, :, pl.ds(start, tk)]
            s3 = s3 - ck[:, None, :]
            ok = causal if diag else None
        elif kind == "dsa":
            msk = extra[0][0, :, pl.ds(start, tk)]
            ok = jnp.broadcast_to((msk > 0)[None], (GROUP, tq, tk))
        elif kind == "nsa_sel":
            ok = jnp.broadcast_to(hit[None], (GROUP, tq, tk))
            ok = (ok & causal) if diag else ok
        elif kind == "nsa_win":
            ok = causal & (s_idx >= t_idx - WINDOW)
        else:
            ok = hit.reshape(GROUP, tq, tk)
            if diag:
                ok = ok | (causal & ((s_idx // MOBA_BLOCK) == (t_idx // MOBA_BLOCK)))
        if ok is not None:
            s3 = jnp.where(ok, s3, NEG)
        s = s3.reshape(rows, tk)
        m_prev = m_ref[...]
        m_new = jnp.maximum(m_prev, jnp.max(s, axis=1, keepdims=True))
        alpha = jnp.exp2(m_prev - m_new)
        p = jnp.exp2(s - pltpu.repeat(m_new, tk // LANES, axis=1))
        acc_ref[...] = acc_ref[...] * alpha + _dot(p.astype(BF16), v)
        m_ref[...] = m_new

    def body(i, carry):
        chunk(c_lo + 2 * i, False, 0)
        chunk(c_lo + 2 * i + 1, False, 1)
        return carry

    n_full = c_diag - c_lo
    s_ref[0] = scores(c_lo)
    lax.fori_loop(0, n_full // 2, body, 0)
    pl.when(n_full % 2 == 1)(functools.partial(chunk, c_diag - 1, False, 0))
    chunk(c_diag, True)
    acc = acc_ref[...]
    o = acc[:, :HEAD_DIM] / jnp.maximum(acc[:, HEAD_DIM:HEAD_DIM + 1], 1e-30)
    o_ref[0] = jnp.concatenate([o[j * tq:(j + 1) * tq] for j in range(GROUP)], axis=1)


def _flash_prompt(kind, q, k, v, extra, tq=Q_TILE, tk=K_TILE):
    bsz, _, _, t, _ = q.shape
    tk = min(tk, t)
    tq = min(tq, t)
    qspec = pl.BlockSpec((1, 1, GROUP, tq, HEAD_DIM), lambda b, g, i: (b, g, 0, i, 0))
    kvspec = pl.BlockSpec((1, N_KV_HEADS, HEAD_DIM, t), lambda b, g, i: (b, 0, 0, 0))
    vspec = pl.BlockSpec((1, N_KV_HEADS, t, LANES), lambda b, g, i: (b, 0, 0, 0))
    if kind == "fox":
        especs = [pl.BlockSpec((1, 1, GROUP, t), lambda b, g, i: (b, g, 0, 0))]
    elif kind == "dsa":
        especs = [pl.BlockSpec((1, tq, t), lambda b, g, i: (b, i, 0))]
    elif kind == "nsa_sel":
        especs = [pl.BlockSpec((1, 1, tq, extra[0].shape[-1]), lambda b, g, i: (b, g, i, 0)),
                  pl.BlockSpec(extra[1].shape, lambda b, g, i: (0, 0, 0))]
    elif kind == "moba":
        especs = [pl.BlockSpec((1, 1, GROUP, tq, extra[0].shape[-1]), lambda b, g, i: (b, g, 0, i, 0)),
                  pl.BlockSpec(extra[1].shape, lambda b, g, i: (0, 0, 0))]
    else:
        especs = []
    rows = GROUP * tq
    return pl.pallas_call(
        functools.partial(_flash_prompt_kernel, kind=kind, tq=tq, tk=tk, n_extra=len(extra)),
        grid=(bsz, N_KV_HEADS, t // tq),
        in_specs=[qspec, kvspec, vspec] + especs,
        out_specs=pl.BlockSpec((1, tq, GROUP * HEAD_DIM), lambda b, g, i: (b, i, g)),
        out_shape=jax.ShapeDtypeStruct((bsz, t, Q_DIM), F32),
        scratch_shapes=[pltpu.VMEM((rows, LANES), F32), pltpu.VMEM((rows, LANES), F32),
                        pltpu.VMEM((2, rows, tk), F32)],
        compiler_params=_params("parallel", "parallel", "parallel"),
        name="flash_prompt_" + kind,
    )(q, k, v, *extra)


def _expand_matrix(n_blocks_padded, block, t, tk):
    s = np.arange(t)
    e = (s[None, :] // block == np.arange(n_blocks_padded)[:, None]).astype(np.float32)
    e = e.reshape(n_blocks_padded, t // tk, tk).transpose(1, 0, 2)
    return jnp.asarray(e, dtype=BF16)


def _dsa_select_prompt_kernel(qi_ref, ki_ref, wi_ref, mask_ref, u_ref, *, tq, t, tk, topk):
    q0 = pl.program_id(1) * tq
    n_act = q0 // tk + 1
    sub = tk // LANES
    w = wi_ref[0]
    t_idx = q0 + lax.broadcasted_iota(I32, (tq, tk), 0)
    k_off = lax.broadcasted_iota(I32, (tq, tk), 1)

    def chunk_at(c):
        return pl.ds(pl.multiple_of(c * tk, tk), tk)

    def fill(c, carry):
        kc = ki_ref[0, chunk_at(c), :]
        s = jnp.zeros((tq, tk), F32)
        for h in range(IDX_HEADS):
            s = s + jnp.maximum(_dot_t(qi_ref[0, h], kc), 0.0) * w[:, h:h + 1]
        s = jnp.where(c * tk + k_off <= t_idx, s, -jnp.inf)
        u_ref[:, chunk_at(c)] = _sortable(s)
        return carry

    lax.fori_loop(0, n_act, fill, 0)

    def count(pred):
        def body(c, acc):
            hit = pred(u_ref[:, chunk_at(c)]).astype(I32)
            for i in range(sub):
                acc = acc + hit[:, i * LANES:(i + 1) * LANES]
            return acc
        acc = lax.fori_loop(0, n_act, body, jnp.zeros((tq, LANES), I32))
        return jnp.sum(acc, axis=1, keepdims=True)

    base = jnp.where(count(lambda u: u >= 0) >= topk, I32(0), I32(INT_MIN))

    def radix(i, base):
        cand = base | jnp.left_shift(I32(1), 30 - i)
        return jnp.where(count(lambda u: u >= cand) >= topk, cand, base)

    thr = lax.fori_loop(0, 31, radix, base)
    need = (topk - count(lambda u: u > thr)).astype(F32)
    r_i = lax.broadcasted_iota(I32, (LANES, LANES), 0)
    c_i = lax.broadcasted_iota(I32, (LANES, LANES), 1)
    tri = jnp.where(r_i <= c_i, 1.0, 0.0).astype(BF16)

    def emit(c, carry):
        for i in range(sub):
            at = pl.ds(pl.multiple_of(c * tk + i * LANES, LANES), LANES)
            u = u_ref[:, at]
            tie = u == thr
            tie_f = jnp.where(tie, 1.0, 0.0)
            inc = _dot(tie_f.astype(BF16), tri)
            sel = (u > thr) | (tie & (carry + inc - tie_f < need))
            keep = sel & (u > KEY_NEG_INF) & (u < KEY_POS_INF)
            mask_ref[0, :, at] = jnp.where(keep, 1.0, 0.0).astype(BF16)
            carry = carry + inc[:, LANES - 1:LANES]
        return carry

    lax.fori_loop(0, n_act, emit, jnp.zeros((tq, 1), F32))

    def clear(c, carry):
        mask_ref[0, :, chunk_at(c)] = jnp.zeros((tq, tk), BF16)
        return carry

    lax.fori_loop(n_act, t // tk, clear, 0)


def _dsa_select_prompt(qi, ki, wi, tq=2 * Q_TILE, tk=K_TILE):
    bsz, _, t, _ = qi.shape
    tq, tk = min(tq, t), min(tk, t)
    topk = min(IDX_TOPK, t // 4)
    return pl.pallas_call(
        functools.partial(_dsa_select_prompt_kernel, tq=tq, t=t, tk=tk, topk=topk),
        grid=(bsz, t // tq),
        in_specs=[pl.BlockSpec((1, IDX_HEADS, tq, IDX_DIM), lambda b, i: (b, 0, i, 0)),
                  pl.BlockSpec((1, t, IDX_DIM), lambda b, i: (b, 0, 0)),
                  pl.BlockSpec((1, tq, IDX_HEADS), lambda b, i: (b, i, 0))],
        out_specs=pl.BlockSpec((1, tq, t), lambda b, i: (b, i, 0)),
        out_shape=jax.ShapeDtypeStruct((bsz, t, t), BF16),
        scratch_shapes=[pltpu.VMEM((tq, t), I32)],
        compiler_params=_params("parallel", "parallel"),
        name="dsa_select_prompt",
    )(qi, ki, wi)


def _dsa_select_sample_kernel(pt_ref, qi_ref, wi_ref, *refs, n_pg, n_steps, tpad, n_new, topk, nb):
    pages = refs[:nb * n_pg]
    new_ref, mask_ref, u_ref = refs[nb * n_pg:]
    j = pl.program_id(1)

    def scores(bi, kc):
        rel = jnp.maximum(_dot(qi_ref[bi], kc.astype(BF16)), 0.0) * wi_ref[bi]
        return rel.reshape(tpad, IDX_HEADS, LANES).sum(axis=1)

    @pl.when(j < n_steps - 1)
    def _():
        for bi in range(nb):
            for i in range(n_pg):
                start = pl.multiple_of((j * n_pg + i) * LANES, LANES)
                u_ref[bi * tpad:(bi + 1) * tpad, pl.ds(start, LANES)] = _sortable(scores(bi, pages[bi * n_pg + i][0]))

    @pl.when(j == n_steps - 1)
    def _():
        t_idx = lax.broadcasted_iota(I32, (tpad, LANES), 0)
        c_idx = lax.broadcasted_iota(I32, (tpad, LANES), 1)
        base = (n_steps - 1) * n_pg * LANES
        for bi in range(nb):
            s = jnp.where((c_idx <= t_idx) & (c_idx < n_new), scores(bi, new_ref[bi]), -jnp.inf)
            u_ref[bi * tpad:(bi + 1) * tpad, base:base + LANES] = _sortable(s)
        for i in range(1, n_pg):
            u_ref[:, base + i * LANES:base + (i + 1) * LANES] = jnp.full((nb * tpad, LANES), KEY_NEG_INF, I32)

        def write(ci, sel, u):
            keep = sel & (u > KEY_NEG_INF) & (u < KEY_POS_INF)
            mask_ref[:, :, ci * LANES:(ci + 1) * LANES] = jnp.where(keep, 1.0, 0.0).reshape(nb, tpad, LANES)

        _emit_selection(u_ref, topk, write)


def _dsa_select_sample(page_table, qi, wi, pool_kidx, new_ki, n_new, n_pg=8):
    bsz, n_pages = page_table.shape
    tpad = qi.shape[1] // IDX_HEADS
    n_steps = n_pages // n_pg + 1
    width = n_steps * n_pg * LANES
    topk = min(IDX_TOPK, (n_pages * PAGE_SIZE + n_new) // 4)
    nb = _batch_rows(bsz)
    fix = lambda b, j, pt: (b, 0, 0)

    def page_map(bi, i):
        return lambda b, j, pt: (pt[b * nb + bi, jnp.minimum(j * n_pg + i, n_pages - 1)], 0, 0)

    grid_spec = pltpu.PrefetchScalarGridSpec(
        num_scalar_prefetch=1,
        grid=(bsz // nb, n_steps),
        in_specs=[pl.BlockSpec((nb, tpad * IDX_HEADS, IDX_DIM), fix), pl.BlockSpec((nb, tpad * IDX_HEADS, 1), fix)]
        + [pl.BlockSpec((1, IDX_DIM, PAGE_SIZE), page_map(bi, i)) for bi in range(nb) for i in range(n_pg)]
        + [pl.BlockSpec((nb, IDX_DIM, PAGE_SIZE), fix)],
        out_specs=pl.BlockSpec((nb, tpad, width), fix),
        scratch_shapes=[pltpu.VMEM((nb * tpad, width), I32)],
    )
    return pl.pallas_call(
        functools.partial(_dsa_select_sample_kernel, n_pg=n_pg, n_steps=n_steps, tpad=tpad, n_new=n_new, topk=topk, nb=nb),
        grid_spec=grid_spec,
        out_shape=jax.ShapeDtypeStruct((bsz, tpad, width), F32),
        compiler_params=_params("parallel", "arbitrary"),
        name="dsa_select_sample",
    )(page_table, qi, wi, *([pool_kidx] * (nb * n_pg)), new_ki)


def _batch_rows(bsz):
    return 4 if bsz % 4 == 0 else (2 if bsz % 2 == 0 else 1)


def _page_map(i, n_pg, n_pages):
    return lambda b, j, pt: (pt[b, jnp.minimum(j * n_pg + i, n_pages - 1)], 0, 0)


def _page_map4(i, n_pg, n_pages):
    return lambda b, j, pt: (pt[b, jnp.minimum(j * n_pg + i, n_pages - 1)], 0, 0, 0)


def _paged_flash_kernel(pt_ref, q_ref, *refs, n_pg, n_steps, nb):
    pages = refs[:nb * n_pg]
    new_ref, bias_ref, o_ref, m_ref, l_ref, acc_ref = refs[nb * n_pg:]
    j = pl.program_id(1)

    @pl.when(j == 0)
    def _():
        m_ref[...] = jnp.full(m_ref.shape, NEG, F32)
        l_ref[...] = jnp.zeros(l_ref.shape, F32)
        acc_ref[...] = jnp.zeros(acc_ref.shape, F32)

    def update(bi, kv):
        q = q_ref[bi]
        s = [_dot(q, k().astype(BF16)) + bias_ref[bi, :, i * LANES:(i + 1) * LANES].astype(F32)
             for i, (k, _) in enumerate(kv)]
        m_prev = m_ref[bi]
        m_new = m_prev
        for si in s:
            m_new = jnp.maximum(m_new, jnp.max(si, axis=1, keepdims=True))
        alpha = jnp.exp(m_prev - m_new)
        l_new = alpha * l_ref[bi]
        acc = acc_ref[bi] * pltpu.repeat(alpha, KV_COLS // LANES, axis=1)
        for si, (_, v) in zip(s, kv):
            p = jnp.where(si > 0.5 * NEG, jnp.exp(si - m_new), 0.0)
            l_new = l_new + jnp.sum(p, axis=1, keepdims=True)
            acc = acc + _dot_t(p.astype(BF16), v().astype(BF16))
        l_ref[bi] = l_new
        acc_ref[bi] = acc
        m_ref[bi] = m_new

    def loaders(ref, lead):
        return (lambda: ref[lead, 0]), (lambda: ref[lead, 1])

    @pl.when(j < n_steps - 1)
    def _():
        for bi in range(nb):
            update(bi, [loaders(pages[bi * n_pg + i], 0) for i in range(n_pg)])

    @pl.when(j == n_steps - 1)
    def _():
        for bi in range(nb):
            update(bi, [loaders(new_ref, bi)])
        o_ref[...] = acc_ref[...] / jnp.maximum(jnp.concatenate([l_ref[...]] * (KV_COLS // LANES), axis=-1), 1e-30)


def _paged_flash(page_table, q_bd, pool_t, new_page_t, bias, n_pg):
    bsz, n_pages = page_table.shape
    n_steps = n_pages // n_pg + 1
    rows = q_bd.shape[1]
    nb = _batch_rows(bsz)
    fix = lambda b, j, pt: (b, 0, 0)
    fix4 = lambda b, j, pt: (b, 0, 0, 0)
    page_block = (1, 2, KV_COLS, PAGE_SIZE)

    def page_map(bi, i):
        return lambda b, j, pt: (pt[b * nb + bi, jnp.minimum(j * n_pg + i, n_pages - 1)], 0, 0, 0)

    grid_spec = pltpu.PrefetchScalarGridSpec(
        num_scalar_prefetch=1,
        grid=(bsz // nb, n_steps),
        in_specs=[pl.BlockSpec((nb, rows, KV_COLS), fix)]
        + [pl.BlockSpec(page_block, page_map(bi, i)) for bi in range(nb) for i in range(n_pg)]
        + [pl.BlockSpec((nb, 2, KV_COLS, PAGE_SIZE), fix4),
           pl.BlockSpec((nb, rows, n_pg * LANES), lambda b, j, pt: (b, 0, j))],
        out_specs=pl.BlockSpec((nb, rows, KV_COLS), fix),
        scratch_shapes=[pltpu.VMEM((nb, rows, LANES), F32), pltpu.VMEM((nb, rows, LANES), F32),
                        pltpu.VMEM((nb, rows, KV_COLS), F32)],
    )
    return pl.pallas_call(
        functools.partial(_paged_flash_kernel, n_pg=n_pg, n_steps=n_steps, nb=nb),
        grid_spec=grid_spec,
        out_shape=jax.ShapeDtypeStruct((bsz, rows, KV_COLS), F32),
        compiler_params=_params("parallel", "arbitrary"),
        name="paged_flash",
    )(page_table, q_bd, *([pool_t] * (nb * n_pg)), new_page_t, bias)


def _cumsum_kernel(pt_ref, *refs, n_pg, n_steps, nb):
    pages = refs[:nb * n_pg]
    new_ref, o_ref, carry_ref = refs[nb * n_pg:]
    j = pl.program_id(1)
    r_i = lax.broadcasted_iota(I32, (LANES, LANES), 0)
    c_i = lax.broadcasted_iota(I32, (LANES, LANES), 1)
    tri = jnp.where(r_i <= c_i, 1.0, 0.0).astype(BF16)

    @pl.when(j == 0)
    def _():
        carry_ref[...] = jnp.zeros(carry_ref.shape, F32)

    ones = jnp.ones((LANES, LANES), BF16)

    def steps(bi, xs):
        local = [_dot_hp(x, tri) for x in xs]
        total = [_dot_hp(x, ones) for x in xs]
        carry = carry_ref[bi]
        for i in range(len(xs)):
            o_ref[bi, :, i * LANES:(i + 1) * LANES] = local[i] + carry
            carry = carry + total[i]
        carry_ref[bi] = carry

    @pl.when(j < n_steps - 1)
    def _():
        for bi in range(nb):
            steps(bi, [pages[bi * n_pg + i][0] for i in range(n_pg)])

    @pl.when(j == n_steps - 1)
    def _():
        for bi in range(nb):
            steps(bi, [new_ref[bi]])
        for i in range(1, n_pg):
            o_ref[:, :, i * LANES:(i + 1) * LANES] = jnp.zeros((nb, N_HEADS, LANES), F32)


def _paged_cumsum(page_table, pool_t, new_t, n_pg):
    bsz, n_pages = page_table.shape
    n_steps = n_pages // n_pg + 1
    nb = _batch_rows(bsz)
    fix = lambda b, j, pt: (b, 0, 0)

    def page_map(bi, i):
        return lambda b, j, pt: (pt[b * nb + bi, jnp.minimum(j * n_pg + i, n_pages - 1)], 0, 0)

    grid_spec = pltpu.PrefetchScalarGridSpec(
        num_scalar_prefetch=1,
        grid=(bsz // nb, n_steps),
        in_specs=[pl.BlockSpec((1, N_HEADS, LANES), page_map(bi, i)) for bi in range(nb) for i in range(n_pg)]
        + [pl.BlockSpec((nb, N_HEADS, LANES), fix)],
        out_specs=pl.BlockSpec((nb, N_HEADS, n_pg * LANES), lambda b, j, pt: (b, 0, j)),
        scratch_shapes=[pltpu.VMEM((nb, N_HEADS, LANES), F32)],
    )
    return pl.pallas_call(
        functools.partial(_cumsum_kernel, n_pg=n_pg, n_steps=n_steps, nb=nb),
        grid_spec=grid_spec,
        out_shape=jax.ShapeDtypeStruct((bsz, N_HEADS, n_steps * n_pg * LANES), F32),
        compiler_params=_params("parallel", "arbitrary"),
        name="fox_cumsum",
    )(page_table, *([pool_t] * (nb * n_pg)), new_t)


def _gelu_tanh(x):
    return 0.5 * x * (1.0 + jnp.tanh(0.7978845608028654 * (x + 0.044715 * x * x * x)))


def _nsa_compress_kernel(pt_ref, *refs, n_pg, n_steps, nc):
    pages = refs[:n_pg]
    pe_ref, w1_ref, w2_ref, o_ref, x_ref = refs[n_pg:]
    j = pl.program_id(1)
    per = PAGE_SIZE // CMP_STRIDE
    n_chunk = x_ref.shape[1]
    for i in range(0, n_pg, 2):
        start = pl.multiple_of((j * n_pg + i) * per, 2 * per)
        for ck in range(2 * N_KV_HEADS):
            x_ref[ck, pl.ds(start, 2 * per), :] = jnp.concatenate([pages[i][0, ck], pages[i + 1][0, ck]], axis=0)

    @pl.when(j == n_steps - 1)
    def _():
        half = CMP_STRIDE * HEAD_DIM
        rows = N_KV_HEADS * n_chunk
        row = lax.broadcasted_iota(I32, (N_KV_HEADS, n_chunk, HEAD_DIM), 1)
        for c in range(2):
            w1 = w1_ref[c]
            part = _dot(x_ref[c * N_KV_HEADS:(c + 1) * N_KV_HEADS].reshape(rows, half), w1)
            pe = pe_ref[c]
            pe_term = _dot(pe[:, :half], w1)[:, :CMP_HID] + _dot(pe[:, half:], w1)[:, CMP_HID:]
            h = pe_term[0:1, :] + part[:, :CMP_HID] + pltpu.roll(part[:, CMP_HID:], rows - 1, 0)
            out = _dot(_gelu_tanh(h).astype(BF16), w2_ref[c]).reshape(N_KV_HEADS, n_chunk, HEAD_DIM)
            o_ref[0, c * N_KV_HEADS:(c + 1) * N_KV_HEADS] = jnp.where(row < nc, out, 0.0)


def _nsa_compress(page_table, pool_t, pe8, w1cat, w2, n_pg=8):
    bsz, n_pages = page_table.shape
    per = PAGE_SIZE // CMP_STRIDE
    n_chunk = n_pages * per
    nc = n_chunk - CMP_LEN // CMP_STRIDE + 1
    n_steps = n_pages // n_pg
    width = CMP_STRIDE * HEAD_DIM

    def page_map(i):
        return lambda b, j, pt: (pt[b, j * n_pg + i], 0, 0, 0)

    fix3 = lambda b, j, pt: (0, 0, 0)
    grid_spec = pltpu.PrefetchScalarGridSpec(
        num_scalar_prefetch=1,
        grid=(bsz, n_steps),
        in_specs=[pl.BlockSpec((1, 2 * N_KV_HEADS, per, width), page_map(i)) for i in range(n_pg)]
        + [pl.BlockSpec(pe8.shape, fix3), pl.BlockSpec(w1cat.shape, fix3), pl.BlockSpec(w2.shape, fix3)],
        out_specs=pl.BlockSpec((1, 2 * N_KV_HEADS, n_chunk, HEAD_DIM), lambda b, j, pt: (b, 0, 0, 0)),
        scratch_shapes=[pltpu.VMEM((2 * N_KV_HEADS, n_chunk, width), BF16)],
    )
    return pl.pallas_call(
        functools.partial(_nsa_compress_kernel, n_pg=n_pg, n_steps=n_steps, nc=nc),
        grid_spec=grid_spec,
        out_shape=jax.ShapeDtypeStruct((bsz, 2 * N_KV_HEADS, n_chunk, HEAD_DIM), F32),
        compiler_params=_params("parallel", "arbitrary"),
        name="nsa_compress",
    )(page_table, *([pool_t] * n_pg), pe8, w1cat, w2), nc


def _nsa_cmp_kernel(q_ref, ck_ref, cv_ref, cover_ref, o_ref, sel_ref, u_ref, *, tq, pos0, nc, n_sel):
    t0 = pos0 + pl.program_id(1) * tq
    ncp = ck_ref.shape[2]
    nsp = cover_ref.shape[0]
    rows = GROUP * tq
    n_idx = lax.broadcasted_iota(I32, (tq, ncp), 1)
    t_idx = t0 + lax.broadcasted_iota(I32, (tq, ncp), 0)
    c_ok = ((n_idx * CMP_STRIDE + CMP_LEN - 1 <= t_idx) & (n_idx < nc))[None]
    blk = lax.broadcasted_iota(I32, (nsp, tq), 0)
    cur = (t0 + lax.broadcasted_iota(I32, (nsp, tq), 1)) // SEL_BLOCK
    forced = (blk == 0) | (blk == cur) | (blk == cur - 1)
    cover_t = cover_ref[...]
    for g in range(N_KV_HEADS):
        q = q_ref[0, g].reshape(rows, HEAD_DIM)
        s3 = jnp.where(c_ok, _dot_t(q, ck_ref[0, g]).reshape(GROUP, tq, ncp), NEG)
        m = jnp.max(s3, axis=-1, keepdims=True)
        e = jnp.where(c_ok, jnp.exp(s3 - m), 0.0)
        p = e / jnp.maximum(jnp.sum(e, axis=-1, keepdims=True), 1e-30)
        o = _dot(p.reshape(rows, ncp).astype(BF16), cv_ref[0, g])
        o_ref[0, :, g * GROUP * HEAD_DIM:(g + 1) * GROUP * HEAD_DIM] = jnp.concatenate(
            [o[j * tq:(j + 1) * tq] for j in range(GROUP)], axis=1)
        psum = p[0] + p[1] + p[2] + p[3]
        hi = psum.astype(BF16)
        r1 = psum - hi.astype(F32)
        mid = r1.astype(BF16)
        lo = (r1 - mid.astype(F32)).astype(BF16)
        imp = _dot_t(cover_t, hi) + _dot_t(cover_t, mid) + _dot_t(cover_t, lo)
        imp = jnp.where(forced, jnp.inf, imp)
        imp = jnp.where(blk <= cur, imp, -jnp.inf)
        u_ref[:, g * tq:(g + 1) * tq] = _sortable(imp)
    sel, u = _topk_mask_cols(u_ref, n_sel)
    sel_ref[0, 0] = jnp.where(sel & (u > KEY_NEG_INF), 1.0, 0.0)


def _nsa_cmp_select(q, cmp_k, cmp_v, pos0, nc, n_keys, tq):
    bsz, _, _, t, _ = q.shape
    ncp = cmp_k.shape[2]
    ns = -(-n_keys // SEL_BLOCK)
    nsp = -(-ns // LANES) * LANES
    n_sel = min(SEL_TOPN, ns)
    c0 = np.arange(ncp)[:, None] * CMP_STRIDE
    s0 = np.arange(nsp)[None, :] * SEL_BLOCK
    cover = (c0 <= s0 + SEL_BLOCK - 1) & (c0 + CMP_LEN - 1 >= s0) & (np.arange(ncp)[:, None] < nc) & (np.arange(nsp)[None, :] < ns)
    cover_t = jnp.asarray(cover.T.astype(np.float32), dtype=BF16)
    nq = t // tq
    qspec = pl.BlockSpec((1, N_KV_HEADS, GROUP, tq, HEAD_DIM), lambda b, i: (b, 0, 0, i, 0))
    cspec = pl.BlockSpec((1, N_KV_HEADS, ncp, HEAD_DIM), lambda b, i: (b, 0, 0, 0))
    o_c, sel = pl.pallas_call(
        functools.partial(_nsa_cmp_kernel, tq=tq, pos0=pos0, nc=nc, n_sel=n_sel),
        grid=(bsz, nq),
        in_specs=[qspec, cspec, cspec, pl.BlockSpec((nsp, ncp), lambda b, i: (0, 0))],
        out_specs=[pl.BlockSpec((1, tq, Q_DIM), lambda b, i: (b, i, 0)),
                   pl.BlockSpec((1, 1, nsp, N_KV_HEADS * tq), lambda b, i: (b, i, 0, 0))],
        out_shape=[jax.ShapeDtypeStruct((bsz, t, Q_DIM), F32), jax.ShapeDtypeStruct((bsz, nq, nsp, N_KV_HEADS * tq), F32)],
        scratch_shapes=[pltpu.VMEM((nsp, N_KV_HEADS * tq), I32)],
        compiler_params=_params("parallel", "parallel"),
        name="nsa_cmp_select",
    )(q, cmp_k, cmp_v, cover_t)
    sel = sel.reshape(bsz, nq, nsp, N_KV_HEADS, tq).transpose(0, 3, 1, 4, 2).reshape(bsz, N_KV_HEADS, t, nsp)
    return o_c, sel


def _kmean_kernel(*refs):
    o_ref = refs[-1]
    tot = jnp.sum(refs[0][0], axis=0, keepdims=True)
    for r in refs[1:-1]:
        tot = tot + jnp.sum(r[0], axis=0, keepdims=True)
    o_ref[0, 0] = tot * (1.0 / MOBA_BLOCK)


def _kmean_prompt(kv):
    bsz, t, _ = kv.shape
    nb = t // MOBA_BLOCK
    return pl.pallas_call(
        _kmean_kernel,
        grid=(bsz, nb),
        in_specs=[pl.BlockSpec((1, MOBA_BLOCK, KV_COLS), lambda b, i: (b, i, 0))],
        out_specs=pl.BlockSpec((1, 1, 1, KV_COLS), lambda b, i: (b, i, 0, 0)),
        out_shape=jax.ShapeDtypeStruct((bsz, nb, 1, KV_COLS), F32),
        compiler_params=_params("parallel", "parallel"),
        name="kmean_prompt",
    )(kv)


def _kmean_sample_kernel(pt_ref, *refs, per, nb):
    o_ref = refs[-1]
    ones = jnp.ones((SUBLANES, PAGE_SIZE), BF16)
    blocks = len(refs[:-1]) // (per * nb)
    for n in range(nb * blocks):
        tot = jnp.zeros((SUBLANES, KV_COLS), F32)
        for r in refs[n * per:(n + 1) * per]:
            x = r[0, 0]
            hi = x.astype(BF16)
            r1 = x - hi.astype(F32)
            mid = r1.astype(BF16)
            lo = (r1 - mid.astype(F32)).astype(BF16)
            tot = tot + _dot_t(ones, hi) + _dot_t(ones, mid) + _dot_t(ones, lo)
        o_ref[n // blocks, n % blocks] = tot[0:1] * (1.0 / MOBA_BLOCK)


def _kmean_sample(page_table, pool_t, n_pg):
    bsz, n_pages = page_table.shape
    per = MOBA_BLOCK // PAGE_SIZE
    n_blocks = n_pages // per
    nb = _batch_rows(bsz)

    def page_map(bi, i):
        return lambda b, n, pt: (pt[b * nb + bi, n * n_pg + i], 0, 0, 0)

    grid_spec = pltpu.PrefetchScalarGridSpec(
        num_scalar_prefetch=1,
        grid=(bsz // nb, n_pages // n_pg),
        in_specs=[pl.BlockSpec((1, 1, KV_COLS, PAGE_SIZE), page_map(bi, i)) for bi in range(nb) for i in range(n_pg)],
        out_specs=pl.BlockSpec((nb, n_pg // per, 1, KV_COLS), lambda b, n, pt: (b, n, 0, 0)),
    )
    return pl.pallas_call(
        functools.partial(_kmean_sample_kernel, per=per, nb=nb),
        grid_spec=grid_spec,
        out_shape=jax.ShapeDtypeStruct((bsz, n_blocks, 1, KV_COLS), F32),
        compiler_params=_params("parallel", "parallel"),
        name="kmean_sample",
    )(page_table, *([pool_t] * (nb * n_pg)))


def _moba_select_kernel(q_ref, km_ref, sel_ref, *, tq, pos0, k_top):
    t0 = pos0 + pl.program_id(1) * tq
    nbp = km_ref.shape[2]
    rows = GROUP * tq
    blk = lax.broadcasted_iota(I32, (nbp, rows), 0)
    n_past = (t0 + (lax.broadcasted_iota(I32, (nbp, rows), 1) & (tq - 1))) // MOBA_BLOCK
    for g in range(N_KV_HEADS):
        q = q_ref[0, g].reshape(rows, HEAD_DIM)
        s = jnp.where(blk < n_past, _dot_t(km_ref[0, g], q), -jnp.inf)
        sel = jnp.zeros((nbp, rows), F32)
        for _ in range(k_top):
            m = jnp.max(s, axis=0, keepdims=True)
            first = jnp.min(jnp.where(s == m, blk, nbp), axis=0, keepdims=True)
            pick = blk == first
            sel = jnp.where(pick & (m > -jnp.inf), 1.0, sel)
            s = jnp.where(pick, -jnp.inf, s)
        sel_ref[0, 0, g] = sel


def _moba_select(q, kmean, pos0, nb, tq):
    bsz, _, _, t, _ = q.shape
    nbp = kmean.shape[2]
    nq = t // tq
    qspec = pl.BlockSpec((1, N_KV_HEADS, GROUP, tq, HEAD_DIM), lambda b, i: (b, 0, 0, i, 0))
    sel = pl.pallas_call(
        functools.partial(_moba_select_kernel, tq=tq, pos0=pos0, k_top=min(MOBA_TOPK, nb)),
        grid=(bsz, nq),
        in_specs=[qspec, pl.BlockSpec((1, N_KV_HEADS, nbp, HEAD_DIM), lambda b, i: (b, 0, 0, 0))],
        out_specs=pl.BlockSpec((1, 1, N_KV_HEADS, nbp, GROUP * tq), lambda b, i: (b, i, 0, 0, 0)),
        out_shape=jax.ShapeDtypeStruct((bsz, nq, N_KV_HEADS, nbp, GROUP * tq), F32),
        compiler_params=_params("parallel", "parallel"),
        name="moba_select",
    )(q, kmean)
    sel = sel.reshape(bsz, nq, N_KV_HEADS, nbp, GROUP, tq).transpose(0, 2, 4, 1, 5, 3)
    return sel.reshape(bsz, N_KV_HEADS, GROUP, t, nbp)


def _rope_tables(pos):
    half = HEAD_DIM // 2
    inv = ROPE_THETA ** (-jnp.arange(half, dtype=F32) / half)
    ang = pos.astype(F32)[:, None] * inv[None, :]
    cos, sin = jnp.cos(ang), jnp.sin(ang)
    rep = LANES // HEAD_DIM
    return jnp.concatenate([cos, cos] * rep, axis=1), jnp.concatenate([-sin, sin] * rep, axis=1)


def _pad_to(x, axis, size):
    pad = [(0, 0)] * x.ndim
    pad[axis] = (0, size - x.shape[axis])
    return jnp.pad(x, pad)


def _q_groups(q):
    b, t = q.shape[:2]
    return q.transpose(0, 2, 1, 3).reshape(b, N_KV_HEADS, GROUP, t, HEAD_DIM)


def _pages_t(kv):
    return kv.transpose(0, 2, 3, 4, 1).reshape(kv.shape[0], 2, KV_COLS, PAGE_SIZE)


def _fit(x, width):
    return x[..., :width] if x.shape[-1] >= width else _pad_to(x, x.ndim - 1, width)


_HEAD_TO_GROUP = np.equal(np.arange(N_HEADS)[:, None] // GROUP, np.arange(N_KV_HEADS)[None, :]).astype(np.float32)


def _q_block_diag(q):
    s, tn = q.shape[:2]
    qb = q[:, :, :, None, :] * jnp.asarray(_HEAD_TO_GROUP, dtype=q.dtype)[None, None, :, :, None]
    return qb.reshape(s, tn * N_HEADS, KV_COLS)


def _extract_block_diag(o, tn):
    s = o.shape[0]
    o5 = o.reshape(s, tn, N_HEADS, N_KV_HEADS, HEAD_DIM) * _HEAD_TO_GROUP[None, None, :, :, None]
    return o5.sum(axis=3).reshape(s * tn, Q_DIM)


def _new_page(kv_new):
    return _pages_t(_pad_to(kv_new, 1, PAGE_SIZE))


def _pick_pages(n_pages):
    for n in (8, 4, 2, 1):
        if n_pages % n == 0:
            return n


def _identity_pages(bsz, n_pages):
    return jnp.arange(bsz * n_pages, dtype=I32).reshape(bsz, n_pages)


def _rows_th(ok, s, tn):
    return jnp.where(ok, 0.0, NEG).astype(BF16).reshape(s, tn * N_HEADS, ok.shape[-1])


def _kv_seg(c_k, rope):
    return (c_k, 2 * KV_COLS, KV_COLS if rope else False, 1.0, F32, "tok")


def _kv_rows(kv, bsz, t):
    return kv.reshape(bsz, t, 2, N_KV_HEADS, HEAD_DIM)


def _kv_forms(c_k, rope):
    return [(c_k, KV_COLS, rope, 1.0, BF16, "keys_t"), (c_k + KV_COLS, KV_COLS, False, 1.0, BF16, "values_1")]


def _dsa_project(x, bsz, t, w_in, cs, q_scale, prompt):
    hm = "heads" if prompt else "tok"
    c_qi = Q_DIM + 2 * KV_COLS
    c_ki = c_qi + IDX_HEADS * IDX_DIM
    segs = [(0, Q_DIM, True, q_scale, BF16, hm), _kv_seg(Q_DIM, True), (c_qi, IDX_HEADS * IDX_DIM, True, 1.0, BF16, hm),
            (c_ki, IDX_DIM, True, 1.0, F32, "tok"), (c_ki + IDX_DIM, IDX_HEADS, False, IDX_SCALE, F32, "tok")]
    if prompt:
        segs += _kv_forms(Q_DIM, True)
    q, kv, qi, ki, wi, *flash_kv = _proj(x, w_in, cs, segs, bsz)
    if not prompt:
        q, qi = q.reshape(bsz, t, N_HEADS, HEAD_DIM), qi.reshape(bsz, t, IDX_HEADS, IDX_DIM)
    return q, qi, wi.reshape(bsz, t, IDX_HEADS), _kv_rows(kv, bsz, t), ki.reshape(bsz, t, IDX_DIM), flash_kv


def _head_groups(q):
    return q.reshape(q.shape[0], N_KV_HEADS, GROUP, q.shape[2], HEAD_DIM)


def _dsa_prompt(x, bsz, t, w_in, cs):
    q, qi, wi, kv, ki, (k_t, v1) = _dsa_project(x, bsz, t, w_in, cs, ATTN_SCALE * LOG2E, True)
    mask = _dsa_select_prompt(qi, ki.astype(BF16), wi)
    o = _flash_prompt("dsa", _head_groups(q), k_t, v1, [mask])
    return o.reshape(bsz * t, Q_DIM), kv, ki


def _dsa_sample(x, s, tn, w_in, cs, page_table, cache_kv, cache_kidx, n_pg):
    q, qi, wi, kv, ki, _ = _dsa_project(x, s, tn, w_in, cs, ATTN_SCALE, False)
    tpad = SUBLANES
    qi_p = _pad_to(qi, 1, tpad).reshape(s, tpad * IDX_HEADS, IDX_DIM)
    wi_p = _pad_to(wi, 1, tpad).reshape(s, tpad * IDX_HEADS, 1)
    mask = _dsa_select_sample(page_table, qi_p, wi_p, cache_kidx.transpose(0, 2, 1),
                              _pad_to(ki, 1, PAGE_SIZE).transpose(0, 2, 1), tn, n_pg)
    ok = jnp.broadcast_to(mask[:, :tn, None, :] > 0.5, (s, tn, N_HEADS, mask.shape[-1]))
    o = _paged_flash(page_table, _q_block_diag(q), _pages_t(cache_kv), _new_page(kv), _rows_th(ok, s, tn), n_pg)
    return _extract_block_diag(o, tn), kv, ki


def _fox_project(x, bsz, t, w_in, b_f, cs, q_scale, prompt):
    segs = [(0, Q_DIM, False, q_scale, BF16, "heads" if prompt else "tok"), _kv_seg(Q_DIM, False),
            (Q_DIM + 2 * KV_COLS, N_HEADS, False, 1.0, F32, "tok")]
    if prompt:
        segs += _kv_forms(Q_DIM, False)
    q, kv, f, *flash_kv = _proj(x, w_in, cs, segs, bsz)
    logf = jax.nn.log_sigmoid(f.reshape(bsz, t, N_HEADS) + b_f)
    if not prompt:
        q = q.reshape(bsz, t, N_HEADS, HEAD_DIM)
    return q, _kv_rows(kv, bsz, t), logf, flash_kv


def _fox_prompt(x, bsz, t, w_in, b_f, cs):
    q, kv, logf, (k_t, v1) = _fox_project(x, bsz, t, w_in, b_f, cs, ATTN_SCALE * LOG2E, True)
    n_pages = t // LANES
    pool_t = logf.reshape(bsz, n_pages, LANES, N_HEADS).transpose(0, 1, 3, 2).reshape(bsz * n_pages, N_HEADS, LANES)
    c = _paged_cumsum(_identity_pages(bsz, n_pages), pool_t, jnp.zeros((bsz, N_HEADS, LANES), F32), _pick_pages(n_pages))
    c = (c[:, :, :t] * LOG2E).reshape(bsz, N_KV_HEADS, GROUP, t)
    o = _flash_prompt("fox", _head_groups(q), k_t, v1, [c])
    return o.reshape(bsz * t, Q_DIM), kv, logf


def _fox_sample(x, s, tn, w_in, b_f, cs, page_table, cache_kv, cache_logf, n_pg):
    q, kv, logf, _ = _fox_project(x, s, tn, w_in, b_f, cs, ATTN_SCALE, False)
    past = page_table.shape[1] * PAGE_SIZE
    c = _paged_cumsum(page_table, cache_logf.transpose(0, 2, 1), _pad_to(logf.transpose(0, 2, 1), 2, LANES), n_pg)
    col = jnp.arange(c.shape[-1])
    valid = (col[None, :] < past) | ((col[None, :] - past <= jnp.arange(tn)[:, None]) & (col[None, :] < past + tn))
    bias = jnp.where(valid[None, :, None, :], -c[:, None, :, :], NEG).reshape(s, tn * N_HEADS, c.shape[-1])
    o = _paged_flash(page_table, _q_block_diag(q), _pages_t(cache_kv), _new_page(kv), bias, n_pg)
    return _extract_block_diag(o, tn), kv, logf


def _nsa_project(x, bsz, t, w_in, b_gate, cs, q_scale, prompt):
    hm = "heads" if prompt else "tok"
    c_slc, c_win = Q_DIM + 2 * KV_COLS, Q_DIM + 4 * KV_COLS
    segs = [(0, Q_DIM, False, ATTN_SCALE, BF16, hm), (0, Q_DIM, True, q_scale, BF16, hm),
            _kv_seg(Q_DIM, False), _kv_seg(c_slc, True), _kv_seg(c_win, True),
            (Q_DIM + 6 * KV_COLS, 3 * N_HEADS, False, 1.0, F32, "tok")]
    if prompt:
        segs += _kv_forms(c_slc, True) + _kv_forms(c_win, True)
    q, q_rot, kv_cmp, kv_slc, kv_win, g, *flash_kv = _proj(x, w_in, cs, segs, bsz)
    gate = jax.nn.sigmoid(g.reshape(bsz, t, 3 * N_HEADS) + b_gate).reshape(bsz, t, 3, N_HEADS)
    if not prompt:
        q, q_rot = q.reshape(bsz, t, N_HEADS, HEAD_DIM), q_rot.reshape(bsz, t, N_HEADS, HEAD_DIM)
    return (q, q_rot, gate, _kv_rows(kv_cmp, bsz, t), _kv_rows(kv_slc, bsz, t), _kv_rows(kv_win, bsz, t), flash_kv)


def _nsa_weights(pe, w1, w2):
    r = CMP_LEN // CMP_STRIDE
    w1cat = w1.reshape(2, r, CMP_STRIDE * HEAD_DIM, CMP_HID).transpose(0, 2, 1, 3).reshape(2, CMP_STRIDE * HEAD_DIM, r * CMP_HID)
    pe8 = jnp.broadcast_to(pe.reshape(2, 1, CMP_LEN * HEAD_DIM), (2, 8, CMP_LEN * HEAD_DIM))
    return pe8.astype(BF16), w1cat.astype(BF16), w2.astype(BF16)


def _chunk_pages(kv):
    n = kv.shape[0]
    per = PAGE_SIZE // CMP_STRIDE
    x = kv.reshape(n, per, CMP_STRIDE, 2 * N_KV_HEADS, HEAD_DIM).transpose(0, 3, 1, 2, 4)
    return x.reshape(n, 2 * N_KV_HEADS, per, CMP_STRIDE * HEAD_DIM).astype(BF16)


def _gate_mix(gate, o_c, o_s, o_w):
    m = gate.shape[0] * gate.shape[1]
    return (gate.reshape(m, 3 * N_HEADS), o_c.reshape(m, Q_DIM), o_s.reshape(m, Q_DIM), o_w.reshape(m, Q_DIM))


def _nsa_prompt(x, bsz, t, w_in, b_gate, cmp_w, cs, tq=Q_TILE, tk=K_TILE):
    q, q_rot, gate, kv_cmp, kv_slc, kv_win, (ks_t, vs1, kw_t, vw1) = _nsa_project(x, bsz, t, w_in, b_gate, cs,
                                                                                  ATTN_SCALE * LOG2E, True)
    n_pages = t // PAGE_SIZE
    cmp, nc = _nsa_compress(_identity_pages(bsz, n_pages), _chunk_pages(kv_cmp.reshape(bsz * n_pages, PAGE_SIZE, 2, N_KV_HEADS, HEAD_DIM)),
                            *cmp_w, n_pg=_pick_pages(n_pages))
    cmp = cmp.astype(BF16)
    tq, tk = min(tq, t), min(tk, t)
    o_c, selblk = _nsa_cmp_select(_head_groups(q), cmp[:, :N_KV_HEADS], cmp[:, N_KV_HEADS:], 0, nc, t, tq)
    qg = _head_groups(q_rot)
    e3 = _expand_matrix(selblk.shape[-1], SEL_BLOCK, t, tk)
    o_s = _flash_prompt("nsa_sel", qg, ks_t, vs1, [selblk, e3], tq=2 * tq, tk=tk)
    o_w = _flash_prompt("nsa_win", qg, kw_t, vw1, [], tk=tk)
    return _gate_mix(gate, o_c, o_s, o_w), kv_cmp, kv_slc, kv_win[:, -min(WINDOW, t):]


def _nsa_sample(x, s, tn, w_in, b_gate, cmp_w, cs, page_table, cache_cmp, cache_slc, state_win, n_pg):
    q, q_rot, gate, kv_cmp, kv_slc, kv_win, _ = _nsa_project(x, s, tn, w_in, b_gate, cs, ATTN_SCALE, False)
    past = page_table.shape[1] * PAGE_SIZE
    tpad = SUBLANES
    cmp, nc = _nsa_compress(page_table, _chunk_pages(cache_cmp), *cmp_w, n_pg=n_pg)
    cmp = cmp.astype(BF16)
    o_c, selblk = _nsa_cmp_select(_q_groups(_pad_to(q, 1, tpad)), cmp[:, :N_KV_HEADS], cmp[:, N_KV_HEADS:],
                                  past, nc, past + tn, tpad)
    o_c = o_c[:, :tn]
    qbd = _q_block_diag(q_rot)
    width = (page_table.shape[1] // n_pg + 1) * n_pg * LANES
    col = jnp.arange(width)
    pos = past + jnp.arange(tn)
    sel_key = jnp.repeat(selblk[:, :, :tn, :-(-width // SEL_BLOCK)] > 0.5, SEL_BLOCK, axis=-1)[..., :width]
    ok = sel_key & (col[None, :] <= pos[:, None])[None, None]
    ok = jnp.broadcast_to(ok.transpose(0, 2, 1, 3)[:, :, :, None, :], (s, tn, N_KV_HEADS, GROUP, width))
    o_s = _paged_flash(page_table, qbd, _pages_t(cache_slc), _new_page(kv_slc),
                       _rows_th(ok.reshape(s, tn, N_HEADS, width), s, tn), n_pg)
    win_buf = state_win.shape[1]
    n_wp = win_buf // PAGE_SIZE
    wcol = jnp.arange(2 * n_wp * LANES)
    win_pos = jnp.where(wcol < win_buf, past - win_buf + wcol, jnp.where(wcol < win_buf + tn, past + wcol - win_buf, -1))
    w_ok = (win_pos[None, :] <= pos[:, None]) & (win_pos[None, :] >= pos[:, None] - WINDOW) & (win_pos[None, :] >= 0)
    w_ok = jnp.broadcast_to(w_ok[None, :, None, :], (s, tn, N_HEADS, wcol.shape[0]))
    win_pages = _pages_t(state_win.reshape(s * n_wp, PAGE_SIZE, 2, N_KV_HEADS, HEAD_DIM))
    o_w = _paged_flash(_identity_pages(s, n_wp), qbd, win_pages, _new_page(kv_win), _rows_th(w_ok, s, tn), n_wp)
    unbd = lambda o: _extract_block_diag(o, tn).reshape(s, tn, Q_DIM)
    win = jnp.concatenate([state_win, kv_win], axis=1)[:, -win_buf:]
    return _gate_mix(gate, o_c, unbd(o_s), unbd(o_w)), kv_cmp, kv_slc, win


def _moba_project(x, bsz, t, w_in, cs, q_scale, prompt):
    segs = [(0, Q_DIM, True, q_scale, BF16, "heads" if prompt else "tok"), _kv_seg(Q_DIM, True)]
    if prompt:
        segs += _kv_forms(Q_DIM, True)
    q, kv, *flash_kv = _proj(x, w_in, cs, segs, bsz)
    if not prompt:
        q = q.reshape(bsz, t, N_HEADS, HEAD_DIM)
    return q, _kv_rows(kv, bsz, t), flash_kv


def _kmean_heads(km):
    b, nb = km.shape[:2]
    return _pad_to(km.reshape(b, nb, N_KV_HEADS, HEAD_DIM).transpose(0, 2, 1, 3), 2, -(-nb // 16) * 16).astype(BF16)


def _moba_prompt(x, bsz, t, w_in, cs, tq=Q_TILE, tk=K_TILE):
    q, kv, (k_t, v1) = _moba_project(x, bsz, t, w_in, cs, ATTN_SCALE * LOG2E, True)
    tq, tk = min(tq, t), min(tk, t)
    nb = -(-t // MOBA_BLOCK)
    km = _kmean_heads(_kmean_prompt(kv.reshape(bsz, t, 2 * KV_COLS)))
    qg = _head_groups(q)
    sel = _moba_select(qg, km, 0, nb, tq)
    e3 = _expand_matrix(km.shape[2], MOBA_BLOCK, t, tk)
    o = _flash_prompt("moba", qg, k_t, v1, [sel, e3], tq=2 * tq, tk=tk)
    return o.reshape(bsz * t, Q_DIM), kv


def _moba_sample(x, s, tn, w_in, cs, page_table, cache_kv, n_pg):
    q, kv, _ = _moba_project(x, s, tn, w_in, cs, ATTN_SCALE, False)
    past = page_table.shape[1] * PAGE_SIZE
    tpad = SUBLANES
    pool = _pages_t(cache_kv)
    nb = -(-(past + tn) // MOBA_BLOCK)
    km = _kmean_heads(_kmean_sample(page_table, pool, n_pg))
    sel = _moba_select(_q_groups(_pad_to(q, 1, tpad)), km, past, nb, tpad)
    width = (page_table.shape[1] // n_pg + 1) * n_pg * LANES
    col = jnp.arange(width)
    pos = past + jnp.arange(tn)
    sel_key = _fit(jnp.repeat(sel[:, :, :, :tn] > 0.5, MOBA_BLOCK, axis=-1), width)
    own = (col[None, :] // MOBA_BLOCK == pos[:, None] // MOBA_BLOCK) & (col[None, :] <= pos[:, None])
    ok = sel_key | own[None, None, None]
    ok = ok.transpose(0, 3, 1, 2, 4).reshape(s, tn, N_HEADS, width)
    o = _paged_flash(page_table, _q_block_diag(q), pool, _new_page(kv), _rows_th(ok, s, tn), n_pg)
    return _extract_block_diag(o, tn), kv


def _cast_w(w):
    return _pad_to(w, 1, -(-w.shape[1] // LANES) * LANES).astype(BF16)


def kernel(x_prompt, x_sample, cache_a_kv, cache_a_kidx, cache_b_kv, cache_b_logf, cache_c_cmp_kv, cache_c_slc_kv, state_c_win_kv, cache_d_kv, page_table, a_w_in, a_w_out, b_w_in, b_b_f, b_w_out, c_w_in, c_b_gate, c_cmp_pe, c_cmp_w1, c_cmp_w2, c_w_out, d_w_in, d_w_out, ln_g, ln_b, ffn_w_gu, ffn_w_down):
    bsz, t, d = x_prompt.shape
    s, tn, _ = x_sample.shape
    n_pages = page_table.shape[1]
    past = n_pages * PAGE_SIZE
    n_pg = _pick_pages(n_pages)
    cs_p = _rope_tables(jnp.arange(t, dtype=I32))
    cs_s = _rope_tables(jnp.tile(past + jnp.arange(tn, dtype=I32), s))
    xp = x_prompt.reshape(bsz * t, d)
    xs = x_sample.reshape(s * tn, d)
    cmp_w = _nsa_weights(c_cmp_pe, c_cmp_w1, c_cmp_w2)
    w_out = [_cast_w(w) for w in (a_w_out, b_w_out, c_w_out, d_w_out)]

    op, a_kv_p, a_kidx_p = _dsa_prompt(xp, bsz, t, _cast_w(a_w_in), cs_p)
    os_, a_kv_s, a_kidx_s = _dsa_sample(xs, s, tn, _cast_w(a_w_in), cs_s, page_table, cache_a_kv, cache_a_kidx, n_pg)

    def finish(i, xp, xs, op, os_):
        xp = _out_ln(op, w_out[i], xp, ln_g[i, 0], ln_b[i, 0])
        xs = _out_ln(os_, w_out[i], xs, ln_g[i, 0], ln_b[i, 0])
        wgu, wd = ffn_w_gu[i].astype(BF16), ffn_w_down[i].astype(BF16)
        xp = _ffn_ln(xp, wgu, wd, ln_g[i, 1], ln_b[i, 1])
        xs = _ffn_ln(xs, wgu, wd, ln_g[i, 1], ln_b[i, 1])
        return xp, xs

    xp, xs = finish(0, xp, xs, op, os_)

    op, b_kv_p, b_logf_p = _fox_prompt(xp, bsz, t, _cast_w(b_w_in), b_b_f, cs_p)
    os_, b_kv_s, b_logf_s = _fox_sample(xs, s, tn, _cast_w(b_w_in), b_b_f, cs_s, page_table, cache_b_kv, cache_b_logf, n_pg)
    xp, xs = finish(1, xp, xs, op, os_)

    op, c_cmp_kv_p, c_slc_kv_p, c_win_kv_p = _nsa_prompt(xp, bsz, t, _cast_w(c_w_in), c_b_gate, cmp_w, cs_p)
    os_, c_cmp_kv_s, c_slc_kv_s, c_win_kv_s = _nsa_sample(xs, s, tn, _cast_w(c_w_in), c_b_gate, cmp_w, cs_s, page_table,
                                                          cache_c_cmp_kv, cache_c_slc_kv, state_c_win_kv, n_pg)
    xp, xs = finish(2, xp, xs, op, os_)

    op, d_kv_p = _moba_prompt(xp, bsz, t, _cast_w(d_w_in), cs_p)
    os_, d_kv_s = _moba_sample(xs, s, tn, _cast_w(d_w_in), cs_s, page_table, cache_d_kv, n_pg)
    xp, xs = finish(3, xp, xs, op, os_)

    return (xp.reshape(bsz, t, d), xs.reshape(s, tn, d), a_kv_p, a_kv_s, a_kidx_p, a_kidx_s, b_kv_p, b_kv_s,
            b_logf_p, b_logf_s, c_cmp_kv_p, c_cmp_kv_s, c_slc_kv_p, c_slc_kv_s, c_win_kv_p, c_win_kv_s, d_kv_p, d_kv_s)
```
